```python
import jax, jax.numpy as jnp
from jax import lax
import numpy as np

D_MODEL = 1024
BATCH = 8
SEQ = 4096
DEPTH = 4

CHUNK = 64
Q_BLOCK = 128

D_MIX = D_MODEL
FOX_HEADS = 8
FOX_HEAD_DIM = 64
FOX_W = FOX_HEADS * FOX_HEAD_DIM
MLA_HEADS = 8
MLA_NOPE = 64
MLA_ROPE = 32
MLA_V = 64
MLA_W = MLA_HEADS * MLA_V
Q_LORA = 256
KV_LORA = 128
ROPE_THETA = 10000.0
EPS = 1e-6

IN_SIZES = (FOX_W, FOX_W, FOX_W, FOX_HEADS, FOX_W,
            Q_LORA, KV_LORA, MLA_ROPE, MLA_W)
N_IN = FOX_W * 4 + FOX_HEADS + Q_LORA + KV_LORA + MLA_ROPE + MLA_W

kernel_name = "hybrid_fox_mla_adaln_trunk"


def rms_norm(x, g):
    xf = x.astype(jnp.float32)
    y = xf * lax.rsqrt(jnp.mean(xf * xf, axis=-1, keepdims=True) + EPS) * g.astype(jnp.float32)
    return y.astype(x.dtype)


def to_blocks(a):
    b, s = a.shape[0], a.shape[1]
    return a.reshape((b, s // Q_BLOCK, Q_BLOCK) + a.shape[2:]).swapaxes(0, 1)


def from_blocks(a):
    nb, b, qb = a.shape[0], a.shape[1], a.shape[2]
    return a.swapaxes(0, 1).reshape((b, nb * qb) + a.shape[3:])


def apply_rope(t, cos, sin):
    tf = t.astype(jnp.float32)
    t1, t2 = jnp.split(tf, 2, axis=-1)
    out = jnp.concatenate([t1 * cos - t2 * sin, t2 * cos + t1 * sin], axis=-1)
    return out.astype(t.dtype)


def fox_attention(q, k, v, log_f):
    s_len = q.shape[1]
    cum = jnp.cumsum(log_f, axis=1)
    cum_k = cum.transpose(0, 2, 1)
    key_idx = jnp.arange(s_len)
    scale = FOX_HEAD_DIM ** -0.5

    def block(args):
        qb, cum_qb, start = args
        s = jnp.einsum('bqhd,bkhd->bhqk', qb, k, preferred_element_type=jnp.float32) * scale
        s = s + cum_qb.transpose(0, 2, 1)[..., None] - cum_k[:, :, None, :]
        q_idx = start + jnp.arange(Q_BLOCK)
        mask = key_idx[None, :] <= q_idx[:, None]
        p = jax.nn.softmax(jnp.where(mask, s, -jnp.inf), axis=-1)
        return jnp.einsum('bhqk,bkhd->bqhd', p.astype(v.dtype), v)

    starts = jnp.arange(s_len // Q_BLOCK, dtype=jnp.int32) * Q_BLOCK
    out = lax.map(block, (to_blocks(q), to_blocks(cum), starts))
    return from_blocks(out)


def mla_attention(q_nope, q_rope, k_nope, k_rope, v):
    s_len = q_nope.shape[1]
    key_chunk = jnp.arange(s_len) // CHUNK
    scale = (MLA_NOPE + MLA_ROPE) ** -0.5

    def block(args):
        qn, qr, start = args
        s = jnp.einsum('bqhd,bkhd->bhqk', qn, k_nope, preferred_element_type=jnp.float32)
        s = s + jnp.einsum('bqhr,bkr->bhqk', qr, k_rope, preferred_element_type=jnp.float32)
        q_chunk = (start + jnp.arange(Q_BLOCK)) // CHUNK
        mask = key_chunk[None, :] <= q_chunk[:, None]
        p = jax.nn.softmax(jnp.where(mask, s * scale, -jnp.inf), axis=-1)
        return jnp.einsum('bhqk,bkhd->bqhd', p.astype(v.dtype), v)

    starts = jnp.arange(s_len // Q_BLOCK, dtype=jnp.int32) * Q_BLOCK
    out = lax.map(block, (to_blocks(q_nope), to_blocks(q_rope), starts))
    return from_blocks(out)


def _fwd_setup_inputs(seed: int = 0) -> dict:
    key = jax.random.key(seed)
    ks = jax.random.split(key, 16)
    f32 = jnp.float32
    x = jax.random.normal(ks[0], (BATCH, SEQ, D_MODEL), f32)
    c = jax.random.normal(ks[1], (BATCH, D_MODEL), f32)
    offset = jax.random.randint(ks[2], (BATCH,), 0, 16, dtype=jnp.int32) * CHUNK
    positions = (offset[:, None] + jnp.arange(SEQ, dtype=jnp.int32)[None, :]).astype(jnp.int32)
    norm_g = 1.0 + 0.02 * jax.random.normal(ks[3], (DEPTH, D_MODEL), f32)
    w_ada = 0.5 * jax.random.normal(ks[4], (DEPTH, D_MODEL, 3 * D_MODEL), f32) * D_MODEL ** -0.5
    b_ada = 0.02 * jax.random.normal(ks[5], (DEPTH, 3 * D_MODEL), f32)
    w_in = jax.random.normal(ks[6], (DEPTH, D_MODEL, N_IN), f32) * D_MODEL ** -0.5
    b_f = jax.random.uniform(ks[7], (DEPTH, FOX_HEADS), f32, 1.0, 4.0)
    q_norm_g = 1.0 + 0.02 * jax.random.normal(ks[8], (DEPTH, Q_LORA), f32)
    w_uq = jax.random.normal(ks[9], (DEPTH, Q_LORA, MLA_HEADS * (MLA_NOPE + MLA_ROPE)), f32) * Q_LORA ** -0.5
    kv_norm_g = 1.0 + 0.02 * jax.random.normal(ks[10], (DEPTH, KV_LORA), f32)
    w_ukv = jax.random.normal(ks[11], (DEPTH, KV_LORA, MLA_HEADS * (MLA_NOPE + MLA_V)), f32) * KV_LORA ** -0.5
    w_out = jax.random.normal(ks[12], (DEPTH, D_MIX, D_MODEL), f32) * D_MIX ** -0.5
    final_g = 1.0 + 0.02 * jax.random.normal(ks[13], (D_MODEL,), f32)
    return {"x": x, "c": c, "positions": positions, "norm_g": norm_g, "w_ada": w_ada,
            "b_ada": b_ada, "w_in": w_in, "b_f": b_f, "q_norm_g": q_norm_g, "w_uq": w_uq,
            "kv_norm_g": kv_norm_g, "w_ukv": w_ukv, "w_out": w_out, "final_g": final_g}


def _fwd_reference(x, c, positions, norm_g, w_ada, b_ada, w_in, b_f, q_norm_g, w_uq,
              kv_norm_g, w_ukv, w_out, final_g):
    b, s_len, _ = x.shape
    splits = [int(i) for i in np.cumsum(IN_SIZES)[:-1]]

    inv_freq = 1.0 / (ROPE_THETA ** (jnp.arange(0, MLA_ROPE, 2, dtype=jnp.float32) / MLA_ROPE))
    ang = positions.astype(jnp.float32)[..., None] * inv_freq
    cos, sin = jnp.cos(ang), jnp.sin(ang)

    c_act = jax.nn.silu(c)
    for l in range(DEPTH):
        mod = c_act @ w_ada[l] + b_ada[l]
        shift, scale, gate = jnp.split(mod, 3, axis=-1)
        h = rms_norm(x, norm_g[l]) * (1.0 + scale[:, None, :]) + shift[:, None, :]

        z = h @ w_in[l]
        fq, fk, fv, ff, fg, q_lat, kv_lat, k_r, mg = jnp.split(z, splits, axis=-1)

        log_f = jax.nn.log_sigmoid(ff.astype(jnp.float32) + b_f[l].astype(jnp.float32))
        y_fox = fox_attention(fq.reshape(b, s_len, FOX_HEADS, FOX_HEAD_DIM),
                              fk.reshape(b, s_len, FOX_HEADS, FOX_HEAD_DIM),
                              fv.reshape(b, s_len, FOX_HEADS, FOX_HEAD_DIM), log_f)
        y_fox = y_fox.reshape(b, s_len, FOX_W) * jax.nn.silu(fg)

        q = (rms_norm(q_lat, q_norm_g[l]) @ w_uq[l]).reshape(b, s_len, MLA_HEADS, MLA_NOPE + MLA_ROPE)
        q_nope, q_rope = q[..., :MLA_NOPE], q[..., MLA_NOPE:]
        q_rope = apply_rope(q_rope, cos[:, :, None, :], sin[:, :, None, :])
        kv = (rms_norm(kv_lat, kv_norm_g[l]) @ w_ukv[l]).reshape(b, s_len, MLA_HEADS, MLA_NOPE + MLA_V)
        k_nope, v = kv[..., :MLA_NOPE], kv[..., MLA_NOPE:]
        k_rope = apply_rope(k_r, cos, sin)
        y_mla = mla_attention(q_nope, q_rope, k_nope, k_rope, v)
        y_mla = y_mla.reshape(b, s_len, MLA_W) * jax.nn.silu(mg)

        y = jnp.concatenate([y_fox, y_mla], axis=-1) @ w_out[l]
        x = x + gate[:, None, :] * y

    return rms_norm(x, final_g)


import jax as _jax
import jax.numpy as _jnp

TWIN_FORMAT = 'train_step'
FWD_PARAMS = ['x', 'c', 'positions', 'norm_g', 'w_ada', 'b_ada', 'w_in', 'b_f', 'q_norm_g', 'w_uq', 'kv_norm_g', 'w_ukv', 'w_out', 'final_g']
TWIN_WEIGHTS = ['norm_g', 'w_ada', 'b_ada', 'w_in', 'b_f', 'q_norm_g', 'w_uq', 'kv_norm_g', 'w_ukv', 'w_out', 'final_g']
TWIN_DIFF_INPUT = 'x'
TWIN_INPUTS = ['x', 'c', 'positions', 'norm_g', 'w_ada', 'b_ada', 'w_in', 'b_f', 'q_norm_g', 'w_uq', 'kv_norm_g', 'w_ukv', 'w_out', 'final_g', 'loss_target', 'm_norm_g', 'm_w_ada', 'm_b_ada', 'm_w_in', 'm_b_f', 'm_q_norm_g', 'm_w_uq', 'm_kv_norm_g', 'm_w_ukv', 'm_w_out', 'm_final_g', 'v_norm_g', 'v_w_ada', 'v_b_ada', 'v_w_in', 'v_b_f', 'v_q_norm_g', 'v_w_uq', 'v_kv_norm_g', 'v_w_ukv', 'v_w_out', 'v_final_g']
TWIN_OUTPUTS = ['loss', 'grad_x', 'grad_norm_g', 'grad_w_ada', 'grad_b_ada', 'grad_w_in', 'grad_b_f', 'grad_q_norm_g', 'grad_w_uq', 'grad_kv_norm_g', 'grad_w_ukv', 'grad_w_out', 'grad_final_g', 'delta_norm_g', 'delta_w_ada', 'delta_b_ada', 'delta_w_in', 'delta_b_f', 'delta_q_norm_g', 'delta_w_uq', 'delta_kv_norm_g', 'delta_w_ukv', 'delta_w_out', 'delta_final_g', 'new_m_norm_g', 'new_m_w_ada', 'new_m_b_ada', 'new_m_w_in', 'new_m_b_f', 'new_m_q_norm_g', 'new_m_w_uq', 'new_m_kv_norm_g', 'new_m_w_ukv', 'new_m_w_out', 'new_m_final_g', 'new_v_norm_g', 'new_v_w_ada', 'new_v_b_ada', 'new_v_w_in', 'new_v_b_f', 'new_v_q_norm_g', 'new_v_w_uq', 'new_v_kv_norm_g', 'new_v_w_ukv', 'new_v_w_out', 'new_v_final_g']
TWIN_LEAF_KINDS = {'loss': 'loss', 'grad_x': 'grad_x', 'grad_norm_g': 'grad_w', 'grad_w_ada': 'grad_w', 'grad_b_ada': 'grad_w', 'grad_w_in': 'grad_w', 'grad_b_f': 'grad_w', 'grad_q_norm_g': 'grad_w', 'grad_w_uq': 'grad_w', 'grad_kv_norm_g': 'grad_w', 'grad_w_ukv': 'grad_w', 'grad_w_out': 'grad_w', 'grad_final_g': 'grad_w', 'delta_norm_g': 'delta_w', 'delta_w_ada': 'delta_w', 'delta_b_ada': 'delta_w', 'delta_w_in': 'delta_w', 'delta_b_f': 'delta_w', 'delta_q_norm_g': 'delta_w', 'delta_w_uq': 'delta_w', 'delta_kv_norm_g': 'delta_w', 'delta_w_ukv': 'delta_w', 'delta_w_out': 'delta_w', 'delta_final_g': 'delta_w', 'new_m_norm_g': 'new_m', 'new_m_w_ada': 'new_m', 'new_m_b_ada': 'new_m', 'new_m_w_in': 'new_m', 'new_m_b_f': 'new_m', 'new_m_q_norm_g': 'new_m', 'new_m_w_uq': 'new_m', 'new_m_kv_norm_g': 'new_m', 'new_m_w_ukv': 'new_m', 'new_m_w_out': 'new_m', 'new_m_final_g': 'new_m', 'new_v_norm_g': 'new_v', 'new_v_w_ada': 'new_v', 'new_v_b_ada': 'new_v', 'new_v_w_in': 'new_v', 'new_v_b_f': 'new_v', 'new_v_q_norm_g': 'new_v', 'new_v_w_uq': 'new_v', 'new_v_kv_norm_g': 'new_v', 'new_v_w_ukv': 'new_v', 'new_v_w_out': 'new_v', 'new_v_final_g': 'new_v'}


def _forward(args):
    return _fwd_reference(*[args[k] for k in FWD_PARAMS])


def _output_shape():
    out = _jax.eval_shape(lambda: _forward(_fwd_setup_inputs(0)))
    return out.shape, out.dtype

N_MICROBATCH = 1
ADAM_LR = 0.001
ADAM_B1 = 0.9
ADAM_B2 = 0.999
ADAM_EPS = 1e-08
ADAM_WD = 0.01
ADAM_STEP = 10
PER_EXAMPLE_BATCH_AXIS = {'x': 0, 'c': 0, 'positions': 0, 'loss_target': 0}
SHARED_INPUTS = []
_WEIGHT_DTYPES = {'norm_g': _jnp.float32, 'w_ada': _jnp.float32, 'b_ada': _jnp.float32, 'w_in': _jnp.float32, 'b_f': _jnp.float32, 'q_norm_g': _jnp.float32, 'w_uq': _jnp.float32, 'kv_norm_g': _jnp.float32, 'w_ukv': _jnp.float32, 'w_out': _jnp.float32, 'final_g': _jnp.float32}
MOMENT_SCALE = {'norm_g': 2.482716e-02, 'w_ada': 2.483895e-02, 'b_ada': 4.034147e-02, 'w_in': 1.588845e-02, 'b_f': 7.148931e-02, 'q_norm_g': 7.735949e-03, 'w_uq': 4.374203e-03, 'kv_norm_g': 2.643855e-02, 'w_ukv': 8.715713e-03, 'w_out': 1.612447e-02, 'final_g': 3.199725e+01}


def _to_microbatches(a, axis):
    t = _jnp.moveaxis(a, axis, 0)
    t = t.reshape((N_MICROBATCH, t.shape[0] // N_MICROBATCH) + t.shape[1:])
    return _jnp.moveaxis(t, 1, axis + 1)


def setup_inputs(seed: int = 0) -> dict:
    inp = _fwd_setup_inputs(seed)
    key = _jax.random.fold_in(_jax.random.key(seed), 7919)
    shape, _ = _output_shape()
    out = dict(inp)
    out["loss_target"] = _jax.random.normal(_jax.random.fold_in(key, 0), shape, _jnp.float32)
    for i, name in enumerate(TWIN_WEIGHTS):
        w = inp[name].astype(_jnp.float32)
        if MOMENT_SCALE is None:
            s = _jnp.sqrt(_jnp.mean(_jnp.square(w)) + 1e-30)
        else:
            s = MOMENT_SCALE[name]
        km, kv = _jax.random.split(_jax.random.fold_in(key, i + 1))
        out[name] = w
        out["m_" + name] = s * _jax.random.normal(km, w.shape, _jnp.float32)
        out["v_" + name] = (s * s) * _jax.random.uniform(kv, w.shape, _jnp.float32, 0.5, 1.5)
    if N_MICROBATCH > 1:
        for name, axis in PER_EXAMPLE_BATCH_AXIS.items():
            out[name] = _to_microbatches(out[name], axis)
    return {'x': out['x'], 'c': out['c'], 'positions': out['positions'], 'norm_g': out['norm_g'], 'w_ada': out['w_ada'], 'b_ada': out['b_ada'], 'w_in': out['w_in'], 'b_f': out['b_f'], 'q_norm_g': out['q_norm_g'], 'w_uq': out['w_uq'], 'kv_norm_g': out['kv_norm_g'], 'w_ukv': out['w_ukv'], 'w_out': out['w_out'], 'final_g': out['final_g'], 'loss_target': out['loss_target'], 'm_norm_g': out['m_norm_g'], 'm_w_ada': out['m_w_ada'], 'm_b_ada': out['m_b_ada'], 'm_w_in': out['m_w_in'], 'm_b_f': out['m_b_f'], 'm_q_norm_g': out['m_q_norm_g'], 'm_w_uq': out['m_w_uq'], 'm_kv_norm_g': out['m_kv_norm_g'], 'm_w_ukv': out['m_w_ukv'], 'm_w_out': out['m_w_out'], 'm_final_g': out['m_final_g'], 'v_norm_g': out['v_norm_g'], 'v_w_ada': out['v_w_ada'], 'v_b_ada': out['v_b_ada'], 'v_w_in': out['v_w_in'], 'v_b_f': out['v_b_f'], 'v_q_norm_g': out['v_q_norm_g'], 'v_w_uq': out['v_w_uq'], 'v_kv_norm_g': out['v_kv_norm_g'], 'v_w_ukv': out['v_w_ukv'], 'v_w_out': out['v_w_out'], 'v_final_g': out['v_final_g']}


def _loss(weights, diff, rest, loss_target):
    with _jax.named_scope("forward"):
        args = {**rest, TWIN_DIFF_INPUT: diff, **{k: w.astype(_WEIGHT_DTYPES[k]) for k, w in weights.items()}}
        y = _forward(args)
    with _jax.named_scope("loss_head"):
        err = _jnp.square(y.astype(_jnp.float32) - loss_target)
        return 0.5 * _jnp.sum(_jnp.mean(err, axis=-1)) if err.ndim else 0.5 * err


def _adamw(w, g, m, v):
    m = ADAM_B1 * m + (1.0 - ADAM_B1) * g
    v = ADAM_B2 * v + (1.0 - ADAM_B2) * _jnp.square(g)
    m_hat = m / (1.0 - ADAM_B1 ** ADAM_STEP)
    v_hat = v / (1.0 - ADAM_B2 ** ADAM_STEP)
    delta = -ADAM_LR * (m_hat / (_jnp.sqrt(v_hat) + ADAM_EPS) + ADAM_WD * w)
    return delta, m, v


def reference(x, c, positions, norm_g, w_ada, b_ada, w_in, b_f, q_norm_g, w_uq, kv_norm_g, w_ukv, w_out, final_g, loss_target, m_norm_g, m_w_ada, m_b_ada, m_w_in, m_b_f, m_q_norm_g, m_w_uq, m_kv_norm_g, m_w_ukv, m_w_out, m_final_g, v_norm_g, v_w_ada, v_b_ada, v_w_in, v_b_f, v_q_norm_g, v_w_uq, v_kv_norm_g, v_w_ukv, v_w_out, v_final_g):
    given = dict(x=x, c=c, positions=positions, norm_g=norm_g, w_ada=w_ada, b_ada=b_ada, w_in=w_in, b_f=b_f, q_norm_g=q_norm_g, w_uq=w_uq, kv_norm_g=kv_norm_g, w_ukv=w_ukv, w_out=w_out, final_g=final_g, loss_target=loss_target, m_norm_g=m_norm_g, m_w_ada=m_w_ada, m_b_ada=m_b_ada, m_w_in=m_w_in, m_b_f=m_b_f, m_q_norm_g=m_q_norm_g, m_w_uq=m_w_uq, m_kv_norm_g=m_kv_norm_g, m_w_ukv=m_w_ukv, m_w_out=m_w_out, m_final_g=m_final_g, v_norm_g=v_norm_g, v_w_ada=v_w_ada, v_b_ada=v_b_ada, v_w_in=v_w_in, v_b_f=v_b_f, v_q_norm_g=v_q_norm_g, v_w_uq=v_w_uq, v_kv_norm_g=v_kv_norm_g, v_w_ukv=v_w_ukv, v_w_out=v_w_out, v_final_g=v_final_g)
    weights = {n: given[n] for n in TWIN_WEIGHTS}
    shared = {n: given[n] for n in SHARED_INPUTS}
    per_example = {n: given[n] for n in ['x', 'c', 'positions']}
    grad_fn = _jax.value_and_grad(_loss, argnums=(0, 1))

    def one_microbatch(ex, loss_target):
        ex = dict(ex)
        diff = ex.pop(TWIN_DIFF_INPUT)
        return grad_fn(weights, diff, {**shared, **ex}, loss_target)

    if N_MICROBATCH == 1:
        loss, (grad_w, grad_x) = one_microbatch(per_example, given["loss_target"])
    else:
        def body(carry, xs):
            loss_sum, grad_sum = carry
            l_k, (gw_k, gx_k) = one_microbatch(xs[0], xs[1])
            with _jax.named_scope("update"):
                return (loss_sum + l_k, _jax.tree.map(_jnp.add, grad_sum, gw_k)), gx_k

        init = (_jnp.zeros((), _jnp.float32), _jax.tree.map(_jnp.zeros_like, weights))
        (loss, grad_w), grad_x = _jax.lax.scan(body, init, (per_example, given["loss_target"]))
    with _jax.named_scope("update"):
        delta_w, new_m, new_v = {}, {}, {}
        for n in TWIN_WEIGHTS:
            delta_w[n], new_m[n], new_v[n] = _adamw(weights[n], grad_w[n], given["m_" + n], given["v_" + n])
    return (loss, grad_x, *[grad_w[n] for n in TWIN_WEIGHTS], *[delta_w[n] for n in TWIN_WEIGHTS],
            *[new_m[n] for n in TWIN_WEIGHTS], *[new_v[n] for n in TWIN_WEIGHTS])
```

```python
import functools

import jax
import jax.numpy as jnp
import numpy as np
from jax import lax
from jax.experimental import pallas as pl
from jax.experimental.pallas import tpu as pltpu

F32 = jnp.float32
MXU = jnp.bfloat16

N_DEV = 8
HEADS = 8
HEAD_DIM = 64
NOPE = 64
ROPE = 32
HALF_ROPE = ROPE // 2
MLA_QK = 128
Q_LORA = 256
KV_LORA = 128
CHUNK = 64
GROUP_W = HEADS * HEAD_DIM
EPS = 1e-6
ROPE_THETA = 10000.0
N_IN = 2984

Z_FQ, Z_FK, Z_FV, Z_FG, Z_MG, Z_QL, Z_KV, Z_MISC, Z_W = 0, 512, 1024, 1536, 2048, 2560, 2816, 2944, 3072
MISC_FF = ROPE

ADAM_LR = 0.001
ADAM_B1 = 0.9
ADAM_B2 = 0.999
ADAM_EPS = 1e-08
ADAM_WD = 0.01
ADAM_STEP = 10

VMEM_LIMIT_V7X = 56 * 1024 * 1024
LANES = 128

_NT = (((1,), (1,)), ((), ()))
_TN = (((0,), (0,)), ((), ()))


def _params(*sem):
    return pltpu.CompilerParams(dimension_semantics=sem, vmem_limit_bytes=VMEM_LIMIT_V7X)


def _sds(shape, dtype=F32):
    return jax.ShapeDtypeStruct(shape, dtype)


def _full(shape):
    nd = len(shape)
    return pl.BlockSpec(shape, lambda *_: (0,) * nd)


def _rows(tm, width, col=0):
    return pl.BlockSpec((tm, width), lambda i: (i, col))


def _exchange(arrs, gather, name):
    n = len(arrs)

    def kern(*refs):
        ins, outs = refs[:n], refs[n:2 * n]
        send_sems, recv_sems, loc_sems = refs[2 * n:]
        x, y, c = lax.axis_index("x"), lax.axis_index("y"), lax.axis_index("c")
        me = 4 * x + 2 * y + c

        def src(i, j):
            return ins[i] if gather[i] else ins[i].at[j]

        local = [pltpu.make_async_copy(src(i, me), outs[i].at[me], loc_sems.at[i]) for i in range(n)]
        for cp in local:
            cp.start()
        sends, recvs = [], []
        for k in range(1, N_DEV):
            px = 1 - x if k & 4 else x
            py = 1 - y if k & 2 else y
            pc = 1 - c if k & 1 else c
            p = 4 * px + 2 * py + pc
            for i in range(n):
                sends.append(pltpu.make_async_remote_copy(
                    src_ref=src(i, p), dst_ref=outs[i].at[me], send_sem=send_sems.at[i, k],
                    recv_sem=recv_sems.at[i, k], device_id=(px, py, pc), device_id_type=pl.DeviceIdType.MESH))
                recvs.append(pltpu.make_async_remote_copy(
                    src_ref=src(i, p), dst_ref=outs[i].at[p], send_sem=send_sems.at[i, k],
                    recv_sem=recv_sems.at[i, k], device_id=(px, py, pc), device_id_type=pl.DeviceIdType.MESH))
        for cp in sends:
            cp.start()
        for cp in recvs:
            cp.wait_recv()
        for cp in sends:
            cp.wait_send()
        for cp in local:
            cp.wait()

    out_shape = []
    for a, g in zip(arrs, gather):
        shp = (N_DEV,) + tuple(a.shape) if g else tuple(a.shape)
        out_shape.append(_sds(shp, a.dtype))
    return pl.pallas_call(
        kern, name=name, out_shape=out_shape,
        in_specs=[pl.BlockSpec(memory_space=pl.ANY)] * n,
        out_specs=[pl.BlockSpec(memory_space=pl.ANY)] * n,
        scratch_shapes=[pltpu.SemaphoreType.DMA((n, N_DEV)), pltpu.SemaphoreType.DMA((n, N_DEV)),
                        pltpu.SemaphoreType.DMA((n,))],
        compiler_params=pltpu.CompilerParams(has_side_effects=True),
    )(*arrs)


def _modpart(cact8, w_ada, b_cols):
    n_l, d, cw = w_ada.shape

    def kern(c_ref, w_ref, b_ref, o_ref):
        o_ref[0] = jnp.dot(c_ref[...].astype(MXU), w_ref[0].astype(MXU), preferred_element_type=F32) + b_ref[0]

    return pl.pallas_call(
        kern, name="modpart", grid=(n_l,), out_shape=_sds((n_l, N_DEV, cw)),
        in_specs=[_full((N_DEV, d)), pl.BlockSpec((1, d, cw), lambda l: (l, 0, 0)),
                  pl.BlockSpec((1, 1, cw), lambda l: (l, 0, 0))],
        out_specs=pl.BlockSpec((1, N_DEV, cw), lambda l: (l, 0, 0)),
        compiler_params=_params("arbitrary"),
    )(cact8, w_ada, b_cols)


def _ada_grad(cact_cols, dmod_cols):
    n_l, _, cw = dmod_cols.shape
    d = cact_cols.shape[1]

    def kern(c_ref, dm_ref, o_ref):
        acc = c_ref[0] * dm_ref[0, 0:1, :]
        for s in range(1, N_DEV):
            acc = acc + c_ref[s] * dm_ref[0, s:s + 1, :]
        o_ref[0] = acc

    return pl.pallas_call(
        kern, name="ada_grad", grid=(n_l,), out_shape=_sds((n_l, d, cw)),
        in_specs=[_full((N_DEV, d, 1)), pl.BlockSpec((1, N_DEV, cw), lambda l: (l, 0, 0))],
        out_specs=pl.BlockSpec((1, d, cw), lambda l: (l, 0, 0)),
        compiler_params=_params("arbitrary"),
    )(cact_cols, dmod_cols)


def _k_in(x, g, mod3, w, tm=256):
    s_len, d = x.shape

    def kern(x_ref, g_ref, mod_ref, w_ref, z_ref, h_ref):
        xv = x_ref[...]
        r = lax.rsqrt(jnp.mean(xv * xv, axis=-1, keepdims=True) + EPS)
        xn = xv * r * g_ref[...]
        h = (xn * (1.0 + mod_ref[1:2, :]) + mod_ref[0:1, :]).astype(MXU)
        h_ref[...] = h
        z_ref[...] = jnp.dot(h, w_ref[...], preferred_element_type=F32)

    return pl.pallas_call(
        kern, name="k_in", grid=(s_len // tm,),
        out_shape=[_sds((s_len, Z_W)), _sds((s_len, d), MXU)],
        in_specs=[_rows(tm, d), _full((1, d)), _full((3, d)), _full((d, Z_W))],
        out_specs=[_rows(tm, Z_W), _rows(tm, d)],
        compiler_params=_params("arbitrary"),
    )(x, g, mod3, w)


def _k_cum(fft, bf):
    nh, s_len = fft.shape

    def kern(ff_ref, b_ref, cum_ref):
        r_i = lax.broadcasted_iota(jnp.int32, (LANES, LANES), 0)
        c_i = lax.broadcasted_iota(jnp.int32, (LANES, LANES), 1)
        upper = (r_i <= c_i).astype(F32)
        carry = jnp.zeros((nh, 1), F32)
        for cb in range(s_len // LANES):
            sl = slice(cb * LANES, (cb + 1) * LANES)
            xc = ff_ref[:, sl] + b_ref[...]
            lf = jnp.minimum(xc, 0.0) - jnp.log(1.0 + jnp.exp(-jnp.abs(xc)))
            cum_ref[:, sl] = jnp.dot(lf, upper, precision=lax.Precision.HIGHEST,
                                     preferred_element_type=F32) + carry
            carry = carry + jnp.sum(lf, axis=1, keepdims=True)

    return pl.pallas_call(
        kern, name="k_cum", out_shape=_sds((nh, s_len)),
        in_specs=[pl.BlockSpec(memory_space=pltpu.VMEM)] * 2,
        out_specs=pl.BlockSpec(memory_space=pltpu.VMEM),
        compiler_params=_params(),
    )(fft, bf)


def _k_cum_bwd(dck, fft, bf):
    nh, s_len = fft.shape

    def kern(dc_ref, ff_ref, b_ref, dff_ref, db_ref):
        r_i = lax.broadcasted_iota(jnp.int32, (LANES, LANES), 0)
        c_i = lax.broadcasted_iota(jnp.int32, (LANES, LANES), 1)
        lower = (r_i >= c_i).astype(F32)
        carry = jnp.zeros((nh, 1), F32)
        db = jnp.zeros((nh, 1), F32)
        for cb in range(s_len // LANES - 1, -1, -1):
            sl = slice(cb * LANES, (cb + 1) * LANES)
            dc = dc_ref[:, sl]
            dlf = jnp.dot(dc, lower, precision=lax.Precision.HIGHEST, preferred_element_type=F32) + carry
            carry = carry + jnp.sum(dc, axis=1, keepdims=True)
            dff = dlf * jax.nn.sigmoid(-(ff_ref[:, sl] + b_ref[...]))
            dff_ref[:, sl] = dff
            db = db + jnp.sum(dff, axis=1, keepdims=True)
        db_ref[...] = jnp.broadcast_to(db, (nh, LANES))

    return pl.pallas_call(
        kern, name="k_cum_bwd", out_shape=[_sds((nh, s_len)), _sds((nh, LANES))],
        in_specs=[pl.BlockSpec(memory_space=pltpu.VMEM)] * 3,
        out_specs=[pl.BlockSpec(memory_space=pltpu.VMEM)] * 2,
        compiler_params=_params(),
    )(dck, fft, bf)


def _swap_halves(t):
    lane = lax.broadcasted_iota(jnp.int32, t.shape, 1)
    return jnp.where(lane < HALF_ROPE, pltpu.roll(t, LANES - HALF_ROPE, 1), pltpu.roll(t, HALF_ROPE, 1))


def _k_prep(z, cos8, sin8, cosk, sink, gq, gkv, wuq, wukv, tm=512):
    s_len = z.shape[0]

    def kern(ql_ref, kvl_ref, misc_ref, cos8_ref, sin8_ref, cosk_ref, sink_ref, gq_ref, gkv_ref,
             wuq_ref, wukv_ref, q_ref, kv_ref, kr_ref, qn_ref, kvn_ref):
        ql = ql_ref[...]
        rq = lax.rsqrt(jnp.mean(ql * ql, axis=-1, keepdims=True) + EPS)
        qn = (ql * rq * gq_ref[...]).astype(MXU)
        qn_ref[...] = qn
        q = jnp.dot(qn, wuq_ref[...], preferred_element_type=F32)
        r1, r2 = q[:, GROUP_W:GROUP_W + LANES], q[:, GROUP_W + LANES:]
        c8, s8 = cos8_ref[...], sin8_ref[...]
        q_ref[:, :GROUP_W] = q[:, :GROUP_W].astype(MXU)
        q_ref[:, GROUP_W:GROUP_W + LANES] = (r1 * c8 - r2 * s8).astype(MXU)
        q_ref[:, GROUP_W + LANES:] = (r2 * c8 + r1 * s8).astype(MXU)
        kvl = kvl_ref[...]
        rk = lax.rsqrt(jnp.mean(kvl * kvl, axis=-1, keepdims=True) + EPS)
        kvn = (kvl * rk * gkv_ref[...]).astype(MXU)
        kvn_ref[...] = kvn
        kv_ref[...] = jnp.dot(kvn, wukv_ref[...], preferred_element_type=F32).astype(MXU)
        misc = misc_ref[...]
        kr_ref[...] = (misc * cosk_ref[...] + _swap_halves(misc) * sink_ref[...]).astype(MXU)

    qw = GROUP_W + 2 * LANES
    return pl.pallas_call(
        kern, name="k_prep", grid=(s_len // tm,),
        out_shape=[_sds((s_len, qw), MXU), _sds((s_len, 2 * GROUP_W), MXU), _sds((s_len, LANES), MXU),
                   _sds((s_len, Q_LORA), MXU), _sds((s_len, KV_LORA), MXU)],
        in_specs=[_rows(tm, Q_LORA, Z_QL // Q_LORA), _rows(tm, KV_LORA, Z_KV // KV_LORA),
                  _rows(tm, LANES, Z_MISC // LANES), _rows(tm, LANES), _rows(tm, LANES), _rows(tm, LANES),
                  _rows(tm, LANES), _full((1, Q_LORA)), _full((1, KV_LORA)), _full((Q_LORA, qw)),
                  _full((KV_LORA, 2 * GROUP_W))],
        out_specs=[_rows(tm, qw), _rows(tm, 2 * GROUP_W), _rows(tm, LANES), _rows(tm, Q_LORA),
                   _rows(tm, KV_LORA)],
        compiler_params=_params("arbitrary"),
    )(z, z, z, cos8, sin8, cosk, sink, gq, gkv, wuq, wukv)


def _diag_mask(t, chunk_mask):
    row = lax.broadcasted_iota(jnp.int32, (t, t), 0)
    col = lax.broadcasted_iota(jnp.int32, (t, t), 1)
    if chunk_mask:
        return (col // CHUNK) <= (row // CHUNK)
    return col <= row


def _attn_fwd(q, k, v, cq, ck, scale, chunk_mask, name, t=256):
    nh, s_len, dqk = q.shape
    dv = v.shape[-1]
    nq = s_len // t
    has_bias = cq is not None

    def kern(*refs):
        if has_bias:
            q_ref, k_ref, v_ref, cq_ref, ck_ref, o_ref, lse_ref = refs
        else:
            q_ref, k_ref, v_ref, o_ref, lse_ref = refs

        def qbody(qi, _):
            qs = pl.multiple_of(qi * t, t)
            qt = q_ref[0, pl.ds(qs, t), :]
            cqt = cq_ref[0, pl.ds(qs, t), :] if has_bias else None

            def step(ki, carry, masked):
                m, l, acc = carry
                ks = pl.multiple_of(ki * t, t)
                kt = k_ref[0, pl.ds(ks, t), :]
                vt = v_ref[0, pl.ds(ks, t), :]
                s = lax.dot_general(qt, kt, _NT, preferred_element_type=F32) * scale
                if has_bias:
                    s = s + cqt - ck_ref[0, :, pl.ds(ks, t)]
                if masked:
                    s = jnp.where(_diag_mask(t, chunk_mask), s, -jnp.inf)
                m_new = jnp.maximum(m, jnp.max(s, axis=1, keepdims=True))
                alpha = jnp.exp(m - m_new)
                p = jnp.exp(s - m_new)
                l = alpha * l + jnp.sum(p, axis=1, keepdims=True)
                acc = alpha * acc + jnp.dot(p.astype(MXU), vt, preferred_element_type=F32)
                return m_new, l, acc

            init = (jnp.full((t, 1), -jnp.inf, F32), jnp.zeros((t, 1), F32), jnp.zeros((t, dv), F32))
            carry = lax.fori_loop(0, qi, lambda ki, cr: step(ki, cr, False), init)
            m, l, acc = step(qi, carry, True)
            o_ref[0, pl.ds(qs, t), :] = acc / l
            lse_ref[0, pl.ds(qs, t), :] = m + jnp.log(l)
            return 0

        lax.fori_loop(0, nq, qbody, 0)

    def head(shape):
        return pl.BlockSpec((1,) + shape, lambda h: (h, 0, 0))

    ins = [q, k, v] + ([cq, ck] if has_bias else [])
    in_specs = [head((s_len, dqk)), head((s_len, dqk)), head((s_len, dv))]
    if has_bias:
        in_specs += [head((s_len, 1)), head((1, s_len))]
    return pl.pallas_call(
        kern, name=name, grid=(nh,),
        out_shape=[_sds((nh, s_len, dv)), _sds((nh, s_len, 1))],
        in_specs=in_specs, out_specs=[head((s_len, dv)), head((s_len, 1))],
        compiler_params=_params("arbitrary"),
    )(*ins)


def _attn_bwd(q, k, v, o, do, lse, cq, ck, scale, chunk_mask, name, t=256):
    nh, s_len, dqk = q.shape
    dv = v.shape[-1]
    nq = s_len // t
    has_bias = cq is not None

    def kern(*refs):
        if has_bias:
            (q_ref, k_ref, v_ref, o_ref, do_ref, lse_ref, cq_ref, ck_ref,
             dq_ref, dk_ref, dv_ref, dck_ref, dcq_ref, delta_ref) = refs
            dcq_ref[...] = jnp.zeros_like(dcq_ref)
        else:
            q_ref, k_ref, v_ref, o_ref, do_ref, lse_ref, dq_ref, dk_ref, dv_ref, delta_ref = refs

        dq_ref[...] = jnp.zeros_like(dq_ref)

        def dbody(qi, _):
            qs = pl.multiple_of(qi * t, t)
            delta_ref[pl.ds(qs, t), :] = jnp.sum(do_ref[0, pl.ds(qs, t), :] * o_ref[0, pl.ds(qs, t), :],
                                                 axis=1, keepdims=True)
            return 0

        lax.fori_loop(0, nq, dbody, 0)

        def kbody(ki, _):
            ks = pl.multiple_of(ki * t, t)
            kt = k_ref[0, pl.ds(ks, t), :]
            vt = v_ref[0, pl.ds(ks, t), :]
            ckt = ck_ref[0, :, pl.ds(ks, t)] if has_bias else None

            def step(qi, carry, masked):
                dk, dvv, dc = carry
                qs = pl.multiple_of(qi * t, t)
                qt = q_ref[0, pl.ds(qs, t), :]
                dob = do_ref[0, pl.ds(qs, t), :].astype(MXU)
                s = lax.dot_general(qt, kt, _NT, preferred_element_type=F32) * scale
                if has_bias:
                    s = s + cq_ref[0, pl.ds(qs, t), :] - ckt
                p = jnp.exp(s - lse_ref[0, pl.ds(qs, t), :])
                if masked:
                    p = jnp.where(_diag_mask(t, chunk_mask), p, 0.0)
                dvv = dvv + lax.dot_general(p.astype(MXU), dob, _TN, preferred_element_type=F32)
                dp = lax.dot_general(dob, vt, _NT, preferred_element_type=F32)
                ds = p * (dp - delta_ref[pl.ds(qs, t), :])
                dsb = ds.astype(MXU)
                dk = dk + lax.dot_general(dsb, qt, _TN, preferred_element_type=F32)
                dq_ref[0, pl.ds(qs, t), :] += jnp.dot(dsb, kt, preferred_element_type=F32) * scale
                if has_bias:
                    dc = dc - jnp.sum(ds, axis=0, keepdims=True)
                    dcq_ref[0, pl.ds(qs, t), :] += jnp.sum(ds, axis=1, keepdims=True)
                return dk, dvv, dc

            init = (jnp.zeros((t, dqk), F32), jnp.zeros((t, dv), F32), jnp.zeros((1, t), F32))
            carry = step(ki, init, True)
            dk, dvv, dc = lax.fori_loop(ki + 1, nq, lambda qi, cr: step(qi, cr, False), carry)
            dk_ref[0, pl.ds(ks, t), :] = dk * scale
            dv_ref[0, pl.ds(ks, t), :] = dvv
            if has_bias:
                dck_ref[0, :, pl.ds(ks, t)] = dc
            return 0

        lax.fori_loop(0, nq, kbody, 0)

    def head(shape):
        return pl.BlockSpec((1,) + shape, lambda h: (h, 0, 0))

    ins = [q, k, v, o, do, lse] + ([cq, ck] if has_bias else [])
    in_specs = [head((s_len, dqk)), head((s_len, dqk)), head((s_len, dv)), head((s_len, dv)),
                head((s_len, dv)), head((s_len, 1))]
    out_shape = [_sds((nh, s_len, dqk)), _sds((nh, s_len, dqk)), _sds((nh, s_len, dv))]
    out_specs = [head((s_len, dqk)), head((s_len, dqk)), head((s_len, dv))]
    if has_bias:
        in_specs += [head((s_len, 1)), head((1, s_len))]
        out_shape += [_sds((nh, 1, s_len)), _sds((nh, s_len, 1))]
        out_specs += [head((1, s_len)), head((s_len, 1))]
    return pl.pallas_call(
        kern, name=name, grid=(nh,), out_shape=out_shape, in_specs=in_specs, out_specs=out_specs,
        scratch_shapes=[pltpu.VMEM((s_len, 1), F32)],
        compiler_params=_params("arbitrary"),
    )(*ins)


def _silu(a):
    return a * jax.nn.sigmoid(a)


def _k_out(of, om, z, x, gate, wout, tm=256):
    s_len, d = x.shape

    def kern(of_ref, om_ref, fg_ref, mg_ref, x_ref, gate_ref, w_ref, xo_ref, y_ref, u_ref):
        u_ref[:, :GROUP_W] = (of_ref[...] * _silu(fg_ref[...])).astype(MXU)
        u_ref[:, GROUP_W:] = (om_ref[...] * _silu(mg_ref[...])).astype(MXU)
        y = jnp.dot(u_ref[...], w_ref[...], preferred_element_type=F32)
        y_ref[...] = y
        xo_ref[...] = x_ref[...] + gate_ref[...] * y

    return pl.pallas_call(
        kern, name="k_out", grid=(s_len // tm,),
        out_shape=[_sds((s_len, d)), _sds((s_len, d)), _sds((s_len, 2 * GROUP_W), MXU)],
        in_specs=[_rows(tm, GROUP_W), _rows(tm, GROUP_W), _rows(tm, GROUP_W, Z_FG // GROUP_W),
                  _rows(tm, GROUP_W, Z_MG // GROUP_W), _rows(tm, d), _full((1, d)), _full((2 * GROUP_W, d))],
        out_specs=[_rows(tm, d), _rows(tm, d), _rows(tm, 2 * GROUP_W)],
        compiler_params=_params("arbitrary"),
    )(of, om, z, z, x, gate, wout)


def _k_loss(x, gf, tgt, tm=256):
    s_len, d = x.shape

    def kern(x_ref, g_ref, t_ref, loss_ref, dx_ref, dg_ref):
        i = pl.program_id(0)
        xv = x_ref[...]
        r = lax.rsqrt(jnp.mean(xv * xv, axis=-1, keepdims=True) + EPS)
        xh = xv * r
        diff = xh * g_ref[...] - t_ref[...]
        part = 0.5 * jnp.sum(jnp.mean(diff * diff, axis=-1, keepdims=True))
        dout = diff * (1.0 / d)
        dxh = dout * g_ref[...]
        dx_ref[...] = r * (dxh - xh * jnp.mean(dxh * xh, axis=-1, keepdims=True))

        @pl.when(i == 0)
        def _():
            loss_ref[...] = jnp.zeros_like(loss_ref)
            dg_ref[...] = jnp.zeros_like(dg_ref)

        loss_ref[...] += jnp.full(loss_ref.shape, part, F32)
        dg_ref[...] += jnp.sum(dout * xh, axis=0, keepdims=True)

    return pl.pallas_call(
        kern, name="k_loss", grid=(s_len // tm,),
        out_shape=[_sds((1, LANES)), _sds((s_len, d)), _sds((1, d))],
        in_specs=[_rows(tm, d), _full((1, d)), _rows(tm, d)],
        out_specs=[_full((1, LANES)), _rows(tm, d), _full((1, d))],
        compiler_params=_params("arbitrary"),
    )(x, gf, tgt)


def _kb_out(dxo, y, gate, wout_t, of, om, z, tm=256):
    s_len, d = dxo.shape

    def kern(dxo_ref, y_ref, gate_ref, wt_ref, of_ref, om_ref, fg_ref, mg_ref,
             dy_ref, dof_ref, dom_ref, dfg_ref, dmg_ref, dgate_ref):
        i = pl.program_id(0)
        dxv = dxo_ref[...]

        @pl.when(i == 0)
        def _():
            dgate_ref[...] = jnp.zeros_like(dgate_ref)

        dgate_ref[...] += jnp.sum(dxv * y_ref[...], axis=0, keepdims=True)
        dyb = (dxv * gate_ref[...]).astype(MXU)
        dy_ref[...] = dyb
        du = jnp.dot(dyb, wt_ref[...], preferred_element_type=F32)
        for du_g, o_ref, g_ref, do_ref, dg_ref in ((du[:, :GROUP_W], of_ref, fg_ref, dof_ref, dfg_ref),
                                                   (du[:, GROUP_W:], om_ref, mg_ref, dom_ref, dmg_ref)):
            a = g_ref[...]
            sg = jax.nn.sigmoid(a)
            do_ref[...] = du_g * (a * sg)
            dg_ref[...] = (du_g * o_ref[...] * (sg * (1.0 + a * (1.0 - sg)))).astype(MXU)

    return pl.pallas_call(
        kern, name="kb_out", grid=(s_len // tm,),
        out_shape=[_sds((s_len, d), MXU), _sds((s_len, GROUP_W)), _sds((s_len, GROUP_W)),
                   _sds((s_len, GROUP_W), MXU), _sds((s_len, GROUP_W), MXU), _sds((1, d))],
        in_specs=[_rows(tm, d), _rows(tm, d), _full((1, d)), _full((d, 2 * GROUP_W)), _rows(tm, GROUP_W),
                  _rows(tm, GROUP_W), _rows(tm, GROUP_W, Z_FG // GROUP_W), _rows(tm, GROUP_W, Z_MG // GROUP_W)],
        out_specs=[_rows(tm, d), _rows(tm, GROUP_W), _rows(tm, GROUP_W), _rows(tm, GROUP_W),
                   _rows(tm, GROUP_W), _full((1, d))],
        compiler_params=_params("arbitrary"),
    )(dxo, y, gate, wout_t, of, om, z, z)


def _kb_prep(dqr, dkv, dkr, dff, z, cos8, sin8, cosk, sink, gq, gkv, wuq_t, wukv_t, tm=512):
    s_len = z.shape[0]
    qw = GROUP_W + 2 * LANES
    tail = Q_LORA + KV_LORA + LANES

    def kern(dqr_ref, dkv_ref, dkr_ref, dff_ref, ql_ref, kvl_ref, cos8_ref, sin8_ref, cosk_ref, sink_ref,
             gq_ref, gkv_ref, wuqt_ref, wukvt_ref, dq_ref, dz_ref, dgq_ref, dgkv_ref):
        i = pl.program_id(0)

        @pl.when(i == 0)
        def _():
            dgq_ref[...] = jnp.zeros_like(dgq_ref)
            dgkv_ref[...] = jnp.zeros_like(dgkv_ref)

        c8, s8 = cos8_ref[...], sin8_ref[...]
        do1 = dqr_ref[:, GROUP_W:GROUP_W + LANES]
        do2 = dqr_ref[:, GROUP_W + LANES:]
        dq_ref[:, :GROUP_W] = dqr_ref[:, :GROUP_W].astype(MXU)
        dq_ref[:, GROUP_W:GROUP_W + LANES] = (do1 * c8 + do2 * s8).astype(MXU)
        dq_ref[:, GROUP_W + LANES:] = (do2 * c8 - do1 * s8).astype(MXU)
        dqn = jnp.dot(dq_ref[...], wuqt_ref[...], preferred_element_type=F32)
        ql = ql_ref[...]
        rq = lax.rsqrt(jnp.mean(ql * ql, axis=-1, keepdims=True) + EPS)
        qh = ql * rq
        dgq_ref[...] += jnp.sum(dqn * qh, axis=0, keepdims=True)
        dqh = dqn * gq_ref[...]
        dz_ref[:, :Q_LORA] = (rq * (dqh - qh * jnp.mean(dqh * qh, axis=-1, keepdims=True))).astype(MXU)

        dkvn = jnp.dot(dkv_ref[...], wukvt_ref[...], preferred_element_type=F32)
        kvl = kvl_ref[...]
        rk = lax.rsqrt(jnp.mean(kvl * kvl, axis=-1, keepdims=True) + EPS)
        kh = kvl * rk
        dgkv_ref[...] += jnp.sum(dkvn * kh, axis=0, keepdims=True)
        dkh = dkvn * gkv_ref[...]
        dz_ref[:, Q_LORA:Q_LORA + KV_LORA] = (
            rk * (dkh - kh * jnp.mean(dkh * kh, axis=-1, keepdims=True))).astype(MXU)

        dkr_v = dkr_ref[...]
        dz_ref[:, Q_LORA + KV_LORA:] = (dkr_v * cosk_ref[...] - _swap_halves(dkr_v) * sink_ref[...]
                                        + dff_ref[...]).astype(MXU)

    return pl.pallas_call(
        kern, name="kb_prep", grid=(s_len // tm,),
        out_shape=[_sds((s_len, qw), MXU), _sds((s_len, tail), MXU), _sds((1, Q_LORA)), _sds((1, KV_LORA))],
        in_specs=[_rows(tm, qw), _rows(tm, 2 * GROUP_W), _rows(tm, LANES), _rows(tm, LANES),
                  _rows(tm, Q_LORA, Z_QL // Q_LORA), _rows(tm, KV_LORA, Z_KV // KV_LORA),
                  _rows(tm, LANES), _rows(tm, LANES), _rows(tm, LANES), _rows(tm, LANES),
                  _full((1, Q_LORA)), _full((1, KV_LORA)), _full((qw, Q_LORA)), _full((2 * GROUP_W, KV_LORA))],
        out_specs=[_rows(tm, qw), _rows(tm, tail), _full((1, Q_LORA)), _full((1, KV_LORA))],
        compiler_params=_params("arbitrary"),
    )(dqr, dkv, dkr, dff, z, z, cos8, sin8, cosk, sink, gq, gkv, wuq_t, wukv_t)


def _kb_in(dz, win_t, x, g, mod3, dxo, tm=256):
    s_len, d = x.shape

    def kern(dz_ref, wt_ref, x_ref, g_ref, mod_ref, dxo_ref, dx_ref, acc_ref):
        i = pl.program_id(0)

        @pl.when(i == 0)
        def _():
            acc_ref[...] = jnp.zeros_like(acc_ref)

        dh = jnp.dot(dz_ref[...], wt_ref[...], preferred_element_type=F32)
        xv = x_ref[...]
        r = lax.rsqrt(jnp.mean(xv * xv, axis=-1, keepdims=True) + EPS)
        xh = xv * r
        xn = xh * g_ref[...]
        dxn = dh * (1.0 + mod_ref[1:2, :])
        acc_ref[0:1, :] += jnp.sum(dh, axis=0, keepdims=True)
        acc_ref[1:2, :] += jnp.sum(dh * xn, axis=0, keepdims=True)
        acc_ref[2:3, :] += jnp.sum(dxn * xh, axis=0, keepdims=True)
        dxh = dxn * g_ref[...]
        dx_ref[...] = dxo_ref[...] + r * (dxh - xh * jnp.mean(dxh * xh, axis=-1, keepdims=True))

    return pl.pallas_call(
        kern, name="kb_in", grid=(s_len // tm,),
        out_shape=[_sds((s_len, d)), _sds((3, d))],
        in_specs=[_rows(tm, Z_W), _full((Z_W, d)), _rows(tm, d), _full((1, d)), _full((3, d)), _rows(tm, d)],
        out_specs=[_rows(tm, d), _full((3, d))],
        compiler_params=_params("arbitrary"),
    )(dz, win_t, x, g, mod3, dxo)


def _matmul_acc(a, b, name, tk=512):
    m, kk = a.shape
    n = b.shape[1]
    tn = n if n <= 1536 else 1536
    tk = min(tk, kk)

    def kern(a_ref, b_ref, o_ref):
        @pl.when(pl.program_id(1) == 0)
        def _():
            o_ref[...] = jnp.zeros_like(o_ref)

        o_ref[...] += jnp.dot(a_ref[...], b_ref[...], preferred_element_type=F32)

    return pl.pallas_call(
        kern, name=name, grid=(n // tn, kk // tk), out_shape=_sds((m, n)),
        in_specs=[pl.BlockSpec((m, tk), lambda j, k: (0, k)), pl.BlockSpec((tk, tn), lambda j, k: (k, j))],
        out_specs=pl.BlockSpec((m, tn), lambda j, k: (0, j)),
        compiler_params=_params("arbitrary", "arbitrary"),
    )(a, b)


def _adamw(gslots, w, m, v, name):
    n, r, c = gslots.shape
    tm = r
    for cand in (256, 128, 64, 32, 16, 8):
        if r % cand == 0:
            tm = cand
            break

    def kern(g_ref, w_ref, m_ref, v_ref, go_ref, d_ref, mo_ref, vo_ref):
        g = g_ref[0]
        for s in range(1, n):
            g = g + g_ref[s]
        m_new = ADAM_B1 * m_ref[...] + (1.0 - ADAM_B1) * g
        v_new = ADAM_B2 * v_ref[...] + (1.0 - ADAM_B2) * (g * g)
        m_hat = m_new / (1.0 - ADAM_B1 ** ADAM_STEP)
        v_hat = v_new / (1.0 - ADAM_B2 ** ADAM_STEP)
        go_ref[...] = g
        mo_ref[...] = m_new
        vo_ref[...] = v_new
        d_ref[...] = -ADAM_LR * (m_hat / (jnp.sqrt(v_hat) + ADAM_EPS) + ADAM_WD * w_ref[...])

    row = pl.BlockSpec((tm, c), lambda i: (i, 0))
    return pl.pallas_call(
        kern, name=name, grid=(r // tm,), out_shape=[_sds((r, c))] * 4,
        in_specs=[pl.BlockSpec((n, tm, c), lambda i: (0, i, 0)), row, row, row],
        out_specs=[row] * 4,
        compiler_params=_params("arbitrary"),
    )(gslots, w, m, v)


def _perm_w_in(w):
    pad = jnp.zeros(w.shape[:-1] + (Z_W - Z_MISC - ROPE - HEADS,), w.dtype)
    return jnp.concatenate([w[..., 0:1536], w[..., 1544:2056], w[..., 2472:2984], w[..., 2056:2312],
                            w[..., 2312:2440], w[..., 2440:2472], w[..., 1536:1544], pad], axis=-1)


def _unperm_w_in(g):
    ff0 = Z_MISC + MISC_FF
    return jnp.concatenate([g[..., 0:1536], g[..., ff0:ff0 + HEADS], g[..., Z_FG:Z_FG + GROUP_W],
                            g[..., Z_QL:Z_QL + Q_LORA], g[..., Z_KV:Z_KV + KV_LORA],
                            g[..., Z_MISC:Z_MISC + ROPE], g[..., Z_MG:Z_MG + GROUP_W]], axis=-1)


def _perm_w_uq(w):
    wh = w.reshape(w.shape[:-1] + (HEADS, NOPE + ROPE))
    lead = w.shape[:-1]
    return jnp.concatenate([wh[..., :NOPE].reshape(lead + (GROUP_W,)),
                            wh[..., NOPE:NOPE + HALF_ROPE].reshape(lead + (LANES,)),
                            wh[..., NOPE + HALF_ROPE:].reshape(lead + (LANES,))], axis=-1)


def _unperm_w_uq(g):
    lead = g.shape[:-1]
    parts = [g[..., :GROUP_W].reshape(lead + (HEADS, NOPE)),
             g[..., GROUP_W:GROUP_W + LANES].reshape(lead + (HEADS, HALF_ROPE)),
             g[..., GROUP_W + LANES:].reshape(lead + (HEADS, HALF_ROPE))]
    return jnp.concatenate(parts, axis=-1).reshape(lead + (HEADS * (NOPE + ROPE),))


def _perm_w_ukv(w):
    lead = w.shape[:-1]
    wh = w.reshape(lead + (HEADS, 2 * HEAD_DIM))
    return jnp.concatenate([wh[..., :NOPE].reshape(lead + (GROUP_W,)),
                            wh[..., NOPE:].reshape(lead + (GROUP_W,))], axis=-1)


def _unperm_w_ukv(g):
    lead = g.shape[:-1]
    parts = [g[..., :GROUP_W].reshape(lead + (HEADS, NOPE)), g[..., GROUP_W:].reshape(lead + (HEADS, HEAD_DIM))]
    return jnp.concatenate(parts, axis=-1).reshape(lead + (2 * GROUP_W,))


def _to_heads(a, width):
    return a.reshape(a.shape[0], HEADS, width).transpose(1, 0, 2)


def _from_heads(a):
    return a.transpose(1, 0, 2).reshape(a.shape[1], -1)


def _rope_tables(positions):
    inv_freq = 1.0 / (ROPE_THETA ** (jnp.arange(0, ROPE, 2, dtype=F32) / ROPE))
    ang = positions.astype(F32)[:, None] * inv_freq
    cos, sin = jnp.cos(ang), jnp.sin(ang)
    zeros = jnp.zeros((cos.shape[0], LANES - ROPE), F32)
    cos8, sin8 = jnp.tile(cos, (1, HEADS)), jnp.tile(sin, (1, HEADS))
    cosk = jnp.concatenate([cos, cos, zeros], axis=1)
    sink = jnp.concatenate([-sin, sin, zeros], axis=1)
    return cos8, sin8, cosk, sink


def _local_step(x, mod, positions, loss_target, norm_g, b_f, q_norm_g, kv_norm_g, final_g, w_in, w_uq, w_ukv, w_out):
    n_l = norm_g.shape[0]
    s_len, d = x.shape
    cos8, sin8, cosk, sink = _rope_tables(positions)
    fox_scale = HEAD_DIM ** -0.5
    mla_scale = (NOPE + ROPE) ** -0.5
    zpad = jnp.zeros((s_len, HEADS, MLA_QK - NOPE - ROPE), MXU)
    saved = []
    for l in range(n_l):
        mod3 = mod[l].reshape(3, d)
        z, h = _k_in(x, norm_g[l][None], mod3, w_in[l])
        fq = _to_heads(z[:, Z_FQ:Z_FQ + GROUP_W].astype(MXU), HEAD_DIM)
        fk = _to_heads(z[:, Z_FK:Z_FK + GROUP_W].astype(MXU), HEAD_DIM)
        fv = _to_heads(z[:, Z_FV:Z_FV + GROUP_W].astype(MXU), HEAD_DIM)
        fft = z[:, Z_MISC + MISC_FF:Z_MISC + MISC_FF + HEADS].T
        bf = b_f[l][:, None]
        cum = _k_cum(fft, bf)
        cq, ck = cum[:, :, None], cum[:, None, :]
        o_fox, lse_fox = _attn_fwd(fq, fk, fv, cq, ck, fox_scale, False, "fox_fwd")
        qr, kvp, kr, qn, kvn = _k_prep(z, cos8, sin8, cosk, sink, q_norm_g[l][None], kv_norm_g[l][None],
                                       w_uq[l], w_ukv[l])
        mq = jnp.concatenate([qr[:, :GROUP_W].reshape(s_len, HEADS, NOPE),
                              qr[:, GROUP_W:GROUP_W + LANES].reshape(s_len, HEADS, HALF_ROPE),
                              qr[:, GROUP_W + LANES:].reshape(s_len, HEADS, HALF_ROPE), zpad],
                             axis=-1).transpose(1, 0, 2)
        mk = jnp.concatenate([kvp[:, :GROUP_W].reshape(s_len, HEADS, NOPE),
                              jnp.broadcast_to(kr[:, None, :ROPE], (s_len, HEADS, ROPE)), zpad],
                             axis=-1).transpose(1, 0, 2)
        mv = _to_heads(kvp[:, GROUP_W:], HEAD_DIM)
        o_mla, lse_mla = _attn_fwd(mq, mk, mv, None, None, mla_scale, True, "mla_fwd")
        of, om = _from_heads(o_fox), _from_heads(o_mla)
        x_new, y, u = _k_out(of, om, z, x, mod3[2:3], w_out[l])
        saved.append((x, z, h, fq, fk, fv, fft, bf, cq, ck, o_fox, lse_fox, mq, mk, mv, o_mla, lse_mla,
                      of, om, qn, kvn, y, u, mod3))
        x = x_new

    loss_row, dx, dfinal = _k_loss(x, final_g[None], loss_target)

    grads = {k: [] for k in ("norm_g", "mod", "w_in", "b_f", "q_norm_g", "w_uq", "kv_norm_g", "w_ukv", "w_out")}
    for l in range(n_l - 1, -1, -1):
        (x_l, z, h, fq, fk, fv, fft, bf, cq, ck, o_fox, lse_fox, mq, mk, mv, o_mla, lse_mla,
         of, om, qn, kvn, y, u, mod3) = saved[l]
        wout_t = w_out[l].T
        dyb, dof, dom, dfg, dmg, dgate = _kb_out(dx, y, mod3[2:3], wout_t, of, om, z)
        grads["w_out"].append(_matmul_acc(u.T, dyb, "dw_out"))
        dfq, dfk, dfv, dck, dcq = _attn_bwd(fq, fk, fv, o_fox, _to_heads(dof, HEAD_DIM), lse_fox, cq, ck,
                                       fox_scale, False, "fox_bwd")
        dfft, dbf = _k_cum_bwd(dck.reshape(HEADS, s_len) + dcq.reshape(HEADS, s_len), fft, bf)
        grads["b_f"].append(dbf[:, 0])
        dmq, dmk, dmv = _attn_bwd(mq, mk, mv, o_mla, _to_heads(dom, HEAD_DIM), lse_mla, None, None,
                                  mla_scale, True, "mla_bwd")
        dmq_t = dmq.transpose(1, 0, 2)
        dqr = jnp.concatenate([dmq_t[:, :, :NOPE].reshape(s_len, GROUP_W),
                               dmq_t[:, :, NOPE:NOPE + HALF_ROPE].reshape(s_len, LANES),
                               dmq_t[:, :, NOPE + HALF_ROPE:NOPE + ROPE].reshape(s_len, LANES)], axis=1)
        dmk_t = dmk.transpose(1, 0, 2)
        dkvp = jnp.concatenate([dmk_t[:, :, :NOPE].reshape(s_len, GROUP_W), _from_heads(dmv)],
                               axis=1).astype(MXU)
        dkr = jnp.pad(jnp.sum(dmk_t[:, :, NOPE:NOPE + ROPE], axis=1), ((0, 0), (0, LANES - ROPE)))
        dff = jnp.pad(dfft.T, ((0, 0), (MISC_FF, LANES - MISC_FF - HEADS)))
        dq_b, dz_tail, dgq, dgkv = _kb_prep(dqr, dkvp, dkr, dff, z, cos8, sin8, cosk, sink,
                                            q_norm_g[l][None], kv_norm_g[l][None], w_uq[l].T, w_ukv[l].T)
        grads["q_norm_g"].append(dgq[0])
        grads["kv_norm_g"].append(dgkv[0])
        grads["w_uq"].append(_matmul_acc(qn.T, dq_b, "dw_uq"))
        grads["w_ukv"].append(_matmul_acc(kvn.T, dkvp, "dw_ukv"))
        dz = jnp.concatenate([_from_heads(dfq).astype(MXU), _from_heads(dfk).astype(MXU),
                              _from_heads(dfv).astype(MXU), dfg, dmg, dz_tail], axis=1)
        grads["w_in"].append(_matmul_acc(h.T, dz, "dw_in"))
        dx, acc3 = _kb_in(dz, w_in[l].T, x_l, norm_g[l][None], mod3, dx)
        grads["norm_g"].append(acc3[2])
        grads["mod"].append(jnp.concatenate([acc3[0], acc3[1], dgate[0]]))
    grads = {k: jnp.stack(v[::-1]) for k, v in grads.items()}
    grads["final_g"] = dfinal[0]
    return loss_row[0, 0], dx, grads


def _pack_small(parts, total):
    flat = jnp.concatenate([p.reshape(-1) for p in parts])
    return jnp.pad(flat, (0, total - flat.shape[0])).reshape(total // LANES, LANES)


def kernel(x, c, positions, norm_g, w_ada, b_ada, w_in, b_f, q_norm_g, w_uq, kv_norm_g, w_ukv, w_out, final_g, loss_target, m_norm_g, m_w_ada, m_b_ada, m_w_in, m_b_f, m_q_norm_g, m_w_uq, m_kv_norm_g, m_w_ukv, m_w_out, m_final_g, v_norm_g, v_w_ada, v_b_ada, v_w_in, v_b_f, v_q_norm_g, v_w_uq, v_kv_norm_g, v_w_ukv, v_w_out, v_final_g):
    n_l, d = norm_g.shape
    s_len = x.shape[1]
    me = 4 * lax.axis_index("x") + 2 * lax.axis_index("y") + lax.axis_index("c")
    ada_c = w_ada.shape[2]
    in_c = w_in.shape[2]
    uq_c = w_uq.shape[2]
    ukv_c = w_ukv.shape[2]
    out_r = w_out.shape[1]

    cact = jnp.broadcast_to(jax.nn.silu(c), (N_DEV, d))
    g_in, g_uq, g_ukv, g_out, g_cact = _exchange(
        [w_in.reshape(n_l * d, in_c).astype(MXU), w_uq.reshape(n_l * Q_LORA, uq_c).astype(MXU),
         w_ukv.reshape(n_l * KV_LORA, ukv_c).astype(MXU), w_out.reshape(n_l * out_r, d).astype(MXU), cact],
        [True] * 5, "gather_weights")
    w_in_f = _perm_w_in(g_in.reshape(N_DEV, n_l, d, in_c).transpose(1, 2, 0, 3).reshape(n_l, d, N_DEV * in_c))
    w_uq_f = _perm_w_uq(g_uq.reshape(N_DEV, n_l, Q_LORA, uq_c).transpose(1, 2, 0, 3).reshape(n_l, Q_LORA, -1))
    w_ukv_f = _perm_w_ukv(g_ukv.reshape(N_DEV, n_l, KV_LORA, ukv_c).transpose(1, 2, 0, 3).reshape(n_l, KV_LORA, -1))
    w_out_f = g_out.reshape(N_DEV, n_l, out_r, d).transpose(1, 0, 2, 3).reshape(n_l, N_DEV * out_r, d)
    cact_all = g_cact[:, 0, :]

    b_cols = lax.dynamic_slice_in_dim(b_ada, me * ada_c, ada_c, axis=1)[:, None, :]
    modpart = _modpart(cact_all, w_ada, b_cols)
    mod_send = jnp.pad(modpart.transpose(1, 0, 2), ((0, 0), (0, 8 - n_l), (0, 0)))
    (mod_recv,) = _exchange([mod_send], [False], "scatter_mod")
    mod = mod_recv.transpose(1, 0, 2).reshape(8, N_DEV * ada_c)[:n_l]

    loss, dx, gr = _local_step(x[0], mod, positions[0], loss_target[0], norm_g, b_f, q_norm_g, kv_norm_g,
                               final_g, w_in_f, w_uq_f, w_ukv_f, w_out_f)

    s_in = _unperm_w_in(gr["w_in"]).reshape(n_l, d, N_DEV, in_c).transpose(2, 0, 1, 3).reshape(N_DEV, n_l * d, in_c)
    s_uq = _unperm_w_uq(gr["w_uq"]).reshape(n_l, Q_LORA, N_DEV, uq_c).transpose(2, 0, 1, 3).reshape(
        N_DEV, n_l * Q_LORA, uq_c)
    s_ukv = _unperm_w_ukv(gr["w_ukv"]).reshape(n_l, KV_LORA, N_DEV, ukv_c).transpose(2, 0, 1, 3).reshape(
        N_DEV, n_l * KV_LORA, ukv_c)
    s_out = gr["w_out"].reshape(n_l, N_DEV, out_r, d).transpose(1, 0, 2, 3).reshape(N_DEV, n_l * out_r, d)
    small_parts = [gr["norm_g"], gr["mod"], gr["b_f"], gr["q_norm_g"], gr["kv_norm_g"], gr["final_g"], cact[0]]
    sizes = [int(np.prod(p.shape)) for p in small_parts]
    total = -(-sum(sizes) // 1024) * 1024
    small = _pack_small(small_parts, total)
    r_in, r_uq, r_ukv, r_out, r_small = _exchange([s_in, s_uq, s_ukv, s_out, small],
                                                  [False, False, False, False, True], "exchange_grads")

    def upd(slots, w, m, v, name):
        shp = w.shape
        w2, m2, v2 = (a.reshape(slots.shape[1:]) for a in (w, m, v))
        return [o.reshape(shp) for o in _adamw(slots, w2, m2, v2, name)]

    o_in = upd(r_in, w_in, m_w_in, v_w_in, "adamw_w_in")
    o_uq = upd(r_uq, w_uq, m_w_uq, v_w_uq, "adamw_w_uq")
    o_ukv = upd(r_ukv, w_ukv, m_w_ukv, v_w_ukv, "adamw_w_ukv")
    o_out = upd(r_out, w_out, m_w_out, v_w_out, "adamw_w_out")

    offs = np.cumsum([0] + sizes)
    flat_all = r_small.reshape(N_DEV, total)
    dmod_all = flat_all[:, offs[1]:offs[2]].reshape(N_DEV, n_l, 3 * d)
    dmod_cols = lax.dynamic_slice_in_dim(dmod_all, me * ada_c, ada_c, axis=2).transpose(1, 0, 2)
    cact_cols = flat_all[:, offs[6]:offs[7]][:, :, None]
    g_ada = _ada_grad(cact_cols, dmod_cols)
    o_ada = upd(g_ada.reshape(1, n_l * d, ada_c), w_ada, m_w_ada, v_w_ada, "adamw_w_ada")

    zero_c = jnp.zeros((d,), F32)
    small_w = [_pack_small([norm_g, b_ada, b_f, q_norm_g, kv_norm_g, final_g, zero_c], total),
               _pack_small([m_norm_g, m_b_ada, m_b_f, m_q_norm_g, m_kv_norm_g, m_final_g, zero_c], total),
               _pack_small([v_norm_g, v_b_ada, v_b_f, v_q_norm_g, v_kv_norm_g, v_final_g, zero_c], total)]
    o_small = [o.reshape(-1) for o in _adamw(r_small, *small_w, "adamw_small")]
    shapes = [norm_g.shape, b_ada.shape, b_f.shape, q_norm_g.shape, kv_norm_g.shape, final_g.shape]

    def small_out(kind, idx):
        return o_small[kind][offs[idx]:offs[idx + 1]].reshape(shapes[idx])

    loss_all = lax.psum(loss, ("x", "y", "c"))
    outs = [loss_all, dx[None]]
    for kind in range(4):
        outs += [small_out(kind, 0), o_ada[kind], small_out(kind, 1), o_in[kind], small_out(kind, 2),
                 small_out(kind, 3), o_uq[kind], small_out(kind, 4), o_ukv[kind], o_out[kind], small_out(kind, 5)]
    return tuple(outs)
```

```python
import functools

import jax
import jax.numpy as jnp
import numpy as np
from jax import lax
from jax.experimental import pallas as pl
from jax.experimental.pallas import tpu as pltpu

F32 = jnp.float32
MXU = jnp.bfloat16

N_DEV = 8
HEADS = 8
HEAD_DIM = 64
NOPE = 64
ROPE = 32
HALF_ROPE = ROPE // 2
MLA_QK = 128
Q_LORA = 256
KV_LORA = 128
CHUNK = 64
GROUP_W = HEADS * HEAD_DIM
EPS = 1e-6
ROPE_THETA = 10000.0
N_IN = 2984

Z_FQ, Z_FK, Z_FV, Z_FG, Z_MG, Z_QL, Z_KV, Z_MISC, Z_W = 0, 512, 1024, 1536, 2048, 2560, 2816, 2944, 3072
MISC_FF = ROPE

ADAM_LR = 0.001
ADAM_B1 = 0.9
ADAM_B2 = 0.999
ADAM_EPS = 1e-08
ADAM_WD = 0.01
ADAM_STEP = 10

VMEM_LIMIT_V7X = 56 * 1024 * 1024
LANES = 128
ATTN_TILE = 512
LOG2E = 1.4426950408889634
FOX_SCALE = HEAD_DIM ** -0.5
MLA_SCALE = (NOPE + ROPE) ** -0.5
L_ROW = HEAD_DIM
AUG = 3

_NT = (((1,), (1,)), ((), ()))
_TN = (((0,), (0,)), ((), ()))


def _params(*sem):
    return pltpu.CompilerParams(dimension_semantics=sem, vmem_limit_bytes=VMEM_LIMIT_V7X)


def _sds(shape, dtype=F32):
    return jax.ShapeDtypeStruct(shape, dtype)


def _full(shape):
    nd = len(shape)
    return pl.BlockSpec(shape, lambda *_: (0,) * nd)


def _rows(tm, width, col=0):
    return pl.BlockSpec((tm, width), lambda i: (i, col))


def _exchange(arrs, gather, name):
    n = len(arrs)

    def kern(*refs):
        ins, outs = refs[:n], refs[n:2 * n]
        send_sems, recv_sems, loc_sems = refs[2 * n:]
        x, y, c = lax.axis_index("x"), lax.axis_index("y"), lax.axis_index("c")
        me = 4 * x + 2 * y + c

        def src(i, j):
            return ins[i] if gather[i] else ins[i].at[j]

        local = [pltpu.make_async_copy(src(i, me), outs[i].at[me], loc_sems.at[i]) for i in range(n)]
        for cp in local:
            cp.start()
        sends, recvs = [], []
        for k in range(1, N_DEV):
            px = 1 - x if k & 4 else x
            py = 1 - y if k & 2 else y
            pc = 1 - c if k & 1 else c
            p = 4 * px + 2 * py + pc
            for i in range(n):
                sends.append(pltpu.make_async_remote_copy(
                    src_ref=src(i, p), dst_ref=outs[i].at[me], send_sem=send_sems.at[i, k],
                    recv_sem=recv_sems.at[i, k], device_id=(px, py, pc), device_id_type=pl.DeviceIdType.MESH))
                recvs.append(pltpu.make_async_remote_copy(
                    src_ref=src(i, p), dst_ref=outs[i].at[p], send_sem=send_sems.at[i, k],
                    recv_sem=recv_sems.at[i, k], device_id=(px, py, pc), device_id_type=pl.DeviceIdType.MESH))
        for cp in sends:
            cp.start()
        for cp in recvs:
            cp.wait_recv()
        for cp in sends:
            cp.wait_send()
        for cp in local:
            cp.wait()

    out_shape = []
    for a, g in zip(arrs, gather):
        shp = (N_DEV,) + tuple(a.shape) if g else tuple(a.shape)
        out_shape.append(_sds(shp, a.dtype))
    return pl.pallas_call(
        kern, name=name, out_shape=out_shape,
        in_specs=[pl.BlockSpec(memory_space=pl.ANY)] * n,
        out_specs=[pl.BlockSpec(memory_space=pl.ANY)] * n,
        scratch_shapes=[pltpu.SemaphoreType.DMA((n, N_DEV)), pltpu.SemaphoreType.DMA((n, N_DEV)),
                        pltpu.SemaphoreType.DMA((n,))],
        compiler_params=pltpu.CompilerParams(has_side_effects=True),
    )(*arrs)


def _modpart(cact8, w_ada, b_cols):
    n_l, d, cw = w_ada.shape

    def kern(c_ref, w_ref, b_ref, o_ref):
        o_ref[0] = jnp.dot(c_ref[...].astype(MXU), w_ref[0].astype(MXU), preferred_element_type=F32) + b_ref[0]

    return pl.pallas_call(
        kern, name="modpart", grid=(n_l,), out_shape=_sds((n_l, N_DEV, cw)),
        in_specs=[_full((N_DEV, d)), pl.BlockSpec((1, d, cw), lambda l: (l, 0, 0)),
                  pl.BlockSpec((1, 1, cw), lambda l: (l, 0, 0))],
        out_specs=pl.BlockSpec((1, N_DEV, cw), lambda l: (l, 0, 0)),
        compiler_params=_params("arbitrary"),
    )(cact8, w_ada, b_cols)


def _ada_grad(cact_cols, dmod_cols):
    n_l, _, cw = dmod_cols.shape
    d = cact_cols.shape[1]

    def kern(c_ref, dm_ref, o_ref):
        acc = c_ref[0] * dm_ref[0, 0:1, :]
        for s in range(1, N_DEV):
            acc = acc + c_ref[s] * dm_ref[0, s:s + 1, :]
        o_ref[0] = acc

    return pl.pallas_call(
        kern, name="ada_grad", grid=(n_l,), out_shape=_sds((n_l, d, cw)),
        in_specs=[_full((N_DEV, d, 1)), pl.BlockSpec((1, N_DEV, cw), lambda l: (l, 0, 0))],
        out_specs=pl.BlockSpec((1, d, cw), lambda l: (l, 0, 0)),
        compiler_params=_params("arbitrary"),
    )(cact_cols, dmod_cols)


def _k_in(x, g, mod3, w, tm=256):
    s_len, d = x.shape

    def kern(x_ref, g_ref, mod_ref, w_ref, z_ref, h_ref):
        xv = x_ref[...]
        r = lax.rsqrt(jnp.mean(xv * xv, axis=-1, keepdims=True) + EPS)
        xn = xv * r * g_ref[...]
        h = (xn * (1.0 + mod_ref[1:2, :]) + mod_ref[0:1, :]).astype(MXU)
        h_ref[...] = h
        z_ref[...] = jnp.dot(h, w_ref[...], preferred_element_type=F32)

    return pl.pallas_call(
        kern, name="k_in", grid=(s_len // tm,),
        out_shape=[_sds((s_len, Z_W)), _sds((s_len, d), MXU)],
        in_specs=[_rows(tm, d), _full((1, d)), _full((3, d)), _full((d, Z_W))],
        out_specs=[_rows(tm, Z_W), _rows(tm, d)],
        compiler_params=_params("arbitrary"),
    )(x, g, mod3, w)


def _k_cum(fft, bf):
    nh, s_len = fft.shape

    def kern(ff_ref, b_ref, cum_ref):
        r_i = lax.broadcasted_iota(jnp.int32, (LANES, LANES), 0)
        c_i = lax.broadcasted_iota(jnp.int32, (LANES, LANES), 1)
        upper = (r_i <= c_i).astype(F32)
        carry = jnp.zeros((nh, 1), F32)
        for cb in range(s_len // LANES):
            sl = slice(cb * LANES, (cb + 1) * LANES)
            xc = ff_ref[:, sl] + b_ref[...]
            lf = jnp.minimum(xc, 0.0) - jnp.log(1.0 + jnp.exp(-jnp.abs(xc)))
            cum_ref[:, sl] = jnp.dot(lf, upper, precision=lax.Precision.HIGHEST,
                                     preferred_element_type=F32) + carry
            carry = carry + jnp.sum(lf, axis=1, keepdims=True)

    return pl.pallas_call(
        kern, name="k_cum", out_shape=_sds((nh, s_len)),
        in_specs=[pl.BlockSpec(memory_space=pltpu.VMEM)] * 2,
        out_specs=pl.BlockSpec(memory_space=pltpu.VMEM),
        compiler_params=_params(),
    )(fft, bf)


def _k_cum_bwd(dck, fft, bf):
    nh, s_len = fft.shape

    def kern(dc_ref, ff_ref, b_ref, dff_ref, db_ref):
        r_i = lax.broadcasted_iota(jnp.int32, (LANES, LANES), 0)
        c_i = lax.broadcasted_iota(jnp.int32, (LANES, LANES), 1)
        lower = (r_i >= c_i).astype(F32)
        carry = jnp.zeros((nh, 1), F32)
        db = jnp.zeros((nh, 1), F32)
        for cb in range(s_len // LANES - 1, -1, -1):
            sl = slice(cb * LANES, (cb + 1) * LANES)
            dc = dc_ref[:, sl]
            dlf = jnp.dot(dc, lower, precision=lax.Precision.HIGHEST, preferred_element_type=F32) + carry
            carry = carry + jnp.sum(dc, axis=1, keepdims=True)
            dff = dlf * jax.nn.sigmoid(-(ff_ref[:, sl] + b_ref[...]))
            dff_ref[:, sl] = dff
            db = db + jnp.sum(dff, axis=1, keepdims=True)
        db_ref[...] = jnp.broadcast_to(db, (nh, LANES))

    return pl.pallas_call(
        kern, name="k_cum_bwd", out_shape=[_sds((nh, s_len)), _sds((nh, LANES))],
        in_specs=[pl.BlockSpec(memory_space=pltpu.VMEM)] * 3,
        out_specs=[pl.BlockSpec(memory_space=pltpu.VMEM)] * 2,
        compiler_params=_params(),
    )(dck, fft, bf)


def _swap_halves(t):
    lane = lax.broadcasted_iota(jnp.int32, t.shape, 1)
    return jnp.where(lane < HALF_ROPE, pltpu.roll(t, LANES - HALF_ROPE, 1), pltpu.roll(t, HALF_ROPE, 1))


def _k_prep(z, cos8, sin8, cosk, sink, gq, gkv, wuq, wukv, tm=512):
    s_len = z.shape[0]

    def kern(ql_ref, kvl_ref, misc_ref, cos8_ref, sin8_ref, cosk_ref, sink_ref, gq_ref, gkv_ref,
             wuq_ref, wukv_ref, q_ref, kv_ref, kr_ref, qn_ref, kvn_ref):
        ql = ql_ref[...]
        rq = lax.rsqrt(jnp.mean(ql * ql, axis=-1, keepdims=True) + EPS)
        qn = (ql * rq * gq_ref[...]).astype(MXU)
        qn_ref[...] = qn
        q = jnp.dot(qn, wuq_ref[...], preferred_element_type=F32)
        r1, r2 = q[:, GROUP_W:GROUP_W + LANES], q[:, GROUP_W + LANES:]
        c8, s8 = cos8_ref[...], sin8_ref[...]
        qc = MLA_SCALE * LOG2E
        q_ref[:, :GROUP_W] = (q[:, :GROUP_W] * qc).astype(MXU)
        q_ref[:, GROUP_W:GROUP_W + LANES] = ((r1 * c8 - r2 * s8) * qc).astype(MXU)
        q_ref[:, GROUP_W + LANES:] = ((r2 * c8 + r1 * s8) * qc).astype(MXU)
        kvl = kvl_ref[...]
        rk = lax.rsqrt(jnp.mean(kvl * kvl, axis=-1, keepdims=True) + EPS)
        kvn = (kvl * rk * gkv_ref[...]).astype(MXU)
        kvn_ref[...] = kvn
        kv_ref[...] = jnp.dot(kvn, wukv_ref[...], preferred_element_type=F32).astype(MXU)
        misc = misc_ref[...]
        kr_ref[...] = (misc * cosk_ref[...] + _swap_halves(misc) * sink_ref[...]).astype(MXU)

    qw = GROUP_W + 2 * LANES
    return pl.pallas_call(
        kern, name="k_prep", grid=(s_len // tm,),
        out_shape=[_sds((s_len, qw), MXU), _sds((s_len, 2 * GROUP_W), MXU), _sds((s_len, LANES), MXU),
                   _sds((s_len, Q_LORA), MXU), _sds((s_len, KV_LORA), MXU)],
        in_specs=[_rows(tm, Q_LORA, Z_QL // Q_LORA), _rows(tm, KV_LORA, Z_KV // KV_LORA),
                  _rows(tm, LANES, Z_MISC // LANES), _rows(tm, LANES), _rows(tm, LANES), _rows(tm, LANES),
                  _rows(tm, LANES), _full((1, Q_LORA)), _full((1, KV_LORA)), _full((Q_LORA, qw)),
                  _full((KV_LORA, 2 * GROUP_W))],
        out_specs=[_rows(tm, qw), _rows(tm, 2 * GROUP_W), _rows(tm, LANES), _rows(tm, Q_LORA),
                   _rows(tm, KV_LORA)],
        compiler_params=_params("arbitrary"),
    )(z, z, z, cos8, sin8, cosk, sink, gq, gkv, wuq, wukv)


def _tile_mask(t, chunk_mask, transposed):
    row = lax.broadcasted_iota(jnp.int32, (t, t), 0)
    col = lax.broadcasted_iota(jnp.int32, (t, t), 1)
    qi, ki = (col, row) if transposed else (row, col)
    if chunk_mask:
        return (ki // CHUNK) <= (qi // CHUNK)
    return ki <= qi


def _attn_fwd2(qa, ka, vat, chunk_mask, name):
    nh, s_len, _ = qa.shape
    t = min(ATTN_TILE, s_len // 2)
    nq = s_len // t

    def kern(q_ref, k_ref, vt_ref, o_ref, lse_ref, m_scr, acc_scr):
        def qbody(qi, _):
            qs = pl.multiple_of(qi * t, t)
            qt = q_ref[0, pl.ds(qs, t), :]
            m_scr[...] = jnp.full(m_scr.shape, -jnp.inf, F32)
            acc_scr[...] = jnp.zeros(acc_scr.shape, F32)

            def step(ki, masked):
                ks = pl.multiple_of(ki * t, t)
                st = lax.dot_general(k_ref[0, pl.ds(ks, t), :], qt, _NT, preferred_element_type=F32)
                if masked:
                    st = jnp.where(_tile_mask(t, chunk_mask, True), st, -jnp.inf)
                m_old = m_scr[...]
                m_new = jnp.maximum(m_old, jnp.max(st, axis=0, keepdims=True))
                pt = jnp.exp2(st - m_new)
                alpha = jnp.exp2(m_old - m_new)
                acc_scr[...] = alpha * acc_scr[...] + jnp.dot(vt_ref[0, :, pl.ds(ks, t)], pt.astype(MXU),
                                                              preferred_element_type=F32)
                m_scr[...] = m_new

            def loop_body(ki, carry):
                step(ki, False)
                return carry

            lax.fori_loop(0, qi, loop_body, 0)
            step(qi, True)
            acc = acc_scr[...]
            row = lax.broadcasted_iota(jnp.int32, acc.shape, 0)
            l = jnp.sum(jnp.where(row == L_ROW, acc, 0.0), axis=0, keepdims=True)
            o_ref[0, :, pl.ds(qs, t)] = acc / l
            lse_ref[0, :, pl.ds(qs, t)] = m_scr[...] + jnp.log2(l)
            return 0

        lax.fori_loop(0, nq, qbody, 0)

    def head(r, w):
        return pl.BlockSpec((1, r, w), lambda h: (h, 0, 0))

    return pl.pallas_call(
        kern, name=name, grid=(nh,),
        out_shape=[_sds((nh, LANES, s_len)), _sds((nh, 1, s_len))],
        in_specs=[head(s_len, LANES), head(s_len, LANES), head(LANES, s_len)],
        out_specs=[head(LANES, s_len), head(1, s_len)],
        scratch_shapes=[pltpu.VMEM((1, t), F32), pltpu.VMEM((LANES, t), F32)],
        compiler_params=_params("arbitrary"),
    )(qa, ka, vat)


def _attn_bwd2(qa, ka, va, doa, chunk_mask, name):
    nh, s_len, _ = qa.shape
    t = min(ATTN_TILE, s_len // 2)
    nq = s_len // t
    qat, doat = qa.transpose(0, 2, 1), doa.transpose(0, 2, 1)

    def kern(q_ref, k_ref, v_ref, do_ref, qt_ref, dot_ref, dq_ref, dkt_ref, dvt_ref, dk_scr, dv_scr):
        dq_ref[...] = jnp.zeros(dq_ref.shape, F32)

        def kbody(ki, _):
            ks = pl.multiple_of(ki * t, t)
            kt = k_ref[0, pl.ds(ks, t), :]
            vt = v_ref[0, pl.ds(ks, t), :]
            dk_scr[...] = jnp.zeros(dk_scr.shape, F32)
            dv_scr[...] = jnp.zeros(dv_scr.shape, F32)

            def step(qi, masked):
                qs = pl.multiple_of(qi * t, t)
                s = lax.dot_general(q_ref[0, pl.ds(qs, t), :], kt, _NT, preferred_element_type=F32)
                p = jnp.exp2(s)
                if masked:
                    p = jnp.where(_tile_mask(t, chunk_mask, False), p, 0.0)
                dpd = lax.dot_general(do_ref[0, pl.ds(qs, t), :], vt, _NT, preferred_element_type=F32)
                dsb = (p * dpd).astype(MXU)
                dv_scr[...] += jnp.dot(dot_ref[0, :, pl.ds(qs, t)], p.astype(MXU), preferred_element_type=F32)
                dk_scr[...] += jnp.dot(qt_ref[0, :, pl.ds(qs, t)], dsb, preferred_element_type=F32)
                dq_ref[0, pl.ds(qs, t), :] += jnp.dot(dsb, kt, preferred_element_type=F32)

            step(ki, True)

            def loop_body(qi, carry):
                step(qi, False)
                return carry

            lax.fori_loop(ki + 1, nq, loop_body, 0)
            dkt_ref[0, :, pl.ds(ks, t)] = dk_scr[...]
            dvt_ref[0, :, pl.ds(ks, t)] = dv_scr[...]
            return 0

        lax.fori_loop(0, nq, kbody, 0)

    def head(r, w):
        return pl.BlockSpec((1, r, w), lambda h: (h, 0, 0))

    return pl.pallas_call(
        kern, name=name, grid=(nh,),
        out_shape=[_sds((nh, s_len, LANES)), _sds((nh, LANES, s_len)), _sds((nh, LANES, s_len))],
        in_specs=[head(s_len, LANES)] * 4 + [head(LANES, s_len)] * 2,
        out_specs=[head(s_len, LANES), head(LANES, s_len), head(LANES, s_len)],
        scratch_shapes=[pltpu.VMEM((LANES, t), F32), pltpu.VMEM((LANES, t), F32)],
        compiler_params=_params("arbitrary"),
    )(qa, ka, va, doa, qat, doat)


def _silu(a):
    return a * jax.nn.sigmoid(a)


def _k_out(of, om, z, x, gate, wout, tm=256):
    s_len, d = x.shape

    def kern(of_ref, om_ref, fg_ref, mg_ref, x_ref, gate_ref, w_ref, xo_ref, y_ref, u_ref):
        u_ref[:, :GROUP_W] = (of_ref[...] * _silu(fg_ref[...])).astype(MXU)
        u_ref[:, GROUP_W:] = (om_ref[...] * _silu(mg_ref[...])).astype(MXU)
        y = jnp.dot(u_ref[...], w_ref[...], preferred_element_type=F32)
        y_ref[...] = y
        xo_ref[...] = x_ref[...] + gate_ref[...] * y

    return pl.pallas_call(
        kern, name="k_out", grid=(s_len // tm,),
        out_shape=[_sds((s_len, d)), _sds((s_len, d)), _sds((s_len, 2 * GROUP_W), MXU)],
        in_specs=[_rows(tm, GROUP_W), _rows(tm, GROUP_W), _rows(tm, GROUP_W, Z_FG // GROUP_W),
                  _rows(tm, GROUP_W, Z_MG // GROUP_W), _rows(tm, d), _full((1, d)), _full((2 * GROUP_W, d))],
        out_specs=[_rows(tm, d), _rows(tm, d), _rows(tm, 2 * GROUP_W)],
        compiler_params=_params("arbitrary"),
    )(of, om, z, z, x, gate, wout)


def _k_loss(x, gf, tgt, tm=256):
    s_len, d = x.shape

    def kern(x_ref, g_ref, t_ref, loss_ref, dx_ref, dg_ref):
        i = pl.program_id(0)
        xv = x_ref[...]
        r = lax.rsqrt(jnp.mean(xv * xv, axis=-1, keepdims=True) + EPS)
        xh = xv * r
        diff = xh * g_ref[...] - t_ref[...]
        part = 0.5 * jnp.sum(jnp.mean(diff * diff, axis=-1, keepdims=True))
        dout = diff * (1.0 / d)
        dxh = dout * g_ref[...]
        dx_ref[...] = r * (dxh - xh * jnp.mean(dxh * xh, axis=-1, keepdims=True))

        @pl.when(i == 0)
        def _():
            loss_ref[...] = jnp.zeros_like(loss_ref)
            dg_ref[...] = jnp.zeros_like(dg_ref)

        loss_ref[...] += jnp.full(loss_ref.shape, part, F32)
        dg_ref[...] += jnp.sum(dout * xh, axis=0, keepdims=True)

    return pl.pallas_call(
        kern, name="k_loss", grid=(s_len // tm,),
        out_shape=[_sds((1, LANES)), _sds((s_len, d)), _sds((1, d))],
        in_specs=[_rows(tm, d), _full((1, d)), _rows(tm, d)],
        out_specs=[_full((1, LANES)), _rows(tm, d), _full((1, d))],
        compiler_params=_params("arbitrary"),
    )(x, gf, tgt)


def _kb_out(dxo, y, gate, wout_t, of, om, z, tm=256):
    s_len, d = dxo.shape

    def kern(dxo_ref, y_ref, gate_ref, wt_ref, of_ref, om_ref, fg_ref, mg_ref,
             dy_ref, dof_ref, dom_ref, dfg_ref, dmg_ref, dlf_ref, dlm_ref, dgate_ref):
        i = pl.program_id(0)
        dxv = dxo_ref[...]

        @pl.when(i == 0)
        def _():
            dgate_ref[...] = jnp.zeros_like(dgate_ref)

        dgate_ref[...] += jnp.sum(dxv * y_ref[...], axis=0, keepdims=True)
        dyb = (dxv * gate_ref[...]).astype(MXU)
        dy_ref[...] = dyb
        du = jnp.dot(dyb, wt_ref[...], preferred_element_type=F32)
        head_of = (lax.broadcasted_iota(jnp.int32, (GROUP_W, LANES), 0) // HEAD_DIM
                   == lax.broadcasted_iota(jnp.int32, (GROUP_W, LANES), 1)).astype(F32)
        for du_g, o_ref, g_ref, do_ref, dg_ref, dl_ref in (
                (du[:, :GROUP_W], of_ref, fg_ref, dof_ref, dfg_ref, dlf_ref),
                (du[:, GROUP_W:], om_ref, mg_ref, dom_ref, dmg_ref, dlm_ref)):
            a = g_ref[...]
            sg = jax.nn.sigmoid(a)
            ov = o_ref[...]
            dov = du_g * (a * sg)
            do_ref[...] = dov.astype(MXU)
            dg_ref[...] = (du_g * ov * (sg * (1.0 + a * (1.0 - sg)))).astype(MXU)
            dl_ref[...] = jnp.dot(dov * ov, head_of, precision=lax.Precision.HIGHEST, preferred_element_type=F32)

    return pl.pallas_call(
        kern, name="kb_out", grid=(s_len // tm,),
        out_shape=[_sds((s_len, d), MXU), _sds((s_len, GROUP_W), MXU), _sds((s_len, GROUP_W), MXU),
                   _sds((s_len, GROUP_W), MXU), _sds((s_len, GROUP_W), MXU), _sds((s_len, LANES)),
                   _sds((s_len, LANES)), _sds((1, d))],
        in_specs=[_rows(tm, d), _rows(tm, d), _full((1, d)), _full((d, 2 * GROUP_W)), _rows(tm, GROUP_W),
                  _rows(tm, GROUP_W), _rows(tm, GROUP_W, Z_FG // GROUP_W), _rows(tm, GROUP_W, Z_MG // GROUP_W)],
        out_specs=[_rows(tm, d), _rows(tm, GROUP_W), _rows(tm, GROUP_W), _rows(tm, GROUP_W),
                   _rows(tm, GROUP_W), _rows(tm, LANES), _rows(tm, LANES), _full((1, d))],
        compiler_params=_params("arbitrary"),
    )(dxo, y, gate, wout_t, of, om, z, z)


def _kb_prep(dqr, dkv, dkr, dff, z, cos8, sin8, cosk, sink, gq, gkv, wuq_t, wukv_t, tm=512):
    s_len = z.shape[0]
    qw = GROUP_W + 2 * LANES
    tail = Q_LORA + KV_LORA + LANES

    def kern(dqr_ref, dkv_ref, dkr_ref, dff_ref, ql_ref, kvl_ref, cos8_ref, sin8_ref, cosk_ref, sink_ref,
             gq_ref, gkv_ref, wuqt_ref, wukvt_ref, dq_ref, dz_ref, dgq_ref, dgkv_ref):
        i = pl.program_id(0)

        @pl.when(i == 0)
        def _():
            dgq_ref[...] = jnp.zeros_like(dgq_ref)
            dgkv_ref[...] = jnp.zeros_like(dgkv_ref)

        c8, s8 = cos8_ref[...], sin8_ref[...]
        do1 = dqr_ref[:, GROUP_W:GROUP_W + LANES]
        do2 = dqr_ref[:, GROUP_W + LANES:]
        dq_ref[:, :GROUP_W] = dqr_ref[:, :GROUP_W].astype(MXU)
        dq_ref[:, GROUP_W:GROUP_W + LANES] = (do1 * c8 + do2 * s8).astype(MXU)
        dq_ref[:, GROUP_W + LANES:] = (do2 * c8 - do1 * s8).astype(MXU)
        dqn = jnp.dot(dq_ref[...], wuqt_ref[...], preferred_element_type=F32)
        ql = ql_ref[...]
        rq = lax.rsqrt(jnp.mean(ql * ql, axis=-1, keepdims=True) + EPS)
        qh = ql * rq
        dgq_ref[...] += jnp.sum(dqn * qh, axis=0, keepdims=True)
        dqh = dqn * gq_ref[...]
        dz_ref[:, :Q_LORA] = (rq * (dqh - qh * jnp.mean(dqh * qh, axis=-1, keepdims=True))).astype(MXU)

        dkvn = jnp.dot(dkv_ref[...], wukvt_ref[...], preferred_element_type=F32)
        kvl = kvl_ref[...]
        rk = lax.rsqrt(jnp.mean(kvl * kvl, axis=-1, keepdims=True) + EPS)
        kh = kvl * rk
        dgkv_ref[...] += jnp.sum(dkvn * kh, axis=0, keepdims=True)
        dkh = dkvn * gkv_ref[...]
        dz_ref[:, Q_LORA:Q_LORA + KV_LORA] = (
            rk * (dkh - kh * jnp.mean(dkh * kh, axis=-1, keepdims=True))).astype(MXU)

        dkr_v = dkr_ref[...]
        dz_ref[:, Q_LORA + KV_LORA:] = (dkr_v * cosk_ref[...] - _swap_halves(dkr_v) * sink_ref[...]
                                        + dff_ref[...]).astype(MXU)

    return pl.pallas_call(
        kern, name="kb_prep", grid=(s_len // tm,),
        out_shape=[_sds((s_len, qw), MXU), _sds((s_len, tail), MXU), _sds((1, Q_LORA)), _sds((1, KV_LORA))],
        in_specs=[_rows(tm, qw), _rows(tm, 2 * GROUP_W), _rows(tm, LANES), _rows(tm, LANES),
                  _rows(tm, Q_LORA, Z_QL // Q_LORA), _rows(tm, KV_LORA, Z_KV // KV_LORA),
                  _rows(tm, LANES), _rows(tm, LANES), _rows(tm, LANES), _rows(tm, LANES),
                  _full((1, Q_LORA)), _full((1, KV_LORA)), _full((qw, Q_LORA)), _full((2 * GROUP_W, KV_LORA))],
        out_specs=[_rows(tm, qw), _rows(tm, tail), _full((1, Q_LORA)), _full((1, KV_LORA))],
        compiler_params=_params("arbitrary"),
    )(dqr, dkv, dkr, dff, z, z, cos8, sin8, cosk, sink, gq, gkv, wuq_t, wukv_t)


def _kb_in(dz, win_t, x, g, mod3, dxo, tm=256):
    s_len, d = x.shape

    def kern(dz_ref, wt_ref, x_ref, g_ref, mod_ref, dxo_ref, dx_ref, acc_ref):
        i = pl.program_id(0)

        @pl.when(i == 0)
        def _():
            acc_ref[...] = jnp.zeros_like(acc_ref)

        dh = jnp.dot(dz_ref[...], wt_ref[...], preferred_element_type=F32)
        xv = x_ref[...]
        r = lax.rsqrt(jnp.mean(xv * xv, axis=-1, keepdims=True) + EPS)
        xh = xv * r
        xn = xh * g_ref[...]
        dxn = dh * (1.0 + mod_ref[1:2, :])
        acc_ref[0:1, :] += jnp.sum(dh, axis=0, keepdims=True)
        acc_ref[1:2, :] += jnp.sum(dh * xn, axis=0, keepdims=True)
        acc_ref[2:3, :] += jnp.sum(dxn * xh, axis=0, keepdims=True)
        dxh = dxn * g_ref[...]
        dx_ref[...] = dxo_ref[...] + r * (dxh - xh * jnp.mean(dxh * xh, axis=-1, keepdims=True))

    return pl.pallas_call(
        kern, name="kb_in", grid=(s_len // tm,),
        out_shape=[_sds((s_len, d)), _sds((3, d))],
        in_specs=[_rows(tm, Z_W), _full((Z_W, d)), _rows(tm, d), _full((1, d)), _full((3, d)), _rows(tm, d)],
        out_specs=[_rows(tm, d), _full((3, d))],
        compiler_params=_params("arbitrary"),
    )(dz, win_t, x, g, mod3, dxo)


def _matmul_acc(a, b, name, tk=512):
    m, kk = a.shape
    n = b.shape[1]
    tn = n if n <= 1536 else 1536
    tk = min(tk, kk)

    def kern(a_ref, b_ref, o_ref):
        @pl.when(pl.program_id(1) == 0)
        def _():
            o_ref[...] = jnp.zeros_like(o_ref)

        o_ref[...] += jnp.dot(a_ref[...], b_ref[...], preferred_element_type=F32)

    return pl.pallas_call(
        kern, name=name, grid=(n // tn, kk // tk), out_shape=_sds((m, n)),
        in_specs=[pl.BlockSpec((m, tk), lambda j, k: (0, k)), pl.BlockSpec((tk, tn), lambda j, k: (k, j))],
        out_specs=pl.BlockSpec((m, tn), lambda j, k: (0, j)),
        compiler_params=_params("arbitrary", "arbitrary"),
    )(a, b)


def _adamw(gslots, w, m, v, name):
    n, r, c = gslots.shape
    tm = r
    for cand in (256, 128, 64, 32, 16, 8):
        if r % cand == 0:
            tm = cand
            break

    def kern(g_ref, w_ref, m_ref, v_ref, go_ref, d_ref, mo_ref, vo_ref):
        g = g_ref[0]
        for s in range(1, n):
            g = g + g_ref[s]
        m_new = ADAM_B1 * m_ref[...] + (1.0 - ADAM_B1) * g
        v_new = ADAM_B2 * v_ref[...] + (1.0 - ADAM_B2) * (g * g)
        m_hat = m_new / (1.0 - ADAM_B1 ** ADAM_STEP)
        v_hat = v_new / (1.0 - ADAM_B2 ** ADAM_STEP)
        go_ref[...] = g
        mo_ref[...] = m_new
        vo_ref[...] = v_new
        d_ref[...] = -ADAM_LR * (m_hat / (jnp.sqrt(v_hat) + ADAM_EPS) + ADAM_WD * w_ref[...])

    row = pl.BlockSpec((tm, c), lambda i: (i, 0))
    return pl.pallas_call(
        kern, name=name, grid=(r // tm,), out_shape=[_sds((r, c))] * 4,
        in_specs=[pl.BlockSpec((n, tm, c), lambda i: (0, i, 0)), row, row, row],
        out_specs=[row] * 4,
        compiler_params=_params("arbitrary"),
    )(gslots, w, m, v)


def _perm_w_in(w):
    pad = jnp.zeros(w.shape[:-1] + (Z_W - Z_MISC - ROPE - HEADS,), w.dtype)
    return jnp.concatenate([w[..., 0:1536], w[..., 1544:2056], w[..., 2472:2984], w[..., 2056:2312],
                            w[..., 2312:2440], w[..., 2440:2472], w[..., 1536:1544], pad], axis=-1)


def _unperm_w_in(g):
    ff0 = Z_MISC + MISC_FF
    return jnp.concatenate([g[..., 0:1536], g[..., ff0:ff0 + HEADS], g[..., Z_FG:Z_FG + GROUP_W],
                            g[..., Z_QL:Z_QL + Q_LORA], g[..., Z_KV:Z_KV + KV_LORA],
                            g[..., Z_MISC:Z_MISC + ROPE], g[..., Z_MG:Z_MG + GROUP_W]], axis=-1)


def _perm_w_uq(w):
    wh = w.reshape(w.shape[:-1] + (HEADS, NOPE + ROPE))
    lead = w.shape[:-1]
    return jnp.concatenate([wh[..., :NOPE].reshape(lead + (GROUP_W,)),
                            wh[..., NOPE:NOPE + HALF_ROPE].reshape(lead + (LANES,)),
                            wh[..., NOPE + HALF_ROPE:].reshape(lead + (LANES,))], axis=-1)


def _unperm_w_uq(g):
    lead = g.shape[:-1]
    parts = [g[..., :GROUP_W].reshape(lead + (HEADS, NOPE)),
             g[..., GROUP_W:GROUP_W + LANES].reshape(lead + (HEADS, HALF_ROPE)),
             g[..., GROUP_W + LANES:].reshape(lead + (HEADS, HALF_ROPE))]
    return jnp.concatenate(parts, axis=-1).reshape(lead + (HEADS * (NOPE + ROPE),))


def _perm_w_ukv(w):
    lead = w.shape[:-1]
    wh = w.reshape(lead + (HEADS, 2 * HEAD_DIM))
    return jnp.concatenate([wh[..., :NOPE].reshape(lead + (GROUP_W,)),
                            wh[..., NOPE:].reshape(lead + (GROUP_W,))], axis=-1)


def _unperm_w_ukv(g):
    lead = g.shape[:-1]
    parts = [g[..., :GROUP_W].reshape(lead + (HEADS, NOPE)), g[..., GROUP_W:].reshape(lead + (HEADS, HEAD_DIM))]
    return jnp.concatenate(parts, axis=-1).reshape(lead + (2 * GROUP_W,))


def _split3(c):
    terms, rest = [], c
    for _ in range(AUG):
        t = lax.reduce_precision(rest, exponent_bits=8, mantissa_bits=7)
        terms.append(t.astype(MXU))
        rest = rest - t
    return terms


def _aug(parts):
    s_len = parts[0].shape[0]
    used = sum(p.shape[-1] for p in parts)
    pad = jnp.zeros((s_len, HEADS, LANES - used), MXU)
    return jnp.concatenate(list(parts) + [pad], axis=-1).transpose(1, 0, 2)


def _rope_tables(positions):
    inv_freq = 1.0 / (ROPE_THETA ** (jnp.arange(0, ROPE, 2, dtype=F32) / ROPE))
    ang = positions.astype(F32)[:, None] * inv_freq
    cos, sin = jnp.cos(ang), jnp.sin(ang)
    zeros = jnp.zeros((cos.shape[0], LANES - ROPE), F32)
    cos8, sin8 = jnp.tile(cos, (1, HEADS)), jnp.tile(sin, (1, HEADS))
    cosk = jnp.concatenate([cos, cos, zeros], axis=1)
    sink = jnp.concatenate([-sin, sin, zeros], axis=1)
    return cos8, sin8, cosk, sink


def _local_step(x, mod, positions, loss_target, norm_g, b_f, q_norm_g, kv_norm_g, final_g, w_in, w_uq, w_ukv, w_out):
    n_l = norm_g.shape[0]
    s_len, d = x.shape
    cos8, sin8, cosk, sink = _rope_tables(positions)
    ones3 = jnp.ones((s_len, HEADS, AUG), MXU)

    def heads3(a, width):
        return a.reshape(s_len, HEADS, width)

    def cols3(a):
        return [t.T[:, :, None] for t in _split3(a)]

    def tokens(at):
        return at.transpose(2, 0, 1).reshape(s_len, -1)

    saved = []
    for l in range(n_l):
        mod3 = mod[l].reshape(3, d)
        z, h = _k_in(x, norm_g[l][None], mod3, w_in[l])
        fft = z[:, Z_MISC + MISC_FF:Z_MISC + MISC_FF + HEADS].T
        bf = b_f[l][:, None]
        cum = _k_cum(fft, bf)
        c3 = cols3(cum * LOG2E)
        qa_f = [heads3((z[:, Z_FQ:Z_FQ + GROUP_W] * (FOX_SCALE * LOG2E)).astype(MXU), HEAD_DIM)] + c3 + [ones3]
        ka_f = _aug([heads3(z[:, Z_FK:Z_FK + GROUP_W].astype(MXU), HEAD_DIM), ones3] + [-t for t in c3] + [ones3])
        va_f = _aug([heads3(z[:, Z_FV:Z_FV + GROUP_W].astype(MXU), HEAD_DIM), ones3])
        ot_f, lse_f = _attn_fwd2(_aug(qa_f), ka_f, va_f.transpose(0, 2, 1), False, "fox_fwd")
        qr, kvp, kr, qn, kvn = _k_prep(z, cos8, sin8, cosk, sink, q_norm_g[l][None], kv_norm_g[l][None],
                                       w_uq[l], w_ukv[l])
        qa_m = [jnp.concatenate([heads3(qr[:, :GROUP_W], NOPE), heads3(qr[:, GROUP_W:GROUP_W + LANES], HALF_ROPE),
                                 heads3(qr[:, GROUP_W + LANES:], HALF_ROPE)], axis=-1)]
        ka_m = _aug([heads3(kvp[:, :GROUP_W], NOPE), jnp.broadcast_to(kr[:, None, :ROPE], (s_len, HEADS, ROPE)),
                     ones3])
        va_m = _aug([heads3(kvp[:, GROUP_W:], HEAD_DIM), ones3])
        ot_m, lse_m = _attn_fwd2(_aug(qa_m), ka_m, va_m.transpose(0, 2, 1), True, "mla_fwd")
        of, om = tokens(ot_f[:, :HEAD_DIM, :]), tokens(ot_m[:, :HEAD_DIM, :])
        x_new, y, u = _k_out(of, om, z, x, mod3[2:3], w_out[l])
        saved.append((x, z, h, fft, bf, qa_f, ka_f, va_f, lse_f, qa_m, ka_m, va_m, lse_m, of, om, qn, kvn, y, u, mod3))
        x = x_new

    loss_row, dx, dfinal = _k_loss(x, final_g[None], loss_target)

    grads = {k: [] for k in ("norm_g", "mod", "w_in", "b_f", "q_norm_g", "w_uq", "kv_norm_g", "w_ukv", "w_out")}
    for l in range(n_l - 1, -1, -1):
        (x_l, z, h, fft, bf, qa_f, ka_f, va_f, lse_f, qa_m, ka_m, va_m, lse_m, of, om, qn, kvn, y, u, mod3) = saved[l]
        wout_t = w_out[l].T
        dyb, dof, dom, dfg, dmg, dlt_f, dlt_m, dgate = _kb_out(dx, y, mod3[2:3], wout_t, of, om, z)
        grads["w_out"].append(_matmul_acc(u.T, dyb, "dw_out"))

        def neg_cols(a):
            return [-t[:, :, None] for t in _split3(a)]

        qa_b = _aug(qa_f + [-t for t in cols3(lse_f.reshape(HEADS, s_len))])
        doa = _aug([heads3(dof, HEAD_DIM)] + neg_cols(dlt_f[:, :HEADS]))
        dq_full, dkt, dvt = _attn_bwd2(qa_b, ka_f, va_f, doa, False, "fox_bwd")
        dfq = (dq_full[:, :, :HEAD_DIM] * FOX_SCALE).transpose(1, 0, 2).reshape(s_len, GROUP_W).astype(MXU)
        dfk = tokens(dkt[:, :HEAD_DIM, :] * (1.0 / LOG2E)).astype(MXU)
        dfv = tokens(dvt[:, :HEAD_DIM, :]).astype(MXU)
        dcum = dq_full[:, :, HEAD_DIM] - dkt[:, HEAD_DIM + AUG, :]
        dfft, dbf = _k_cum_bwd(dcum, fft, bf)
        grads["b_f"].append(dbf[:, 0])

        qa_b = _aug(qa_m + [-t for t in cols3(lse_m.reshape(HEADS, s_len))])
        doa = _aug([heads3(dom, HEAD_DIM)] + neg_cols(dlt_m[:, :HEADS]))
        dq_full, dkt, dvt = _attn_bwd2(qa_b, ka_m, va_m, doa, True, "mla_bwd")
        dmq = (dq_full[:, :, :NOPE + ROPE] * MLA_SCALE).transpose(1, 0, 2)
        dqr = jnp.concatenate([dmq[:, :, :NOPE].reshape(s_len, GROUP_W),
                               dmq[:, :, NOPE:NOPE + HALF_ROPE].reshape(s_len, LANES),
                               dmq[:, :, NOPE + HALF_ROPE:].reshape(s_len, LANES)], axis=1)
        dkvp = jnp.concatenate([tokens(dkt[:, :NOPE, :] * (1.0 / LOG2E)), tokens(dvt[:, :HEAD_DIM, :])],
                               axis=1).astype(MXU)
        dkr = jnp.pad((jnp.sum(dkt[:, NOPE:NOPE + ROPE, :], axis=0) * (1.0 / LOG2E)).T, ((0, 0), (0, LANES - ROPE)))
        dff = jnp.pad(dfft.T, ((0, 0), (MISC_FF, LANES - MISC_FF - HEADS)))
        dq_b, dz_tail, dgq, dgkv = _kb_prep(dqr, dkvp, dkr, dff, z, cos8, sin8, cosk, sink,
                                            q_norm_g[l][None], kv_norm_g[l][None], w_uq[l].T, w_ukv[l].T)
        grads["q_norm_g"].append(dgq[0])
        grads["kv_norm_g"].append(dgkv[0])
        grads["w_uq"].append(_matmul_acc(qn.T, dq_b, "dw_uq"))
        grads["w_ukv"].append(_matmul_acc(kvn.T, dkvp, "dw_ukv"))
        dz = jnp.concatenate([dfq, dfk, dfv, dfg, dmg, dz_tail], axis=1)
        grads["w_in"].append(_matmul_acc(h.T, dz, "dw_in"))
        dx, acc3 = _kb_in(dz, w_in[l].T, x_l, norm_g[l][None], mod3, dx)
        grads["norm_g"].append(acc3[2])
        grads["mod"].append(jnp.concatenate([acc3[0], acc3[1], dgate[0]]))
    grads = {k: jnp.stack(v[::-1]) for k, v in grads.items()}
    grads["final_g"] = dfinal[0]
    return loss_row[0, 0], dx, grads


def _pack_small(parts, total):
    flat = jnp.concatenate([p.reshape(-1) for p in parts])
    return jnp.pad(flat, (0, total - flat.shape[0])).reshape(total // LANES, LANES)


def kernel(x, c, positions, norm_g, w_ada, b_ada, w_in, b_f, q_norm_g, w_uq, kv_norm_g, w_ukv, w_out, final_g, loss_target, m_norm_g, m_w_ada, m_b_ada, m_w_in, m_b_f, m_q_norm_g, m_w_uq, m_kv_norm_g, m_w_ukv, m_w_out, m_final_g, v_norm_g, v_w_ada, v_b_ada, v_w_in, v_b_f, v_q_norm_g, v_w_uq, v_kv_norm_g, v_w_ukv, v_w_out, v_final_g):
    n_l, d = norm_g.shape
    s_len = x.shape[1]
    me = 4 * lax.axis_index("x") + 2 * lax.axis_index("y") + lax.axis_index("c")
    ada_c = w_ada.shape[2]
    in_c = w_in.shape[2]
    uq_c = w_uq.shape[2]
    ukv_c = w_ukv.shape[2]
    out_r = w_out.shape[1]

    cact = jnp.broadcast_to(jax.nn.silu(c), (N_DEV, d))
    g_in, g_uq, g_ukv, g_out, g_cact = _exchange(
        [w_in.reshape(n_l * d, in_c).astype(MXU), w_uq.reshape(n_l * Q_LORA, uq_c).astype(MXU),
         w_ukv.reshape(n_l * KV_LORA, ukv_c).astype(MXU), w_out.reshape(n_l * out_r, d).astype(MXU), cact],
        [True] * 5, "gather_weights")
    w_in_f = _perm_w_in(g_in.reshape(N_DEV, n_l, d, in_c).transpose(1, 2, 0, 3).reshape(n_l, d, N_DEV * in_c))
    w_uq_f = _perm_w_uq(g_uq.reshape(N_DEV, n_l, Q_LORA, uq_c).transpose(1, 2, 0, 3).reshape(n_l, Q_LORA, -1))
    w_ukv_f = _perm_w_ukv(g_ukv.reshape(N_DEV, n_l, KV_LORA, ukv_c).transpose(1, 2, 0, 3).reshape(n_l, KV_LORA, -1))
    w_out_f = g_out.reshape(N_DEV, n_l, out_r, d).transpose(1, 0, 2, 3).reshape(n_l, N_DEV * out_r, d)
    cact_all = g_cact[:, 0, :]

    b_cols = lax.dynamic_slice_in_dim(b_ada, me * ada_c, ada_c, axis=1)[:, None, :]
    modpart = _modpart(cact_all, w_ada, b_cols)
    mod_send = jnp.pad(modpart.transpose(1, 0, 2), ((0, 0), (0, 8 - n_l), (0, 0)))
    (mod_recv,) = _exchange([mod_send], [False], "scatter_mod")
    mod = mod_recv.transpose(1, 0, 2).reshape(8, N_DEV * ada_c)[:n_l]

    loss, dx, gr = _local_step(x[0], mod, positions[0], loss_target[0], norm_g, b_f, q_norm_g, kv_norm_g,
                               final_g, w_in_f, w_uq_f, w_ukv_f, w_out_f)

    s_in = _unperm_w_in(gr["w_in"]).reshape(n_l, d, N_DEV, in_c).transpose(2, 0, 1, 3).reshape(N_DEV, n_l * d, in_c)
    s_uq = _unperm_w_uq(gr["w_uq"]).reshape(n_l, Q_LORA, N_DEV, uq_c).transpose(2, 0, 1, 3).reshape(
        N_DEV, n_l * Q_LORA, uq_c)
    s_ukv = _unperm_w_ukv(gr["w_ukv"]).reshape(n_l, KV_LORA, N_DEV, ukv_c).transpose(2, 0, 1, 3).reshape(
        N_DEV, n_l * KV_LORA, ukv_c)
    s_out = gr["w_out"].reshape(n_l, N_DEV, out_r, d).transpose(1, 0, 2, 3).reshape(N_DEV, n_l * out_r, d)
    small_parts = [gr["norm_g"], gr["mod"], gr["b_f"], gr["q_norm_g"], gr["kv_norm_g"], gr["final_g"], cact[0]]
    sizes = [int(np.prod(p.shape)) for p in small_parts]
    total = -(-sum(sizes) // 1024) * 1024
    small = _pack_small(small_parts, total)
    r_in, r_uq, r_ukv, r_out, r_small = _exchange([s_in, s_uq, s_ukv, s_out, small],
                                                  [False, False, False, False, True], "exchange_grads")

    def upd(slots, w, m, v, name):
        shp = w.shape
        w2, m2, v2 = (a.reshape(slots.shape[1:]) for a in (w, m, v))
        return [o.reshape(shp) for o in _adamw(slots, w2, m2, v2, name)]

    o_in = upd(r_in, w_in, m_w_in, v_w_in, "adamw_w_in")
    o_uq = upd(r_uq, w_uq, m_w_uq, v_w_uq, "adamw_w_uq")
    o_ukv = upd(r_ukv, w_ukv, m_w_ukv, v_w_ukv, "adamw_w_ukv")
    o_out = upd(r_out, w_out, m_w_out, v_w_out, "adamw_w_out")

    offs = np.cumsum([0] + sizes)
    flat_all = r_small.reshape(N_DEV, total)
    dmod_all = flat_all[:, offs[1]:offs[2]].reshape(N_DEV, n_l, 3 * d)
    dmod_cols = lax.dynamic_slice_in_dim(dmod_all, me * ada_c, ada_c, axis=2).transpose(1, 0, 2)
    cact_cols = flat_all[:, offs[6]:offs[7]][:, :, None]
    g_ada = _ada_grad(cact_cols, dmod_cols)
    o_ada = upd(g_ada.reshape(1, n_l * d, ada_c), w_ada, m_w_ada, v_w_ada, "adamw_w_ada")

    zero_c = jnp.zeros((d,), F32)
    small_w = [_pack_small([norm_g, b_ada, b_f, q_norm_g, kv_norm_g, final_g, zero_c], total),
               _pack_small([m_norm_g, m_b_ada, m_b_f, m_q_norm_g, m_kv_norm_g, m_final_g, zero_c], total),
               _pack_small([v_norm_g, v_b_ada, v_b_f, v_q_norm_g, v_kv_norm_g, v_final_g, zero_c], total)]
    o_small = [o.reshape(-1) for o in _adamw(r_small, *small_w, "adamw_small")]
    shapes = [norm_g.shape, b_ada.shape, b_f.shape, q_norm_g.shape, kv_norm_g.shape, final_g.shape]

    def small_out(kind, idx):
        return o_small[kind][offs[idx]:offs[idx + 1]].reshape(shapes[idx])

    loss_all = lax.psum(loss, ("x", "y", "c"))
    outs = [loss_all, dx[None]]
    for kind in range(4):
        outs += [small_out(kind, 0), o_ada[kind], small_out(kind, 1), o_in[kind], small_out(kind, 2),
                 small_out(kind, 3), o_uq[kind], small_out(kind, 4), o_ukv[kind], o_out[kind], small_out(kind, 5)]
    return tuple(outs)
```

```python
import jax
import jax.numpy as jnp
import numpy as np
from jax import lax
from jax.experimental import pallas as pl
from jax.experimental.pallas import tpu as pltpu

F32 = jnp.float32
MXU = jnp.bfloat16

N_DEV = 8
HEADS = 8
PAIRS = HEADS // 2
HEAD_DIM = 64
NOPE = 64
ROPE = 32
HALF_ROPE = ROPE // 2
Q_LORA = 256
KV_LORA = 128
CHUNK = 64
GROUP_W = HEADS * HEAD_DIM
ROPE_W = HEADS * ROPE
EPS = 1e-6
ROPE_THETA = 10000.0
N_IN = 2984

Z_FQ, Z_FK, Z_FV, Z_FG, Z_MG, Z_QL, Z_KV, Z_MISC, Z_W = 0, 512, 1024, 1536, 2048, 2560, 2816, 2944, 3072
MISC_FF = ROPE

ADAM_LR = 0.001
ADAM_B1 = 0.9
ADAM_B2 = 0.999
ADAM_EPS = 1e-08
ADAM_WD = 0.01
ADAM_STEP = 10

VMEM_LIMIT_V7X = 56 * 1024 * 1024
LANES = 128
ATTN_TILE = 512
LOG2E = 1.4426950408889634
FOX_SCALE = HEAD_DIM ** -0.5
MLA_SCALE = (NOPE + ROPE) ** -0.5

_NT = (((1,), (1,)), ((), ()))


def _params(*sem):
    return pltpu.CompilerParams(dimension_semantics=sem, vmem_limit_bytes=VMEM_LIMIT_V7X)


def _sds(shape, dtype=F32):
    return jax.ShapeDtypeStruct(shape, dtype)


def _full(shape):
    nd = len(shape)
    return pl.BlockSpec(shape, lambda *_: (0,) * nd)


def _rows(tm, width, col=0):
    return pl.BlockSpec((tm, width), lambda i: (i, col))


def _exchange(arrs, gather, name):
    n = len(arrs)

    def kern(*refs):
        ins, outs = refs[:n], refs[n:2 * n]
        send_sems, recv_sems, loc_sems = refs[2 * n:]
        x, y, c = lax.axis_index("x"), lax.axis_index("y"), lax.axis_index("c")
        me = 4 * x + 2 * y + c

        def src(i, j):
            return ins[i] if gather[i] else ins[i].at[j]

        local = [pltpu.make_async_copy(src(i, me), outs[i].at[me], loc_sems.at[i]) for i in range(n)]
        for cp in local:
            cp.start()
        sends, recvs = [], []
        for k in range(1, N_DEV):
            px = 1 - x if k & 4 else x
            py = 1 - y if k & 2 else y
            pc = 1 - c if k & 1 else c
            p = 4 * px + 2 * py + pc
            for i in range(n):
                sends.append(pltpu.make_async_remote_copy(
                    src_ref=src(i, p), dst_ref=outs[i].at[me], send_sem=send_sems.at[i, k],
                    recv_sem=recv_sems.at[i, k], device_id=(px, py, pc), device_id_type=pl.DeviceIdType.MESH))
                recvs.append(pltpu.make_async_remote_copy(
                    src_ref=src(i, p), dst_ref=outs[i].at[p], send_sem=send_sems.at[i, k],
                    recv_sem=recv_sems.at[i, k], device_id=(px, py, pc), device_id_type=pl.DeviceIdType.MESH))
        for cp in sends:
            cp.start()
        for cp in recvs:
            cp.wait_recv()
        for cp in sends:
            cp.wait_send()
        for cp in local:
            cp.wait()

    out_shape = []
    for a, g in zip(arrs, gather):
        shp = (N_DEV,) + tuple(a.shape) if g else tuple(a.shape)
        out_shape.append(_sds(shp, a.dtype))
    return pl.pallas_call(
        kern, name=name, out_shape=out_shape,
        in_specs=[pl.BlockSpec(memory_space=pl.ANY)] * n,
        out_specs=[pl.BlockSpec(memory_space=pl.ANY)] * n,
        scratch_shapes=[pltpu.SemaphoreType.DMA((n, N_DEV)), pltpu.SemaphoreType.DMA((n, N_DEV)),
                        pltpu.SemaphoreType.DMA((n,))],
        compiler_params=pltpu.CompilerParams(has_side_effects=True),
    )(*arrs)


def _modpart(cact8, w_ada, b_cols):
    n_l, d, cw = w_ada.shape

    def kern(c_ref, w_ref, b_ref, o_ref):
        o_ref[0] = jnp.dot(c_ref[...].astype(MXU), w_ref[0].astype(MXU), preferred_element_type=F32) + b_ref[0]

    return pl.pallas_call(
        kern, name="modpart", grid=(n_l,), out_shape=_sds((n_l, N_DEV, cw)),
        in_specs=[_full((N_DEV, d)), pl.BlockSpec((1, d, cw), lambda l: (l, 0, 0)),
                  pl.BlockSpec((1, 1, cw), lambda l: (l, 0, 0))],
        out_specs=pl.BlockSpec((1, N_DEV, cw), lambda l: (l, 0, 0)),
        compiler_params=_params("arbitrary"),
    )(cact8, w_ada, b_cols)


def _ada_grad(cact_cols, dmod_cols):
    n_l, _, cw = dmod_cols.shape
    d = cact_cols.shape[1]

    def kern(c_ref, dm_ref, o_ref):
        acc = c_ref[0] * dm_ref[0, 0:1, :]
        for s in range(1, N_DEV):
            acc = acc + c_ref[s] * dm_ref[0, s:s + 1, :]
        o_ref[0] = acc

    return pl.pallas_call(
        kern, name="ada_grad", grid=(n_l,), out_shape=_sds((n_l, d, cw)),
        in_specs=[_full((N_DEV, d, 1)), pl.BlockSpec((1, N_DEV, cw), lambda l: (l, 0, 0))],
        out_specs=pl.BlockSpec((1, d, cw), lambda l: (l, 0, 0)),
        compiler_params=_params("arbitrary"),
    )(cact_cols, dmod_cols)


def _k_in(x, g, mod3, w, tm=256):
    s_len, d = x.shape
    qkv_w = 3 * GROUP_W

    def kern(x_ref, g_ref, mod_ref, w_ref, z_ref, h_ref, qkv_ref):
        xv = x_ref[...]
        r = lax.rsqrt(jnp.mean(xv * xv, axis=-1, keepdims=True) + EPS)
        xn = xv * r * g_ref[...]
        h = (xn * (1.0 + mod_ref[1:2, :]) + mod_ref[0:1, :]).astype(MXU)
        h_ref[...] = h
        z = jnp.dot(h, w_ref[...], preferred_element_type=F32)
        z_ref[...] = z
        qkv_ref[:, :GROUP_W] = (z[:, Z_FQ:Z_FQ + GROUP_W] * (FOX_SCALE * LOG2E)).astype(MXU)
        qkv_ref[:, GROUP_W:] = z[:, Z_FK:Z_FK + 2 * GROUP_W].astype(MXU)

    return pl.pallas_call(
        kern, name="k_in", grid=(s_len // tm,),
        out_shape=[_sds((s_len, Z_W)), _sds((s_len, d), MXU), _sds((s_len, qkv_w), MXU)],
        in_specs=[_rows(tm, d), _full((1, d)), _full((3, d)), _full((d, Z_W))],
        out_specs=[_rows(tm, Z_W), _rows(tm, d), _rows(tm, qkv_w)],
        compiler_params=_params("arbitrary"),
    )(x, g, mod3, w)


def _k_cum(fft, bf):
    nh, s_len = fft.shape

    def kern(ff_ref, b_ref, cum_ref):
        r_i = lax.broadcasted_iota(jnp.int32, (LANES, LANES), 0)
        c_i = lax.broadcasted_iota(jnp.int32, (LANES, LANES), 1)
        upper = (r_i <= c_i).astype(F32)
        carry = jnp.zeros((nh, 1), F32)
        for cb in range(s_len // LANES):
            sl = slice(cb * LANES, (cb + 1) * LANES)
            xc = ff_ref[:, sl] + b_ref[...]
            lf = jnp.minimum(xc, 0.0) - jnp.log(1.0 + jnp.exp(-jnp.abs(xc)))
            cum_ref[:, sl] = jnp.dot(lf, upper, precision=lax.Precision.HIGHEST,
                                     preferred_element_type=F32) + carry
            carry = carry + jnp.sum(lf, axis=1, keepdims=True)

    return pl.pallas_call(
        kern, name="k_cum", out_shape=_sds((nh, s_len)),
        in_specs=[pl.BlockSpec(memory_space=pltpu.VMEM)] * 2,
        out_specs=pl.BlockSpec(memory_space=pltpu.VMEM),
        compiler_params=_params(),
    )(fft, bf)


def _k_cum_bwd(dck, fft, bf):
    nh, s_len = fft.shape

    def kern(dc_ref, ff_ref, b_ref, dff_ref, db_ref):
        r_i = lax.broadcasted_iota(jnp.int32, (LANES, LANES), 0)
        c_i = lax.broadcasted_iota(jnp.int32, (LANES, LANES), 1)
        lower = (r_i >= c_i).astype(F32)
        carry = jnp.zeros((nh, 1), F32)
        db = jnp.zeros((nh, 1), F32)
        for cb in range(s_len // LANES - 1, -1, -1):
            sl = slice(cb * LANES, (cb + 1) * LANES)
            dc = dc_ref[:, sl]
            dlf = jnp.dot(dc, lower, precision=lax.Precision.HIGHEST, preferred_element_type=F32) + carry
            carry = carry + jnp.sum(dc, axis=1, keepdims=True)
            dff = dlf * jax.nn.sigmoid(-(ff_ref[:, sl] + b_ref[...]))
            dff_ref[:, sl] = dff
            db = db + jnp.sum(dff, axis=1, keepdims=True)
        db_ref[...] = jnp.broadcast_to(db, (nh, LANES))

    return pl.pallas_call(
        kern, name="k_cum_bwd", out_shape=[_sds((nh, s_len)), _sds((nh, LANES))],
        in_specs=[pl.BlockSpec(memory_space=pltpu.VMEM)] * 3,
        out_specs=[pl.BlockSpec(memory_space=pltpu.VMEM)] * 2,
        compiler_params=_params(),
    )(dck, fft, bf)


def _swap16(t):
    lane = lax.broadcasted_iota(jnp.int32, t.shape, 1)
    return jnp.where(lane % ROPE < HALF_ROPE, pltpu.roll(t, LANES - HALF_ROPE, 1), pltpu.roll(t, HALF_ROPE, 1))


def _rope(t, cos, sin):
    return t * cos + _swap16(t) * sin


def _rope_bwd(dt, cos, sin):
    return dt * cos - _swap16(dt) * sin


def _k_prep(z, cos, sin, gq, gkv, wuq, wukv, tm=512):
    s_len = z.shape[0]
    qc = MLA_SCALE * LOG2E

    def kern(ql_ref, kvl_ref, misc_ref, cos_ref, sin_ref, gq_ref, gkv_ref, wuq_ref, wukv_ref,
             qn_out, qr_out, kn_out, v_out, kr_out, qn_ref, kvn_ref):
        cs, sn = cos_ref[...], sin_ref[...]
        ql = ql_ref[...]
        rq = lax.rsqrt(jnp.mean(ql * ql, axis=-1, keepdims=True) + EPS)
        qn = (ql * rq * gq_ref[...]).astype(MXU)
        qn_ref[...] = qn
        q = jnp.dot(qn, wuq_ref[...], preferred_element_type=F32)
        qn_out[...] = (q[:, :GROUP_W] * qc).astype(MXU)
        for half in range(ROPE_W // LANES):
            lo = GROUP_W + half * LANES
            qr_out[:, half * LANES:(half + 1) * LANES] = (_rope(q[:, lo:lo + LANES], cs, sn) * qc).astype(MXU)
        kvl = kvl_ref[...]
        rk = lax.rsqrt(jnp.mean(kvl * kvl, axis=-1, keepdims=True) + EPS)
        kvn = (kvl * rk * gkv_ref[...]).astype(MXU)
        kvn_ref[...] = kvn
        kv = jnp.dot(kvn, wukv_ref[...], preferred_element_type=F32)
        kn_out[...] = kv[:, :GROUP_W].astype(MXU)
        v_out[...] = kv[:, GROUP_W:].astype(MXU)
        misc = misc_ref[...]
        lane = lax.broadcasted_iota(jnp.int32, misc.shape, 1)
        kr = jnp.where(lane < ROPE, _rope(misc, cs, sn), 0.0)
        kr4 = kr
        for rep in range(1, LANES // ROPE):
            kr4 = kr4 + pltpu.roll(kr, rep * ROPE, 1)
        kr_out[...] = kr4.astype(MXU)

    return pl.pallas_call(
        kern, name="k_prep", grid=(s_len // tm,),
        out_shape=[_sds((s_len, GROUP_W), MXU), _sds((s_len, ROPE_W), MXU), _sds((s_len, GROUP_W), MXU),
                   _sds((s_len, GROUP_W), MXU), _sds((s_len, LANES), MXU), _sds((s_len, Q_LORA), MXU),
                   _sds((s_len, KV_LORA), MXU)],
        in_specs=[_rows(tm, Q_LORA, Z_QL // Q_LORA), _rows(tm, KV_LORA, Z_KV // KV_LORA),
                  _rows(tm, LANES, Z_MISC // LANES), _rows(tm, LANES), _rows(tm, LANES),
                  _full((1, Q_LORA)), _full((1, KV_LORA)), _full((Q_LORA, GROUP_W + ROPE_W)),
                  _full((KV_LORA, 2 * GROUP_W))],
        out_specs=[_rows(tm, GROUP_W), _rows(tm, ROPE_W), _rows(tm, GROUP_W), _rows(tm, GROUP_W), _rows(tm, LANES),
                   _rows(tm, Q_LORA), _rows(tm, KV_LORA)],
        compiler_params=_params("arbitrary"),
    )(z, z, z, cos, sin, gq, gkv, wuq, wukv)


def _tile_mask(t, chunk_mask, transposed):
    row = lax.broadcasted_iota(jnp.int32, (t, t), 0)
    col = lax.broadcasted_iota(jnp.int32, (t, t), 1)
    qi, ki = (col, row) if transposed else (row, col)
    if chunk_mask:
        return (ki // CHUNK) <= (qi // CHUNK)
    return ki <= qi


def _keep_head(x, hh, axis, rope_group):
    idx = lax.broadcasted_iota(jnp.int32, x.shape, axis)
    keep = (idx >= hh * HEAD_DIM) & (idx < (hh + 1) * HEAD_DIM)
    if x.shape[axis] != LANES:
        keep = keep | ((idx >= LANES + rope_group * ROPE) & (idx < LANES + (rope_group + 1) * ROPE))
    return jnp.where(keep, x, jnp.zeros_like(x))


def _attn_fwd(q, q_blk, k, k_blk, vt, bias, rope, chunk_mask, name):
    s_len = q.shape[0]
    t = min(ATTN_TILE, s_len // 2)
    nq = s_len // t

    def kern(*refs):
        q_ref, k_ref, vt_ref = refs[:3]
        pos = 3
        if bias is not None:
            cq_ref, ck_ref = refs[pos:pos + 2]
            pos += 2
        if rope is not None:
            qr_ref, kr_ref = refs[pos:pos + 2]
            pos += 2
        o_ref, lse_ref, m_scr, l_scr, acc_scr = refs[pos:]
        pj = pl.program_id(0)

        def qbody(qi, _):
            qs = pl.multiple_of(qi * t, t)
            qt = q_ref[pl.ds(qs, t), :]
            if rope is not None:
                qt = jnp.concatenate([qt, qr_ref[pl.ds(qs, t), :]], axis=1)
            qh = [_keep_head(qt, hh, 1, (pj % 2) * 2 + hh) for hh in range(2)]
            m_scr[...] = jnp.full(m_scr.shape, -jnp.inf, F32)
            l_scr[...] = jnp.zeros(l_scr.shape, F32)
            acc_scr[...] = jnp.zeros(acc_scr.shape, F32)

            def step(ki, masked):
                ks = pl.multiple_of(ki * t, t)
                kt = k_ref[pl.ds(ks, t), :]
                if rope is not None:
                    kt = jnp.concatenate([kt, kr_ref[pl.ds(ks, t), :]], axis=1)
                vtt = vt_ref[:, pl.ds(ks, t)]
                for hh in range(2):
                    st = lax.dot_general(kt, qh[hh], _NT, preferred_element_type=F32)
                    if bias is not None:
                        st = st + cq_ref[hh, :, pl.ds(qs, t)] - ck_ref[hh, pl.ds(ks, t), :]
                    if masked:
                        st = jnp.where(_tile_mask(t, chunk_mask, True), st, -jnp.inf)
                    m_old = m_scr[hh]
                    m_new = jnp.maximum(m_old, jnp.max(st, axis=0, keepdims=True))
                    pt = jnp.exp2(st - m_new)
                    alpha = jnp.exp2(m_old - m_new)
                    l_scr[hh] = alpha * l_scr[hh] + jnp.sum(pt, axis=0, keepdims=True)
                    acc_scr[hh] = alpha * acc_scr[hh] + jnp.dot(vtt, pt.astype(MXU), preferred_element_type=F32)
                    m_scr[hh] = m_new

            def loop_body(ki, carry):
                step(ki, False)
                return carry

            lax.fori_loop(0, qi, loop_body, 0)
            step(qi, True)
            o0 = acc_scr[0] / l_scr[0]
            o1 = acc_scr[1] / l_scr[1]
            row = lax.broadcasted_iota(jnp.int32, o0.shape, 0)
            o_ref[:, pl.ds(qs, t)] = jnp.where(row < HEAD_DIM, o0, o1)
            for hh in range(2):
                lse_ref[hh, :, pl.ds(qs, t)] = m_scr[hh] + jnp.log2(l_scr[hh])
            return 0

        lax.fori_loop(0, nq, qbody, 0)

    trn = pl.BlockSpec((LANES, s_len), lambda j: (j, 0))
    rowb = pl.BlockSpec((2, 1, s_len), lambda j: (j, 0, 0))
    ins = [q, k, vt]
    in_specs = [pl.BlockSpec((s_len, LANES), lambda j: (0, q_blk + j)),
                pl.BlockSpec((s_len, LANES), lambda j: (0, k_blk + j)), trn]
    if bias is not None:
        ins += list(bias)
        in_specs += [rowb, pl.BlockSpec((2, s_len, 1), lambda j: (j, 0, 0))]
    if rope is not None:
        ins += list(rope)
        in_specs += [pl.BlockSpec((s_len, LANES), lambda j: (0, j // 2)), _full((s_len, LANES))]
    return pl.pallas_call(
        kern, name=name, grid=(PAIRS,),
        out_shape=[_sds((PAIRS * LANES, s_len)), _sds((HEADS, 1, s_len))],
        in_specs=in_specs, out_specs=[trn, rowb],
        scratch_shapes=[pltpu.VMEM((2, 1, t), F32), pltpu.VMEM((2, 1, t), F32), pltpu.VMEM((2, LANES, t), F32)],
        compiler_params=_params("arbitrary"),
    )(*ins)


def _attn_bwd(q, q_blk, k, k_blk, v, v_blk, do, qt_, dot_, pack, ck_row, rope, chunk_mask, q_scale, k_scale, name):
    s_len = q.shape[0]
    t = min(ATTN_TILE, s_len // 2)
    nq = s_len // t
    has_bias = ck_row is not None
    kw = 2 * LANES if rope is not None else LANES

    def kern(*refs):
        q_ref, k_ref, v_ref, do_ref, qt_ref, dot_ref, pack_ref = refs[:7]
        pos = 7
        if has_bias:
            ck_ref = refs[pos]
            pos += 1
        if rope is not None:
            qr_ref, kr_ref, qrt_ref = refs[pos:pos + 3]
            pos += 3
        dq_ref, dkt_ref, dvt_ref = refs[pos:pos + 3]
        pos += 3
        if has_bias:
            dcq_ref, dck_ref = refs[pos:pos + 2]
            pos += 2
        if rope is not None:
            dqr_ref, dkrt_ref = refs[pos:pos + 2]
            pos += 2
        dq_scr, dcq_scr = refs[pos:]
        pj = pl.program_id(0)

        dkt_ref[...] = jnp.zeros(dkt_ref.shape, F32)
        dvt_ref[...] = jnp.zeros(dvt_ref.shape, F32)
        if has_bias:
            dck_ref[...] = jnp.zeros(dck_ref.shape, F32)

            @pl.when(pj == 0)
            def _():
                dcq_ref[...] = jnp.zeros(dcq_ref.shape, F32)
        if rope is not None:
            dkrt_ref[...] = jnp.zeros(dkrt_ref.shape, F32)

        def keep(x, hh, axis):
            return _keep_head(x, hh, axis, (pj % 2) * 2 + hh)

        def qbody(qi, _):
            qs = pl.multiple_of(qi * t, t)
            qt = q_ref[pl.ds(qs, t), :]
            qtt = qt_ref[:, pl.ds(qs, t)]
            if rope is not None:
                qt = jnp.concatenate([qt, qr_ref[pl.ds(qs, t), :]], axis=1)
                qtt = jnp.concatenate([qtt, qrt_ref[:, pl.ds(qs, t)]], axis=0)
            dot = do_ref[pl.ds(qs, t), :]
            dott = dot_ref[:, pl.ds(qs, t)]
            pk = pack_ref[pl.ds(qs, t), :]
            lane = lax.broadcasted_iota(jnp.int32, pk.shape, 1)
            qh = [keep(qt, hh, 1) for hh in range(2)]
            qth = [keep(qtt, hh, 0) for hh in range(2)]
            doh = [keep(dot, hh, 1) for hh in range(2)]
            doth = [keep(dott, hh, 0) for hh in range(2)]
            a_col = [jnp.sum(jnp.where(lane == 2 * pj + hh, pk, 0.0), axis=1, keepdims=True) for hh in range(2)]
            d_col = [jnp.sum(jnp.where(lane == HEADS + 2 * pj + hh, pk, 0.0), axis=1, keepdims=True)
                     for hh in range(2)]
            dq_scr[...] = jnp.zeros(dq_scr.shape, F32)
            if has_bias:
                dcq_scr[...] = jnp.zeros(dcq_scr.shape, F32)

            def step(ki, masked):
                ks = pl.multiple_of(ki * t, t)
                kt = k_ref[pl.ds(ks, t), :]
                if rope is not None:
                    kt = jnp.concatenate([kt, kr_ref[pl.ds(ks, t), :]], axis=1)
                vt = v_ref[pl.ds(ks, t), :]
                dq_acc = dq_scr[...]
                dk_acc = jnp.zeros((kw, t), F32)
                dv_acc = jnp.zeros((LANES, t), F32)
                for hh in range(2):
                    s = lax.dot_general(qh[hh], kt, _NT, preferred_element_type=F32) + a_col[hh]
                    if has_bias:
                        s = s - ck_ref[hh, :, pl.ds(ks, t)]
                    p = jnp.exp2(s)
                    if masked:
                        p = jnp.where(_tile_mask(t, chunk_mask, False), p, 0.0)
                    dpd = lax.dot_general(doh[hh], vt, _NT, preferred_element_type=F32)
                    ds = p * (dpd - d_col[hh])
                    dsb = ds.astype(MXU)
                    dv_acc = dv_acc + jnp.dot(doth[hh], p.astype(MXU), preferred_element_type=F32)
                    dk_acc = dk_acc + jnp.dot(qth[hh], dsb, preferred_element_type=F32)
                    dq_acc = dq_acc + jnp.dot(dsb, keep(kt, hh, 1), preferred_element_type=F32)
                    if has_bias:
                        dcq_scr[hh] += jnp.sum(ds, axis=1, keepdims=True)
                        dck_ref[hh, :, pl.ds(ks, t)] += -jnp.sum(ds, axis=0, keepdims=True)
                dq_scr[...] = dq_acc
                dvt_ref[:, pl.ds(ks, t)] += dv_acc
                dkt_ref[:, pl.ds(ks, t)] += dk_acc[:LANES, :] * k_scale
                if rope is not None:
                    dkrt_ref[0, :, pl.ds(ks, t)] += dk_acc[LANES:, :] * k_scale

            def loop_body(ki, carry):
                step(ki, False)
                return carry

            lax.fori_loop(0, qi, loop_body, 0)
            step(qi, True)
            dq_ref[pl.ds(qs, t), :] = (dq_scr[:, :LANES] * q_scale).astype(dq_ref.dtype)
            if rope is not None:
                dqr_ref[0, pl.ds(qs, t), :] = dq_scr[:, LANES:] * q_scale
            if has_bias:
                old = dcq_ref[pl.ds(qs, t), :]
                dcq_ref[pl.ds(qs, t), :] = jnp.where(lane == 2 * pj, dcq_scr[0],
                                                     jnp.where(lane == 2 * pj + 1, dcq_scr[1], old))
            return 0

        lax.fori_loop(0, nq, qbody, 0)

    def tok(blk):
        return pl.BlockSpec((s_len, LANES), lambda j: (0, blk + j))

    trn = pl.BlockSpec((LANES, s_len), lambda j: (j, 0))
    shared = _full((s_len, LANES))
    rowb = pl.BlockSpec((2, 1, s_len), lambda j: (j, 0, 0))
    ins = [q, k, v, do, qt_, dot_, pack]
    in_specs = [tok(q_blk), tok(k_blk), tok(v_blk), tok(0), trn, trn, shared]
    out_shape = [_sds((s_len, PAIRS * LANES), MXU), _sds((PAIRS * LANES, s_len)), _sds((PAIRS * LANES, s_len))]
    out_specs = [tok(0), trn, trn]
    if has_bias:
        ins.append(ck_row)
        in_specs.append(rowb)
        out_shape += [_sds((s_len, LANES)), _sds((HEADS, 1, s_len))]
        out_specs += [shared, rowb]
    if rope is not None:
        ins += list(rope)
        in_specs += [pl.BlockSpec((s_len, LANES), lambda j: (0, j // 2)), shared,
                     pl.BlockSpec((LANES, s_len), lambda j: (j // 2, 0))]
        out_shape += [_sds((PAIRS, s_len, LANES)), _sds((PAIRS, LANES, s_len))]
        out_specs += [pl.BlockSpec((1, s_len, LANES), lambda j: (j, 0, 0)),
                      pl.BlockSpec((1, LANES, s_len), lambda j: (j, 0, 0))]
    return pl.pallas_call(
        kern, name=name, grid=(PAIRS,), out_shape=out_shape, in_specs=in_specs, out_specs=out_specs,
        scratch_shapes=[pltpu.VMEM((t, kw), F32), pltpu.VMEM((2, t, 1), F32)],
        compiler_params=_params("arbitrary"),
    )(*ins)


def _silu(a):
    return a * jax.nn.sigmoid(a)


def _k_out(of, om, z, x, gate, wout, tm=256):
    s_len, d = x.shape

    def kern(of_ref, om_ref, fg_ref, mg_ref, x_ref, gate_ref, w_ref, xo_ref, y_ref, u_ref):
        u_ref[:, :GROUP_W] = (of_ref[...] * _silu(fg_ref[...])).astype(MXU)
        u_ref[:, GROUP_W:] = (om_ref[...] * _silu(mg_ref[...])).astype(MXU)
        y = jnp.dot(u_ref[...], w_ref[...], preferred_element_type=F32)
        y_ref[...] = y
        xo_ref[...] = x_ref[...] + gate_ref[...] * y

    return pl.pallas_call(
        kern, name="k_out", grid=(s_len // tm,),
        out_shape=[_sds((s_len, d)), _sds((s_len, d)), _sds((s_len, 2 * GROUP_W), MXU)],
        in_specs=[_rows(tm, GROUP_W), _rows(tm, GROUP_W), _rows(tm, GROUP_W, Z_FG // GROUP_W),
                  _rows(tm, GROUP_W, Z_MG // GROUP_W), _rows(tm, d), _full((1, d)), _full((2 * GROUP_W, d))],
        out_specs=[_rows(tm, d), _rows(tm, d), _rows(tm, 2 * GROUP_W)],
        compiler_params=_params("arbitrary"),
    )(of, om, z, z, x, gate, wout)


def _k_loss(x, gf, tgt, tm=256):
    s_len, d = x.shape

    def kern(x_ref, g_ref, t_ref, loss_ref, dx_ref, dg_ref):
        i = pl.program_id(0)
        xv = x_ref[...]
        r = lax.rsqrt(jnp.mean(xv * xv, axis=-1, keepdims=True) + EPS)
        xh = xv * r
        diff = xh * g_ref[...] - t_ref[...]
        part = 0.5 * jnp.sum(jnp.mean(diff * diff, axis=-1, keepdims=True))
        dout = diff * (1.0 / d)
        dxh = dout * g_ref[...]
        dx_ref[...] = r * (dxh - xh * jnp.mean(dxh * xh, axis=-1, keepdims=True))

        @pl.when(i == 0)
        def _():
            loss_ref[...] = jnp.zeros_like(loss_ref)
            dg_ref[...] = jnp.zeros_like(dg_ref)

        loss_ref[...] += jnp.full(loss_ref.shape, part, F32)
        dg_ref[...] += jnp.sum(dout * xh, axis=0, keepdims=True)

    return pl.pallas_call(
        kern, name="k_loss", grid=(s_len // tm,),
        out_shape=[_sds((1, LANES)), _sds((s_len, d)), _sds((1, d))],
        in_specs=[_rows(tm, d), _full((1, d)), _rows(tm, d)],
        out_specs=[_full((1, LANES)), _rows(tm, d), _full((1, d))],
        compiler_params=_params("arbitrary"),
    )(x, gf, tgt)


def _kb_out(dxo, y, gate, wout_t, of, om, z, tm=256):
    s_len, d = dxo.shape

    def kern(dxo_ref, y_ref, gate_ref, wt_ref, of_ref, om_ref, fg_ref, mg_ref,
             dy_ref, dof_ref, dom_ref, dfg_ref, dmg_ref, dlf_ref, dlm_ref, dgate_ref):
        i = pl.program_id(0)
        dxv = dxo_ref[...]

        @pl.when(i == 0)
        def _():
            dgate_ref[...] = jnp.zeros_like(dgate_ref)

        dgate_ref[...] += jnp.sum(dxv * y_ref[...], axis=0, keepdims=True)
        dyb = (dxv * gate_ref[...]).astype(MXU)
        dy_ref[...] = dyb
        du = jnp.dot(dyb, wt_ref[...], preferred_element_type=F32)
        head_of = (lax.broadcasted_iota(jnp.int32, (GROUP_W, LANES), 0) // HEAD_DIM
                   == lax.broadcasted_iota(jnp.int32, (GROUP_W, LANES), 1)).astype(F32)
        for du_g, o_ref, g_ref, do_ref, dg_ref, dl_ref in (
                (du[:, :GROUP_W], of_ref, fg_ref, dof_ref, dfg_ref, dlf_ref),
                (du[:, GROUP_W:], om_ref, mg_ref, dom_ref, dmg_ref, dlm_ref)):
            a = g_ref[...]
            sg = jax.nn.sigmoid(a)
            ov = o_ref[...]
            dov = du_g * (a * sg)
            do_ref[...] = dov.astype(MXU)
            dg_ref[...] = (du_g * ov * (sg * (1.0 + a * (1.0 - sg)))).astype(MXU)
            dl_ref[...] = jnp.dot(dov * ov, head_of, precision=lax.Precision.HIGHEST, preferred_element_type=F32)

    return pl.pallas_call(
        kern, name="kb_out", grid=(s_len // tm,),
        out_shape=[_sds((s_len, d), MXU), _sds((s_len, GROUP_W), MXU), _sds((s_len, GROUP_W), MXU),
                   _sds((s_len, GROUP_W), MXU), _sds((s_len, GROUP_W), MXU), _sds((s_len, LANES)),
                   _sds((s_len, LANES)), _sds((1, d))],
        in_specs=[_rows(tm, d), _rows(tm, d), _full((1, d)), _full((d, 2 * GROUP_W)), _rows(tm, GROUP_W),
                  _rows(tm, GROUP_W), _rows(tm, GROUP_W, Z_FG // GROUP_W), _rows(tm, GROUP_W, Z_MG // GROUP_W)],
        out_specs=[_rows(tm, d), _rows(tm, GROUP_W), _rows(tm, GROUP_W), _rows(tm, GROUP_W),
                   _rows(tm, GROUP_W), _rows(tm, LANES), _rows(tm, LANES), _full((1, d))],
        compiler_params=_params("arbitrary"),
    )(dxo, y, gate, wout_t, of, om, z, z)


def _kb_prep(dqn, dqr, dkn, dv, dkr4, dff, z, cos, sin, gq, gkv, wuq_t, wukv_t, tm=512):
    s_len = z.shape[0]
    qw = GROUP_W + ROPE_W
    tail = Q_LORA + KV_LORA + LANES

    def kern(dqn_ref, dqr_ref, dkn_ref, dv_ref, dkr_ref, dff_ref, ql_ref, kvl_ref, cos_ref, sin_ref,
             gq_ref, gkv_ref, wuqt_ref, wukvt_ref, dq_ref, dz_ref, dgq_ref, dgkv_ref):
        i = pl.program_id(0)

        @pl.when(i == 0)
        def _():
            dgq_ref[...] = jnp.zeros_like(dgq_ref)
            dgkv_ref[...] = jnp.zeros_like(dgkv_ref)

        cs, sn = cos_ref[...], sin_ref[...]
        dq_ref[:, :GROUP_W] = dqn_ref[...]
        for half in range(ROPE_W // LANES):
            sl = slice(half * LANES, (half + 1) * LANES)
            dq_ref[:, GROUP_W + half * LANES:GROUP_W + (half + 1) * LANES] = _rope_bwd(dqr_ref[:, sl], cs, sn).astype(MXU)
        dqn = jnp.dot(dq_ref[...], wuqt_ref[...], preferred_element_type=F32)
        ql = ql_ref[...]
        rq = lax.rsqrt(jnp.mean(ql * ql, axis=-1, keepdims=True) + EPS)
        qh = ql * rq
        dgq_ref[...] += jnp.sum(dqn * qh, axis=0, keepdims=True)
        dqh = dqn * gq_ref[...]
        dz_ref[:, :Q_LORA] = (rq * (dqh - qh * jnp.mean(dqh * qh, axis=-1, keepdims=True))).astype(MXU)

        dkvn = (jnp.dot(dkn_ref[...], wukvt_ref[:GROUP_W, :], preferred_element_type=F32)
                + jnp.dot(dv_ref[...], wukvt_ref[GROUP_W:, :], preferred_element_type=F32))
        kvl = kvl_ref[...]
        rk = lax.rsqrt(jnp.mean(kvl * kvl, axis=-1, keepdims=True) + EPS)
        kh = kvl * rk
        dgkv_ref[...] += jnp.sum(dkvn * kh, axis=0, keepdims=True)
        dkh = dkvn * gkv_ref[...]
        dz_ref[:, Q_LORA:Q_LORA + KV_LORA] = (
            rk * (dkh - kh * jnp.mean(dkh * kh, axis=-1, keepdims=True))).astype(MXU)

        g4 = dkr_ref[...]
        g = g4
        for rep in range(1, LANES // ROPE):
            g = g + pltpu.roll(g4, rep * ROPE, 1)
        lane = lax.broadcasted_iota(jnp.int32, g.shape, 1)
        dmisc = jnp.where(lane < ROPE, _rope_bwd(g, cs, sn), 0.0) + dff_ref[...]
        dz_ref[:, Q_LORA + KV_LORA:] = dmisc.astype(MXU)

    return pl.pallas_call(
        kern, name="kb_prep", grid=(s_len // tm,),
        out_shape=[_sds((s_len, qw), MXU), _sds((s_len, tail), MXU), _sds((1, Q_LORA)), _sds((1, KV_LORA))],
        in_specs=[_rows(tm, GROUP_W), _rows(tm, ROPE_W), _rows(tm, GROUP_W), _rows(tm, GROUP_W), _rows(tm, LANES),
                  _rows(tm, LANES), _rows(tm, Q_LORA, Z_QL // Q_LORA), _rows(tm, KV_LORA, Z_KV // KV_LORA),
                  _rows(tm, LANES), _rows(tm, LANES), _full((1, Q_LORA)), _full((1, KV_LORA)),
                  _full((qw, Q_LORA)), _full((2 * GROUP_W, KV_LORA))],
        out_specs=[_rows(tm, qw), _rows(tm, tail), _full((1, Q_LORA)), _full((1, KV_LORA))],
        compiler_params=_params("arbitrary"),
    )(dqn, dqr, dkn, dv, dkr4, dff, z, z, cos, sin, gq, gkv, wuq_t, wukv_t)


def _kb_in(dz, win_t, x, g, mod3, dxo, tm=256):
    s_len, d = x.shape

    def kern(dz_ref, wt_ref, x_ref, g_ref, mod_ref, dxo_ref, dx_ref, acc_ref):
        i = pl.program_id(0)

        @pl.when(i == 0)
        def _():
            acc_ref[...] = jnp.zeros_like(acc_ref)

        dh = jnp.dot(dz_ref[...], wt_ref[...], preferred_element_type=F32)
        xv = x_ref[...]
        r = lax.rsqrt(jnp.mean(xv * xv, axis=-1, keepdims=True) + EPS)
        xh = xv * r
        xn = xh * g_ref[...]
        dxn = dh * (1.0 + mod_ref[1:2, :])
        acc_ref[0:1, :] += jnp.sum(dh, axis=0, keepdims=True)
        acc_ref[1:2, :] += jnp.sum(dh * xn, axis=0, keepdims=True)
        acc_ref[2:3, :] += jnp.sum(dxn * xh, axis=0, keepdims=True)
        dxh = dxn * g_ref[...]
        dx_ref[...] = dxo_ref[...] + r * (dxh - xh * jnp.mean(dxh * xh, axis=-1, keepdims=True))

    return pl.pallas_call(
        kern, name="kb_in", grid=(s_len // tm,),
        out_shape=[_sds((s_len, d)), _sds((3, d))],
        in_specs=[_rows(tm, Z_W), _full((Z_W, d)), _rows(tm, d), _full((1, d)), _full((3, d)), _rows(tm, d)],
        out_specs=[_rows(tm, d), _full((3, d))],
        compiler_params=_params("arbitrary"),
    )(dz, win_t, x, g, mod3, dxo)


def _matmul_acc(a, b, name, tk=512):
    m, kk = a.shape
    n = b.shape[1]
    tn = n if n <= 1536 else 1536
    tk = min(tk, kk)

    def kern(a_ref, b_ref, o_ref):
        @pl.when(pl.program_id(1) == 0)
        def _():
            o_ref[...] = jnp.zeros_like(o_ref)

        o_ref[...] += jnp.dot(a_ref[...], b_ref[...], preferred_element_type=F32)

    return pl.pallas_call(
        kern, name=name, grid=(n // tn, kk // tk), out_shape=_sds((m, n)),
        in_specs=[pl.BlockSpec((m, tk), lambda j, k: (0, k)), pl.BlockSpec((tk, tn), lambda j, k: (k, j))],
        out_specs=pl.BlockSpec((m, tn), lambda j, k: (0, j)),
        compiler_params=_params("arbitrary", "arbitrary"),
    )(a, b)


def _adamw(gslots, w, m, v, name):
    n, r, c = gslots.shape
    tm = r
    for cand in (256, 128, 64, 32, 16, 8):
        if r % cand == 0:
            tm = cand
            break

    def kern(g_ref, w_ref, m_ref, v_ref, go_ref, d_ref, mo_ref, vo_ref):
        g = g_ref[0]
        for s in range(1, n):
            g = g + g_ref[s]
        m_new = ADAM_B1 * m_ref[...] + (1.0 - ADAM_B1) * g
        v_new = ADAM_B2 * v_ref[...] + (1.0 - ADAM_B2) * (g * g)
        m_hat = m_new / (1.0 - ADAM_B1 ** ADAM_STEP)
        v_hat = v_new / (1.0 - ADAM_B2 ** ADAM_STEP)
        go_ref[...] = g
        mo_ref[...] = m_new
        vo_ref[...] = v_new
        d_ref[...] = -ADAM_LR * (m_hat / (jnp.sqrt(v_hat) + ADAM_EPS) + ADAM_WD * w_ref[...])

    row = pl.BlockSpec((tm, c), lambda i: (i, 0))
    return pl.pallas_call(
        kern, name=name, grid=(r // tm,), out_shape=[_sds((r, c))] * 4,
        in_specs=[pl.BlockSpec((n, tm, c), lambda i: (0, i, 0)), row, row, row],
        out_specs=[row] * 4,
        compiler_params=_params("arbitrary"),
    )(gslots, w, m, v)


def _perm_w_in(w):
    pad = jnp.zeros(w.shape[:-1] + (Z_W - Z_MISC - ROPE - HEADS,), w.dtype)
    return jnp.concatenate([w[..., 0:1536], w[..., 1544:2056], w[..., 2472:2984], w[..., 2056:2312],
                            w[..., 2312:2440], w[..., 2440:2472], w[..., 1536:1544], pad], axis=-1)


def _unperm_w_in(g):
    ff0 = Z_MISC + MISC_FF
    return jnp.concatenate([g[..., 0:1536], g[..., ff0:ff0 + HEADS], g[..., Z_FG:Z_FG + GROUP_W],
                            g[..., Z_QL:Z_QL + Q_LORA], g[..., Z_KV:Z_KV + KV_LORA],
                            g[..., Z_MISC:Z_MISC + ROPE], g[..., Z_MG:Z_MG + GROUP_W]], axis=-1)


def _perm_w_uq(w):
    lead = w.shape[:-1]
    wh = w.reshape(lead + (HEADS, NOPE + ROPE))
    return jnp.concatenate([wh[..., :NOPE].reshape(lead + (GROUP_W,)),
                            wh[..., NOPE:].reshape(lead + (ROPE_W,))], axis=-1)


def _unperm_w_uq(g):
    lead = g.shape[:-1]
    parts = [g[..., :GROUP_W].reshape(lead + (HEADS, NOPE)), g[..., GROUP_W:].reshape(lead + (HEADS, ROPE))]
    return jnp.concatenate(parts, axis=-1).reshape(lead + (HEADS * (NOPE + ROPE),))


def _perm_w_ukv(w):
    lead = w.shape[:-1]
    wh = w.reshape(lead + (HEADS, 2 * HEAD_DIM))
    return jnp.concatenate([wh[..., :NOPE].reshape(lead + (GROUP_W,)),
                            wh[..., NOPE:].reshape(lead + (GROUP_W,))], axis=-1)


def _unperm_w_ukv(g):
    lead = g.shape[:-1]
    parts = [g[..., :GROUP_W].reshape(lead + (HEADS, NOPE)), g[..., GROUP_W:].reshape(lead + (HEADS, HEAD_DIM))]
    return jnp.concatenate(parts, axis=-1).reshape(lead + (2 * GROUP_W,))


def _rope_tables(positions):
    inv_freq = 1.0 / (ROPE_THETA ** (jnp.arange(0, ROPE, 2, dtype=F32) / ROPE))
    ang = positions.astype(F32)[:, None] * inv_freq
    cos, sin = jnp.cos(ang), jnp.sin(ang)
    reps = LANES // ROPE
    return jnp.tile(jnp.concatenate([cos, cos], axis=1), (1, reps)), jnp.tile(jnp.concatenate([-sin, sin], axis=1), (1, reps))


def _local_step(x, mod, positions, loss_target, norm_g, b_f, q_norm_g, kv_norm_g, final_g, w_in, w_uq, w_ukv, w_out):
    n_l = norm_g.shape[0]
    s_len, d = x.shape
    cos, sin = _rope_tables(positions)
    qb, kb, vb = Z_FQ // LANES, Z_FK // LANES, Z_FV // LANES

    def pack_rows(a_rows, delta):
        return jnp.concatenate([a_rows.T, delta[:, :HEADS], jnp.zeros((s_len, LANES - 2 * HEADS), F32)], axis=1)

    saved = []
    for l in range(n_l):
        mod3 = mod[l].reshape(3, d)
        z, h, qkv = _k_in(x, norm_g[l][None], mod3, w_in[l])
        fft = z[:, Z_MISC + MISC_FF:Z_MISC + MISC_FF + HEADS].T
        bf = b_f[l][:, None]
        c2 = _k_cum(fft, bf) * LOG2E
        ot_f, lse_f = _attn_fwd(qkv, qb, qkv, kb, qkv[:, Z_FV:].T, (c2[:, None, :], c2[:, :, None]), None, False,
                                "fox_fwd")
        mq, mqr, mk, mv, kr4, qn, kvn = _k_prep(z, cos, sin, q_norm_g[l][None], kv_norm_g[l][None], w_uq[l], w_ukv[l])
        ot_m, lse_m = _attn_fwd(mq, 0, mk, 0, mv.T, None, (mqr, kr4), True, "mla_fwd")
        of, om = ot_f.T, ot_m.T
        x_new, y, u = _k_out(of, om, z, x, mod3[2:3], w_out[l])
        saved.append((x, z, h, qkv, fft, bf, c2, lse_f, mq, mqr, mk, mv, kr4, lse_m, of, om, qn, kvn, y, u, mod3))
        x = x_new

    loss_row, dx, dfinal = _k_loss(x, final_g[None], loss_target)

    grads = {k: [] for k in ("norm_g", "mod", "w_in", "b_f", "q_norm_g", "w_uq", "kv_norm_g", "w_ukv", "w_out")}
    for l in range(n_l - 1, -1, -1):
        (x_l, z, h, qkv, fft, bf, c2, lse_f, mq, mqr, mk, mv, kr4, lse_m, of, om, qn, kvn, y, u, mod3) = saved[l]
        dyb, dof, dom, dfg, dmg, dlt_f, dlt_m, dgate = _kb_out(dx, y, mod3[2:3], w_out[l].T, of, om, z)
        grads["w_out"].append(_matmul_acc(u.T, dyb, "dw_out"))

        dfq, dkt, dvt, dcq, dck = _attn_bwd(
            qkv, qb, qkv, kb, qkv, vb, dof, qkv[:, :GROUP_W].T, dof.T,
            pack_rows(c2 - lse_f.reshape(HEADS, s_len), dlt_f), c2[:, None, :], None, False,
            FOX_SCALE, 1.0 / LOG2E, "fox_bwd")
        dfk, dfv = dkt.T.astype(MXU), dvt.T.astype(MXU)
        dfft, dbf = _k_cum_bwd(dcq[:, :HEADS].T + dck.reshape(HEADS, s_len), fft, bf)
        grads["b_f"].append(dbf[:, 0])

        dmq, dkt, dvt, dqr4, dkr4t = _attn_bwd(
            mq, 0, mk, 0, mv, 0, dom, mq.T, dom.T, pack_rows(-lse_m.reshape(HEADS, s_len), dlt_m), None,
            (mqr, kr4, mqr.T), True, MLA_SCALE, 1.0 / LOG2E, "mla_bwd")
        dkn, dmv = dkt.T.astype(MXU), dvt.T.astype(MXU)
        dqr = jnp.concatenate([dqr4[0] + dqr4[1], dqr4[2] + dqr4[3]], axis=1)
        dkr4 = (dkr4t[0] + dkr4t[1] + dkr4t[2] + dkr4t[3]).T
        dff = jnp.pad(dfft.T, ((0, 0), (MISC_FF, LANES - MISC_FF - HEADS)))
        dq_b, dz_tail, dgq, dgkv = _kb_prep(dmq, dqr, dkn, dmv, dkr4, dff, z, cos, sin, q_norm_g[l][None],
                                            kv_norm_g[l][None], w_uq[l].T, w_ukv[l].T)
        grads["q_norm_g"].append(dgq[0])
        grads["kv_norm_g"].append(dgkv[0])
        grads["w_uq"].append(_matmul_acc(qn.T, dq_b, "dw_uq"))
        kvn_t = kvn.T
        grads["w_ukv"].append(jnp.concatenate([_matmul_acc(kvn_t, dkn, "dw_ukv_k"), _matmul_acc(kvn_t, dmv, "dw_ukv_v")],
                                              axis=1))
        dz = jnp.concatenate([dfq, dfk, dfv, dfg, dmg, dz_tail], axis=1)
        grads["w_in"].append(_matmul_acc(h.T, dz, "dw_in"))
        dx, acc3 = _kb_in(dz, w_in[l].T, x_l, norm_g[l][None], mod3, dx)
        grads["norm_g"].append(acc3[2])
        grads["mod"].append(jnp.concatenate([acc3[0], acc3[1], dgate[0]]))
    grads = {k: jnp.stack(v[::-1]) for k, v in grads.items()}
    grads["final_g"] = dfinal[0]
    return loss_row[0, 0], dx, grads


def _pack_small(parts, total):
    flat = jnp.concatenate([p.reshape(-1) for p in parts])
    return jnp.pad(flat, (0, total - flat.shape[0])).reshape(total // LANES, LANES)


def kernel(x, c, positions, norm_g, w_ada, b_ada, w_in, b_f, q_norm_g, w_uq, kv_norm_g, w_ukv, w_out, final_g, loss_target, m_norm_g, m_w_ada, m_b_ada, m_w_in, m_b_f, m_q_norm_g, m_w_uq, m_kv_norm_g, m_w_ukv, m_w_out, m_final_g, v_norm_g, v_w_ada, v_b_ada, v_w_in, v_b_f, v_q_norm_g, v_w_uq, v_kv_norm_g, v_w_ukv, v_w_out, v_final_g):
    n_l, d = norm_g.shape
    me = 4 * lax.axis_index("x") + 2 * lax.axis_index("y") + lax.axis_index("c")
    ada_c = w_ada.shape[2]
    in_c = w_in.shape[2]
    uq_c = w_uq.shape[2]
    ukv_c = w_ukv.shape[2]
    out_r = w_out.shape[1]

    cact = jnp.broadcast_to(jax.nn.silu(c), (N_DEV, d))
    g_in, g_uq, g_ukv, g_out, g_cact = _exchange(
        [w_in.reshape(n_l * d, in_c).astype(MXU), w_uq.reshape(n_l * Q_LORA, uq_c).astype(MXU),
         w_ukv.reshape(n_l * KV_LORA, ukv_c).astype(MXU), w_out.reshape(n_l * out_r, d).astype(MXU), cact],
        [True] * 5, "gather_weights")
    w_in_f = _perm_w_in(g_in.reshape(N_DEV, n_l, d, in_c).transpose(1, 2, 0, 3).reshape(n_l, d, N_DEV * in_c))
    w_uq_f = _perm_w_uq(g_uq.reshape(N_DEV, n_l, Q_LORA, uq_c).transpose(1, 2, 0, 3).reshape(n_l, Q_LORA, -1))
    w_ukv_f = _perm_w_ukv(g_ukv.reshape(N_DEV, n_l, KV_LORA, ukv_c).transpose(1, 2, 0, 3).reshape(n_l, KV_LORA, -1))
    w_out_f = g_out.reshape(N_DEV, n_l, out_r, d).transpose(1, 0, 2, 3).reshape(n_l, N_DEV * out_r, d)
    cact_all = g_cact[:, 0, :]

    b_cols = lax.dynamic_slice_in_dim(b_ada, me * ada_c, ada_c, axis=1)[:, None, :]
    modpart = _modpart(cact_all, w_ada, b_cols)
    mod_send = jnp.pad(modpart.transpose(1, 0, 2), ((0, 0), (0, 8 - n_l), (0, 0)))
    (mod_recv,) = _exchange([mod_send], [False], "scatter_mod")
    mod = mod_recv.transpose(1, 0, 2).reshape(8, N_DEV * ada_c)[:n_l]

    loss, dx, gr = _local_step(x[0], mod, positions[0], loss_target[0], norm_g, b_f, q_norm_g, kv_norm_g,
                               final_g, w_in_f, w_uq_f, w_ukv_f, w_out_f)

    s_in = _unperm_w_in(gr["w_in"]).reshape(n_l, d, N_DEV, in_c).transpose(2, 0, 1, 3).reshape(N_DEV, n_l * d, in_c)
    s_uq = _unperm_w_uq(gr["w_uq"]).reshape(n_l, Q_LORA, N_DEV, uq_c).transpose(2, 0, 1, 3).reshape(
        N_DEV, n_l * Q_LORA, uq_c)
    s_ukv = _unperm_w_ukv(gr["w_ukv"]).reshape(n_l, KV_LORA, N_DEV, ukv_c).transpose(2, 0, 1, 3).reshape(
        N_DEV, n_l * KV_LORA, ukv_c)
    s_out = gr["w_out"].reshape(n_l, N_DEV, out_r, d).transpose(1, 0, 2, 3).reshape(N_DEV, n_l * out_r, d)
    small_parts = [gr["norm_g"], gr["mod"], gr["b_f"], gr["q_norm_g"], gr["kv_norm_g"], gr["final_g"], cact[0]]
    sizes = [int(np.prod(p.shape)) for p in small_parts]
    total = -(-sum(sizes) // 1024) * 1024
    small = _pack_small(small_parts, total)
    r_in, r_uq, r_ukv, r_out, r_small = _exchange([s_in, s_uq, s_ukv, s_out, small],
                                                  [False, False, False, False, True], "exchange_grads")

    def upd(slots, w, m, v, name):
        shp = w.shape
        w2, m2, v2 = (a.reshape(slots.shape[1:]) for a in (w, m, v))
        return [o.reshape(shp) for o in _adamw(slots, w2, m2, v2, name)]

    o_in = upd(r_in, w_in, m_w_in, v_w_in, "adamw_w_in")
    o_uq = upd(r_uq, w_uq, m_w_uq, v_w_uq, "adamw_w_uq")
    o_ukv = upd(r_ukv, w_ukv, m_w_ukv, v_w_ukv, "adamw_w_ukv")
    o_out = upd(r_out, w_out, m_w_out, v_w_out, "adamw_w_out")

    offs = np.cumsum([0] + sizes)
    flat_all = r_small.reshape(N_DEV, total)
    dmod_all = flat_all[:, offs[1]:offs[2]].reshape(N_DEV, n_l, 3 * d)
    dmod_cols = lax.dynamic_slice_in_dim(dmod_all, me * ada_c, ada_c, axis=2).transpose(1, 0, 2)
    cact_cols = flat_all[:, offs[6]:offs[7]][:, :, None]
    g_ada = _ada_grad(cact_cols, dmod_cols)
    o_ada = upd(g_ada.reshape(1, n_l * d, ada_c), w_ada, m_w_ada, v_w_ada, "adamw_w_ada")

    zero_c = jnp.zeros((d,), F32)
    small_w = [_pack_small([norm_g, b_ada, b_f, q_norm_g, kv_norm_g, final_g, zero_c], total),
               _pack_small([m_norm_g, m_b_ada, m_b_f, m_q_norm_g, m_kv_norm_g, m_final_g, zero_c], total),
               _pack_small([v_norm_g, v_b_ada, v_b_f, v_q_norm_g, v_kv_norm_g, v_final_g, zero_c], total)]
    o_small = [o.reshape(-1) for o in _adamw(r_small, *small_w, "adamw_small")]
    shapes = [norm_g.shape, b_ada.shape, b_f.shape, q_norm_g.shape, kv_norm_g.shape, final_g.shape]

    def small_out(kind, idx):
        return o_small[kind][offs[idx]:offs[idx + 1]].reshape(shapes[idx])

    loss_all = lax.psum(loss, ("x", "y", "c"))
    outs = [loss_all, dx[None]]
    for kind in range(4):
        outs += [small_out(kind, 0), o_ada[kind], small_out(kind, 1), o_in[kind], small_out(kind, 2),
                 small_out(kind, 3), o_uq[kind], small_out(kind, 4), o_ukv[kind], o_out[kind], small_out(kind, 5)]
    return tuple(outs)
```

```python
import jax
import jax.numpy as jnp
import numpy as np
from jax import lax
from jax.experimental import pallas as pl
from jax.experimental.pallas import tpu as pltpu

F32 = jnp.float32
MXU = jnp.bfloat16

N_DEV = 8
HEADS = 8
PAIRS = HEADS // 2
HEAD_DIM = 64
NOPE = 64
ROPE = 32
HALF_ROPE = ROPE // 2
Q_LORA = 256
KV_LORA = 128
CHUNK = 64
GROUP_W = HEADS * HEAD_DIM
ROPE_W = HEADS * ROPE
EPS = 1e-6
ROPE_THETA = 10000.0
N_IN = 2984

Z_FQ, Z_FK, Z_FV, Z_FG, Z_MG, Z_QL, Z_KV, Z_MISC, Z_W = 0, 512, 1024, 1536, 2048, 2560, 2816, 2944, 3072
MISC_FF = ROPE

ADAM_LR = 0.001
ADAM_B1 = 0.9
ADAM_B2 = 0.999
ADAM_EPS = 1e-08
ADAM_WD = 0.01
ADAM_STEP = 10

VMEM_LIMIT_V7X = 56 * 1024 * 1024
LANES = 128
ATTN_TILE = 512
LOG2E = 1.4426950408889634
FOX_SCALE = HEAD_DIM ** -0.5
MLA_SCALE = (NOPE + ROPE) ** -0.5

_NT = (((1,), (1,)), ((), ()))
_TN = (((0,), (0,)), ((), ()))


def _params(*sem):
    return pltpu.CompilerParams(dimension_semantics=sem, vmem_limit_bytes=VMEM_LIMIT_V7X)


def _sds(shape, dtype=F32):
    return jax.ShapeDtypeStruct(shape, dtype)


def _full(shape):
    nd = len(shape)
    return pl.BlockSpec(shape, lambda *_: (0,) * nd)


def _rows(tm, width, col=0):
    return pl.BlockSpec((tm, width), lambda i: (i, col))


def _exchange(arrs, gather, name):
    n = len(arrs)

    def kern(*refs):
        ins, outs = refs[:n], refs[n:2 * n]
        send_sems, recv_sems, loc_sems = refs[2 * n:]
        x, y, c = lax.axis_index("x"), lax.axis_index("y"), lax.axis_index("c")
        me = 4 * x + 2 * y + c

        def src(i, j):
            return ins[i] if gather[i] else ins[i].at[j]

        local = [pltpu.make_async_copy(src(i, me), outs[i].at[me], loc_sems.at[i]) for i in range(n)]
        for cp in local:
            cp.start()
        sends, recvs = [], []
        for k in range(1, N_DEV):
            px = 1 - x if k & 4 else x
            py = 1 - y if k & 2 else y
            pc = 1 - c if k & 1 else c
            p = 4 * px + 2 * py + pc
            for i in range(n):
                sends.append(pltpu.make_async_remote_copy(
                    src_ref=src(i, p), dst_ref=outs[i].at[me], send_sem=send_sems.at[i, k],
                    recv_sem=recv_sems.at[i, k], device_id=(px, py, pc), device_id_type=pl.DeviceIdType.MESH))
                recvs.append(pltpu.make_async_remote_copy(
                    src_ref=src(i, p), dst_ref=outs[i].at[p], send_sem=send_sems.at[i, k],
                    recv_sem=recv_sems.at[i, k], device_id=(px, py, pc), device_id_type=pl.DeviceIdType.MESH))
        for cp in sends:
            cp.start()
        for cp in recvs:
            cp.wait_recv()
        for cp in sends:
            cp.wait_send()
        for cp in local:
            cp.wait()

    out_shape = []
    for a, g in zip(arrs, gather):
        shp = (N_DEV,) + tuple(a.shape) if g else tuple(a.shape)
        out_shape.append(_sds(shp, a.dtype))
    return pl.pallas_call(
        kern, name=name, out_shape=out_shape,
        in_specs=[pl.BlockSpec(memory_space=pl.ANY)] * n,
        out_specs=[pl.BlockSpec(memory_space=pl.ANY)] * n,
        scratch_shapes=[pltpu.SemaphoreType.DMA((n, N_DEV)), pltpu.SemaphoreType.DMA((n, N_DEV)),
                        pltpu.SemaphoreType.DMA((n,))],
        compiler_params=pltpu.CompilerParams(has_side_effects=True),
    )(*arrs)


def _modpart(cact8, w_ada, b_cols):
    n_l, d, cw = w_ada.shape

    def kern(c_ref, w_ref, b_ref, o_ref):
        o_ref[0] = jnp.dot(c_ref[...].astype(MXU), w_ref[0].astype(MXU), preferred_element_type=F32) + b_ref[0]

    return pl.pallas_call(
        kern, name="modpart", grid=(n_l,), out_shape=_sds((n_l, N_DEV, cw)),
        in_specs=[_full((N_DEV, d)), pl.BlockSpec((1, d, cw), lambda l: (l, 0, 0)),
                  pl.BlockSpec((1, 1, cw), lambda l: (l, 0, 0))],
        out_specs=pl.BlockSpec((1, N_DEV, cw), lambda l: (l, 0, 0)),
        compiler_params=_params("arbitrary"),
    )(cact8, w_ada, b_cols)


def _ada_grad(cact_cols, dmod_cols):
    n_l, _, cw = dmod_cols.shape
    d = cact_cols.shape[1]

    def kern(c_ref, dm_ref, o_ref):
        acc = c_ref[0] * dm_ref[0, 0:1, :]
        for s in range(1, N_DEV):
            acc = acc + c_ref[s] * dm_ref[0, s:s + 1, :]
        o_ref[0] = acc

    return pl.pallas_call(
        kern, name="ada_grad", grid=(n_l,), out_shape=_sds((n_l, d, cw)),
        in_specs=[_full((N_DEV, d, 1)), pl.BlockSpec((1, N_DEV, cw), lambda l: (l, 0, 0))],
        out_specs=pl.BlockSpec((1, d, cw), lambda l: (l, 0, 0)),
        compiler_params=_params("arbitrary"),
    )(cact_cols, dmod_cols)


def _k_in(x, g, mod3, w, tm=256):
    s_len, d = x.shape
    qkv_w = 3 * GROUP_W

    def kern(x_ref, g_ref, mod_ref, w_ref, z_ref, h_ref, qkv_ref):
        xv = x_ref[...]
        r = lax.rsqrt(jnp.mean(xv * xv, axis=-1, keepdims=True) + EPS)
        xn = xv * r * g_ref[...]
        h = (xn * (1.0 + mod_ref[1:2, :]) + mod_ref[0:1, :]).astype(MXU)
        h_ref[...] = h
        z = jnp.dot(h, w_ref[...], preferred_element_type=F32)
        z_ref[...] = z
        qkv_ref[:, :GROUP_W] = (z[:, Z_FQ:Z_FQ + GROUP_W] * (FOX_SCALE * LOG2E)).astype(MXU)
        qkv_ref[:, GROUP_W:] = z[:, Z_FK:Z_FK + 2 * GROUP_W].astype(MXU)

    return pl.pallas_call(
        kern, name="k_in", grid=(s_len // tm,),
        out_shape=[_sds((s_len, Z_W)), _sds((s_len, d), MXU), _sds((s_len, qkv_w), MXU)],
        in_specs=[_rows(tm, d), _full((1, d)), _full((3, d)), _full((d, Z_W))],
        out_specs=[_rows(tm, Z_W), _rows(tm, d), _rows(tm, qkv_w)],
        compiler_params=_params("arbitrary"),
    )(x, g, mod3, w)


def _k_cum(fft, bf):
    nh, s_len = fft.shape

    def kern(ff_ref, b_ref, cum_ref):
        r_i = lax.broadcasted_iota(jnp.int32, (LANES, LANES), 0)
        c_i = lax.broadcasted_iota(jnp.int32, (LANES, LANES), 1)
        upper = (r_i <= c_i).astype(F32)
        carry = jnp.zeros((nh, 1), F32)
        for cb in range(s_len // LANES):
            sl = slice(cb * LANES, (cb + 1) * LANES)
            xc = ff_ref[:, sl] + b_ref[...]
            lf = jnp.minimum(xc, 0.0) - jnp.log(1.0 + jnp.exp(-jnp.abs(xc)))
            cum_ref[:, sl] = jnp.dot(lf, upper, precision=lax.Precision.HIGHEST,
                                     preferred_element_type=F32) + carry
            carry = carry + jnp.sum(lf, axis=1, keepdims=True)

    return pl.pallas_call(
        kern, name="k_cum", out_shape=_sds((nh, s_len)),
        in_specs=[pl.BlockSpec(memory_space=pltpu.VMEM)] * 2,
        out_specs=pl.BlockSpec(memory_space=pltpu.VMEM),
        compiler_params=_params(),
    )(fft, bf)


def _k_cum_bwd(dck, fft, bf):
    nh, s_len = fft.shape

    def kern(dc_ref, ff_ref, b_ref, dff_ref, db_ref):
        r_i = lax.broadcasted_iota(jnp.int32, (LANES, LANES), 0)
        c_i = lax.broadcasted_iota(jnp.int32, (LANES, LANES), 1)
        lower = (r_i >= c_i).astype(F32)
        carry = jnp.zeros((nh, 1), F32)
        db = jnp.zeros((nh, 1), F32)
        for cb in range(s_len // LANES - 1, -1, -1):
            sl = slice(cb * LANES, (cb + 1) * LANES)
            dc = dc_ref[:, sl]
            dlf = jnp.dot(dc, lower, precision=lax.Precision.HIGHEST, preferred_element_type=F32) + carry
            carry = carry + jnp.sum(dc, axis=1, keepdims=True)
            dff = dlf * jax.nn.sigmoid(-(ff_ref[:, sl] + b_ref[...]))
            dff_ref[:, sl] = dff
            db = db + jnp.sum(dff, axis=1, keepdims=True)
        db_ref[...] = jnp.broadcast_to(db, (nh, LANES))

    return pl.pallas_call(
        kern, name="k_cum_bwd", out_shape=[_sds((nh, s_len)), _sds((nh, LANES))],
        in_specs=[pl.BlockSpec(memory_space=pltpu.VMEM)] * 3,
        out_specs=[pl.BlockSpec(memory_space=pltpu.VMEM)] * 2,
        compiler_params=_params(),
    )(dck, fft, bf)


def _swap16(t):
    lane = lax.broadcasted_iota(jnp.int32, t.shape, 1)
    return jnp.where(lane % ROPE < HALF_ROPE, pltpu.roll(t, LANES - HALF_ROPE, 1), pltpu.roll(t, HALF_ROPE, 1))


def _rope(t, cos, sin):
    return t * cos + _swap16(t) * sin


def _rope_bwd(dt, cos, sin):
    return dt * cos - _swap16(dt) * sin


def _k_prep(z, cos, sin, gq, gkv, wuq, wukv, tm=512):
    s_len = z.shape[0]
    qc = MLA_SCALE * LOG2E

    def kern(ql_ref, kvl_ref, misc_ref, cos_ref, sin_ref, gq_ref, gkv_ref, wuq_ref, wukv_ref,
             qn_out, qr_out, kn_out, v_out, kr_out, qn_ref, kvn_ref):
        cs, sn = cos_ref[...], sin_ref[...]
        ql = ql_ref[...]
        rq = lax.rsqrt(jnp.mean(ql * ql, axis=-1, keepdims=True) + EPS)
        qn = (ql * rq * gq_ref[...]).astype(MXU)
        qn_ref[...] = qn
        q = jnp.dot(qn, wuq_ref[...], preferred_element_type=F32)
        qn_out[...] = (q[:, :GROUP_W] * qc).astype(MXU)
        for half in range(ROPE_W // LANES):
            lo = GROUP_W + half * LANES
            qr_out[:, half * LANES:(half + 1) * LANES] = (_rope(q[:, lo:lo + LANES], cs, sn) * qc).astype(MXU)
        kvl = kvl_ref[...]
        rk = lax.rsqrt(jnp.mean(kvl * kvl, axis=-1, keepdims=True) + EPS)
        kvn = (kvl * rk * gkv_ref[...]).astype(MXU)
        kvn_ref[...] = kvn
        kv = jnp.dot(kvn, wukv_ref[...], preferred_element_type=F32)
        kn_out[...] = kv[:, :GROUP_W].astype(MXU)
        v_out[...] = kv[:, GROUP_W:].astype(MXU)
        misc = misc_ref[...]
        lane = lax.broadcasted_iota(jnp.int32, misc.shape, 1)
        kr = jnp.where(lane < ROPE, _rope(misc, cs, sn), 0.0)
        kr4 = kr
        for rep in range(1, LANES // ROPE):
            kr4 = kr4 + pltpu.roll(kr, rep * ROPE, 1)
        kr_out[...] = kr4.astype(MXU)

    return pl.pallas_call(
        kern, name="k_prep", grid=(s_len // tm,),
        out_shape=[_sds((s_len, GROUP_W), MXU), _sds((s_len, ROPE_W), MXU), _sds((s_len, GROUP_W), MXU),
                   _sds((s_len, GROUP_W), MXU), _sds((s_len, LANES), MXU), _sds((s_len, Q_LORA), MXU),
                   _sds((s_len, KV_LORA), MXU)],
        in_specs=[_rows(tm, Q_LORA, Z_QL // Q_LORA), _rows(tm, KV_LORA, Z_KV // KV_LORA),
                  _rows(tm, LANES, Z_MISC // LANES), _rows(tm, LANES), _rows(tm, LANES),
                  _full((1, Q_LORA)), _full((1, KV_LORA)), _full((Q_LORA, GROUP_W + ROPE_W)),
                  _full((KV_LORA, 2 * GROUP_W))],
        out_specs=[_rows(tm, GROUP_W), _rows(tm, ROPE_W), _rows(tm, GROUP_W), _rows(tm, GROUP_W), _rows(tm, LANES),
                   _rows(tm, Q_LORA), _rows(tm, KV_LORA)],
        compiler_params=_params("arbitrary"),
    )(z, z, z, cos, sin, gq, gkv, wuq, wukv)


def _tile_mask(t, chunk_mask, transposed):
    row = lax.broadcasted_iota(jnp.int32, (t, t), 0)
    col = lax.broadcasted_iota(jnp.int32, (t, t), 1)
    qi, ki = (col, row) if transposed else (row, col)
    if chunk_mask:
        return (ki // CHUNK) <= (qi // CHUNK)
    return ki <= qi


def _keep_head(x, hh, axis, rope_group):
    idx = lax.broadcasted_iota(jnp.int32, x.shape, axis)
    keep = (idx >= hh * HEAD_DIM) & (idx < (hh + 1) * HEAD_DIM)
    if x.shape[axis] != LANES:
        keep = keep | ((idx >= LANES + rope_group * ROPE) & (idx < LANES + (rope_group + 1) * ROPE))
    return jnp.where(keep, x, jnp.zeros_like(x))


def _attn_fwd(q, q_blk, k, k_blk, vt, bias, rope, chunk_mask, name):
    s_len = q.shape[0]
    t = min(ATTN_TILE, s_len // 2)
    nq = s_len // t

    def kern(*refs):
        q_ref, k_ref, vt_ref = refs[:3]
        pos = 3
        if bias is not None:
            cq_ref, ck_ref = refs[pos:pos + 2]
            pos += 2
        if rope is not None:
            qr_ref, kr_ref = refs[pos:pos + 2]
            pos += 2
        o_ref, lse_ref, m_scr, l_scr, acc_scr = refs[pos:]
        pj = pl.program_id(0)

        def qbody(qi, _):
            qs = pl.multiple_of(qi * t, t)
            qt = q_ref[pl.ds(qs, t), :]
            if rope is not None:
                qt = jnp.concatenate([qt, qr_ref[pl.ds(qs, t), :]], axis=1)
            qh = [_keep_head(qt, hh, 1, (pj % 2) * 2 + hh) for hh in range(2)]
            m_scr[...] = jnp.full(m_scr.shape, -jnp.inf, F32)
            l_scr[...] = jnp.zeros(l_scr.shape, F32)
            acc_scr[...] = jnp.zeros(acc_scr.shape, F32)

            def step(ki, masked):
                ks = pl.multiple_of(ki * t, t)
                kt = k_ref[pl.ds(ks, t), :]
                if rope is not None:
                    kt = jnp.concatenate([kt, kr_ref[pl.ds(ks, t), :]], axis=1)
                vtt = vt_ref[:, pl.ds(ks, t)]
                for hh in range(2):
                    st = lax.dot_general(kt, qh[hh], _NT, preferred_element_type=F32)
                    if bias is not None:
                        st = st + cq_ref[hh, :, pl.ds(qs, t)] - ck_ref[hh, pl.ds(ks, t), :]
                    if masked:
                        st = jnp.where(_tile_mask(t, chunk_mask, True), st, -jnp.inf)
                    m_old = m_scr[hh]
                    m_new = jnp.maximum(m_old, jnp.max(st, axis=0, keepdims=True))
                    pt = jnp.exp2(st - m_new)
                    alpha = jnp.exp2(m_old - m_new)
                    l_scr[hh] = alpha * l_scr[hh] + jnp.sum(pt, axis=0, keepdims=True)
                    acc_scr[hh] = alpha * acc_scr[hh] + jnp.dot(vtt, pt.astype(MXU), preferred_element_type=F32)
                    m_scr[hh] = m_new

            def loop_body(ki, carry):
                step(ki, False)
                return carry

            lax.fori_loop(0, qi, loop_body, 0)
            step(qi, True)
            o0 = acc_scr[0] / l_scr[0]
            o1 = acc_scr[1] / l_scr[1]
            row = lax.broadcasted_iota(jnp.int32, o0.shape, 0)
            o_ref[:, pl.ds(qs, t)] = jnp.where(row < HEAD_DIM, o0, o1)
            for hh in range(2):
                lse_ref[hh, :, pl.ds(qs, t)] = m_scr[hh] + jnp.log2(l_scr[hh])
            return 0

        lax.fori_loop(0, nq, qbody, 0)

    trn = pl.BlockSpec((LANES, s_len), lambda j: (j, 0))
    rowb = pl.BlockSpec((2, 1, s_len), lambda j: (j, 0, 0))
    ins = [q, k, vt]
    in_specs = [pl.BlockSpec((s_len, LANES), lambda j: (0, q_blk + j)),
                pl.BlockSpec((s_len, LANES), lambda j: (0, k_blk + j)), trn]
    if bias is not None:
        ins += list(bias)
        in_specs += [rowb, pl.BlockSpec((2, s_len, 1), lambda j: (j, 0, 0))]
    if rope is not None:
        ins += list(rope)
        in_specs += [pl.BlockSpec((s_len, LANES), lambda j: (0, j // 2)), _full((s_len, LANES))]
    return pl.pallas_call(
        kern, name=name, grid=(PAIRS,),
        out_shape=[_sds((PAIRS * LANES, s_len)), _sds((HEADS, 1, s_len))],
        in_specs=in_specs, out_specs=[trn, rowb],
        scratch_shapes=[pltpu.VMEM((2, 1, t), F32), pltpu.VMEM((2, 1, t), F32), pltpu.VMEM((2, LANES, t), F32)],
        compiler_params=_params("arbitrary"),
    )(*ins)


def _attn_bwd(q, q_blk, k, k_blk, v, v_blk, do, qt_, dot_, pack, ck_row, rope, chunk_mask, q_scale, k_scale, name):
    s_len = q.shape[0]
    t = min(ATTN_TILE, s_len // 2)
    nq = s_len // t
    has_bias = ck_row is not None
    kw = 2 * LANES if rope is not None else LANES

    def kern(*refs):
        q_ref, k_ref, v_ref, do_ref, qt_ref, dot_ref, pack_ref = refs[:7]
        pos = 7
        if has_bias:
            ck_ref = refs[pos]
            pos += 1
        if rope is not None:
            qr_ref, kr_ref, qrt_ref = refs[pos:pos + 3]
            pos += 3
        dq_ref, dkt_ref, dvt_ref = refs[pos:pos + 3]
        pos += 3
        if has_bias:
            dcq_ref, dck_ref = refs[pos:pos + 2]
            pos += 2
        if rope is not None:
            dqr_ref, dkrt_ref = refs[pos:pos + 2]
            pos += 2
        dq_scr, dcq_scr = refs[pos:]
        pj = pl.program_id(0)

        dkt_ref[...] = jnp.zeros(dkt_ref.shape, F32)
        dvt_ref[...] = jnp.zeros(dvt_ref.shape, F32)
        if has_bias:
            dck_ref[...] = jnp.zeros(dck_ref.shape, F32)

            @pl.when(pj == 0)
            def _():
                dcq_ref[...] = jnp.zeros(dcq_ref.shape, F32)
        if rope is not None:
            dkrt_ref[...] = jnp.zeros(dkrt_ref.shape, F32)

        def keep(x, hh, axis):
            return _keep_head(x, hh, axis, (pj % 2) * 2 + hh)

        def qbody(qi, _):
            qs = pl.multiple_of(qi * t, t)
            qt = q_ref[pl.ds(qs, t), :]
            qtt = qt_ref[:, pl.ds(qs, t)]
            if rope is not None:
                qt = jnp.concatenate([qt, qr_ref[pl.ds(qs, t), :]], axis=1)
                qtt = jnp.concatenate([qtt, qrt_ref[:, pl.ds(qs, t)]], axis=0)
            dot = do_ref[pl.ds(qs, t), :]
            dott = dot_ref[:, pl.ds(qs, t)]
            pk = pack_ref[pl.ds(qs, t), :]
            lane = lax.broadcasted_iota(jnp.int32, pk.shape, 1)
            qh = [keep(qt, hh, 1) for hh in range(2)]
            qth = [keep(qtt, hh, 0) for hh in range(2)]
            doh = [keep(dot, hh, 1) for hh in range(2)]
            doth = [keep(dott, hh, 0) for hh in range(2)]
            a_col = [jnp.sum(jnp.where(lane == 2 * pj + hh, pk, 0.0), axis=1, keepdims=True) for hh in range(2)]
            d_col = [jnp.sum(jnp.where(lane == HEADS + 2 * pj + hh, pk, 0.0), axis=1, keepdims=True)
                     for hh in range(2)]
            dq_scr[...] = jnp.zeros(dq_scr.shape, F32)
            if has_bias:
                dcq_scr[...] = jnp.zeros(dcq_scr.shape, F32)

            def step(ki, masked):
                ks = pl.multiple_of(ki * t, t)
                kt = k_ref[pl.ds(ks, t), :]
                if rope is not None:
                    kt = jnp.concatenate([kt, kr_ref[pl.ds(ks, t), :]], axis=1)
                vt = v_ref[pl.ds(ks, t), :]
                dq_acc = dq_scr[...]
                dk_acc = jnp.zeros((kw, t), F32)
                dv_acc = jnp.zeros((LANES, t), F32)
                for hh in range(2):
                    s = lax.dot_general(qh[hh], kt, _NT, preferred_element_type=F32) + a_col[hh]
                    if has_bias:
                        s = s - ck_ref[hh, :, pl.ds(ks, t)]
                    p = jnp.exp2(s)
                    if masked:
                        p = jnp.where(_tile_mask(t, chunk_mask, False), p, 0.0)
                    dpd = lax.dot_general(doh[hh], vt, _NT, preferred_element_type=F32)
                    ds = p * (dpd - d_col[hh])
                    dsb = ds.astype(MXU)
                    dv_acc = dv_acc + jnp.dot(doth[hh], p.astype(MXU), preferred_element_type=F32)
                    dk_acc = dk_acc + jnp.dot(qth[hh], dsb, preferred_element_type=F32)
                    dq_acc = dq_acc + jnp.dot(dsb, keep(kt, hh, 1), preferred_element_type=F32)
                    if has_bias:
                        dcq_scr[hh] += jnp.sum(ds, axis=1, keepdims=True)
                        dck_ref[hh, :, pl.ds(ks, t)] += -jnp.sum(ds, axis=0, keepdims=True)
                dq_scr[...] = dq_acc
                dvt_ref[:, pl.ds(ks, t)] += dv_acc
                dkt_ref[:, pl.ds(ks, t)] += dk_acc[:LANES, :] * k_scale
                if rope is not None:
                    dkrt_ref[0, :, pl.ds(ks, t)] += dk_acc[LANES:, :] * k_scale

            def loop_body(ki, carry):
                step(ki, False)
                return carry

            lax.fori_loop(0, qi, loop_body, 0)
            step(qi, True)
            dq_ref[pl.ds(qs, t), :] = (dq_scr[:, :LANES] * q_scale).astype(dq_ref.dtype)
            if rope is not None:
                dqr_ref[0, pl.ds(qs, t), :] = dq_scr[:, LANES:] * q_scale
            if has_bias:
                old = dcq_ref[pl.ds(qs, t), :]
                dcq_ref[pl.ds(qs, t), :] = jnp.where(lane == 2 * pj, dcq_scr[0],
                                                     jnp.where(lane == 2 * pj + 1, dcq_scr[1], old))
            return 0

        lax.fori_loop(0, nq, qbody, 0)

    def tok(blk):
        return pl.BlockSpec((s_len, LANES), lambda j: (0, blk + j))

    trn = pl.BlockSpec((LANES, s_len), lambda j: (j, 0))
    shared = _full((s_len, LANES))
    rowb = pl.BlockSpec((2, 1, s_len), lambda j: (j, 0, 0))
    ins = [q, k, v, do, qt_, dot_, pack]
    in_specs = [tok(q_blk), tok(k_blk), tok(v_blk), tok(0), trn, trn, shared]
    out_shape = [_sds((s_len, PAIRS * LANES), MXU), _sds((PAIRS * LANES, s_len)), _sds((PAIRS * LANES, s_len))]
    out_specs = [tok(0), trn, trn]
    if has_bias:
        ins.append(ck_row)
        in_specs.append(rowb)
        out_shape += [_sds((s_len, LANES)), _sds((HEADS, 1, s_len))]
        out_specs += [shared, rowb]
    if rope is not None:
        ins += list(rope)
        in_specs += [pl.BlockSpec((s_len, LANES), lambda j: (0, j // 2)), shared,
                     pl.BlockSpec((LANES, s_len), lambda j: (j // 2, 0))]
        out_shape += [_sds((PAIRS, s_len, LANES)), _sds((PAIRS, LANES, s_len))]
        out_specs += [pl.BlockSpec((1, s_len, LANES), lambda j: (j, 0, 0)),
                      pl.BlockSpec((1, LANES, s_len), lambda j: (j, 0, 0))]
    return pl.pallas_call(
        kern, name=name, grid=(PAIRS,), out_shape=out_shape, in_specs=in_specs, out_specs=out_specs,
        scratch_shapes=[pltpu.VMEM((t, kw), F32), pltpu.VMEM((2, t, 1), F32)],
        compiler_params=_params("arbitrary"),
    )(*ins)


def _attention_fwd(q, q_blk, k, k_blk, v, v_blk, bias, rope, chunk_mask, name):
    s_len = q.shape[0]
    t = min(ATTN_TILE, s_len // 2)
    nq = s_len // t

    def kern(*refs):
        q_ref, k_ref, v_ref = refs[:3]
        pos = 3
        if bias is not None:
            cq_ref, ck_ref = refs[pos:pos + 2]
            pos += 2
        if rope is not None:
            qr_ref, kr_ref = refs[pos:pos + 2]
            pos += 2
        o_ref, lse_ref, vt_scr, m_scr, l_scr, acc_scr = refs[pos:]
        pj = pl.program_id(0)
        for i in range(nq):
            vt_scr[:, i * t:(i + 1) * t] = v_ref[i * t:(i + 1) * t, :].T

        def qbody(qi, _):
            qs = pl.multiple_of(qi * t, t)
            qt = q_ref[pl.ds(qs, t), :]
            if rope is not None:
                qt = jnp.concatenate([qt, qr_ref[pl.ds(qs, t), :]], axis=1)
            qh = [_keep_head(qt, hh, 1, (pj % 2) * 2 + hh) for hh in range(2)]
            m_scr[...] = jnp.full(m_scr.shape, -jnp.inf, F32)
            l_scr[...] = jnp.zeros(l_scr.shape, F32)
            acc_scr[...] = jnp.zeros(acc_scr.shape, F32)

            def step(ki, masked):
                ks = pl.multiple_of(ki * t, t)
                kt = k_ref[pl.ds(ks, t), :]
                if rope is not None:
                    kt = jnp.concatenate([kt, kr_ref[pl.ds(ks, t), :]], axis=1)
                vtt = vt_scr[:, pl.ds(ks, t)]
                for hh in range(2):
                    st = lax.dot_general(kt, qh[hh], _NT, preferred_element_type=F32)
                    if bias is not None:
                        st = st + cq_ref[hh, :, pl.ds(qs, t)] - ck_ref[hh, pl.ds(ks, t), :]
                    if masked:
                        st = jnp.where(_tile_mask(t, chunk_mask, True), st, -jnp.inf)
                    m_old = m_scr[hh]
                    m_new = jnp.maximum(m_old, jnp.max(st, axis=0, keepdims=True))
                    pt = jnp.exp2(st - m_new)
                    alpha = jnp.exp2(m_old - m_new)
                    l_scr[hh] = alpha * l_scr[hh] + jnp.sum(pt, axis=0, keepdims=True)
                    acc_scr[hh] = alpha * acc_scr[hh] + jnp.dot(vtt, pt.astype(MXU), preferred_element_type=F32)
                    m_scr[hh] = m_new

            def loop_body(ki, carry):
                step(ki, False)
                return carry

            lax.fori_loop(0, qi, loop_body, 0)
            step(qi, True)
            o0 = acc_scr[0] / l_scr[0]
            o1 = acc_scr[1] / l_scr[1]
            row = lax.broadcasted_iota(jnp.int32, o0.shape, 0)
            o_ref[pl.ds(qs, t), :] = jnp.where(row < HEAD_DIM, o0, o1).T
            for hh in range(2):
                lse_ref[hh, :, pl.ds(qs, t)] = m_scr[hh] + jnp.log2(l_scr[hh])
            return 0

        lax.fori_loop(0, nq, qbody, 0)

    def tok(blk):
        return pl.BlockSpec((s_len, LANES), lambda j: (0, blk + j))

    rowb = pl.BlockSpec((2, 1, s_len), lambda j: (j, 0, 0))
    ins = [q, k, v]
    in_specs = [tok(q_blk), tok(k_blk), tok(v_blk)]
    if bias is not None:
        ins += list(bias)
        in_specs += [rowb, pl.BlockSpec((2, s_len, 1), lambda j: (j, 0, 0))]
    if rope is not None:
        ins += list(rope)
        in_specs += [pl.BlockSpec((s_len, LANES), lambda j: (0, j // 2)), _full((s_len, LANES))]
    return pl.pallas_call(
        kern, name=name, grid=(PAIRS,),
        out_shape=[_sds((s_len, PAIRS * LANES)), _sds((HEADS, 1, s_len))],
        in_specs=in_specs, out_specs=[tok(0), rowb],
        scratch_shapes=[pltpu.VMEM((LANES, s_len), v.dtype), pltpu.VMEM((2, 1, t), F32), pltpu.VMEM((2, 1, t), F32),
                        pltpu.VMEM((2, LANES, t), F32)],
        compiler_params=_params("arbitrary"),
    )(*ins)


def _attention_bwd(q, q_blk, k, k_blk, v, v_blk, do, pack, ck_row, rope, chunk_mask, q_scale, k_scale, name):
    s_len = q.shape[0]
    t = min(ATTN_TILE, s_len // 2)
    nq = s_len // t
    has_bias = ck_row is not None
    kw = 2 * LANES if rope is not None else LANES

    def kern(*refs):
        q_ref, k_ref, v_ref, do_ref, pack_ref = refs[:5]
        pos = 5
        if has_bias:
            ck_ref = refs[pos]
            pos += 1
        if rope is not None:
            qr_ref, kr_ref = refs[pos:pos + 2]
            pos += 2
        dq_ref, dk_ref, dv_ref = refs[pos:pos + 3]
        pos += 3
        if has_bias:
            dcq_ref, dck_ref = refs[pos:pos + 2]
            pos += 2
        if rope is not None:
            dqr_ref, dkr_ref = refs[pos:pos + 2]
            pos += 2
        qt_scr, dot_scr, dkt_scr, dvt_scr, dq_scr, dcq_scr = refs[pos:]
        pj = pl.program_id(0)

        for i in range(nq):
            sl = slice(i * t, (i + 1) * t)
            qt_scr[:LANES, sl] = q_ref[sl, :].T
            dot_scr[:, sl] = do_ref[sl, :].T
            if rope is not None:
                qt_scr[LANES:, sl] = qr_ref[sl, :].T
        dkt_scr[...] = jnp.zeros(dkt_scr.shape, F32)
        dvt_scr[...] = jnp.zeros(dvt_scr.shape, F32)
        if has_bias:
            dck_ref[...] = jnp.zeros(dck_ref.shape, F32)

            @pl.when(pj == 0)
            def _():
                dcq_ref[...] = jnp.zeros(dcq_ref.shape, F32)

        def keep(x, hh, axis):
            return _keep_head(x, hh, axis, (pj % 2) * 2 + hh)

        def qbody(qi, _):
            qs = pl.multiple_of(qi * t, t)
            qt = q_ref[pl.ds(qs, t), :]
            if rope is not None:
                qt = jnp.concatenate([qt, qr_ref[pl.ds(qs, t), :]], axis=1)
            qtt = qt_scr[:, pl.ds(qs, t)]
            dot = do_ref[pl.ds(qs, t), :]
            dott = dot_scr[:, pl.ds(qs, t)]
            pk = pack_ref[pl.ds(qs, t), :]
            lane = lax.broadcasted_iota(jnp.int32, pk.shape, 1)
            qh = [keep(qt, hh, 1) for hh in range(2)]
            qth = [keep(qtt, hh, 0) for hh in range(2)]
            doh = [keep(dot, hh, 1) for hh in range(2)]
            doth = [keep(dott, hh, 0) for hh in range(2)]
            a_col = [jnp.sum(jnp.where(lane == 2 * pj + hh, pk, 0.0), axis=1, keepdims=True) for hh in range(2)]
            d_col = [jnp.sum(jnp.where(lane == HEADS + 2 * pj + hh, pk, 0.0), axis=1, keepdims=True)
                     for hh in range(2)]
            dq_scr[...] = jnp.zeros(dq_scr.shape, F32)
            if has_bias:
                dcq_scr[...] = jnp.zeros(dcq_scr.shape, F32)

            def step(ki, masked):
                ks = pl.multiple_of(ki * t, t)
                kt = k_ref[pl.ds(ks, t), :]
                if rope is not None:
                    kt = jnp.concatenate([kt, kr_ref[pl.ds(ks, t), :]], axis=1)
                vt = v_ref[pl.ds(ks, t), :]
                dq_acc = dq_scr[...]
                dk_acc = jnp.zeros((kw, t), F32)
                dv_acc = jnp.zeros((LANES, t), F32)
                for hh in range(2):
                    s = lax.dot_general(qh[hh], kt, _NT, preferred_element_type=F32) + a_col[hh]
                    if has_bias:
                        s = s - ck_ref[hh, :, pl.ds(ks, t)]
                    p = jnp.exp2(s)
                    if masked:
                        p = jnp.where(_tile_mask(t, chunk_mask, False), p, 0.0)
                    dpd = lax.dot_general(doh[hh], vt, _NT, preferred_element_type=F32)
                    ds = p * (dpd - d_col[hh])
                    dsb = ds.astype(MXU)
                    dv_acc = dv_acc + jnp.dot(doth[hh], p.astype(MXU), preferred_element_type=F32)
                    dk_acc = dk_acc + jnp.dot(qth[hh], dsb, preferred_element_type=F32)
                    dq_acc = dq_acc + jnp.dot(dsb, keep(kt, hh, 1), preferred_element_type=F32)
                    if has_bias:
                        dcq_scr[hh] += jnp.sum(ds, axis=1, keepdims=True)
                        dck_ref[hh, :, pl.ds(ks, t)] += -jnp.sum(ds, axis=0, keepdims=True)
                dq_scr[...] = dq_acc
                dvt_scr[:, pl.ds(ks, t)] += dv_acc
                dkt_scr[:, pl.ds(ks, t)] += dk_acc

            def loop_body(ki, carry):
                step(ki, False)
                return carry

            lax.fori_loop(0, qi, loop_body, 0)
            step(qi, True)
            dq_ref[pl.ds(qs, t), :] = (dq_scr[:, :LANES] * q_scale).astype(dq_ref.dtype)
            if rope is not None:
                dqr_ref[0, pl.ds(qs, t), :] = dq_scr[:, LANES:] * q_scale
            if has_bias:
                old = dcq_ref[pl.ds(qs, t), :]
                dcq_ref[pl.ds(qs, t), :] = jnp.where(lane == 2 * pj, dcq_scr[0],
                                                     jnp.where(lane == 2 * pj + 1, dcq_scr[1], old))
            return 0

        lax.fori_loop(0, nq, qbody, 0)
        for i in range(nq):
            sl = slice(i * t, (i + 1) * t)
            dk_ref[sl, :] = (dkt_scr[:LANES, sl].T * k_scale).astype(dk_ref.dtype)
            dv_ref[sl, :] = dvt_scr[:, sl].T.astype(dv_ref.dtype)
            if rope is not None:
                dkr_ref[0, sl, :] = dkt_scr[LANES:, sl].T * k_scale

    def tok(blk):
        return pl.BlockSpec((s_len, LANES), lambda j: (0, blk + j))

    shared = _full((s_len, LANES))
    rowb = pl.BlockSpec((2, 1, s_len), lambda j: (j, 0, 0))
    slab = pl.BlockSpec((1, s_len, LANES), lambda j: (j, 0, 0))
    ins = [q, k, v, do, pack]
    in_specs = [tok(q_blk), tok(k_blk), tok(v_blk), tok(0), shared]
    out_shape = [_sds((s_len, PAIRS * LANES), MXU)] * 3
    out_specs = [tok(0)] * 3
    if has_bias:
        ins.append(ck_row)
        in_specs.append(rowb)
        out_shape += [_sds((s_len, LANES)), _sds((HEADS, 1, s_len))]
        out_specs += [shared, rowb]
    if rope is not None:
        ins += list(rope)
        in_specs += [pl.BlockSpec((s_len, LANES), lambda j: (0, j // 2)), shared]
        out_shape += [_sds((PAIRS, s_len, LANES))] * 2
        out_specs += [slab, slab]
    return pl.pallas_call(
        kern, name=name, grid=(PAIRS,), out_shape=out_shape, in_specs=in_specs, out_specs=out_specs,
        scratch_shapes=[pltpu.VMEM((kw, s_len), q.dtype), pltpu.VMEM((LANES, s_len), do.dtype),
                        pltpu.VMEM((kw, s_len), F32), pltpu.VMEM((LANES, s_len), F32),
                        pltpu.VMEM((t, kw), F32), pltpu.VMEM((2, t, 1), F32)],
        compiler_params=_params("arbitrary"),
    )(*ins)


def _silu(a):
    return a * jax.nn.sigmoid(a)


def _k_out(of, om, z, x, gate, wout, tm=256):
    s_len, d = x.shape

    def kern(of_ref, om_ref, fg_ref, mg_ref, x_ref, gate_ref, w_ref, xo_ref, y_ref, u_ref):
        u_ref[:, :GROUP_W] = (of_ref[...] * _silu(fg_ref[...])).astype(MXU)
        u_ref[:, GROUP_W:] = (om_ref[...] * _silu(mg_ref[...])).astype(MXU)
        y = jnp.dot(u_ref[...], w_ref[...], preferred_element_type=F32)
        y_ref[...] = y
        xo_ref[...] = x_ref[...] + gate_ref[...] * y

    return pl.pallas_call(
        kern, name="k_out", grid=(s_len // tm,),
        out_shape=[_sds((s_len, d)), _sds((s_len, d)), _sds((s_len, 2 * GROUP_W), MXU)],
        in_specs=[_rows(tm, GROUP_W), _rows(tm, GROUP_W), _rows(tm, GROUP_W, Z_FG // GROUP_W),
                  _rows(tm, GROUP_W, Z_MG // GROUP_W), _rows(tm, d), _full((1, d)), _full((2 * GROUP_W, d))],
        out_specs=[_rows(tm, d), _rows(tm, d), _rows(tm, 2 * GROUP_W)],
        compiler_params=_params("arbitrary"),
    )(of, om, z, z, x, gate, wout)


def _k_loss(x, gf, tgt, tm=256):
    s_len, d = x.shape

    def kern(x_ref, g_ref, t_ref, loss_ref, dx_ref, dg_ref):
        i = pl.program_id(0)
        xv = x_ref[...]
        r = lax.rsqrt(jnp.mean(xv * xv, axis=-1, keepdims=True) + EPS)
        xh = xv * r
        diff = xh * g_ref[...] - t_ref[...]
        part = 0.5 * jnp.sum(jnp.mean(diff * diff, axis=-1, keepdims=True))
        dout = diff * (1.0 / d)
        dxh = dout * g_ref[...]
        dx_ref[...] = r * (dxh - xh * jnp.mean(dxh * xh, axis=-1, keepdims=True))

        @pl.when(i == 0)
        def _():
            loss_ref[...] = jnp.zeros_like(loss_ref)
            dg_ref[...] = jnp.zeros_like(dg_ref)

        loss_ref[...] += jnp.full(loss_ref.shape, part, F32)
        dg_ref[...] += jnp.sum(dout * xh, axis=0, keepdims=True)

    return pl.pallas_call(
        kern, name="k_loss", grid=(s_len // tm,),
        out_shape=[_sds((1, LANES)), _sds((s_len, d)), _sds((1, d))],
        in_specs=[_rows(tm, d), _full((1, d)), _rows(tm, d)],
        out_specs=[_full((1, LANES)), _rows(tm, d), _full((1, d))],
        compiler_params=_params("arbitrary"),
    )(x, gf, tgt)


def _kb_out(dxo, y, gate, wout, of, om, z, tm=256):
    s_len, d = dxo.shape

    def kern(dxo_ref, y_ref, gate_ref, wt_ref, of_ref, om_ref, fg_ref, mg_ref,
             dy_ref, dof_ref, dom_ref, dfg_ref, dmg_ref, dlf_ref, dlm_ref, dgate_ref):
        i = pl.program_id(0)
        dxv = dxo_ref[...]

        @pl.when(i == 0)
        def _():
            dgate_ref[...] = jnp.zeros_like(dgate_ref)

        dgate_ref[...] += jnp.sum(dxv * y_ref[...], axis=0, keepdims=True)
        dyb = (dxv * gate_ref[...]).astype(MXU)
        dy_ref[...] = dyb
        du = lax.dot_general(dyb, wt_ref[...], _NT, preferred_element_type=F32)
        head_of = (lax.broadcasted_iota(jnp.int32, (GROUP_W, LANES), 0) // HEAD_DIM
                   == lax.broadcasted_iota(jnp.int32, (GROUP_W, LANES), 1)).astype(F32)
        for du_g, o_ref, g_ref, do_ref, dg_ref, dl_ref in (
                (du[:, :GROUP_W], of_ref, fg_ref, dof_ref, dfg_ref, dlf_ref),
                (du[:, GROUP_W:], om_ref, mg_ref, dom_ref, dmg_ref, dlm_ref)):
            a = g_ref[...]
            sg = jax.nn.sigmoid(a)
            ov = o_ref[...]
            dov = du_g * (a * sg)
            do_ref[...] = dov.astype(MXU)
            dg_ref[...] = (du_g * ov * (sg * (1.0 + a * (1.0 - sg)))).astype(MXU)
            dl_ref[...] = jnp.dot(dov * ov, head_of, precision=lax.Precision.HIGHEST, preferred_element_type=F32)

    return pl.pallas_call(
        kern, name="kb_out", grid=(s_len // tm,),
        out_shape=[_sds((s_len, d), MXU), _sds((s_len, GROUP_W), MXU), _sds((s_len, GROUP_W), MXU),
                   _sds((s_len, GROUP_W), MXU), _sds((s_len, GROUP_W), MXU), _sds((s_len, LANES)),
                   _sds((s_len, LANES)), _sds((1, d))],
        in_specs=[_rows(tm, d), _rows(tm, d), _full((1, d)), _full((2 * GROUP_W, d)), _rows(tm, GROUP_W),
                  _rows(tm, GROUP_W), _rows(tm, GROUP_W, Z_FG // GROUP_W), _rows(tm, GROUP_W, Z_MG // GROUP_W)],
        out_specs=[_rows(tm, d), _rows(tm, GROUP_W), _rows(tm, GROUP_W), _rows(tm, GROUP_W),
                   _rows(tm, GROUP_W), _rows(tm, LANES), _rows(tm, LANES), _full((1, d))],
        compiler_params=_params("arbitrary"),
    )(dxo, y, gate, wout, of, om, z, z)


def _kb_prep(dqn, dqr, dkn, dv, dkr4, dff, z, cos, sin, gq, gkv, wuq_t, wukv_t, tm=512):
    s_len = z.shape[0]
    qw = GROUP_W + ROPE_W
    tail = Q_LORA + KV_LORA + LANES

    def kern(dqn_ref, dqr_ref, dkn_ref, dv_ref, dkr_ref, dff_ref, ql_ref, kvl_ref, cos_ref, sin_ref,
             gq_ref, gkv_ref, wuqt_ref, wukvt_ref, dq_ref, dz_ref, dgq_ref, dgkv_ref):
        i = pl.program_id(0)

        @pl.when(i == 0)
        def _():
            dgq_ref[...] = jnp.zeros_like(dgq_ref)
            dgkv_ref[...] = jnp.zeros_like(dgkv_ref)

        cs, sn = cos_ref[...], sin_ref[...]
        dq_ref[:, :GROUP_W] = dqn_ref[...]
        for half in range(ROPE_W // LANES):
            sl = slice(half * LANES, (half + 1) * LANES)
            dq_ref[:, GROUP_W + half * LANES:GROUP_W + (half + 1) * LANES] = _rope_bwd(dqr_ref[:, sl], cs, sn).astype(MXU)
        dqn = lax.dot_general(dq_ref[...], wuqt_ref[...], _NT, preferred_element_type=F32)
        ql = ql_ref[...]
        rq = lax.rsqrt(jnp.mean(ql * ql, axis=-1, keepdims=True) + EPS)
        qh = ql * rq
        dgq_ref[...] += jnp.sum(dqn * qh, axis=0, keepdims=True)
        dqh = dqn * gq_ref[...]
        dz_ref[:, :Q_LORA] = (rq * (dqh - qh * jnp.mean(dqh * qh, axis=-1, keepdims=True))).astype(MXU)

        dkvn = (lax.dot_general(dkn_ref[...], wukvt_ref[:, :GROUP_W], _NT, preferred_element_type=F32)
                + lax.dot_general(dv_ref[...], wukvt_ref[:, GROUP_W:], _NT, preferred_element_type=F32))
        kvl = kvl_ref[...]
        rk = lax.rsqrt(jnp.mean(kvl * kvl, axis=-1, keepdims=True) + EPS)
        kh = kvl * rk
        dgkv_ref[...] += jnp.sum(dkvn * kh, axis=0, keepdims=True)
        dkh = dkvn * gkv_ref[...]
        dz_ref[:, Q_LORA:Q_LORA + KV_LORA] = (
            rk * (dkh - kh * jnp.mean(dkh * kh, axis=-1, keepdims=True))).astype(MXU)

        g4 = dkr_ref[...]
        g = g4
        for rep in range(1, LANES // ROPE):
            g = g + pltpu.roll(g4, rep * ROPE, 1)
        lane = lax.broadcasted_iota(jnp.int32, g.shape, 1)
        dmisc = jnp.where(lane < ROPE, _rope_bwd(g, cs, sn), 0.0) + dff_ref[...]
        dz_ref[:, Q_LORA + KV_LORA:] = dmisc.astype(MXU)

    return pl.pallas_call(
        kern, name="kb_prep", grid=(s_len // tm,),
        out_shape=[_sds((s_len, qw), MXU), _sds((s_len, tail), MXU), _sds((1, Q_LORA)), _sds((1, KV_LORA))],
        in_specs=[_rows(tm, GROUP_W), _rows(tm, ROPE_W), _rows(tm, GROUP_W), _rows(tm, GROUP_W), _rows(tm, LANES),
                  _rows(tm, LANES), _rows(tm, Q_LORA, Z_QL // Q_LORA), _rows(tm, KV_LORA, Z_KV // KV_LORA),
                  _rows(tm, LANES), _rows(tm, LANES), _full((1, Q_LORA)), _full((1, KV_LORA)),
                  _full((Q_LORA, qw)), _full((KV_LORA, 2 * GROUP_W))],
        out_specs=[_rows(tm, qw), _rows(tm, tail), _full((1, Q_LORA)), _full((1, KV_LORA))],
        compiler_params=_params("arbitrary"),
    )(dqn, dqr, dkn, dv, dkr4, dff, z, z, cos, sin, gq, gkv, wuq_t, wukv_t)


def _kb_in(dz_pieces, w, x, g, mod3, dxo, tm=256):
    s_len, d = x.shape
    widths = [p.shape[1] for p in dz_pieces]
    n_p = len(widths)

    def kern(*refs):
        dz_refs = refs[:n_p]
        w_ref, x_ref, g_ref, mod_ref, dxo_ref, dx_ref, acc_ref = refs[n_p:]
        i = pl.program_id(0)

        @pl.when(i == 0)
        def _():
            acc_ref[...] = jnp.zeros_like(acc_ref)

        dh = jnp.zeros((tm, d), F32)
        lo = 0
        for p_ref, wd in zip(dz_refs, widths):
            dh = dh + lax.dot_general(p_ref[...], w_ref[:, lo:lo + wd], _NT, preferred_element_type=F32)
            lo += wd
        xv = x_ref[...]
        r = lax.rsqrt(jnp.mean(xv * xv, axis=-1, keepdims=True) + EPS)
        xh = xv * r
        xn = xh * g_ref[...]
        dxn = dh * (1.0 + mod_ref[1:2, :])
        acc_ref[0:1, :] += jnp.sum(dh, axis=0, keepdims=True)
        acc_ref[1:2, :] += jnp.sum(dh * xn, axis=0, keepdims=True)
        acc_ref[2:3, :] += jnp.sum(dxn * xh, axis=0, keepdims=True)
        dxh = dxn * g_ref[...]
        dx_ref[...] = dxo_ref[...] + r * (dxh - xh * jnp.mean(dxh * xh, axis=-1, keepdims=True))

    return pl.pallas_call(
        kern, name="kb_in", grid=(s_len // tm,),
        out_shape=[_sds((s_len, d)), _sds((3, d))],
        in_specs=[_rows(tm, wd) for wd in widths] + [_full((d, Z_W)), _rows(tm, d), _full((1, d)), _full((3, d)),
                                                     _rows(tm, d)],
        out_specs=[_rows(tm, d), _full((3, d))],
        compiler_params=_params("arbitrary"),
    )(*dz_pieces, w, x, g, mod3, dxo)


def _weight_grad(a, pieces, name, tk=512):
    s_len, m = a.shape
    widths = [p.shape[1] for p in pieces]
    n = sum(widths)
    tk = min(tk, s_len)

    def kern(a_ref, *refs):
        o_ref = refs[-1]

        @pl.when(pl.program_id(0) == 0)
        def _():
            o_ref[...] = jnp.zeros_like(o_ref)

        at = a_ref[...]
        lo = 0
        for p_ref, w in zip(refs[:-1], widths):
            o_ref[:, lo:lo + w] += lax.dot_general(at, p_ref[...], _TN, preferred_element_type=F32)
            lo += w

    return pl.pallas_call(
        kern, name=name, grid=(s_len // tk,), out_shape=_sds((m, n)),
        in_specs=[_rows(tk, m)] + [_rows(tk, w) for w in widths],
        out_specs=_full((m, n)),
        compiler_params=_params("arbitrary"),
    )(a, *pieces)


def _adamw(gslots, w, m, v, name):
    n, r, c = gslots.shape
    tm = r
    for cand in (256, 128, 64, 32, 16, 8):
        if r % cand == 0:
            tm = cand
            break

    def kern(g_ref, w_ref, m_ref, v_ref, go_ref, d_ref, mo_ref, vo_ref):
        g = g_ref[0]
        for s in range(1, n):
            g = g + g_ref[s]
        m_new = ADAM_B1 * m_ref[...] + (1.0 - ADAM_B1) * g
        v_new = ADAM_B2 * v_ref[...] + (1.0 - ADAM_B2) * (g * g)
        m_hat = m_new / (1.0 - ADAM_B1 ** ADAM_STEP)
        v_hat = v_new / (1.0 - ADAM_B2 ** ADAM_STEP)
        go_ref[...] = g
        mo_ref[...] = m_new
        vo_ref[...] = v_new
        d_ref[...] = -ADAM_LR * (m_hat / (jnp.sqrt(v_hat) + ADAM_EPS) + ADAM_WD * w_ref[...])

    row = pl.BlockSpec((tm, c), lambda i: (i, 0))
    return pl.pallas_call(
        kern, name=name, grid=(r // tm,), out_shape=[_sds((r, c))] * 4,
        in_specs=[pl.BlockSpec((n, tm, c), lambda i: (0, i, 0)), row, row, row],
        out_specs=[row] * 4,
        compiler_params=_params("arbitrary"),
    )(gslots, w, m, v)


def _perm_w_in(w):
    pad = jnp.zeros(w.shape[:-1] + (Z_W - Z_MISC - ROPE - HEADS,), w.dtype)
    return jnp.concatenate([w[..., 0:1536], w[..., 1544:2056], w[..., 2472:2984], w[..., 2056:2312],
                            w[..., 2312:2440], w[..., 2440:2472], w[..., 1536:1544], pad], axis=-1)


def _unperm_w_in(g):
    ff0 = Z_MISC + MISC_FF
    return jnp.concatenate([g[..., 0:1536], g[..., ff0:ff0 + HEADS], g[..., Z_FG:Z_FG + GROUP_W],
                            g[..., Z_QL:Z_QL + Q_LORA], g[..., Z_KV:Z_KV + KV_LORA],
                            g[..., Z_MISC:Z_MISC + ROPE], g[..., Z_MG:Z_MG + GROUP_W]], axis=-1)


def _perm_w_uq(w):
    lead = w.shape[:-1]
    wh = w.reshape(lead + (HEADS, NOPE + ROPE))
    return jnp.concatenate([wh[..., :NOPE].reshape(lead + (GROUP_W,)),
                            wh[..., NOPE:].reshape(lead + (ROPE_W,))], axis=-1)


def _unperm_w_uq(g):
    lead = g.shape[:-1]
    parts = [g[..., :GROUP_W].reshape(lead + (HEADS, NOPE)), g[..., GROUP_W:].reshape(lead + (HEADS, ROPE))]
    return jnp.concatenate(parts, axis=-1).reshape(lead + (HEADS * (NOPE + ROPE),))


def _perm_w_ukv(w):
    lead = w.shape[:-1]
    wh = w.reshape(lead + (HEADS, 2 * HEAD_DIM))
    return jnp.concatenate([wh[..., :NOPE].reshape(lead + (GROUP_W,)),
                            wh[..., NOPE:].reshape(lead + (GROUP_W,))], axis=-1)


def _unperm_w_ukv(g):
    lead = g.shape[:-1]
    parts = [g[..., :GROUP_W].reshape(lead + (HEADS, NOPE)), g[..., GROUP_W:].reshape(lead + (HEADS, HEAD_DIM))]
    return jnp.concatenate(parts, axis=-1).reshape(lead + (2 * GROUP_W,))


def _rope_tables(positions):
    inv_freq = 1.0 / (ROPE_THETA ** (jnp.arange(0, ROPE, 2, dtype=F32) / ROPE))
    ang = positions.astype(F32)[:, None] * inv_freq
    cos, sin = jnp.cos(ang), jnp.sin(ang)
    reps = LANES // ROPE
    return jnp.tile(jnp.concatenate([cos, cos], axis=1), (1, reps)), jnp.tile(jnp.concatenate([-sin, sin], axis=1), (1, reps))


def _local_step(x, mod, positions, loss_target, norm_g, b_f, q_norm_g, kv_norm_g, final_g, w_in, w_uq, w_ukv, w_out):
    n_l = norm_g.shape[0]
    s_len, d = x.shape
    cos, sin = _rope_tables(positions)
    qb, kb, vb = Z_FQ // LANES, Z_FK // LANES, Z_FV // LANES

    def pack_rows(a_rows, delta):
        return jnp.concatenate([a_rows.T, delta[:, :HEADS], jnp.zeros((s_len, LANES - 2 * HEADS), F32)], axis=1)

    saved = []
    for l in range(n_l):
        mod3 = mod[l].reshape(3, d)
        z, h, qkv = _k_in(x, norm_g[l][None], mod3, w_in[l])
        fft = z[:, Z_MISC + MISC_FF:Z_MISC + MISC_FF + HEADS].T
        bf = b_f[l][:, None]
        c2 = _k_cum(fft, bf) * LOG2E
        of, lse_f = _attention_fwd(qkv, qb, qkv, kb, qkv, vb, (c2[:, None, :], c2[:, :, None]), None, False,
                                   "fox_fwd")
        mq, mqr, mk, mv, kr4, qn, kvn = _k_prep(z, cos, sin, q_norm_g[l][None], kv_norm_g[l][None], w_uq[l], w_ukv[l])
        om, lse_m = _attention_fwd(mq, 0, mk, 0, mv, 0, None, (mqr, kr4), True, "mla_fwd")
        x_new, y, u = _k_out(of, om, z, x, mod3[2:3], w_out[l])
        saved.append((x, z, h, qkv, fft, bf, c2, lse_f, mq, mqr, mk, mv, kr4, lse_m, of, om, qn, kvn, y, u, mod3))
        x = x_new

    loss_row, dx, dfinal = _k_loss(x, final_g[None], loss_target)

    grads = {k: [] for k in ("norm_g", "mod", "w_in", "b_f", "q_norm_g", "w_uq", "kv_norm_g", "w_ukv", "w_out")}
    for l in range(n_l - 1, -1, -1):
        (x_l, z, h, qkv, fft, bf, c2, lse_f, mq, mqr, mk, mv, kr4, lse_m, of, om, qn, kvn, y, u, mod3) = saved[l]
        dyb, dof, dom, dfg, dmg, dlt_f, dlt_m, dgate = _kb_out(dx, y, mod3[2:3], w_out[l], of, om, z)
        grads["w_out"].append(_weight_grad(u, [dyb], "dw_out"))

        dfq, dfk, dfv, dcq, dck = _attention_bwd(
            qkv, qb, qkv, kb, qkv, vb, dof, pack_rows(c2 - lse_f.reshape(HEADS, s_len), dlt_f), c2[:, None, :], None,
            False, FOX_SCALE, 1.0 / LOG2E, "fox_bwd")
        dfft, dbf = _k_cum_bwd(dcq[:, :HEADS].T + dck.reshape(HEADS, s_len), fft, bf)
        grads["b_f"].append(dbf[:, 0])

        dmq, dkn, dmv, dqr4, dkr4s = _attention_bwd(
            mq, 0, mk, 0, mv, 0, dom, pack_rows(-lse_m.reshape(HEADS, s_len), dlt_m), None, (mqr, kr4), True,
            MLA_SCALE, 1.0 / LOG2E, "mla_bwd")
        dqr = jnp.concatenate([dqr4[0] + dqr4[1], dqr4[2] + dqr4[3]], axis=1)
        dkr4 = dkr4s[0] + dkr4s[1] + dkr4s[2] + dkr4s[3]
        dff = jnp.pad(dfft.T, ((0, 0), (MISC_FF, LANES - MISC_FF - HEADS)))
        dq_b, dz_tail, dgq, dgkv = _kb_prep(dmq, dqr, dkn, dmv, dkr4, dff, z, cos, sin, q_norm_g[l][None],
                                            kv_norm_g[l][None], w_uq[l], w_ukv[l])
        grads["q_norm_g"].append(dgq[0])
        grads["kv_norm_g"].append(dgkv[0])
        grads["w_uq"].append(_weight_grad(qn, [dq_b], "dw_uq"))
        grads["w_ukv"].append(_weight_grad(kvn, [dkn, dmv], "dw_ukv"))
        dz = [dfq, dfk, dfv, dfg, dmg, dz_tail]
        grads["w_in"].append(_weight_grad(h, dz, "dw_in"))
        dx, acc3 = _kb_in(dz, w_in[l], x_l, norm_g[l][None], mod3, dx)
        grads["norm_g"].append(acc3[2])
        grads["mod"].append(jnp.concatenate([acc3[0], acc3[1], dgate[0]]))
    grads = {k: jnp.stack(v[::-1]) for k, v in grads.items()}
    grads["final_g"] = dfinal[0]
    return loss_row[0, 0], dx, grads


def _pack_small(parts, total):
    flat = jnp.concatenate([p.reshape(-1) for p in parts])
    return jnp.pad(flat, (0, total - flat.shape[0])).reshape(total // LANES, LANES)


def kernel(x, c, positions, norm_g, w_ada, b_ada, w_in, b_f, q_norm_g, w_uq, kv_norm_g, w_ukv, w_out, final_g, loss_target, m_norm_g, m_w_ada, m_b_ada, m_w_in, m_b_f, m_q_norm_g, m_w_uq, m_kv_norm_g, m_w_ukv, m_w_out, m_final_g, v_norm_g, v_w_ada, v_b_ada, v_w_in, v_b_f, v_q_norm_g, v_w_uq, v_kv_norm_g, v_w_ukv, v_w_out, v_final_g):
    n_l, d = norm_g.shape
    me = 4 * lax.axis_index("x") + 2 * lax.axis_index("y") + lax.axis_index("c")
    ada_c = w_ada.shape[2]
    in_c = w_in.shape[2]
    uq_c = w_uq.shape[2]
    ukv_c = w_ukv.shape[2]
    out_r = w_out.shape[1]

    cact = jnp.broadcast_to(jax.nn.silu(c), (N_DEV, d))
    g_in, g_uq, g_ukv, g_out, g_cact = _exchange(
        [w_in.reshape(n_l * d, in_c).astype(MXU), w_uq.reshape(n_l * Q_LORA, uq_c).astype(MXU),
         w_ukv.reshape(n_l * KV_LORA, ukv_c).astype(MXU), w_out.reshape(n_l * out_r, d).astype(MXU), cact],
        [True] * 5, "gather_weights")
    w_in_f = _perm_w_in(g_in.reshape(N_DEV, n_l, d, in_c).transpose(1, 2, 0, 3).reshape(n_l, d, N_DEV * in_c))
    w_uq_f = _perm_w_uq(g_uq.reshape(N_DEV, n_l, Q_LORA, uq_c).transpose(1, 2, 0, 3).reshape(n_l, Q_LORA, -1))
    w_ukv_f = _perm_w_ukv(g_ukv.reshape(N_DEV, n_l, KV_LORA, ukv_c).transpose(1, 2, 0, 3).reshape(n_l, KV_LORA, -1))
    w_out_f = g_out.reshape(N_DEV, n_l, out_r, d).transpose(1, 0, 2, 3).reshape(n_l, N_DEV * out_r, d)
    cact_all = g_cact[:, 0, :]

    b_cols = lax.dynamic_slice_in_dim(b_ada, me * ada_c, ada_c, axis=1)[:, None, :]
    modpart = _modpart(cact_all, w_ada, b_cols)
    mod_send = jnp.pad(modpart.transpose(1, 0, 2), ((0, 0), (0, 8 - n_l), (0, 0)))
    (mod_recv,) = _exchange([mod_send], [False], "scatter_mod")
    mod = mod_recv.transpose(1, 0, 2).reshape(8, N_DEV * ada_c)[:n_l]

    loss, dx, gr = _local_step(x[0], mod, positions[0], loss_target[0], norm_g, b_f, q_norm_g, kv_norm_g,
                               final_g, w_in_f, w_uq_f, w_ukv_f, w_out_f)

    s_in = _unperm_w_in(gr["w_in"]).reshape(n_l, d, N_DEV, in_c).transpose(2, 0, 1, 3).reshape(N_DEV, n_l * d, in_c)
    s_uq = _unperm_w_uq(gr["w_uq"]).reshape(n_l, Q_LORA, N_DEV, uq_c).transpose(2, 0, 1, 3).reshape(
        N_DEV, n_l * Q_LORA, uq_c)
    s_ukv = _unperm_w_ukv(gr["w_ukv"]).reshape(n_l, KV_LORA, N_DEV, ukv_c).transpose(2, 0, 1, 3).reshape(
        N_DEV, n_l * KV_LORA, ukv_c)
    s_out = gr["w_out"].reshape(n_l, N_DEV, out_r, d).transpose(1, 0, 2, 3).reshape(N_DEV, n_l * out_r, d)
    small_parts = [gr["norm_g"], gr["mod"], gr["b_f"], gr["q_norm_g"], gr["kv_norm_g"], gr["final_g"], cact[0]]
    sizes = [int(np.prod(p.shape)) for p in small_parts]
    total = -(-sum(sizes) // 1024) * 1024
    small = _pack_small(small_parts, total)
    r_in, r_uq, r_ukv, r_out, r_small = _exchange([s_in, s_uq, s_ukv, s_out, small],
                                                  [False, False, False, False, True], "exchange_grads")

    def upd(slots, w, m, v, name):
        shp = w.shape
        w2, m2, v2 = (a.reshape(slots.shape[1:]) for a in (w, m, v))
        return [o.reshape(shp) for o in _adamw(slots, w2, m2, v2, name)]

    o_in = upd(r_in, w_in, m_w_in, v_w_in, "adamw_w_in")
    o_uq = upd(r_uq, w_uq, m_w_uq, v_w_uq, "adamw_w_uq")
    o_ukv = upd(r_ukv, w_ukv, m_w_ukv, v_w_ukv, "adamw_w_ukv")
    o_out = upd(r_out, w_out, m_w_out, v_w_out, "adamw_w_out")

    offs = np.cumsum([0] + sizes)
    flat_all = r_small.reshape(N_DEV, total)
    dmod_all = flat_all[:, offs[1]:offs[2]].reshape(N_DEV, n_l, 3 * d)
    dmod_cols = lax.dynamic_slice_in_dim(dmod_all, me * ada_c, ada_c, axis=2).transpose(1, 0, 2)
    cact_cols = flat_all[:, offs[6]:offs[7]][:, :, None]
    g_ada = _ada_grad(cact_cols, dmod_cols)
    o_ada = upd(g_ada.reshape(1, n_l * d, ada_c), w_ada, m_w_ada, v_w_ada, "adamw_w_ada")

    zero_c = jnp.zeros((d,), F32)
    small_w = [_pack_small([norm_g, b_ada, b_f, q_norm_g, kv_norm_g, final_g, zero_c], total),
               _pack_small([m_norm_g, m_b_ada, m_b_f, m_q_norm_g, m_kv_norm_g, m_final_g, zero_c], total),
               _pack_small([v_norm_g, v_b_ada, v_b_f, v_q_norm_g, v_kv_norm_g, v_final_g, zero_c], total)]
    o_small = [o.reshape(-1) for o in _adamw(r_small, *small_w, "adamw_small")]
    shapes = [norm_g.shape, b_ada.shape, b_f.shape, q_norm_g.shape, kv_norm_g.shape, final_g.shape]

    def small_out(kind, idx):
        return o_small[kind][offs[idx]:offs[idx + 1]].reshape(shapes[idx])

    loss_all = lax.psum(loss, ("x", "y", "c"))
    outs = [loss_all, dx[None]]
    for kind in range(4):
        outs += [small_out(kind, 0), o_ada[kind], small_out(kind, 1), o_in[kind], small_out(kind, 2),
                 small_out(kind, 3), o_uq[kind], small_out(kind, 4), o_ukv[kind], o_out[kind], small_out(kind, 5)]
    return tuple(outs)
```

```python
import jax
import jax.numpy as jnp
import numpy as np
from jax import lax
from jax.experimental import pallas as pl
from jax.experimental.pallas import tpu as pltpu

F32 = jnp.float32
MXU = jnp.bfloat16

N_DEV = 8
HEADS = 8
PAIRS = HEADS // 2
HEAD_DIM = 64
NOPE = 64
ROPE = 32
HALF_ROPE = ROPE // 2
Q_LORA = 256
KV_LORA = 128
CHUNK = 64
GROUP_W = HEADS * HEAD_DIM
ROPE_W = HEADS * ROPE
EPS = 1e-6
ROPE_THETA = 10000.0
N_IN = 2984

Z_FQ, Z_FK, Z_FV, Z_FG, Z_MG, Z_QL, Z_KV, Z_MISC, Z_W = 0, 512, 1024, 1536, 2048, 2560, 2816, 2944, 3072
MISC_FF = ROPE

ADAM_LR = 0.001
ADAM_B1 = 0.9
ADAM_B2 = 0.999
ADAM_EPS = 1e-08
ADAM_WD = 0.01
ADAM_STEP = 10

VMEM_LIMIT_V7X = 56 * 1024 * 1024
LANES = 128
ATTN_TILE = 512
LOG2E = 1.4426950408889634
FOX_SCALE = HEAD_DIM ** -0.5
MLA_SCALE = (NOPE + ROPE) ** -0.5

_NT = (((1,), (1,)), ((), ()))
_TN = (((0,), (0,)), ((), ()))


def _params(*sem, side_effects=False):
    return pltpu.CompilerParams(dimension_semantics=sem, vmem_limit_bytes=VMEM_LIMIT_V7X,
                                has_side_effects=side_effects)


def _sds(shape, dtype=F32):
    return jax.ShapeDtypeStruct(shape, dtype)


def _full(shape):
    nd = len(shape)
    return pl.BlockSpec(shape, lambda *_: (0,) * nd)


def _rows(tm, width, col=0):
    return pl.BlockSpec((tm, width), lambda i: (i, col))


def _exchange(arrs, gather, name):
    n = len(arrs)

    def kern(*refs):
        copies = _exchange_copies(refs[:n], refs[n:2 * n], gather, *refs[2 * n:])
        _exchange_start(copies)
        _exchange_wait(copies)

    return pl.pallas_call(
        kern, name=name, out_shape=_exchange_out_shapes(arrs, gather),
        in_specs=[pl.BlockSpec(memory_space=pl.ANY)] * n,
        out_specs=[pl.BlockSpec(memory_space=pl.ANY)] * n,
        scratch_shapes=_exchange_sems(n),
        compiler_params=pltpu.CompilerParams(has_side_effects=True),
    )(*arrs)


def _exchange_out_shapes(arrs, gather):
    return [_sds((N_DEV,) + tuple(a.shape) if g else tuple(a.shape), a.dtype) for a, g in zip(arrs, gather)]


def _exchange_sems(n):
    return [pltpu.SemaphoreType.DMA((n, N_DEV)), pltpu.SemaphoreType.DMA((n, N_DEV)), pltpu.SemaphoreType.DMA((n,))]


def _exchange_copies(ins, outs, gather, send_sems, recv_sems, loc_sems):
    n = len(ins)
    x, y, c = lax.axis_index("x"), lax.axis_index("y"), lax.axis_index("c")
    me = 4 * x + 2 * y + c

    def src(i, j):
        return ins[i] if gather[i] else ins[i].at[j]

    local = [pltpu.make_async_copy(src(i, me), outs[i].at[me], loc_sems.at[i]) for i in range(n)]
    sends, recvs = [], []
    for k in range(1, N_DEV):
        px = 1 - x if k & 4 else x
        py = 1 - y if k & 2 else y
        pc = 1 - c if k & 1 else c
        p = 4 * px + 2 * py + pc
        for i in range(n):
            sends.append(pltpu.make_async_remote_copy(
                src_ref=src(i, p), dst_ref=outs[i].at[me], send_sem=send_sems.at[i, k],
                recv_sem=recv_sems.at[i, k], device_id=(px, py, pc), device_id_type=pl.DeviceIdType.MESH))
            recvs.append(pltpu.make_async_remote_copy(
                src_ref=src(i, p), dst_ref=outs[i].at[p], send_sem=send_sems.at[i, k],
                recv_sem=recv_sems.at[i, k], device_id=(px, py, pc), device_id_type=pl.DeviceIdType.MESH))
    return local, sends, recvs


def _exchange_start(copies):
    local, sends, _ = copies
    for cp in local + sends:
        cp.start()


def _exchange_wait(copies):
    local, sends, recvs = copies
    for cp in recvs:
        cp.wait_recv()
    for cp in sends:
        cp.wait_send()
    for cp in local:
        cp.wait()


def _modpart(cact8, w_ada, b_cols):
    n_l, d, cw = w_ada.shape

    def kern(c_ref, w_ref, b_ref, o_ref):
        o_ref[0] = jnp.dot(c_ref[...].astype(MXU), w_ref[0].astype(MXU), preferred_element_type=F32) + b_ref[0]

    return pl.pallas_call(
        kern, name="modpart", grid=(n_l,), out_shape=_sds((n_l, N_DEV, cw)),
        in_specs=[_full((N_DEV, d)), pl.BlockSpec((1, d, cw), lambda l: (l, 0, 0)),
                  pl.BlockSpec((1, 1, cw), lambda l: (l, 0, 0))],
        out_specs=pl.BlockSpec((1, N_DEV, cw), lambda l: (l, 0, 0)),
        compiler_params=_params("arbitrary"),
    )(cact8, w_ada, b_cols)


def _ada_grad(cact_cols, dmod_cols):
    n_l, _, cw = dmod_cols.shape
    d = cact_cols.shape[1]

    def kern(c_ref, dm_ref, o_ref):
        acc = c_ref[0] * dm_ref[0, 0:1, :]
        for s in range(1, N_DEV):
            acc = acc + c_ref[s] * dm_ref[0, s:s + 1, :]
        o_ref[0] = acc

    return pl.pallas_call(
        kern, name="ada_grad", grid=(n_l,), out_shape=_sds((n_l, d, cw)),
        in_specs=[_full((N_DEV, d, 1)), pl.BlockSpec((1, N_DEV, cw), lambda l: (l, 0, 0))],
        out_specs=pl.BlockSpec((1, d, cw), lambda l: (l, 0, 0)),
        compiler_params=_params("arbitrary"),
    )(cact_cols, dmod_cols)


def _k_in(x, g, mod3, w, tm=256):
    s_len, d = x.shape
    qkv_w = 3 * GROUP_W

    def kern(x_ref, g_ref, mod_ref, w_ref, z_ref, h_ref, qkv_ref):
        xv = x_ref[...]
        r = lax.rsqrt(jnp.mean(xv * xv, axis=-1, keepdims=True) + EPS)
        xn = xv * r * g_ref[...]
        h = (xn * (1.0 + mod_ref[1:2, :]) + mod_ref[0:1, :]).astype(MXU)
        h_ref[...] = h
        z = jnp.dot(h, w_ref[...], preferred_element_type=F32)
        z_ref[...] = z
        qkv_ref[:, :GROUP_W] = (z[:, Z_FQ:Z_FQ + GROUP_W] * (FOX_SCALE * LOG2E)).astype(MXU)
        qkv_ref[:, GROUP_W:] = z[:, Z_FK:Z_FK + 2 * GROUP_W].astype(MXU)

    return pl.pallas_call(
        kern, name="k_in", grid=(s_len // tm,),
        out_shape=[_sds((s_len, Z_W)), _sds((s_len, d), MXU), _sds((s_len, qkv_w), MXU)],
        in_specs=[_rows(tm, d), _full((1, d)), _full((3, d)), _full((d, Z_W))],
        out_specs=[_rows(tm, Z_W), _rows(tm, d), _rows(tm, qkv_w)],
        compiler_params=_params("arbitrary"),
    )(x, g, mod3, w)


def _k_cum(fft, bf):
    nh, s_len = fft.shape

    def kern(ff_ref, b_ref, cum_ref):
        r_i = lax.broadcasted_iota(jnp.int32, (LANES, LANES), 0)
        c_i = lax.broadcasted_iota(jnp.int32, (LANES, LANES), 1)
        upper = (r_i <= c_i).astype(F32)
        carry = jnp.zeros((nh, 1), F32)
        for cb in range(s_len // LANES):
            sl = slice(cb * LANES, (cb + 1) * LANES)
            xc = ff_ref[:, sl] + b_ref[...]
            lf = jnp.minimum(xc, 0.0) - jnp.log(1.0 + jnp.exp(-jnp.abs(xc)))
            cum_ref[:, sl] = jnp.dot(lf, upper, precision=lax.Precision.HIGHEST,
                                     preferred_element_type=F32) + carry
            carry = carry + jnp.sum(lf, axis=1, keepdims=True)

    return pl.pallas_call(
        kern, name="k_cum", out_shape=_sds((nh, s_len)),
        in_specs=[pl.BlockSpec(memory_space=pltpu.VMEM)] * 2,
        out_specs=pl.BlockSpec(memory_space=pltpu.VMEM),
        compiler_params=_params(),
    )(fft, bf)


def _k_cum_bwd(dck, fft, bf):
    nh, s_len = fft.shape

    def kern(dc_ref, ff_ref, b_ref, dff_ref, db_ref):
        r_i = lax.broadcasted_iota(jnp.int32, (LANES, LANES), 0)
        c_i = lax.broadcasted_iota(jnp.int32, (LANES, LANES), 1)
        lower = (r_i >= c_i).astype(F32)
        carry = jnp.zeros((nh, 1), F32)
        db = jnp.zeros((nh, 1), F32)
        for cb in range(s_len // LANES - 1, -1, -1):
            sl = slice(cb * LANES, (cb + 1) * LANES)
            dc = dc_ref[:, sl]
            dlf = jnp.dot(dc, lower, precision=lax.Precision.HIGHEST, preferred_element_type=F32) + carry
            carry = carry + jnp.sum(dc, axis=1, keepdims=True)
            dff = dlf * jax.nn.sigmoid(-(ff_ref[:, sl] + b_ref[...]))
            dff_ref[:, sl] = dff
            db = db + jnp.sum(dff, axis=1, keepdims=True)
        db_ref[...] = jnp.broadcast_to(db, (nh, LANES))

    return pl.pallas_call(
        kern, name="k_cum_bwd", out_shape=[_sds((nh, s_len)), _sds((nh, LANES))],
        in_specs=[pl.BlockSpec(memory_space=pltpu.VMEM)] * 3,
        out_specs=[pl.BlockSpec(memory_space=pltpu.VMEM)] * 2,
        compiler_params=_params(),
    )(dck, fft, bf)


def _swap16(t):
    lane = lax.broadcasted_iota(jnp.int32, t.shape, 1)
    return jnp.where(lane % ROPE < HALF_ROPE, pltpu.roll(t, LANES - HALF_ROPE, 1), pltpu.roll(t, HALF_ROPE, 1))


def _rope(t, cos, sin):
    return t * cos + _swap16(t) * sin


def _rope_bwd(dt, cos, sin):
    return dt * cos - _swap16(dt) * sin


def _k_prep(z, cos, sin, gq, gkv, wuq, wukv, tm=512):
    s_len = z.shape[0]
    qc = MLA_SCALE * LOG2E

    def kern(ql_ref, kvl_ref, misc_ref, cos_ref, sin_ref, gq_ref, gkv_ref, wuq_ref, wukv_ref,
             qn_out, qr_out, kn_out, v_out, kr_out, qn_ref, kvn_ref):
        cs, sn = cos_ref[...], sin_ref[...]
        ql = ql_ref[...]
        rq = lax.rsqrt(jnp.mean(ql * ql, axis=-1, keepdims=True) + EPS)
        qn = (ql * rq * gq_ref[...]).astype(MXU)
        qn_ref[...] = qn
        q = jnp.dot(qn, wuq_ref[...], preferred_element_type=F32)
        qn_out[...] = (q[:, :GROUP_W] * qc).astype(MXU)
        for half in range(ROPE_W // LANES):
            lo = GROUP_W + half * LANES
            qr_out[:, half * LANES:(half + 1) * LANES] = (_rope(q[:, lo:lo + LANES], cs, sn) * qc).astype(MXU)
        kvl = kvl_ref[...]
        rk = lax.rsqrt(jnp.mean(kvl * kvl, axis=-1, keepdims=True) + EPS)
        kvn = (kvl * rk * gkv_ref[...]).astype(MXU)
        kvn_ref[...] = kvn
        kv = jnp.dot(kvn, wukv_ref[...], preferred_element_type=F32)
        kn_out[...] = kv[:, :GROUP_W].astype(MXU)
        v_out[...] = kv[:, GROUP_W:].astype(MXU)
        misc = misc_ref[...]
        lane = lax.broadcasted_iota(jnp.int32, misc.shape, 1)
        kr = jnp.where(lane < ROPE, _rope(misc, cs, sn), 0.0)
        kr4 = kr
        for rep in range(1, LANES // ROPE):
            kr4 = kr4 + pltpu.roll(kr, rep * ROPE, 1)
        kr_out[...] = kr4.astype(MXU)

    return pl.pallas_call(
        kern, name="k_prep", grid=(s_len // tm,),
        out_shape=[_sds((s_len, GROUP_W), MXU), _sds((s_len, ROPE_W), MXU), _sds((s_len, GROUP_W), MXU),
                   _sds((s_len, GROUP_W), MXU), _sds((s_len, LANES), MXU), _sds((s_len, Q_LORA), MXU),
                   _sds((s_len, KV_LORA), MXU)],
        in_specs=[_rows(tm, Q_LORA, Z_QL // Q_LORA), _rows(tm, KV_LORA, Z_KV // KV_LORA),
                  _rows(tm, LANES, Z_MISC // LANES), _rows(tm, LANES), _rows(tm, LANES),
                  _full((1, Q_LORA)), _full((1, KV_LORA)), _full((Q_LORA, GROUP_W + ROPE_W)),
                  _full((KV_LORA, 2 * GROUP_W))],
        out_specs=[_rows(tm, GROUP_W), _rows(tm, ROPE_W), _rows(tm, GROUP_W), _rows(tm, GROUP_W), _rows(tm, LANES),
                   _rows(tm, Q_LORA), _rows(tm, KV_LORA)],
        compiler_params=_params("arbitrary"),
    )(z, z, z, cos, sin, gq, gkv, wuq, wukv)


def _tile_mask(t, chunk_mask, transposed):
    row = lax.broadcasted_iota(jnp.int32, (t, t), 0)
    col = lax.broadcasted_iota(jnp.int32, (t, t), 1)
    qi, ki = (col, row) if transposed else (row, col)
    if chunk_mask:
        return (ki // CHUNK) <= (qi // CHUNK)
    return ki <= qi


def _keep_head(x, hh, axis, rope_group):
    idx = lax.broadcasted_iota(jnp.int32, x.shape, axis)
    keep = (idx >= hh * HEAD_DIM) & (idx < (hh + 1) * HEAD_DIM)
    if x.shape[axis] != LANES:
        keep = keep | ((idx >= LANES + rope_group * ROPE) & (idx < LANES + (rope_group + 1) * ROPE))
    return jnp.where(keep, x, jnp.zeros_like(x))


def _attention_fwd(q, q_blk, k, k_blk, v, v_blk, bias, rope, chunk_mask, name, side=None):
    s_len = q.shape[0]
    t = min(ATTN_TILE, s_len // 2)
    nq = s_len // t
    n_side = len(side[0]) if side else 0

    def kern(*refs):
        q_ref, k_ref, v_ref = refs[:3]
        pos = 3
        if bias is not None:
            cq_ref, ck_ref = refs[pos:pos + 2]
            pos += 2
        if rope is not None:
            qr_ref, kr_ref = refs[pos:pos + 2]
            pos += 2
        side_in = refs[pos:pos + n_side]
        pos += n_side
        o_ref, lse_ref = refs[pos:pos + 2]
        side_out = refs[pos + 2:pos + 2 + n_side]
        vt_scr, m_scr, l_scr, acc_scr = refs[pos + 2 + n_side:pos + 6 + n_side]
        sems = refs[pos + 6 + n_side:]
        pj = pl.program_id(0)
        if n_side:
            @pl.when(pj == 0)
            def _():
                _exchange_start(_exchange_copies(side_in, side_out, side[1], *sems))
        for i in range(nq):
            vt_scr[:, i * t:(i + 1) * t] = v_ref[i * t:(i + 1) * t, :].T

        def qbody(qi, _):
            qs = pl.multiple_of(qi * t, t)
            qt = q_ref[pl.ds(qs, t), :]
            if rope is not None:
                qt = jnp.concatenate([qt, qr_ref[pl.ds(qs, t), :]], axis=1)
            qh = [_keep_head(qt, hh, 1, (pj % 2) * 2 + hh) for hh in range(2)]
            m_scr[...] = jnp.full(m_scr.shape, -jnp.inf, F32)
            l_scr[...] = jnp.zeros(l_scr.shape, F32)
            acc_scr[...] = jnp.zeros(acc_scr.shape, F32)

            def step(ki, masked):
                ks = pl.multiple_of(ki * t, t)
                kt = k_ref[pl.ds(ks, t), :]
                if rope is not None:
                    kt = jnp.concatenate([kt, kr_ref[pl.ds(ks, t), :]], axis=1)
                vtt = vt_scr[:, pl.ds(ks, t)]
                for hh in range(2):
                    st = lax.dot_general(kt, qh[hh], _NT, preferred_element_type=F32)
                    if bias is not None:
                        st = st + cq_ref[hh, :, pl.ds(qs, t)] - ck_ref[hh, pl.ds(ks, t), :]
                    if masked:
                        st = jnp.where(_tile_mask(t, chunk_mask, True), st, -jnp.inf)
                    m_old = m_scr[hh]
                    m_new = jnp.maximum(m_old, jnp.max(st, axis=0, keepdims=True))
                    pt = jnp.exp2(st - m_new)
                    alpha = jnp.exp2(m_old - m_new)
                    l_scr[hh] = alpha * l_scr[hh] + jnp.sum(pt, axis=0, keepdims=True)
                    acc_scr[hh] = alpha * acc_scr[hh] + jnp.dot(vtt, pt.astype(MXU), preferred_element_type=F32)
                    m_scr[hh] = m_new

            def loop_body(ki, carry):
                step(ki, False)
                return carry

            lax.fori_loop(0, qi, loop_body, 0)
            step(qi, True)
            o0 = acc_scr[0] / l_scr[0]
            o1 = acc_scr[1] / l_scr[1]
            row = lax.broadcasted_iota(jnp.int32, o0.shape, 0)
            o_ref[pl.ds(qs, t), :] = jnp.where(row < HEAD_DIM, o0, o1).T
            for hh in range(2):
                lse_ref[hh, :, pl.ds(qs, t)] = m_scr[hh] + jnp.log2(l_scr[hh])
            return 0

        lax.fori_loop(0, nq, qbody, 0)
        if n_side:
            @pl.when(pj == PAIRS - 1)
            def _():
                _exchange_wait(_exchange_copies(side_in, side_out, side[1], *sems))

    def tok(blk):
        return pl.BlockSpec((s_len, LANES), lambda j: (0, blk + j))

    rowb = pl.BlockSpec((2, 1, s_len), lambda j: (j, 0, 0))
    hbm = pl.BlockSpec(memory_space=pl.ANY)
    ins = [q, k, v]
    in_specs = [tok(q_blk), tok(k_blk), tok(v_blk)]
    if bias is not None:
        ins += list(bias)
        in_specs += [rowb, pl.BlockSpec((2, s_len, 1), lambda j: (j, 0, 0))]
    if rope is not None:
        ins += list(rope)
        in_specs += [pl.BlockSpec((s_len, LANES), lambda j: (0, j // 2)), _full((s_len, LANES))]
    out_shape = [_sds((s_len, PAIRS * LANES)), _sds((HEADS, 1, s_len))]
    scratch = [pltpu.VMEM((LANES, s_len), v.dtype), pltpu.VMEM((2, 1, t), F32), pltpu.VMEM((2, 1, t), F32),
               pltpu.VMEM((2, LANES, t), F32)]
    if n_side:
        ins += list(side[0])
        out_shape += _exchange_out_shapes(*side)
        scratch += _exchange_sems(n_side)
    return pl.pallas_call(
        kern, name=name, grid=(PAIRS,), out_shape=out_shape,
        in_specs=in_specs + [hbm] * n_side, out_specs=[tok(0), rowb] + [hbm] * n_side,
        scratch_shapes=scratch,
        compiler_params=_params("arbitrary", side_effects=bool(n_side)),
    )(*ins)


def _attention_bwd(q, q_blk, k, k_blk, v, v_blk, do, pack, ck_row, rope, chunk_mask, q_scale, k_scale, name,
                   side=None):
    s_len = q.shape[0]
    t = min(ATTN_TILE, s_len // 2)
    nq = s_len // t
    has_bias = ck_row is not None
    kw = 2 * LANES if rope is not None else LANES
    n_side = len(side[0]) if side else 0

    def kern(*refs):
        q_ref, k_ref, v_ref, do_ref, pack_ref = refs[:5]
        pos = 5
        if has_bias:
            ck_ref = refs[pos]
            pos += 1
        if rope is not None:
            qr_ref, kr_ref = refs[pos:pos + 2]
            pos += 2
        side_in = refs[pos:pos + n_side]
        pos += n_side
        dq_ref, dk_ref, dv_ref = refs[pos:pos + 3]
        pos += 3
        if has_bias:
            dcq_ref, dck_ref = refs[pos:pos + 2]
            pos += 2
        if rope is not None:
            dqr_ref, dkr_ref = refs[pos:pos + 2]
            pos += 2
        side_out = refs[pos:pos + n_side]
        pos += n_side
        qt_scr, dot_scr, dkt_scr, dvt_scr, dq_scr, dcq_scr = refs[pos:pos + 6]
        sems = refs[pos + 6:]
        pj = pl.program_id(0)
        if n_side:
            @pl.when(pj == 0)
            def _():
                _exchange_start(_exchange_copies(side_in, side_out, side[1], *sems))

        for i in range(nq):
            sl = slice(i * t, (i + 1) * t)
            qt_scr[:LANES, sl] = q_ref[sl, :].T
            dot_scr[:, sl] = do_ref[sl, :].T
            if rope is not None:
                qt_scr[LANES:, sl] = qr_ref[sl, :].T
        dkt_scr[...] = jnp.zeros(dkt_scr.shape, F32)
        dvt_scr[...] = jnp.zeros(dvt_scr.shape, F32)
        if has_bias:
            dck_ref[...] = jnp.zeros(dck_ref.shape, F32)

            @pl.when(pj == 0)
            def _():
                dcq_ref[...] = jnp.zeros(dcq_ref.shape, F32)

        def keep(x, hh, axis):
            return _keep_head(x, hh, axis, (pj % 2) * 2 + hh)

        def qbody(qi, _):
            qs = pl.multiple_of(qi * t, t)
            qt = q_ref[pl.ds(qs, t), :]
            if rope is not None:
                qt = jnp.concatenate([qt, qr_ref[pl.ds(qs, t), :]], axis=1)
            qtt = qt_scr[:, pl.ds(qs, t)]
            dot = do_ref[pl.ds(qs, t), :]
            dott = dot_scr[:, pl.ds(qs, t)]
            pk = pack_ref[pl.ds(qs, t), :]
            lane = lax.broadcasted_iota(jnp.int32, pk.shape, 1)
            qh = [keep(qt, hh, 1) for hh in range(2)]
            qth = [keep(qtt, hh, 0) for hh in range(2)]
            doh = [keep(dot, hh, 1) for hh in range(2)]
            doth = [keep(dott, hh, 0) for hh in range(2)]
            a_col = [jnp.sum(jnp.where(lane == 2 * pj + hh, pk, 0.0), axis=1, keepdims=True) for hh in range(2)]
            d_col = [jnp.sum(jnp.where(lane == HEADS + 2 * pj + hh, pk, 0.0), axis=1, keepdims=True)
                     for hh in range(2)]
            dq_scr[...] = jnp.zeros(dq_scr.shape, F32)
            if has_bias:
                dcq_scr[...] = jnp.zeros(dcq_scr.shape, F32)

            def step(ki, masked):
                ks = pl.multiple_of(ki * t, t)
                kt = k_ref[pl.ds(ks, t), :]
                if rope is not None:
                    kt = jnp.concatenate([kt, kr_ref[pl.ds(ks, t), :]], axis=1)
                vt = v_ref[pl.ds(ks, t), :]
                dq_acc = dq_scr[...]
                dk_acc = jnp.zeros((kw, t), F32)
                dv_acc = jnp.zeros((LANES, t), F32)
                for hh in range(2):
                    s = lax.dot_general(qh[hh], kt, _NT, preferred_element_type=F32) + a_col[hh]
                    if has_bias:
                        s = s - ck_ref[hh, :, pl.ds(ks, t)]
                    p = jnp.exp2(s)
                    if masked:
                        p = jnp.where(_tile_mask(t, chunk_mask, False), p, 0.0)
                    dpd = lax.dot_general(doh[hh], vt, _NT, preferred_element_type=F32)
                    ds = p * (dpd - d_col[hh])
                    dsb = ds.astype(MXU)
                    dv_acc = dv_acc + jnp.dot(doth[hh], p.astype(MXU), preferred_element_type=F32)
                    dk_acc = dk_acc + jnp.dot(qth[hh], dsb, preferred_element_type=F32)
                    dq_acc = dq_acc + jnp.dot(dsb, keep(kt, hh, 1), preferred_element_type=F32)
                    if has_bias:
                        dcq_scr[hh] += jnp.sum(ds, axis=1, keepdims=True)
                        dck_ref[hh, :, pl.ds(ks, t)] += -jnp.sum(ds, axis=0, keepdims=True)
                dq_scr[...] = dq_acc
                dvt_scr[:, pl.ds(ks, t)] += dv_acc
                dkt_scr[:, pl.ds(ks, t)] += dk_acc

            def loop_body(ki, carry):
                step(ki, False)
                return carry

            lax.fori_loop(0, qi, loop_body, 0)
            step(qi, True)
            dq_ref[pl.ds(qs, t), :] = (dq_scr[:, :LANES] * q_scale).astype(dq_ref.dtype)
            if rope is not None:
                dqr_ref[0, pl.ds(qs, t), :] = dq_scr[:, LANES:] * q_scale
            if has_bias:
                old = dcq_ref[pl.ds(qs, t), :]
                dcq_ref[pl.ds(qs, t), :] = jnp.where(lane == 2 * pj, dcq_scr[0],
                                                     jnp.where(lane == 2 * pj + 1, dcq_scr[1], old))
            return 0

        lax.fori_loop(0, nq, qbody, 0)
        for i in range(nq):
            sl = slice(i * t, (i + 1) * t)
            dk_ref[sl, :] = (dkt_scr[:LANES, sl].T * k_scale).astype(dk_ref.dtype)
            dv_ref[sl, :] = dvt_scr[:, sl].T.astype(dv_ref.dtype)
            if rope is not None:
                dkr_ref[0, sl, :] = dkt_scr[LANES:, sl].T * k_scale
        if n_side:
            @pl.when(pj == PAIRS - 1)
            def _():
                _exchange_wait(_exchange_copies(side_in, side_out, side[1], *sems))

    def tok(blk):
        return pl.BlockSpec((s_len, LANES), lambda j: (0, blk + j))

    shared = _full((s_len, LANES))
    rowb = pl.BlockSpec((2, 1, s_len), lambda j: (j, 0, 0))
    slab = pl.BlockSpec((1, s_len, LANES), lambda j: (j, 0, 0))
    hbm = pl.BlockSpec(memory_space=pl.ANY)
    ins = [q, k, v, do, pack]
    in_specs = [tok(q_blk), tok(k_blk), tok(v_blk), tok(0), shared]
    out_shape = [_sds((s_len, PAIRS * LANES), MXU)] * 3
    out_specs = [tok(0)] * 3
    if has_bias:
        ins.append(ck_row)
        in_specs.append(rowb)
        out_shape += [_sds((s_len, LANES)), _sds((HEADS, 1, s_len))]
        out_specs += [shared, rowb]
    if rope is not None:
        ins += list(rope)
        in_specs += [pl.BlockSpec((s_len, LANES), lambda j: (0, j // 2)), shared]
        out_shape += [_sds((PAIRS, s_len, LANES))] * 2
        out_specs += [slab, slab]
    scratch = [pltpu.VMEM((kw, s_len), q.dtype), pltpu.VMEM((LANES, s_len), do.dtype),
               pltpu.VMEM((kw, s_len), F32), pltpu.VMEM((LANES, s_len), F32),
               pltpu.VMEM((t, kw), F32), pltpu.VMEM((2, t, 1), F32)]
    if n_side:
        ins += list(side[0])
        out_shape += _exchange_out_shapes(*side)
        scratch += _exchange_sems(n_side)
    return pl.pallas_call(
        kern, name=name, grid=(PAIRS,), out_shape=out_shape,
        in_specs=in_specs + [hbm] * n_side, out_specs=out_specs + [hbm] * n_side, scratch_shapes=scratch,
        compiler_params=_params("arbitrary", side_effects=bool(n_side)),
    )(*ins)


def _silu(a):
    return a * jax.nn.sigmoid(a)


def _k_out(of, om, z, x, gate, wout, tm=256):
    s_len, d = x.shape

    def kern(of_ref, om_ref, fg_ref, mg_ref, x_ref, gate_ref, w_ref, xo_ref, y_ref, u_ref):
        u_ref[:, :GROUP_W] = (of_ref[...] * _silu(fg_ref[...])).astype(MXU)
        u_ref[:, GROUP_W:] = (om_ref[...] * _silu(mg_ref[...])).astype(MXU)
        y = jnp.dot(u_ref[...], w_ref[...], preferred_element_type=F32)
        y_ref[...] = y
        xo_ref[...] = x_ref[...] + gate_ref[...] * y

    return pl.pallas_call(
        kern, name="k_out", grid=(s_len // tm,),
        out_shape=[_sds((s_len, d)), _sds((s_len, d)), _sds((s_len, 2 * GROUP_W), MXU)],
        in_specs=[_rows(tm, GROUP_W), _rows(tm, GROUP_W), _rows(tm, GROUP_W, Z_FG // GROUP_W),
                  _rows(tm, GROUP_W, Z_MG // GROUP_W), _rows(tm, d), _full((1, d)), _full((2 * GROUP_W, d))],
        out_specs=[_rows(tm, d), _rows(tm, d), _rows(tm, 2 * GROUP_W)],
        compiler_params=_params("arbitrary"),
    )(of, om, z, z, x, gate, wout)


def _k_loss(x, gf, tgt, tm=256):
    s_len, d = x.shape

    def kern(x_ref, g_ref, t_ref, loss_ref, dx_ref, dg_ref):
        i = pl.program_id(0)
        xv = x_ref[...]
        r = lax.rsqrt(jnp.mean(xv * xv, axis=-1, keepdims=True) + EPS)
        xh = xv * r
        diff = xh * g_ref[...] - t_ref[...]
        part = 0.5 * jnp.sum(jnp.mean(diff * diff, axis=-1, keepdims=True))
        dout = diff * (1.0 / d)
        dxh = dout * g_ref[...]
        dx_ref[...] = r * (dxh - xh * jnp.mean(dxh * xh, axis=-1, keepdims=True))

        @pl.when(i == 0)
        def _():
            loss_ref[...] = jnp.zeros_like(loss_ref)
            dg_ref[...] = jnp.zeros_like(dg_ref)

        loss_ref[...] += jnp.full(loss_ref.shape, part, F32)
        dg_ref[...] += jnp.sum(dout * xh, axis=0, keepdims=True)

    return pl.pallas_call(
        kern, name="k_loss", grid=(s_len // tm,),
        out_shape=[_sds((1, LANES)), _sds((s_len, d)), _sds((1, d))],
        in_specs=[_rows(tm, d), _full((1, d)), _rows(tm, d)],
        out_specs=[_full((1, LANES)), _rows(tm, d), _full((1, d))],
        compiler_params=_params("arbitrary"),
    )(x, gf, tgt)


def _kb_out(dxo, y, gate, wout, of, om, z, tm=256):
    s_len, d = dxo.shape

    def kern(dxo_ref, y_ref, gate_ref, wt_ref, of_ref, om_ref, fg_ref, mg_ref,
             dy_ref, dof_ref, dom_ref, dfg_ref, dmg_ref, dlf_ref, dlm_ref, dgate_ref):
        i = pl.program_id(0)
        dxv = dxo_ref[...]

        @pl.when(i == 0)
        def _():
            dgate_ref[...] = jnp.zeros_like(dgate_ref)

        dgate_ref[...] += jnp.sum(dxv * y_ref[...], axis=0, keepdims=True)
        dyb = (dxv * gate_ref[...]).astype(MXU)
        dy_ref[...] = dyb
        du = lax.dot_general(dyb, wt_ref[...], _NT, preferred_element_type=F32)
        head_of = (lax.broadcasted_iota(jnp.int32, (GROUP_W, LANES), 0) // HEAD_DIM
                   == lax.broadcasted_iota(jnp.int32, (GROUP_W, LANES), 1)).astype(F32)
        for du_g, o_ref, g_ref, do_ref, dg_ref, dl_ref in (
                (du[:, :GROUP_W], of_ref, fg_ref, dof_ref, dfg_ref, dlf_ref),
                (du[:, GROUP_W:], om_ref, mg_ref, dom_ref, dmg_ref, dlm_ref)):
            a = g_ref[...]
            sg = jax.nn.sigmoid(a)
            ov = o_ref[...]
            dov = du_g * (a * sg)
            do_ref[...] = dov.astype(MXU)
            dg_ref[...] = (du_g * ov * (sg * (1.0 + a * (1.0 - sg)))).astype(MXU)
            dl_ref[...] = jnp.dot(dov * ov, head_of, precision=lax.Precision.HIGHEST, preferred_element_type=F32)

    return pl.pallas_call(
        kern, name="kb_out", grid=(s_len // tm,),
        out_shape=[_sds((s_len, d), MXU), _sds((s_len, GROUP_W), MXU), _sds((s_len, GROUP_W), MXU),
                   _sds((s_len, GROUP_W), MXU), _sds((s_len, GROUP_W), MXU), _sds((s_len, LANES)),
                   _sds((s_len, LANES)), _sds((1, d))],
        in_specs=[_rows(tm, d), _rows(tm, d), _full((1, d)), _full((2 * GROUP_W, d)), _rows(tm, GROUP_W),
                  _rows(tm, GROUP_W), _rows(tm, GROUP_W, Z_FG // GROUP_W), _rows(tm, GROUP_W, Z_MG // GROUP_W)],
        out_specs=[_rows(tm, d), _rows(tm, GROUP_W), _rows(tm, GROUP_W), _rows(tm, GROUP_W),
                   _rows(tm, GROUP_W), _rows(tm, LANES), _rows(tm, LANES), _full((1, d))],
        compiler_params=_params("arbitrary"),
    )(dxo, y, gate, wout, of, om, z, z)


def _kb_prep(dqn, dqr, dkn, dv, dkr4, dff, z, cos, sin, gq, gkv, wuq_t, wukv_t, tm=512):
    s_len = z.shape[0]
    qw = GROUP_W + ROPE_W
    tail = Q_LORA + KV_LORA + LANES

    def kern(dqn_ref, dqr_ref, dkn_ref, dv_ref, dkr_ref, dff_ref, ql_ref, kvl_ref, cos_ref, sin_ref,
             gq_ref, gkv_ref, wuqt_ref, wukvt_ref, dq_ref, dz_ref, dgq_ref, dgkv_ref):
        i = pl.program_id(0)

        @pl.when(i == 0)
        def _():
            dgq_ref[...] = jnp.zeros_like(dgq_ref)
            dgkv_ref[...] = jnp.zeros_like(dgkv_ref)

        cs, sn = cos_ref[...], sin_ref[...]
        dq_ref[:, :GROUP_W] = dqn_ref[...]
        for half in range(ROPE_W // LANES):
            sl = slice(half * LANES, (half + 1) * LANES)
            dq_ref[:, GROUP_W + half * LANES:GROUP_W + (half + 1) * LANES] = _rope_bwd(dqr_ref[:, sl], cs, sn).astype(MXU)
        dqn = lax.dot_general(dq_ref[...], wuqt_ref[...], _NT, preferred_element_type=F32)
        ql = ql_ref[...]
        rq = lax.rsqrt(jnp.mean(ql * ql, axis=-1, keepdims=True) + EPS)
        qh = ql * rq
        dgq_ref[...] += jnp.sum(dqn * qh, axis=0, keepdims=True)
        dqh = dqn * gq_ref[...]
        dz_ref[:, :Q_LORA] = (rq * (dqh - qh * jnp.mean(dqh * qh, axis=-1, keepdims=True))).astype(MXU)

        dkvn = (lax.dot_general(dkn_ref[...], wukvt_ref[:, :GROUP_W], _NT, preferred_element_type=F32)
                + lax.dot_general(dv_ref[...], wukvt_ref[:, GROUP_W:], _NT, preferred_element_type=F32))
        kvl = kvl_ref[...]
        rk = lax.rsqrt(jnp.mean(kvl * kvl, axis=-1, keepdims=True) + EPS)
        kh = kvl * rk
        dgkv_ref[...] += jnp.sum(dkvn * kh, axis=0, keepdims=True)
        dkh = dkvn * gkv_ref[...]
        dz_ref[:, Q_LORA:Q_LORA + KV_LORA] = (
            rk * (dkh - kh * jnp.mean(dkh * kh, axis=-1, keepdims=True))).astype(MXU)

        g4 = dkr_ref[...]
        g = g4
        for rep in range(1, LANES // ROPE):
            g = g + pltpu.roll(g4, rep * ROPE, 1)
        lane = lax.broadcasted_iota(jnp.int32, g.shape, 1)
        dmisc = jnp.where(lane < ROPE, _rope_bwd(g, cs, sn), 0.0) + dff_ref[...]
        dz_ref[:, Q_LORA + KV_LORA:] = dmisc.astype(MXU)

    return pl.pallas_call(
        kern, name="kb_prep", grid=(s_len // tm,),
        out_shape=[_sds((s_len, qw), MXU), _sds((s_len, tail), MXU), _sds((1, Q_LORA)), _sds((1, KV_LORA))],
        in_specs=[_rows(tm, GROUP_W), _rows(tm, ROPE_W), _rows(tm, GROUP_W), _rows(tm, GROUP_W), _rows(tm, LANES),
                  _rows(tm, LANES), _rows(tm, Q_LORA, Z_QL // Q_LORA), _rows(tm, KV_LORA, Z_KV // KV_LORA),
                  _rows(tm, LANES), _rows(tm, LANES), _full((1, Q_LORA)), _full((1, KV_LORA)),
                  _full((Q_LORA, qw)), _full((KV_LORA, 2 * GROUP_W))],
        out_specs=[_rows(tm, qw), _rows(tm, tail), _full((1, Q_LORA)), _full((1, KV_LORA))],
        compiler_params=_params("arbitrary"),
    )(dqn, dqr, dkn, dv, dkr4, dff, z, z, cos, sin, gq, gkv, wuq_t, wukv_t)


def _kb_in(dz_pieces, w, x, g, mod3, dxo, tm=256):
    s_len, d = x.shape
    widths = [p.shape[1] for p in dz_pieces]
    n_p = len(widths)

    def kern(*refs):
        dz_refs = refs[:n_p]
        w_ref, x_ref, g_ref, mod_ref, dxo_ref, dx_ref, acc_ref = refs[n_p:]
        i = pl.program_id(0)

        @pl.when(i == 0)
        def _():
            acc_ref[...] = jnp.zeros_like(acc_ref)

        dh = jnp.zeros((tm, d), F32)
        lo = 0
        for p_ref, wd in zip(dz_refs, widths):
            dh = dh + lax.dot_general(p_ref[...], w_ref[:, lo:lo + wd], _NT, preferred_element_type=F32)
            lo += wd
        xv = x_ref[...]
        r = lax.rsqrt(jnp.mean(xv * xv, axis=-1, keepdims=True) + EPS)
        xh = xv * r
        xn = xh * g_ref[...]
        dxn = dh * (1.0 + mod_ref[1:2, :])
        acc_ref[0:1, :] += jnp.sum(dh, axis=0, keepdims=True)
        acc_ref[1:2, :] += jnp.sum(dh * xn, axis=0, keepdims=True)
        acc_ref[2:3, :] += jnp.sum(dxn * xh, axis=0, keepdims=True)
        dxh = dxn * g_ref[...]
        dx_ref[...] = dxo_ref[...] + r * (dxh - xh * jnp.mean(dxh * xh, axis=-1, keepdims=True))

    return pl.pallas_call(
        kern, name="kb_in", grid=(s_len // tm,),
        out_shape=[_sds((s_len, d)), _sds((3, d))],
        in_specs=[_rows(tm, wd) for wd in widths] + [_full((d, Z_W)), _rows(tm, d), _full((1, d)), _full((3, d)),
                                                     _rows(tm, d)],
        out_specs=[_rows(tm, d), _full((3, d))],
        compiler_params=_params("arbitrary"),
    )(*dz_pieces, w, x, g, mod3, dxo)


def _weight_grad(a, pieces, name, tk=512):
    s_len, m = a.shape
    widths = [p.shape[1] for p in pieces]
    n = sum(widths)
    tk = min(tk, s_len)

    def kern(a_ref, *refs):
        o_ref = refs[-1]

        @pl.when(pl.program_id(0) == 0)
        def _():
            o_ref[...] = jnp.zeros_like(o_ref)

        at = a_ref[...]
        lo = 0
        for p_ref, w in zip(refs[:-1], widths):
            o_ref[:, lo:lo + w] += lax.dot_general(at, p_ref[...], _TN, preferred_element_type=F32)
            lo += w

    return pl.pallas_call(
        kern, name=name, grid=(s_len // tk,), out_shape=_sds((m, n)),
        in_specs=[_rows(tk, m)] + [_rows(tk, w) for w in widths],
        out_specs=_full((m, n)),
        compiler_params=_params("arbitrary"),
    )(a, *pieces)


def _adamw(gslots, w, m, v, name):
    n, r, c = gslots.shape
    tm = r
    for cand in (256, 128, 64, 32, 16, 8):
        if r % cand == 0:
            tm = cand
            break

    def kern(g_ref, w_ref, m_ref, v_ref, go_ref, d_ref, mo_ref, vo_ref):
        g = g_ref[0]
        for s in range(1, n):
            g = g + g_ref[s]
        m_new = ADAM_B1 * m_ref[...] + (1.0 - ADAM_B1) * g
        v_new = ADAM_B2 * v_ref[...] + (1.0 - ADAM_B2) * (g * g)
        m_hat = m_new / (1.0 - ADAM_B1 ** ADAM_STEP)
        v_hat = v_new / (1.0 - ADAM_B2 ** ADAM_STEP)
        go_ref[...] = g
        mo_ref[...] = m_new
        vo_ref[...] = v_new
        d_ref[...] = -ADAM_LR * (m_hat / (jnp.sqrt(v_hat) + ADAM_EPS) + ADAM_WD * w_ref[...])

    row = pl.BlockSpec((tm, c), lambda i: (i, 0))
    return pl.pallas_call(
        kern, name=name, grid=(r // tm,), out_shape=[_sds((r, c))] * 4,
        in_specs=[pl.BlockSpec((n, tm, c), lambda i: (0, i, 0)), row, row, row],
        out_specs=[row] * 4,
        compiler_params=_params("arbitrary"),
    )(gslots, w, m, v)


def _perm_w_in(w):
    pad = jnp.zeros(w.shape[:-1] + (Z_W - Z_MISC - ROPE - HEADS,), w.dtype)
    return jnp.concatenate([w[..., 0:1536], w[..., 1544:2056], w[..., 2472:2984], w[..., 2056:2312],
                            w[..., 2312:2440], w[..., 2440:2472], w[..., 1536:1544], pad], axis=-1)


def _unperm_w_in(g):
    ff0 = Z_MISC + MISC_FF
    return jnp.concatenate([g[..., 0:1536], g[..., ff0:ff0 + HEADS], g[..., Z_FG:Z_FG + GROUP_W],
                            g[..., Z_QL:Z_QL + Q_LORA], g[..., Z_KV:Z_KV + KV_LORA],
                            g[..., Z_MISC:Z_MISC + ROPE], g[..., Z_MG:Z_MG + GROUP_W]], axis=-1)


def _perm_w_uq(w):
    lead = w.shape[:-1]
    wh = w.reshape(lead + (HEADS, NOPE + ROPE))
    return jnp.concatenate([wh[..., :NOPE].reshape(lead + (GROUP_W,)),
                            wh[..., NOPE:].reshape(lead + (ROPE_W,))], axis=-1)


def _unperm_w_uq(g):
    lead = g.shape[:-1]
    parts = [g[..., :GROUP_W].reshape(lead + (HEADS, NOPE)), g[..., GROUP_W:].reshape(lead + (HEADS, ROPE))]
    return jnp.concatenate(parts, axis=-1).reshape(lead + (HEADS * (NOPE + ROPE),))


def _perm_w_ukv(w):
    lead = w.shape[:-1]
    wh = w.reshape(lead + (HEADS, 2 * HEAD_DIM))
    return jnp.concatenate([wh[..., :NOPE].reshape(lead + (GROUP_W,)),
                            wh[..., NOPE:].reshape(lead + (GROUP_W,))], axis=-1)


def _unperm_w_ukv(g):
    lead = g.shape[:-1]
    parts = [g[..., :GROUP_W].reshape(lead + (HEADS, NOPE)), g[..., GROUP_W:].reshape(lead + (HEADS, HEAD_DIM))]
    return jnp.concatenate(parts, axis=-1).reshape(lead + (2 * GROUP_W,))


def _rope_tables(positions):
    inv_freq = 1.0 / (ROPE_THETA ** (jnp.arange(0, ROPE, 2, dtype=F32) / ROPE))
    ang = positions.astype(F32)[:, None] * inv_freq
    cos, sin = jnp.cos(ang), jnp.sin(ang)
    reps = LANES // ROPE
    return jnp.tile(jnp.concatenate([cos, cos], axis=1), (1, reps)), jnp.tile(jnp.concatenate([-sin, sin], axis=1), (1, reps))


def _full_weights(g_in, g_uq, g_ukv, g_out):
    def cols(g):
        return g.transpose(1, 0, 2).reshape(g.shape[1], -1)
    return (_perm_w_in(cols(g_in)), _perm_w_uq(cols(g_uq)), _perm_w_ukv(cols(g_ukv)),
            g_out.reshape(-1, g_out.shape[2]))


def _grad_slabs(dw_in, dw_uq, dw_ukv, dw_out):
    def cols(g):
        return g.reshape(g.shape[0], N_DEV, -1).transpose(1, 0, 2)
    return [cols(_unperm_w_in(dw_in)), cols(_unperm_w_uq(dw_uq)), cols(_unperm_w_ukv(dw_ukv)),
            dw_out.reshape(N_DEV, -1, dw_out.shape[1])]


def _local_step(x, mod, positions, loss_target, norm_g, b_f, q_norm_g, kv_norm_g, final_g, weights, shards=None):
    n_l = norm_g.shape[0]
    s_len, d = x.shape
    cos, sin = _rope_tables(positions)
    qb, kb, vb = Z_FQ // LANES, Z_FK // LANES, Z_FV // LANES
    weights = list(weights)

    def pack_rows(a_rows, delta):
        return jnp.concatenate([a_rows.T, delta[:, :HEADS], jnp.zeros((s_len, LANES - 2 * HEADS), F32)], axis=1)

    saved = []
    for l in range(n_l):
        w_in, w_uq, w_ukv, w_out = weights[l]
        mod3 = mod[l].reshape(3, d)
        z, h, qkv = _k_in(x, norm_g[l][None], mod3, w_in)
        fft = z[:, Z_MISC + MISC_FF:Z_MISC + MISC_FF + HEADS].T
        bf = b_f[l][:, None]
        c2 = _k_cum(fft, bf) * LOG2E
        side = (list(shards[l + 1]), [True] * 4) if shards is not None and l + 1 < n_l else None
        of, lse_f, *gathered = _attention_fwd(qkv, qb, qkv, kb, qkv, vb, (c2[:, None, :], c2[:, :, None]), None, False,
                                              "fox_fwd_gather" if side else "fox_fwd", side)
        if side:
            weights.append(_full_weights(*gathered))
        mq, mqr, mk, mv, kr4, qn, kvn = _k_prep(z, cos, sin, q_norm_g[l][None], kv_norm_g[l][None], w_uq, w_ukv)
        om, lse_m = _attention_fwd(mq, 0, mk, 0, mv, 0, None, (mqr, kr4), True, "mla_fwd")
        x_new, y, u = _k_out(of, om, z, x, mod3[2:3], w_out)
        saved.append((x, z, h, qkv, fft, bf, c2, lse_f, mq, mqr, mk, mv, kr4, lse_m, of, om, qn, kvn, y, u, mod3))
        x = x_new

    loss_row, dx, dfinal = _k_loss(x, final_g[None], loss_target)

    grads = {k: [] for k in ("norm_g", "mod", "w_in", "b_f", "q_norm_g", "w_uq", "kv_norm_g", "w_ukv", "w_out")}
    received, pending = {}, None
    for l in range(n_l - 1, -1, -1):
        (x_l, z, h, qkv, fft, bf, c2, lse_f, mq, mqr, mk, mv, kr4, lse_m, of, om, qn, kvn, y, u, mod3) = saved[l]
        w_in, w_uq, w_ukv, w_out = weights[l]
        dyb, dof, dom, dfg, dmg, dlt_f, dlt_m, dgate = _kb_out(dx, y, mod3[2:3], w_out, of, om, z)
        dw_out = _weight_grad(u, [dyb], "dw_out")

        side = (pending, [False] * 4) if pending is not None else None
        dfq, dfk, dfv, dcq, dck, *arrived = _attention_bwd(
            qkv, qb, qkv, kb, qkv, vb, dof, pack_rows(c2 - lse_f.reshape(HEADS, s_len), dlt_f), c2[:, None, :], None,
            False, FOX_SCALE, 1.0 / LOG2E, "fox_bwd_exchange" if side else "fox_bwd", side)
        if side:
            received[l + 1] = arrived
        dfft, dbf = _k_cum_bwd(dcq[:, :HEADS].T + dck.reshape(HEADS, s_len), fft, bf)
        grads["b_f"].append(dbf[:, 0])

        dmq, dkn, dmv, dqr4, dkr4s = _attention_bwd(
            mq, 0, mk, 0, mv, 0, dom, pack_rows(-lse_m.reshape(HEADS, s_len), dlt_m), None, (mqr, kr4), True,
            MLA_SCALE, 1.0 / LOG2E, "mla_bwd")
        dqr = jnp.concatenate([dqr4[0] + dqr4[1], dqr4[2] + dqr4[3]], axis=1)
        dkr4 = dkr4s[0] + dkr4s[1] + dkr4s[2] + dkr4s[3]
        dff = jnp.pad(dfft.T, ((0, 0), (MISC_FF, LANES - MISC_FF - HEADS)))
        dq_b, dz_tail, dgq, dgkv = _kb_prep(dmq, dqr, dkn, dmv, dkr4, dff, z, cos, sin, q_norm_g[l][None],
                                            kv_norm_g[l][None], w_uq, w_ukv)
        grads["q_norm_g"].append(dgq[0])
        grads["kv_norm_g"].append(dgkv[0])
        dw_uq = _weight_grad(qn, [dq_b], "dw_uq")
        dw_ukv = _weight_grad(kvn, [dkn, dmv], "dw_ukv")
        dz = [dfq, dfk, dfv, dfg, dmg, dz_tail]
        dw_in = _weight_grad(h, dz, "dw_in")
        dx, acc3 = _kb_in(dz, w_in, x_l, norm_g[l][None], mod3, dx)
        grads["norm_g"].append(acc3[2])
        grads["mod"].append(jnp.concatenate([acc3[0], acc3[1], dgate[0]]))
        if shards is not None:
            pending = _grad_slabs(dw_in, dw_uq, dw_ukv, dw_out)
        else:
            for name, g in (("w_in", dw_in), ("w_uq", dw_uq), ("w_ukv", dw_ukv), ("w_out", dw_out)):
                grads[name].append(g)
    grads = {k: jnp.stack(v[::-1]) for k, v in grads.items() if v}
    grads["final_g"] = dfinal[0]
    if shards is None:
        return loss_row[0, 0], dx, grads
    return loss_row[0, 0], dx, grads, received, pending


def _pack_small(parts, total):
    flat = jnp.concatenate([p.reshape(-1) for p in parts])
    return jnp.pad(flat, (0, total - flat.shape[0])).reshape(total // LANES, LANES)


def kernel(x, c, positions, norm_g, w_ada, b_ada, w_in, b_f, q_norm_g, w_uq, kv_norm_g, w_ukv, w_out, final_g, loss_target, m_norm_g, m_w_ada, m_b_ada, m_w_in, m_b_f, m_q_norm_g, m_w_uq, m_kv_norm_g, m_w_ukv, m_w_out, m_final_g, v_norm_g, v_w_ada, v_b_ada, v_w_in, v_b_f, v_q_norm_g, v_w_uq, v_kv_norm_g, v_w_ukv, v_w_out, v_final_g):
    n_l, d = norm_g.shape
    me = 4 * lax.axis_index("x") + 2 * lax.axis_index("y") + lax.axis_index("c")
    ada_c = w_ada.shape[2]
    in_c = w_in.shape[2]
    uq_c = w_uq.shape[2]
    ukv_c = w_ukv.shape[2]
    out_r = w_out.shape[1]

    cact = jnp.broadcast_to(jax.nn.silu(c), (N_DEV, d))
    shards = [[w[l].astype(MXU) for w in (w_in, w_uq, w_ukv, w_out)] for l in range(n_l)]
    *g_w0, g_cact = _exchange(shards[0] + [cact], [True] * 5, "gather_layer0")
    cact_all = g_cact[:, 0, :]

    b_cols = lax.dynamic_slice_in_dim(b_ada, me * ada_c, ada_c, axis=1)[:, None, :]
    modpart = _modpart(cact_all, w_ada, b_cols)
    mod_send = jnp.pad(modpart.transpose(1, 0, 2), ((0, 0), (0, 8 - n_l), (0, 0)))
    (mod_recv,) = _exchange([mod_send], [False], "scatter_mod")
    mod = mod_recv.transpose(1, 0, 2).reshape(8, N_DEV * ada_c)[:n_l]

    loss, dx, gr, received, pending = _local_step(x[0], mod, positions[0], loss_target[0], norm_g, b_f, q_norm_g,
                                                  kv_norm_g, final_g, [_full_weights(*g_w0)], shards)

    small_parts = [gr["norm_g"], gr["mod"], gr["b_f"], gr["q_norm_g"], gr["kv_norm_g"], gr["final_g"], cact[0]]
    sizes = [int(np.prod(p.shape)) for p in small_parts]
    total = -(-sum(sizes) // 1024) * 1024
    small = _pack_small(small_parts, total)
    *received[0], r_small = _exchange(pending + [small], [False, False, False, False, True], "exchange_layer0")
    r_in, r_uq, r_ukv, r_out = (jnp.stack([received[l][i] for l in range(n_l)], axis=1).reshape(
        N_DEV, -1, received[0][i].shape[2]) for i in range(4))

    def upd(slots, w, m, v, name):
        shp = w.shape
        w2, m2, v2 = (a.reshape(slots.shape[1:]) for a in (w, m, v))
        return [o.reshape(shp) for o in _adamw(slots, w2, m2, v2, name)]

    o_in = upd(r_in, w_in, m_w_in, v_w_in, "adamw_w_in")
    o_uq = upd(r_uq, w_uq, m_w_uq, v_w_uq, "adamw_w_uq")
    o_ukv = upd(r_ukv, w_ukv, m_w_ukv, v_w_ukv, "adamw_w_ukv")
    o_out = upd(r_out, w_out, m_w_out, v_w_out, "adamw_w_out")

    offs = np.cumsum([0] + sizes)
    flat_all = r_small.reshape(N_DEV, total)
    dmod_all = flat_all[:, offs[1]:offs[2]].reshape(N_DEV, n_l, 3 * d)
    dmod_cols = lax.dynamic_slice_in_dim(dmod_all, me * ada_c, ada_c, axis=2).transpose(1, 0, 2)
    cact_cols = flat_all[:, offs[6]:offs[7]][:, :, None]
    g_ada = _ada_grad(cact_cols, dmod_cols)
    o_ada = upd(g_ada.reshape(1, n_l * d, ada_c), w_ada, m_w_ada, v_w_ada, "adamw_w_ada")

    zero_c = jnp.zeros((d,), F32)
    small_w = [_pack_small([norm_g, b_ada, b_f, q_norm_g, kv_norm_g, final_g, zero_c], total),
               _pack_small([m_norm_g, m_b_ada, m_b_f, m_q_norm_g, m_kv_norm_g, m_final_g, zero_c], total),
               _pack_small([v_norm_g, v_b_ada, v_b_f, v_q_norm_g, v_kv_norm_g, v_final_g, zero_c], total)]
    o_small = [o.reshape(-1) for o in _adamw(r_small, *small_w, "adamw_small")]
    shapes = [norm_g.shape, b_ada.shape, b_f.shape, q_norm_g.shape, kv_norm_g.shape, final_g.shape]

    def small_out(kind, idx):
        return o_small[kind][offs[idx]:offs[idx + 1]].reshape(shapes[idx])

    loss_all = lax.psum(loss, ("x", "y", "c"))
    outs = [loss_all, dx[None]]
    for kind in range(4):
        outs += [small_out(kind, 0), o_ada[kind], small_out(kind, 1), o_in[kind], small_out(kind, 2),
                 small_out(kind, 3), o_uq[kind], small_out(kind, 4), o_ukv[kind], o_out[kind], small_out(kind, 5)]
    return tuple(outs)
```

```python
import jax
import jax.numpy as jnp
import numpy as np
from jax import lax
from jax.experimental import pallas as pl
from jax.experimental.pallas import tpu as pltpu

F32 = jnp.float32
MXU = jnp.bfloat16

N_DEV = 8
HEADS = 8
PAIRS = HEADS // 2
HEAD_DIM = 64
NOPE = 64
ROPE = 32
HALF_ROPE = ROPE // 2
Q_LORA = 256
KV_LORA = 128
CHUNK = 64
GROUP_W = HEADS * HEAD_DIM
ROPE_W = HEADS * ROPE
EPS = 1e-6
ROPE_THETA = 10000.0
N_IN = 2984

Z_FQ, Z_FK, Z_FV, Z_FG, Z_MG, Z_QL, Z_KV, Z_MISC, Z_W = 0, 512, 1024, 1536, 2048, 2560, 2816, 2944, 3072
MISC_FF = ROPE
TAIL_W = Z_W - Z_QL
T_KV, T_MISC = Q_LORA, Q_LORA + KV_LORA

ADAM_LR = 0.001
ADAM_B1 = 0.9
ADAM_B2 = 0.999
ADAM_EPS = 1e-08
ADAM_WD = 0.01
ADAM_STEP = 10

VMEM_LIMIT_V7X = 56 * 1024 * 1024
LANES = 128
ATTN_TILE = 512
V_ROWS = HEAD_DIM + 16
LOG2E = 1.4426950408889634
FOX_SCALE = HEAD_DIM ** -0.5
MLA_SCALE = (NOPE + ROPE) ** -0.5

_NT = (((1,), (1,)), ((), ()))
_TN = (((0,), (0,)), ((), ()))


def _params(*sem, side_effects=False):
    return pltpu.CompilerParams(dimension_semantics=sem, vmem_limit_bytes=VMEM_LIMIT_V7X,
                                has_side_effects=side_effects)


def _sds(shape, dtype=F32):
    return jax.ShapeDtypeStruct(shape, dtype)


def _full(shape):
    nd = len(shape)
    return pl.BlockSpec(shape, lambda *_: (0,) * nd)


def _rows(tm, width, col=0):
    return pl.BlockSpec((tm, width), lambda i: (i, col))


def _exchange(arrs, gather, name):
    n = len(arrs)

    def kern(*refs):
        copies = _exchange_copies(refs[:n], refs[n:2 * n], gather, *refs[2 * n:])
        _exchange_start(copies)
        _exchange_wait(copies)

    return pl.pallas_call(
        kern, name=name, out_shape=_exchange_out_shapes(arrs, gather),
        in_specs=[pl.BlockSpec(memory_space=pl.ANY)] * n,
        out_specs=[pl.BlockSpec(memory_space=pl.ANY)] * n,
        scratch_shapes=_exchange_sems(n),
        compiler_params=pltpu.CompilerParams(has_side_effects=True),
    )(*arrs)


def _exchange_out_shapes(arrs, gather):
    return [_sds((N_DEV,) + tuple(a.shape) if g else tuple(a.shape), a.dtype) for a, g in zip(arrs, gather)]


def _exchange_sems(n):
    return [pltpu.SemaphoreType.DMA((n, N_DEV)), pltpu.SemaphoreType.DMA((n, N_DEV)), pltpu.SemaphoreType.DMA((n,))]


def _exchange_copies(ins, outs, gather, send_sems, recv_sems, loc_sems, recv=True):
    n = len(ins)
    x, y, c = lax.axis_index("x"), lax.axis_index("y"), lax.axis_index("c")
    me = 4 * x + 2 * y + c

    def src(i, j):
        return ins[i] if gather[i] else ins[i].at[j]

    local = [pltpu.make_async_copy(src(i, me), outs[i].at[me], loc_sems.at[i]) for i in range(n)]
    sends, recvs = [], []
    for k in range(1, N_DEV):
        px = 1 - x if k & 4 else x
        py = 1 - y if k & 2 else y
        pc = 1 - c if k & 1 else c
        p = 4 * px + 2 * py + pc
        for i in range(n):
            sends.append(pltpu.make_async_remote_copy(
                src_ref=src(i, p), dst_ref=outs[i].at[me], send_sem=send_sems.at[i, k],
                recv_sem=recv_sems.at[i, k], device_id=(px, py, pc), device_id_type=pl.DeviceIdType.MESH))
            if recv:
                recvs.append(pltpu.make_async_remote_copy(
                    src_ref=src(i, p), dst_ref=outs[i].at[p], send_sem=send_sems.at[i, k],
                    recv_sem=recv_sems.at[i, k], device_id=(px, py, pc), device_id_type=pl.DeviceIdType.MESH))
    return local, sends, recvs


def _exchange_start(copies):
    local, sends, _ = copies
    for cp in local + sends:
        cp.start()


def _exchange_wait(copies):
    local, sends, recvs = copies
    for cp in recvs:
        cp.wait_recv()
    for cp in sends:
        cp.wait_send()
    for cp in local:
        cp.wait()


def _modpart(cact8, w_ada, b_cols):
    n_l, d, cw = w_ada.shape

    def kern(c_ref, w_ref, b_ref, o_ref):
        o_ref[0] = jnp.dot(c_ref[...].astype(MXU), w_ref[0].astype(MXU), preferred_element_type=F32) + b_ref[0]

    return pl.pallas_call(
        kern, name="modpart", grid=(n_l,), out_shape=_sds((n_l, N_DEV, cw)),
        in_specs=[_full((N_DEV, d)), pl.BlockSpec((1, d, cw), lambda l: (l, 0, 0)),
                  pl.BlockSpec((1, 1, cw), lambda l: (l, 0, 0))],
        out_specs=pl.BlockSpec((1, N_DEV, cw), lambda l: (l, 0, 0)),
        compiler_params=_params("arbitrary"),
    )(cact8, w_ada, b_cols)


def _ada_grad(cact_cols, dmod_cols):
    n_l, _, cw = dmod_cols.shape
    d = cact_cols.shape[1]

    def kern(c_ref, dm_ref, o_ref):
        acc = c_ref[0] * dm_ref[0, 0:1, :]
        for s in range(1, N_DEV):
            acc = acc + c_ref[s] * dm_ref[0, s:s + 1, :]
        o_ref[0] = acc

    return pl.pallas_call(
        kern, name="ada_grad", grid=(n_l,), out_shape=_sds((n_l, d, cw)),
        in_specs=[_full((N_DEV, d, 1)), pl.BlockSpec((1, N_DEV, cw), lambda l: (l, 0, 0))],
        out_specs=pl.BlockSpec((1, d, cw), lambda l: (l, 0, 0)),
        compiler_params=_params("arbitrary"),
    )(cact_cols, dmod_cols)


def _k_in(x, g, mod3, w, tm=256):
    s_len, d = x.shape
    qkv_w = 3 * GROUP_W

    def kern(x_ref, g_ref, mod_ref, w_ref, h_ref, qkv_ref, gates_ref, tail_ref):
        xv = x_ref[...]
        r = lax.rsqrt(jnp.mean(xv * xv, axis=-1, keepdims=True) + EPS)
        xn = xv * r * g_ref[...]
        h = (xn * (1.0 + mod_ref[1:2, :]) + mod_ref[0:1, :]).astype(MXU)
        h_ref[...] = h
        z = jnp.dot(h, w_ref[...], preferred_element_type=F32)
        qkv_ref[:, :GROUP_W] = (z[:, Z_FQ:Z_FQ + GROUP_W] * (FOX_SCALE * LOG2E)).astype(MXU)
        qkv_ref[:, GROUP_W:] = z[:, Z_FK:Z_FK + 2 * GROUP_W].astype(MXU)
        gates_ref[...] = z[:, Z_FG:Z_QL]
        tail_ref[...] = z[:, Z_QL:]

    return pl.pallas_call(
        kern, name="k_in", grid=(s_len // tm,),
        out_shape=[_sds((s_len, d), MXU), _sds((s_len, qkv_w), MXU), _sds((s_len, Z_QL - Z_FG)),
                   _sds((s_len, TAIL_W))],
        in_specs=[_rows(tm, d), _full((1, d)), _full((3, d)), _full((d, Z_W))],
        out_specs=[_rows(tm, d), _rows(tm, qkv_w), _rows(tm, Z_QL - Z_FG), _rows(tm, TAIL_W)],
        compiler_params=_params("arbitrary"),
    )(x, g, mod3, w)


def _k_cum(fft, bf):
    nh, s_len = fft.shape

    def kern(ff_ref, b_ref, cum_ref):
        r_i = lax.broadcasted_iota(jnp.int32, (LANES, LANES), 0)
        c_i = lax.broadcasted_iota(jnp.int32, (LANES, LANES), 1)
        upper = (r_i <= c_i).astype(F32)
        carry = jnp.zeros((nh, 1), F32)
        for cb in range(s_len // LANES):
            sl = slice(cb * LANES, (cb + 1) * LANES)
            xc = ff_ref[:, sl] + b_ref[...]
            lf = jnp.minimum(xc, 0.0) - jnp.log(1.0 + jnp.exp(-jnp.abs(xc)))
            cum_ref[:, sl] = jnp.dot(lf, upper, precision=lax.Precision.HIGHEST,
                                     preferred_element_type=F32) + carry
            carry = carry + jnp.sum(lf, axis=1, keepdims=True)

    return pl.pallas_call(
        kern, name="k_cum", out_shape=_sds((nh, s_len)),
        in_specs=[pl.BlockSpec(memory_space=pltpu.VMEM)] * 2,
        out_specs=pl.BlockSpec(memory_space=pltpu.VMEM),
        compiler_params=_params(),
    )(fft, bf)


def _k_cum_bwd(dck, fft, bf):
    nh, s_len = fft.shape

    def kern(dc_ref, ff_ref, b_ref, dff_ref, db_ref):
        r_i = lax.broadcasted_iota(jnp.int32, (LANES, LANES), 0)
        c_i = lax.broadcasted_iota(jnp.int32, (LANES, LANES), 1)
        lower = (r_i >= c_i).astype(F32)
        carry = jnp.zeros((nh, 1), F32)
        db = jnp.zeros((nh, 1), F32)
        for cb in range(s_len // LANES - 1, -1, -1):
            sl = slice(cb * LANES, (cb + 1) * LANES)
            dc = dc_ref[:, sl]
            dlf = jnp.dot(dc, lower, precision=lax.Precision.HIGHEST, preferred_element_type=F32) + carry
            carry = carry + jnp.sum(dc, axis=1, keepdims=True)
            dff = dlf * jax.nn.sigmoid(-(ff_ref[:, sl] + b_ref[...]))
            dff_ref[:, sl] = dff
            db = db + jnp.sum(dff, axis=1, keepdims=True)
        db_ref[...] = jnp.broadcast_to(db, (nh, LANES))

    return pl.pallas_call(
        kern, name="k_cum_bwd", out_shape=[_sds((nh, s_len)), _sds((nh, LANES))],
        in_specs=[pl.BlockSpec(memory_space=pltpu.VMEM)] * 3,
        out_specs=[pl.BlockSpec(memory_space=pltpu.VMEM)] * 2,
        compiler_params=_params(),
    )(dck, fft, bf)


def _swap16(t):
    lane = lax.broadcasted_iota(jnp.int32, t.shape, 1)
    return jnp.where(lane % ROPE < HALF_ROPE, pltpu.roll(t, LANES - HALF_ROPE, 1), pltpu.roll(t, HALF_ROPE, 1))


def _rope(t, cos, sin):
    return t * cos + _swap16(t) * sin


def _rope_bwd(dt, cos, sin):
    return dt * cos - _swap16(dt) * sin


def _k_prep(tail, cos, sin, gq, gkv, wuq, wukv, tm=512):
    s_len = tail.shape[0]
    qc = MLA_SCALE * LOG2E

    def kern(tail_ref, cos_ref, sin_ref, gq_ref, gkv_ref, wuq_ref, wukv_ref,
             qn_out, qr_out, kn_out, v_out, kr_out, qn_ref, kvn_ref):
        cs, sn = cos_ref[...], sin_ref[...]
        ql = tail_ref[:, :T_KV]
        rq = lax.rsqrt(jnp.mean(ql * ql, axis=-1, keepdims=True) + EPS)
        qn = (ql * rq * gq_ref[...]).astype(MXU)
        qn_ref[...] = qn
        q = jnp.dot(qn, wuq_ref[...], preferred_element_type=F32)
        qn_out[...] = (q[:, :GROUP_W] * qc).astype(MXU)
        for half in range(ROPE_W // LANES):
            lo = GROUP_W + half * LANES
            qr_out[:, half * LANES:(half + 1) * LANES] = (_rope(q[:, lo:lo + LANES], cs, sn) * qc).astype(MXU)
        kvl = tail_ref[:, T_KV:T_MISC]
        rk = lax.rsqrt(jnp.mean(kvl * kvl, axis=-1, keepdims=True) + EPS)
        kvn = (kvl * rk * gkv_ref[...]).astype(MXU)
        kvn_ref[...] = kvn
        kv = jnp.dot(kvn, wukv_ref[...], preferred_element_type=F32)
        kn_out[...] = kv[:, :GROUP_W].astype(MXU)
        v_out[...] = kv[:, GROUP_W:].astype(MXU)
        misc = tail_ref[:, T_MISC:]
        lane = lax.broadcasted_iota(jnp.int32, misc.shape, 1)
        kr = jnp.where(lane < ROPE, _rope(misc, cs, sn), 0.0)
        kr4 = kr
        for rep in range(1, LANES // ROPE):
            kr4 = kr4 + pltpu.roll(kr, rep * ROPE, 1)
        kr_out[...] = kr4.astype(MXU)

    return pl.pallas_call(
        kern, name="k_prep", grid=(s_len // tm,),
        out_shape=[_sds((s_len, GROUP_W), MXU), _sds((s_len, ROPE_W), MXU), _sds((s_len, GROUP_W), MXU),
                   _sds((s_len, GROUP_W), MXU), _sds((s_len, LANES), MXU), _sds((s_len, Q_LORA), MXU),
                   _sds((s_len, KV_LORA), MXU)],
        in_specs=[_rows(tm, TAIL_W), _rows(tm, LANES), _rows(tm, LANES),
                  _full((1, Q_LORA)), _full((1, KV_LORA)), _full((Q_LORA, GROUP_W + ROPE_W)),
                  _full((KV_LORA, 2 * GROUP_W))],
        out_specs=[_rows(tm, GROUP_W), _rows(tm, ROPE_W), _rows(tm, GROUP_W), _rows(tm, GROUP_W), _rows(tm, LANES),
                   _rows(tm, Q_LORA), _rows(tm, KV_LORA)],
        compiler_params=_params("arbitrary"),
    )(tail, cos, sin, gq, gkv, wuq, wukv)


def _tile_mask(t, chunk_mask, transposed):
    row = lax.broadcasted_iota(jnp.int32, (t, t), 0)
    col = lax.broadcasted_iota(jnp.int32, (t, t), 1)
    qi, ki = (col, row) if transposed else (row, col)
    if chunk_mask:
        return (ki // CHUNK) <= (qi // CHUNK)
    return ki <= qi


def _keep_head(x, hh, axis, rope_group):
    idx = lax.broadcasted_iota(jnp.int32, x.shape, axis)
    keep = (idx >= hh * HEAD_DIM) & (idx < (hh + 1) * HEAD_DIM)
    if x.shape[axis] != LANES:
        keep = keep | ((idx >= LANES + rope_group * ROPE) & (idx < LANES + (rope_group + 1) * ROPE))
    return jnp.where(keep, x, jnp.zeros_like(x))


def _attention_fwd(q, q_blk, k, k_blk, v, v_blk, bias, rope, chunk_mask, name, side=None):
    s_len = q.shape[0]
    t = min(ATTN_TILE, s_len // 2)
    nq = s_len // t
    n_side = len(side[0]) if side else 0

    def kern(*refs):
        q_ref, k_ref, v_ref = refs[:3]
        pos = 3
        if bias is not None:
            cq_ref, ck_ref = refs[pos:pos + 2]
            pos += 2
        if rope is not None:
            qr_ref, kr_ref = refs[pos:pos + 2]
            pos += 2
        side_in = refs[pos:pos + n_side]
        pos += n_side
        o_ref, lse_ref = refs[pos:pos + 2]
        side_out = refs[pos + 2:pos + 2 + n_side]
        vt_scr, m_scr, acc_scr = refs[pos + 2 + n_side:pos + 5 + n_side]
        sems = refs[pos + 5 + n_side:]
        pj = pl.program_id(0)
        if n_side:
            @pl.when(pj == 0)
            def _():
                _exchange_start(_exchange_copies(side_in, side_out, side[1], *sems, recv=False))
        vt_scr[:, HEAD_DIM:, :] = jnp.ones((2, V_ROWS - HEAD_DIM, s_len), vt_scr.dtype)
        for i in range(nq):
            vtt = v_ref[i * t:(i + 1) * t, :].T
            for hh in range(2):
                vt_scr[hh, :HEAD_DIM, i * t:(i + 1) * t] = vtt[hh * HEAD_DIM:(hh + 1) * HEAD_DIM, :]

        def qbody(qi, _):
            qs = pl.multiple_of(qi * t, t)
            qt = q_ref[pl.ds(qs, t), :]
            if rope is not None:
                qt = jnp.concatenate([qt, qr_ref[pl.ds(qs, t), :]], axis=1)
            qh = [_keep_head(qt, hh, 1, (pj % 2) * 2 + hh) for hh in range(2)]
            m_scr[...] = jnp.full(m_scr.shape, -jnp.inf, F32)
            acc_scr[...] = jnp.zeros(acc_scr.shape, F32)

            def step(ki, masked):
                ks = pl.multiple_of(ki * t, t)
                kt = k_ref[pl.ds(ks, t), :]
                if rope is not None:
                    kt = jnp.concatenate([kt, kr_ref[pl.ds(ks, t), :]], axis=1)
                for hh in range(2):
                    st = lax.dot_general(kt, qh[hh], _NT, preferred_element_type=F32)
                    if bias is not None:
                        st = st + cq_ref[hh, :, pl.ds(qs, t)] - ck_ref[hh, pl.ds(ks, t), :]
                    if masked:
                        st = jnp.where(_tile_mask(t, chunk_mask, True), st, -jnp.inf)
                    m_old = m_scr[hh]
                    m_new = jnp.maximum(m_old, jnp.max(st, axis=0, keepdims=True))
                    pt = jnp.exp2(st - m_new)
                    alpha = jnp.exp2(m_old - m_new)
                    acc_scr[hh] = alpha * acc_scr[hh] + jnp.dot(vt_scr[hh, :, pl.ds(ks, t)], pt.astype(MXU),
                                                                preferred_element_type=F32)
                    m_scr[hh] = m_new

            def loop_body(ki, carry):
                step(ki, False)
                return carry

            lax.fori_loop(0, qi, loop_body, 0)
            step(qi, True)
            outs = []
            for hh in range(2):
                acc = acc_scr[hh]
                l = acc[HEAD_DIM:HEAD_DIM + 1, :]
                outs.append(acc[:HEAD_DIM, :] / l)
                lse_ref[hh, :, pl.ds(qs, t)] = m_scr[hh] + jnp.log2(l)
            o_ref[pl.ds(qs, t), :] = jnp.concatenate(outs, axis=0).T
            return 0

        lax.fori_loop(0, nq, qbody, 0)
        if n_side:
            @pl.when(pj == PAIRS - 1)
            def _():
                _exchange_wait(_exchange_copies(side_in, side_out, side[1], *sems))

    def tok(blk):
        return pl.BlockSpec((s_len, LANES), lambda j: (0, blk + j))

    rowb = pl.BlockSpec((2, 1, s_len), lambda j: (j, 0, 0))
    hbm = pl.BlockSpec(memory_space=pl.ANY)
    ins = [q, k, v]
    in_specs = [tok(q_blk), tok(k_blk), tok(v_blk)]
    if bias is not None:
        ins += list(bias)
        in_specs += [rowb, pl.BlockSpec((2, s_len, 1), lambda j: (j, 0, 0))]
    if rope is not None:
        ins += list(rope)
        in_specs += [pl.BlockSpec((s_len, LANES), lambda j: (0, j // 2)), _full((s_len, LANES))]
    out_shape = [_sds((s_len, PAIRS * LANES)), _sds((HEADS, 1, s_len))]
    scratch = [pltpu.VMEM((2, V_ROWS, s_len), v.dtype), pltpu.VMEM((2, 1, t), F32), pltpu.VMEM((2, V_ROWS, t), F32)]
    if n_side:
        ins += list(side[0])
        out_shape += _exchange_out_shapes(*side)
        scratch += _exchange_sems(n_side)
    return pl.pallas_call(
        kern, name=name, grid=(PAIRS,), out_shape=out_shape,
        in_specs=in_specs + [hbm] * n_side, out_specs=[tok(0), rowb] + [hbm] * n_side,
        scratch_shapes=scratch,
        compiler_params=_params("arbitrary", side_effects=bool(n_side)),
    )(*ins)


def _attention_bwd(q, q_blk, k, k_blk, v, v_blk, do, pack, ck_row, rope, chunk_mask, q_scale, k_scale, name,
                   side=None):
    s_len = q.shape[0]
    t = min(ATTN_TILE, s_len // 2)
    nq = s_len // t
    has_bias = ck_row is not None
    kw = 2 * LANES if rope is not None else LANES
    n_side = len(side[0]) if side else 0

    def kern(*refs):
        q_ref, k_ref, v_ref, do_ref, pack_ref = refs[:5]
        pos = 5
        if has_bias:
            ck_ref = refs[pos]
            pos += 1
        if rope is not None:
            qr_ref, kr_ref = refs[pos:pos + 2]
            pos += 2
        side_in = refs[pos:pos + n_side]
        pos += n_side
        dq_ref, dk_ref, dv_ref = refs[pos:pos + 3]
        pos += 3
        if has_bias:
            dcq_ref, dck_ref = refs[pos:pos + 2]
            pos += 2
        if rope is not None:
            dqr_ref, dkr_ref = refs[pos:pos + 2]
            pos += 2
        side_out = refs[pos:pos + n_side]
        pos += n_side
        qt_scr, dot_scr, dkt_scr, dvt_scr, dq_scr, dcq_scr = refs[pos:pos + 6]
        sems = refs[pos + 6:]
        pj = pl.program_id(0)
        if n_side:
            @pl.when(pj == 0)
            def _():
                _exchange_start(_exchange_copies(side_in, side_out, side[1], *sems, recv=False))

        for i in range(nq):
            sl = slice(i * t, (i + 1) * t)
            qt_scr[:LANES, sl] = q_ref[sl, :].T
            dot_scr[:, sl] = do_ref[sl, :].T
            if rope is not None:
                qt_scr[LANES:, sl] = qr_ref[sl, :].T
        dkt_scr[...] = jnp.zeros(dkt_scr.shape, F32)
        dvt_scr[...] = jnp.zeros(dvt_scr.shape, F32)
        if has_bias:
            dck_ref[...] = jnp.zeros(dck_ref.shape, F32)

            @pl.when(pj == 0)
            def _():
                dcq_ref[...] = jnp.zeros(dcq_ref.shape, F32)

        def keep(x, hh, axis):
            return _keep_head(x, hh, axis, (pj % 2) * 2 + hh)

        def qbody(qi, _):
            qs = pl.multiple_of(qi * t, t)
            qt = q_ref[pl.ds(qs, t), :]
            if rope is not None:
                qt = jnp.concatenate([qt, qr_ref[pl.ds(qs, t), :]], axis=1)
            dot = do_ref[pl.ds(qs, t), :]
            pk = pack_ref[pl.ds(qs, t), :]
            lane = lax.broadcasted_iota(jnp.int32, pk.shape, 1)
            qh = [keep(qt, hh, 1) for hh in range(2)]
            doh = [keep(dot, hh, 1) for hh in range(2)]
            a_col = [jnp.sum(jnp.where(lane == 2 * pj + hh, pk, 0.0), axis=1, keepdims=True) for hh in range(2)]
            d_col = [jnp.sum(jnp.where(lane == HEADS + 2 * pj + hh, pk, 0.0), axis=1, keepdims=True)
                     for hh in range(2)]
            dq_scr[...] = jnp.zeros(dq_scr.shape, F32)
            if has_bias:
                dcq_scr[...] = jnp.zeros(dcq_scr.shape, F32)

            def step(ki, masked):
                ks = pl.multiple_of(ki * t, t)
                kt = k_ref[pl.ds(ks, t), :]
                if rope is not None:
                    kt = jnp.concatenate([kt, kr_ref[pl.ds(ks, t), :]], axis=1)
                vt = v_ref[pl.ds(ks, t), :]
                dq_acc = dq_scr[...]
                for hh in range(2):
                    rows = slice(hh * HEAD_DIM, (hh + 1) * HEAD_DIM)
                    s = lax.dot_general(qh[hh], kt, _NT, preferred_element_type=F32) + a_col[hh]
                    if has_bias:
                        s = s - ck_ref[hh, :, pl.ds(ks, t)]
                    p = jnp.exp2(s)
                    if masked:
                        p = jnp.where(_tile_mask(t, chunk_mask, False), p, 0.0)
                    dpd = lax.dot_general(doh[hh], vt, _NT, preferred_element_type=F32)
                    ds = p * (dpd - d_col[hh])
                    dsb = ds.astype(MXU)
                    dvt_scr[rows, pl.ds(ks, t)] += jnp.dot(dot_scr[rows, pl.ds(qs, t)], p.astype(MXU),
                                                           preferred_element_type=F32)
                    dkt_scr[rows, pl.ds(ks, t)] += jnp.dot(qt_scr[rows, pl.ds(qs, t)], dsb,
                                                           preferred_element_type=F32)
                    if rope is not None:
                        rr = pl.ds(pl.multiple_of(LANES + ((pj % 2) * 2 + hh) * ROPE, ROPE), ROPE)
                        dkt_scr[rr, pl.ds(ks, t)] += jnp.dot(qt_scr[rr, pl.ds(qs, t)], dsb,
                                                             preferred_element_type=F32)
                    dq_acc = dq_acc + jnp.dot(dsb, keep(kt, hh, 1), preferred_element_type=F32)
                    if has_bias:
                        dcq_scr[hh] += jnp.sum(ds, axis=1, keepdims=True)
                        dck_ref[hh, :, pl.ds(ks, t)] += -jnp.sum(ds, axis=0, keepdims=True)
                dq_scr[...] = dq_acc

            def loop_body(ki, carry):
                step(ki, False)
                return carry

            lax.fori_loop(0, qi, loop_body, 0)
            step(qi, True)
            dq_ref[pl.ds(qs, t), :] = (dq_scr[:, :LANES] * q_scale).astype(dq_ref.dtype)
            if rope is not None:
                dqr_ref[0, pl.ds(qs, t), :] = dq_scr[:, LANES:] * q_scale
            if has_bias:
                old = dcq_ref[pl.ds(qs, t), :]
                dcq_ref[pl.ds(qs, t), :] = jnp.where(lane == 2 * pj, dcq_scr[0],
                                                     jnp.where(lane == 2 * pj + 1, dcq_scr[1], old))
            return 0

        lax.fori_loop(0, nq, qbody, 0)
        for i in range(nq):
            sl = slice(i * t, (i + 1) * t)
            dk_ref[sl, :] = (dkt_scr[:LANES, sl].T * k_scale).astype(dk_ref.dtype)
            dv_ref[sl, :] = dvt_scr[:, sl].T.astype(dv_ref.dtype)
            if rope is not None:
                dkr_ref[0, sl, :] = dkt_scr[LANES:, sl].T * k_scale
        if n_side:
            @pl.when(pj == PAIRS - 1)
            def _():
                _exchange_wait(_exchange_copies(side_in, side_out, side[1], *sems))

    def tok(blk):
        return pl.BlockSpec((s_len, LANES), lambda j: (0, blk + j))

    shared = _full((s_len, LANES))
    rowb = pl.BlockSpec((2, 1, s_len), lambda j: (j, 0, 0))
    slab = pl.BlockSpec((1, s_len, LANES), lambda j: (j, 0, 0))
    hbm = pl.BlockSpec(memory_space=pl.ANY)
    ins = [q, k, v, do, pack]
    in_specs = [tok(q_blk), tok(k_blk), tok(v_blk), tok(0), shared]
    out_shape = [_sds((s_len, PAIRS * LANES), MXU)] * 3
    out_specs = [tok(0)] * 3
    if has_bias:
        ins.append(ck_row)
        in_specs.append(rowb)
        out_shape += [_sds((s_len, LANES)), _sds((HEADS, 1, s_len))]
        out_specs += [shared, rowb]
    if rope is not None:
        ins += list(rope)
        in_specs += [pl.BlockSpec((s_len, LANES), lambda j: (0, j // 2)), shared]
        out_shape += [_sds((PAIRS, s_len, LANES))] * 2
        out_specs += [slab, slab]
    scratch = [pltpu.VMEM((kw, s_len), q.dtype), pltpu.VMEM((LANES, s_len), do.dtype),
               pltpu.VMEM((kw, s_len), F32), pltpu.VMEM((LANES, s_len), F32),
               pltpu.VMEM((t, kw), F32), pltpu.VMEM((2, t, 1), F32)]
    if n_side:
        ins += list(side[0])
        out_shape += _exchange_out_shapes(*side)
        scratch += _exchange_sems(n_side)
    return pl.pallas_call(
        kern, name=name, grid=(PAIRS,), out_shape=out_shape,
        in_specs=in_specs + [hbm] * n_side, out_specs=out_specs + [hbm] * n_side, scratch_shapes=scratch,
        compiler_params=_params("arbitrary", side_effects=bool(n_side)),
    )(*ins)


def _silu(a):
    return a * jax.nn.sigmoid(a)


def _k_out(of, om, gates, x, gate, wout, tm=256):
    s_len, d = x.shape

    def kern(of_ref, om_ref, gates_ref, x_ref, gate_ref, w_ref, xo_ref, y_ref, u_ref):
        u_ref[:, :GROUP_W] = (of_ref[...] * _silu(gates_ref[:, :GROUP_W])).astype(MXU)
        u_ref[:, GROUP_W:] = (om_ref[...] * _silu(gates_ref[:, GROUP_W:])).astype(MXU)
        y = jnp.dot(u_ref[...], w_ref[...], preferred_element_type=F32)
        y_ref[...] = y
        xo_ref[...] = x_ref[...] + gate_ref[...] * y

    return pl.pallas_call(
        kern, name="k_out", grid=(s_len // tm,),
        out_shape=[_sds((s_len, d)), _sds((s_len, d)), _sds((s_len, 2 * GROUP_W), MXU)],
        in_specs=[_rows(tm, GROUP_W), _rows(tm, GROUP_W), _rows(tm, 2 * GROUP_W), _rows(tm, d), _full((1, d)),
                  _full((2 * GROUP_W, d))],
        out_specs=[_rows(tm, d), _rows(tm, d), _rows(tm, 2 * GROUP_W)],
        compiler_params=_params("arbitrary"),
    )(of, om, gates, x, gate, wout)


def _k_loss(x, gf, tgt, tm=256):
    s_len, d = x.shape

    def kern(x_ref, g_ref, t_ref, loss_ref, dx_ref, dg_ref):
        i = pl.program_id(0)
        xv = x_ref[...]
        r = lax.rsqrt(jnp.mean(xv * xv, axis=-1, keepdims=True) + EPS)
        xh = xv * r
        diff = xh * g_ref[...] - t_ref[...]
        part = 0.5 * jnp.sum(jnp.mean(diff * diff, axis=-1, keepdims=True))
        dout = diff * (1.0 / d)
        dxh = dout * g_ref[...]
        dx_ref[...] = r * (dxh - xh * jnp.mean(dxh * xh, axis=-1, keepdims=True))

        @pl.when(i == 0)
        def _():
            loss_ref[...] = jnp.zeros_like(loss_ref)
            dg_ref[...] = jnp.zeros_like(dg_ref)

        loss_ref[...] += jnp.full(loss_ref.shape, part, F32)
        dg_ref[...] += jnp.sum(dout * xh, axis=0, keepdims=True)

    return pl.pallas_call(
        kern, name="k_loss", grid=(s_len // tm,),
        out_shape=[_sds((1, LANES)), _sds((s_len, d)), _sds((1, d))],
        in_specs=[_rows(tm, d), _full((1, d)), _rows(tm, d)],
        out_specs=[_full((1, LANES)), _rows(tm, d), _full((1, d))],
        compiler_params=_params("arbitrary"),
    )(x, gf, tgt)


def _kb_out(dxo, y, gate, wout, of, om, gates, tm=256):
    s_len, d = dxo.shape

    def kern(dxo_ref, y_ref, gate_ref, wt_ref, of_ref, om_ref, gates_ref,
             dy_ref, dof_ref, dom_ref, dfg_ref, dmg_ref, dlf_ref, dlm_ref, dgate_ref):
        i = pl.program_id(0)
        dxv = dxo_ref[...]

        @pl.when(i == 0)
        def _():
            dgate_ref[...] = jnp.zeros_like(dgate_ref)

        dgate_ref[...] += jnp.sum(dxv * y_ref[...], axis=0, keepdims=True)
        dyb = (dxv * gate_ref[...]).astype(MXU)
        dy_ref[...] = dyb
        du = lax.dot_general(dyb, wt_ref[...], _NT, preferred_element_type=F32)
        head_of = (lax.broadcasted_iota(jnp.int32, (GROUP_W, LANES), 0) // HEAD_DIM
                   == lax.broadcasted_iota(jnp.int32, (GROUP_W, LANES), 1)).astype(F32)
        for du_g, o_ref, a, do_ref, dg_ref, dl_ref in (
                (du[:, :GROUP_W], of_ref, gates_ref[:, :GROUP_W], dof_ref, dfg_ref, dlf_ref),
                (du[:, GROUP_W:], om_ref, gates_ref[:, GROUP_W:], dom_ref, dmg_ref, dlm_ref)):
            sg = jax.nn.sigmoid(a)
            ov = o_ref[...]
            dov = du_g * (a * sg)
            do_ref[...] = dov.astype(MXU)
            dg_ref[...] = (du_g * ov * (sg * (1.0 + a * (1.0 - sg)))).astype(MXU)
            dl_ref[...] = jnp.dot(dov * ov, head_of, precision=lax.Precision.HIGHEST, preferred_element_type=F32)

    return pl.pallas_call(
        kern, name="kb_out", grid=(s_len // tm,),
        out_shape=[_sds((s_len, d), MXU), _sds((s_len, GROUP_W), MXU), _sds((s_len, GROUP_W), MXU),
                   _sds((s_len, GROUP_W), MXU), _sds((s_len, GROUP_W), MXU), _sds((s_len, LANES)),
                   _sds((s_len, LANES)), _sds((1, d))],
        in_specs=[_rows(tm, d), _rows(tm, d), _full((1, d)), _full((2 * GROUP_W, d)), _rows(tm, GROUP_W),
                  _rows(tm, GROUP_W), _rows(tm, 2 * GROUP_W)],
        out_specs=[_rows(tm, d), _rows(tm, GROUP_W), _rows(tm, GROUP_W), _rows(tm, GROUP_W),
                   _rows(tm, GROUP_W), _rows(tm, LANES), _rows(tm, LANES), _full((1, d))],
        compiler_params=_params("arbitrary"),
    )(dxo, y, gate, wout, of, om, gates)


def _kb_prep(dqn, dqr, dkn, dv, dkr4, dff, tail, cos, sin, gq, gkv, wuq_t, wukv_t, tm=512):
    s_len = tail.shape[0]
    qw = GROUP_W + ROPE_W

    def kern(dqn_ref, dqr_ref, dkn_ref, dv_ref, dkr_ref, dff_ref, tail_ref, cos_ref, sin_ref,
             gq_ref, gkv_ref, wuqt_ref, wukvt_ref, dq_ref, dz_ref, dgq_ref, dgkv_ref):
        i = pl.program_id(0)

        @pl.when(i == 0)
        def _():
            dgq_ref[...] = jnp.zeros_like(dgq_ref)
            dgkv_ref[...] = jnp.zeros_like(dgkv_ref)

        cs, sn = cos_ref[...], sin_ref[...]
        dq_ref[:, :GROUP_W] = dqn_ref[...]
        for half in range(ROPE_W // LANES):
            sl = slice(half * LANES, (half + 1) * LANES)
            dq_ref[:, GROUP_W + half * LANES:GROUP_W + (half + 1) * LANES] = _rope_bwd(dqr_ref[:, sl], cs, sn).astype(MXU)
        dqn = lax.dot_general(dq_ref[...], wuqt_ref[...], _NT, preferred_element_type=F32)
        ql = tail_ref[:, :T_KV]
        rq = lax.rsqrt(jnp.mean(ql * ql, axis=-1, keepdims=True) + EPS)
        qh = ql * rq
        dgq_ref[...] += jnp.sum(dqn * qh, axis=0, keepdims=True)
        dqh = dqn * gq_ref[...]
        dz_ref[:, :Q_LORA] = (rq * (dqh - qh * jnp.mean(dqh * qh, axis=-1, keepdims=True))).astype(MXU)

        dkvn = (lax.dot_general(dkn_ref[...], wukvt_ref[:, :GROUP_W], _NT, preferred_element_type=F32)
                + lax.dot_general(dv_ref[...], wukvt_ref[:, GROUP_W:], _NT, preferred_element_type=F32))
        kvl = tail_ref[:, T_KV:T_MISC]
        rk = lax.rsqrt(jnp.mean(kvl * kvl, axis=-1, keepdims=True) + EPS)
        kh = kvl * rk
        dgkv_ref[...] += jnp.sum(dkvn * kh, axis=0, keepdims=True)
        dkh = dkvn * gkv_ref[...]
        dz_ref[:, Q_LORA:Q_LORA + KV_LORA] = (
            rk * (dkh - kh * jnp.mean(dkh * kh, axis=-1, keepdims=True))).astype(MXU)

        g4 = dkr_ref[...]
        g = g4
        for rep in range(1, LANES // ROPE):
            g = g + pltpu.roll(g4, rep * ROPE, 1)
        lane = lax.broadcasted_iota(jnp.int32, g.shape, 1)
        dmisc = jnp.where(lane < ROPE, _rope_bwd(g, cs, sn), 0.0) + dff_ref[...]
        dz_ref[:, Q_LORA + KV_LORA:] = dmisc.astype(MXU)

    return pl.pallas_call(
        kern, name="kb_prep", grid=(s_len // tm,),
        out_shape=[_sds((s_len, qw), MXU), _sds((s_len, TAIL_W), MXU), _sds((1, Q_LORA)), _sds((1, KV_LORA))],
        in_specs=[_rows(tm, GROUP_W), _rows(tm, ROPE_W), _rows(tm, GROUP_W), _rows(tm, GROUP_W), _rows(tm, LANES),
                  _rows(tm, LANES), _rows(tm, TAIL_W),
                  _rows(tm, LANES), _rows(tm, LANES), _full((1, Q_LORA)), _full((1, KV_LORA)),
                  _full((Q_LORA, qw)), _full((KV_LORA, 2 * GROUP_W))],
        out_specs=[_rows(tm, qw), _rows(tm, TAIL_W), _full((1, Q_LORA)), _full((1, KV_LORA))],
        compiler_params=_params("arbitrary"),
    )(dqn, dqr, dkn, dv, dkr4, dff, tail, cos, sin, gq, gkv, wuq_t, wukv_t)


def _kb_in(dz_pieces, w, x, g, mod3, dxo, tm=256):
    s_len, d = x.shape
    widths = [p.shape[1] for p in dz_pieces]
    n_p = len(widths)

    def kern(*refs):
        dz_refs = refs[:n_p]
        w_ref, x_ref, g_ref, mod_ref, dxo_ref, dx_ref, acc_ref = refs[n_p:]
        i = pl.program_id(0)

        @pl.when(i == 0)
        def _():
            acc_ref[...] = jnp.zeros_like(acc_ref)

        dh = jnp.zeros((tm, d), F32)
        lo = 0
        for p_ref, wd in zip(dz_refs, widths):
            dh = dh + lax.dot_general(p_ref[...], w_ref[:, lo:lo + wd], _NT, preferred_element_type=F32)
            lo += wd
        xv = x_ref[...]
        r = lax.rsqrt(jnp.mean(xv * xv, axis=-1, keepdims=True) + EPS)
        xh = xv * r
        xn = xh * g_ref[...]
        dxn = dh * (1.0 + mod_ref[1:2, :])
        acc_ref[0:1, :] += jnp.sum(dh, axis=0, keepdims=True)
        acc_ref[1:2, :] += jnp.sum(dh * xn, axis=0, keepdims=True)
        acc_ref[2:3, :] += jnp.sum(dxn * xh, axis=0, keepdims=True)
        dxh = dxn * g_ref[...]
        dx_ref[...] = dxo_ref[...] + r * (dxh - xh * jnp.mean(dxh * xh, axis=-1, keepdims=True))

    return pl.pallas_call(
        kern, name="kb_in", grid=(s_len // tm,),
        out_shape=[_sds((s_len, d)), _sds((3, d))],
        in_specs=[_rows(tm, wd) for wd in widths] + [_full((d, Z_W)), _rows(tm, d), _full((1, d)), _full((3, d)),
                                                     _rows(tm, d)],
        out_specs=[_rows(tm, d), _full((3, d))],
        compiler_params=_params("arbitrary"),
    )(*dz_pieces, w, x, g, mod3, dxo)


def _weight_grad(a, pieces, name, tk=512):
    s_len, m = a.shape
    widths = [p.shape[1] for p in pieces]
    n = sum(widths)
    tk = min(tk, s_len)

    def kern(a_ref, *refs):
        o_ref = refs[-1]

        @pl.when(pl.program_id(0) == 0)
        def _():
            o_ref[...] = jnp.zeros_like(o_ref)

        at = a_ref[...]
        lo = 0
        for p_ref, w in zip(refs[:-1], widths):
            o_ref[:, lo:lo + w] += lax.dot_general(at, p_ref[...], _TN, preferred_element_type=F32)
            lo += w

    return pl.pallas_call(
        kern, name=name, grid=(s_len // tk,), out_shape=_sds((m, n)),
        in_specs=[_rows(tk, m)] + [_rows(tk, w) for w in widths],
        out_specs=_full((m, n)),
        compiler_params=_params("arbitrary"),
    )(a, *pieces)


def _adamw(slabs, w, m, v, name):
    n_l = len(slabs)
    n, r, c = slabs[0].shape
    tm = r
    for cand in (256, 128, 64, 32, 16, 8):
        if r % cand == 0:
            tm = cand
            break
    steps = r // tm

    def kern(*refs):
        g_refs = refs[:n_l]
        w_ref, m_ref, v_ref, go_ref, d_ref, mo_ref, vo_ref, g_scr = refs[n_l:]
        for ll in range(n_l):
            @pl.when(pl.program_id(0) == ll)
            def _(g_ref=g_refs[ll]):
                g = g_ref[0].astype(F32)
                for s in range(1, n):
                    g = g + g_ref[s].astype(F32)
                g_scr[...] = g

        g = g_scr[...]
        m_new = ADAM_B1 * m_ref[...] + (1.0 - ADAM_B1) * g
        v_new = ADAM_B2 * v_ref[...] + (1.0 - ADAM_B2) * (g * g)
        m_hat = m_new / (1.0 - ADAM_B1 ** ADAM_STEP)
        v_hat = v_new / (1.0 - ADAM_B2 ** ADAM_STEP)
        go_ref[...] = g
        mo_ref[...] = m_new
        vo_ref[...] = v_new
        d_ref[...] = -ADAM_LR * (m_hat / (jnp.sqrt(v_hat) + ADAM_EPS) + ADAM_WD * w_ref[...])

    row = pl.BlockSpec((tm, c), lambda l, i: (l * steps + i, 0))

    def slab_spec(ll):
        return pl.BlockSpec((n, tm, c), lambda l, i: (0, jnp.where(l == ll, i, 0), 0))

    return pl.pallas_call(
        kern, name=name, grid=(n_l, steps), out_shape=[_sds((n_l * r, c))] * 4,
        in_specs=[slab_spec(ll) for ll in range(n_l)] + [row, row, row],
        out_specs=[row] * 4,
        scratch_shapes=[pltpu.VMEM((tm, c), F32)],
        compiler_params=_params("arbitrary", "arbitrary"),
    )(*slabs, w, m, v)


def _perm_w_in(w):
    pad = jnp.zeros(w.shape[:-1] + (Z_W - Z_MISC - ROPE - HEADS,), w.dtype)
    return jnp.concatenate([w[..., 0:1536], w[..., 1544:2056], w[..., 2472:2984], w[..., 2056:2312],
                            w[..., 2312:2440], w[..., 2440:2472], w[..., 1536:1544], pad], axis=-1)


def _unperm_w_in(g):
    ff0 = Z_MISC + MISC_FF
    return jnp.concatenate([g[..., 0:1536], g[..., ff0:ff0 + HEADS], g[..., Z_FG:Z_FG + GROUP_W],
                            g[..., Z_QL:Z_QL + Q_LORA], g[..., Z_KV:Z_KV + KV_LORA],
                            g[..., Z_MISC:Z_MISC + ROPE], g[..., Z_MG:Z_MG + GROUP_W]], axis=-1)


def _perm_w_uq(w):
    lead = w.shape[:-1]
    wh = w.reshape(lead + (HEADS, NOPE + ROPE))
    return jnp.concatenate([wh[..., :NOPE].reshape(lead + (GROUP_W,)),
                            wh[..., NOPE:].reshape(lead + (ROPE_W,))], axis=-1)


def _unperm_w_uq(g):
    lead = g.shape[:-1]
    parts = [g[..., :GROUP_W].reshape(lead + (HEADS, NOPE)), g[..., GROUP_W:].reshape(lead + (HEADS, ROPE))]
    return jnp.concatenate(parts, axis=-1).reshape(lead + (HEADS * (NOPE + ROPE),))


def _perm_w_ukv(w):
    lead = w.shape[:-1]
    wh = w.reshape(lead + (HEADS, 2 * HEAD_DIM))
    return jnp.concatenate([wh[..., :NOPE].reshape(lead + (GROUP_W,)),
                            wh[..., NOPE:].reshape(lead + (GROUP_W,))], axis=-1)


def _unperm_w_ukv(g):
    lead = g.shape[:-1]
    parts = [g[..., :GROUP_W].reshape(lead + (HEADS, NOPE)), g[..., GROUP_W:].reshape(lead + (HEADS, HEAD_DIM))]
    return jnp.concatenate(parts, axis=-1).reshape(lead + (2 * GROUP_W,))


def _rope_tables(positions):
    inv_freq = 1.0 / (ROPE_THETA ** (jnp.arange(0, ROPE, 2, dtype=F32) / ROPE))
    ang = positions.astype(F32)[:, None] * inv_freq
    cos, sin = jnp.cos(ang), jnp.sin(ang)
    reps = LANES // ROPE
    return jnp.tile(jnp.concatenate([cos, cos], axis=1), (1, reps)), jnp.tile(jnp.concatenate([-sin, sin], axis=1), (1, reps))


def _full_weights(g_in, g_uq, g_ukv, g_out):
    def cols(g):
        return g.transpose(1, 0, 2).reshape(g.shape[1], -1)
    return (_perm_w_in(cols(g_in)), _perm_w_uq(cols(g_uq)), _perm_w_ukv(cols(g_ukv)),
            g_out.reshape(-1, g_out.shape[2]))


def _grad_slabs(dw_in, dw_uq, dw_ukv, dw_out):
    def cols(g):
        return g.reshape(g.shape[0], N_DEV, -1).transpose(1, 0, 2)
    return [cols(_unperm_w_in(dw_in)), cols(_unperm_w_uq(dw_uq)), cols(_unperm_w_ukv(dw_ukv)),
            dw_out.reshape(N_DEV, -1, dw_out.shape[1])]


def _local_step(x, mod, positions, loss_target, norm_g, b_f, q_norm_g, kv_norm_g, final_g, weights, shards=None):
    n_l = norm_g.shape[0]
    s_len, d = x.shape
    cos, sin = _rope_tables(positions)
    qb, kb, vb = Z_FQ // LANES, Z_FK // LANES, Z_FV // LANES
    weights = list(weights)

    def pack_rows(a_rows, delta):
        return jnp.concatenate([a_rows.T, delta[:, :HEADS], jnp.zeros((s_len, LANES - 2 * HEADS), F32)], axis=1)

    saved = []
    for l in range(n_l):
        w_in, w_uq, w_ukv, w_out = weights[l]
        mod3 = mod[l].reshape(3, d)
        h, qkv, gates, tail = _k_in(x, norm_g[l][None], mod3, w_in)
        fft = tail[:, T_MISC + MISC_FF:T_MISC + MISC_FF + HEADS].T
        bf = b_f[l][:, None]
        c2 = _k_cum(fft, bf) * LOG2E
        side = (list(shards[l + 1]), [True] * 4) if shards is not None and l + 1 < n_l else None
        of, lse_f, *gathered = _attention_fwd(qkv, qb, qkv, kb, qkv, vb, (c2[:, None, :], c2[:, :, None]), None, False,
                                              "fox_fwd_gather" if side else "fox_fwd", side)
        if side:
            weights.append(_full_weights(*gathered))
        mq, mqr, mk, mv, kr4, qn, kvn = _k_prep(tail, cos, sin, q_norm_g[l][None], kv_norm_g[l][None], w_uq, w_ukv)
        om, lse_m = _attention_fwd(mq, 0, mk, 0, mv, 0, None, (mqr, kr4), True, "mla_fwd")
        x_new, y, u = _k_out(of, om, gates, x, mod3[2:3], w_out)
        saved.append((x, gates, tail, h, qkv, fft, bf, c2, lse_f, mq, mqr, mk, mv, kr4, lse_m, of, om, qn, kvn, y, u,
                      mod3))
        x = x_new

    loss_row, dx, dfinal = _k_loss(x, final_g[None], loss_target)

    grads = {k: [] for k in ("norm_g", "mod", "w_in", "b_f", "q_norm_g", "w_uq", "kv_norm_g", "w_ukv", "w_out")}
    received, pending = {}, None
    for l in range(n_l - 1, -1, -1):
        (x_l, gates, tail, h, qkv, fft, bf, c2, lse_f, mq, mqr, mk, mv, kr4, lse_m, of, om, qn, kvn, y, u,
         mod3) = saved[l]
        w_in, w_uq, w_ukv, w_out = weights[l]
        dyb, dof, dom, dfg, dmg, dlt_f, dlt_m, dgate = _kb_out(dx, y, mod3[2:3], w_out, of, om, gates)
        dw_out = _weight_grad(u, [dyb], "dw_out")

        side = (pending, [False] * 4) if pending is not None else None
        dfq, dfk, dfv, dcq, dck, *arrived = _attention_bwd(
            qkv, qb, qkv, kb, qkv, vb, dof, pack_rows(c2 - lse_f.reshape(HEADS, s_len), dlt_f), c2[:, None, :], None,
            False, FOX_SCALE, 1.0 / LOG2E, "fox_bwd_exchange" if side else "fox_bwd", side)
        if side:
            received[l + 1] = arrived
        dfft, dbf = _k_cum_bwd(dcq[:, :HEADS].T + dck.reshape(HEADS, s_len), fft, bf)
        grads["b_f"].append(dbf[:, 0])

        dmq, dkn, dmv, dqr4, dkr4s = _attention_bwd(
            mq, 0, mk, 0, mv, 0, dom, pack_rows(-lse_m.reshape(HEADS, s_len), dlt_m), None, (mqr, kr4), True,
            MLA_SCALE, 1.0 / LOG2E, "mla_bwd")
        dqr = jnp.concatenate([dqr4[0] + dqr4[1], dqr4[2] + dqr4[3]], axis=1)
        dkr4 = dkr4s[0] + dkr4s[1] + dkr4s[2] + dkr4s[3]
        dff = jnp.pad(dfft.T, ((0, 0), (MISC_FF, LANES - MISC_FF - HEADS)))
        dq_b, dz_tail, dgq, dgkv = _kb_prep(dmq, dqr, dkn, dmv, dkr4, dff, tail, cos, sin, q_norm_g[l][None],
                                            kv_norm_g[l][None], w_uq, w_ukv)
        grads["q_norm_g"].append(dgq[0])
        grads["kv_norm_g"].append(dgkv[0])
        dw_uq = _weight_grad(qn, [dq_b], "dw_uq")
        dw_ukv = _weight_grad(kvn, [dkn, dmv], "dw_ukv")
        dz = [dfq, dfk, dfv, dfg, dmg, dz_tail]
        dw_in = _weight_grad(h, dz, "dw_in")
        dx, acc3 = _kb_in(dz, w_in, x_l, norm_g[l][None], mod3, dx)
        grads["norm_g"].append(acc3[2])
        grads["mod"].append(jnp.concatenate([acc3[0], acc3[1], dgate[0]]))
        if shards is not None:
            pending = _grad_slabs(dw_in, dw_uq, dw_ukv, dw_out)
        else:
            for name, g in (("w_in", dw_in), ("w_uq", dw_uq), ("w_ukv", dw_ukv), ("w_out", dw_out)):
                grads[name].append(g)
    grads = {k: jnp.stack(v[::-1]) for k, v in grads.items() if v}
    grads["final_g"] = dfinal[0]
    if shards is None:
        return loss_row[0, 0], dx, grads
    return loss_row[0, 0], dx, grads, received, pending


def _pack_small(parts, total):
    flat = jnp.concatenate([p.reshape(-1) for p in parts])
    return jnp.pad(flat, (0, total - flat.shape[0])).reshape(total // LANES, LANES)


def kernel(x, c, positions, norm_g, w_ada, b_ada, w_in, b_f, q_norm_g, w_uq, kv_norm_g, w_ukv, w_out, final_g, loss_target, m_norm_g, m_w_ada, m_b_ada, m_w_in, m_b_f, m_q_norm_g, m_w_uq, m_kv_norm_g, m_w_ukv, m_w_out, m_final_g, v_norm_g, v_w_ada, v_b_ada, v_w_in, v_b_f, v_q_norm_g, v_w_uq, v_kv_norm_g, v_w_ukv, v_w_out, v_final_g):
    n_l, d = norm_g.shape
    me = 4 * lax.axis_index("x") + 2 * lax.axis_index("y") + lax.axis_index("c")
    ada_c = w_ada.shape[2]

    cact = jnp.broadcast_to(jax.nn.silu(c), (N_DEV, d))
    shards = [[w[l].astype(MXU) for w in (w_in, w_uq, w_ukv, w_out)] for l in range(n_l)]
    *g_w0, g_cact = _exchange(shards[0] + [cact], [True] * 5, "gather_layer0")
    cact_all = g_cact[:, 0, :]

    b_cols = lax.dynamic_slice_in_dim(b_ada, me * ada_c, ada_c, axis=1)[:, None, :]
    modpart = _modpart(cact_all, w_ada, b_cols)
    mod_send = jnp.pad(modpart.transpose(1, 0, 2), ((0, 0), (0, 8 - n_l), (0, 0)))
    (mod_recv,) = _exchange([mod_send], [False], "scatter_mod")
    mod = mod_recv.transpose(1, 0, 2).reshape(8, N_DEV * ada_c)[:n_l]

    loss, dx, gr, received, pending = _local_step(x[0], mod, positions[0], loss_target[0], norm_g, b_f, q_norm_g,
                                                  kv_norm_g, final_g, [_full_weights(*g_w0)], shards)

    small_parts = [gr["norm_g"], gr["mod"], gr["b_f"], gr["q_norm_g"], gr["kv_norm_g"], gr["final_g"], cact[0]]
    sizes = [int(np.prod(p.shape)) for p in small_parts]
    total = -(-sum(sizes) // 1024) * 1024
    small = _pack_small(small_parts, total)
    *received[0], r_small = _exchange([p.astype(MXU) for p in pending] + [small],
                                      [False, False, False, False, True], "exchange_layer0")
    r_in, r_uq, r_ukv, r_out = ([received[l][i] for l in range(n_l)] for i in range(4))

    def upd(slabs, w, m, v, name):
        shp = w.shape
        w2, m2, v2 = (a.reshape(-1, slabs[0].shape[2]) for a in (w, m, v))
        return [o.reshape(shp) for o in _adamw(slabs, w2, m2, v2, name)]

    o_in = upd(r_in, w_in, m_w_in, v_w_in, "adamw_w_in")
    o_uq = upd(r_uq, w_uq, m_w_uq, v_w_uq, "adamw_w_uq")
    o_ukv = upd(r_ukv, w_ukv, m_w_ukv, v_w_ukv, "adamw_w_ukv")
    o_out = upd(r_out, w_out, m_w_out, v_w_out, "adamw_w_out")

    offs = np.cumsum([0] + sizes)
    flat_all = r_small.reshape(N_DEV, total)
    dmod_all = flat_all[:, offs[1]:offs[2]].reshape(N_DEV, n_l, 3 * d)
    dmod_cols = lax.dynamic_slice_in_dim(dmod_all, me * ada_c, ada_c, axis=2).transpose(1, 0, 2)
    cact_cols = flat_all[:, offs[6]:offs[7]][:, :, None]
    g_ada = _ada_grad(cact_cols, dmod_cols)
    o_ada = upd([g_ada.reshape(1, n_l * d, ada_c)], w_ada, m_w_ada, v_w_ada, "adamw_w_ada")

    zero_c = jnp.zeros((d,), F32)
    small_w = [_pack_small([norm_g, b_ada, b_f, q_norm_g, kv_norm_g, final_g, zero_c], total),
               _pack_small([m_norm_g, m_b_ada, m_b_f, m_q_norm_g, m_kv_norm_g, m_final_g, zero_c], total),
               _pack_small([v_norm_g, v_b_ada, v_b_f, v_q_norm_g, v_kv_norm_g, v_final_g, zero_c], total)]
    o_small = [o.reshape(-1) for o in _adamw([r_small], *small_w, "adamw_small")]
    shapes = [norm_g.shape, b_ada.shape, b_f.shape, q_norm_g.shape, kv_norm_g.shape, final_g.shape]

    def small_out(kind, idx):
        return o_small[kind][offs[idx]:offs[idx + 1]].reshape(shapes[idx])

    loss_all = lax.psum(loss, ("x", "y", "c"))
    outs = [loss_all, dx[None]]
    for kind in range(4):
        outs += [small_out(kind, 0), o_ada[kind], small_out(kind, 1), o_in[kind], small_out(kind, 2),
                 small_out(kind, 3), o_uq[kind], small_out(kind, 4), o_ukv[kind], o_out[kind], small_out(kind, 5)]
    return tuple(outs)
```

```python
import jax
import jax.numpy as jnp
import numpy as np
from jax import lax
from jax.experimental import pallas as pl
from jax.experimental.pallas import tpu as pltpu

F32 = jnp.float32
MXU = jnp.bfloat16

N_DEV = 8
HEADS = 8
PAIRS = HEADS // 2
HEAD_DIM = 64
NOPE = 64
ROPE = 32
HALF_ROPE = ROPE // 2
Q_LORA = 256
KV_LORA = 128
CHUNK = 64
GROUP_W = HEADS * HEAD_DIM
ROPE_W = HEADS * ROPE
EPS = 1e-6
ROPE_THETA = 10000.0
N_IN = 2984

Z_FQ, Z_FK, Z_FV, Z_FG, Z_MG, Z_QL, Z_KV, Z_MISC, Z_W = 0, 512, 1024, 1536, 2048, 2560, 2816, 2944, 3072
MISC_FF = ROPE
TAIL_W = Z_W - Z_QL
T_KV, T_MISC = Q_LORA, Q_LORA + KV_LORA

ADAM_LR = 0.001
ADAM_B1 = 0.9
ADAM_B2 = 0.999
ADAM_EPS = 1e-08
ADAM_WD = 0.01
ADAM_STEP = 10

VMEM_LIMIT_V7X = 56 * 1024 * 1024
LANES = 128
ATTN_TILE = 512
V_ROWS = HEAD_DIM + 16
LOG2E = 1.4426950408889634
FOX_SCALE = HEAD_DIM ** -0.5
MLA_SCALE = (NOPE + ROPE) ** -0.5

_NT = (((1,), (1,)), ((), ()))
_TN = (((0,), (0,)), ((), ()))


def _params(*sem, side_effects=False):
    return pltpu.CompilerParams(dimension_semantics=sem, vmem_limit_bytes=VMEM_LIMIT_V7X,
                                has_side_effects=side_effects)


def _sds(shape, dtype=F32):
    return jax.ShapeDtypeStruct(shape, dtype)


def _full(shape):
    nd = len(shape)
    return pl.BlockSpec(shape, lambda *_: (0,) * nd)


def _rows(tm, width, col=0):
    return pl.BlockSpec((tm, width), lambda i: (i, col))


def _exchange(arrs, gather, name):
    n = len(arrs)

    def kern(*refs):
        copies = _exchange_copies(refs[:n], refs[n:2 * n], gather, *refs[2 * n:])
        _exchange_start(copies)
        _exchange_wait(copies)

    return pl.pallas_call(
        kern, name=name, out_shape=_exchange_out_shapes(arrs, gather),
        in_specs=[pl.BlockSpec(memory_space=pl.ANY)] * n,
        out_specs=[pl.BlockSpec(memory_space=pl.ANY)] * n,
        scratch_shapes=_exchange_sems(n),
        compiler_params=pltpu.CompilerParams(has_side_effects=True),
    )(*arrs)


def _exchange_out_shapes(arrs, gather):
    return [_sds((N_DEV,) + tuple(a.shape) if g else tuple(a.shape), a.dtype) for a, g in zip(arrs, gather)]


def _exchange_sems(n):
    return [pltpu.SemaphoreType.DMA((n, N_DEV)), pltpu.SemaphoreType.DMA((n, N_DEV)), pltpu.SemaphoreType.DMA((n,))]


def _exchange_copies(ins, outs, gather, send_sems, recv_sems, loc_sems, recv=True):
    n = len(ins)
    x, y, c = lax.axis_index("x"), lax.axis_index("y"), lax.axis_index("c")
    me = 4 * x + 2 * y + c

    def src(i, j):
        return ins[i] if gather[i] else ins[i].at[j]

    local = [pltpu.make_async_copy(src(i, me), outs[i].at[me], loc_sems.at[i]) for i in range(n)]
    sends, recvs = [], []
    for k in range(1, N_DEV):
        px = 1 - x if k & 4 else x
        py = 1 - y if k & 2 else y
        pc = 1 - c if k & 1 else c
        p = 4 * px + 2 * py + pc
        for i in range(n):
            sends.append(pltpu.make_async_remote_copy(
                src_ref=src(i, p), dst_ref=outs[i].at[me], send_sem=send_sems.at[i, k],
                recv_sem=recv_sems.at[i, k], device_id=(px, py, pc), device_id_type=pl.DeviceIdType.MESH))
            if recv:
                recvs.append(pltpu.make_async_remote_copy(
                    src_ref=src(i, p), dst_ref=outs[i].at[p], send_sem=send_sems.at[i, k],
                    recv_sem=recv_sems.at[i, k], device_id=(px, py, pc), device_id_type=pl.DeviceIdType.MESH))
    return local, sends, recvs


def _exchange_start(copies):
    local, sends, _ = copies
    for cp in local + sends:
        cp.start()


def _exchange_wait(copies):
    local, sends, recvs = copies
    for cp in recvs:
        cp.wait_recv()
    for cp in sends:
        cp.wait_send()
    for cp in local:
        cp.wait()


def _modpart(cact8, w_ada, b_cols):
    n_l, d, cw = w_ada.shape

    def kern(c_ref, w_ref, b_ref, o_ref):
        o_ref[0] = jnp.dot(c_ref[...].astype(MXU), w_ref[0].astype(MXU), preferred_element_type=F32) + b_ref[0]

    return pl.pallas_call(
        kern, name="modpart", grid=(n_l,), out_shape=_sds((n_l, N_DEV, cw)),
        in_specs=[_full((N_DEV, d)), pl.BlockSpec((1, d, cw), lambda l: (l, 0, 0)),
                  pl.BlockSpec((1, 1, cw), lambda l: (l, 0, 0))],
        out_specs=pl.BlockSpec((1, N_DEV, cw), lambda l: (l, 0, 0)),
        compiler_params=_params("arbitrary"),
    )(cact8, w_ada, b_cols)


def _ada_grad(cact_cols, dmod_cols):
    n_l, _, cw = dmod_cols.shape
    d = cact_cols.shape[1]

    def kern(c_ref, dm_ref, o_ref):
        acc = c_ref[0] * dm_ref[0, 0:1, :]
        for s in range(1, N_DEV):
            acc = acc + c_ref[s] * dm_ref[0, s:s + 1, :]
        o_ref[0] = acc

    return pl.pallas_call(
        kern, name="ada_grad", grid=(n_l,), out_shape=_sds((n_l, d, cw)),
        in_specs=[_full((N_DEV, d, 1)), pl.BlockSpec((1, N_DEV, cw), lambda l: (l, 0, 0))],
        out_specs=pl.BlockSpec((1, d, cw), lambda l: (l, 0, 0)),
        compiler_params=_params("arbitrary"),
    )(cact_cols, dmod_cols)


def _k_in(x, g, mod3, w, tm=256):
    s_len, d = x.shape
    qkv_w = 3 * GROUP_W

    def kern(x_ref, g_ref, mod_ref, w_ref, h_ref, qkv_ref, gates_ref, tail_ref):
        xv = x_ref[...]
        r = lax.rsqrt(jnp.mean(xv * xv, axis=-1, keepdims=True) + EPS)
        xn = xv * r * g_ref[...]
        h = (xn * (1.0 + mod_ref[1:2, :]) + mod_ref[0:1, :]).astype(MXU)
        h_ref[...] = h
        z = jnp.dot(h, w_ref[...], preferred_element_type=F32)
        qkv_ref[:, :GROUP_W] = (z[:, Z_FQ:Z_FQ + GROUP_W] * (FOX_SCALE * LOG2E)).astype(MXU)
        qkv_ref[:, GROUP_W:] = z[:, Z_FK:Z_FK + 2 * GROUP_W].astype(MXU)
        gates_ref[...] = z[:, Z_FG:Z_QL]
        tail_ref[...] = z[:, Z_QL:]

    return pl.pallas_call(
        kern, name="k_in", grid=(s_len // tm,),
        out_shape=[_sds((s_len, d), MXU), _sds((s_len, qkv_w), MXU), _sds((s_len, Z_QL - Z_FG)),
                   _sds((s_len, TAIL_W))],
        in_specs=[_rows(tm, d), _full((1, d)), _full((3, d)), _full((d, Z_W))],
        out_specs=[_rows(tm, d), _rows(tm, qkv_w), _rows(tm, Z_QL - Z_FG), _rows(tm, TAIL_W)],
        compiler_params=_params("arbitrary"),
    )(x, g, mod3, w)


def _k_cum(fft, bf):
    nh, s_len = fft.shape

    def kern(ff_ref, b_ref, cum_ref):
        r_i = lax.broadcasted_iota(jnp.int32, (LANES, LANES), 0)
        c_i = lax.broadcasted_iota(jnp.int32, (LANES, LANES), 1)
        upper = (r_i <= c_i).astype(F32)
        carry = jnp.zeros((nh, 1), F32)
        for cb in range(s_len // LANES):
            sl = slice(cb * LANES, (cb + 1) * LANES)
            xc = ff_ref[:, sl] + b_ref[...]
            lf = jnp.minimum(xc, 0.0) - jnp.log(1.0 + jnp.exp(-jnp.abs(xc)))
            cum_ref[:, sl] = jnp.dot(lf, upper, precision=lax.Precision.HIGHEST,
                                     preferred_element_type=F32) + carry
            carry = carry + jnp.sum(lf, axis=1, keepdims=True)

    return pl.pallas_call(
        kern, name="k_cum", out_shape=_sds((nh, s_len)),
        in_specs=[pl.BlockSpec(memory_space=pltpu.VMEM)] * 2,
        out_specs=pl.BlockSpec(memory_space=pltpu.VMEM),
        compiler_params=_params(),
    )(fft, bf)


def _k_cum_bwd(dck, fft, bf):
    nh, s_len = fft.shape

    def kern(dc_ref, ff_ref, b_ref, dff_ref, db_ref):
        r_i = lax.broadcasted_iota(jnp.int32, (LANES, LANES), 0)
        c_i = lax.broadcasted_iota(jnp.int32, (LANES, LANES), 1)
        lower = (r_i >= c_i).astype(F32)
        carry = jnp.zeros((nh, 1), F32)
        db = jnp.zeros((nh, 1), F32)
        for cb in range(s_len // LANES - 1, -1, -1):
            sl = slice(cb * LANES, (cb + 1) * LANES)
            dc = dc_ref[:, sl]
            dlf = jnp.dot(dc, lower, precision=lax.Precision.HIGHEST, preferred_element_type=F32) + carry
            carry = carry + jnp.sum(dc, axis=1, keepdims=True)
            dff = dlf * jax.nn.sigmoid(-(ff_ref[:, sl] + b_ref[...]))
            dff_ref[:, sl] = dff
            db = db + jnp.sum(dff, axis=1, keepdims=True)
        db_ref[...] = jnp.broadcast_to(db, (nh, LANES))

    return pl.pallas_call(
        kern, name="k_cum_bwd", out_shape=[_sds((nh, s_len)), _sds((nh, LANES))],
        in_specs=[pl.BlockSpec(memory_space=pltpu.VMEM)] * 3,
        out_specs=[pl.BlockSpec(memory_space=pltpu.VMEM)] * 2,
        compiler_params=_params(),
    )(dck, fft, bf)


def _swap16(t):
    lane = lax.broadcasted_iota(jnp.int32, t.shape, 1)
    return jnp.where(lane % ROPE < HALF_ROPE, pltpu.roll(t, LANES - HALF_ROPE, 1), pltpu.roll(t, HALF_ROPE, 1))


def _rope(t, cos, sin):
    return t * cos + _swap16(t) * sin


def _rope_bwd(dt, cos, sin):
    return dt * cos - _swap16(dt) * sin


def _k_prep(tail, cos, sin, gq, gkv, wuq, wukv, tm=512):
    s_len = tail.shape[0]
    qc = MLA_SCALE * LOG2E

    def kern(tail_ref, cos_ref, sin_ref, gq_ref, gkv_ref, wuq_ref, wukv_ref,
             qn_out, qr_out, kn_out, v_out, kr_out, qn_ref, kvn_ref):
        cs, sn = cos_ref[...], sin_ref[...]
        ql = tail_ref[:, :T_KV]
        rq = lax.rsqrt(jnp.mean(ql * ql, axis=-1, keepdims=True) + EPS)
        qn = (ql * rq * gq_ref[...]).astype(MXU)
        qn_ref[...] = qn
        q = jnp.dot(qn, wuq_ref[...], preferred_element_type=F32)
        qn_out[...] = (q[:, :GROUP_W] * qc).astype(MXU)
        for half in range(ROPE_W // LANES):
            lo = GROUP_W + half * LANES
            qr_out[:, half * LANES:(half + 1) * LANES] = (_rope(q[:, lo:lo + LANES], cs, sn) * qc).astype(MXU)
        kvl = tail_ref[:, T_KV:T_MISC]
        rk = lax.rsqrt(jnp.mean(kvl * kvl, axis=-1, keepdims=True) + EPS)
        kvn = (kvl * rk * gkv_ref[...]).astype(MXU)
        kvn_ref[...] = kvn
        kv = jnp.dot(kvn, wukv_ref[...], preferred_element_type=F32)
        kn_out[...] = kv[:, :GROUP_W].astype(MXU)
        v_out[...] = kv[:, GROUP_W:].astype(MXU)
        misc = tail_ref[:, T_MISC:]
        lane = lax.broadcasted_iota(jnp.int32, misc.shape, 1)
        kr = jnp.where(lane < ROPE, _rope(misc, cs, sn), 0.0)
        kr4 = kr
        for rep in range(1, LANES // ROPE):
            kr4 = kr4 + pltpu.roll(kr, rep * ROPE, 1)
        kr_out[...] = kr4.astype(MXU)

    return pl.pallas_call(
        kern, name="k_prep", grid=(s_len // tm,),
        out_shape=[_sds((s_len, GROUP_W), MXU), _sds((s_len, ROPE_W), MXU), _sds((s_len, GROUP_W), MXU),
                   _sds((s_len, GROUP_W), MXU), _sds((s_len, LANES), MXU), _sds((s_len, Q_LORA), MXU),
                   _sds((s_len, KV_LORA), MXU)],
        in_specs=[_rows(tm, TAIL_W), _rows(tm, LANES), _rows(tm, LANES),
                  _full((1, Q_LORA)), _full((1, KV_LORA)), _full((Q_LORA, GROUP_W + ROPE_W)),
                  _full((KV_LORA, 2 * GROUP_W))],
        out_specs=[_rows(tm, GROUP_W), _rows(tm, ROPE_W), _rows(tm, GROUP_W), _rows(tm, GROUP_W), _rows(tm, LANES),
                   _rows(tm, Q_LORA), _rows(tm, KV_LORA)],
        compiler_params=_params("arbitrary"),
    )(tail, cos, sin, gq, gkv, wuq, wukv)


def _tile_mask(t, chunk_mask, transposed):
    row = lax.broadcasted_iota(jnp.int32, (t, t), 0)
    col = lax.broadcasted_iota(jnp.int32, (t, t), 1)
    qi, ki = (col, row) if transposed else (row, col)
    if chunk_mask:
        return (ki // CHUNK) <= (qi // CHUNK)
    return ki <= qi


def _keep_head(x, hh, axis, rope_group):
    idx = lax.broadcasted_iota(jnp.int32, x.shape, axis)
    keep = (idx >= hh * HEAD_DIM) & (idx < (hh + 1) * HEAD_DIM)
    if x.shape[axis] != LANES:
        keep = keep | ((idx >= LANES + rope_group * ROPE) & (idx < LANES + (rope_group + 1) * ROPE))
    return jnp.where(keep, x, jnp.zeros_like(x))


def _attention_fwd(q, q_blk, k, k_blk, v, v_blk, bias, rope, chunk_mask, name, side=None):
    s_len = q.shape[0]
    t = min(ATTN_TILE, s_len // 2)
    nq = s_len // t
    n_side = len(side[0]) if side else 0

    def kern(*refs):
        q_ref, k_ref, v_ref = refs[:3]
        pos = 3
        if bias is not None:
            ck_ref = refs[pos]
            pos += 1
        if rope is not None:
            qr_ref, kr_ref = refs[pos:pos + 2]
            pos += 2
        side_in = refs[pos:pos + n_side]
        pos += n_side
        o_ref, lse_ref = refs[pos:pos + 2]
        side_out = refs[pos + 2:pos + 2 + n_side]
        vt_scr, m_scr, acc_scr, ck_scr = refs[pos + 2 + n_side:pos + 6 + n_side]
        sems = refs[pos + 6 + n_side:]
        pj = pl.program_id(0)
        if n_side:
            @pl.when(pj == 0)
            def _():
                _exchange_start(_exchange_copies(side_in, side_out, side[1], *sems, recv=False))
        vt_scr[:, HEAD_DIM:, :] = jnp.ones((2, V_ROWS - HEAD_DIM, s_len), vt_scr.dtype)
        for i in range(nq):
            vtt = v_ref[i * t:(i + 1) * t, :].T
            for hh in range(2):
                vt_scr[hh, :HEAD_DIM, i * t:(i + 1) * t] = vtt[hh * HEAD_DIM:(hh + 1) * HEAD_DIM, :]
                if bias is not None:
                    ckt = ck_ref[i * t:(i + 1) * t, :]
                    lane = lax.broadcasted_iota(jnp.int32, ckt.shape, 1)
                    ck_scr[hh, i * t:(i + 1) * t, :] = jnp.sum(jnp.where(lane == 2 * pj + hh, ckt, 0.0), axis=1,
                                                               keepdims=True)

        def qbody(qi, _):
            qs = pl.multiple_of(qi * t, t)
            qt = q_ref[pl.ds(qs, t), :]
            if rope is not None:
                qt = jnp.concatenate([qt, qr_ref[pl.ds(qs, t), :]], axis=1)
            qh = [_keep_head(qt, hh, 1, (pj % 2) * 2 + hh) for hh in range(2)]
            m_scr[...] = jnp.full(m_scr.shape, -jnp.inf, F32)
            acc_scr[...] = jnp.zeros(acc_scr.shape, F32)

            def step(ki, masked):
                ks = pl.multiple_of(ki * t, t)
                kt = k_ref[pl.ds(ks, t), :]
                if rope is not None:
                    kt = jnp.concatenate([kt, kr_ref[pl.ds(ks, t), :]], axis=1)
                sts = [lax.dot_general(kt, qh[hh], _NT, preferred_element_type=F32) for hh in range(2)]
                if bias is not None:
                    sts = [sts[hh] - ck_scr[hh, pl.ds(ks, t), :] for hh in range(2)]
                if masked:
                    sts = [jnp.where(_tile_mask(t, chunk_mask, True), st, -jnp.inf) for st in sts]
                m_old = [m_scr[hh] for hh in range(2)]
                m_new = [jnp.maximum(m_old[hh], jnp.max(sts[hh], axis=0, keepdims=True)) for hh in range(2)]
                pts = [jnp.exp2(sts[hh] - m_new[hh]).astype(MXU) for hh in range(2)]
                for hh in range(2):
                    alpha = jnp.exp2(m_old[hh] - m_new[hh])
                    acc_scr[hh] = alpha * acc_scr[hh] + jnp.dot(vt_scr[hh, :, pl.ds(ks, t)], pts[hh],
                                                                preferred_element_type=F32)
                    m_scr[hh] = m_new[hh]

            def loop_body(ki, carry):
                step(ki, False)
                return carry

            lax.fori_loop(0, qi, loop_body, 0)
            step(qi, True)
            outs = []
            for hh in range(2):
                acc = acc_scr[hh]
                l = acc[HEAD_DIM:HEAD_DIM + 1, :]
                outs.append(acc[:HEAD_DIM, :] / l)
                lse_ref[hh, :, pl.ds(qs, t)] = m_scr[hh] + jnp.log2(l)
            o_ref[pl.ds(qs, t), :] = jnp.concatenate(outs, axis=0).T
            return 0

        lax.fori_loop(0, nq, qbody, 0)
        if n_side:
            @pl.when(pj == PAIRS - 1)
            def _():
                _exchange_wait(_exchange_copies(side_in, side_out, side[1], *sems))

    def tok(blk):
        return pl.BlockSpec((s_len, LANES), lambda j: (0, blk + j))

    rowb = pl.BlockSpec((2, 1, s_len), lambda j: (j, 0, 0))
    hbm = pl.BlockSpec(memory_space=pl.ANY)
    ins = [q, k, v]
    in_specs = [tok(q_blk), tok(k_blk), tok(v_blk)]
    if bias is not None:
        ins.append(bias)
        in_specs.append(_full((s_len, LANES)))
    if rope is not None:
        ins += list(rope)
        in_specs += [pl.BlockSpec((s_len, LANES), lambda j: (0, j // 2)), _full((s_len, LANES))]
    out_shape = [_sds((s_len, PAIRS * LANES)), _sds((HEADS, 1, s_len))]
    scratch = [pltpu.VMEM((2, V_ROWS, s_len), v.dtype), pltpu.VMEM((2, 1, t), F32), pltpu.VMEM((2, V_ROWS, t), F32),
               pltpu.VMEM((2, s_len if bias is not None else 8, 1), F32)]
    if n_side:
        ins += list(side[0])
        out_shape += _exchange_out_shapes(*side)
        scratch += _exchange_sems(n_side)
    return pl.pallas_call(
        kern, name=name, grid=(PAIRS,), out_shape=out_shape,
        in_specs=in_specs + [hbm] * n_side, out_specs=[tok(0), rowb] + [hbm] * n_side,
        scratch_shapes=scratch,
        compiler_params=_params("arbitrary", side_effects=bool(n_side)),
    )(*ins)


def _attention_bwd(q, q_blk, k, k_blk, v, v_blk, do, pack, ck_row, rope, chunk_mask, q_scale, k_scale, name,
                   side=None):
    s_len = q.shape[0]
    t = min(ATTN_TILE, s_len // 2)
    nq = s_len // t
    has_bias = ck_row is not None
    kw = 2 * LANES if rope is not None else LANES
    n_side = len(side[0]) if side else 0

    def kern(*refs):
        q_ref, k_ref, v_ref, do_ref, pack_ref = refs[:5]
        pos = 5
        if has_bias:
            ck_ref = refs[pos]
            pos += 1
        if rope is not None:
            qr_ref, kr_ref = refs[pos:pos + 2]
            pos += 2
        side_in = refs[pos:pos + n_side]
        pos += n_side
        dq_ref, dk_ref, dv_ref = refs[pos:pos + 3]
        pos += 3
        if has_bias:
            dcq_ref, dck_ref = refs[pos:pos + 2]
            pos += 2
        if rope is not None:
            dqr_ref, dkr_ref = refs[pos:pos + 2]
            pos += 2
        side_out = refs[pos:pos + n_side]
        pos += n_side
        qt_scr, dot_scr, dkt_scr, dvt_scr, dq_scr, dcq_scr = refs[pos:pos + 6]
        sems = refs[pos + 6:]
        pj = pl.program_id(0)
        if n_side:
            @pl.when(pj == 0)
            def _():
                _exchange_start(_exchange_copies(side_in, side_out, side[1], *sems, recv=False))

        for i in range(nq):
            sl = slice(i * t, (i + 1) * t)
            qt_scr[:LANES, sl] = q_ref[sl, :].T
            dot_scr[:, sl] = do_ref[sl, :].T
            if rope is not None:
                qt_scr[LANES:, sl] = qr_ref[sl, :].T
        dkt_scr[...] = jnp.zeros(dkt_scr.shape, F32)
        dvt_scr[...] = jnp.zeros(dvt_scr.shape, F32)
        if has_bias:
            dck_ref[...] = jnp.zeros(dck_ref.shape, F32)

            @pl.when(pj == 0)
            def _():
                dcq_ref[...] = jnp.zeros(dcq_ref.shape, F32)

        def keep(x, hh, axis):
            return _keep_head(x, hh, axis, (pj % 2) * 2 + hh)

        def qbody(qi, _):
            qs = pl.multiple_of(qi * t, t)
            qt = q_ref[pl.ds(qs, t), :]
            if rope is not None:
                qt = jnp.concatenate([qt, qr_ref[pl.ds(qs, t), :]], axis=1)
            dot = do_ref[pl.ds(qs, t), :]
            pk = pack_ref[pl.ds(qs, t), :]
            lane = lax.broadcasted_iota(jnp.int32, pk.shape, 1)
            qh = [keep(qt, hh, 1) for hh in range(2)]
            doh = [keep(dot, hh, 1) for hh in range(2)]
            a_col = [jnp.sum(jnp.where(lane == 2 * pj + hh, pk, 0.0), axis=1, keepdims=True) for hh in range(2)]
            d_col = [jnp.sum(jnp.where(lane == HEADS + 2 * pj + hh, pk, 0.0), axis=1, keepdims=True)
                     for hh in range(2)]
            dq_scr[...] = jnp.zeros(dq_scr.shape, F32)
            if has_bias:
                dcq_scr[...] = jnp.zeros(dcq_scr.shape, F32)

            def step(ki, masked):
                ks = pl.multiple_of(ki * t, t)
                kt = k_ref[pl.ds(ks, t), :]
                if rope is not None:
                    kt = jnp.concatenate([kt, kr_ref[pl.ds(ks, t), :]], axis=1)
                vt = v_ref[pl.ds(ks, t), :]
                ss = [lax.dot_general(qh[hh], kt, _NT, preferred_element_type=F32) + a_col[hh] for hh in range(2)]
                if has_bias:
                    ss = [ss[hh] - ck_ref[hh, :, pl.ds(ks, t)] for hh in range(2)]
                dpds = [lax.dot_general(doh[hh], vt, _NT, preferred_element_type=F32) for hh in range(2)]
                ps = [jnp.exp2(s) for s in ss]
                if masked:
                    ps = [jnp.where(_tile_mask(t, chunk_mask, False), p, 0.0) for p in ps]
                dss = [ps[hh] * (dpds[hh] - d_col[hh]) for hh in range(2)]
                dq_acc = dq_scr[...]
                for hh in range(2):
                    rows = slice(hh * HEAD_DIM, (hh + 1) * HEAD_DIM)
                    dsb = dss[hh].astype(MXU)
                    dvt_scr[rows, pl.ds(ks, t)] += jnp.dot(dot_scr[rows, pl.ds(qs, t)], ps[hh].astype(MXU),
                                                           preferred_element_type=F32)
                    dkt_scr[rows, pl.ds(ks, t)] += jnp.dot(qt_scr[rows, pl.ds(qs, t)], dsb,
                                                           preferred_element_type=F32)
                    if rope is not None:
                        rr = pl.ds(pl.multiple_of(LANES + ((pj % 2) * 2 + hh) * ROPE, ROPE), ROPE)
                        dkt_scr[rr, pl.ds(ks, t)] += jnp.dot(qt_scr[rr, pl.ds(qs, t)], dsb,
                                                             preferred_element_type=F32)
                    dq_acc = dq_acc + jnp.dot(dsb, keep(kt, hh, 1), preferred_element_type=F32)
                    if has_bias:
                        dcq_scr[hh] += jnp.sum(dss[hh], axis=1, keepdims=True)
                        dck_ref[hh, :, pl.ds(ks, t)] += -jnp.sum(dss[hh], axis=0, keepdims=True)
                dq_scr[...] = dq_acc

            def loop_body(ki, carry):
                step(ki, False)
                return carry

            lax.fori_loop(0, qi, loop_body, 0)
            step(qi, True)
            dq_ref[pl.ds(qs, t), :] = (dq_scr[:, :LANES] * q_scale).astype(dq_ref.dtype)
            if rope is not None:
                dqr_ref[0, pl.ds(qs, t), :] = dq_scr[:, LANES:] * q_scale
            if has_bias:
                old = dcq_ref[pl.ds(qs, t), :]
                dcq_ref[pl.ds(qs, t), :] = jnp.where(lane == 2 * pj, dcq_scr[0],
                                                     jnp.where(lane == 2 * pj + 1, dcq_scr[1], old))
            return 0

        lax.fori_loop(0, nq, qbody, 0)
        for i in range(nq):
            sl = slice(i * t, (i + 1) * t)
            dk_ref[sl, :] = (dkt_scr[:LANES, sl].T * k_scale).astype(dk_ref.dtype)
            dv_ref[sl, :] = dvt_scr[:, sl].T.astype(dv_ref.dtype)
            if rope is not None:
                dkr_ref[0, sl, :] = dkt_scr[LANES:, sl].T * k_scale
        if n_side:
            @pl.when(pj == PAIRS - 1)
            def _():
                _exchange_wait(_exchange_copies(side_in, side_out, side[1], *sems))

    def tok(blk):
        return pl.BlockSpec((s_len, LANES), lambda j: (0, blk + j))

    shared = _full((s_len, LANES))
    rowb = pl.BlockSpec((2, 1, s_len), lambda j: (j, 0, 0))
    slab = pl.BlockSpec((1, s_len, LANES), lambda j: (j, 0, 0))
    hbm = pl.BlockSpec(memory_space=pl.ANY)
    ins = [q, k, v, do, pack]
    in_specs = [tok(q_blk), tok(k_blk), tok(v_blk), tok(0), shared]
    out_shape = [_sds((s_len, PAIRS * LANES), MXU)] * 3
    out_specs = [tok(0)] * 3
    if has_bias:
        ins.append(ck_row)
        in_specs.append(rowb)
        out_shape += [_sds((s_len, LANES)), _sds((HEADS, 1, s_len))]
        out_specs += [shared, rowb]
    if rope is not None:
        ins += list(rope)
        in_specs += [pl.BlockSpec((s_len, LANES), lambda j: (0, j // 2)), shared]
        out_shape += [_sds((PAIRS, s_len, LANES))] * 2
        out_specs += [slab, slab]
    scratch = [pltpu.VMEM((kw, s_len), q.dtype), pltpu.VMEM((LANES, s_len), do.dtype),
               pltpu.VMEM((kw, s_len), F32), pltpu.VMEM((LANES, s_len), F32),
               pltpu.VMEM((t, kw), F32), pltpu.VMEM((2, t, 1), F32)]
    if n_side:
        ins += list(side[0])
        out_shape += _exchange_out_shapes(*side)
        scratch += _exchange_sems(n_side)
    return pl.pallas_call(
        kern, name=name, grid=(PAIRS,), out_shape=out_shape,
        in_specs=in_specs + [hbm] * n_side, out_specs=out_specs + [hbm] * n_side, scratch_shapes=scratch,
        compiler_params=_params("arbitrary", side_effects=bool(n_side)),
    )(*ins)


def _silu(a):
    return a * jax.nn.sigmoid(a)


def _k_out(of, om, gates, x, gate, wout, tm=256):
    s_len, d = x.shape

    def kern(of_ref, om_ref, gates_ref, x_ref, gate_ref, w_ref, xo_ref, y_ref, u_ref):
        u_ref[:, :GROUP_W] = (of_ref[...] * _silu(gates_ref[:, :GROUP_W])).astype(MXU)
        u_ref[:, GROUP_W:] = (om_ref[...] * _silu(gates_ref[:, GROUP_W:])).astype(MXU)
        y = jnp.dot(u_ref[...], w_ref[...], preferred_element_type=F32)
        y_ref[...] = y
        xo_ref[...] = x_ref[...] + gate_ref[...] * y

    return pl.pallas_call(
        kern, name="k_out", grid=(s_len // tm,),
        out_shape=[_sds((s_len, d)), _sds((s_len, d)), _sds((s_len, 2 * GROUP_W), MXU)],
        in_specs=[_rows(tm, GROUP_W), _rows(tm, GROUP_W), _rows(tm, 2 * GROUP_W), _rows(tm, d), _full((1, d)),
                  _full((2 * GROUP_W, d))],
        out_specs=[_rows(tm, d), _rows(tm, d), _rows(tm, 2 * GROUP_W)],
        compiler_params=_params("arbitrary"),
    )(of, om, gates, x, gate, wout)


def _k_loss(x, gf, tgt, tm=256):
    s_len, d = x.shape

    def kern(x_ref, g_ref, t_ref, loss_ref, dx_ref, dg_ref):
        i = pl.program_id(0)
        xv = x_ref[...]
        r = lax.rsqrt(jnp.mean(xv * xv, axis=-1, keepdims=True) + EPS)
        xh = xv * r
        diff = xh * g_ref[...] - t_ref[...]
        part = 0.5 * jnp.sum(jnp.mean(diff * diff, axis=-1, keepdims=True))
        dout = diff * (1.0 / d)
        dxh = dout * g_ref[...]
        dx_ref[...] = r * (dxh - xh * jnp.mean(dxh * xh, axis=-1, keepdims=True))

        @pl.when(i == 0)
        def _():
            loss_ref[...] = jnp.zeros_like(loss_ref)
            dg_ref[...] = jnp.zeros_like(dg_ref)

        loss_ref[...] += jnp.full(loss_ref.shape, part, F32)
        dg_ref[...] += jnp.sum(dout * xh, axis=0, keepdims=True)

    return pl.pallas_call(
        kern, name="k_loss", grid=(s_len // tm,),
        out_shape=[_sds((1, LANES)), _sds((s_len, d)), _sds((1, d))],
        in_specs=[_rows(tm, d), _full((1, d)), _rows(tm, d)],
        out_specs=[_full((1, LANES)), _rows(tm, d), _full((1, d))],
        compiler_params=_params("arbitrary"),
    )(x, gf, tgt)


def _kb_out(dxo, y, gate, wout, of, om, gates, tm=256):
    s_len, d = dxo.shape

    def kern(dxo_ref, y_ref, gate_ref, wt_ref, of_ref, om_ref, gates_ref,
             dy_ref, dof_ref, dom_ref, dfg_ref, dmg_ref, dlf_ref, dlm_ref, dgate_ref):
        i = pl.program_id(0)
        dxv = dxo_ref[...]

        @pl.when(i == 0)
        def _():
            dgate_ref[...] = jnp.zeros_like(dgate_ref)

        dgate_ref[...] += jnp.sum(dxv * y_ref[...], axis=0, keepdims=True)
        dyb = (dxv * gate_ref[...]).astype(MXU)
        dy_ref[...] = dyb
        du = lax.dot_general(dyb, wt_ref[...], _NT, preferred_element_type=F32)
        head_of = (lax.broadcasted_iota(jnp.int32, (GROUP_W, LANES), 0) // HEAD_DIM
                   == lax.broadcasted_iota(jnp.int32, (GROUP_W, LANES), 1)).astype(F32)
        for du_g, o_ref, a, do_ref, dg_ref, dl_ref in (
                (du[:, :GROUP_W], of_ref, gates_ref[:, :GROUP_W], dof_ref, dfg_ref, dlf_ref),
                (du[:, GROUP_W:], om_ref, gates_ref[:, GROUP_W:], dom_ref, dmg_ref, dlm_ref)):
            sg = jax.nn.sigmoid(a)
            ov = o_ref[...]
            dov = du_g * (a * sg)
            do_ref[...] = dov.astype(MXU)
            dg_ref[...] = (du_g * ov * (sg * (1.0 + a * (1.0 - sg)))).astype(MXU)
            dl_ref[...] = jnp.dot(dov * ov, head_of, precision=lax.Precision.HIGHEST, preferred_element_type=F32)

    return pl.pallas_call(
        kern, name="kb_out", grid=(s_len // tm,),
        out_shape=[_sds((s_len, d), MXU), _sds((s_len, GROUP_W), MXU), _sds((s_len, GROUP_W), MXU),
                   _sds((s_len, GROUP_W), MXU), _sds((s_len, GROUP_W), MXU), _sds((s_len, LANES)),
                   _sds((s_len, LANES)), _sds((1, d))],
        in_specs=[_rows(tm, d), _rows(tm, d), _full((1, d)), _full((2 * GROUP_W, d)), _rows(tm, GROUP_W),
                  _rows(tm, GROUP_W), _rows(tm, 2 * GROUP_W)],
        out_specs=[_rows(tm, d), _rows(tm, GROUP_W), _rows(tm, GROUP_W), _rows(tm, GROUP_W),
                   _rows(tm, GROUP_W), _rows(tm, LANES), _rows(tm, LANES), _full((1, d))],
        compiler_params=_params("arbitrary"),
    )(dxo, y, gate, wout, of, om, gates)


def _kb_prep(dqn, dqr, dkn, dv, dkr4, dff, tail, cos, sin, gq, gkv, wuq_t, wukv_t, tm=512):
    s_len = tail.shape[0]
    qw = GROUP_W + ROPE_W

    def kern(dqn_ref, dqr_ref, dkn_ref, dv_ref, dkr_ref, dff_ref, tail_ref, cos_ref, sin_ref,
             gq_ref, gkv_ref, wuqt_ref, wukvt_ref, dq_ref, dz_ref, dgq_ref, dgkv_ref):
        i = pl.program_id(0)

        @pl.when(i == 0)
        def _():
            dgq_ref[...] = jnp.zeros_like(dgq_ref)
            dgkv_ref[...] = jnp.zeros_like(dgkv_ref)

        cs, sn = cos_ref[...], sin_ref[...]
        dq_ref[:, :GROUP_W] = dqn_ref[...]
        for half in range(ROPE_W // LANES):
            sl = slice(half * LANES, (half + 1) * LANES)
            dq_ref[:, GROUP_W + half * LANES:GROUP_W + (half + 1) * LANES] = _rope_bwd(dqr_ref[:, sl], cs, sn).astype(MXU)
        dqn = lax.dot_general(dq_ref[...], wuqt_ref[...], _NT, preferred_element_type=F32)
        ql = tail_ref[:, :T_KV]
        rq = lax.rsqrt(jnp.mean(ql * ql, axis=-1, keepdims=True) + EPS)
        qh = ql * rq
        dgq_ref[...] += jnp.sum(dqn * qh, axis=0, keepdims=True)
        dqh = dqn * gq_ref[...]
        dz_ref[:, :Q_LORA] = (rq * (dqh - qh * jnp.mean(dqh * qh, axis=-1, keepdims=True))).astype(MXU)

        dkvn = (lax.dot_general(dkn_ref[...], wukvt_ref[:, :GROUP_W], _NT, preferred_element_type=F32)
                + lax.dot_general(dv_ref[...], wukvt_ref[:, GROUP_W:], _NT, preferred_element_type=F32))
        kvl = tail_ref[:, T_KV:T_MISC]
        rk = lax.rsqrt(jnp.mean(kvl * kvl, axis=-1, keepdims=True) + EPS)
        kh = kvl * rk
        dgkv_ref[...] += jnp.sum(dkvn * kh, axis=0, keepdims=True)
        dkh = dkvn * gkv_ref[...]
        dz_ref[:, Q_LORA:Q_LORA + KV_LORA] = (
            rk * (dkh - kh * jnp.mean(dkh * kh, axis=-1, keepdims=True))).astype(MXU)

        g4 = dkr_ref[...]
        g = g4
        for rep in range(1, LANES // ROPE):
            g = g + pltpu.roll(g4, rep * ROPE, 1)
        lane = lax.broadcasted_iota(jnp.int32, g.shape, 1)
        dmisc = jnp.where(lane < ROPE, _rope_bwd(g, cs, sn), 0.0) + dff_ref[...]
        dz_ref[:, Q_LORA + KV_LORA:] = dmisc.astype(MXU)

    return pl.pallas_call(
        kern, name="kb_prep", grid=(s_len // tm,),
        out_shape=[_sds((s_len, qw), MXU), _sds((s_len, TAIL_W), MXU), _sds((1, Q_LORA)), _sds((1, KV_LORA))],
        in_specs=[_rows(tm, GROUP_W), _rows(tm, ROPE_W), _rows(tm, GROUP_W), _rows(tm, GROUP_W), _rows(tm, LANES),
                  _rows(tm, LANES), _rows(tm, TAIL_W),
                  _rows(tm, LANES), _rows(tm, LANES), _full((1, Q_LORA)), _full((1, KV_LORA)),
                  _full((Q_LORA, qw)), _full((KV_LORA, 2 * GROUP_W))],
        out_specs=[_rows(tm, qw), _rows(tm, TAIL_W), _full((1, Q_LORA)), _full((1, KV_LORA))],
        compiler_params=_params("arbitrary"),
    )(dqn, dqr, dkn, dv, dkr4, dff, tail, cos, sin, gq, gkv, wuq_t, wukv_t)


def _kb_in(dz_pieces, w, x, g, mod3, dxo, tm=256):
    s_len, d = x.shape
    widths = [p.shape[1] for p in dz_pieces]
    n_p = len(widths)

    def kern(*refs):
        dz_refs = refs[:n_p]
        w_ref, x_ref, g_ref, mod_ref, dxo_ref, dx_ref, acc_ref = refs[n_p:]
        i = pl.program_id(0)

        @pl.when(i == 0)
        def _():
            acc_ref[...] = jnp.zeros_like(acc_ref)

        dh = jnp.zeros((tm, d), F32)
        lo = 0
        for p_ref, wd in zip(dz_refs, widths):
            dh = dh + lax.dot_general(p_ref[...], w_ref[:, lo:lo + wd], _NT, preferred_element_type=F32)
            lo += wd
        xv = x_ref[...]
        r = lax.rsqrt(jnp.mean(xv * xv, axis=-1, keepdims=True) + EPS)
        xh = xv * r
        xn = xh * g_ref[...]
        dxn = dh * (1.0 + mod_ref[1:2, :])
        acc_ref[0:1, :] += jnp.sum(dh, axis=0, keepdims=True)
        acc_ref[1:2, :] += jnp.sum(dh * xn, axis=0, keepdims=True)
        acc_ref[2:3, :] += jnp.sum(dxn * xh, axis=0, keepdims=True)
        dxh = dxn * g_ref[...]
        dx_ref[...] = dxo_ref[...] + r * (dxh - xh * jnp.mean(dxh * xh, axis=-1, keepdims=True))

    return pl.pallas_call(
        kern, name="kb_in", grid=(s_len // tm,),
        out_shape=[_sds((s_len, d)), _sds((3, d))],
        in_specs=[_rows(tm, wd) for wd in widths] + [_full((d, Z_W)), _rows(tm, d), _full((1, d)), _full((3, d)),
                                                     _rows(tm, d)],
        out_specs=[_rows(tm, d), _full((3, d))],
        compiler_params=_params("arbitrary"),
    )(*dz_pieces, w, x, g, mod3, dxo)


def _weight_grad(a, pieces, name, tk=512):
    s_len, m = a.shape
    widths = [p.shape[1] for p in pieces]
    n = sum(widths)
    tk = min(tk, s_len)

    def kern(a_ref, *refs):
        o_ref = refs[-1]

        @pl.when(pl.program_id(0) == 0)
        def _():
            o_ref[...] = jnp.zeros_like(o_ref)

        at = a_ref[...]
        lo = 0
        for p_ref, w in zip(refs[:-1], widths):
            o_ref[:, lo:lo + w] += lax.dot_general(at, p_ref[...], _TN, preferred_element_type=F32)
            lo += w

    return pl.pallas_call(
        kern, name=name, grid=(s_len // tk,), out_shape=_sds((m, n)),
        in_specs=[_rows(tk, m)] + [_rows(tk, w) for w in widths],
        out_specs=_full((m, n)),
        compiler_params=_params("arbitrary"),
    )(a, *pieces)


def _adamw(slabs, w, m, v, name):
    n_l = len(slabs)
    n, r, c = slabs[0].shape
    tm = r
    for cand in (256, 128, 64, 32, 16, 8):
        if r % cand == 0:
            tm = cand
            break
    steps = r // tm

    def kern(*refs):
        g_refs = refs[:n_l]
        w_ref, m_ref, v_ref, go_ref, d_ref, mo_ref, vo_ref, g_scr = refs[n_l:]
        for ll in range(n_l):
            @pl.when(pl.program_id(0) == ll)
            def _(g_ref=g_refs[ll]):
                g = g_ref[0].astype(F32)
                for s in range(1, n):
                    g = g + g_ref[s].astype(F32)
                g_scr[...] = g

        g = g_scr[...]
        m_new = ADAM_B1 * m_ref[...] + (1.0 - ADAM_B1) * g
        v_new = ADAM_B2 * v_ref[...] + (1.0 - ADAM_B2) * (g * g)
        m_hat = m_new / (1.0 - ADAM_B1 ** ADAM_STEP)
        v_hat = v_new / (1.0 - ADAM_B2 ** ADAM_STEP)
        go_ref[...] = g
        mo_ref[...] = m_new
        vo_ref[...] = v_new
        d_ref[...] = -ADAM_LR * (m_hat / (jnp.sqrt(v_hat) + ADAM_EPS) + ADAM_WD * w_ref[...])

    row = pl.BlockSpec((tm, c), lambda l, i: (l * steps + i, 0))

    def slab_spec(ll):
        return pl.BlockSpec((n, tm, c), lambda l, i: (0, jnp.where(l == ll, i, 0), 0))

    return pl.pallas_call(
        kern, name=name, grid=(n_l, steps), out_shape=[_sds((n_l * r, c))] * 4,
        in_specs=[slab_spec(ll) for ll in range(n_l)] + [row, row, row],
        out_specs=[row] * 4,
        scratch_shapes=[pltpu.VMEM((tm, c), F32)],
        compiler_params=_params("arbitrary", "arbitrary"),
    )(*slabs, w, m, v)


def _perm_w_in(w):
    pad = jnp.zeros(w.shape[:-1] + (Z_W - Z_MISC - ROPE - HEADS,), w.dtype)
    return jnp.concatenate([w[..., 0:1536], w[..., 1544:2056], w[..., 2472:2984], w[..., 2056:2312],
                            w[..., 2312:2440], w[..., 2440:2472], w[..., 1536:1544], pad], axis=-1)


def _unperm_w_in(g):
    ff0 = Z_MISC + MISC_FF
    return jnp.concatenate([g[..., 0:1536], g[..., ff0:ff0 + HEADS], g[..., Z_FG:Z_FG + GROUP_W],
                            g[..., Z_QL:Z_QL + Q_LORA], g[..., Z_KV:Z_KV + KV_LORA],
                            g[..., Z_MISC:Z_MISC + ROPE], g[..., Z_MG:Z_MG + GROUP_W]], axis=-1)


def _perm_w_uq(w):
    lead = w.shape[:-1]
    wh = w.reshape(lead + (HEADS, NOPE + ROPE))
    return jnp.concatenate([wh[..., :NOPE].reshape(lead + (GROUP_W,)),
                            wh[..., NOPE:].reshape(lead + (ROPE_W,))], axis=-1)


def _unperm_w_uq(g):
    lead = g.shape[:-1]
    parts = [g[..., :GROUP_W].reshape(lead + (HEADS, NOPE)), g[..., GROUP_W:].reshape(lead + (HEADS, ROPE))]
    return jnp.concatenate(parts, axis=-1).reshape(lead + (HEADS * (NOPE + ROPE),))


def _perm_w_ukv(w):
    lead = w.shape[:-1]
    wh = w.reshape(lead + (HEADS, 2 * HEAD_DIM))
    return jnp.concatenate([wh[..., :NOPE].reshape(lead + (GROUP_W,)),
                            wh[..., NOPE:].reshape(lead + (GROUP_W,))], axis=-1)


def _unperm_w_ukv(g):
    lead = g.shape[:-1]
    parts = [g[..., :GROUP_W].reshape(lead + (HEADS, NOPE)), g[..., GROUP_W:].reshape(lead + (HEADS, HEAD_DIM))]
    return jnp.concatenate(parts, axis=-1).reshape(lead + (2 * GROUP_W,))


def _rope_tables(positions):
    inv_freq = 1.0 / (ROPE_THETA ** (jnp.arange(0, ROPE, 2, dtype=F32) / ROPE))
    ang = positions.astype(F32)[:, None] * inv_freq
    cos, sin = jnp.cos(ang), jnp.sin(ang)
    reps = LANES // ROPE
    return jnp.tile(jnp.concatenate([cos, cos], axis=1), (1, reps)), jnp.tile(jnp.concatenate([-sin, sin], axis=1), (1, reps))


def _full_weights(g_in, g_uq, g_ukv, g_out):
    def cols(g):
        return g.transpose(1, 0, 2).reshape(g.shape[1], -1)
    return (_perm_w_in(cols(g_in)), _perm_w_uq(cols(g_uq)), _perm_w_ukv(cols(g_ukv)),
            g_out.reshape(-1, g_out.shape[2]))


def _grad_slabs(dw_in, dw_uq, dw_ukv, dw_out):
    def cols(g):
        return g.reshape(g.shape[0], N_DEV, -1).transpose(1, 0, 2)
    return [cols(_unperm_w_in(dw_in)), cols(_unperm_w_uq(dw_uq)), cols(_unperm_w_ukv(dw_ukv)),
            dw_out.reshape(N_DEV, -1, dw_out.shape[1])]


def _local_step(x, mod, positions, loss_target, norm_g, b_f, q_norm_g, kv_norm_g, final_g, weights, shards=None):
    n_l = norm_g.shape[0]
    s_len, d = x.shape
    cos, sin = _rope_tables(positions)
    qb, kb, vb = Z_FQ // LANES, Z_FK // LANES, Z_FV // LANES
    weights = list(weights)

    def pack_rows(a_rows, delta):
        return jnp.concatenate([a_rows.T, delta[:, :HEADS], jnp.zeros((s_len, LANES - 2 * HEADS), F32)], axis=1)

    saved = []
    for l in range(n_l):
        w_in, w_uq, w_ukv, w_out = weights[l]
        mod3 = mod[l].reshape(3, d)
        h, qkv, gates, tail = _k_in(x, norm_g[l][None], mod3, w_in)
        fft = tail[:, T_MISC + MISC_FF:T_MISC + MISC_FF + HEADS].T
        bf = b_f[l][:, None]
        c2 = _k_cum(fft, bf) * LOG2E
        side = (list(shards[l + 1]), [True] * 4) if shards is not None and l + 1 < n_l else None
        ck_lanes = jnp.pad(c2.T, ((0, 0), (0, LANES - HEADS)))
        of, lse_f, *gathered = _attention_fwd(qkv, qb, qkv, kb, qkv, vb, ck_lanes, None, False,
                                              "fox_fwd_gather" if side else "fox_fwd", side)
        if side:
            weights.append(_full_weights(*gathered))
        mq, mqr, mk, mv, kr4, qn, kvn = _k_prep(tail, cos, sin, q_norm_g[l][None], kv_norm_g[l][None], w_uq, w_ukv)
        om, lse_m = _attention_fwd(mq, 0, mk, 0, mv, 0, None, (mqr, kr4), True, "mla_fwd")
        x_new, y, u = _k_out(of, om, gates, x, mod3[2:3], w_out)
        saved.append((x, gates, tail, h, qkv, fft, bf, c2, lse_f, mq, mqr, mk, mv, kr4, lse_m, of, om, qn, kvn, y, u,
                      mod3))
        x = x_new

    loss_row, dx, dfinal = _k_loss(x, final_g[None], loss_target)

    grads = {k: [] for k in ("norm_g", "mod", "w_in", "b_f", "q_norm_g", "w_uq", "kv_norm_g", "w_ukv", "w_out")}
    received, pending = {}, None
    for l in range(n_l - 1, -1, -1):
        (x_l, gates, tail, h, qkv, fft, bf, c2, lse_f, mq, mqr, mk, mv, kr4, lse_m, of, om, qn, kvn, y, u,
         mod3) = saved[l]
        w_in, w_uq, w_ukv, w_out = weights[l]
        dyb, dof, dom, dfg, dmg, dlt_f, dlt_m, dgate = _kb_out(dx, y, mod3[2:3], w_out, of, om, gates)
        dw_out = _weight_grad(u, [dyb], "dw_out")

        side = (pending, [False] * 4) if pending is not None else None
        dfq, dfk, dfv, dcq, dck, *arrived = _attention_bwd(
            qkv, qb, qkv, kb, qkv, vb, dof, pack_rows(-lse_f.reshape(HEADS, s_len), dlt_f), c2[:, None, :], None,
            False, FOX_SCALE, 1.0 / LOG2E, "fox_bwd_exchange" if side else "fox_bwd", side)
        if side:
            received[l + 1] = arrived
        dfft, dbf = _k_cum_bwd(dcq[:, :HEADS].T + dck.reshape(HEADS, s_len), fft, bf)
        grads["b_f"].append(dbf[:, 0])

        dmq, dkn, dmv, dqr4, dkr4s = _attention_bwd(
            mq, 0, mk, 0, mv, 0, dom, pack_rows(-lse_m.reshape(HEADS, s_len), dlt_m), None, (mqr, kr4), True,
            MLA_SCALE, 1.0 / LOG2E, "mla_bwd")
        dqr = jnp.concatenate([dqr4[0] + dqr4[1], dqr4[2] + dqr4[3]], axis=1)
        dkr4 = dkr4s[0] + dkr4s[1] + dkr4s[2] + dkr4s[3]
        dff = jnp.pad(dfft.T, ((0, 0), (MISC_FF, LANES - MISC_FF - HEADS)))
        dq_b, dz_tail, dgq, dgkv = _kb_prep(dmq, dqr, dkn, dmv, dkr4, dff, tail, cos, sin, q_norm_g[l][None],
                                            kv_norm_g[l][None], w_uq, w_ukv)
        grads["q_norm_g"].append(dgq[0])
        grads["kv_norm_g"].append(dgkv[0])
        dw_uq = _weight_grad(qn, [dq_b], "dw_uq")
        dw_ukv = _weight_grad(kvn, [dkn, dmv], "dw_ukv")
        dz = [dfq, dfk, dfv, dfg, dmg, dz_tail]
        dw_in = _weight_grad(h, dz, "dw_in")
        dx, acc3 = _kb_in(dz, w_in, x_l, norm_g[l][None], mod3, dx)
        grads["norm_g"].append(acc3[2])
        grads["mod"].append(jnp.concatenate([acc3[0], acc3[1], dgate[0]]))
        if shards is not None:
            pending = _grad_slabs(dw_in, dw_uq, dw_ukv, dw_out)
        else:
            for name, g in (("w_in", dw_in), ("w_uq", dw_uq), ("w_ukv", dw_ukv), ("w_out", dw_out)):
                grads[name].append(g)
    grads = {k: jnp.stack(v[::-1]) for k, v in grads.items() if v}
    grads["final_g"] = dfinal[0]
    if shards is None:
        return loss_row[0, 0], dx, grads
    return loss_row[0, 0], dx, grads, received, pending


def _pack_small(parts, total):
    flat = jnp.concatenate([p.reshape(-1) for p in parts])
    return jnp.pad(flat, (0, total - flat.shape[0])).reshape(total // LANES, LANES)


def kernel(x, c, positions, norm_g, w_ada, b_ada, w_in, b_f, q_norm_g, w_uq, kv_norm_g, w_ukv, w_out, final_g, loss_target, m_norm_g, m_w_ada, m_b_ada, m_w_in, m_b_f, m_q_norm_g, m_w_uq, m_kv_norm_g, m_w_ukv, m_w_out, m_final_g, v_norm_g, v_w_ada, v_b_ada, v_w_in, v_b_f, v_q_norm_g, v_w_uq, v_kv_norm_g, v_w_ukv, v_w_out, v_final_g):
    n_l, d = norm_g.shape
    me = 4 * lax.axis_index("x") + 2 * lax.axis_index("y") + lax.axis_index("c")
    ada_c = w_ada.shape[2]

    cact = jnp.broadcast_to(jax.nn.silu(c), (N_DEV, d))
    shards = [[w[l].astype(MXU) for w in (w_in, w_uq, w_ukv, w_out)] for l in range(n_l)]
    *g_w0, g_cact = _exchange(shards[0] + [cact], [True] * 5, "gather_layer0")
    cact_all = g_cact[:, 0, :]

    b_cols = lax.dynamic_slice_in_dim(b_ada, me * ada_c, ada_c, axis=1)[:, None, :]
    modpart = _modpart(cact_all, w_ada, b_cols)
    mod_send = jnp.pad(modpart.transpose(1, 0, 2), ((0, 0), (0, 8 - n_l), (0, 0)))
    (mod_recv,) = _exchange([mod_send], [False], "scatter_mod")
    mod = mod_recv.transpose(1, 0, 2).reshape(8, N_DEV * ada_c)[:n_l]

    loss, dx, gr, received, pending = _local_step(x[0], mod, positions[0], loss_target[0], norm_g, b_f, q_norm_g,
                                                  kv_norm_g, final_g, [_full_weights(*g_w0)], shards)

    small_parts = [gr["norm_g"], gr["mod"], gr["b_f"], gr["q_norm_g"], gr["kv_norm_g"], gr["final_g"], cact[0]]
    sizes = [int(np.prod(p.shape)) for p in small_parts]
    total = -(-sum(sizes) // 1024) * 1024
    small = _pack_small(small_parts, total)
    *received[0], r_small = _exchange([p.astype(MXU) for p in pending] + [small],
                                      [False, False, False, False, True], "exchange_layer0")
    r_in, r_uq, r_ukv, r_out = ([received[l][i] for l in range(n_l)] for i in range(4))

    def upd(slabs, w, m, v, name):
        shp = w.shape
        w2, m2, v2 = (a.reshape(-1, slabs[0].shape[2]) for a in (w, m, v))
        return [o.reshape(shp) for o in _adamw(slabs, w2, m2, v2, name)]

    o_in = upd(r_in, w_in, m_w_in, v_w_in, "adamw_w_in")
    o_uq = upd(r_uq, w_uq, m_w_uq, v_w_uq, "adamw_w_uq")
    o_ukv = upd(r_ukv, w_ukv, m_w_ukv, v_w_ukv, "adamw_w_ukv")
    o_out = upd(r_out, w_out, m_w_out, v_w_out, "adamw_w_out")

    offs = np.cumsum([0] + sizes)
    flat_all = r_small.reshape(N_DEV, total)
    dmod_all = flat_all[:, offs[1]:offs[2]].reshape(N_DEV, n_l, 3 * d)
    dmod_cols = lax.dynamic_slice_in_dim(dmod_all, me * ada_c, ada_c, axis=2).transpose(1, 0, 2)
    cact_cols = flat_all[:, offs[6]:offs[7]][:, :, None]
    g_ada = _ada_grad(cact_cols, dmod_cols)
    o_ada = upd([g_ada.reshape(1, n_l * d, ada_c)], w_ada, m_w_ada, v_w_ada, "adamw_w_ada")

    zero_c = jnp.zeros((d,), F32)
    small_w = [_pack_small([norm_g, b_ada, b_f, q_norm_g, kv_norm_g, final_g, zero_c], total),
               _pack_small([m_norm_g, m_b_ada, m_b_f, m_q_norm_g, m_kv_norm_g, m_final_g, zero_c], total),
               _pack_small([v_norm_g, v_b_ada, v_b_f, v_q_norm_g, v_kv_norm_g, v_final_g, zero_c], total)]
    o_small = [o.reshape(-1) for o in _adamw([r_small], *small_w, "adamw_small")]
    shapes = [norm_g.shape, b_ada.shape, b_f.shape, q_norm_g.shape, kv_norm_g.shape, final_g.shape]

    def small_out(kind, idx):
        return o_small[kind][offs[idx]:offs[idx + 1]].reshape(shapes[idx])

    loss_all = lax.psum(loss, ("x", "y", "c"))
    outs = [loss_all, dx[None]]
    for kind in range(4):
        outs += [small_out(kind, 0), o_ada[kind], small_out(kind, 1), o_in[kind], small_out(kind, 2),
                 small_out(kind, 3), o_uq[kind], small_out(kind, 4), o_ukv[kind], o_out[kind], small_out(kind, 5)]
    return tuple(outs)
```

```python
import jax
import jax.numpy as jnp
import numpy as np
from jax import lax
from jax.experimental import pallas as pl
from jax.experimental.pallas import tpu as pltpu

F32 = jnp.float32
MXU = jnp.bfloat16

N_DEV = 8
HEADS = 8
PAIRS = HEADS // 2
HEAD_DIM = 64
NOPE = 64
ROPE = 32
HALF_ROPE = ROPE // 2
Q_LORA = 256
KV_LORA = 128
CHUNK = 64
GROUP_W = HEADS * HEAD_DIM
ROPE_W = HEADS * ROPE
EPS = 1e-6
ROPE_THETA = 10000.0
N_IN = 2984

Z_FQ, Z_FK, Z_FV, Z_FG, Z_MG, Z_QL, Z_KV, Z_MISC, Z_W = 0, 512, 1024, 1536, 2048, 2560, 2816, 2944, 3072
MISC_FF = ROPE
TAIL_W = Z_W - Z_QL
T_KV, T_MISC = Q_LORA, Q_LORA + KV_LORA

ADAM_LR = 0.001
ADAM_B1 = 0.9
ADAM_B2 = 0.999
ADAM_EPS = 1e-08
ADAM_WD = 0.01
ADAM_STEP = 10

VMEM_LIMIT_V7X = 56 * 1024 * 1024
LANES = 128
ATTN_TILE = 512
V_ROWS = HEAD_DIM + 16
LOG2E = 1.4426950408889634
FOX_SCALE = HEAD_DIM ** -0.5
MLA_SCALE = (NOPE + ROPE) ** -0.5

_NT = (((1,), (1,)), ((), ()))
_TN = (((0,), (0,)), ((), ()))


def _params(*sem, side_effects=False):
    return pltpu.CompilerParams(dimension_semantics=sem, vmem_limit_bytes=VMEM_LIMIT_V7X,
                                has_side_effects=side_effects)


def _sds(shape, dtype=F32):
    return jax.ShapeDtypeStruct(shape, dtype)


def _full(shape):
    nd = len(shape)
    return pl.BlockSpec(shape, lambda *_: (0,) * nd)


def _rows(tm, width, col=0):
    return pl.BlockSpec((tm, width), lambda i: (i, col))


def _exchange(arrs, gather, name):
    n = len(arrs)

    def kern(*refs):
        copies = _exchange_copies(refs[:n], refs[n:2 * n], gather, *refs[2 * n:])
        _exchange_start(copies)
        _exchange_wait(copies)

    return pl.pallas_call(
        kern, name=name, out_shape=_exchange_out_shapes(arrs, gather),
        in_specs=[pl.BlockSpec(memory_space=pl.ANY)] * n,
        out_specs=[pl.BlockSpec(memory_space=pl.ANY)] * n,
        scratch_shapes=_exchange_sems(n),
        compiler_params=pltpu.CompilerParams(has_side_effects=True),
    )(*arrs)


def _exchange_out_shapes(arrs, gather):
    return [_sds((N_DEV,) + tuple(a.shape) if g else tuple(a.shape), a.dtype) for a, g in zip(arrs, gather)]


def _exchange_sems(n):
    return [pltpu.SemaphoreType.DMA((n, N_DEV)), pltpu.SemaphoreType.DMA((n, N_DEV)), pltpu.SemaphoreType.DMA((n,))]


def _exchange_copies(ins, outs, gather, send_sems, recv_sems, loc_sems, recv=True):
    n = len(ins)
    x, y, c = lax.axis_index("x"), lax.axis_index("y"), lax.axis_index("c")
    me = 4 * x + 2 * y + c

    def src(i, j):
        return ins[i] if gather[i] else ins[i].at[j]

    local = [pltpu.make_async_copy(src(i, me), outs[i].at[me], loc_sems.at[i]) for i in range(n)]
    sends, recvs = [], []
    for k in range(1, N_DEV):
        px = 1 - x if k & 4 else x
        py = 1 - y if k & 2 else y
        pc = 1 - c if k & 1 else c
        p = 4 * px + 2 * py + pc
        for i in range(n):
            sends.append(pltpu.make_async_remote_copy(
                src_ref=src(i, p), dst_ref=outs[i].at[me], send_sem=send_sems.at[i, k],
                recv_sem=recv_sems.at[i, k], device_id=(px, py, pc), device_id_type=pl.DeviceIdType.MESH))
            if recv:
                recvs.append(pltpu.make_async_remote_copy(
                    src_ref=src(i, p), dst_ref=outs[i].at[p], send_sem=send_sems.at[i, k],
                    recv_sem=recv_sems.at[i, k], device_id=(px, py, pc), device_id_type=pl.DeviceIdType.MESH))
    return local, sends, recvs


def _exchange_start(copies):
    local, sends, _ = copies
    for cp in local + sends:
        cp.start()


def _exchange_wait(copies):
    local, sends, recvs = copies
    for cp in recvs:
        cp.wait_recv()
    for cp in sends:
        cp.wait_send()
    for cp in local:
        cp.wait()


def _modpart(cact8, w_ada, b_cols):
    n_l, d, cw = w_ada.shape

    def kern(c_ref, w_ref, b_ref, o_ref):
        o_ref[0] = jnp.dot(c_ref[...].astype(MXU), w_ref[0].astype(MXU), preferred_element_type=F32) + b_ref[0]

    return pl.pallas_call(
        kern, name="modpart", grid=(n_l,), out_shape=_sds((n_l, N_DEV, cw)),
        in_specs=[_full((N_DEV, d)), pl.BlockSpec((1, d, cw), lambda l: (l, 0, 0)),
                  pl.BlockSpec((1, 1, cw), lambda l: (l, 0, 0))],
        out_specs=pl.BlockSpec((1, N_DEV, cw), lambda l: (l, 0, 0)),
        compiler_params=_params("arbitrary"),
    )(cact8, w_ada, b_cols)


def _ada_grad(cact_cols, dmod_cols):
    n_l, _, cw = dmod_cols.shape
    d = cact_cols.shape[1]

    def kern(c_ref, dm_ref, o_ref):
        acc = c_ref[0] * dm_ref[0, 0:1, :]
        for s in range(1, N_DEV):
            acc = acc + c_ref[s] * dm_ref[0, s:s + 1, :]
        o_ref[0] = acc

    return pl.pallas_call(
        kern, name="ada_grad", grid=(n_l,), out_shape=_sds((n_l, d, cw)),
        in_specs=[_full((N_DEV, d, 1)), pl.BlockSpec((1, N_DEV, cw), lambda l: (l, 0, 0))],
        out_specs=pl.BlockSpec((1, d, cw), lambda l: (l, 0, 0)),
        compiler_params=_params("arbitrary"),
    )(cact_cols, dmod_cols)


def _k_in(x, g, mod3, w, tm=256):
    s_len, d = x.shape
    qkv_w = 3 * GROUP_W

    def kern(x_ref, g_ref, mod_ref, w_ref, h_ref, qkv_ref, gates_ref, tail_ref):
        xv = x_ref[...]
        r = lax.rsqrt(jnp.mean(xv * xv, axis=-1, keepdims=True) + EPS)
        xn = xv * r * g_ref[...]
        h = (xn * (1.0 + mod_ref[1:2, :]) + mod_ref[0:1, :]).astype(MXU)
        h_ref[...] = h
        z = jnp.dot(h, w_ref[...], preferred_element_type=F32)
        qkv_ref[:, :GROUP_W] = (z[:, Z_FQ:Z_FQ + GROUP_W] * (FOX_SCALE * LOG2E)).astype(MXU)
        qkv_ref[:, GROUP_W:] = z[:, Z_FK:Z_FK + 2 * GROUP_W].astype(MXU)
        gates_ref[...] = z[:, Z_FG:Z_QL]
        tail_ref[...] = z[:, Z_QL:]

    return pl.pallas_call(
        kern, name="k_in", grid=(s_len // tm,),
        out_shape=[_sds((s_len, d), MXU), _sds((s_len, qkv_w), MXU), _sds((s_len, Z_QL - Z_FG)),
                   _sds((s_len, TAIL_W))],
        in_specs=[_rows(tm, d), _full((1, d)), _full((3, d)), _full((d, Z_W))],
        out_specs=[_rows(tm, d), _rows(tm, qkv_w), _rows(tm, Z_QL - Z_FG), _rows(tm, TAIL_W)],
        compiler_params=_params("arbitrary"),
    )(x, g, mod3, w)


def _scan_matrices(rows, chunks, reverse):
    r_i = lax.broadcasted_iota(jnp.int32, (LANES, LANES), 0)
    c_i = lax.broadcasted_iota(jnp.int32, (LANES, LANES), 1)
    a_i = lax.broadcasted_iota(jnp.int32, (rows, rows), 0)
    b_i = lax.broadcasted_iota(jnp.int32, (rows, rows), 1)
    same_head = (a_i // chunks) == (b_i // chunks)
    if reverse:
        return (r_i >= c_i).astype(F32), (same_head & (b_i > a_i)).astype(F32)
    return (r_i <= c_i).astype(F32), (same_head & (b_i < a_i)).astype(F32)


def _scan_rows(x, inner, outer):
    tot = jnp.broadcast_to(jnp.sum(x, axis=1, keepdims=True), x.shape)
    return (jnp.dot(x, inner, precision=lax.Precision.HIGHEST, preferred_element_type=F32)
            + jnp.dot(outer, tot, precision=lax.Precision.HIGHEST, preferred_element_type=F32))


def _k_cum(ff_rows, b_rows, chunks):
    rows = ff_rows.shape[0]

    def kern(ff_ref, b_ref, cum_ref):
        xc = ff_ref[...] + b_ref[...]
        lf = jnp.minimum(xc, 0.0) - jnp.log(1.0 + jnp.exp(-jnp.abs(xc)))
        cum_ref[...] = _scan_rows(lf, *_scan_matrices(rows, chunks, False))

    return pl.pallas_call(
        kern, name="k_cum", out_shape=_sds((rows, LANES)),
        in_specs=[pl.BlockSpec(memory_space=pltpu.VMEM)] * 2,
        out_specs=pl.BlockSpec(memory_space=pltpu.VMEM),
        compiler_params=_params(),
    )(ff_rows, b_rows)


def _k_cum_bwd(dc_rows, ff_rows, b_rows, chunks):
    rows = ff_rows.shape[0]

    def kern(dc_ref, ff_ref, b_ref, dff_ref, db_ref):
        dlf = _scan_rows(dc_ref[...], *_scan_matrices(rows, chunks, True))
        dff = dlf * jax.nn.sigmoid(-(ff_ref[...] + b_ref[...]))
        dff_ref[...] = dff
        db_ref[...] = jnp.broadcast_to(jnp.sum(dff, axis=1, keepdims=True), dff.shape)

    return pl.pallas_call(
        kern, name="k_cum_bwd", out_shape=[_sds((rows, LANES)), _sds((rows, LANES))],
        in_specs=[pl.BlockSpec(memory_space=pltpu.VMEM)] * 3,
        out_specs=[pl.BlockSpec(memory_space=pltpu.VMEM)] * 2,
        compiler_params=_params(),
    )(dc_rows, ff_rows, b_rows)


def _swap16(t):
    lane = lax.broadcasted_iota(jnp.int32, t.shape, 1)
    return jnp.where(lane % ROPE < HALF_ROPE, pltpu.roll(t, LANES - HALF_ROPE, 1), pltpu.roll(t, HALF_ROPE, 1))


def _rope(t, cos, sin):
    return t * cos + _swap16(t) * sin


def _rope_bwd(dt, cos, sin):
    return dt * cos - _swap16(dt) * sin


def _k_prep(tail, cos, sin, gq, gkv, wuq, wukv, tm=512):
    s_len = tail.shape[0]
    qc = MLA_SCALE * LOG2E

    def kern(tail_ref, cos_ref, sin_ref, gq_ref, gkv_ref, wuq_ref, wukv_ref,
             qn_out, qr_out, kn_out, v_out, kr_out, qn_ref, kvn_ref):
        cs, sn = cos_ref[...], sin_ref[...]
        ql = tail_ref[:, :T_KV]
        rq = lax.rsqrt(jnp.mean(ql * ql, axis=-1, keepdims=True) + EPS)
        qn = (ql * rq * gq_ref[...]).astype(MXU)
        qn_ref[...] = qn
        q = jnp.dot(qn, wuq_ref[...], preferred_element_type=F32)
        qn_out[...] = (q[:, :GROUP_W] * qc).astype(MXU)
        for half in range(ROPE_W // LANES):
            lo = GROUP_W + half * LANES
            qr_out[:, half * LANES:(half + 1) * LANES] = (_rope(q[:, lo:lo + LANES], cs, sn) * qc).astype(MXU)
        kvl = tail_ref[:, T_KV:T_MISC]
        rk = lax.rsqrt(jnp.mean(kvl * kvl, axis=-1, keepdims=True) + EPS)
        kvn = (kvl * rk * gkv_ref[...]).astype(MXU)
        kvn_ref[...] = kvn
        kv = jnp.dot(kvn, wukv_ref[...], preferred_element_type=F32)
        kn_out[...] = kv[:, :GROUP_W].astype(MXU)
        v_out[...] = kv[:, GROUP_W:].astype(MXU)
        misc = tail_ref[:, T_MISC:]
        lane = lax.broadcasted_iota(jnp.int32, misc.shape, 1)
        kr = jnp.where(lane < ROPE, _rope(misc, cs, sn), 0.0)
        kr4 = kr
        for rep in range(1, LANES // ROPE):
            kr4 = kr4 + pltpu.roll(kr, rep * ROPE, 1)
        kr_out[...] = kr4.astype(MXU)

    return pl.pallas_call(
        kern, name="k_prep", grid=(s_len // tm,),
        out_shape=[_sds((s_len, GROUP_W), MXU), _sds((s_len, ROPE_W), MXU), _sds((s_len, GROUP_W), MXU),
                   _sds((s_len, GROUP_W), MXU), _sds((s_len, LANES), MXU), _sds((s_len, Q_LORA), MXU),
                   _sds((s_len, KV_LORA), MXU)],
        in_specs=[_rows(tm, TAIL_W), _rows(tm, LANES), _rows(tm, LANES),
                  _full((1, Q_LORA)), _full((1, KV_LORA)), _full((Q_LORA, GROUP_W + ROPE_W)),
                  _full((KV_LORA, 2 * GROUP_W))],
        out_specs=[_rows(tm, GROUP_W), _rows(tm, ROPE_W), _rows(tm, GROUP_W), _rows(tm, GROUP_W), _rows(tm, LANES),
                   _rows(tm, Q_LORA), _rows(tm, KV_LORA)],
        compiler_params=_params("arbitrary"),
    )(tail, cos, sin, gq, gkv, wuq, wukv)


def _block_mask(kn, qn, q_off, chunk_mask, transposed):
    shape = (kn, qn) if transposed else (qn, kn)
    row = lax.broadcasted_iota(jnp.int32, shape, 0)
    col = lax.broadcasted_iota(jnp.int32, shape, 1)
    qi, ki = (col + q_off, row) if transposed else (row + q_off, col)
    if chunk_mask:
        return (ki // CHUNK) <= (qi // CHUNK)
    return ki <= qi


def _keep_head(x, hh, axis, rope_group):
    idx = lax.broadcasted_iota(jnp.int32, x.shape, axis)
    keep = (idx >= hh * HEAD_DIM) & (idx < (hh + 1) * HEAD_DIM)
    if x.shape[axis] != LANES:
        keep = keep | ((idx >= LANES + rope_group * ROPE) & (idx < LANES + (rope_group + 1) * ROPE))
    return jnp.where(keep, x, jnp.zeros_like(x))


def _attention_fwd(q, q_blk, k, k_blk, v, v_blk, bias, rope, chunk_mask, name, side=None):
    s_len = q.shape[0]
    t = min(ATTN_TILE, s_len // 2)
    nq = s_len // t
    n_side = len(side[0]) if side else 0

    def kern(*refs):
        q_ref, k_ref, v_ref = refs[:3]
        pos = 3
        if bias is not None:
            ck_ref = refs[pos]
            pos += 1
        if rope is not None:
            qr_ref, kr_ref = refs[pos:pos + 2]
            pos += 2
        side_in = refs[pos:pos + n_side]
        pos += n_side
        o_ref, lse_ref = refs[pos:pos + 2]
        side_out = refs[pos + 2:pos + 2 + n_side]
        vt_scr, m_scr, acc_scr, ck_scr = refs[pos + 2 + n_side:pos + 6 + n_side]
        sems = refs[pos + 6 + n_side:]
        pj = pl.program_id(0)
        if n_side:
            @pl.when(pj == 0)
            def _():
                _exchange_start(_exchange_copies(side_in, side_out, side[1], *sems, recv=False))
        vt_scr[:, HEAD_DIM:, :] = jnp.ones((2, V_ROWS - HEAD_DIM, s_len), vt_scr.dtype)
        for i in range(nq):
            vtt = v_ref[i * t:(i + 1) * t, :].T
            for hh in range(2):
                vt_scr[hh, :HEAD_DIM, i * t:(i + 1) * t] = vtt[hh * HEAD_DIM:(hh + 1) * HEAD_DIM, :]
                if bias is not None:
                    ckt = ck_ref[i * t:(i + 1) * t, :]
                    lane = lax.broadcasted_iota(jnp.int32, ckt.shape, 1)
                    ck_scr[hh, i * t:(i + 1) * t, :] = jnp.sum(jnp.where(lane == 2 * pj + hh, ckt, 0.0), axis=1,
                                                               keepdims=True)

        def qbody(qi, _):
            qs = pl.multiple_of(qi * t, t)
            qt = q_ref[pl.ds(qs, t), :]
            if rope is not None:
                qt = jnp.concatenate([qt, qr_ref[pl.ds(qs, t), :]], axis=1)
            qh = [_keep_head(qt, hh, 1, (pj % 2) * 2 + hh) for hh in range(2)]
            m_scr[...] = jnp.full(m_scr.shape, -jnp.inf, F32)
            acc_scr[...] = jnp.zeros(acc_scr.shape, F32)

            def block(ks, kn, q0, qn, masked):
                kt = k_ref[pl.ds(ks, kn), :]
                if rope is not None:
                    kt = jnp.concatenate([kt, kr_ref[pl.ds(ks, kn), :]], axis=1)
                qc = slice(q0, q0 + qn)
                sts = [lax.dot_general(kt, qh[hh][qc], _NT, preferred_element_type=F32) for hh in range(2)]
                if bias is not None:
                    sts = [sts[hh] - ck_scr[hh, pl.ds(ks, kn), :] for hh in range(2)]
                if masked:
                    sts = [jnp.where(_block_mask(kn, qn, q0, chunk_mask, True), st, -jnp.inf) for st in sts]
                m_old = [m_scr[hh, :, qc] for hh in range(2)]
                m_new = [jnp.maximum(m_old[hh], jnp.max(sts[hh], axis=0, keepdims=True)) for hh in range(2)]
                pts = [jnp.exp2(sts[hh] - m_new[hh]).astype(MXU) for hh in range(2)]
                for hh in range(2):
                    alpha = jnp.exp2(m_old[hh] - m_new[hh])
                    acc_scr[hh, :, qc] = alpha * acc_scr[hh, :, qc] + jnp.dot(vt_scr[hh, :, pl.ds(ks, kn)], pts[hh],
                                                                            preferred_element_type=F32)
                    m_scr[hh, :, qc] = m_new[hh]

            def loop_body(ki, carry):
                block(pl.multiple_of(ki * t, t), t, 0, t, False)
                return carry

            lax.fori_loop(0, qi, loop_body, 0)
            block(qs, t, 0, t, True)
            outs = []
            for hh in range(2):
                acc = acc_scr[hh]
                l = acc[HEAD_DIM:HEAD_DIM + 1, :]
                outs.append(acc[:HEAD_DIM, :] / l)
                lse_ref[hh, :, pl.ds(qs, t)] = m_scr[hh] + jnp.log2(l)
            o_ref[pl.ds(qs, t), :] = jnp.concatenate(outs, axis=0).T
            return 0

        lax.fori_loop(0, nq, qbody, 0)
        if n_side:
            @pl.when(pj == PAIRS - 1)
            def _():
                _exchange_wait(_exchange_copies(side_in, side_out, side[1], *sems))

    def tok(blk):
        return pl.BlockSpec((s_len, LANES), lambda j: (0, blk + j))

    rowb = pl.BlockSpec((2, 1, s_len), lambda j: (j, 0, 0))
    hbm = pl.BlockSpec(memory_space=pl.ANY)
    ins = [q, k, v]
    in_specs = [tok(q_blk), tok(k_blk), tok(v_blk)]
    if bias is not None:
        ins.append(bias)
        in_specs.append(_full((s_len, LANES)))
    if rope is not None:
        ins += list(rope)
        in_specs += [pl.BlockSpec((s_len, LANES), lambda j: (0, j // 2)), _full((s_len, LANES))]
    out_shape = [_sds((s_len, PAIRS * LANES)), _sds((HEADS, 1, s_len))]
    scratch = [pltpu.VMEM((2, V_ROWS, s_len), v.dtype), pltpu.VMEM((2, 1, t), F32), pltpu.VMEM((2, V_ROWS, t), F32),
               pltpu.VMEM((2, s_len if bias is not None else 8, 1), F32)]
    if n_side:
        ins += list(side[0])
        out_shape += _exchange_out_shapes(*side)
        scratch += _exchange_sems(n_side)
    return pl.pallas_call(
        kern, name=name, grid=(PAIRS,), out_shape=out_shape,
        in_specs=in_specs + [hbm] * n_side, out_specs=[tok(0), rowb] + [hbm] * n_side,
        scratch_shapes=scratch,
        compiler_params=_params("arbitrary", side_effects=bool(n_side)),
    )(*ins)


def _attention_bwd(q, q_blk, k, k_blk, v, v_blk, do, pack, ck_row, rope, chunk_mask, q_scale, k_scale, name,
                   side=None):
    s_len = q.shape[0]
    t = min(ATTN_TILE, s_len // 2)
    nq = s_len // t
    has_bias = ck_row is not None
    kw = 2 * LANES if rope is not None else LANES
    n_side = len(side[0]) if side else 0

    def kern(*refs):
        q_ref, k_ref, v_ref, do_ref, pack_ref = refs[:5]
        pos = 5
        if has_bias:
            ck_ref = refs[pos]
            pos += 1
        if rope is not None:
            qr_ref, kr_ref = refs[pos:pos + 2]
            pos += 2
        side_in = refs[pos:pos + n_side]
        pos += n_side
        dq_ref, dk_ref, dv_ref = refs[pos:pos + 3]
        pos += 3
        if has_bias:
            dcq_ref, dck_ref = refs[pos:pos + 2]
            pos += 2
        if rope is not None:
            dqr_ref, dkr_ref = refs[pos:pos + 2]
            pos += 2
        side_out = refs[pos:pos + n_side]
        pos += n_side
        qt_scr, dot_scr, dkt_scr, dvt_scr, dq_scr, dcq_scr = refs[pos:pos + 6]
        sems = refs[pos + 6:]
        pj = pl.program_id(0)
        if n_side:
            @pl.when(pj == 0)
            def _():
                _exchange_start(_exchange_copies(side_in, side_out, side[1], *sems, recv=False))

        for i in range(nq):
            sl = slice(i * t, (i + 1) * t)
            qt_scr[:LANES, sl] = q_ref[sl, :].T
            dot_scr[:, sl] = do_ref[sl, :].T
            if rope is not None:
                qt_scr[LANES:, sl] = qr_ref[sl, :].T
        dkt_scr[...] = jnp.zeros(dkt_scr.shape, F32)
        dvt_scr[...] = jnp.zeros(dvt_scr.shape, F32)
        if has_bias:
            dck_ref[...] = jnp.zeros(dck_ref.shape, F32)

            @pl.when(pj == 0)
            def _():
                dcq_ref[...] = jnp.zeros(dcq_ref.shape, F32)

        def keep(x, hh, axis):
            return _keep_head(x, hh, axis, (pj % 2) * 2 + hh)

        def qbody(qi, _):
            qs = pl.multiple_of(qi * t, t)
            qt = q_ref[pl.ds(qs, t), :]
            if rope is not None:
                qt = jnp.concatenate([qt, qr_ref[pl.ds(qs, t), :]], axis=1)
            dot = do_ref[pl.ds(qs, t), :]
            pk = pack_ref[pl.ds(qs, t), :]
            lane = lax.broadcasted_iota(jnp.int32, pk.shape, 1)
            qh = [keep(qt, hh, 1) for hh in range(2)]
            doh = [keep(dot, hh, 1) for hh in range(2)]
            a_col = [jnp.sum(jnp.where(lane == 2 * pj + hh, pk, 0.0), axis=1, keepdims=True) for hh in range(2)]
            d_col = [jnp.sum(jnp.where(lane == HEADS + 2 * pj + hh, pk, 0.0), axis=1, keepdims=True)
                     for hh in range(2)]
            dq_scr[...] = jnp.zeros(dq_scr.shape, F32)
            if has_bias:
                dcq_scr[...] = jnp.zeros(dcq_scr.shape, F32)

            def block(ks, kn, q0, qn, masked):
                kt = k_ref[pl.ds(ks, kn), :]
                if rope is not None:
                    kt = jnp.concatenate([kt, kr_ref[pl.ds(ks, kn), :]], axis=1)
                vt = v_ref[pl.ds(ks, kn), :]
                qr_ = slice(q0, q0 + qn)
                qcols = pl.ds(pl.multiple_of(qs + q0, t // 2), qn)
                ss = [lax.dot_general(qh[hh][qr_], kt, _NT, preferred_element_type=F32) + a_col[hh][qr_]
                      for hh in range(2)]
                if has_bias:
                    ss = [ss[hh] - ck_ref[hh, :, pl.ds(ks, kn)] for hh in range(2)]
                dpds = [lax.dot_general(doh[hh][qr_], vt, _NT, preferred_element_type=F32) for hh in range(2)]
                ps = [jnp.exp2(s) for s in ss]
                if masked:
                    ps = [jnp.where(_block_mask(kn, qn, q0, chunk_mask, False), p, 0.0) for p in ps]
                dss = [ps[hh] * (dpds[hh] - d_col[hh][qr_]) for hh in range(2)]
                dq_acc = dq_scr[qr_, :]
                for hh in range(2):
                    rows = slice(hh * HEAD_DIM, (hh + 1) * HEAD_DIM)
                    dsb = dss[hh].astype(MXU)
                    dvt_scr[rows, pl.ds(ks, kn)] += jnp.dot(dot_scr[rows, qcols], ps[hh].astype(MXU),
                                                            preferred_element_type=F32)
                    dkt_scr[rows, pl.ds(ks, kn)] += jnp.dot(qt_scr[rows, qcols], dsb, preferred_element_type=F32)
                    if rope is not None:
                        rr = pl.ds(pl.multiple_of(LANES + ((pj % 2) * 2 + hh) * ROPE, ROPE), ROPE)
                        dkt_scr[rr, pl.ds(ks, kn)] += jnp.dot(qt_scr[rr, qcols], dsb, preferred_element_type=F32)
                    dq_acc = dq_acc + jnp.dot(dsb, keep(kt, hh, 1), preferred_element_type=F32)
                    if has_bias:
                        dcq_scr[hh, qr_, :] += jnp.sum(dss[hh], axis=1, keepdims=True)
                        dck_ref[hh, :, pl.ds(ks, kn)] += -jnp.sum(dss[hh], axis=0, keepdims=True)
                dq_scr[qr_, :] = dq_acc

            def loop_body(ki, carry):
                block(pl.multiple_of(ki * t, t), t, 0, t, False)
                return carry

            lax.fori_loop(0, qi, loop_body, 0)
            block(qs, t // 2, 0, t // 2, True)
            block(qs, t, t // 2, t // 2, True)
            dq_ref[pl.ds(qs, t), :] = (dq_scr[:, :LANES] * q_scale).astype(dq_ref.dtype)
            if rope is not None:
                dqr_ref[0, pl.ds(qs, t), :] = dq_scr[:, LANES:] * q_scale
            if has_bias:
                old = dcq_ref[pl.ds(qs, t), :]
                dcq_ref[pl.ds(qs, t), :] = jnp.where(lane == 2 * pj, dcq_scr[0],
                                                     jnp.where(lane == 2 * pj + 1, dcq_scr[1], old))
            return 0

        lax.fori_loop(0, nq, qbody, 0)
        for i in range(nq):
            sl = slice(i * t, (i + 1) * t)
            dk_ref[sl, :] = (dkt_scr[:LANES, sl].T * k_scale).astype(dk_ref.dtype)
            dv_ref[sl, :] = dvt_scr[:, sl].T.astype(dv_ref.dtype)
            if rope is not None:
                dkr_ref[0, sl, :] = dkt_scr[LANES:, sl].T * k_scale
        if n_side:
            @pl.when(pj == PAIRS - 1)
            def _():
                _exchange_wait(_exchange_copies(side_in, side_out, side[1], *sems))

    def tok(blk):
        return pl.BlockSpec((s_len, LANES), lambda j: (0, blk + j))

    shared = _full((s_len, LANES))
    rowb = pl.BlockSpec((2, 1, s_len), lambda j: (j, 0, 0))
    slab = pl.BlockSpec((1, s_len, LANES), lambda j: (j, 0, 0))
    hbm = pl.BlockSpec(memory_space=pl.ANY)
    ins = [q, k, v, do, pack]
    in_specs = [tok(q_blk), tok(k_blk), tok(v_blk), tok(0), shared]
    out_shape = [_sds((s_len, PAIRS * LANES), MXU)] * 3
    out_specs = [tok(0)] * 3
    if has_bias:
        ins.append(ck_row)
        in_specs.append(rowb)
        out_shape += [_sds((s_len, LANES)), _sds((HEADS, 1, s_len))]
        out_specs += [shared, rowb]
    if rope is not None:
        ins += list(rope)
        in_specs += [pl.BlockSpec((s_len, LANES), lambda j: (0, j // 2)), shared]
        out_shape += [_sds((PAIRS, s_len, LANES))] * 2
        out_specs += [slab, slab]
    scratch = [pltpu.VMEM((kw, s_len), q.dtype), pltpu.VMEM((LANES, s_len), do.dtype),
               pltpu.VMEM((kw, s_len), F32), pltpu.VMEM((LANES, s_len), F32),
               pltpu.VMEM((t, kw), F32), pltpu.VMEM((2, t, 1), F32)]
    if n_side:
        ins += list(side[0])
        out_shape += _exchange_out_shapes(*side)
        scratch += _exchange_sems(n_side)
    return pl.pallas_call(
        kern, name=name, grid=(PAIRS,), out_shape=out_shape,
        in_specs=in_specs + [hbm] * n_side, out_specs=out_specs + [hbm] * n_side, scratch_shapes=scratch,
        compiler_params=_params("arbitrary", side_effects=bool(n_side)),
    )(*ins)


def _silu(a):
    return a * jax.nn.sigmoid(a)


def _k_out(of, om, gates, x, gate, wout, tm=256):
    s_len, d = x.shape

    def kern(of_ref, om_ref, gates_ref, x_ref, gate_ref, w_ref, xo_ref, y_ref, u_ref):
        u_ref[:, :GROUP_W] = (of_ref[...] * _silu(gates_ref[:, :GROUP_W])).astype(MXU)
        u_ref[:, GROUP_W:] = (om_ref[...] * _silu(gates_ref[:, GROUP_W:])).astype(MXU)
        y = jnp.dot(u_ref[...], w_ref[...], preferred_element_type=F32)
        y_ref[...] = y
        xo_ref[...] = x_ref[...] + gate_ref[...] * y

    return pl.pallas_call(
        kern, name="k_out", grid=(s_len // tm,),
        out_shape=[_sds((s_len, d)), _sds((s_len, d)), _sds((s_len, 2 * GROUP_W), MXU)],
        in_specs=[_rows(tm, GROUP_W), _rows(tm, GROUP_W), _rows(tm, 2 * GROUP_W), _rows(tm, d), _full((1, d)),
                  _full((2 * GROUP_W, d))],
        out_specs=[_rows(tm, d), _rows(tm, d), _rows(tm, 2 * GROUP_W)],
        compiler_params=_params("arbitrary"),
    )(of, om, gates, x, gate, wout)


def _k_loss(x, gf, tgt, tm=256):
    s_len, d = x.shape

    def kern(x_ref, g_ref, t_ref, loss_ref, dx_ref, dg_ref):
        i = pl.program_id(0)
        xv = x_ref[...]
        r = lax.rsqrt(jnp.mean(xv * xv, axis=-1, keepdims=True) + EPS)
        xh = xv * r
        diff = xh * g_ref[...] - t_ref[...]
        part = 0.5 * jnp.sum(jnp.mean(diff * diff, axis=-1, keepdims=True))
        dout = diff * (1.0 / d)
        dxh = dout * g_ref[...]
        dx_ref[...] = r * (dxh - xh * jnp.mean(dxh * xh, axis=-1, keepdims=True))

        @pl.when(i == 0)
        def _():
            loss_ref[...] = jnp.zeros_like(loss_ref)
            dg_ref[...] = jnp.zeros_like(dg_ref)

        loss_ref[...] += jnp.full(loss_ref.shape, part, F32)
        dg_ref[...] += jnp.sum(dout * xh, axis=0, keepdims=True)

    return pl.pallas_call(
        kern, name="k_loss", grid=(s_len // tm,),
        out_shape=[_sds((1, LANES)), _sds((s_len, d)), _sds((1, d))],
        in_specs=[_rows(tm, d), _full((1, d)), _rows(tm, d)],
        out_specs=[_full((1, LANES)), _rows(tm, d), _full((1, d))],
        compiler_params=_params("arbitrary"),
    )(x, gf, tgt)


def _kb_out(dxo, y, gate, wout, of, om, gates, tm=256):
    s_len, d = dxo.shape

    def kern(dxo_ref, y_ref, gate_ref, wt_ref, of_ref, om_ref, gates_ref,
             dy_ref, dof_ref, dom_ref, dfg_ref, dmg_ref, dlf_ref, dlm_ref, dgate_ref):
        i = pl.program_id(0)
        dxv = dxo_ref[...]

        @pl.when(i == 0)
        def _():
            dgate_ref[...] = jnp.zeros_like(dgate_ref)

        dgate_ref[...] += jnp.sum(dxv * y_ref[...], axis=0, keepdims=True)
        dyb = (dxv * gate_ref[...]).astype(MXU)
        dy_ref[...] = dyb
        du = lax.dot_general(dyb, wt_ref[...], _NT, preferred_element_type=F32)
        head_of = (lax.broadcasted_iota(jnp.int32, (GROUP_W, LANES), 0) // HEAD_DIM
                   == lax.broadcasted_iota(jnp.int32, (GROUP_W, LANES), 1)).astype(F32)
        for du_g, o_ref, a, do_ref, dg_ref, dl_ref in (
                (du[:, :GROUP_W], of_ref, gates_ref[:, :GROUP_W], dof_ref, dfg_ref, dlf_ref),
                (du[:, GROUP_W:], om_ref, gates_ref[:, GROUP_W:], dom_ref, dmg_ref, dlm_ref)):
            sg = jax.nn.sigmoid(a)
            ov = o_ref[...]
            dov = du_g * (a * sg)
            do_ref[...] = dov.astype(MXU)
            dg_ref[...] = (du_g * ov * (sg * (1.0 + a * (1.0 - sg)))).astype(MXU)
            dl_ref[...] = jnp.dot(dov * ov, head_of, precision=lax.Precision.HIGH, preferred_element_type=F32)

    return pl.pallas_call(
        kern, name="kb_out", grid=(s_len // tm,),
        out_shape=[_sds((s_len, d), MXU), _sds((s_len, GROUP_W), MXU), _sds((s_len, GROUP_W), MXU),
                   _sds((s_len, GROUP_W), MXU), _sds((s_len, GROUP_W), MXU), _sds((s_len, LANES)),
                   _sds((s_len, LANES)), _sds((1, d))],
        in_specs=[_rows(tm, d), _rows(tm, d), _full((1, d)), _full((2 * GROUP_W, d)), _rows(tm, GROUP_W),
                  _rows(tm, GROUP_W), _rows(tm, 2 * GROUP_W)],
        out_specs=[_rows(tm, d), _rows(tm, GROUP_W), _rows(tm, GROUP_W), _rows(tm, GROUP_W),
                   _rows(tm, GROUP_W), _rows(tm, LANES), _rows(tm, LANES), _full((1, d))],
        compiler_params=_params("arbitrary"),
    )(dxo, y, gate, wout, of, om, gates)


def _kb_prep(dqn, dqr, dkn, dv, dkr4, dff, tail, cos, sin, gq, gkv, wuq_t, wukv_t, tm=512):
    s_len = tail.shape[0]
    qw = GROUP_W + ROPE_W

    def kern(dqn_ref, dqr_ref, dkn_ref, dv_ref, dkr_ref, dff_ref, tail_ref, cos_ref, sin_ref,
             gq_ref, gkv_ref, wuqt_ref, wukvt_ref, dq_ref, dz_ref, dgq_ref, dgkv_ref):
        i = pl.program_id(0)

        @pl.when(i == 0)
        def _():
            dgq_ref[...] = jnp.zeros_like(dgq_ref)
            dgkv_ref[...] = jnp.zeros_like(dgkv_ref)

        cs, sn = cos_ref[...], sin_ref[...]
        dq_ref[:, :GROUP_W] = dqn_ref[...]
        for half in range(ROPE_W // LANES):
            sl = slice(half * LANES, (half + 1) * LANES)
            dq_ref[:, GROUP_W + half * LANES:GROUP_W + (half + 1) * LANES] = _rope_bwd(dqr_ref[:, sl], cs, sn).astype(MXU)
        dqn = lax.dot_general(dq_ref[...], wuqt_ref[...], _NT, preferred_element_type=F32)
        ql = tail_ref[:, :T_KV]
        rq = lax.rsqrt(jnp.mean(ql * ql, axis=-1, keepdims=True) + EPS)
        qh = ql * rq
        dgq_ref[...] += jnp.sum(dqn * qh, axis=0, keepdims=True)
        dqh = dqn * gq_ref[...]
        dz_ref[:, :Q_LORA] = (rq * (dqh - qh * jnp.mean(dqh * qh, axis=-1, keepdims=True))).astype(MXU)

        dkvn = (lax.dot_general(dkn_ref[...], wukvt_ref[:, :GROUP_W], _NT, preferred_element_type=F32)
                + lax.dot_general(dv_ref[...], wukvt_ref[:, GROUP_W:], _NT, preferred_element_type=F32))
        kvl = tail_ref[:, T_KV:T_MISC]
        rk = lax.rsqrt(jnp.mean(kvl * kvl, axis=-1, keepdims=True) + EPS)
        kh = kvl * rk
        dgkv_ref[...] += jnp.sum(dkvn * kh, axis=0, keepdims=True)
        dkh = dkvn * gkv_ref[...]
        dz_ref[:, Q_LORA:Q_LORA + KV_LORA] = (
            rk * (dkh - kh * jnp.mean(dkh * kh, axis=-1, keepdims=True))).astype(MXU)

        g4 = dkr_ref[...]
        g = g4
        for rep in range(1, LANES // ROPE):
            g = g + pltpu.roll(g4, rep * ROPE, 1)
        lane = lax.broadcasted_iota(jnp.int32, g.shape, 1)
        dmisc = jnp.where(lane < ROPE, _rope_bwd(g, cs, sn), 0.0) + dff_ref[...]
        dz_ref[:, Q_LORA + KV_LORA:] = dmisc.astype(MXU)

    return pl.pallas_call(
        kern, name="kb_prep", grid=(s_len // tm,),
        out_shape=[_sds((s_len, qw), MXU), _sds((s_len, TAIL_W), MXU), _sds((1, Q_LORA)), _sds((1, KV_LORA))],
        in_specs=[_rows(tm, GROUP_W), _rows(tm, ROPE_W), _rows(tm, GROUP_W), _rows(tm, GROUP_W), _rows(tm, LANES),
                  _rows(tm, LANES), _rows(tm, TAIL_W),
                  _rows(tm, LANES), _rows(tm, LANES), _full((1, Q_LORA)), _full((1, KV_LORA)),
                  _full((Q_LORA, qw)), _full((KV_LORA, 2 * GROUP_W))],
        out_specs=[_rows(tm, qw), _rows(tm, TAIL_W), _full((1, Q_LORA)), _full((1, KV_LORA))],
        compiler_params=_params("arbitrary"),
    )(dqn, dqr, dkn, dv, dkr4, dff, tail, cos, sin, gq, gkv, wuq_t, wukv_t)


def _kb_in(dz_pieces, w, x, g, mod3, dxo, tm=256):
    s_len, d = x.shape
    widths = [p.shape[1] for p in dz_pieces]
    n_p = len(widths)

    def kern(*refs):
        dz_refs = refs[:n_p]
        w_ref, x_ref, g_ref, mod_ref, dxo_ref, dx_ref, acc_ref = refs[n_p:]
        i = pl.program_id(0)

        @pl.when(i == 0)
        def _():
            acc_ref[...] = jnp.zeros_like(acc_ref)

        dh = jnp.zeros((tm, d), F32)
        lo = 0
        for p_ref, wd in zip(dz_refs, widths):
            dh = dh + lax.dot_general(p_ref[...], w_ref[:, lo:lo + wd], _NT, preferred_element_type=F32)
            lo += wd
        xv = x_ref[...]
        r = lax.rsqrt(jnp.mean(xv * xv, axis=-1, keepdims=True) + EPS)
        xh = xv * r
        xn = xh * g_ref[...]
        dxn = dh * (1.0 + mod_ref[1:2, :])
        acc_ref[0:1, :] += jnp.sum(dh, axis=0, keepdims=True)
        acc_ref[1:2, :] += jnp.sum(dh * xn, axis=0, keepdims=True)
        acc_ref[2:3, :] += jnp.sum(dxn * xh, axis=0, keepdims=True)
        dxh = dxn * g_ref[...]
        dx_ref[...] = dxo_ref[...] + r * (dxh - xh * jnp.mean(dxh * xh, axis=-1, keepdims=True))

    return pl.pallas_call(
        kern, name="kb_in", grid=(s_len // tm,),
        out_shape=[_sds((s_len, d)), _sds((3, d))],
        in_specs=[_rows(tm, wd) for wd in widths] + [_full((d, Z_W)), _rows(tm, d), _full((1, d)), _full((3, d)),
                                                     _rows(tm, d)],
        out_specs=[_rows(tm, d), _full((3, d))],
        compiler_params=_params("arbitrary"),
    )(*dz_pieces, w, x, g, mod3, dxo)


def _weight_grad(a, pieces, name, tk=512):
    s_len, m = a.shape
    widths = [p.shape[1] for p in pieces]
    n = sum(widths)
    tk = min(tk, s_len)

    def kern(a_ref, *refs):
        o_ref = refs[-1]

        @pl.when(pl.program_id(0) == 0)
        def _():
            o_ref[...] = jnp.zeros_like(o_ref)

        at = a_ref[...]
        lo = 0
        for p_ref, w in zip(refs[:-1], widths):
            o_ref[:, lo:lo + w] += lax.dot_general(at, p_ref[...], _TN, preferred_element_type=F32)
            lo += w

    return pl.pallas_call(
        kern, name=name, grid=(s_len // tk,), out_shape=_sds((m, n)),
        in_specs=[_rows(tk, m)] + [_rows(tk, w) for w in widths],
        out_specs=_full((m, n)),
        compiler_params=_params("arbitrary"),
    )(a, *pieces)


def _adamw(slabs, w, m, v, name):
    n_l = len(slabs)
    n, r, c = slabs[0].shape
    tm = r
    for cand in (256, 128, 64, 32, 16, 8):
        if r % cand == 0:
            tm = cand
            break
    steps = r // tm

    def kern(*refs):
        g_refs = refs[:n_l]
        w_ref, m_ref, v_ref, go_ref, d_ref, mo_ref, vo_ref, g_scr = refs[n_l:]
        for ll in range(n_l):
            @pl.when(pl.program_id(0) == ll)
            def _(g_ref=g_refs[ll]):
                g = g_ref[0].astype(F32)
                for s in range(1, n):
                    g = g + g_ref[s].astype(F32)
                g_scr[...] = g

        g = g_scr[...]
        m_new = ADAM_B1 * m_ref[...] + (1.0 - ADAM_B1) * g
        v_new = ADAM_B2 * v_ref[...] + (1.0 - ADAM_B2) * (g * g)
        m_hat = m_new / (1.0 - ADAM_B1 ** ADAM_STEP)
        v_hat = v_new / (1.0 - ADAM_B2 ** ADAM_STEP)
        go_ref[...] = g
        mo_ref[...] = m_new
        vo_ref[...] = v_new
        d_ref[...] = -ADAM_LR * (m_hat / (jnp.sqrt(v_hat) + ADAM_EPS) + ADAM_WD * w_ref[...])

    row = pl.BlockSpec((tm, c), lambda l, i: (l * steps + i, 0))

    def slab_spec(ll):
        return pl.BlockSpec((n, tm, c), lambda l, i: (0, jnp.where(l == ll, i, 0), 0))

    return pl.pallas_call(
        kern, name=name, grid=(n_l, steps), out_shape=[_sds((n_l * r, c))] * 4,
        in_specs=[slab_spec(ll) for ll in range(n_l)] + [row, row, row],
        out_specs=[row] * 4,
        scratch_shapes=[pltpu.VMEM((tm, c), F32)],
        compiler_params=_params("arbitrary", "arbitrary"),
    )(*slabs, w, m, v)


def _perm_w_in(w):
    pad = jnp.zeros(w.shape[:-1] + (Z_W - Z_MISC - ROPE - HEADS,), w.dtype)
    return jnp.concatenate([w[..., 0:1536], w[..., 1544:2056], w[..., 2472:2984], w[..., 2056:2312],
                            w[..., 2312:2440], w[..., 2440:2472], w[..., 1536:1544], pad], axis=-1)


def _unperm_w_in(g):
    ff0 = Z_MISC + MISC_FF
    return jnp.concatenate([g[..., 0:1536], g[..., ff0:ff0 + HEADS], g[..., Z_FG:Z_FG + GROUP_W],
                            g[..., Z_QL:Z_QL + Q_LORA], g[..., Z_KV:Z_KV + KV_LORA],
                            g[..., Z_MISC:Z_MISC + ROPE], g[..., Z_MG:Z_MG + GROUP_W]], axis=-1)


def _perm_w_uq(w):
    lead = w.shape[:-1]
    wh = w.reshape(lead + (HEADS, NOPE + ROPE))
    return jnp.concatenate([wh[..., :NOPE].reshape(lead + (GROUP_W,)),
                            wh[..., NOPE:].reshape(lead + (ROPE_W,))], axis=-1)


def _unperm_w_uq(g):
    lead = g.shape[:-1]
    parts = [g[..., :GROUP_W].reshape(lead + (HEADS, NOPE)), g[..., GROUP_W:].reshape(lead + (HEADS, ROPE))]
    return jnp.concatenate(parts, axis=-1).reshape(lead + (HEADS * (NOPE + ROPE),))


def _perm_w_ukv(w):
    lead = w.shape[:-1]
    wh = w.reshape(lead + (HEADS, 2 * HEAD_DIM))
    return jnp.concatenate([wh[..., :NOPE].reshape(lead + (GROUP_W,)),
                            wh[..., NOPE:].reshape(lead + (GROUP_W,))], axis=-1)


def _unperm_w_ukv(g):
    lead = g.shape[:-1]
    parts = [g[..., :GROUP_W].reshape(lead + (HEADS, NOPE)), g[..., GROUP_W:].reshape(lead + (HEADS, HEAD_DIM))]
    return jnp.concatenate(parts, axis=-1).reshape(lead + (2 * GROUP_W,))


def _rope_tables(positions):
    inv_freq = 1.0 / (ROPE_THETA ** (jnp.arange(0, ROPE, 2, dtype=F32) / ROPE))
    ang = positions.astype(F32)[:, None] * inv_freq
    cos, sin = jnp.cos(ang), jnp.sin(ang)
    reps = LANES // ROPE
    return jnp.tile(jnp.concatenate([cos, cos], axis=1), (1, reps)), jnp.tile(jnp.concatenate([-sin, sin], axis=1), (1, reps))


def _full_weights(g_in, g_uq, g_ukv, g_out):
    def cols(g):
        return g.transpose(1, 0, 2).reshape(g.shape[1], -1)
    return (_perm_w_in(cols(g_in)), _perm_w_uq(cols(g_uq)), _perm_w_ukv(cols(g_ukv)),
            g_out.reshape(-1, g_out.shape[2]))


def _grad_slabs(dw_in, dw_uq, dw_ukv, dw_out):
    def cols(g):
        return g.reshape(g.shape[0], N_DEV, -1).transpose(1, 0, 2)
    return [cols(_unperm_w_in(dw_in)), cols(_unperm_w_uq(dw_uq)), cols(_unperm_w_ukv(dw_ukv)),
            dw_out.reshape(N_DEV, -1, dw_out.shape[1])]


def _local_step(x, mod, positions, loss_target, norm_g, b_f, q_norm_g, kv_norm_g, final_g, weights, shards=None):
    n_l = norm_g.shape[0]
    s_len, d = x.shape
    cos, sin = _rope_tables(positions)
    qb, kb, vb = Z_FQ // LANES, Z_FK // LANES, Z_FV // LANES
    chunks = s_len // LANES
    weights = list(weights)

    def pack_rows(a_rows, delta):
        return jnp.concatenate([a_rows.T, delta[:, :HEADS], jnp.zeros((s_len, LANES - 2 * HEADS), F32)], axis=1)

    saved = []
    for l in range(n_l):
        w_in, w_uq, w_ukv, w_out = weights[l]
        mod3 = mod[l].reshape(3, d)
        h, qkv, gates, tail = _k_in(x, norm_g[l][None], mod3, w_in)
        fft = tail[:, T_MISC + MISC_FF:T_MISC + MISC_FF + HEADS].T.reshape(HEADS * chunks, LANES)
        bf = jnp.repeat(b_f[l], chunks)[:, None]
        c2 = _k_cum(fft, bf, chunks).reshape(HEADS, s_len) * LOG2E
        side = (list(shards[l + 1]), [True] * 4) if shards is not None and l + 1 < n_l else None
        ck_lanes = jnp.pad(c2.T, ((0, 0), (0, LANES - HEADS)))
        of, lse_f, *gathered = _attention_fwd(qkv, qb, qkv, kb, qkv, vb, ck_lanes, None, False,
                                              "fox_fwd_gather" if side else "fox_fwd", side)
        if side:
            weights.append(_full_weights(*gathered))
        mq, mqr, mk, mv, kr4, qn, kvn = _k_prep(tail, cos, sin, q_norm_g[l][None], kv_norm_g[l][None], w_uq, w_ukv)
        om, lse_m = _attention_fwd(mq, 0, mk, 0, mv, 0, None, (mqr, kr4), True, "mla_fwd")
        x_new, y, u = _k_out(of, om, gates, x, mod3[2:3], w_out)
        saved.append((x, gates, tail, h, qkv, fft, bf, c2, lse_f, mq, mqr, mk, mv, kr4, lse_m, of, om, qn, kvn, y, u,
                      mod3))
        x = x_new

    loss_row, dx, dfinal = _k_loss(x, final_g[None], loss_target)

    grads = {k: [] for k in ("norm_g", "mod", "w_in", "b_f", "q_norm_g", "w_uq", "kv_norm_g", "w_ukv", "w_out")}
    received, pending = {}, None
    for l in range(n_l - 1, -1, -1):
        (x_l, gates, tail, h, qkv, fft, bf, c2, lse_f, mq, mqr, mk, mv, kr4, lse_m, of, om, qn, kvn, y, u,
         mod3) = saved[l]
        w_in, w_uq, w_ukv, w_out = weights[l]
        dyb, dof, dom, dfg, dmg, dlt_f, dlt_m, dgate = _kb_out(dx, y, mod3[2:3], w_out, of, om, gates)
        dw_out = _weight_grad(u, [dyb], "dw_out")

        side = (pending, [False] * 4) if pending is not None else None
        dfq, dfk, dfv, dcq, dck, *arrived = _attention_bwd(
            qkv, qb, qkv, kb, qkv, vb, dof, pack_rows(-lse_f.reshape(HEADS, s_len), dlt_f), c2[:, None, :], None,
            False, FOX_SCALE, 1.0 / LOG2E, "fox_bwd_exchange" if side else "fox_bwd", side)
        if side:
            received[l + 1] = arrived
        dcum = (dcq[:, :HEADS].T + dck.reshape(HEADS, s_len)).reshape(HEADS * chunks, LANES)
        dff_rows, dbf_rows = _k_cum_bwd(dcum, fft, bf, chunks)
        dfft = dff_rows.reshape(HEADS, s_len)
        grads["b_f"].append(jnp.sum(dbf_rows[:, 0].reshape(HEADS, chunks), axis=1))

        dmq, dkn, dmv, dqr4, dkr4s = _attention_bwd(
            mq, 0, mk, 0, mv, 0, dom, pack_rows(-lse_m.reshape(HEADS, s_len), dlt_m), None, (mqr, kr4), True,
            MLA_SCALE, 1.0 / LOG2E, "mla_bwd")
        dqr = jnp.concatenate([dqr4[0] + dqr4[1], dqr4[2] + dqr4[3]], axis=1)
        dkr4 = dkr4s[0] + dkr4s[1] + dkr4s[2] + dkr4s[3]
        dff = jnp.pad(dfft.T, ((0, 0), (MISC_FF, LANES - MISC_FF - HEADS)))
        dq_b, dz_tail, dgq, dgkv = _kb_prep(dmq, dqr, dkn, dmv, dkr4, dff, tail, cos, sin, q_norm_g[l][None],
                                            kv_norm_g[l][None], w_uq, w_ukv)
        grads["q_norm_g"].append(dgq[0])
        grads["kv_norm_g"].append(dgkv[0])
        dw_uq = _weight_grad(qn, [dq_b], "dw_uq")
        dw_ukv = _weight_grad(kvn, [dkn, dmv], "dw_ukv")
        dz = [dfq, dfk, dfv, dfg, dmg, dz_tail]
        dw_in = _weight_grad(h, dz, "dw_in")
        dx, acc3 = _kb_in(dz, w_in, x_l, norm_g[l][None], mod3, dx)
        grads["norm_g"].append(acc3[2])
        grads["mod"].append(jnp.concatenate([acc3[0], acc3[1], dgate[0]]))
        if shards is not None:
            pending = _grad_slabs(dw_in, dw_uq, dw_ukv, dw_out)
        else:
            for name, g in (("w_in", dw_in), ("w_uq", dw_uq), ("w_ukv", dw_ukv), ("w_out", dw_out)):
                grads[name].append(g)
    grads = {k: jnp.stack(v[::-1]) for k, v in grads.items() if v}
    grads["final_g"] = dfinal[0]
    if shards is None:
        return loss_row[0, 0], dx, grads
    return loss_row[0, 0], dx, grads, received, pending


def _pack_small(parts, total):
    flat = jnp.concatenate([p.reshape(-1) for p in parts])
    return jnp.pad(flat, (0, total - flat.shape[0])).reshape(total // LANES, LANES)


def kernel(x, c, positions, norm_g, w_ada, b_ada, w_in, b_f, q_norm_g, w_uq, kv_norm_g, w_ukv, w_out, final_g, loss_target, m_norm_g, m_w_ada, m_b_ada, m_w_in, m_b_f, m_q_norm_g, m_w_uq, m_kv_norm_g, m_w_ukv, m_w_out, m_final_g, v_norm_g, v_w_ada, v_b_ada, v_w_in, v_b_f, v_q_norm_g, v_w_uq, v_kv_norm_g, v_w_ukv, v_w_out, v_final_g):
    n_l, d = norm_g.shape
    me = 4 * lax.axis_index("x") + 2 * lax.axis_index("y") + lax.axis_index("c")
    ada_c = w_ada.shape[2]

    cact = jnp.broadcast_to(jax.nn.silu(c), (N_DEV, d))
    shards = [[w[l].astype(MXU) for w in (w_in, w_uq, w_ukv, w_out)] for l in range(n_l)]
    *g_w0, g_cact = _exchange(shards[0] + [cact], [True] * 5, "gather_layer0")
    cact_all = g_cact[:, 0, :]

    b_cols = lax.dynamic_slice_in_dim(b_ada, me * ada_c, ada_c, axis=1)[:, None, :]
    modpart = _modpart(cact_all, w_ada, b_cols)
    mod_send = jnp.pad(modpart.transpose(1, 0, 2), ((0, 0), (0, 8 - n_l), (0, 0)))
    (mod_recv,) = _exchange([mod_send], [False], "scatter_mod")
    mod = mod_recv.transpose(1, 0, 2).reshape(8, N_DEV * ada_c)[:n_l]

    loss, dx, gr, received, pending = _local_step(x[0], mod, positions[0], loss_target[0], norm_g, b_f, q_norm_g,
                                                  kv_norm_g, final_g, [_full_weights(*g_w0)], shards)

    small_parts = [gr["norm_g"], gr["mod"], gr["b_f"], gr["q_norm_g"], gr["kv_norm_g"], gr["final_g"], cact[0]]
    sizes = [int(np.prod(p.shape)) for p in small_parts]
    total = -(-sum(sizes) // 1024) * 1024
    small = _pack_small(small_parts, total)
    *received[0], r_small = _exchange([p.astype(MXU) for p in pending] + [small],
                                      [False, False, False, False, True], "exchange_layer0")
    r_in, r_uq, r_ukv, r_out = ([received[l][i] for l in range(n_l)] for i in range(4))

    def upd(slabs, w, m, v, name):
        shp = w.shape
        w2, m2, v2 = (a.reshape(-1, slabs[0].shape[2]) for a in (w, m, v))
        return [o.reshape(shp) for o in _adamw(slabs, w2, m2, v2, name)]

    o_in = upd(r_in, w_in, m_w_in, v_w_in, "adamw_w_in")
    o_uq = upd(r_uq, w_uq, m_w_uq, v_w_uq, "adamw_w_uq")
    o_ukv = upd(r_ukv, w_ukv, m_w_ukv, v_w_ukv, "adamw_w_ukv")
    o_out = upd(r_out, w_out, m_w_out, v_w_out, "adamw_w_out")

    offs = np.cumsum([0] + sizes)
    flat_all = r_small.reshape(N_DEV, total)
    dmod_all = flat_all[:, offs[1]:offs[2]].reshape(N_DEV, n_l, 3 * d)
    dmod_cols = lax.dynamic_slice_in_dim(dmod_all, me * ada_c, ada_c, axis=2).transpose(1, 0, 2)
    cact_cols = flat_all[:, offs[6]:offs[7]][:, :, None]
    g_ada = _ada_grad(cact_cols, dmod_cols)
    o_ada = upd([g_ada.reshape(1, n_l * d, ada_c)], w_ada, m_w_ada, v_w_ada, "adamw_w_ada")

    zero_c = jnp.zeros((d,), F32)
    small_w = [_pack_small([norm_g, b_ada, b_f, q_norm_g, kv_norm_g, final_g, zero_c], total),
               _pack_small([m_norm_g, m_b_ada, m_b_f, m_q_norm_g, m_kv_norm_g, m_final_g, zero_c], total),
               _pack_small([v_norm_g, v_b_ada, v_b_f, v_q_norm_g, v_kv_norm_g, v_final_g, zero_c], total)]
    o_small = [o.reshape(-1) for o in _adamw([r_small], *small_w, "adamw_small")]
    shapes = [norm_g.shape, b_ada.shape, b_f.shape, q_norm_g.shape, kv_norm_g.shape, final_g.shape]

    def small_out(kind, idx):
        return o_small[kind][offs[idx]:offs[idx + 1]].reshape(shapes[idx])

    loss_all = lax.psum(loss, ("x", "y", "c"))
    outs = [loss_all, dx[None]]
    for kind in range(4):
        outs += [small_out(kind, 0), o_ada[kind], small_out(kind, 1), o_in[kind], small_out(kind, 2),
                 small_out(kind, 3), o_uq[kind], small_out(kind, 4), o_ukv[kind], o_out[kind], small_out(kind, 5)]
    return tuple(outs)
```

```python
import jax
import jax.numpy as jnp
import numpy as np
from jax import lax
from jax.experimental import pallas as pl
from jax.experimental.pallas import tpu as pltpu

F32 = jnp.float32
MXU = jnp.bfloat16

N_DEV = 8
HEADS = 8
PAIRS = HEADS // 2
HEAD_DIM = 64
NOPE = 64
ROPE = 32
HALF_ROPE = ROPE // 2
Q_LORA = 256
KV_LORA = 128
CHUNK = 64
GROUP_W = HEADS * HEAD_DIM
ROPE_W = HEADS * ROPE
EPS = 1e-6
ROPE_THETA = 10000.0
N_IN = 2984

Z_FQ, Z_FK, Z_FV, Z_FG, Z_MG, Z_QL, Z_KV, Z_MISC, Z_W = 0, 512, 1024, 1536, 2048, 2560, 2816, 2944, 3072
MISC_FF = ROPE
TAIL_W = Z_W - Z_QL
T_KV, T_MISC = Q_LORA, Q_LORA + KV_LORA

ADAM_LR = 0.001
ADAM_B1 = 0.9
ADAM_B2 = 0.999
ADAM_EPS = 1e-08
ADAM_WD = 0.01
ADAM_STEP = 10

VMEM_LIMIT_V7X = 56 * 1024 * 1024
LANES = 128
ATTN_TILE = 512
V_ROWS = HEAD_DIM + 16
LOG2E = 1.4426950408889634
FOX_SCALE = HEAD_DIM ** -0.5
MLA_SCALE = (NOPE + ROPE) ** -0.5

_NT = (((1,), (1,)), ((), ()))
_TN = (((0,), (0,)), ((), ()))


def _params(*sem, side_effects=False):
    return pltpu.CompilerParams(dimension_semantics=sem, vmem_limit_bytes=VMEM_LIMIT_V7X,
                                has_side_effects=side_effects)


def _sds(shape, dtype=F32):
    return jax.ShapeDtypeStruct(shape, dtype)


def _full(shape):
    nd = len(shape)
    return pl.BlockSpec(shape, lambda *_: (0,) * nd)


def _rows(tm, width, col=0):
    return pl.BlockSpec((tm, width), lambda i: (i, col))


def _exchange(arrs, gather, name):
    n = len(arrs)

    def kern(*refs):
        copies = _exchange_copies(refs[:n], refs[n:2 * n], gather, *refs[2 * n:])
        _exchange_start(copies)
        _exchange_wait(copies)

    return pl.pallas_call(
        kern, name=name, out_shape=_exchange_out_shapes(arrs, gather),
        in_specs=[pl.BlockSpec(memory_space=pl.ANY)] * n,
        out_specs=[pl.BlockSpec(memory_space=pl.ANY)] * n,
        scratch_shapes=_exchange_sems(n),
        compiler_params=pltpu.CompilerParams(has_side_effects=True),
    )(*arrs)


def _exchange_out_shapes(arrs, gather):
    return [_sds((N_DEV,) + tuple(a.shape) if g else tuple(a.shape), a.dtype) for a, g in zip(arrs, gather)]


def _exchange_sems(n):
    return [pltpu.SemaphoreType.DMA((n, N_DEV)), pltpu.SemaphoreType.DMA((n, N_DEV)), pltpu.SemaphoreType.DMA((n,))]


def _exchange_copies(ins, outs, gather, send_sems, recv_sems, loc_sems, recv=True):
    n = len(ins)
    x, y, c = lax.axis_index("x"), lax.axis_index("y"), lax.axis_index("c")
    me = 4 * x + 2 * y + c

    def src(i, j):
        return ins[i] if gather[i] else ins[i].at[j]

    local = [pltpu.make_async_copy(src(i, me), outs[i].at[me], loc_sems.at[i]) for i in range(n)]
    sends, recvs = [], []
    for k in range(1, N_DEV):
        px = 1 - x if k & 4 else x
        py = 1 - y if k & 2 else y
        pc = 1 - c if k & 1 else c
        p = 4 * px + 2 * py + pc
        for i in range(n):
            sends.append(pltpu.make_async_remote_copy(
                src_ref=src(i, p), dst_ref=outs[i].at[me], send_sem=send_sems.at[i, k],
                recv_sem=recv_sems.at[i, k], device_id=(px, py, pc), device_id_type=pl.DeviceIdType.MESH))
            if recv:
                recvs.append(pltpu.make_async_remote_copy(
                    src_ref=src(i, p), dst_ref=outs[i].at[p], send_sem=send_sems.at[i, k],
                    recv_sem=recv_sems.at[i, k], device_id=(px, py, pc), device_id_type=pl.DeviceIdType.MESH))
    return local, sends, recvs


def _exchange_start(copies):
    local, sends, _ = copies
    for cp in local + sends:
        cp.start()


def _exchange_wait(copies):
    local, sends, recvs = copies
    for cp in recvs:
        cp.wait_recv()
    for cp in sends:
        cp.wait_send()
    for cp in local:
        cp.wait()


def _modpart(cact8, w_ada, b_cols):
    n_l, d, cw = w_ada.shape

    def kern(c_ref, w_ref, b_ref, o_ref):
        o_ref[0] = jnp.dot(c_ref[...].astype(MXU), w_ref[0].astype(MXU), preferred_element_type=F32) + b_ref[0]

    return pl.pallas_call(
        kern, name="modpart", grid=(n_l,), out_shape=_sds((n_l, N_DEV, cw)),
        in_specs=[_full((N_DEV, d)), pl.BlockSpec((1, d, cw), lambda l: (l, 0, 0)),
                  pl.BlockSpec((1, 1, cw), lambda l: (l, 0, 0))],
        out_specs=pl.BlockSpec((1, N_DEV, cw), lambda l: (l, 0, 0)),
        compiler_params=_params("arbitrary"),
    )(cact8, w_ada, b_cols)


def _ada_grad(cact_cols, dmod_cols):
    n_l, _, cw = dmod_cols.shape
    d = cact_cols.shape[1]

    def kern(c_ref, dm_ref, o_ref):
        acc = c_ref[0] * dm_ref[0, 0:1, :]
        for s in range(1, N_DEV):
            acc = acc + c_ref[s] * dm_ref[0, s:s + 1, :]
        o_ref[0] = acc

    return pl.pallas_call(
        kern, name="ada_grad", grid=(n_l,), out_shape=_sds((n_l, d, cw)),
        in_specs=[_full((N_DEV, d, 1)), pl.BlockSpec((1, N_DEV, cw), lambda l: (l, 0, 0))],
        out_specs=pl.BlockSpec((1, d, cw), lambda l: (l, 0, 0)),
        compiler_params=_params("arbitrary"),
    )(cact_cols, dmod_cols)


def _k_in(x, g, mod3, w, tm=256):
    s_len, d = x.shape
    qkv_w = 3 * GROUP_W

    def kern(x_ref, g_ref, mod_ref, w_ref, h_ref, qkv_ref, gates_ref, tail_ref):
        xv = x_ref[...]
        r = lax.rsqrt(jnp.mean(xv * xv, axis=-1, keepdims=True) + EPS)
        xn = xv * r * g_ref[...]
        h = (xn * (1.0 + mod_ref[1:2, :]) + mod_ref[0:1, :]).astype(MXU)
        h_ref[...] = h
        z = jnp.dot(h, w_ref[...], preferred_element_type=F32)
        qkv_ref[:, :GROUP_W] = (z[:, Z_FQ:Z_FQ + GROUP_W] * (FOX_SCALE * LOG2E)).astype(MXU)
        qkv_ref[:, GROUP_W:] = z[:, Z_FK:Z_FK + 2 * GROUP_W].astype(MXU)
        gates_ref[...] = z[:, Z_FG:Z_QL]
        tail_ref[...] = z[:, Z_QL:]

    return pl.pallas_call(
        kern, name="k_in", grid=(s_len // tm,),
        out_shape=[_sds((s_len, d), MXU), _sds((s_len, qkv_w), MXU), _sds((s_len, Z_QL - Z_FG)),
                   _sds((s_len, TAIL_W))],
        in_specs=[_rows(tm, d), _full((1, d)), _full((3, d)), _full((d, Z_W))],
        out_specs=[_rows(tm, d), _rows(tm, qkv_w), _rows(tm, Z_QL - Z_FG), _rows(tm, TAIL_W)],
        compiler_params=_params("arbitrary"),
    )(x, g, mod3, w)


def _scan_matrices(rows, chunks, reverse):
    r_i = lax.broadcasted_iota(jnp.int32, (LANES, LANES), 0)
    c_i = lax.broadcasted_iota(jnp.int32, (LANES, LANES), 1)
    a_i = lax.broadcasted_iota(jnp.int32, (rows, rows), 0)
    b_i = lax.broadcasted_iota(jnp.int32, (rows, rows), 1)
    same_head = (a_i // chunks) == (b_i // chunks)
    if reverse:
        return (r_i >= c_i).astype(F32), (same_head & (b_i > a_i)).astype(F32)
    return (r_i <= c_i).astype(F32), (same_head & (b_i < a_i)).astype(F32)


def _scan_rows(x, inner, outer):
    tot = jnp.broadcast_to(jnp.sum(x, axis=1, keepdims=True), x.shape)
    return (jnp.dot(x, inner, precision=lax.Precision.HIGHEST, preferred_element_type=F32)
            + jnp.dot(outer, tot, precision=lax.Precision.HIGHEST, preferred_element_type=F32))


def _k_cum(ff_rows, b_rows, chunks):
    rows = ff_rows.shape[0]

    def kern(ff_ref, b_ref, cum_ref):
        xc = ff_ref[...] + b_ref[...]
        lf = jnp.minimum(xc, 0.0) - jnp.log(1.0 + jnp.exp(-jnp.abs(xc)))
        cum_ref[...] = _scan_rows(lf, *_scan_matrices(rows, chunks, False))

    return pl.pallas_call(
        kern, name="k_cum", out_shape=_sds((rows, LANES)),
        in_specs=[pl.BlockSpec(memory_space=pltpu.VMEM)] * 2,
        out_specs=pl.BlockSpec(memory_space=pltpu.VMEM),
        compiler_params=_params(),
    )(ff_rows, b_rows)


def _k_cum_bwd(dc_rows, ff_rows, b_rows, chunks):
    rows = ff_rows.shape[0]

    def kern(dc_ref, ff_ref, b_ref, dff_ref, db_ref):
        dlf = _scan_rows(dc_ref[...], *_scan_matrices(rows, chunks, True))
        dff = dlf * jax.nn.sigmoid(-(ff_ref[...] + b_ref[...]))
        dff_ref[...] = dff
        db_ref[...] = jnp.broadcast_to(jnp.sum(dff, axis=1, keepdims=True), dff.shape)

    return pl.pallas_call(
        kern, name="k_cum_bwd", out_shape=[_sds((rows, LANES)), _sds((rows, LANES))],
        in_specs=[pl.BlockSpec(memory_space=pltpu.VMEM)] * 3,
        out_specs=[pl.BlockSpec(memory_space=pltpu.VMEM)] * 2,
        compiler_params=_params(),
    )(dc_rows, ff_rows, b_rows)


def _swap16(t):
    lane = lax.broadcasted_iota(jnp.int32, t.shape, 1)
    return jnp.where(lane % ROPE < HALF_ROPE, pltpu.roll(t, LANES - HALF_ROPE, 1), pltpu.roll(t, HALF_ROPE, 1))


def _rope(t, cos, sin):
    return t * cos + _swap16(t) * sin


def _rope_bwd(dt, cos, sin):
    return dt * cos - _swap16(dt) * sin


def _k_prep(tail, cos, sin, gq, gkv, wuq, wukv, tm=512):
    s_len = tail.shape[0]
    qc = MLA_SCALE * LOG2E

    def kern(tail_ref, cos_ref, sin_ref, gq_ref, gkv_ref, wuq_ref, wukv_ref,
             qn_out, qr_out, kn_out, v_out, kr_out, qn_ref, kvn_ref):
        cs, sn = cos_ref[...], sin_ref[...]
        ql = tail_ref[:, :T_KV]
        rq = lax.rsqrt(jnp.mean(ql * ql, axis=-1, keepdims=True) + EPS)
        qn = (ql * rq * gq_ref[...]).astype(MXU)
        qn_ref[...] = qn
        q = jnp.dot(qn, wuq_ref[...], preferred_element_type=F32)
        qn_out[...] = (q[:, :GROUP_W] * qc).astype(MXU)
        for blk in range(PAIRS):
            lo = GROUP_W + blk * LANES
            qr_out[:, blk * LANES:(blk + 1) * LANES] = (_rope(q[:, lo:lo + LANES], cs, sn) * qc).astype(MXU)
        kvl = tail_ref[:, T_KV:T_MISC]
        rk = lax.rsqrt(jnp.mean(kvl * kvl, axis=-1, keepdims=True) + EPS)
        kvn = (kvl * rk * gkv_ref[...]).astype(MXU)
        kvn_ref[...] = kvn
        kv = jnp.dot(kvn, wukv_ref[...], preferred_element_type=F32)
        kn_out[...] = kv[:, :GROUP_W].astype(MXU)
        v_out[...] = kv[:, GROUP_W:].astype(MXU)
        misc = tail_ref[:, T_MISC:]
        lane = lax.broadcasted_iota(jnp.int32, misc.shape, 1)
        kr = jnp.where(lane < ROPE, _rope(misc, cs, sn), 0.0)
        kr_out[...] = (kr + pltpu.roll(kr, HEAD_DIM, 1)).astype(MXU)

    return pl.pallas_call(
        kern, name="k_prep", grid=(s_len // tm,),
        out_shape=[_sds((s_len, GROUP_W), MXU), _sds((s_len, GROUP_W), MXU), _sds((s_len, GROUP_W), MXU),
                   _sds((s_len, GROUP_W), MXU), _sds((s_len, LANES), MXU), _sds((s_len, Q_LORA), MXU),
                   _sds((s_len, KV_LORA), MXU)],
        in_specs=[_rows(tm, TAIL_W), _rows(tm, LANES), _rows(tm, LANES),
                  _full((1, Q_LORA)), _full((1, KV_LORA)), _full((Q_LORA, 2 * GROUP_W)),
                  _full((KV_LORA, 2 * GROUP_W))],
        out_specs=[_rows(tm, GROUP_W), _rows(tm, GROUP_W), _rows(tm, GROUP_W), _rows(tm, GROUP_W), _rows(tm, LANES),
                   _rows(tm, Q_LORA), _rows(tm, KV_LORA)],
        compiler_params=_params("arbitrary"),
    )(tail, cos, sin, gq, gkv, wuq, wukv)


def _block_mask(kn, qn, q_off, chunk_mask, transposed):
    shape = (kn, qn) if transposed else (qn, kn)
    row = lax.broadcasted_iota(jnp.int32, shape, 0)
    col = lax.broadcasted_iota(jnp.int32, shape, 1)
    qi, ki = (col + q_off, row) if transposed else (row + q_off, col)
    if chunk_mask:
        return (ki // CHUNK) <= (qi // CHUNK)
    return ki <= qi


def _head_operand(x, hh, other=None):
    lane = lax.broadcasted_iota(jnp.int32, x.shape, 1)
    own = (lane >= hh * HEAD_DIM) & (lane < (hh + 1) * HEAD_DIM)
    return jnp.where(own, x, jnp.zeros_like(x) if other is None else other)


def _attention_fwd(q, q_blk, k, k_blk, v, v_blk, bias, rope, chunk_mask, name, side=None):
    s_len = q.shape[0]
    t = min(ATTN_TILE, s_len // 2)
    nq = s_len // t
    n_side = len(side[0]) if side else 0

    def kern(*refs):
        q_ref, k_ref, v_ref = refs[:3]
        pos = 3
        if bias is not None:
            ck_ref = refs[pos]
            pos += 1
        if rope is not None:
            qr_ref, kr_ref = refs[pos:pos + 2]
            pos += 2
        side_in = refs[pos:pos + n_side]
        pos += n_side
        o_ref, lse_ref = refs[pos:pos + 2]
        side_out = refs[pos + 2:pos + 2 + n_side]
        vt_scr, m_scr, acc_scr, ck_scr = refs[pos + 2 + n_side:pos + 6 + n_side]
        sems = refs[pos + 6 + n_side:]
        pj = pl.program_id(0)
        if n_side:
            @pl.when(pj == 0)
            def _():
                _exchange_start(_exchange_copies(side_in, side_out, side[1], *sems, recv=False))
        vt_scr[:, HEAD_DIM:, :] = jnp.ones((2, V_ROWS - HEAD_DIM, s_len), vt_scr.dtype)
        for i in range(nq):
            vtt = v_ref[i * t:(i + 1) * t, :].T
            for hh in range(2):
                vt_scr[hh, :HEAD_DIM, i * t:(i + 1) * t] = vtt[hh * HEAD_DIM:(hh + 1) * HEAD_DIM, :]
                if bias is not None:
                    ckt = ck_ref[i * t:(i + 1) * t, :]
                    lane = lax.broadcasted_iota(jnp.int32, ckt.shape, 1)
                    ck_scr[hh, i * t:(i + 1) * t, :] = jnp.sum(jnp.where(lane == 2 * pj + hh, ckt, 0.0), axis=1,
                                                               keepdims=True)

        def qbody(qi, _):
            qs = pl.multiple_of(qi * t, t)
            qt = q_ref[pl.ds(qs, t), :]
            qrt = qr_ref[pl.ds(qs, t), :] if rope is not None else None
            qh = [_head_operand(qt, hh, qrt) for hh in range(2)]
            m_scr[...] = jnp.full(m_scr.shape, -jnp.inf, F32)
            acc_scr[...] = jnp.zeros(acc_scr.shape, F32)

            def block(ks, kn, q0, qn, masked):
                kt = k_ref[pl.ds(ks, kn), :]
                kh = [_head_operand(kt, hh, kr_ref[pl.ds(ks, kn), :]) for hh in range(2)] if rope is not None else [kt, kt]
                qc = slice(q0, q0 + qn)
                sts = [lax.dot_general(kh[hh], qh[hh][qc], _NT, preferred_element_type=F32) for hh in range(2)]
                if bias is not None:
                    sts = [sts[hh] - ck_scr[hh, pl.ds(ks, kn), :] for hh in range(2)]
                if masked:
                    sts = [jnp.where(_block_mask(kn, qn, q0, chunk_mask, True), st, -jnp.inf) for st in sts]
                m_old = [m_scr[hh, :, qc] for hh in range(2)]
                m_new = [jnp.maximum(m_old[hh], jnp.max(sts[hh], axis=0, keepdims=True)) for hh in range(2)]
                pts = [jnp.exp2(sts[hh] - m_new[hh]).astype(MXU) for hh in range(2)]
                for hh in range(2):
                    alpha = jnp.exp2(m_old[hh] - m_new[hh])
                    acc_scr[hh, :, qc] = alpha * acc_scr[hh, :, qc] + jnp.dot(vt_scr[hh, :, pl.ds(ks, kn)], pts[hh],
                                                                            preferred_element_type=F32)
                    m_scr[hh, :, qc] = m_new[hh]

            def loop_body(ki, carry):
                block(pl.multiple_of(ki * t, t), t, 0, t, False)
                return carry

            lax.fori_loop(0, qi, loop_body, 0)
            block(qs, t, 0, t, True)
            outs = []
            for hh in range(2):
                acc = acc_scr[hh]
                l = acc[HEAD_DIM:HEAD_DIM + 1, :]
                outs.append(acc[:HEAD_DIM, :] / l)
                lse_ref[hh, :, pl.ds(qs, t)] = m_scr[hh] + jnp.log2(l)
            o_ref[pl.ds(qs, t), :] = jnp.concatenate(outs, axis=0).T
            return 0

        lax.fori_loop(0, nq, qbody, 0)
        if n_side:
            @pl.when(pj == PAIRS - 1)
            def _():
                _exchange_wait(_exchange_copies(side_in, side_out, side[1], *sems))

    def tok(blk):
        return pl.BlockSpec((s_len, LANES), lambda j: (0, blk + j))

    rowb = pl.BlockSpec((2, 1, s_len), lambda j: (j, 0, 0))
    hbm = pl.BlockSpec(memory_space=pl.ANY)
    ins = [q, k, v]
    in_specs = [tok(q_blk), tok(k_blk), tok(v_blk)]
    if bias is not None:
        ins.append(bias)
        in_specs.append(_full((s_len, LANES)))
    if rope is not None:
        ins += list(rope)
        in_specs += [tok(0), _full((s_len, LANES))]
    out_shape = [_sds((s_len, PAIRS * LANES)), _sds((HEADS, 1, s_len))]
    scratch = [pltpu.VMEM((2, V_ROWS, s_len), v.dtype), pltpu.VMEM((2, 1, t), F32), pltpu.VMEM((2, V_ROWS, t), F32),
               pltpu.VMEM((2, s_len if bias is not None else 8, 1), F32)]
    if n_side:
        ins += list(side[0])
        out_shape += _exchange_out_shapes(*side)
        scratch += _exchange_sems(n_side)
    return pl.pallas_call(
        kern, name=name, grid=(PAIRS,), out_shape=out_shape,
        in_specs=in_specs + [hbm] * n_side, out_specs=[tok(0), rowb] + [hbm] * n_side,
        scratch_shapes=scratch,
        compiler_params=_params("arbitrary", side_effects=bool(n_side)),
    )(*ins)


def _attention_bwd(q, q_blk, k, k_blk, v, v_blk, do, pack, ck_row, rope, chunk_mask, q_scale, k_scale, name,
                   side=None):
    s_len = q.shape[0]
    t = min(ATTN_TILE, s_len // 2)
    nq = s_len // t
    has_bias = ck_row is not None
    nv = 2 if rope is not None else 1
    n_side = len(side[0]) if side else 0

    def kern(*refs):
        q_ref, k_ref, v_ref, do_ref, pack_ref = refs[:5]
        pos = 5
        if has_bias:
            ck_ref = refs[pos]
            pos += 1
        if rope is not None:
            qr_ref, kr_ref = refs[pos:pos + 2]
            pos += 2
        side_in = refs[pos:pos + n_side]
        pos += n_side
        dq_ref, dk_ref, dv_ref = refs[pos:pos + 3]
        pos += 3
        if has_bias:
            dcq_ref, dck_ref = refs[pos:pos + 2]
            pos += 2
        if rope is not None:
            dqr_ref, dkr_ref = refs[pos:pos + 2]
            pos += 2
        side_out = refs[pos:pos + n_side]
        pos += n_side
        qt_scr, dot_scr, dkt_scr, dvt_scr, dq_scr, dcq_scr = refs[pos:pos + 6]
        sems = refs[pos + 6:]
        pj = pl.program_id(0)
        if n_side:
            @pl.when(pj == 0)
            def _():
                _exchange_start(_exchange_copies(side_in, side_out, side[1], *sems, recv=False))

        for i in range(nq):
            sl = slice(i * t, (i + 1) * t)
            dot_scr[:, sl] = do_ref[sl, :].T
            if rope is not None:
                for hh in range(2):
                    qt_scr[hh, :, sl] = _head_operand(q_ref[sl, :], hh, qr_ref[sl, :]).T
            else:
                qt_scr[0, :, sl] = q_ref[sl, :].T
        dkt_scr[...] = jnp.zeros(dkt_scr.shape, F32)
        dvt_scr[...] = jnp.zeros(dvt_scr.shape, F32)
        if has_bias:
            dck_ref[...] = jnp.zeros(dck_ref.shape, F32)

            @pl.when(pj == 0)
            def _():
                dcq_ref[...] = jnp.zeros(dcq_ref.shape, F32)

        def qbody(qi, _):
            qs = pl.multiple_of(qi * t, t)
            qt = q_ref[pl.ds(qs, t), :]
            qrt = qr_ref[pl.ds(qs, t), :] if rope is not None else None
            dot = do_ref[pl.ds(qs, t), :]
            pk = pack_ref[pl.ds(qs, t), :]
            lane = lax.broadcasted_iota(jnp.int32, pk.shape, 1)
            qh = [_head_operand(qt, hh, qrt) for hh in range(2)]
            doh = [_head_operand(dot, hh) for hh in range(2)]
            a_col = [jnp.sum(jnp.where(lane == 2 * pj + hh, pk, 0.0), axis=1, keepdims=True) for hh in range(2)]
            d_col = [jnp.sum(jnp.where(lane == HEADS + 2 * pj + hh, pk, 0.0), axis=1, keepdims=True)
                     for hh in range(2)]
            dq_scr[...] = jnp.zeros(dq_scr.shape, F32)
            if has_bias:
                dcq_scr[...] = jnp.zeros(dcq_scr.shape, F32)

            def block(ks, kn, q0, qn, masked):
                kt = k_ref[pl.ds(ks, kn), :]
                krt = kr_ref[pl.ds(ks, kn), :] if rope is not None else None
                kh = [_head_operand(kt, hh, krt) for hh in range(2)]
                vt = v_ref[pl.ds(ks, kn), :]
                qr_ = slice(q0, q0 + qn)
                qcols = pl.ds(pl.multiple_of(qs + q0, t // 2), qn)
                ss = [lax.dot_general(qh[hh][qr_], kh[hh] if rope is not None else kt, _NT,
                                      preferred_element_type=F32) + a_col[hh][qr_] for hh in range(2)]
                if has_bias:
                    ss = [ss[hh] - ck_ref[hh, :, pl.ds(ks, kn)] for hh in range(2)]
                dpds = [lax.dot_general(doh[hh][qr_], vt, _NT, preferred_element_type=F32) for hh in range(2)]
                ps = [jnp.exp2(s) for s in ss]
                if masked:
                    ps = [jnp.where(_block_mask(kn, qn, q0, chunk_mask, False), p, 0.0) for p in ps]
                dss = [ps[hh] * (dpds[hh] - d_col[hh][qr_]) for hh in range(2)]
                for hh in range(2):
                    rows = slice(hh * HEAD_DIM, (hh + 1) * HEAD_DIM)
                    dsb = dss[hh].astype(MXU)
                    dvt_scr[rows, pl.ds(ks, kn)] += jnp.dot(dot_scr[rows, qcols], ps[hh].astype(MXU),
                                                            preferred_element_type=F32)
                    if rope is not None:
                        dkt_scr[hh, :, pl.ds(ks, kn)] += jnp.dot(qt_scr[hh, :, qcols], dsb,
                                                                 preferred_element_type=F32)
                    else:
                        dkt_scr[0, rows, pl.ds(ks, kn)] += jnp.dot(qt_scr[0, rows, qcols], dsb,
                                                                   preferred_element_type=F32)
                    dq_scr[hh if rope is not None else 0, qr_, :] += jnp.dot(dsb, kh[hh], preferred_element_type=F32)
                    if has_bias:
                        dcq_scr[hh, qr_, :] += jnp.sum(dss[hh], axis=1, keepdims=True)
                        dck_ref[hh, :, pl.ds(ks, kn)] += -jnp.sum(dss[hh], axis=0, keepdims=True)

            def loop_body(ki, carry):
                block(pl.multiple_of(ki * t, t), t, 0, t, False)
                return carry

            lax.fori_loop(0, qi, loop_body, 0)
            block(qs, t // 2, 0, t // 2, True)
            block(qs, t, t // 2, t // 2, True)
            if rope is not None:
                first = lane < HEAD_DIM
                dq_ref[pl.ds(qs, t), :] = (jnp.where(first, dq_scr[0], dq_scr[1]) * q_scale).astype(dq_ref.dtype)
                dqr_ref[pl.ds(qs, t), :] = jnp.where(first, dq_scr[1], dq_scr[0]) * q_scale
            else:
                dq_ref[pl.ds(qs, t), :] = (dq_scr[0] * q_scale).astype(dq_ref.dtype)
            if has_bias:
                old = dcq_ref[pl.ds(qs, t), :]
                dcq_ref[pl.ds(qs, t), :] = jnp.where(lane == 2 * pj, dcq_scr[0],
                                                     jnp.where(lane == 2 * pj + 1, dcq_scr[1], old))
            return 0

        lax.fori_loop(0, nq, qbody, 0)
        for i in range(nq):
            sl = slice(i * t, (i + 1) * t)
            dv_ref[sl, :] = dvt_scr[:, sl].T.astype(dv_ref.dtype)
            if rope is not None:
                d0, d1 = dkt_scr[0, :, sl], dkt_scr[1, :, sl]
                first = lax.broadcasted_iota(jnp.int32, d0.shape, 0) < HEAD_DIM
                dk_ref[sl, :] = (jnp.where(first, d0, d1).T * k_scale).astype(dk_ref.dtype)
                dkr_ref[0, sl, :] = jnp.where(first, d1, d0).T * k_scale
            else:
                dk_ref[sl, :] = (dkt_scr[0, :, sl].T * k_scale).astype(dk_ref.dtype)
        if n_side:
            @pl.when(pj == PAIRS - 1)
            def _():
                _exchange_wait(_exchange_copies(side_in, side_out, side[1], *sems))

    def tok(blk):
        return pl.BlockSpec((s_len, LANES), lambda j: (0, blk + j))

    shared = _full((s_len, LANES))
    rowb = pl.BlockSpec((2, 1, s_len), lambda j: (j, 0, 0))
    slab = pl.BlockSpec((1, s_len, LANES), lambda j: (j, 0, 0))
    hbm = pl.BlockSpec(memory_space=pl.ANY)
    ins = [q, k, v, do, pack]
    in_specs = [tok(q_blk), tok(k_blk), tok(v_blk), tok(0), shared]
    out_shape = [_sds((s_len, PAIRS * LANES), MXU)] * 3
    out_specs = [tok(0)] * 3
    if has_bias:
        ins.append(ck_row)
        in_specs.append(rowb)
        out_shape += [_sds((s_len, LANES)), _sds((HEADS, 1, s_len))]
        out_specs += [shared, rowb]
    if rope is not None:
        ins += list(rope)
        in_specs += [tok(0), shared]
        out_shape += [_sds((s_len, PAIRS * LANES)), _sds((PAIRS, s_len, LANES))]
        out_specs += [tok(0), slab]
    scratch = [pltpu.VMEM((nv, LANES, s_len), q.dtype), pltpu.VMEM((LANES, s_len), do.dtype),
               pltpu.VMEM((nv, LANES, s_len), F32), pltpu.VMEM((LANES, s_len), F32),
               pltpu.VMEM((nv, t, LANES), F32), pltpu.VMEM((2, t, 1), F32)]
    if n_side:
        ins += list(side[0])
        out_shape += _exchange_out_shapes(*side)
        scratch += _exchange_sems(n_side)
    return pl.pallas_call(
        kern, name=name, grid=(PAIRS,), out_shape=out_shape,
        in_specs=in_specs + [hbm] * n_side, out_specs=out_specs + [hbm] * n_side, scratch_shapes=scratch,
        compiler_params=_params("arbitrary", side_effects=bool(n_side)),
    )(*ins)


def _silu(a):
    return a * jax.nn.sigmoid(a)


def _k_out(of, om, gates, x, gate, wout, tm=256):
    s_len, d = x.shape

    def kern(of_ref, om_ref, gates_ref, x_ref, gate_ref, w_ref, xo_ref, y_ref, u_ref):
        u_ref[:, :GROUP_W] = (of_ref[...] * _silu(gates_ref[:, :GROUP_W])).astype(MXU)
        u_ref[:, GROUP_W:] = (om_ref[...] * _silu(gates_ref[:, GROUP_W:])).astype(MXU)
        y = jnp.dot(u_ref[...], w_ref[...], preferred_element_type=F32)
        y_ref[...] = y
        xo_ref[...] = x_ref[...] + gate_ref[...] * y

    return pl.pallas_call(
        kern, name="k_out", grid=(s_len // tm,),
        out_shape=[_sds((s_len, d)), _sds((s_len, d)), _sds((s_len, 2 * GROUP_W), MXU)],
        in_specs=[_rows(tm, GROUP_W), _rows(tm, GROUP_W), _rows(tm, 2 * GROUP_W), _rows(tm, d), _full((1, d)),
                  _full((2 * GROUP_W, d))],
        out_specs=[_rows(tm, d), _rows(tm, d), _rows(tm, 2 * GROUP_W)],
        compiler_params=_params("arbitrary"),
    )(of, om, gates, x, gate, wout)


def _k_loss(x, gf, tgt, tm=256):
    s_len, d = x.shape

    def kern(x_ref, g_ref, t_ref, loss_ref, dx_ref, dg_ref):
        i = pl.program_id(0)
        xv = x_ref[...]
        r = lax.rsqrt(jnp.mean(xv * xv, axis=-1, keepdims=True) + EPS)
        xh = xv * r
        diff = xh * g_ref[...] - t_ref[...]
        part = 0.5 * jnp.sum(jnp.mean(diff * diff, axis=-1, keepdims=True))
        dout = diff * (1.0 / d)
        dxh = dout * g_ref[...]
        dx_ref[...] = r * (dxh - xh * jnp.mean(dxh * xh, axis=-1, keepdims=True))

        @pl.when(i == 0)
        def _():
            loss_ref[...] = jnp.zeros_like(loss_ref)
            dg_ref[...] = jnp.zeros_like(dg_ref)

        loss_ref[...] += jnp.full(loss_ref.shape, part, F32)
        dg_ref[...] += jnp.sum(dout * xh, axis=0, keepdims=True)

    return pl.pallas_call(
        kern, name="k_loss", grid=(s_len // tm,),
        out_shape=[_sds((1, LANES)), _sds((s_len, d)), _sds((1, d))],
        in_specs=[_rows(tm, d), _full((1, d)), _rows(tm, d)],
        out_specs=[_full((1, LANES)), _rows(tm, d), _full((1, d))],
        compiler_params=_params("arbitrary"),
    )(x, gf, tgt)


def _kb_out(dxo, y, gate, wout, of, om, gates, tm=256):
    s_len, d = dxo.shape

    def kern(dxo_ref, y_ref, gate_ref, wt_ref, of_ref, om_ref, gates_ref,
             dy_ref, dof_ref, dom_ref, dfg_ref, dmg_ref, dlf_ref, dlm_ref, dgate_ref):
        i = pl.program_id(0)
        dxv = dxo_ref[...]

        @pl.when(i == 0)
        def _():
            dgate_ref[...] = jnp.zeros_like(dgate_ref)

        dgate_ref[...] += jnp.sum(dxv * y_ref[...], axis=0, keepdims=True)
        dyb = (dxv * gate_ref[...]).astype(MXU)
        dy_ref[...] = dyb
        du = lax.dot_general(dyb, wt_ref[...], _NT, preferred_element_type=F32)
        head_of = (lax.broadcasted_iota(jnp.int32, (GROUP_W, LANES), 0) // HEAD_DIM
                   == lax.broadcasted_iota(jnp.int32, (GROUP_W, LANES), 1)).astype(F32)
        for du_g, o_ref, a, do_ref, dg_ref, dl_ref in (
                (du[:, :GROUP_W], of_ref, gates_ref[:, :GROUP_W], dof_ref, dfg_ref, dlf_ref),
                (du[:, GROUP_W:], om_ref, gates_ref[:, GROUP_W:], dom_ref, dmg_ref, dlm_ref)):
            sg = jax.nn.sigmoid(a)
            ov = o_ref[...]
            dov = du_g * (a * sg)
            do_ref[...] = dov.astype(MXU)
            dg_ref[...] = (du_g * ov * (sg * (1.0 + a * (1.0 - sg)))).astype(MXU)
            dl_ref[...] = jnp.dot(dov * ov, head_of, precision=lax.Precision.HIGH, preferred_element_type=F32)

    return pl.pallas_call(
        kern, name="kb_out", grid=(s_len // tm,),
        out_shape=[_sds((s_len, d), MXU), _sds((s_len, GROUP_W), MXU), _sds((s_len, GROUP_W), MXU),
                   _sds((s_len, GROUP_W), MXU), _sds((s_len, GROUP_W), MXU), _sds((s_len, LANES)),
                   _sds((s_len, LANES)), _sds((1, d))],
        in_specs=[_rows(tm, d), _rows(tm, d), _full((1, d)), _full((2 * GROUP_W, d)), _rows(tm, GROUP_W),
                  _rows(tm, GROUP_W), _rows(tm, 2 * GROUP_W)],
        out_specs=[_rows(tm, d), _rows(tm, GROUP_W), _rows(tm, GROUP_W), _rows(tm, GROUP_W),
                   _rows(tm, GROUP_W), _rows(tm, LANES), _rows(tm, LANES), _full((1, d))],
        compiler_params=_params("arbitrary"),
    )(dxo, y, gate, wout, of, om, gates)


def _kb_prep(dqn, dqr, dkn, dv, dkr, dff, tail, cos, sin, gq, gkv, wuq_t, wukv_t, tm=512):
    s_len = tail.shape[0]
    qw = 2 * GROUP_W

    def kern(dqn_ref, dqr_ref, dkn_ref, dv_ref, dkr_ref, dff_ref, tail_ref, cos_ref, sin_ref,
             gq_ref, gkv_ref, wuqt_ref, wukvt_ref, dq_ref, dz_ref, dgq_ref, dgkv_ref):
        i = pl.program_id(0)

        @pl.when(i == 0)
        def _():
            dgq_ref[...] = jnp.zeros_like(dgq_ref)
            dgkv_ref[...] = jnp.zeros_like(dgkv_ref)

        cs, sn = cos_ref[...], sin_ref[...]
        dq_ref[:, :GROUP_W] = dqn_ref[...]
        for blk in range(PAIRS):
            sl = slice(blk * LANES, (blk + 1) * LANES)
            dq_ref[:, GROUP_W + blk * LANES:GROUP_W + (blk + 1) * LANES] = _rope_bwd(dqr_ref[:, sl], cs, sn).astype(MXU)
        dqn = lax.dot_general(dq_ref[...], wuqt_ref[...], _NT, preferred_element_type=F32)
        ql = tail_ref[:, :T_KV]
        rq = lax.rsqrt(jnp.mean(ql * ql, axis=-1, keepdims=True) + EPS)
        qh = ql * rq
        dgq_ref[...] += jnp.sum(dqn * qh, axis=0, keepdims=True)
        dqh = dqn * gq_ref[...]
        dz_ref[:, :Q_LORA] = (rq * (dqh - qh * jnp.mean(dqh * qh, axis=-1, keepdims=True))).astype(MXU)

        dkvn = (lax.dot_general(dkn_ref[...], wukvt_ref[:, :GROUP_W], _NT, preferred_element_type=F32)
                + lax.dot_general(dv_ref[...], wukvt_ref[:, GROUP_W:], _NT, preferred_element_type=F32))
        kvl = tail_ref[:, T_KV:T_MISC]
        rk = lax.rsqrt(jnp.mean(kvl * kvl, axis=-1, keepdims=True) + EPS)
        kh = kvl * rk
        dgkv_ref[...] += jnp.sum(dkvn * kh, axis=0, keepdims=True)
        dkh = dkvn * gkv_ref[...]
        dz_ref[:, Q_LORA:Q_LORA + KV_LORA] = (
            rk * (dkh - kh * jnp.mean(dkh * kh, axis=-1, keepdims=True))).astype(MXU)

        g = dkr_ref[...] + pltpu.roll(dkr_ref[...], HEAD_DIM, 1)
        lane = lax.broadcasted_iota(jnp.int32, g.shape, 1)
        dmisc = jnp.where(lane < ROPE, _rope_bwd(g, cs, sn), 0.0) + dff_ref[...]
        dz_ref[:, Q_LORA + KV_LORA:] = dmisc.astype(MXU)

    return pl.pallas_call(
        kern, name="kb_prep", grid=(s_len // tm,),
        out_shape=[_sds((s_len, qw), MXU), _sds((s_len, TAIL_W), MXU), _sds((1, Q_LORA)), _sds((1, KV_LORA))],
        in_specs=[_rows(tm, GROUP_W), _rows(tm, GROUP_W), _rows(tm, GROUP_W), _rows(tm, GROUP_W), _rows(tm, LANES),
                  _rows(tm, LANES), _rows(tm, TAIL_W),
                  _rows(tm, LANES), _rows(tm, LANES), _full((1, Q_LORA)), _full((1, KV_LORA)),
                  _full((Q_LORA, qw)), _full((KV_LORA, 2 * GROUP_W))],
        out_specs=[_rows(tm, qw), _rows(tm, TAIL_W), _full((1, Q_LORA)), _full((1, KV_LORA))],
        compiler_params=_params("arbitrary"),
    )(dqn, dqr, dkn, dv, dkr, dff, tail, cos, sin, gq, gkv, wuq_t, wukv_t)


def _kb_in(dz_pieces, w, x, g, mod3, dxo, tm=256):
    s_len, d = x.shape
    widths = [p.shape[1] for p in dz_pieces]
    n_p = len(widths)

    def kern(*refs):
        dz_refs = refs[:n_p]
        w_ref, x_ref, g_ref, mod_ref, dxo_ref, dx_ref, acc_ref = refs[n_p:]
        i = pl.program_id(0)

        @pl.when(i == 0)
        def _():
            acc_ref[...] = jnp.zeros_like(acc_ref)

        dh = jnp.zeros((tm, d), F32)
        lo = 0
        for p_ref, wd in zip(dz_refs, widths):
            dh = dh + lax.dot_general(p_ref[...], w_ref[:, lo:lo + wd], _NT, preferred_element_type=F32)
            lo += wd
        xv = x_ref[...]
        r = lax.rsqrt(jnp.mean(xv * xv, axis=-1, keepdims=True) + EPS)
        xh = xv * r
        xn = xh * g_ref[...]
        dxn = dh * (1.0 + mod_ref[1:2, :])
        acc_ref[0:1, :] += jnp.sum(dh, axis=0, keepdims=True)
        acc_ref[1:2, :] += jnp.sum(dh * xn, axis=0, keepdims=True)
        acc_ref[2:3, :] += jnp.sum(dxn * xh, axis=0, keepdims=True)
        dxh = dxn * g_ref[...]
        dx_ref[...] = dxo_ref[...] + r * (dxh - xh * jnp.mean(dxh * xh, axis=-1, keepdims=True))

    return pl.pallas_call(
        kern, name="kb_in", grid=(s_len // tm,),
        out_shape=[_sds((s_len, d)), _sds((3, d))],
        in_specs=[_rows(tm, wd) for wd in widths] + [_full((d, Z_W)), _rows(tm, d), _full((1, d)), _full((3, d)),
                                                     _rows(tm, d)],
        out_specs=[_rows(tm, d), _full((3, d))],
        compiler_params=_params("arbitrary"),
    )(*dz_pieces, w, x, g, mod3, dxo)


def _weight_grad(a, pieces, name, tk=512):
    s_len, m = a.shape
    widths = [p.shape[1] for p in pieces]
    n = sum(widths)
    tk = min(tk, s_len)

    def kern(a_ref, *refs):
        o_ref = refs[-1]

        @pl.when(pl.program_id(0) == 0)
        def _():
            o_ref[...] = jnp.zeros_like(o_ref)

        at = a_ref[...]
        lo = 0
        for p_ref, w in zip(refs[:-1], widths):
            o_ref[:, lo:lo + w] += lax.dot_general(at, p_ref[...], _TN, preferred_element_type=F32)
            lo += w

    return pl.pallas_call(
        kern, name=name, grid=(s_len // tk,), out_shape=_sds((m, n)),
        in_specs=[_rows(tk, m)] + [_rows(tk, w) for w in widths],
        out_specs=_full((m, n)),
        compiler_params=_params("arbitrary"),
    )(a, *pieces)


def _adamw(slabs, w, m, v, name):
    n_l = len(slabs)
    n, r, c = slabs[0].shape
    tm = r
    for cand in (256, 128, 64, 32, 16, 8):
        if r % cand == 0:
            tm = cand
            break
    steps = r // tm

    def kern(*refs):
        g_refs = refs[:n_l]
        w_ref, m_ref, v_ref, go_ref, d_ref, mo_ref, vo_ref, g_scr = refs[n_l:]
        for ll in range(n_l):
            @pl.when(pl.program_id(0) == ll)
            def _(g_ref=g_refs[ll]):
                g = g_ref[0].astype(F32)
                for s in range(1, n):
                    g = g + g_ref[s].astype(F32)
                g_scr[...] = g

        g = g_scr[...]
        m_new = ADAM_B1 * m_ref[...] + (1.0 - ADAM_B1) * g
        v_new = ADAM_B2 * v_ref[...] + (1.0 - ADAM_B2) * (g * g)
        m_hat = m_new / (1.0 - ADAM_B1 ** ADAM_STEP)
        v_hat = v_new / (1.0 - ADAM_B2 ** ADAM_STEP)
        go_ref[...] = g
        mo_ref[...] = m_new
        vo_ref[...] = v_new
        d_ref[...] = -ADAM_LR * (m_hat / (jnp.sqrt(v_hat) + ADAM_EPS) + ADAM_WD * w_ref[...])

    row = pl.BlockSpec((tm, c), lambda l, i: (l * steps + i, 0))

    def slab_spec(ll):
        return pl.BlockSpec((n, tm, c), lambda l, i: (0, jnp.where(l == ll, i, 0), 0))

    return pl.pallas_call(
        kern, name=name, grid=(n_l, steps), out_shape=[_sds((n_l * r, c))] * 4,
        in_specs=[slab_spec(ll) for ll in range(n_l)] + [row, row, row],
        out_specs=[row] * 4,
        scratch_shapes=[pltpu.VMEM((tm, c), F32)],
        compiler_params=_params("arbitrary", "arbitrary"),
    )(*slabs, w, m, v)


def _perm_w_in(w):
    pad = jnp.zeros(w.shape[:-1] + (Z_W - Z_MISC - ROPE - HEADS,), w.dtype)
    return jnp.concatenate([w[..., 0:1536], w[..., 1544:2056], w[..., 2472:2984], w[..., 2056:2312],
                            w[..., 2312:2440], w[..., 2440:2472], w[..., 1536:1544], pad], axis=-1)


def _unperm_w_in(g):
    ff0 = Z_MISC + MISC_FF
    return jnp.concatenate([g[..., 0:1536], g[..., ff0:ff0 + HEADS], g[..., Z_FG:Z_FG + GROUP_W],
                            g[..., Z_QL:Z_QL + Q_LORA], g[..., Z_KV:Z_KV + KV_LORA],
                            g[..., Z_MISC:Z_MISC + ROPE], g[..., Z_MG:Z_MG + GROUP_W]], axis=-1)


def _perm_w_uq(w):
    lead = w.shape[:-1]
    wh = w.reshape(lead + (PAIRS, 2, NOPE + ROPE))
    zero = jnp.zeros(lead + (PAIRS, HEAD_DIM - ROPE), w.dtype)
    rope = jnp.concatenate([wh[..., 1, NOPE:], zero, wh[..., 0, NOPE:], zero], axis=-1)
    return jnp.concatenate([wh[..., :NOPE].reshape(lead + (GROUP_W,)), rope.reshape(lead + (GROUP_W,))], axis=-1)


def _unperm_w_uq(g):
    lead = g.shape[:-1]
    nope = g[..., :GROUP_W].reshape(lead + (PAIRS, 2, NOPE))
    rp = g[..., GROUP_W:].reshape(lead + (PAIRS, 2, HEAD_DIM))[..., :ROPE]
    return jnp.concatenate([nope, rp[..., ::-1, :]], axis=-1).reshape(lead + (HEADS * (NOPE + ROPE),))


def _perm_w_ukv(w):
    lead = w.shape[:-1]
    wh = w.reshape(lead + (HEADS, 2 * HEAD_DIM))
    return jnp.concatenate([wh[..., :NOPE].reshape(lead + (GROUP_W,)),
                            wh[..., NOPE:].reshape(lead + (GROUP_W,))], axis=-1)


def _unperm_w_ukv(g):
    lead = g.shape[:-1]
    parts = [g[..., :GROUP_W].reshape(lead + (HEADS, NOPE)), g[..., GROUP_W:].reshape(lead + (HEADS, HEAD_DIM))]
    return jnp.concatenate(parts, axis=-1).reshape(lead + (2 * GROUP_W,))


def _rope_tables(positions):
    inv_freq = 1.0 / (ROPE_THETA ** (jnp.arange(0, ROPE, 2, dtype=F32) / ROPE))
    ang = positions.astype(F32)[:, None] * inv_freq
    cos, sin = jnp.cos(ang), jnp.sin(ang)
    reps = LANES // ROPE
    return jnp.tile(jnp.concatenate([cos, cos], axis=1), (1, reps)), jnp.tile(jnp.concatenate([-sin, sin], axis=1), (1, reps))


def _full_weights(g_in, g_uq, g_ukv, g_out):
    def cols(g):
        return g.transpose(1, 0, 2).reshape(g.shape[1], -1)
    return (_perm_w_in(cols(g_in)), _perm_w_uq(cols(g_uq)), _perm_w_ukv(cols(g_ukv)),
            g_out.reshape(-1, g_out.shape[2]))


def _grad_slabs(dw_in, dw_uq, dw_ukv, dw_out):
    def cols(g):
        return g.reshape(g.shape[0], N_DEV, -1).transpose(1, 0, 2)
    return [cols(_unperm_w_in(dw_in)), cols(_unperm_w_uq(dw_uq)), cols(_unperm_w_ukv(dw_ukv)),
            dw_out.reshape(N_DEV, -1, dw_out.shape[1])]


def _local_step(x, mod, positions, loss_target, norm_g, b_f, q_norm_g, kv_norm_g, final_g, weights, shards=None):
    n_l = norm_g.shape[0]
    s_len, d = x.shape
    cos, sin = _rope_tables(positions)
    qb, kb, vb = Z_FQ // LANES, Z_FK // LANES, Z_FV // LANES
    chunks = s_len // LANES
    weights = list(weights)

    def pack_rows(a_rows, delta):
        return jnp.concatenate([a_rows.T, delta[:, :HEADS], jnp.zeros((s_len, LANES - 2 * HEADS), F32)], axis=1)

    saved = []
    for l in range(n_l):
        w_in, w_uq, w_ukv, w_out = weights[l]
        mod3 = mod[l].reshape(3, d)
        h, qkv, gates, tail = _k_in(x, norm_g[l][None], mod3, w_in)
        fft = tail[:, T_MISC + MISC_FF:T_MISC + MISC_FF + HEADS].T.reshape(HEADS * chunks, LANES)
        bf = jnp.repeat(b_f[l], chunks)[:, None]
        c2 = _k_cum(fft, bf, chunks).reshape(HEADS, s_len) * LOG2E
        side = (list(shards[l + 1]), [True] * 4) if shards is not None and l + 1 < n_l else None
        ck_lanes = jnp.pad(c2.T, ((0, 0), (0, LANES - HEADS)))
        of, lse_f, *gathered = _attention_fwd(qkv, qb, qkv, kb, qkv, vb, ck_lanes, None, False,
                                              "fox_fwd_gather" if side else "fox_fwd", side)
        if side:
            weights.append(_full_weights(*gathered))
        mq, mqr, mk, mv, kr2, qn, kvn = _k_prep(tail, cos, sin, q_norm_g[l][None], kv_norm_g[l][None], w_uq, w_ukv)
        om, lse_m = _attention_fwd(mq, 0, mk, 0, mv, 0, None, (mqr, kr2), True, "mla_fwd")
        x_new, y, u = _k_out(of, om, gates, x, mod3[2:3], w_out)
        saved.append((x, gates, tail, h, qkv, fft, bf, c2, lse_f, mq, mqr, mk, mv, kr2, lse_m, of, om, qn, kvn, y, u,
                      mod3))
        x = x_new

    loss_row, dx, dfinal = _k_loss(x, final_g[None], loss_target)

    grads = {k: [] for k in ("norm_g", "mod", "w_in", "b_f", "q_norm_g", "w_uq", "kv_norm_g", "w_ukv", "w_out")}
    received, pending = {}, None
    for l in range(n_l - 1, -1, -1):
        (x_l, gates, tail, h, qkv, fft, bf, c2, lse_f, mq, mqr, mk, mv, kr2, lse_m, of, om, qn, kvn, y, u,
         mod3) = saved[l]
        w_in, w_uq, w_ukv, w_out = weights[l]
        dyb, dof, dom, dfg, dmg, dlt_f, dlt_m, dgate = _kb_out(dx, y, mod3[2:3], w_out, of, om, gates)
        dw_out = _weight_grad(u, [dyb], "dw_out")

        side = (pending, [False] * 4) if pending is not None else None
        dfq, dfk, dfv, dcq, dck, *arrived = _attention_bwd(
            qkv, qb, qkv, kb, qkv, vb, dof, pack_rows(-lse_f.reshape(HEADS, s_len), dlt_f), c2[:, None, :], None,
            False, FOX_SCALE, 1.0 / LOG2E, "fox_bwd_exchange" if side else "fox_bwd", side)
        if side:
            received[l + 1] = arrived
        dcum = (dcq[:, :HEADS].T + dck.reshape(HEADS, s_len)).reshape(HEADS * chunks, LANES)
        dff_rows, dbf_rows = _k_cum_bwd(dcum, fft, bf, chunks)
        dfft = dff_rows.reshape(HEADS, s_len)
        grads["b_f"].append(jnp.sum(dbf_rows[:, 0].reshape(HEADS, chunks), axis=1))

        dmq, dkn, dmv, dqr, dkr_pairs = _attention_bwd(
            mq, 0, mk, 0, mv, 0, dom, pack_rows(-lse_m.reshape(HEADS, s_len), dlt_m), None, (mqr, kr2), True,
            MLA_SCALE, 1.0 / LOG2E, "mla_bwd")
        dkr = dkr_pairs[0] + dkr_pairs[1] + dkr_pairs[2] + dkr_pairs[3]
        dff = jnp.pad(dfft.T, ((0, 0), (MISC_FF, LANES - MISC_FF - HEADS)))
        dq_b, dz_tail, dgq, dgkv = _kb_prep(dmq, dqr, dkn, dmv, dkr, dff, tail, cos, sin, q_norm_g[l][None],
                                            kv_norm_g[l][None], w_uq, w_ukv)
        grads["q_norm_g"].append(dgq[0])
        grads["kv_norm_g"].append(dgkv[0])
        dw_uq = _weight_grad(qn, [dq_b], "dw_uq")
        dw_ukv = _weight_grad(kvn, [dkn, dmv], "dw_ukv")
        dz = [dfq, dfk, dfv, dfg, dmg, dz_tail]
        dw_in = _weight_grad(h, dz, "dw_in")
        dx, acc3 = _kb_in(dz, w_in, x_l, norm_g[l][None], mod3, dx)
        grads["norm_g"].append(acc3[2])
        grads["mod"].append(jnp.concatenate([acc3[0], acc3[1], dgate[0]]))
        if shards is not None:
            pending = _grad_slabs(dw_in, dw_uq, dw_ukv, dw_out)
        else:
            for name, g in (("w_in", dw_in), ("w_uq", dw_uq), ("w_ukv", dw_ukv), ("w_out", dw_out)):
                grads[name].append(g)
    grads = {k: jnp.stack(v[::-1]) for k, v in grads.items() if v}
    grads["final_g"] = dfinal[0]
    if shards is None:
        return loss_row[0, 0], dx, grads
    return loss_row[0, 0], dx, grads, received, pending


def _pack_small(parts, total):
    flat = jnp.concatenate([p.reshape(-1) for p in parts])
    return jnp.pad(flat, (0, total - flat.shape[0])).reshape(total // LANES, LANES)


def kernel(x, c, positions, norm_g, w_ada, b_ada, w_in, b_f, q_norm_g, w_uq, kv_norm_g, w_ukv, w_out, final_g, loss_target, m_norm_g, m_w_ada, m_b_ada, m_w_in, m_b_f, m_q_norm_g, m_w_uq, m_kv_norm_g, m_w_ukv, m_w_out, m_final_g, v_norm_g, v_w_ada, v_b_ada, v_w_in, v_b_f, v_q_norm_g, v_w_uq, v_kv_norm_g, v_w_ukv, v_w_out, v_final_g):
    n_l, d = norm_g.shape
    me = 4 * lax.axis_index("x") + 2 * lax.axis_index("y") + lax.axis_index("c")
    ada_c = w_ada.shape[2]

    cact = jnp.broadcast_to(jax.nn.silu(c), (N_DEV, d))
    shards = [[w[l].astype(MXU) for w in (w_in, w_uq, w_ukv, w_out)] for l in range(n_l)]
    *g_w0, g_cact = _exchange(shards[0] + [cact], [True] * 5, "gather_layer0")
    cact_all = g_cact[:, 0, :]

    b_cols = lax.dynamic_slice_in_dim(b_ada, me * ada_c, ada_c, axis=1)[:, None, :]
    modpart = _modpart(cact_all, w_ada, b_cols)
    mod_send = jnp.pad(modpart.transpose(1, 0, 2), ((0, 0), (0, 8 - n_l), (0, 0)))
    (mod_recv,) = _exchange([mod_send], [False], "scatter_mod")
    mod = mod_recv.transpose(1, 0, 2).reshape(8, N_DEV * ada_c)[:n_l]

    loss, dx, gr, received, pending = _local_step(x[0], mod, positions[0], loss_target[0], norm_g, b_f, q_norm_g,
                                                  kv_norm_g, final_g, [_full_weights(*g_w0)], shards)

    small_parts = [gr["norm_g"], gr["mod"], gr["b_f"], gr["q_norm_g"], gr["kv_norm_g"], gr["final_g"], cact[0]]
    sizes = [int(np.prod(p.shape)) for p in small_parts]
    total = -(-sum(sizes) // 1024) * 1024
    small = _pack_small(small_parts, total)
    *received[0], r_small = _exchange([p.astype(MXU) for p in pending] + [small],
                                      [False, False, False, False, True], "exchange_layer0")
    r_in, r_uq, r_ukv, r_out = ([received[l][i] for l in range(n_l)] for i in range(4))

    def upd(slabs, w, m, v, name):
        shp = w.shape
        w2, m2, v2 = (a.reshape(-1, slabs[0].shape[2]) for a in (w, m, v))
        return [o.reshape(shp) for o in _adamw(slabs, w2, m2, v2, name)]

    o_in = upd(r_in, w_in, m_w_in, v_w_in, "adamw_w_in")
    o_uq = upd(r_uq, w_uq, m_w_uq, v_w_uq, "adamw_w_uq")
    o_ukv = upd(r_ukv, w_ukv, m_w_ukv, v_w_ukv, "adamw_w_ukv")
    o_out = upd(r_out, w_out, m_w_out, v_w_out, "adamw_w_out")

    offs = np.cumsum([0] + sizes)
    flat_all = r_small.reshape(N_DEV, total)
    dmod_all = flat_all[:, offs[1]:offs[2]].reshape(N_DEV, n_l, 3 * d)
    dmod_cols = lax.dynamic_slice_in_dim(dmod_all, me * ada_c, ada_c, axis=2).transpose(1, 0, 2)
    cact_cols = flat_all[:, offs[6]:offs[7]][:, :, None]
    g_ada = _ada_grad(cact_cols, dmod_cols)
    o_ada = upd([g_ada.reshape(1, n_l * d, ada_c)], w_ada, m_w_ada, v_w_ada, "adamw_w_ada")

    zero_c = jnp.zeros((d,), F32)
    small_w = [_pack_small([norm_g, b_ada, b_f, q_norm_g, kv_norm_g, final_g, zero_c], total),
               _pack_small([m_norm_g, m_b_ada, m_b_f, m_q_norm_g, m_kv_norm_g, m_final_g, zero_c], total),
               _pack_small([v_norm_g, v_b_ada, v_b_f, v_q_norm_g, v_kv_norm_g, v_final_g, zero_c], total)]
    o_small = [o.reshape(-1) for o in _adamw([r_small], *small_w, "adamw_small")]
    shapes = [norm_g.shape, b_ada.shape, b_f.shape, q_norm_g.shape, kv_norm_g.shape, final_g.shape]

    def small_out(kind, idx):
        return o_small[kind][offs[idx]:offs[idx + 1]].reshape(shapes[idx])

    loss_all = lax.psum(loss, ("x", "y", "c"))
    outs = [loss_all, dx[None]]
    for kind in range(4):
        outs += [small_out(kind, 0), o_ada[kind], small_out(kind, 1), o_in[kind], small_out(kind, 2),
                 small_out(kind, 3), o_uq[kind], small_out(kind, 4), o_ukv[kind], o_out[kind], small_out(kind, 5)]
    return tuple(outs)
```

```python
import jax
import jax.numpy as jnp
import numpy as np
from jax import lax
from jax.experimental import pallas as pl
from jax.experimental.pallas import tpu as pltpu

F32 = jnp.float32
MXU = jnp.bfloat16

N_DEV = 8
HEADS = 8
PAIRS = HEADS // 2
HEAD_DIM = 64
NOPE = 64
ROPE = 32
HALF_ROPE = ROPE // 2
Q_LORA = 256
KV_LORA = 128
CHUNK = 64
GROUP_W = HEADS * HEAD_DIM
ROPE_W = HEADS * ROPE
EPS = 1e-6
ROPE_THETA = 10000.0
N_IN = 2984

Z_FQ, Z_FK, Z_FV, Z_FG, Z_MG, Z_QL, Z_KV, Z_MISC, Z_W = 0, 512, 1024, 1536, 2048, 2560, 2816, 2944, 3072
MISC_FF = ROPE
TAIL_W = Z_W - Z_QL
T_KV, T_MISC = Q_LORA, Q_LORA + KV_LORA

ADAM_LR = 0.001
ADAM_B1 = 0.9
ADAM_B2 = 0.999
ADAM_EPS = 1e-08
ADAM_WD = 0.01
ADAM_STEP = 10

VMEM_LIMIT_V7X = 56 * 1024 * 1024
LANES = 128
ATTN_TILE = 1024
V_ROWS = HEAD_DIM + 16
LOG2E = 1.4426950408889634
FOX_SCALE = HEAD_DIM ** -0.5
MLA_SCALE = (NOPE + ROPE) ** -0.5

_NT = (((1,), (1,)), ((), ()))
_TN = (((0,), (0,)), ((), ()))


def _params(*sem, side_effects=False):
    return pltpu.CompilerParams(dimension_semantics=sem, vmem_limit_bytes=VMEM_LIMIT_V7X,
                                has_side_effects=side_effects)


def _sds(shape, dtype=F32):
    return jax.ShapeDtypeStruct(shape, dtype)


def _full(shape):
    nd = len(shape)
    return pl.BlockSpec(shape, lambda *_: (0,) * nd)


def _rows(tm, width, col=0):
    return pl.BlockSpec((tm, width), lambda i: (i, col))


def _exchange(arrs, gather, name):
    n = len(arrs)

    def kern(*refs):
        copies = _exchange_copies(refs[:n], refs[n:2 * n], gather, *refs[2 * n:])
        _exchange_start(copies)
        _exchange_wait(copies)

    return pl.pallas_call(
        kern, name=name, out_shape=_exchange_out_shapes(arrs, gather),
        in_specs=[pl.BlockSpec(memory_space=pl.ANY)] * n,
        out_specs=[pl.BlockSpec(memory_space=pl.ANY)] * n,
        scratch_shapes=_exchange_sems(n),
        compiler_params=pltpu.CompilerParams(has_side_effects=True),
    )(*arrs)


def _exchange_out_shapes(arrs, gather):
    return [_sds((N_DEV,) + tuple(a.shape) if g else tuple(a.shape), a.dtype) for a, g in zip(arrs, gather)]


def _exchange_sems(n):
    return [pltpu.SemaphoreType.DMA((n, N_DEV)), pltpu.SemaphoreType.DMA((n, N_DEV)), pltpu.SemaphoreType.DMA((n,))]


def _exchange_copies(ins, outs, gather, send_sems, recv_sems, loc_sems, recv=True):
    n = len(ins)
    x, y, c = lax.axis_index("x"), lax.axis_index("y"), lax.axis_index("c")
    me = 4 * x + 2 * y + c

    def src(i, j):
        return ins[i] if gather[i] else ins[i].at[j]

    local = [pltpu.make_async_copy(src(i, me), outs[i].at[me], loc_sems.at[i]) for i in range(n)]
    sends, recvs = [], []
    for k in range(1, N_DEV):
        px = 1 - x if k & 4 else x
        py = 1 - y if k & 2 else y
        pc = 1 - c if k & 1 else c
        p = 4 * px + 2 * py + pc
        for i in range(n):
            sends.append(pltpu.make_async_remote_copy(
                src_ref=src(i, p), dst_ref=outs[i].at[me], send_sem=send_sems.at[i, k],
                recv_sem=recv_sems.at[i, k], device_id=(px, py, pc), device_id_type=pl.DeviceIdType.MESH))
            if recv:
                recvs.append(pltpu.make_async_remote_copy(
                    src_ref=src(i, p), dst_ref=outs[i].at[p], send_sem=send_sems.at[i, k],
                    recv_sem=recv_sems.at[i, k], device_id=(px, py, pc), device_id_type=pl.DeviceIdType.MESH))
    return local, sends, recvs


def _exchange_start(copies):
    local, sends, _ = copies
    for cp in local + sends:
        cp.start()


def _exchange_wait(copies):
    local, sends, recvs = copies
    for cp in recvs:
        cp.wait_recv()
    for cp in sends:
        cp.wait_send()
    for cp in local:
        cp.wait()


def _modpart(cact8, w_ada, b_cols):
    n_l, d, cw = w_ada.shape

    def kern(c_ref, w_ref, b_ref, o_ref):
        o_ref[0] = jnp.dot(c_ref[...].astype(MXU), w_ref[0].astype(MXU), preferred_element_type=F32) + b_ref[0]

    return pl.pallas_call(
        kern, name="modpart", grid=(n_l,), out_shape=_sds((n_l, N_DEV, cw)),
        in_specs=[_full((N_DEV, d)), pl.BlockSpec((1, d, cw), lambda l: (l, 0, 0)),
                  pl.BlockSpec((1, 1, cw), lambda l: (l, 0, 0))],
        out_specs=pl.BlockSpec((1, N_DEV, cw), lambda l: (l, 0, 0)),
        compiler_params=_params("arbitrary"),
    )(cact8, w_ada, b_cols)


def _ada_grad(cact_cols, dmod_cols):
    n_l, _, cw = dmod_cols.shape
    d = cact_cols.shape[1]

    def kern(c_ref, dm_ref, o_ref):
        acc = c_ref[0] * dm_ref[0, 0:1, :]
        for s in range(1, N_DEV):
            acc = acc + c_ref[s] * dm_ref[0, s:s + 1, :]
        o_ref[0] = acc

    return pl.pallas_call(
        kern, name="ada_grad", grid=(n_l,), out_shape=_sds((n_l, d, cw)),
        in_specs=[_full((N_DEV, d, 1)), pl.BlockSpec((1, N_DEV, cw), lambda l: (l, 0, 0))],
        out_specs=pl.BlockSpec((1, d, cw), lambda l: (l, 0, 0)),
        compiler_params=_params("arbitrary"),
    )(cact_cols, dmod_cols)


def _k_in(x, g, mod3, w, tm=256):
    s_len, d = x.shape
    qkv_w = 3 * GROUP_W

    def kern(x_ref, g_ref, mod_ref, w_ref, h_ref, qkv_ref, gates_ref, tail_ref):
        xv = x_ref[...]
        r = lax.rsqrt(jnp.mean(xv * xv, axis=-1, keepdims=True) + EPS)
        xn = xv * r * g_ref[...]
        h = (xn * (1.0 + mod_ref[1:2, :]) + mod_ref[0:1, :]).astype(MXU)
        h_ref[...] = h
        z = jnp.dot(h, w_ref[...], preferred_element_type=F32)
        qkv_ref[:, :GROUP_W] = (z[:, Z_FQ:Z_FQ + GROUP_W] * (FOX_SCALE * LOG2E)).astype(MXU)
        qkv_ref[:, GROUP_W:] = z[:, Z_FK:Z_FK + 2 * GROUP_W].astype(MXU)
        gates_ref[...] = z[:, Z_FG:Z_QL]
        tail_ref[...] = z[:, Z_QL:]

    return pl.pallas_call(
        kern, name="k_in", grid=(s_len // tm,),
        out_shape=[_sds((s_len, d), MXU), _sds((s_len, qkv_w), MXU), _sds((s_len, Z_QL - Z_FG)),
                   _sds((s_len, TAIL_W))],
        in_specs=[_rows(tm, d), _full((1, d)), _full((3, d)), _full((d, Z_W))],
        out_specs=[_rows(tm, d), _rows(tm, qkv_w), _rows(tm, Z_QL - Z_FG), _rows(tm, TAIL_W)],
        compiler_params=_params("arbitrary"),
    )(x, g, mod3, w)


def _scan_matrices(rows, chunks, reverse):
    r_i = lax.broadcasted_iota(jnp.int32, (LANES, LANES), 0)
    c_i = lax.broadcasted_iota(jnp.int32, (LANES, LANES), 1)
    a_i = lax.broadcasted_iota(jnp.int32, (rows, rows), 0)
    b_i = lax.broadcasted_iota(jnp.int32, (rows, rows), 1)
    same_head = (a_i // chunks) == (b_i // chunks)
    if reverse:
        return (r_i >= c_i).astype(F32), (same_head & (b_i > a_i)).astype(F32)
    return (r_i <= c_i).astype(F32), (same_head & (b_i < a_i)).astype(F32)


def _scan_rows(x, inner, outer):
    tot = jnp.broadcast_to(jnp.sum(x, axis=1, keepdims=True), x.shape)
    return (jnp.dot(x, inner, precision=lax.Precision.HIGHEST, preferred_element_type=F32)
            + jnp.dot(outer, tot, precision=lax.Precision.HIGHEST, preferred_element_type=F32))


def _k_cum(ff_rows, b_rows, chunks):
    rows = ff_rows.shape[0]

    def kern(ff_ref, b_ref, cum_ref):
        xc = ff_ref[...] + b_ref[...]
        lf = jnp.minimum(xc, 0.0) - jnp.log(1.0 + jnp.exp(-jnp.abs(xc)))
        cum_ref[...] = _scan_rows(lf, *_scan_matrices(rows, chunks, False))

    return pl.pallas_call(
        kern, name="k_cum", out_shape=_sds((rows, LANES)),
        in_specs=[pl.BlockSpec(memory_space=pltpu.VMEM)] * 2,
        out_specs=pl.BlockSpec(memory_space=pltpu.VMEM),
        compiler_params=_params(),
    )(ff_rows, b_rows)


def _k_cum_bwd(dc_rows, ff_rows, b_rows, chunks):
    rows = ff_rows.shape[0]

    def kern(dc_ref, ff_ref, b_ref, dff_ref, db_ref):
        dlf = _scan_rows(dc_ref[...], *_scan_matrices(rows, chunks, True))
        dff = dlf * jax.nn.sigmoid(-(ff_ref[...] + b_ref[...]))
        dff_ref[...] = dff
        db_ref[...] = jnp.broadcast_to(jnp.sum(dff, axis=1, keepdims=True), dff.shape)

    return pl.pallas_call(
        kern, name="k_cum_bwd", out_shape=[_sds((rows, LANES)), _sds((rows, LANES))],
        in_specs=[pl.BlockSpec(memory_space=pltpu.VMEM)] * 3,
        out_specs=[pl.BlockSpec(memory_space=pltpu.VMEM)] * 2,
        compiler_params=_params(),
    )(dc_rows, ff_rows, b_rows)


def _swap16(t):
    lane = lax.broadcasted_iota(jnp.int32, t.shape, 1)
    return jnp.where(lane % ROPE < HALF_ROPE, pltpu.roll(t, LANES - HALF_ROPE, 1), pltpu.roll(t, HALF_ROPE, 1))


def _rope(t, cos, sin):
    return t * cos + _swap16(t) * sin


def _rope_bwd(dt, cos, sin):
    return dt * cos - _swap16(dt) * sin


def _k_prep(tail, cos, sin, gq, gkv, wuq, wukv, tm=512):
    s_len = tail.shape[0]
    qc = MLA_SCALE * LOG2E

    def kern(tail_ref, cos_ref, sin_ref, gq_ref, gkv_ref, wuq_ref, wukv_ref,
             qn_out, qr_out, kn_out, v_out, kr_out, qn_ref, kvn_ref):
        cs, sn = cos_ref[...], sin_ref[...]
        ql = tail_ref[:, :T_KV]
        rq = lax.rsqrt(jnp.mean(ql * ql, axis=-1, keepdims=True) + EPS)
        qn = (ql * rq * gq_ref[...]).astype(MXU)
        qn_ref[...] = qn
        q = jnp.dot(qn, wuq_ref[...], preferred_element_type=F32)
        qn_out[...] = (q[:, :GROUP_W] * qc).astype(MXU)
        for blk in range(PAIRS):
            lo = GROUP_W + blk * LANES
            qr_out[:, blk * LANES:(blk + 1) * LANES] = (_rope(q[:, lo:lo + LANES], cs, sn) * qc).astype(MXU)
        kvl = tail_ref[:, T_KV:T_MISC]
        rk = lax.rsqrt(jnp.mean(kvl * kvl, axis=-1, keepdims=True) + EPS)
        kvn = (kvl * rk * gkv_ref[...]).astype(MXU)
        kvn_ref[...] = kvn
        kv = jnp.dot(kvn, wukv_ref[...], preferred_element_type=F32)
        kn_out[...] = kv[:, :GROUP_W].astype(MXU)
        v_out[...] = kv[:, GROUP_W:].astype(MXU)
        misc = tail_ref[:, T_MISC:]
        lane = lax.broadcasted_iota(jnp.int32, misc.shape, 1)
        kr = jnp.where(lane < ROPE, _rope(misc, cs, sn), 0.0)
        kr_out[...] = (kr + pltpu.roll(kr, HEAD_DIM, 1)).astype(MXU)

    return pl.pallas_call(
        kern, name="k_prep", grid=(s_len // tm,),
        out_shape=[_sds((s_len, GROUP_W), MXU), _sds((s_len, GROUP_W), MXU), _sds((s_len, GROUP_W), MXU),
                   _sds((s_len, GROUP_W), MXU), _sds((s_len, LANES), MXU), _sds((s_len, Q_LORA), MXU),
                   _sds((s_len, KV_LORA), MXU)],
        in_specs=[_rows(tm, TAIL_W), _rows(tm, LANES), _rows(tm, LANES),
                  _full((1, Q_LORA)), _full((1, KV_LORA)), _full((Q_LORA, 2 * GROUP_W)),
                  _full((KV_LORA, 2 * GROUP_W))],
        out_specs=[_rows(tm, GROUP_W), _rows(tm, GROUP_W), _rows(tm, GROUP_W), _rows(tm, GROUP_W), _rows(tm, LANES),
                   _rows(tm, Q_LORA), _rows(tm, KV_LORA)],
        compiler_params=_params("arbitrary"),
    )(tail, cos, sin, gq, gkv, wuq, wukv)


def _block_mask(kn, qn, q_off, chunk_mask, transposed):
    shape = (kn, qn) if transposed else (qn, kn)
    row = lax.broadcasted_iota(jnp.int32, shape, 0)
    col = lax.broadcasted_iota(jnp.int32, shape, 1)
    qi, ki = (col + q_off, row) if transposed else (row + q_off, col)
    if chunk_mask:
        return (ki // CHUNK) <= (qi // CHUNK)
    return ki <= qi


def _head_operand(x, hh, other=None):
    lane = lax.broadcasted_iota(jnp.int32, x.shape, 1)
    own = (lane >= hh * HEAD_DIM) & (lane < (hh + 1) * HEAD_DIM)
    return jnp.where(own, x, jnp.zeros_like(x) if other is None else other)


def _attention_fwd(q, q_blk, k, k_blk, v, v_blk, bias, rope, chunk_mask, name, side=None):
    s_len = q.shape[0]
    t = min(ATTN_TILE, s_len // 2)
    nq = s_len // t
    n_side = len(side[0]) if side else 0

    def kern(*refs):
        q_ref, k_ref, v_ref = refs[:3]
        pos = 3
        if bias is not None:
            ck_ref = refs[pos]
            pos += 1
        if rope is not None:
            qr_ref, kr_ref = refs[pos:pos + 2]
            pos += 2
        side_in = refs[pos:pos + n_side]
        pos += n_side
        o_ref, lse_ref = refs[pos:pos + 2]
        side_out = refs[pos + 2:pos + 2 + n_side]
        vt_scr, m_scr, acc_scr, ck_scr = refs[pos + 2 + n_side:pos + 6 + n_side]
        sems = refs[pos + 6 + n_side:]
        pj = pl.program_id(0)
        if n_side:
            @pl.when(pj == 0)
            def _():
                _exchange_start(_exchange_copies(side_in, side_out, side[1], *sems, recv=False))
        vt_scr[:, HEAD_DIM:, :] = jnp.ones((2, V_ROWS - HEAD_DIM, s_len), vt_scr.dtype)
        for i in range(nq):
            vtt = v_ref[i * t:(i + 1) * t, :].T
            for hh in range(2):
                vt_scr[hh, :HEAD_DIM, i * t:(i + 1) * t] = vtt[hh * HEAD_DIM:(hh + 1) * HEAD_DIM, :]
                if bias is not None:
                    ckt = ck_ref[i * t:(i + 1) * t, :]
                    lane = lax.broadcasted_iota(jnp.int32, ckt.shape, 1)
                    ck_scr[hh, i * t:(i + 1) * t, :] = jnp.sum(jnp.where(lane == 2 * pj + hh, ckt, 0.0), axis=1,
                                                               keepdims=True)

        def qbody(qi, _):
            qs = pl.multiple_of(qi * t, t)
            qt = q_ref[pl.ds(qs, t), :]
            qrt = qr_ref[pl.ds(qs, t), :] if rope is not None else None
            qh = [_head_operand(qt, hh, qrt) for hh in range(2)]
            m_scr[...] = jnp.full(m_scr.shape, -jnp.inf, F32)
            acc_scr[...] = jnp.zeros(acc_scr.shape, F32)

            def block(ks, kn, q0, qn, masked):
                kt = k_ref[pl.ds(ks, kn), :]
                kh = [_head_operand(kt, hh, kr_ref[pl.ds(ks, kn), :]) for hh in range(2)] if rope is not None else [kt, kt]
                qc = slice(q0, q0 + qn)
                sts = [lax.dot_general(kh[hh], qh[hh][qc], _NT, preferred_element_type=F32) for hh in range(2)]
                if bias is not None:
                    sts = [sts[hh] - ck_scr[hh, pl.ds(ks, kn), :] for hh in range(2)]
                if masked:
                    sts = [jnp.where(_block_mask(kn, qn, q0, chunk_mask, True), st, -jnp.inf) for st in sts]
                m_old = [m_scr[hh, :, qc] for hh in range(2)]
                m_new = [jnp.maximum(m_old[hh], jnp.max(sts[hh], axis=0, keepdims=True)) for hh in range(2)]
                pts = [jnp.exp2(sts[hh] - m_new[hh]).astype(MXU) for hh in range(2)]
                for hh in range(2):
                    alpha = jnp.exp2(m_old[hh] - m_new[hh])
                    acc_scr[hh, :, qc] = alpha * acc_scr[hh, :, qc] + jnp.dot(vt_scr[hh, :, pl.ds(ks, kn)], pts[hh],
                                                                            preferred_element_type=F32)
                    m_scr[hh, :, qc] = m_new[hh]

            def loop_body(ki, carry):
                block(pl.multiple_of(ki * t, t), t, 0, t, False)
                return carry

            lax.fori_loop(0, qi, loop_body, 0)
            block(qs, t, 0, t, True)
            outs = []
            for hh in range(2):
                acc = acc_scr[hh]
                l = acc[HEAD_DIM:HEAD_DIM + 1, :]
                outs.append(acc[:HEAD_DIM, :] / l)
                lse_ref[hh, :, pl.ds(qs, t)] = m_scr[hh] + jnp.log2(l)
            o_ref[pl.ds(qs, t), :] = jnp.concatenate(outs, axis=0).T
            return 0

        lax.fori_loop(0, nq, qbody, 0)
        if n_side:
            @pl.when(pj == PAIRS - 1)
            def _():
                _exchange_wait(_exchange_copies(side_in, side_out, side[1], *sems))

    def tok(blk):
        return pl.BlockSpec((s_len, LANES), lambda j: (0, blk + j))

    rowb = pl.BlockSpec((2, 1, s_len), lambda j: (j, 0, 0))
    hbm = pl.BlockSpec(memory_space=pl.ANY)
    ins = [q, k, v]
    in_specs = [tok(q_blk), tok(k_blk), tok(v_blk)]
    if bias is not None:
        ins.append(bias)
        in_specs.append(_full((s_len, LANES)))
    if rope is not None:
        ins += list(rope)
        in_specs += [tok(0), _full((s_len, LANES))]
    out_shape = [_sds((s_len, PAIRS * LANES)), _sds((HEADS, 1, s_len))]
    scratch = [pltpu.VMEM((2, V_ROWS, s_len), v.dtype), pltpu.VMEM((2, 1, t), F32), pltpu.VMEM((2, V_ROWS, t), F32),
               pltpu.VMEM((2, s_len if bias is not None else 8, 1), F32)]
    if n_side:
        ins += list(side[0])
        out_shape += _exchange_out_shapes(*side)
        scratch += _exchange_sems(n_side)
    return pl.pallas_call(
        kern, name=name, grid=(PAIRS,), out_shape=out_shape,
        in_specs=in_specs + [hbm] * n_side, out_specs=[tok(0), rowb] + [hbm] * n_side,
        scratch_shapes=scratch,
        compiler_params=_params("arbitrary", side_effects=bool(n_side)),
    )(*ins)


def _attention_bwd(q, q_blk, k, k_blk, v, v_blk, do, pack, ck_row, rope, chunk_mask, q_scale, k_scale, name,
                   side=None):
    s_len = q.shape[0]
    t = min(ATTN_TILE, s_len // 2)
    nq = s_len // t
    has_bias = ck_row is not None
    nv = 2 if rope is not None else 1
    n_side = len(side[0]) if side else 0

    def kern(*refs):
        q_ref, k_ref, v_ref, do_ref, pack_ref = refs[:5]
        pos = 5
        if has_bias:
            ck_ref = refs[pos]
            pos += 1
        if rope is not None:
            qr_ref, kr_ref = refs[pos:pos + 2]
            pos += 2
        side_in = refs[pos:pos + n_side]
        pos += n_side
        dq_ref, dk_ref, dv_ref = refs[pos:pos + 3]
        pos += 3
        if has_bias:
            dcq_ref, dck_ref = refs[pos:pos + 2]
            pos += 2
        if rope is not None:
            dqr_ref, dkr_ref = refs[pos:pos + 2]
            pos += 2
        side_out = refs[pos:pos + n_side]
        pos += n_side
        qt_scr, dot_scr, dkt_scr, dvt_scr, dq_scr, dcq_scr = refs[pos:pos + 6]
        sems = refs[pos + 6:]
        pj = pl.program_id(0)
        if n_side:
            @pl.when(pj == 0)
            def _():
                _exchange_start(_exchange_copies(side_in, side_out, side[1], *sems, recv=False))

        for i in range(nq):
            sl = slice(i * t, (i + 1) * t)
            dot_scr[:, sl] = do_ref[sl, :].T
            if rope is not None:
                for hh in range(2):
                    qt_scr[hh, :, sl] = _head_operand(q_ref[sl, :], hh, qr_ref[sl, :]).T
            else:
                qt_scr[0, :, sl] = q_ref[sl, :].T
        dkt_scr[...] = jnp.zeros(dkt_scr.shape, F32)
        dvt_scr[...] = jnp.zeros(dvt_scr.shape, F32)
        if has_bias:
            dck_ref[...] = jnp.zeros(dck_ref.shape, F32)

            @pl.when(pj == 0)
            def _():
                dcq_ref[...] = jnp.zeros(dcq_ref.shape, F32)

        def qbody(qi, _):
            qs = pl.multiple_of(qi * t, t)
            qt = q_ref[pl.ds(qs, t), :]
            qrt = qr_ref[pl.ds(qs, t), :] if rope is not None else None
            dot = do_ref[pl.ds(qs, t), :]
            pk = pack_ref[pl.ds(qs, t), :]
            lane = lax.broadcasted_iota(jnp.int32, pk.shape, 1)
            qh = [_head_operand(qt, hh, qrt) for hh in range(2)]
            doh = [_head_operand(dot, hh) for hh in range(2)]
            a_col = [jnp.sum(jnp.where(lane == 2 * pj + hh, pk, 0.0), axis=1, keepdims=True) for hh in range(2)]
            d_col = [jnp.sum(jnp.where(lane == HEADS + 2 * pj + hh, pk, 0.0), axis=1, keepdims=True)
                     for hh in range(2)]
            dq_scr[...] = jnp.zeros(dq_scr.shape, F32)
            if has_bias:
                dcq_scr[...] = jnp.zeros(dcq_scr.shape, F32)

            def block(ks, kn, q0, qn, masked):
                kt = k_ref[pl.ds(ks, kn), :]
                krt = kr_ref[pl.ds(ks, kn), :] if rope is not None else None
                kh = [_head_operand(kt, hh, krt) for hh in range(2)]
                vt = v_ref[pl.ds(ks, kn), :]
                qr_ = slice(q0, q0 + qn)
                qcols = pl.ds(pl.multiple_of(qs + q0, t // 2), qn)
                ss = [lax.dot_general(qh[hh][qr_], kh[hh] if rope is not None else kt, _NT,
                                      preferred_element_type=F32) + a_col[hh][qr_] for hh in range(2)]
                if has_bias:
                    ss = [ss[hh] - ck_ref[hh, :, pl.ds(ks, kn)] for hh in range(2)]
                dpds = [lax.dot_general(doh[hh][qr_], vt, _NT, preferred_element_type=F32) for hh in range(2)]
                ps = [jnp.exp2(s) for s in ss]
                if masked:
                    ps = [jnp.where(_block_mask(kn, qn, q0, chunk_mask, False), p, 0.0) for p in ps]
                dss = [ps[hh] * (dpds[hh] - d_col[hh][qr_]) for hh in range(2)]
                for hh in range(2):
                    rows = slice(hh * HEAD_DIM, (hh + 1) * HEAD_DIM)
                    dsb = dss[hh].astype(MXU)
                    dvt_scr[rows, pl.ds(ks, kn)] += jnp.dot(dot_scr[rows, qcols], ps[hh].astype(MXU),
                                                            preferred_element_type=F32)
                    if rope is not None:
                        dkt_scr[hh, :, pl.ds(ks, kn)] += jnp.dot(qt_scr[hh, :, qcols], dsb,
                                                                 preferred_element_type=F32)
                    else:
                        dkt_scr[0, rows, pl.ds(ks, kn)] += jnp.dot(qt_scr[0, rows, qcols], dsb,
                                                                   preferred_element_type=F32)
                    dq_scr[hh if rope is not None else 0, qr_, :] += jnp.dot(dsb, kh[hh], preferred_element_type=F32)
                    if has_bias:
                        dcq_scr[hh, qr_, :] += jnp.sum(dss[hh], axis=1, keepdims=True)
                        dck_ref[hh, :, pl.ds(ks, kn)] += -jnp.sum(dss[hh], axis=0, keepdims=True)

            def loop_body(ki, carry):
                block(pl.multiple_of(ki * t, t), t, 0, t, False)
                return carry

            lax.fori_loop(0, qi, loop_body, 0)
            block(qs, t // 2, 0, t // 2, True)
            block(qs, t, t // 2, t // 2, True)
            if rope is not None:
                first = lane < HEAD_DIM
                dq_ref[pl.ds(qs, t), :] = (jnp.where(first, dq_scr[0], dq_scr[1]) * q_scale).astype(dq_ref.dtype)
                dqr_ref[pl.ds(qs, t), :] = jnp.where(first, dq_scr[1], dq_scr[0]) * q_scale
            else:
                dq_ref[pl.ds(qs, t), :] = (dq_scr[0] * q_scale).astype(dq_ref.dtype)
            if has_bias:
                old = dcq_ref[pl.ds(qs, t), :]
                dcq_ref[pl.ds(qs, t), :] = jnp.where(lane == 2 * pj, dcq_scr[0],
                                                     jnp.where(lane == 2 * pj + 1, dcq_scr[1], old))
            return 0

        lax.fori_loop(0, nq, qbody, 0)
        for i in range(nq):
            sl = slice(i * t, (i + 1) * t)
            dv_ref[sl, :] = dvt_scr[:, sl].T.astype(dv_ref.dtype)
            if rope is not None:
                d0, d1 = dkt_scr[0, :, sl], dkt_scr[1, :, sl]
                first = lax.broadcasted_iota(jnp.int32, d0.shape, 0) < HEAD_DIM
                dk_ref[sl, :] = (jnp.where(first, d0, d1).T * k_scale).astype(dk_ref.dtype)
                dkr_ref[0, sl, :] = jnp.where(first, d1, d0).T * k_scale
            else:
                dk_ref[sl, :] = (dkt_scr[0, :, sl].T * k_scale).astype(dk_ref.dtype)
        if n_side:
            @pl.when(pj == PAIRS - 1)
            def _():
                _exchange_wait(_exchange_copies(side_in, side_out, side[1], *sems))

    def tok(blk):
        return pl.BlockSpec((s_len, LANES), lambda j: (0, blk + j))

    shared = _full((s_len, LANES))
    rowb = pl.BlockSpec((2, 1, s_len), lambda j: (j, 0, 0))
    slab = pl.BlockSpec((1, s_len, LANES), lambda j: (j, 0, 0))
    hbm = pl.BlockSpec(memory_space=pl.ANY)
    ins = [q, k, v, do, pack]
    in_specs = [tok(q_blk), tok(k_blk), tok(v_blk), tok(0), shared]
    out_shape = [_sds((s_len, PAIRS * LANES), MXU)] * 3
    out_specs = [tok(0)] * 3
    if has_bias:
        ins.append(ck_row)
        in_specs.append(rowb)
        out_shape += [_sds((s_len, LANES)), _sds((HEADS, 1, s_len))]
        out_specs += [shared, rowb]
    if rope is not None:
        ins += list(rope)
        in_specs += [tok(0), shared]
        out_shape += [_sds((s_len, PAIRS * LANES)), _sds((PAIRS, s_len, LANES))]
        out_specs += [tok(0), slab]
    scratch = [pltpu.VMEM((nv, LANES, s_len), q.dtype), pltpu.VMEM((LANES, s_len), do.dtype),
               pltpu.VMEM((nv, LANES, s_len), F32), pltpu.VMEM((LANES, s_len), F32),
               pltpu.VMEM((nv, t, LANES), F32), pltpu.VMEM((2, t, 1), F32)]
    if n_side:
        ins += list(side[0])
        out_shape += _exchange_out_shapes(*side)
        scratch += _exchange_sems(n_side)
    return pl.pallas_call(
        kern, name=name, grid=(PAIRS,), out_shape=out_shape,
        in_specs=in_specs + [hbm] * n_side, out_specs=out_specs + [hbm] * n_side, scratch_shapes=scratch,
        compiler_params=_params("arbitrary", side_effects=bool(n_side)),
    )(*ins)


def _silu(a):
    return a * jax.nn.sigmoid(a)


def _k_out(of, om, gates, x, gate, wout, tm=256):
    s_len, d = x.shape

    def kern(of_ref, om_ref, gates_ref, x_ref, gate_ref, w_ref, xo_ref, y_ref, u_ref):
        u_ref[:, :GROUP_W] = (of_ref[...] * _silu(gates_ref[:, :GROUP_W])).astype(MXU)
        u_ref[:, GROUP_W:] = (om_ref[...] * _silu(gates_ref[:, GROUP_W:])).astype(MXU)
        y = jnp.dot(u_ref[...], w_ref[...], preferred_element_type=F32)
        y_ref[...] = y
        xo_ref[...] = x_ref[...] + gate_ref[...] * y

    return pl.pallas_call(
        kern, name="k_out", grid=(s_len // tm,),
        out_shape=[_sds((s_len, d)), _sds((s_len, d)), _sds((s_len, 2 * GROUP_W), MXU)],
        in_specs=[_rows(tm, GROUP_W), _rows(tm, GROUP_W), _rows(tm, 2 * GROUP_W), _rows(tm, d), _full((1, d)),
                  _full((2 * GROUP_W, d))],
        out_specs=[_rows(tm, d), _rows(tm, d), _rows(tm, 2 * GROUP_W)],
        compiler_params=_params("arbitrary"),
    )(of, om, gates, x, gate, wout)


def _k_loss(x, gf, tgt, tm=256):
    s_len, d = x.shape

    def kern(x_ref, g_ref, t_ref, loss_ref, dx_ref, dg_ref):
        i = pl.program_id(0)
        xv = x_ref[...]
        r = lax.rsqrt(jnp.mean(xv * xv, axis=-1, keepdims=True) + EPS)
        xh = xv * r
        diff = xh * g_ref[...] - t_ref[...]
        part = 0.5 * jnp.sum(jnp.mean(diff * diff, axis=-1, keepdims=True))
        dout = diff * (1.0 / d)
        dxh = dout * g_ref[...]
        dx_ref[...] = r * (dxh - xh * jnp.mean(dxh * xh, axis=-1, keepdims=True))

        @pl.when(i == 0)
        def _():
            loss_ref[...] = jnp.zeros_like(loss_ref)
            dg_ref[...] = jnp.zeros_like(dg_ref)

        loss_ref[...] += jnp.full(loss_ref.shape, part, F32)
        dg_ref[...] += jnp.sum(dout * xh, axis=0, keepdims=True)

    return pl.pallas_call(
        kern, name="k_loss", grid=(s_len // tm,),
        out_shape=[_sds((1, LANES)), _sds((s_len, d)), _sds((1, d))],
        in_specs=[_rows(tm, d), _full((1, d)), _rows(tm, d)],
        out_specs=[_full((1, LANES)), _rows(tm, d), _full((1, d))],
        compiler_params=_params("arbitrary"),
    )(x, gf, tgt)


def _kb_out(dxo, y, gate, wout, of, om, gates, tm=256):
    s_len, d = dxo.shape

    def kern(dxo_ref, y_ref, gate_ref, wt_ref, of_ref, om_ref, gates_ref,
             dy_ref, dof_ref, dom_ref, dfg_ref, dmg_ref, dlf_ref, dlm_ref, dgate_ref):
        i = pl.program_id(0)
        dxv = dxo_ref[...]

        @pl.when(i == 0)
        def _():
            dgate_ref[...] = jnp.zeros_like(dgate_ref)

        dgate_ref[...] += jnp.sum(dxv * y_ref[...], axis=0, keepdims=True)
        dyb = (dxv * gate_ref[...]).astype(MXU)
        dy_ref[...] = dyb
        du = lax.dot_general(dyb, wt_ref[...], _NT, preferred_element_type=F32)
        head_of = (lax.broadcasted_iota(jnp.int32, (GROUP_W, LANES), 0) // HEAD_DIM
                   == lax.broadcasted_iota(jnp.int32, (GROUP_W, LANES), 1)).astype(F32)
        for du_g, o_ref, a, do_ref, dg_ref, dl_ref in (
                (du[:, :GROUP_W], of_ref, gates_ref[:, :GROUP_W], dof_ref, dfg_ref, dlf_ref),
                (du[:, GROUP_W:], om_ref, gates_ref[:, GROUP_W:], dom_ref, dmg_ref, dlm_ref)):
            sg = jax.nn.sigmoid(a)
            ov = o_ref[...]
            dov = du_g * (a * sg)
            do_ref[...] = dov.astype(MXU)
            dg_ref[...] = (du_g * ov * (sg * (1.0 + a * (1.0 - sg)))).astype(MXU)
            dl_ref[...] = jnp.dot(dov * ov, head_of, precision=lax.Precision.HIGH, preferred_element_type=F32)

    return pl.pallas_call(
        kern, name="kb_out", grid=(s_len // tm,),
        out_shape=[_sds((s_len, d), MXU), _sds((s_len, GROUP_W), MXU), _sds((s_len, GROUP_W), MXU),
                   _sds((s_len, GROUP_W), MXU), _sds((s_len, GROUP_W), MXU), _sds((s_len, LANES)),
                   _sds((s_len, LANES)), _sds((1, d))],
        in_specs=[_rows(tm, d), _rows(tm, d), _full((1, d)), _full((2 * GROUP_W, d)), _rows(tm, GROUP_W),
                  _rows(tm, GROUP_W), _rows(tm, 2 * GROUP_W)],
        out_specs=[_rows(tm, d), _rows(tm, GROUP_W), _rows(tm, GROUP_W), _rows(tm, GROUP_W),
                   _rows(tm, GROUP_W), _rows(tm, LANES), _rows(tm, LANES), _full((1, d))],
        compiler_params=_params("arbitrary"),
    )(dxo, y, gate, wout, of, om, gates)


def _kb_prep(dqn, dqr, dkn, dv, dkr, dff, tail, cos, sin, gq, gkv, wuq_t, wukv_t, tm=512):
    s_len = tail.shape[0]
    qw = 2 * GROUP_W

    def kern(dqn_ref, dqr_ref, dkn_ref, dv_ref, dkr_ref, dff_ref, tail_ref, cos_ref, sin_ref,
             gq_ref, gkv_ref, wuqt_ref, wukvt_ref, dq_ref, dz_ref, dgq_ref, dgkv_ref):
        i = pl.program_id(0)

        @pl.when(i == 0)
        def _():
            dgq_ref[...] = jnp.zeros_like(dgq_ref)
            dgkv_ref[...] = jnp.zeros_like(dgkv_ref)

        cs, sn = cos_ref[...], sin_ref[...]
        dq_ref[:, :GROUP_W] = dqn_ref[...]
        for blk in range(PAIRS):
            sl = slice(blk * LANES, (blk + 1) * LANES)
            dq_ref[:, GROUP_W + blk * LANES:GROUP_W + (blk + 1) * LANES] = _rope_bwd(dqr_ref[:, sl], cs, sn).astype(MXU)
        dqn = lax.dot_general(dq_ref[...], wuqt_ref[...], _NT, preferred_element_type=F32)
        ql = tail_ref[:, :T_KV]
        rq = lax.rsqrt(jnp.mean(ql * ql, axis=-1, keepdims=True) + EPS)
        qh = ql * rq
        dgq_ref[...] += jnp.sum(dqn * qh, axis=0, keepdims=True)
        dqh = dqn * gq_ref[...]
        dz_ref[:, :Q_LORA] = (rq * (dqh - qh * jnp.mean(dqh * qh, axis=-1, keepdims=True))).astype(MXU)

        dkvn = (lax.dot_general(dkn_ref[...], wukvt_ref[:, :GROUP_W], _NT, preferred_element_type=F32)
                + lax.dot_general(dv_ref[...], wukvt_ref[:, GROUP_W:], _NT, preferred_element_type=F32))
        kvl = tail_ref[:, T_KV:T_MISC]
        rk = lax.rsqrt(jnp.mean(kvl * kvl, axis=-1, keepdims=True) + EPS)
        kh = kvl * rk
        dgkv_ref[...] += jnp.sum(dkvn * kh, axis=0, keepdims=True)
        dkh = dkvn * gkv_ref[...]
        dz_ref[:, Q_LORA:Q_LORA + KV_LORA] = (
            rk * (dkh - kh * jnp.mean(dkh * kh, axis=-1, keepdims=True))).astype(MXU)

        g = dkr_ref[...] + pltpu.roll(dkr_ref[...], HEAD_DIM, 1)
        lane = lax.broadcasted_iota(jnp.int32, g.shape, 1)
        dmisc = jnp.where(lane < ROPE, _rope_bwd(g, cs, sn), 0.0) + dff_ref[...]
        dz_ref[:, Q_LORA + KV_LORA:] = dmisc.astype(MXU)

    return pl.pallas_call(
        kern, name="kb_prep", grid=(s_len // tm,),
        out_shape=[_sds((s_len, qw), MXU), _sds((s_len, TAIL_W), MXU), _sds((1, Q_LORA)), _sds((1, KV_LORA))],
        in_specs=[_rows(tm, GROUP_W), _rows(tm, GROUP_W), _rows(tm, GROUP_W), _rows(tm, GROUP_W), _rows(tm, LANES),
                  _rows(tm, LANES), _rows(tm, TAIL_W),
                  _rows(tm, LANES), _rows(tm, LANES), _full((1, Q_LORA)), _full((1, KV_LORA)),
                  _full((Q_LORA, qw)), _full((KV_LORA, 2 * GROUP_W))],
        out_specs=[_rows(tm, qw), _rows(tm, TAIL_W), _full((1, Q_LORA)), _full((1, KV_LORA))],
        compiler_params=_params("arbitrary"),
    )(dqn, dqr, dkn, dv, dkr, dff, tail, cos, sin, gq, gkv, wuq_t, wukv_t)


def _kb_in(dz_pieces, w, x, g, mod3, dxo, tm=256):
    s_len, d = x.shape
    widths = [p.shape[1] for p in dz_pieces]
    n_p = len(widths)

    def kern(*refs):
        dz_refs = refs[:n_p]
        w_ref, x_ref, g_ref, mod_ref, dxo_ref, dx_ref, acc_ref = refs[n_p:]
        i = pl.program_id(0)

        @pl.when(i == 0)
        def _():
            acc_ref[...] = jnp.zeros_like(acc_ref)

        dh = jnp.zeros((tm, d), F32)
        lo = 0
        for p_ref, wd in zip(dz_refs, widths):
            dh = dh + lax.dot_general(p_ref[...], w_ref[:, lo:lo + wd], _NT, preferred_element_type=F32)
            lo += wd
        xv = x_ref[...]
        r = lax.rsqrt(jnp.mean(xv * xv, axis=-1, keepdims=True) + EPS)
        xh = xv * r
        xn = xh * g_ref[...]
        dxn = dh * (1.0 + mod_ref[1:2, :])
        acc_ref[0:1, :] += jnp.sum(dh, axis=0, keepdims=True)
        acc_ref[1:2, :] += jnp.sum(dh * xn, axis=0, keepdims=True)
        acc_ref[2:3, :] += jnp.sum(dxn * xh, axis=0, keepdims=True)
        dxh = dxn * g_ref[...]
        dx_ref[...] = dxo_ref[...] + r * (dxh - xh * jnp.mean(dxh * xh, axis=-1, keepdims=True))

    return pl.pallas_call(
        kern, name="kb_in", grid=(s_len // tm,),
        out_shape=[_sds((s_len, d)), _sds((3, d))],
        in_specs=[_rows(tm, wd) for wd in widths] + [_full((d, Z_W)), _rows(tm, d), _full((1, d)), _full((3, d)),
                                                     _rows(tm, d)],
        out_specs=[_rows(tm, d), _full((3, d))],
        compiler_params=_params("arbitrary"),
    )(*dz_pieces, w, x, g, mod3, dxo)


def _weight_grad(a, pieces, name, tk=512):
    s_len, m = a.shape
    widths = [p.shape[1] for p in pieces]
    n = sum(widths)
    tk = min(tk, s_len)

    def kern(a_ref, *refs):
        o_ref = refs[-1]

        @pl.when(pl.program_id(0) == 0)
        def _():
            o_ref[...] = jnp.zeros_like(o_ref)

        at = a_ref[...]
        lo = 0
        for p_ref, w in zip(refs[:-1], widths):
            o_ref[:, lo:lo + w] += lax.dot_general(at, p_ref[...], _TN, preferred_element_type=F32)
            lo += w

    return pl.pallas_call(
        kern, name=name, grid=(s_len // tk,), out_shape=_sds((m, n)),
        in_specs=[_rows(tk, m)] + [_rows(tk, w) for w in widths],
        out_specs=_full((m, n)),
        compiler_params=_params("arbitrary"),
    )(a, *pieces)


def _adamw(slabs, w, m, v, name):
    n_l = len(slabs)
    n, r, c = slabs[0].shape
    tm = r
    for cand in (256, 128, 64, 32, 16, 8):
        if r % cand == 0:
            tm = cand
            break
    steps = r // tm

    def kern(*refs):
        g_refs = refs[:n_l]
        w_ref, m_ref, v_ref, go_ref, d_ref, mo_ref, vo_ref, g_scr = refs[n_l:]
        for ll in range(n_l):
            @pl.when(pl.program_id(0) == ll)
            def _(g_ref=g_refs[ll]):
                g = g_ref[0].astype(F32)
                for s in range(1, n):
                    g = g + g_ref[s].astype(F32)
                g_scr[...] = g

        g = g_scr[...]
        m_new = ADAM_B1 * m_ref[...] + (1.0 - ADAM_B1) * g
        v_new = ADAM_B2 * v_ref[...] + (1.0 - ADAM_B2) * (g * g)
        m_hat = m_new / (1.0 - ADAM_B1 ** ADAM_STEP)
        v_hat = v_new / (1.0 - ADAM_B2 ** ADAM_STEP)
        go_ref[...] = g
        mo_ref[...] = m_new
        vo_ref[...] = v_new
        d_ref[...] = -ADAM_LR * (m_hat / (jnp.sqrt(v_hat) + ADAM_EPS) + ADAM_WD * w_ref[...])

    row = pl.BlockSpec((tm, c), lambda l, i: (l * steps + i, 0))

    def slab_spec(ll):
        return pl.BlockSpec((n, tm, c), lambda l, i: (0, jnp.where(l == ll, i, 0), 0))

    return pl.pallas_call(
        kern, name=name, grid=(n_l, steps), out_shape=[_sds((n_l * r, c))] * 4,
        in_specs=[slab_spec(ll) for ll in range(n_l)] + [row, row, row],
        out_specs=[row] * 4,
        scratch_shapes=[pltpu.VMEM((tm, c), F32)],
        compiler_params=_params("arbitrary", "arbitrary"),
    )(*slabs, w, m, v)


def _perm_w_in(w):
    pad = jnp.zeros(w.shape[:-1] + (Z_W - Z_MISC - ROPE - HEADS,), w.dtype)
    return jnp.concatenate([w[..., 0:1536], w[..., 1544:2056], w[..., 2472:2984], w[..., 2056:2312],
                            w[..., 2312:2440], w[..., 2440:2472], w[..., 1536:1544], pad], axis=-1)


def _unperm_w_in(g):
    ff0 = Z_MISC + MISC_FF
    return jnp.concatenate([g[..., 0:1536], g[..., ff0:ff0 + HEADS], g[..., Z_FG:Z_FG + GROUP_W],
                            g[..., Z_QL:Z_QL + Q_LORA], g[..., Z_KV:Z_KV + KV_LORA],
                            g[..., Z_MISC:Z_MISC + ROPE], g[..., Z_MG:Z_MG + GROUP_W]], axis=-1)


def _perm_w_uq(w):
    lead = w.shape[:-1]
    wh = w.reshape(lead + (PAIRS, 2, NOPE + ROPE))
    zero = jnp.zeros(lead + (PAIRS, HEAD_DIM - ROPE), w.dtype)
    rope = jnp.concatenate([wh[..., 1, NOPE:], zero, wh[..., 0, NOPE:], zero], axis=-1)
    return jnp.concatenate([wh[..., :NOPE].reshape(lead + (GROUP_W,)), rope.reshape(lead + (GROUP_W,))], axis=-1)


def _unperm_w_uq(g):
    lead = g.shape[:-1]
    nope = g[..., :GROUP_W].reshape(lead + (PAIRS, 2, NOPE))
    rp = g[..., GROUP_W:].reshape(lead + (PAIRS, 2, HEAD_DIM))[..., :ROPE]
    return jnp.concatenate([nope, rp[..., ::-1, :]], axis=-1).reshape(lead + (HEADS * (NOPE + ROPE),))


def _perm_w_ukv(w):
    lead = w.shape[:-1]
    wh = w.reshape(lead + (HEADS, 2 * HEAD_DIM))
    return jnp.concatenate([wh[..., :NOPE].reshape(lead + (GROUP_W,)),
                            wh[..., NOPE:].reshape(lead + (GROUP_W,))], axis=-1)


def _unperm_w_ukv(g):
    lead = g.shape[:-1]
    parts = [g[..., :GROUP_W].reshape(lead + (HEADS, NOPE)), g[..., GROUP_W:].reshape(lead + (HEADS, HEAD_DIM))]
    return jnp.concatenate(parts, axis=-1).reshape(lead + (2 * GROUP_W,))


def _rope_tables(positions):
    inv_freq = 1.0 / (ROPE_THETA ** (jnp.arange(0, ROPE, 2, dtype=F32) / ROPE))
    ang = positions.astype(F32)[:, None] * inv_freq
    cos, sin = jnp.cos(ang), jnp.sin(ang)
    reps = LANES // ROPE
    return jnp.tile(jnp.concatenate([cos, cos], axis=1), (1, reps)), jnp.tile(jnp.concatenate([-sin, sin], axis=1), (1, reps))


def _full_weights(g_in, g_uq, g_ukv, g_out):
    def cols(g):
        return g.transpose(1, 0, 2).reshape(g.shape[1], -1)
    return (_perm_w_in(cols(g_in)), _perm_w_uq(cols(g_uq)), _perm_w_ukv(cols(g_ukv)),
            g_out.reshape(-1, g_out.shape[2]))


def _grad_slabs(dw_in, dw_uq, dw_ukv, dw_out):
    def cols(g):
        return g.reshape(g.shape[0], N_DEV, -1).transpose(1, 0, 2)
    return [cols(_unperm_w_in(dw_in)), cols(_unperm_w_uq(dw_uq)), cols(_unperm_w_ukv(dw_ukv)),
            dw_out.reshape(N_DEV, -1, dw_out.shape[1])]


def _local_step(x, mod, positions, loss_target, norm_g, b_f, q_norm_g, kv_norm_g, final_g, weights, shards=None):
    n_l = norm_g.shape[0]
    s_len, d = x.shape
    cos, sin = _rope_tables(positions)
    qb, kb, vb = Z_FQ // LANES, Z_FK // LANES, Z_FV // LANES
    chunks = s_len // LANES
    weights = list(weights)

    def pack_rows(a_rows, delta):
        return jnp.concatenate([a_rows.T, delta[:, :HEADS], jnp.zeros((s_len, LANES - 2 * HEADS), F32)], axis=1)

    saved = []
    for l in range(n_l):
        w_in, w_uq, w_ukv, w_out = weights[l]
        mod3 = mod[l].reshape(3, d)
        h, qkv, gates, tail = _k_in(x, norm_g[l][None], mod3, w_in)
        fft = tail[:, T_MISC + MISC_FF:T_MISC + MISC_FF + HEADS].T.reshape(HEADS * chunks, LANES)
        bf = jnp.repeat(b_f[l], chunks)[:, None]
        c2 = _k_cum(fft, bf, chunks).reshape(HEADS, s_len) * LOG2E
        side = (list(shards[l + 1]), [True] * 4) if shards is not None and l + 1 < n_l else None
        ck_lanes = jnp.pad(c2.T, ((0, 0), (0, LANES - HEADS)))
        of, lse_f, *gathered = _attention_fwd(qkv, qb, qkv, kb, qkv, vb, ck_lanes, None, False,
                                              "fox_fwd_gather" if side else "fox_fwd", side)
        if side:
            weights.append(_full_weights(*gathered))
        mq, mqr, mk, mv, kr2, qn, kvn = _k_prep(tail, cos, sin, q_norm_g[l][None], kv_norm_g[l][None], w_uq, w_ukv)
        om, lse_m = _attention_fwd(mq, 0, mk, 0, mv, 0, None, (mqr, kr2), True, "mla_fwd")
        x_new, y, u = _k_out(of, om, gates, x, mod3[2:3], w_out)
        saved.append((x, gates, tail, h, qkv, fft, bf, c2, lse_f, mq, mqr, mk, mv, kr2, lse_m, of, om, qn, kvn, y, u,
                      mod3))
        x = x_new

    loss_row, dx, dfinal = _k_loss(x, final_g[None], loss_target)

    grads = {k: [] for k in ("norm_g", "mod", "w_in", "b_f", "q_norm_g", "w_uq", "kv_norm_g", "w_ukv", "w_out")}
    received, pending = {}, None
    for l in range(n_l - 1, -1, -1):
        (x_l, gates, tail, h, qkv, fft, bf, c2, lse_f, mq, mqr, mk, mv, kr2, lse_m, of, om, qn, kvn, y, u,
         mod3) = saved[l]
        w_in, w_uq, w_ukv, w_out = weights[l]
        dyb, dof, dom, dfg, dmg, dlt_f, dlt_m, dgate = _kb_out(dx, y, mod3[2:3], w_out, of, om, gates)
        dw_out = _weight_grad(u, [dyb], "dw_out")

        side = (pending, [False] * 4) if pending is not None else None
        dfq, dfk, dfv, dcq, dck, *arrived = _attention_bwd(
            qkv, qb, qkv, kb, qkv, vb, dof, pack_rows(-lse_f.reshape(HEADS, s_len), dlt_f), c2[:, None, :], None,
            False, FOX_SCALE, 1.0 / LOG2E, "fox_bwd_exchange" if side else "fox_bwd", side)
        if side:
            received[l + 1] = arrived
        dcum = (dcq[:, :HEADS].T + dck.reshape(HEADS, s_len)).reshape(HEADS * chunks, LANES)
        dff_rows, dbf_rows = _k_cum_bwd(dcum, fft, bf, chunks)
        dfft = dff_rows.reshape(HEADS, s_len)
        grads["b_f"].append(jnp.sum(dbf_rows[:, 0].reshape(HEADS, chunks), axis=1))

        dmq, dkn, dmv, dqr, dkr_pairs = _attention_bwd(
            mq, 0, mk, 0, mv, 0, dom, pack_rows(-lse_m.reshape(HEADS, s_len), dlt_m), None, (mqr, kr2), True,
            MLA_SCALE, 1.0 / LOG2E, "mla_bwd")
        dkr = dkr_pairs[0] + dkr_pairs[1] + dkr_pairs[2] + dkr_pairs[3]
        dff = jnp.pad(dfft.T, ((0, 0), (MISC_FF, LANES - MISC_FF - HEADS)))
        dq_b, dz_tail, dgq, dgkv = _kb_prep(dmq, dqr, dkn, dmv, dkr, dff, tail, cos, sin, q_norm_g[l][None],
                                            kv_norm_g[l][None], w_uq, w_ukv)
        grads["q_norm_g"].append(dgq[0])
        grads["kv_norm_g"].append(dgkv[0])
        dw_uq = _weight_grad(qn, [dq_b], "dw_uq")
        dw_ukv = _weight_grad(kvn, [dkn, dmv], "dw_ukv")
        dz = [dfq, dfk, dfv, dfg, dmg, dz_tail]
        dw_in = _weight_grad(h, dz, "dw_in")
        dx, acc3 = _kb_in(dz, w_in, x_l, norm_g[l][None], mod3, dx)
        grads["norm_g"].append(acc3[2])
        grads["mod"].append(jnp.concatenate([acc3[0], acc3[1], dgate[0]]))
        if shards is not None:
            pending = _grad_slabs(dw_in, dw_uq, dw_ukv, dw_out)
        else:
            for name, g in (("w_in", dw_in), ("w_uq", dw_uq), ("w_ukv", dw_ukv), ("w_out", dw_out)):
                grads[name].append(g)
    grads = {k: jnp.stack(v[::-1]) for k, v in grads.items() if v}
    grads["final_g"] = dfinal[0]
    if shards is None:
        return loss_row[0, 0], dx, grads
    return loss_row[0, 0], dx, grads, received, pending


def _pack_small(parts, total):
    flat = jnp.concatenate([p.reshape(-1) for p in parts])
    return jnp.pad(flat, (0, total - flat.shape[0])).reshape(total // LANES, LANES)


def kernel(x, c, positions, norm_g, w_ada, b_ada, w_in, b_f, q_norm_g, w_uq, kv_norm_g, w_ukv, w_out, final_g, loss_target, m_norm_g, m_w_ada, m_b_ada, m_w_in, m_b_f, m_q_norm_g, m_w_uq, m_kv_norm_g, m_w_ukv, m_w_out, m_final_g, v_norm_g, v_w_ada, v_b_ada, v_w_in, v_b_f, v_q_norm_g, v_w_uq, v_kv_norm_g, v_w_ukv, v_w_out, v_final_g):
    n_l, d = norm_g.shape
    me = 4 * lax.axis_index("x") + 2 * lax.axis_index("y") + lax.axis_index("c")
    ada_c = w_ada.shape[2]

    cact = jnp.broadcast_to(jax.nn.silu(c), (N_DEV, d))
    shards = [[w[l].astype(MXU) for w in (w_in, w_uq, w_ukv, w_out)] for l in range(n_l)]
    *g_w0, g_cact = _exchange(shards[0] + [cact], [True] * 5, "gather_layer0")
    cact_all = g_cact[:, 0, :]

    b_cols = lax.dynamic_slice_in_dim(b_ada, me * ada_c, ada_c, axis=1)[:, None, :]
    modpart = _modpart(cact_all, w_ada, b_cols)
    mod_send = jnp.pad(modpart.transpose(1, 0, 2), ((0, 0), (0, 8 - n_l), (0, 0)))
    (mod_recv,) = _exchange([mod_send], [False], "scatter_mod")
    mod = mod_recv.transpose(1, 0, 2).reshape(8, N_DEV * ada_c)[:n_l]

    loss, dx, gr, received, pending = _local_step(x[0], mod, positions[0], loss_target[0], norm_g, b_f, q_norm_g,
                                                  kv_norm_g, final_g, [_full_weights(*g_w0)], shards)

    small_parts = [gr["norm_g"], gr["mod"], gr["b_f"], gr["q_norm_g"], gr["kv_norm_g"], gr["final_g"], cact[0]]
    sizes = [int(np.prod(p.shape)) for p in small_parts]
    total = -(-sum(sizes) // 1024) * 1024
    small = _pack_small(small_parts, total)
    *received[0], r_small = _exchange([p.astype(MXU) for p in pending] + [small],
                                      [False, False, False, False, True], "exchange_layer0")
    r_in, r_uq, r_ukv, r_out = ([received[l][i] for l in range(n_l)] for i in range(4))

    def upd(slabs, w, m, v, name):
        shp = w.shape
        w2, m2, v2 = (a.reshape(-1, slabs[0].shape[2]) for a in (w, m, v))
        return [o.reshape(shp) for o in _adamw(slabs, w2, m2, v2, name)]

    o_in = upd(r_in, w_in, m_w_in, v_w_in, "adamw_w_in")
    o_uq = upd(r_uq, w_uq, m_w_uq, v_w_uq, "adamw_w_uq")
    o_ukv = upd(r_ukv, w_ukv, m_w_ukv, v_w_ukv, "adamw_w_ukv")
    o_out = upd(r_out, w_out, m_w_out, v_w_out, "adamw_w_out")

    offs = np.cumsum([0] + sizes)
    flat_all = r_small.reshape(N_DEV, total)
    dmod_all = flat_all[:, offs[1]:offs[2]].reshape(N_DEV, n_l, 3 * d)
    dmod_cols = lax.dynamic_slice_in_dim(dmod_all, me * ada_c, ada_c, axis=2).transpose(1, 0, 2)
    cact_cols = flat_all[:, offs[6]:offs[7]][:, :, None]
    g_ada = _ada_grad(cact_cols, dmod_cols)
    o_ada = upd([g_ada.reshape(1, n_l * d, ada_c)], w_ada, m_w_ada, v_w_ada, "adamw_w_ada")

    zero_c = jnp.zeros((d,), F32)
    small_w = [_pack_small([norm_g, b_ada, b_f, q_norm_g, kv_norm_g, final_g, zero_c], total),
               _pack_small([m_norm_g, m_b_ada, m_b_f, m_q_norm_g, m_kv_norm_g, m_final_g, zero_c], total),
               _pack_small([v_norm_g, v_b_ada, v_b_f, v_q_norm_g, v_kv_norm_g, v_final_g, zero_c], total)]
    o_small = [o.reshape(-1) for o in _adamw([r_small], *small_w, "adamw_small")]
    shapes = [norm_g.shape, b_ada.shape, b_f.shape, q_norm_g.shape, kv_norm_g.shape, final_g.shape]

    def small_out(kind, idx):
        return o_small[kind][offs[idx]:offs[idx + 1]].reshape(shapes[idx])

    loss_all = lax.psum(loss, ("x", "y", "c"))
    outs = [loss_all, dx[None]]
    for kind in range(4):
        outs += [small_out(kind, 0), o_ada[kind], small_out(kind, 1), o_in[kind], small_out(kind, 2),
                 small_out(kind, 3), o_uq[kind], small_out(kind, 4), o_ukv[kind], o_out[kind], small_out(kind, 5)]
    return tuple(outs)
```

```python
import jax
import jax.numpy as jnp
import numpy as np
from jax import lax
from jax.experimental import pallas as pl
from jax.experimental.pallas import tpu as pltpu

F32 = jnp.float32
MXU = jnp.bfloat16

N_DEV = 8
HEADS = 8
PAIRS = HEADS // 2
HEAD_DIM = 64
NOPE = 64
ROPE = 32
HALF_ROPE = ROPE // 2
Q_LORA = 256
KV_LORA = 128
CHUNK = 64
GROUP_W = HEADS * HEAD_DIM
ROPE_W = HEADS * ROPE
EPS = 1e-6
ROPE_THETA = 10000.0
N_IN = 2984

Z_FQ, Z_FK, Z_FV, Z_FG, Z_MG, Z_QL, Z_KV, Z_MISC, Z_W = 0, 512, 1024, 1536, 2048, 2560, 2816, 2944, 3072
MISC_FF = ROPE
TAIL_W = Z_W - Z_QL
T_KV, T_MISC = Q_LORA, Q_LORA + KV_LORA

ADAM_LR = 0.001
ADAM_B1 = 0.9
ADAM_B2 = 0.999
ADAM_EPS = 1e-08
ADAM_WD = 0.01
ADAM_STEP = 10

VMEM_LIMIT_V7X = 56 * 1024 * 1024
LANES = 128
ATTN_TILE = 1024
V_ROWS = HEAD_DIM + 16
LOG2E = 1.4426950408889634
FOX_SCALE = HEAD_DIM ** -0.5
MLA_SCALE = (NOPE + ROPE) ** -0.5

_NT = (((1,), (1,)), ((), ()))
_TN = (((0,), (0,)), ((), ()))


def _params(*sem, side_effects=False):
    return pltpu.CompilerParams(dimension_semantics=sem, vmem_limit_bytes=VMEM_LIMIT_V7X,
                                has_side_effects=side_effects)


def _sds(shape, dtype=F32):
    return jax.ShapeDtypeStruct(shape, dtype)


def _full(shape):
    nd = len(shape)
    return pl.BlockSpec(shape, lambda *_: (0,) * nd)


def _rows(tm, width, col=0):
    return pl.BlockSpec((tm, width), lambda i: (i, col))


def _exchange(arrs, gather, name):
    n = len(arrs)

    def kern(*refs):
        copies = _exchange_copies(refs[:n], refs[n:2 * n], gather, *refs[2 * n:])
        _exchange_start(copies)
        _exchange_wait(copies)

    return pl.pallas_call(
        kern, name=name, out_shape=_exchange_out_shapes(arrs, gather),
        in_specs=[pl.BlockSpec(memory_space=pl.ANY)] * n,
        out_specs=[pl.BlockSpec(memory_space=pl.ANY)] * n,
        scratch_shapes=_exchange_sems(n),
        compiler_params=pltpu.CompilerParams(has_side_effects=True),
    )(*arrs)


def _gather_two_level(arrs, name):
    n = len(arrs)

    def kern(*refs):
        ins, outs = refs[:n], refs[n:2 * n]
        send_sems, recv_sems, loc_sems = refs[2 * n:]
        x, y, c = lax.axis_index("x"), lax.axis_index("y"), lax.axis_index("c")
        me, sibling = (x, y, c), (x, y, 1 - c)
        chips = [(1 - x, y), (x, 1 - y), (1 - x, 1 - y)]

        def slot(i, dev):
            return outs[i].at[4 * dev[0] + 2 * dev[1] + dev[2]]

        def copy(i, k, block, to, src=None):
            return pltpu.make_async_remote_copy(
                src_ref=slot(i, block) if src is None else src, dst_ref=slot(i, block), send_sem=send_sems.at[i, k],
                recv_sem=recv_sems.at[i, k], device_id=to, device_id_type=pl.DeviceIdType.MESH)

        mine = [pltpu.make_async_copy(ins[i], slot(i, me), loc_sems.at[i]) for i in range(n)]
        first = [copy(i, 0, me, sibling, src=ins[i]) for i in range(n)]
        first += [copy(i, 1 + j, me, (*chip, c), src=ins[i]) for j, chip in enumerate(chips) for i in range(n)]
        for cp in mine + first:
            cp.start()
        passed = []
        for j, chip in enumerate(chips):
            for i in range(n):
                copy(i, 1 + j, (*chip, c), me).wait_recv()
                fwd = copy(i, 4 + j, (*chip, c), sibling)
                fwd.start()
                passed.append(fwd)
        for i in range(n):
            copy(i, 0, sibling, me).wait_recv()
            for j, chip in enumerate(chips):
                copy(i, 4 + j, (*chip, 1 - c), me).wait_recv()
        for cp in first + passed:
            cp.wait_send()
        for cp in mine:
            cp.wait()

    return pl.pallas_call(
        kern, name=name, out_shape=_exchange_out_shapes(arrs, [True] * n),
        in_specs=[pl.BlockSpec(memory_space=pl.ANY)] * n,
        out_specs=[pl.BlockSpec(memory_space=pl.ANY)] * n,
        scratch_shapes=_exchange_sems(n),
        compiler_params=pltpu.CompilerParams(has_side_effects=True),
    )(*arrs)


def _exchange_out_shapes(arrs, gather):
    return [_sds((N_DEV,) + tuple(a.shape) if g else tuple(a.shape), a.dtype) for a, g in zip(arrs, gather)]


def _exchange_sems(n):
    return [pltpu.SemaphoreType.DMA((n, N_DEV)), pltpu.SemaphoreType.DMA((n, N_DEV)), pltpu.SemaphoreType.DMA((n,))]


def _exchange_copies(ins, outs, gather, send_sems, recv_sems, loc_sems, recv=True):
    n = len(ins)
    x, y, c = lax.axis_index("x"), lax.axis_index("y"), lax.axis_index("c")
    me = 4 * x + 2 * y + c

    def src(i, j):
        return ins[i] if gather[i] else ins[i].at[j]

    local = [pltpu.make_async_copy(src(i, me), outs[i].at[me], loc_sems.at[i]) for i in range(n)]
    sends, recvs = [], []
    for k in range(1, N_DEV):
        px = 1 - x if k & 4 else x
        py = 1 - y if k & 2 else y
        pc = 1 - c if k & 1 else c
        p = 4 * px + 2 * py + pc
        for i in range(n):
            sends.append(pltpu.make_async_remote_copy(
                src_ref=src(i, p), dst_ref=outs[i].at[me], send_sem=send_sems.at[i, k],
                recv_sem=recv_sems.at[i, k], device_id=(px, py, pc), device_id_type=pl.DeviceIdType.MESH))
            if recv:
                recvs.append(pltpu.make_async_remote_copy(
                    src_ref=src(i, p), dst_ref=outs[i].at[p], send_sem=send_sems.at[i, k],
                    recv_sem=recv_sems.at[i, k], device_id=(px, py, pc), device_id_type=pl.DeviceIdType.MESH))
    return local, sends, recvs


def _exchange_start(copies):
    local, sends, _ = copies
    for cp in local + sends:
        cp.start()


def _exchange_wait(copies):
    local, sends, recvs = copies
    for cp in recvs:
        cp.wait_recv()
    for cp in sends:
        cp.wait_send()
    for cp in local:
        cp.wait()


def _modpart(cact8, w_ada, b_cols):
    n_l, d, cw = w_ada.shape

    def kern(c_ref, w_ref, b_ref, o_ref):
        o_ref[0] = jnp.dot(c_ref[...].astype(MXU), w_ref[0].astype(MXU), preferred_element_type=F32) + b_ref[0]

    return pl.pallas_call(
        kern, name="modpart", grid=(n_l,), out_shape=_sds((n_l, N_DEV, cw)),
        in_specs=[_full((N_DEV, d)), pl.BlockSpec((1, d, cw), lambda l: (l, 0, 0)),
                  pl.BlockSpec((1, 1, cw), lambda l: (l, 0, 0))],
        out_specs=pl.BlockSpec((1, N_DEV, cw), lambda l: (l, 0, 0)),
        compiler_params=_params("arbitrary"),
    )(cact8, w_ada, b_cols)


def _ada_grad(cact_cols, dmod_cols):
    n_l, _, cw = dmod_cols.shape
    d = cact_cols.shape[1]

    def kern(c_ref, dm_ref, o_ref):
        acc = c_ref[0] * dm_ref[0, 0:1, :]
        for s in range(1, N_DEV):
            acc = acc + c_ref[s] * dm_ref[0, s:s + 1, :]
        o_ref[0] = acc

    return pl.pallas_call(
        kern, name="ada_grad", grid=(n_l,), out_shape=_sds((n_l, d, cw)),
        in_specs=[_full((N_DEV, d, 1)), pl.BlockSpec((1, N_DEV, cw), lambda l: (l, 0, 0))],
        out_specs=pl.BlockSpec((1, d, cw), lambda l: (l, 0, 0)),
        compiler_params=_params("arbitrary"),
    )(cact_cols, dmod_cols)


def _k_in(x, g, mod3, w, tm=256):
    s_len, d = x.shape
    qkv_w = 3 * GROUP_W

    def kern(x_ref, g_ref, mod_ref, w_ref, h_ref, qkv_ref, gates_ref, tail_ref):
        xv = x_ref[...]
        r = lax.rsqrt(jnp.mean(xv * xv, axis=-1, keepdims=True) + EPS)
        xn = xv * r * g_ref[...]
        h = (xn * (1.0 + mod_ref[1:2, :]) + mod_ref[0:1, :]).astype(MXU)
        h_ref[...] = h
        z = jnp.dot(h, w_ref[...], preferred_element_type=F32)
        qkv_ref[:, :GROUP_W] = (z[:, Z_FQ:Z_FQ + GROUP_W] * (FOX_SCALE * LOG2E)).astype(MXU)
        qkv_ref[:, GROUP_W:] = z[:, Z_FK:Z_FK + 2 * GROUP_W].astype(MXU)
        gates_ref[...] = z[:, Z_FG:Z_QL]
        tail_ref[...] = z[:, Z_QL:]

    return pl.pallas_call(
        kern, name="k_in", grid=(s_len // tm,),
        out_shape=[_sds((s_len, d), MXU), _sds((s_len, qkv_w), MXU), _sds((s_len, Z_QL - Z_FG)),
                   _sds((s_len, TAIL_W))],
        in_specs=[_rows(tm, d), _full((1, d)), _full((3, d)), _full((d, Z_W))],
        out_specs=[_rows(tm, d), _rows(tm, qkv_w), _rows(tm, Z_QL - Z_FG), _rows(tm, TAIL_W)],
        compiler_params=_params("arbitrary"),
    )(x, g, mod3, w)


def _scan_matrices(rows, chunks, reverse):
    r_i = lax.broadcasted_iota(jnp.int32, (LANES, LANES), 0)
    c_i = lax.broadcasted_iota(jnp.int32, (LANES, LANES), 1)
    a_i = lax.broadcasted_iota(jnp.int32, (rows, rows), 0)
    b_i = lax.broadcasted_iota(jnp.int32, (rows, rows), 1)
    same_head = (a_i // chunks) == (b_i // chunks)
    if reverse:
        return (r_i >= c_i).astype(F32), (same_head & (b_i > a_i)).astype(F32)
    return (r_i <= c_i).astype(F32), (same_head & (b_i < a_i)).astype(F32)


def _scan_rows(x, inner, outer):
    tot = jnp.broadcast_to(jnp.sum(x, axis=1, keepdims=True), x.shape)
    return (jnp.dot(x, inner, precision=lax.Precision.HIGHEST, preferred_element_type=F32)
            + jnp.dot(outer, tot, precision=lax.Precision.HIGHEST, preferred_element_type=F32))


def _k_cum(ff_rows, b_rows, chunks):
    rows = ff_rows.shape[0]

    def kern(ff_ref, b_ref, cum_ref):
        xc = ff_ref[...] + b_ref[...]
        lf = jnp.minimum(xc, 0.0) - jnp.log(1.0 + jnp.exp(-jnp.abs(xc)))
        cum_ref[...] = _scan_rows(lf, *_scan_matrices(rows, chunks, False))

    return pl.pallas_call(
        kern, name="k_cum", out_shape=_sds((rows, LANES)),
        in_specs=[pl.BlockSpec(memory_space=pltpu.VMEM)] * 2,
        out_specs=pl.BlockSpec(memory_space=pltpu.VMEM),
        compiler_params=_params(),
    )(ff_rows, b_rows)


def _k_cum_bwd(dc_rows, ff_rows, b_rows, chunks):
    rows = ff_rows.shape[0]

    def kern(dc_ref, ff_ref, b_ref, dff_ref, db_ref):
        dlf = _scan_rows(dc_ref[...], *_scan_matrices(rows, chunks, True))
        dff = dlf * jax.nn.sigmoid(-(ff_ref[...] + b_ref[...]))
        dff_ref[...] = dff
        db_ref[...] = jnp.broadcast_to(jnp.sum(dff, axis=1, keepdims=True), dff.shape)

    return pl.pallas_call(
        kern, name="k_cum_bwd", out_shape=[_sds((rows, LANES)), _sds((rows, LANES))],
        in_specs=[pl.BlockSpec(memory_space=pltpu.VMEM)] * 3,
        out_specs=[pl.BlockSpec(memory_space=pltpu.VMEM)] * 2,
        compiler_params=_params(),
    )(dc_rows, ff_rows, b_rows)


def _swap16(t):
    lane = lax.broadcasted_iota(jnp.int32, t.shape, 1)
    return jnp.where(lane % ROPE < HALF_ROPE, pltpu.roll(t, LANES - HALF_ROPE, 1), pltpu.roll(t, HALF_ROPE, 1))


def _rope(t, cos, sin):
    return t * cos + _swap16(t) * sin


def _rope_bwd(dt, cos, sin):
    return dt * cos - _swap16(dt) * sin


def _k_prep(tail, cos, sin, gq, gkv, wuq, wukv, tm=512):
    s_len = tail.shape[0]
    qc = MLA_SCALE * LOG2E

    def kern(tail_ref, cos_ref, sin_ref, gq_ref, gkv_ref, wuq_ref, wukv_ref,
             qn_out, qr_out, kn_out, v_out, kr_out, qn_ref, kvn_ref):
        cs, sn = cos_ref[...], sin_ref[...]
        ql = tail_ref[:, :T_KV]
        rq = lax.rsqrt(jnp.mean(ql * ql, axis=-1, keepdims=True) + EPS)
        qn = (ql * rq * gq_ref[...]).astype(MXU)
        qn_ref[...] = qn
        q = jnp.dot(qn, wuq_ref[...], preferred_element_type=F32)
        qn_out[...] = (q[:, :GROUP_W] * qc).astype(MXU)
        for blk in range(PAIRS):
            lo = GROUP_W + blk * LANES
            qr_out[:, blk * LANES:(blk + 1) * LANES] = (_rope(q[:, lo:lo + LANES], cs, sn) * qc).astype(MXU)
        kvl = tail_ref[:, T_KV:T_MISC]
        rk = lax.rsqrt(jnp.mean(kvl * kvl, axis=-1, keepdims=True) + EPS)
        kvn = (kvl * rk * gkv_ref[...]).astype(MXU)
        kvn_ref[...] = kvn
        kv = jnp.dot(kvn, wukv_ref[...], preferred_element_type=F32)
        kn_out[...] = kv[:, :GROUP_W].astype(MXU)
        v_out[...] = kv[:, GROUP_W:].astype(MXU)
        misc = tail_ref[:, T_MISC:]
        lane = lax.broadcasted_iota(jnp.int32, misc.shape, 1)
        kr = jnp.where(lane < ROPE, _rope(misc, cs, sn), 0.0)
        kr_out[...] = (kr + pltpu.roll(kr, HEAD_DIM, 1)).astype(MXU)

    return pl.pallas_call(
        kern, name="k_prep", grid=(s_len // tm,),
        out_shape=[_sds((s_len, GROUP_W), MXU), _sds((s_len, GROUP_W), MXU), _sds((s_len, GROUP_W), MXU),
                   _sds((s_len, GROUP_W), MXU), _sds((s_len, LANES), MXU), _sds((s_len, Q_LORA), MXU),
                   _sds((s_len, KV_LORA), MXU)],
        in_specs=[_rows(tm, TAIL_W), _rows(tm, LANES), _rows(tm, LANES),
                  _full((1, Q_LORA)), _full((1, KV_LORA)), _full((Q_LORA, 2 * GROUP_W)),
                  _full((KV_LORA, 2 * GROUP_W))],
        out_specs=[_rows(tm, GROUP_W), _rows(tm, GROUP_W), _rows(tm, GROUP_W), _rows(tm, GROUP_W), _rows(tm, LANES),
                   _rows(tm, Q_LORA), _rows(tm, KV_LORA)],
        compiler_params=_params("arbitrary"),
    )(tail, cos, sin, gq, gkv, wuq, wukv)


def _block_mask(kn, qn, q_off, chunk_mask, transposed):
    shape = (kn, qn) if transposed else (qn, kn)
    row = lax.broadcasted_iota(jnp.int32, shape, 0)
    col = lax.broadcasted_iota(jnp.int32, shape, 1)
    qi, ki = (col + q_off, row) if transposed else (row + q_off, col)
    if chunk_mask:
        return (ki // CHUNK) <= (qi // CHUNK)
    return ki <= qi


def _head_operand(x, hh, other=None):
    lane = lax.broadcasted_iota(jnp.int32, x.shape, 1)
    own = (lane >= hh * HEAD_DIM) & (lane < (hh + 1) * HEAD_DIM)
    return jnp.where(own, x, jnp.zeros_like(x) if other is None else other)


def _attention_fwd(q, q_blk, k, k_blk, v, v_blk, bias, rope, chunk_mask, name, side=None):
    s_len = q.shape[0]
    t = min(ATTN_TILE, s_len // 2)
    nq = s_len // t
    n_side = len(side[0]) if side else 0

    def kern(*refs):
        q_ref, k_ref, v_ref = refs[:3]
        pos = 3
        if bias is not None:
            ck_ref = refs[pos]
            pos += 1
        if rope is not None:
            qr_ref, kr_ref = refs[pos:pos + 2]
            pos += 2
        side_in = refs[pos:pos + n_side]
        pos += n_side
        o_ref, lse_ref = refs[pos:pos + 2]
        side_out = refs[pos + 2:pos + 2 + n_side]
        vt_scr, m_scr, acc_scr, ck_scr = refs[pos + 2 + n_side:pos + 6 + n_side]
        sems = refs[pos + 6 + n_side:]
        pj = pl.program_id(0)
        if n_side:
            @pl.when(pj == 0)
            def _():
                _exchange_start(_exchange_copies(side_in, side_out, side[1], *sems, recv=False))
        vt_scr[:, HEAD_DIM:, :] = jnp.ones((2, V_ROWS - HEAD_DIM, s_len), vt_scr.dtype)
        for i in range(nq):
            vtt = v_ref[i * t:(i + 1) * t, :].T
            for hh in range(2):
                vt_scr[hh, :HEAD_DIM, i * t:(i + 1) * t] = vtt[hh * HEAD_DIM:(hh + 1) * HEAD_DIM, :]
                if bias is not None:
                    ckt = ck_ref[i * t:(i + 1) * t, :]
                    lane = lax.broadcasted_iota(jnp.int32, ckt.shape, 1)
                    ck_scr[hh, i * t:(i + 1) * t, :] = jnp.sum(jnp.where(lane == 2 * pj + hh, ckt, 0.0), axis=1,
                                                               keepdims=True)

        def qbody(qi, _):
            qs = pl.multiple_of(qi * t, t)
            qt = q_ref[pl.ds(qs, t), :]
            qrt = qr_ref[pl.ds(qs, t), :] if rope is not None else None
            qh = [_head_operand(qt, hh, qrt) for hh in range(2)]
            m_scr[...] = jnp.full(m_scr.shape, -jnp.inf, F32)
            acc_scr[...] = jnp.zeros(acc_scr.shape, F32)

            def block(ks, kn, q0, qn, masked):
                kt = k_ref[pl.ds(ks, kn), :]
                kh = [_head_operand(kt, hh, kr_ref[pl.ds(ks, kn), :]) for hh in range(2)] if rope is not None else [kt, kt]
                qc = slice(q0, q0 + qn)
                sts = [lax.dot_general(kh[hh], qh[hh][qc], _NT, preferred_element_type=F32) for hh in range(2)]
                if bias is not None:
                    sts = [sts[hh] - ck_scr[hh, pl.ds(ks, kn), :] for hh in range(2)]
                if masked:
                    sts = [jnp.where(_block_mask(kn, qn, q0, chunk_mask, True), st, -jnp.inf) for st in sts]
                m_old = [m_scr[hh, :, qc] for hh in range(2)]
                m_new = [jnp.maximum(m_old[hh], jnp.max(sts[hh], axis=0, keepdims=True)) for hh in range(2)]
                pts = [jnp.exp2(sts[hh] - m_new[hh]).astype(MXU) for hh in range(2)]
                for hh in range(2):
                    alpha = jnp.exp2(m_old[hh] - m_new[hh])
                    acc_scr[hh, :, qc] = alpha * acc_scr[hh, :, qc] + jnp.dot(vt_scr[hh, :, pl.ds(ks, kn)], pts[hh],
                                                                            preferred_element_type=F32)
                    m_scr[hh, :, qc] = m_new[hh]

            def loop_body(ki, carry):
                block(pl.multiple_of(ki * t, t), t, 0, t, False)
                return carry

            lax.fori_loop(0, qi, loop_body, 0)
            block(qs, t, 0, t, True)
            outs = []
            for hh in range(2):
                acc = acc_scr[hh]
                l = acc[HEAD_DIM:HEAD_DIM + 1, :]
                outs.append(acc[:HEAD_DIM, :] / l)
                lse_ref[hh, :, pl.ds(qs, t)] = m_scr[hh] + jnp.log2(l)
            o_ref[pl.ds(qs, t), :] = jnp.concatenate(outs, axis=0).T
            return 0

        lax.fori_loop(0, nq, qbody, 0)
        if n_side:
            @pl.when(pj == PAIRS - 1)
            def _():
                _exchange_wait(_exchange_copies(side_in, side_out, side[1], *sems))

    def tok(blk):
        return pl.BlockSpec((s_len, LANES), lambda j: (0, blk + j))

    rowb = pl.BlockSpec((2, 1, s_len), lambda j: (j, 0, 0))
    hbm = pl.BlockSpec(memory_space=pl.ANY)
    ins = [q, k, v]
    in_specs = [tok(q_blk), tok(k_blk), tok(v_blk)]
    if bias is not None:
        ins.append(bias)
        in_specs.append(_full((s_len, LANES)))
    if rope is not None:
        ins += list(rope)
        in_specs += [tok(0), _full((s_len, LANES))]
    out_shape = [_sds((s_len, PAIRS * LANES)), _sds((HEADS, 1, s_len))]
    scratch = [pltpu.VMEM((2, V_ROWS, s_len), v.dtype), pltpu.VMEM((2, 1, t), F32), pltpu.VMEM((2, V_ROWS, t), F32),
               pltpu.VMEM((2, s_len if bias is not None else 8, 1), F32)]
    if n_side:
        ins += list(side[0])
        out_shape += _exchange_out_shapes(*side)
        scratch += _exchange_sems(n_side)
    return pl.pallas_call(
        kern, name=name, grid=(PAIRS,), out_shape=out_shape,
        in_specs=in_specs + [hbm] * n_side, out_specs=[tok(0), rowb] + [hbm] * n_side,
        scratch_shapes=scratch,
        compiler_params=_params("arbitrary", side_effects=bool(n_side)),
    )(*ins)


def _attention_bwd(q, q_blk, k, k_blk, v, v_blk, do, pack, ck_row, rope, chunk_mask, q_scale, k_scale, name,
                   side=None):
    s_len = q.shape[0]
    t = min(ATTN_TILE, s_len // 2)
    nq = s_len // t
    has_bias = ck_row is not None
    nv = 2 if rope is not None else 1
    n_side = len(side[0]) if side else 0

    def kern(*refs):
        q_ref, k_ref, v_ref, do_ref, pack_ref = refs[:5]
        pos = 5
        if has_bias:
            ck_ref = refs[pos]
            pos += 1
        if rope is not None:
            qr_ref, kr_ref = refs[pos:pos + 2]
            pos += 2
        side_in = refs[pos:pos + n_side]
        pos += n_side
        dq_ref, dk_ref, dv_ref = refs[pos:pos + 3]
        pos += 3
        if has_bias:
            dcq_ref, dck_ref = refs[pos:pos + 2]
            pos += 2
        if rope is not None:
            dqr_ref, dkr_ref = refs[pos:pos + 2]
            pos += 2
        side_out = refs[pos:pos + n_side]
        pos += n_side
        qt_scr, dot_scr, dkt_scr, dvt_scr, dq_scr, dcq_scr = refs[pos:pos + 6]
        sems = refs[pos + 6:]
        pj = pl.program_id(0)
        if n_side:
            @pl.when(pj == 0)
            def _():
                _exchange_start(_exchange_copies(side_in, side_out, side[1], *sems, recv=False))

        for i in range(nq):
            sl = slice(i * t, (i + 1) * t)
            dot_scr[:, sl] = do_ref[sl, :].T
            if rope is not None:
                for hh in range(2):
                    qt_scr[hh, :, sl] = _head_operand(q_ref[sl, :], hh, qr_ref[sl, :]).T
            else:
                qt_scr[0, :, sl] = q_ref[sl, :].T
        dkt_scr[...] = jnp.zeros(dkt_scr.shape, F32)
        dvt_scr[...] = jnp.zeros(dvt_scr.shape, F32)
        if has_bias:
            dck_ref[...] = jnp.zeros(dck_ref.shape, F32)

            @pl.when(pj == 0)
            def _():
                dcq_ref[...] = jnp.zeros(dcq_ref.shape, F32)

        def qbody(qi, _):
            qs = pl.multiple_of(qi * t, t)
            qt = q_ref[pl.ds(qs, t), :]
            qrt = qr_ref[pl.ds(qs, t), :] if rope is not None else None
            dot = do_ref[pl.ds(qs, t), :]
            pk = pack_ref[pl.ds(qs, t), :]
            lane = lax.broadcasted_iota(jnp.int32, pk.shape, 1)
            qh = [_head_operand(qt, hh, qrt) for hh in range(2)]
            doh = [_head_operand(dot, hh) for hh in range(2)]
            a_col = [jnp.sum(jnp.where(lane == 2 * pj + hh, pk, 0.0), axis=1, keepdims=True) for hh in range(2)]
            d_col = [jnp.sum(jnp.where(lane == HEADS + 2 * pj + hh, pk, 0.0), axis=1, keepdims=True)
                     for hh in range(2)]
            dq_scr[...] = jnp.zeros(dq_scr.shape, F32)
            if has_bias:
                dcq_scr[...] = jnp.zeros(dcq_scr.shape, F32)

            def block(ks, kn, q0, qn, masked):
                kt = k_ref[pl.ds(ks, kn), :]
                krt = kr_ref[pl.ds(ks, kn), :] if rope is not None else None
                kh = [_head_operand(kt, hh, krt) for hh in range(2)]
                vt = v_ref[pl.ds(ks, kn), :]
                qr_ = slice(q0, q0 + qn)
                qcols = pl.ds(pl.multiple_of(qs + q0, t // 2), qn)
                ss = [lax.dot_general(qh[hh][qr_], kh[hh] if rope is not None else kt, _NT,
                                      preferred_element_type=F32) + a_col[hh][qr_] for hh in range(2)]
                if has_bias:
                    ss = [ss[hh] - ck_ref[hh, :, pl.ds(ks, kn)] for hh in range(2)]
                dpds = [lax.dot_general(doh[hh][qr_], vt, _NT, preferred_element_type=F32) for hh in range(2)]
                ps = [jnp.exp2(s) for s in ss]
                if masked:
                    ps = [jnp.where(_block_mask(kn, qn, q0, chunk_mask, False), p, 0.0) for p in ps]
                dss = [ps[hh] * (dpds[hh] - d_col[hh][qr_]) for hh in range(2)]
                for hh in range(2):
                    rows = slice(hh * HEAD_DIM, (hh + 1) * HEAD_DIM)
                    dsb = dss[hh].astype(MXU)
                    dvt_scr[rows, pl.ds(ks, kn)] += jnp.dot(dot_scr[rows, qcols], ps[hh].astype(MXU),
                                                            preferred_element_type=F32)
                    if rope is not None:
                        dkt_scr[hh, :, pl.ds(ks, kn)] += jnp.dot(qt_scr[hh, :, qcols], dsb,
                                                                 preferred_element_type=F32)
                    else:
                        dkt_scr[0, rows, pl.ds(ks, kn)] += jnp.dot(qt_scr[0, rows, qcols], dsb,
                                                                   preferred_element_type=F32)
                    dq_scr[hh if rope is not None else 0, qr_, :] += jnp.dot(dsb, kh[hh], preferred_element_type=F32)
                    if has_bias:
                        dcq_scr[hh, qr_, :] += jnp.sum(dss[hh], axis=1, keepdims=True)
                        dck_ref[hh, :, pl.ds(ks, kn)] += -jnp.sum(dss[hh], axis=0, keepdims=True)

            def loop_body(ki, carry):
                block(pl.multiple_of(ki * t, t), t, 0, t, False)
                return carry

            lax.fori_loop(0, qi, loop_body, 0)
            block(qs, t // 2, 0, t // 2, True)
            block(qs, t, t // 2, t // 2, True)
            if rope is not None:
                first = lane < HEAD_DIM
                dq_ref[pl.ds(qs, t), :] = (jnp.where(first, dq_scr[0], dq_scr[1]) * q_scale).astype(dq_ref.dtype)
                dqr_ref[pl.ds(qs, t), :] = jnp.where(first, dq_scr[1], dq_scr[0]) * q_scale
            else:
                dq_ref[pl.ds(qs, t), :] = (dq_scr[0] * q_scale).astype(dq_ref.dtype)
            if has_bias:
                old = dcq_ref[pl.ds(qs, t), :]
                dcq_ref[pl.ds(qs, t), :] = jnp.where(lane == 2 * pj, dcq_scr[0],
                                                     jnp.where(lane == 2 * pj + 1, dcq_scr[1], old))
            return 0

        lax.fori_loop(0, nq, qbody, 0)
        for i in range(nq):
            sl = slice(i * t, (i + 1) * t)
            dv_ref[sl, :] = dvt_scr[:, sl].T.astype(dv_ref.dtype)
            if rope is not None:
                d0, d1 = dkt_scr[0, :, sl], dkt_scr[1, :, sl]
                first = lax.broadcasted_iota(jnp.int32, d0.shape, 0) < HEAD_DIM
                dk_ref[sl, :] = (jnp.where(first, d0, d1).T * k_scale).astype(dk_ref.dtype)
                dkr_ref[0, sl, :] = jnp.where(first, d1, d0).T * k_scale
            else:
                dk_ref[sl, :] = (dkt_scr[0, :, sl].T * k_scale).astype(dk_ref.dtype)
        if n_side:
            @pl.when(pj == PAIRS - 1)
            def _():
                _exchange_wait(_exchange_copies(side_in, side_out, side[1], *sems))

    def tok(blk):
        return pl.BlockSpec((s_len, LANES), lambda j: (0, blk + j))

    shared = _full((s_len, LANES))
    rowb = pl.BlockSpec((2, 1, s_len), lambda j: (j, 0, 0))
    slab = pl.BlockSpec((1, s_len, LANES), lambda j: (j, 0, 0))
    hbm = pl.BlockSpec(memory_space=pl.ANY)
    ins = [q, k, v, do, pack]
    in_specs = [tok(q_blk), tok(k_blk), tok(v_blk), tok(0), shared]
    out_shape = [_sds((s_len, PAIRS * LANES), MXU)] * 3
    out_specs = [tok(0)] * 3
    if has_bias:
        ins.append(ck_row)
        in_specs.append(rowb)
        out_shape += [_sds((s_len, LANES)), _sds((HEADS, 1, s_len))]
        out_specs += [shared, rowb]
    if rope is not None:
        ins += list(rope)
        in_specs += [tok(0), shared]
        out_shape += [_sds((s_len, PAIRS * LANES)), _sds((PAIRS, s_len, LANES))]
        out_specs += [tok(0), slab]
    scratch = [pltpu.VMEM((nv, LANES, s_len), q.dtype), pltpu.VMEM((LANES, s_len), do.dtype),
               pltpu.VMEM((nv, LANES, s_len), F32), pltpu.VMEM((LANES, s_len), F32),
               pltpu.VMEM((nv, t, LANES), F32), pltpu.VMEM((2, t, 1), F32)]
    if n_side:
        ins += list(side[0])
        out_shape += _exchange_out_shapes(*side)
        scratch += _exchange_sems(n_side)
    return pl.pallas_call(
        kern, name=name, grid=(PAIRS,), out_shape=out_shape,
        in_specs=in_specs + [hbm] * n_side, out_specs=out_specs + [hbm] * n_side, scratch_shapes=scratch,
        compiler_params=_params("arbitrary", side_effects=bool(n_side)),
    )(*ins)


def _silu(a):
    return a * jax.nn.sigmoid(a)


def _k_out(of, om, gates, x, gate, wout, tm=256):
    s_len, d = x.shape

    def kern(of_ref, om_ref, gates_ref, x_ref, gate_ref, w_ref, xo_ref, y_ref, u_ref):
        u_ref[:, :GROUP_W] = (of_ref[...] * _silu(gates_ref[:, :GROUP_W])).astype(MXU)
        u_ref[:, GROUP_W:] = (om_ref[...] * _silu(gates_ref[:, GROUP_W:])).astype(MXU)
        y = jnp.dot(u_ref[...], w_ref[...], preferred_element_type=F32)
        y_ref[...] = y
        xo_ref[...] = x_ref[...] + gate_ref[...] * y

    return pl.pallas_call(
        kern, name="k_out", grid=(s_len // tm,),
        out_shape=[_sds((s_len, d)), _sds((s_len, d)), _sds((s_len, 2 * GROUP_W), MXU)],
        in_specs=[_rows(tm, GROUP_W), _rows(tm, GROUP_W), _rows(tm, 2 * GROUP_W), _rows(tm, d), _full((1, d)),
                  _full((2 * GROUP_W, d))],
        out_specs=[_rows(tm, d), _rows(tm, d), _rows(tm, 2 * GROUP_W)],
        compiler_params=_params("arbitrary"),
    )(of, om, gates, x, gate, wout)


def _k_loss(x, gf, tgt, tm=256):
    s_len, d = x.shape

    def kern(x_ref, g_ref, t_ref, loss_ref, dx_ref, dg_ref):
        i = pl.program_id(0)
        xv = x_ref[...]
        r = lax.rsqrt(jnp.mean(xv * xv, axis=-1, keepdims=True) + EPS)
        xh = xv * r
        diff = xh * g_ref[...] - t_ref[...]
        part = 0.5 * jnp.sum(jnp.mean(diff * diff, axis=-1, keepdims=True))
        dout = diff * (1.0 / d)
        dxh = dout * g_ref[...]
        dx_ref[...] = r * (dxh - xh * jnp.mean(dxh * xh, axis=-1, keepdims=True))

        @pl.when(i == 0)
        def _():
            loss_ref[...] = jnp.zeros_like(loss_ref)
            dg_ref[...] = jnp.zeros_like(dg_ref)

        loss_ref[...] += jnp.full(loss_ref.shape, part, F32)
        dg_ref[...] += jnp.sum(dout * xh, axis=0, keepdims=True)

    return pl.pallas_call(
        kern, name="k_loss", grid=(s_len // tm,),
        out_shape=[_sds((1, LANES)), _sds((s_len, d)), _sds((1, d))],
        in_specs=[_rows(tm, d), _full((1, d)), _rows(tm, d)],
        out_specs=[_full((1, LANES)), _rows(tm, d), _full((1, d))],
        compiler_params=_params("arbitrary"),
    )(x, gf, tgt)


def _kb_out(dxo, y, gate, wout, of, om, gates, tm=256):
    s_len, d = dxo.shape

    def kern(dxo_ref, y_ref, gate_ref, wt_ref, of_ref, om_ref, gates_ref,
             dy_ref, dof_ref, dom_ref, dfg_ref, dmg_ref, dlf_ref, dlm_ref, dgate_ref):
        i = pl.program_id(0)
        dxv = dxo_ref[...]

        @pl.when(i == 0)
        def _():
            dgate_ref[...] = jnp.zeros_like(dgate_ref)

        dgate_ref[...] += jnp.sum(dxv * y_ref[...], axis=0, keepdims=True)
        dyb = (dxv * gate_ref[...]).astype(MXU)
        dy_ref[...] = dyb
        du = lax.dot_general(dyb, wt_ref[...], _NT, preferred_element_type=F32)
        head_of = (lax.broadcasted_iota(jnp.int32, (GROUP_W, LANES), 0) // HEAD_DIM
                   == lax.broadcasted_iota(jnp.int32, (GROUP_W, LANES), 1)).astype(F32)
        for du_g, o_ref, a, do_ref, dg_ref, dl_ref in (
                (du[:, :GROUP_W], of_ref, gates_ref[:, :GROUP_W], dof_ref, dfg_ref, dlf_ref),
                (du[:, GROUP_W:], om_ref, gates_ref[:, GROUP_W:], dom_ref, dmg_ref, dlm_ref)):
            sg = jax.nn.sigmoid(a)
            ov = o_ref[...]
            dov = du_g * (a * sg)
            do_ref[...] = dov.astype(MXU)
            dg_ref[...] = (du_g * ov * (sg * (1.0 + a * (1.0 - sg)))).astype(MXU)
            dl_ref[...] = jnp.dot(dov * ov, head_of, precision=lax.Precision.HIGH, preferred_element_type=F32)

    return pl.pallas_call(
        kern, name="kb_out", grid=(s_len // tm,),
        out_shape=[_sds((s_len, d), MXU), _sds((s_len, GROUP_W), MXU), _sds((s_len, GROUP_W), MXU),
                   _sds((s_len, GROUP_W), MXU), _sds((s_len, GROUP_W), MXU), _sds((s_len, LANES)),
                   _sds((s_len, LANES)), _sds((1, d))],
        in_specs=[_rows(tm, d), _rows(tm, d), _full((1, d)), _full((2 * GROUP_W, d)), _rows(tm, GROUP_W),
                  _rows(tm, GROUP_W), _rows(tm, 2 * GROUP_W)],
        out_specs=[_rows(tm, d), _rows(tm, GROUP_W), _rows(tm, GROUP_W), _rows(tm, GROUP_W),
                   _rows(tm, GROUP_W), _rows(tm, LANES), _rows(tm, LANES), _full((1, d))],
        compiler_params=_params("arbitrary"),
    )(dxo, y, gate, wout, of, om, gates)


def _kb_prep(dqn, dqr, dkn, dv, dkr, dff, tail, cos, sin, gq, gkv, wuq_t, wukv_t, tm=512):
    s_len = tail.shape[0]
    qw = 2 * GROUP_W

    def kern(dqn_ref, dqr_ref, dkn_ref, dv_ref, dkr_ref, dff_ref, tail_ref, cos_ref, sin_ref,
             gq_ref, gkv_ref, wuqt_ref, wukvt_ref, dq_ref, dz_ref, dgq_ref, dgkv_ref):
        i = pl.program_id(0)

        @pl.when(i == 0)
        def _():
            dgq_ref[...] = jnp.zeros_like(dgq_ref)
            dgkv_ref[...] = jnp.zeros_like(dgkv_ref)

        cs, sn = cos_ref[...], sin_ref[...]
        dq_ref[:, :GROUP_W] = dqn_ref[...]
        for blk in range(PAIRS):
            sl = slice(blk * LANES, (blk + 1) * LANES)
            dq_ref[:, GROUP_W + blk * LANES:GROUP_W + (blk + 1) * LANES] = _rope_bwd(dqr_ref[:, sl], cs, sn).astype(MXU)
        dqn = lax.dot_general(dq_ref[...], wuqt_ref[...], _NT, preferred_element_type=F32)
        ql = tail_ref[:, :T_KV]
        rq = lax.rsqrt(jnp.mean(ql * ql, axis=-1, keepdims=True) + EPS)
        qh = ql * rq
        dgq_ref[...] += jnp.sum(dqn * qh, axis=0, keepdims=True)
        dqh = dqn * gq_ref[...]
        dz_ref[:, :Q_LORA] = (rq * (dqh - qh * jnp.mean(dqh * qh, axis=-1, keepdims=True))).astype(MXU)

        dkvn = (lax.dot_general(dkn_ref[...], wukvt_ref[:, :GROUP_W], _NT, preferred_element_type=F32)
                + lax.dot_general(dv_ref[...], wukvt_ref[:, GROUP_W:], _NT, preferred_element_type=F32))
        kvl = tail_ref[:, T_KV:T_MISC]
        rk = lax.rsqrt(jnp.mean(kvl * kvl, axis=-1, keepdims=True) + EPS)
        kh = kvl * rk
        dgkv_ref[...] += jnp.sum(dkvn * kh, axis=0, keepdims=True)
        dkh = dkvn * gkv_ref[...]
        dz_ref[:, Q_LORA:Q_LORA + KV_LORA] = (
            rk * (dkh - kh * jnp.mean(dkh * kh, axis=-1, keepdims=True))).astype(MXU)

        g = dkr_ref[...] + pltpu.roll(dkr_ref[...], HEAD_DIM, 1)
        lane = lax.broadcasted_iota(jnp.int32, g.shape, 1)
        dmisc = jnp.where(lane < ROPE, _rope_bwd(g, cs, sn), 0.0) + dff_ref[...]
        dz_ref[:, Q_LORA + KV_LORA:] = dmisc.astype(MXU)

    return pl.pallas_call(
        kern, name="kb_prep", grid=(s_len // tm,),
        out_shape=[_sds((s_len, qw), MXU), _sds((s_len, TAIL_W), MXU), _sds((1, Q_LORA)), _sds((1, KV_LORA))],
        in_specs=[_rows(tm, GROUP_W), _rows(tm, GROUP_W), _rows(tm, GROUP_W), _rows(tm, GROUP_W), _rows(tm, LANES),
                  _rows(tm, LANES), _rows(tm, TAIL_W),
                  _rows(tm, LANES), _rows(tm, LANES), _full((1, Q_LORA)), _full((1, KV_LORA)),
                  _full((Q_LORA, qw)), _full((KV_LORA, 2 * GROUP_W))],
        out_specs=[_rows(tm, qw), _rows(tm, TAIL_W), _full((1, Q_LORA)), _full((1, KV_LORA))],
        compiler_params=_params("arbitrary"),
    )(dqn, dqr, dkn, dv, dkr, dff, tail, cos, sin, gq, gkv, wuq_t, wukv_t)


def _kb_in(dz_pieces, w, x, g, mod3, dxo, tm=256):
    s_len, d = x.shape
    widths = [p.shape[1] for p in dz_pieces]
    n_p = len(widths)

    def kern(*refs):
        dz_refs = refs[:n_p]
        w_ref, x_ref, g_ref, mod_ref, dxo_ref, dx_ref, acc_ref = refs[n_p:]
        i = pl.program_id(0)

        @pl.when(i == 0)
        def _():
            acc_ref[...] = jnp.zeros_like(acc_ref)

        dh = jnp.zeros((tm, d), F32)
        lo = 0
        for p_ref, wd in zip(dz_refs, widths):
            dh = dh + lax.dot_general(p_ref[...], w_ref[:, lo:lo + wd], _NT, preferred_element_type=F32)
            lo += wd
        xv = x_ref[...]
        r = lax.rsqrt(jnp.mean(xv * xv, axis=-1, keepdims=True) + EPS)
        xh = xv * r
        xn = xh * g_ref[...]
        dxn = dh * (1.0 + mod_ref[1:2, :])
        acc_ref[0:1, :] += jnp.sum(dh, axis=0, keepdims=True)
        acc_ref[1:2, :] += jnp.sum(dh * xn, axis=0, keepdims=True)
        acc_ref[2:3, :] += jnp.sum(dxn * xh, axis=0, keepdims=True)
        dxh = dxn * g_ref[...]
        dx_ref[...] = dxo_ref[...] + r * (dxh - xh * jnp.mean(dxh * xh, axis=-1, keepdims=True))

    return pl.pallas_call(
        kern, name="kb_in", grid=(s_len // tm,),
        out_shape=[_sds((s_len, d)), _sds((3, d))],
        in_specs=[_rows(tm, wd) for wd in widths] + [_full((d, Z_W)), _rows(tm, d), _full((1, d)), _full((3, d)),
                                                     _rows(tm, d)],
        out_specs=[_rows(tm, d), _full((3, d))],
        compiler_params=_params("arbitrary"),
    )(*dz_pieces, w, x, g, mod3, dxo)


def _weight_grad(a, pieces, name, tk=512):
    s_len, m = a.shape
    widths = [p.shape[1] for p in pieces]
    n = sum(widths)
    tk = min(tk, s_len)

    def kern(a_ref, *refs):
        o_ref = refs[-1]

        @pl.when(pl.program_id(0) == 0)
        def _():
            o_ref[...] = jnp.zeros_like(o_ref)

        at = a_ref[...]
        lo = 0
        for p_ref, w in zip(refs[:-1], widths):
            o_ref[:, lo:lo + w] += lax.dot_general(at, p_ref[...], _TN, preferred_element_type=F32)
            lo += w

    return pl.pallas_call(
        kern, name=name, grid=(s_len // tk,), out_shape=_sds((m, n)),
        in_specs=[_rows(tk, m)] + [_rows(tk, w) for w in widths],
        out_specs=_full((m, n)),
        compiler_params=_params("arbitrary"),
    )(a, *pieces)


def _adamw(slabs, w, m, v, name):
    n_l = len(slabs)
    n, r, c = slabs[0].shape
    tm = r
    for cand in (256, 128, 64, 32, 16, 8):
        if r % cand == 0:
            tm = cand
            break
    steps = r // tm

    def kern(*refs):
        g_refs = refs[:n_l]
        w_ref, m_ref, v_ref, go_ref, d_ref, mo_ref, vo_ref, g_scr = refs[n_l:]
        for ll in range(n_l):
            @pl.when(pl.program_id(0) == ll)
            def _(g_ref=g_refs[ll]):
                g = g_ref[0].astype(F32)
                for s in range(1, n):
                    g = g + g_ref[s].astype(F32)
                g_scr[...] = g

        g = g_scr[...]
        m_new = ADAM_B1 * m_ref[...] + (1.0 - ADAM_B1) * g
        v_new = ADAM_B2 * v_ref[...] + (1.0 - ADAM_B2) * (g * g)
        m_hat = m_new / (1.0 - ADAM_B1 ** ADAM_STEP)
        v_hat = v_new / (1.0 - ADAM_B2 ** ADAM_STEP)
        go_ref[...] = g
        mo_ref[...] = m_new
        vo_ref[...] = v_new
        d_ref[...] = -ADAM_LR * (m_hat / (jnp.sqrt(v_hat) + ADAM_EPS) + ADAM_WD * w_ref[...])

    row = pl.BlockSpec((tm, c), lambda l, i: (l * steps + i, 0))

    def slab_spec(ll):
        return pl.BlockSpec((n, tm, c), lambda l, i: (0, jnp.where(l == ll, i, 0), 0))

    return pl.pallas_call(
        kern, name=name, grid=(n_l, steps), out_shape=[_sds((n_l * r, c))] * 4,
        in_specs=[slab_spec(ll) for ll in range(n_l)] + [row, row, row],
        out_specs=[row] * 4,
        scratch_shapes=[pltpu.VMEM((tm, c), F32)],
        compiler_params=_params("arbitrary", "arbitrary"),
    )(*slabs, w, m, v)


def _perm_w_in(w):
    pad = jnp.zeros(w.shape[:-1] + (Z_W - Z_MISC - ROPE - HEADS,), w.dtype)
    return jnp.concatenate([w[..., 0:1536], w[..., 1544:2056], w[..., 2472:2984], w[..., 2056:2312],
                            w[..., 2312:2440], w[..., 2440:2472], w[..., 1536:1544], pad], axis=-1)


def _unperm_w_in(g):
    ff0 = Z_MISC + MISC_FF
    return jnp.concatenate([g[..., 0:1536], g[..., ff0:ff0 + HEADS], g[..., Z_FG:Z_FG + GROUP_W],
                            g[..., Z_QL:Z_QL + Q_LORA], g[..., Z_KV:Z_KV + KV_LORA],
                            g[..., Z_MISC:Z_MISC + ROPE], g[..., Z_MG:Z_MG + GROUP_W]], axis=-1)


def _perm_w_uq(w):
    lead = w.shape[:-1]
    wh = w.reshape(lead + (PAIRS, 2, NOPE + ROPE))
    zero = jnp.zeros(lead + (PAIRS, HEAD_DIM - ROPE), w.dtype)
    rope = jnp.concatenate([wh[..., 1, NOPE:], zero, wh[..., 0, NOPE:], zero], axis=-1)
    return jnp.concatenate([wh[..., :NOPE].reshape(lead + (GROUP_W,)), rope.reshape(lead + (GROUP_W,))], axis=-1)


def _unperm_w_uq(g):
    lead = g.shape[:-1]
    nope = g[..., :GROUP_W].reshape(lead + (PAIRS, 2, NOPE))
    rp = g[..., GROUP_W:].reshape(lead + (PAIRS, 2, HEAD_DIM))[..., :ROPE]
    return jnp.concatenate([nope, rp[..., ::-1, :]], axis=-1).reshape(lead + (HEADS * (NOPE + ROPE),))


def _perm_w_ukv(w):
    lead = w.shape[:-1]
    wh = w.reshape(lead + (HEADS, 2 * HEAD_DIM))
    return jnp.concatenate([wh[..., :NOPE].reshape(lead + (GROUP_W,)),
                            wh[..., NOPE:].reshape(lead + (GROUP_W,))], axis=-1)


def _unperm_w_ukv(g):
    lead = g.shape[:-1]
    parts = [g[..., :GROUP_W].reshape(lead + (HEADS, NOPE)), g[..., GROUP_W:].reshape(lead + (HEADS, HEAD_DIM))]
    return jnp.concatenate(parts, axis=-1).reshape(lead + (2 * GROUP_W,))


def _rope_tables(positions):
    inv_freq = 1.0 / (ROPE_THETA ** (jnp.arange(0, ROPE, 2, dtype=F32) / ROPE))
    ang = positions.astype(F32)[:, None] * inv_freq
    cos, sin = jnp.cos(ang), jnp.sin(ang)
    reps = LANES // ROPE
    return jnp.tile(jnp.concatenate([cos, cos], axis=1), (1, reps)), jnp.tile(jnp.concatenate([-sin, sin], axis=1), (1, reps))


def _full_weights(g_in, g_uq, g_ukv, g_out):
    def cols(g):
        return g.transpose(1, 0, 2).reshape(g.shape[1], -1)
    return (_perm_w_in(cols(g_in)), _perm_w_uq(cols(g_uq)), _perm_w_ukv(cols(g_ukv)),
            g_out.reshape(-1, g_out.shape[2]))


def _grad_slabs(dw_in, dw_uq, dw_ukv, dw_out):
    def cols(g):
        return g.reshape(g.shape[0], N_DEV, -1).transpose(1, 0, 2)
    dw_in, dw_uq, dw_ukv, dw_out = (g.astype(MXU) for g in (dw_in, dw_uq, dw_ukv, dw_out))
    return [cols(_unperm_w_in(dw_in)), cols(_unperm_w_uq(dw_uq)), cols(_unperm_w_ukv(dw_ukv)),
            dw_out.reshape(N_DEV, -1, dw_out.shape[1])]


def _local_step(x, mod, positions, loss_target, norm_g, b_f, q_norm_g, kv_norm_g, final_g, weights, shards=None):
    n_l = norm_g.shape[0]
    s_len, d = x.shape
    cos, sin = _rope_tables(positions)
    qb, kb, vb = Z_FQ // LANES, Z_FK // LANES, Z_FV // LANES
    chunks = s_len // LANES
    weights = list(weights)

    def pack_rows(a_rows, delta):
        return jnp.concatenate([a_rows.T, delta[:, :HEADS], jnp.zeros((s_len, LANES - 2 * HEADS), F32)], axis=1)

    saved = []
    for l in range(n_l):
        w_in, w_uq, w_ukv, w_out = weights[l]
        mod3 = mod[l].reshape(3, d)
        h, qkv, gates, tail = _k_in(x, norm_g[l][None], mod3, w_in)
        fft = tail[:, T_MISC + MISC_FF:T_MISC + MISC_FF + HEADS].T.reshape(HEADS * chunks, LANES)
        bf = jnp.repeat(b_f[l], chunks)[:, None]
        c2 = _k_cum(fft, bf, chunks).reshape(HEADS, s_len) * LOG2E
        side = (list(shards[l + 1]), [True] * 4) if shards is not None and l + 1 < n_l else None
        ck_lanes = jnp.pad(c2.T, ((0, 0), (0, LANES - HEADS)))
        of, lse_f, *gathered = _attention_fwd(qkv, qb, qkv, kb, qkv, vb, ck_lanes, None, False,
                                              "fox_fwd_gather" if side else "fox_fwd", side)
        if side:
            weights.append(_full_weights(*gathered))
        mq, mqr, mk, mv, kr2, qn, kvn = _k_prep(tail, cos, sin, q_norm_g[l][None], kv_norm_g[l][None], w_uq, w_ukv)
        om, lse_m = _attention_fwd(mq, 0, mk, 0, mv, 0, None, (mqr, kr2), True, "mla_fwd")
        x_new, y, u = _k_out(of, om, gates, x, mod3[2:3], w_out)
        saved.append((x, gates, tail, h, qkv, fft, bf, c2, lse_f, mq, mqr, mk, mv, kr2, lse_m, of, om, qn, kvn, y, u,
                      mod3))
        x = x_new

    loss_row, dx, dfinal = _k_loss(x, final_g[None], loss_target)

    grads = {k: [] for k in ("norm_g", "mod", "w_in", "b_f", "q_norm_g", "w_uq", "kv_norm_g", "w_ukv", "w_out")}
    received, pending = {}, None
    for l in range(n_l - 1, -1, -1):
        (x_l, gates, tail, h, qkv, fft, bf, c2, lse_f, mq, mqr, mk, mv, kr2, lse_m, of, om, qn, kvn, y, u,
         mod3) = saved[l]
        w_in, w_uq, w_ukv, w_out = weights[l]
        dyb, dof, dom, dfg, dmg, dlt_f, dlt_m, dgate = _kb_out(dx, y, mod3[2:3], w_out, of, om, gates)
        dw_out = _weight_grad(u, [dyb], "dw_out")

        side = (pending, [False] * 4) if pending is not None else None
        dfq, dfk, dfv, dcq, dck, *arrived = _attention_bwd(
            qkv, qb, qkv, kb, qkv, vb, dof, pack_rows(-lse_f.reshape(HEADS, s_len), dlt_f), c2[:, None, :], None,
            False, FOX_SCALE, 1.0 / LOG2E, "fox_bwd_exchange" if side else "fox_bwd", side)
        if side:
            received[l + 1] = arrived
        dcum = (dcq[:, :HEADS].T + dck.reshape(HEADS, s_len)).reshape(HEADS * chunks, LANES)
        dff_rows, dbf_rows = _k_cum_bwd(dcum, fft, bf, chunks)
        dfft = dff_rows.reshape(HEADS, s_len)
        grads["b_f"].append(jnp.sum(dbf_rows[:, 0].reshape(HEADS, chunks), axis=1))

        dmq, dkn, dmv, dqr, dkr_pairs = _attention_bwd(
            mq, 0, mk, 0, mv, 0, dom, pack_rows(-lse_m.reshape(HEADS, s_len), dlt_m), None, (mqr, kr2), True,
            MLA_SCALE, 1.0 / LOG2E, "mla_bwd")
        dkr = dkr_pairs[0] + dkr_pairs[1] + dkr_pairs[2] + dkr_pairs[3]
        dff = jnp.pad(dfft.T, ((0, 0), (MISC_FF, LANES - MISC_FF - HEADS)))
        dq_b, dz_tail, dgq, dgkv = _kb_prep(dmq, dqr, dkn, dmv, dkr, dff, tail, cos, sin, q_norm_g[l][None],
                                            kv_norm_g[l][None], w_uq, w_ukv)
        grads["q_norm_g"].append(dgq[0])
        grads["kv_norm_g"].append(dgkv[0])
        dw_uq = _weight_grad(qn, [dq_b], "dw_uq")
        dw_ukv = _weight_grad(kvn, [dkn, dmv], "dw_ukv")
        dz = [dfq, dfk, dfv, dfg, dmg, dz_tail]
        dw_in = _weight_grad(h, dz, "dw_in")
        dx, acc3 = _kb_in(dz, w_in, x_l, norm_g[l][None], mod3, dx)
        grads["norm_g"].append(acc3[2])
        grads["mod"].append(jnp.concatenate([acc3[0], acc3[1], dgate[0]]))
        if shards is not None:
            pending = _grad_slabs(dw_in, dw_uq, dw_ukv, dw_out)
        else:
            for name, g in (("w_in", dw_in), ("w_uq", dw_uq), ("w_ukv", dw_ukv), ("w_out", dw_out)):
                grads[name].append(g)
    grads = {k: jnp.stack(v[::-1]) for k, v in grads.items() if v}
    grads["final_g"] = dfinal[0]
    if shards is None:
        return loss_row[0, 0], dx, grads
    return loss_row[0, 0], dx, grads, received, pending


def _pack_small(parts, total):
    flat = jnp.concatenate([p.reshape(-1) for p in parts])
    return jnp.pad(flat, (0, total - flat.shape[0])).reshape(total // LANES, LANES)


def kernel(x, c, positions, norm_g, w_ada, b_ada, w_in, b_f, q_norm_g, w_uq, kv_norm_g, w_ukv, w_out, final_g, loss_target, m_norm_g, m_w_ada, m_b_ada, m_w_in, m_b_f, m_q_norm_g, m_w_uq, m_kv_norm_g, m_w_ukv, m_w_out, m_final_g, v_norm_g, v_w_ada, v_b_ada, v_w_in, v_b_f, v_q_norm_g, v_w_uq, v_kv_norm_g, v_w_ukv, v_w_out, v_final_g):
    n_l, d = norm_g.shape
    me = 4 * lax.axis_index("x") + 2 * lax.axis_index("y") + lax.axis_index("c")
    ada_c = w_ada.shape[2]

    cact = jnp.broadcast_to(jax.nn.silu(c), (N_DEV, d))
    shards = [[w[l].astype(MXU) for w in (w_in, w_uq, w_ukv, w_out)] for l in range(n_l)]
    *g_w0, g_cact = _gather_two_level(shards[0] + [cact], "gather_layer0")
    cact_all = g_cact[:, 0, :]

    b_cols = lax.dynamic_slice_in_dim(b_ada, me * ada_c, ada_c, axis=1)[:, None, :]
    modpart = _modpart(cact_all, w_ada, b_cols)
    mod_send = jnp.pad(modpart.transpose(1, 0, 2), ((0, 0), (0, 8 - n_l), (0, 0)))
    (mod_recv,) = _exchange([mod_send], [False], "scatter_mod")
    mod = mod_recv.transpose(1, 0, 2).reshape(8, N_DEV * ada_c)[:n_l]

    loss, dx, gr, received, pending = _local_step(x[0], mod, positions[0], loss_target[0], norm_g, b_f, q_norm_g,
                                                  kv_norm_g, final_g, [_full_weights(*g_w0)], shards)

    small_parts = [gr["norm_g"], gr["mod"], gr["b_f"], gr["q_norm_g"], gr["kv_norm_g"], gr["final_g"], cact[0]]
    sizes = [int(np.prod(p.shape)) for p in small_parts]
    total = -(-sum(sizes) // 1024) * 1024
    small = _pack_small(small_parts, total)
    *received[0], r_small = _exchange(pending + [small], [False, False, False, False, True], "exchange_layer0")
    r_in, r_uq, r_ukv, r_out = ([received[l][i] for l in range(n_l)] for i in range(4))

    def upd(slabs, w, m, v, name):
        shp = w.shape
        w2, m2, v2 = (a.reshape(-1, slabs[0].shape[2]) for a in (w, m, v))
        return [o.reshape(shp) for o in _adamw(slabs, w2, m2, v2, name)]

    o_in = upd(r_in, w_in, m_w_in, v_w_in, "adamw_w_in")
    o_uq = upd(r_uq, w_uq, m_w_uq, v_w_uq, "adamw_w_uq")
    o_ukv = upd(r_ukv, w_ukv, m_w_ukv, v_w_ukv, "adamw_w_ukv")
    o_out = upd(r_out, w_out, m_w_out, v_w_out, "adamw_w_out")

    offs = np.cumsum([0] + sizes)
    flat_all = r_small.reshape(N_DEV, total)
    dmod_all = flat_all[:, offs[1]:offs[2]].reshape(N_DEV, n_l, 3 * d)
    dmod_cols = lax.dynamic_slice_in_dim(dmod_all, me * ada_c, ada_c, axis=2).transpose(1, 0, 2)
    cact_cols = flat_all[:, offs[6]:offs[7]][:, :, None]
    g_ada = _ada_grad(cact_cols, dmod_cols)
    o_ada = upd([g_ada.reshape(1, n_l * d, ada_c)], w_ada, m_w_ada, v_w_ada, "adamw_w_ada")

    zero_c = jnp.zeros((d,), F32)
    small_w = [_pack_small([norm_g, b_ada, b_f, q_norm_g, kv_norm_g, final_g, zero_c], total),
               _pack_small([m_norm_g, m_b_ada, m_b_f, m_q_norm_g, m_kv_norm_g, m_final_g, zero_c], total),
               _pack_small([v_norm_g, v_b_ada, v_b_f, v_q_norm_g, v_kv_norm_g, v_final_g, zero_c], total)]
    o_small = [o.reshape(-1) for o in _adamw([r_small], *small_w, "adamw_small")]
    shapes = [norm_g.shape, b_ada.shape, b_f.shape, q_norm_g.shape, kv_norm_g.shape, final_g.shape]

    def small_out(kind, idx):
        return o_small[kind][offs[idx]:offs[idx + 1]].reshape(shapes[idx])

    loss_all = lax.psum(loss, ("x", "y", "c"))
    outs = [loss_all, dx[None]]
    for kind in range(4):
        outs += [small_out(kind, 0), o_ada[kind], small_out(kind, 1), o_in[kind], small_out(kind, 2),
                 small_out(kind, 3), o_uq[kind], small_out(kind, 4), o_ukv[kind], o_out[kind], small_out(kind, 5)]
    return tuple(outs)
```

```python
import jax
import jax.numpy as jnp
import numpy as np
from jax import lax
from jax.experimental import pallas as pl
from jax.experimental.pallas import tpu as pltpu

F32 = jnp.float32
MXU = jnp.bfloat16

N_DEV = 8
HEADS = 8
PAIRS = HEADS // 2
HEAD_DIM = 64
NOPE = 64
ROPE = 32
HALF_ROPE = ROPE // 2
Q_LORA = 256
KV_LORA = 128
CHUNK = 64
GROUP_W = HEADS * HEAD_DIM
ROPE_W = HEADS * ROPE
EPS = 1e-6
ROPE_THETA = 10000.0
N_IN = 2984

Z_FQ, Z_FK, Z_FV, Z_FG, Z_MG, Z_QL, Z_KV, Z_MISC, Z_W = 0, 512, 1024, 1536, 2048, 2560, 2816, 2944, 3072
MISC_FF = ROPE
TAIL_W = Z_W - Z_QL
T_KV, T_MISC = Q_LORA, Q_LORA + KV_LORA

ADAM_LR = 0.001
ADAM_B1 = 0.9
ADAM_B2 = 0.999
ADAM_EPS = 1e-08
ADAM_WD = 0.01
ADAM_STEP = 10

VMEM_LIMIT_V7X = 56 * 1024 * 1024
LANES = 128
ATTN_TILE = 1024
V_ROWS = HEAD_DIM + 16
LOG2E = 1.4426950408889634
FOX_SCALE = HEAD_DIM ** -0.5
MLA_SCALE = (NOPE + ROPE) ** -0.5

_NT = (((1,), (1,)), ((), ()))
_TN = (((0,), (0,)), ((), ()))


def _params(*sem, side_effects=False):
    return pltpu.CompilerParams(dimension_semantics=sem, vmem_limit_bytes=VMEM_LIMIT_V7X,
                                has_side_effects=side_effects)


def _sds(shape, dtype=F32):
    return jax.ShapeDtypeStruct(shape, dtype)


def _full(shape):
    nd = len(shape)
    return pl.BlockSpec(shape, lambda *_: (0,) * nd)


def _rows(tm, width, col=0):
    return pl.BlockSpec((tm, width), lambda i: (i, col))


def _exchange(arrs, gather, name):
    n = len(arrs)

    def kern(*refs):
        copies = _exchange_copies(refs[:n], refs[n:2 * n], gather, *refs[2 * n:])
        _exchange_start(copies)
        _exchange_wait(copies)

    return pl.pallas_call(
        kern, name=name, out_shape=_exchange_out_shapes(arrs, gather),
        in_specs=[pl.BlockSpec(memory_space=pl.ANY)] * n,
        out_specs=[pl.BlockSpec(memory_space=pl.ANY)] * n,
        scratch_shapes=_exchange_sems(n),
        compiler_params=pltpu.CompilerParams(has_side_effects=True),
    )(*arrs)


def _gather_two_level(arrs, name):
    n = len(arrs)

    def kern(*refs):
        ins, outs = refs[:n], refs[n:2 * n]
        send_sems, recv_sems, loc_sems = refs[2 * n:]
        x, y, c = lax.axis_index("x"), lax.axis_index("y"), lax.axis_index("c")
        me, sibling = (x, y, c), (x, y, 1 - c)
        chips = [(1 - x, y), (x, 1 - y), (1 - x, 1 - y)]

        def slot(i, dev):
            return outs[i].at[4 * dev[0] + 2 * dev[1] + dev[2]]

        def copy(i, k, block, to, src=None):
            return pltpu.make_async_remote_copy(
                src_ref=slot(i, block) if src is None else src, dst_ref=slot(i, block), send_sem=send_sems.at[i, k],
                recv_sem=recv_sems.at[i, k], device_id=to, device_id_type=pl.DeviceIdType.MESH)

        mine = [pltpu.make_async_copy(ins[i], slot(i, me), loc_sems.at[i]) for i in range(n)]
        first = [copy(i, 0, me, sibling, src=ins[i]) for i in range(n)]
        first += [copy(i, 1 + j, me, (*chip, c), src=ins[i]) for j, chip in enumerate(chips) for i in range(n)]
        for cp in mine + first:
            cp.start()
        passed = []
        for j, chip in enumerate(chips):
            for i in range(n):
                copy(i, 1 + j, (*chip, c), me).wait_recv()
                fwd = copy(i, 4 + j, (*chip, c), sibling)
                fwd.start()
                passed.append(fwd)
        for i in range(n):
            copy(i, 0, sibling, me).wait_recv()
            for j, chip in enumerate(chips):
                copy(i, 4 + j, (*chip, 1 - c), me).wait_recv()
        for cp in first + passed:
            cp.wait_send()
        for cp in mine:
            cp.wait()

    return pl.pallas_call(
        kern, name=name, out_shape=_exchange_out_shapes(arrs, [True] * n),
        in_specs=[pl.BlockSpec(memory_space=pl.ANY)] * n,
        out_specs=[pl.BlockSpec(memory_space=pl.ANY)] * n,
        scratch_shapes=_exchange_sems(n),
        compiler_params=pltpu.CompilerParams(has_side_effects=True),
    )(*arrs)


def _exchange_out_shapes(arrs, gather):
    return [_sds((N_DEV,) + tuple(a.shape) if g else tuple(a.shape), a.dtype) for a, g in zip(arrs, gather)]


def _exchange_sems(n):
    return [pltpu.SemaphoreType.DMA((n, N_DEV)), pltpu.SemaphoreType.DMA((n, N_DEV)), pltpu.SemaphoreType.DMA((n,))]


def _exchange_copies(ins, outs, gather, send_sems, recv_sems, loc_sems, recv=True):
    n = len(ins)
    x, y, c = lax.axis_index("x"), lax.axis_index("y"), lax.axis_index("c")
    me = 4 * x + 2 * y + c

    def src(i, j):
        return ins[i] if gather[i] else ins[i].at[j]

    local = [pltpu.make_async_copy(src(i, me), outs[i].at[me], loc_sems.at[i]) for i in range(n)]
    sends, recvs = [], []
    for k in range(1, N_DEV):
        px = 1 - x if k & 4 else x
        py = 1 - y if k & 2 else y
        pc = 1 - c if k & 1 else c
        p = 4 * px + 2 * py + pc
        for i in range(n):
            sends.append(pltpu.make_async_remote_copy(
                src_ref=src(i, p), dst_ref=outs[i].at[me], send_sem=send_sems.at[i, k],
                recv_sem=recv_sems.at[i, k], device_id=(px, py, pc), device_id_type=pl.DeviceIdType.MESH))
            if recv:
                recvs.append(pltpu.make_async_remote_copy(
                    src_ref=src(i, p), dst_ref=outs[i].at[p], send_sem=send_sems.at[i, k],
                    recv_sem=recv_sems.at[i, k], device_id=(px, py, pc), device_id_type=pl.DeviceIdType.MESH))
    return local, sends, recvs


def _exchange_start(copies):
    local, sends, _ = copies
    for cp in local + sends:
        cp.start()


def _exchange_wait(copies):
    local, sends, recvs = copies
    for cp in recvs:
        cp.wait_recv()
    for cp in sends:
        cp.wait_send()
    for cp in local:
        cp.wait()


def _modpart(cact8, w_ada, b_cols):
    n_l, d, cw = w_ada.shape

    def kern(c_ref, w_ref, b_ref, o_ref):
        o_ref[0] = jnp.dot(c_ref[...].astype(MXU), w_ref[0].astype(MXU), preferred_element_type=F32) + b_ref[0]

    return pl.pallas_call(
        kern, name="modpart", grid=(n_l,), out_shape=_sds((n_l, N_DEV, cw)),
        in_specs=[_full((N_DEV, d)), pl.BlockSpec((1, d, cw), lambda l: (l, 0, 0)),
                  pl.BlockSpec((1, 1, cw), lambda l: (l, 0, 0))],
        out_specs=pl.BlockSpec((1, N_DEV, cw), lambda l: (l, 0, 0)),
        compiler_params=_params("arbitrary"),
    )(cact8, w_ada, b_cols)


def _ada_grad(cact_cols, dmod_cols):
    n_l, _, cw = dmod_cols.shape
    d = cact_cols.shape[1]

    def kern(c_ref, dm_ref, o_ref):
        acc = c_ref[0] * dm_ref[0, 0:1, :]
        for s in range(1, N_DEV):
            acc = acc + c_ref[s] * dm_ref[0, s:s + 1, :]
        o_ref[0] = acc

    return pl.pallas_call(
        kern, name="ada_grad", grid=(n_l,), out_shape=_sds((n_l, d, cw)),
        in_specs=[_full((N_DEV, d, 1)), pl.BlockSpec((1, N_DEV, cw), lambda l: (l, 0, 0))],
        out_specs=pl.BlockSpec((1, d, cw), lambda l: (l, 0, 0)),
        compiler_params=_params("arbitrary"),
    )(cact_cols, dmod_cols)


def _k_in(x, g, mod3, w, tm=256):
    s_len, d = x.shape
    qkv_w = 3 * GROUP_W

    def kern(x_ref, g_ref, mod_ref, w_ref, h_ref, qkv_ref, gates_ref, tail_ref):
        xv = x_ref[...]
        r = lax.rsqrt(jnp.mean(xv * xv, axis=-1, keepdims=True) + EPS)
        xn = xv * r * g_ref[...]
        h = (xn * (1.0 + mod_ref[1:2, :]) + mod_ref[0:1, :]).astype(MXU)
        h_ref[...] = h
        z = jnp.dot(h, w_ref[...], preferred_element_type=F32)
        qkv_ref[:, :GROUP_W] = (z[:, Z_FQ:Z_FQ + GROUP_W] * (FOX_SCALE * LOG2E)).astype(MXU)
        qkv_ref[:, GROUP_W:] = z[:, Z_FK:Z_FK + 2 * GROUP_W].astype(MXU)
        gates_ref[...] = z[:, Z_FG:Z_QL]
        tail_ref[...] = z[:, Z_QL:]

    return pl.pallas_call(
        kern, name="k_in", grid=(s_len // tm,),
        out_shape=[_sds((s_len, d), MXU), _sds((s_len, qkv_w), MXU), _sds((s_len, Z_QL - Z_FG)),
                   _sds((s_len, TAIL_W))],
        in_specs=[_rows(tm, d), _full((1, d)), _full((3, d)), _full((d, Z_W))],
        out_specs=[_rows(tm, d), _rows(tm, qkv_w), _rows(tm, Z_QL - Z_FG), _rows(tm, TAIL_W)],
        compiler_params=_params("arbitrary"),
    )(x, g, mod3, w)


def _scan_matrices(rows, chunks, reverse):
    r_i = lax.broadcasted_iota(jnp.int32, (LANES, LANES), 0)
    c_i = lax.broadcasted_iota(jnp.int32, (LANES, LANES), 1)
    a_i = lax.broadcasted_iota(jnp.int32, (rows, rows), 0)
    b_i = lax.broadcasted_iota(jnp.int32, (rows, rows), 1)
    same_head = (a_i // chunks) == (b_i // chunks)
    if reverse:
        return (r_i >= c_i).astype(F32), (same_head & (b_i > a_i)).astype(F32)
    return (r_i <= c_i).astype(F32), (same_head & (b_i < a_i)).astype(F32)


def _scan_rows(x, inner, outer):
    tot = jnp.broadcast_to(jnp.sum(x, axis=1, keepdims=True), x.shape)
    return (jnp.dot(x, inner, precision=lax.Precision.HIGHEST, preferred_element_type=F32)
            + jnp.dot(outer, tot, precision=lax.Precision.HIGHEST, preferred_element_type=F32))


def _k_cum(ff_rows, b_rows, chunks):
    rows = ff_rows.shape[0]

    def kern(ff_ref, b_ref, cum_ref):
        xc = ff_ref[...] + b_ref[...]
        lf = jnp.minimum(xc, 0.0) - jnp.log(1.0 + jnp.exp(-jnp.abs(xc)))
        cum_ref[...] = _scan_rows(lf, *_scan_matrices(rows, chunks, False))

    return pl.pallas_call(
        kern, name="k_cum", out_shape=_sds((rows, LANES)),
        in_specs=[pl.BlockSpec(memory_space=pltpu.VMEM)] * 2,
        out_specs=pl.BlockSpec(memory_space=pltpu.VMEM),
        compiler_params=_params(),
    )(ff_rows, b_rows)


def _k_cum_bwd(dc_rows, ff_rows, b_rows, chunks):
    rows = ff_rows.shape[0]

    def kern(dc_ref, ff_ref, b_ref, dff_ref, db_ref):
        dlf = _scan_rows(dc_ref[...], *_scan_matrices(rows, chunks, True))
        dff = dlf * jax.nn.sigmoid(-(ff_ref[...] + b_ref[...]))
        dff_ref[...] = dff
        db_ref[...] = jnp.broadcast_to(jnp.sum(dff, axis=1, keepdims=True), dff.shape)

    return pl.pallas_call(
        kern, name="k_cum_bwd", out_shape=[_sds((rows, LANES)), _sds((rows, LANES))],
        in_specs=[pl.BlockSpec(memory_space=pltpu.VMEM)] * 3,
        out_specs=[pl.BlockSpec(memory_space=pltpu.VMEM)] * 2,
        compiler_params=_params(),
    )(dc_rows, ff_rows, b_rows)


def _swap16(t):
    lane = lax.broadcasted_iota(jnp.int32, t.shape, 1)
    return jnp.where(lane % ROPE < HALF_ROPE, pltpu.roll(t, LANES - HALF_ROPE, 1), pltpu.roll(t, HALF_ROPE, 1))


def _rope(t, cos, sin):
    return t * cos + _swap16(t) * sin


def _rope_bwd(dt, cos, sin):
    return dt * cos - _swap16(dt) * sin


def _k_prep(tail, cos, sin, gq, gkv, wuq, wukv, tm=512):
    s_len = tail.shape[0]
    qc = MLA_SCALE * LOG2E

    def kern(tail_ref, cos_ref, sin_ref, gq_ref, gkv_ref, wuq_ref, wukv_ref,
             qn_out, qr_out, kn_out, v_out, kr_out, qn_ref, kvn_ref):
        cs, sn = cos_ref[...], sin_ref[...]
        ql = tail_ref[:, :T_KV]
        rq = lax.rsqrt(jnp.mean(ql * ql, axis=-1, keepdims=True) + EPS)
        qn = (ql * rq * gq_ref[...]).astype(MXU)
        qn_ref[...] = qn
        q = jnp.dot(qn, wuq_ref[...], preferred_element_type=F32)
        qn_out[...] = (q[:, :GROUP_W] * qc).astype(MXU)
        for blk in range(PAIRS):
            lo = GROUP_W + blk * LANES
            qr_out[:, blk * LANES:(blk + 1) * LANES] = (_rope(q[:, lo:lo + LANES], cs, sn) * qc).astype(MXU)
        kvl = tail_ref[:, T_KV:T_MISC]
        rk = lax.rsqrt(jnp.mean(kvl * kvl, axis=-1, keepdims=True) + EPS)
        kvn = (kvl * rk * gkv_ref[...]).astype(MXU)
        kvn_ref[...] = kvn
        kv = jnp.dot(kvn, wukv_ref[...], preferred_element_type=F32)
        kn_out[...] = kv[:, :GROUP_W].astype(MXU)
        v_out[...] = kv[:, GROUP_W:].astype(MXU)
        misc = tail_ref[:, T_MISC:]
        lane = lax.broadcasted_iota(jnp.int32, misc.shape, 1)
        kr = jnp.where(lane < ROPE, _rope(misc, cs, sn), 0.0)
        kr_out[...] = (kr + pltpu.roll(kr, HEAD_DIM, 1)).astype(MXU)

    return pl.pallas_call(
        kern, name="k_prep", grid=(s_len // tm,),
        out_shape=[_sds((s_len, GROUP_W), MXU), _sds((s_len, GROUP_W), MXU), _sds((s_len, GROUP_W), MXU),
                   _sds((s_len, GROUP_W), MXU), _sds((s_len, LANES), MXU), _sds((s_len, Q_LORA), MXU),
                   _sds((s_len, KV_LORA), MXU)],
        in_specs=[_rows(tm, TAIL_W), _rows(tm, LANES), _rows(tm, LANES),
                  _full((1, Q_LORA)), _full((1, KV_LORA)), _full((Q_LORA, 2 * GROUP_W)),
                  _full((KV_LORA, 2 * GROUP_W))],
        out_specs=[_rows(tm, GROUP_W), _rows(tm, GROUP_W), _rows(tm, GROUP_W), _rows(tm, GROUP_W), _rows(tm, LANES),
                   _rows(tm, Q_LORA), _rows(tm, KV_LORA)],
        compiler_params=_params("arbitrary"),
    )(tail, cos, sin, gq, gkv, wuq, wukv)


def _block_mask(kn, qn, q_off, chunk_mask, transposed):
    shape = (kn, qn) if transposed else (qn, kn)
    row = lax.broadcasted_iota(jnp.int32, shape, 0)
    col = lax.broadcasted_iota(jnp.int32, shape, 1)
    qi, ki = (col + q_off, row) if transposed else (row + q_off, col)
    if chunk_mask:
        return (ki // CHUNK) <= (qi // CHUNK)
    return ki <= qi


def _head_operand(x, hh, other=None):
    lane = lax.broadcasted_iota(jnp.int32, x.shape, 1)
    own = (lane >= hh * HEAD_DIM) & (lane < (hh + 1) * HEAD_DIM)
    return jnp.where(own, x, jnp.zeros_like(x) if other is None else other)


def _attention_fwd(q, q_blk, k, k_blk, v, v_blk, bias, rope, chunk_mask, name, side=None):
    s_len = q.shape[0]
    t = min(ATTN_TILE, s_len // 2)
    nq = s_len // t
    n_side = len(side[0]) if side else 0

    def kern(*refs):
        q_ref, k_ref, v_ref = refs[:3]
        pos = 3
        if bias is not None:
            ck_ref = refs[pos]
            pos += 1
        if rope is not None:
            qr_ref, kr_ref = refs[pos:pos + 2]
            pos += 2
        side_in = refs[pos:pos + n_side]
        pos += n_side
        o_ref, lse_ref = refs[pos:pos + 2]
        side_out = refs[pos + 2:pos + 2 + n_side]
        vt_scr, m_scr, acc_scr, ck_scr = refs[pos + 2 + n_side:pos + 6 + n_side]
        sems = refs[pos + 6 + n_side:]
        pj = pl.program_id(0)
        if n_side:
            @pl.when(pj == 0)
            def _():
                _exchange_start(_exchange_copies(side_in, side_out, side[1], *sems, recv=False))
        vt_scr[:, HEAD_DIM:, :] = jnp.ones((2, V_ROWS - HEAD_DIM, s_len), vt_scr.dtype)
        for i in range(nq):
            vtt = v_ref[i * t:(i + 1) * t, :].T
            for hh in range(2):
                vt_scr[hh, :HEAD_DIM, i * t:(i + 1) * t] = vtt[hh * HEAD_DIM:(hh + 1) * HEAD_DIM, :]
                if bias is not None:
                    ckt = ck_ref[i * t:(i + 1) * t, :]
                    lane = lax.broadcasted_iota(jnp.int32, ckt.shape, 1)
                    ck_scr[hh, i * t:(i + 1) * t, :] = jnp.sum(jnp.where(lane == 2 * pj + hh, ckt, 0.0), axis=1,
                                                               keepdims=True)

        def qbody(qi, _):
            qs = pl.multiple_of(qi * t, t)
            qt = q_ref[pl.ds(qs, t), :]
            qrt = qr_ref[pl.ds(qs, t), :] if rope is not None else None
            qh = [_head_operand(qt, hh, qrt) for hh in range(2)]
            m_scr[...] = jnp.full(m_scr.shape, -jnp.inf, F32)
            acc_scr[...] = jnp.zeros(acc_scr.shape, F32)

            def block(ks, kn, q0, qn, masked):
                kt = k_ref[pl.ds(ks, kn), :]
                kh = [_head_operand(kt, hh, kr_ref[pl.ds(ks, kn), :]) for hh in range(2)] if rope is not None else [kt, kt]
                qc = slice(q0, q0 + qn)
                sts = [lax.dot_general(kh[hh], qh[hh][qc], _NT, preferred_element_type=F32) for hh in range(2)]
                if bias is not None:
                    sts = [sts[hh] - ck_scr[hh, pl.ds(ks, kn), :] for hh in range(2)]
                if masked:
                    sts = [jnp.where(_block_mask(kn, qn, q0, chunk_mask, True), st, -jnp.inf) for st in sts]
                m_old = [m_scr[hh, :, qc] for hh in range(2)]
                m_new = [jnp.maximum(m_old[hh], jnp.max(sts[hh], axis=0, keepdims=True)) for hh in range(2)]
                pts = [jnp.exp2(sts[hh] - m_new[hh]).astype(MXU) for hh in range(2)]
                for hh in range(2):
                    alpha = jnp.exp2(m_old[hh] - m_new[hh])
                    acc_scr[hh, :, qc] = alpha * acc_scr[hh, :, qc] + jnp.dot(vt_scr[hh, :, pl.ds(ks, kn)], pts[hh],
                                                                            preferred_element_type=F32)
                    m_scr[hh, :, qc] = m_new[hh]

            def loop_body(ki, carry):
                block(pl.multiple_of(ki * t, t), t, 0, t, False)
                return carry

            lax.fori_loop(0, qi, loop_body, 0)
            block(qs, t, 0, t, True)
            outs = []
            for hh in range(2):
                acc = acc_scr[hh]
                l = acc[HEAD_DIM:HEAD_DIM + 1, :]
                outs.append(acc[:HEAD_DIM, :] / l)
                lse_ref[hh, :, pl.ds(qs, t)] = m_scr[hh] + jnp.log2(l)
            o_ref[pl.ds(qs, t), :] = jnp.concatenate(outs, axis=0).T
            return 0

        lax.fori_loop(0, nq, qbody, 0)
        if n_side:
            @pl.when(pj == PAIRS - 1)
            def _():
                _exchange_wait(_exchange_copies(side_in, side_out, side[1], *sems))

    def tok(blk):
        return pl.BlockSpec((s_len, LANES), lambda j: (0, blk + j))

    rowb = pl.BlockSpec((2, 1, s_len), lambda j: (j, 0, 0))
    hbm = pl.BlockSpec(memory_space=pl.ANY)
    ins = [q, k, v]
    in_specs = [tok(q_blk), tok(k_blk), tok(v_blk)]
    if bias is not None:
        ins.append(bias)
        in_specs.append(_full((s_len, LANES)))
    if rope is not None:
        ins += list(rope)
        in_specs += [tok(0), _full((s_len, LANES))]
    out_shape = [_sds((s_len, PAIRS * LANES)), _sds((HEADS, 1, s_len))]
    scratch = [pltpu.VMEM((2, V_ROWS, s_len), v.dtype), pltpu.VMEM((2, 1, t), F32), pltpu.VMEM((2, V_ROWS, t), F32),
               pltpu.VMEM((2, s_len if bias is not None else 8, 1), F32)]
    if n_side:
        ins += list(side[0])
        out_shape += _exchange_out_shapes(*side)
        scratch += _exchange_sems(n_side)
    return pl.pallas_call(
        kern, name=name, grid=(PAIRS,), out_shape=out_shape,
        in_specs=in_specs + [hbm] * n_side, out_specs=[tok(0), rowb] + [hbm] * n_side,
        scratch_shapes=scratch,
        compiler_params=_params("arbitrary", side_effects=bool(n_side)),
    )(*ins)


def _attention_bwd(q, q_blk, k, k_blk, v, v_blk, do, pack, ck_row, rope, chunk_mask, q_scale, k_scale, name,
                   side=None):
    s_len = q.shape[0]
    t = min(ATTN_TILE, s_len // 2)
    nq = s_len // t
    has_bias = ck_row is not None
    nv = 2 if rope is not None else 1
    n_side = len(side[0]) if side else 0

    def kern(*refs):
        q_ref, k_ref, v_ref, do_ref, pack_ref = refs[:5]
        pos = 5
        if has_bias:
            ck_ref = refs[pos]
            pos += 1
        if rope is not None:
            qr_ref, kr_ref = refs[pos:pos + 2]
            pos += 2
        side_in = refs[pos:pos + n_side]
        pos += n_side
        dq_ref, dk_ref, dv_ref = refs[pos:pos + 3]
        pos += 3
        if has_bias:
            dcq_ref, dck_ref = refs[pos:pos + 2]
            pos += 2
        if rope is not None:
            dqr_ref, dkr_ref = refs[pos:pos + 2]
            pos += 2
        side_out = refs[pos:pos + n_side]
        pos += n_side
        qt_scr, dot_scr, dkt_scr, dvt_scr, dq_scr, dcq_scr = refs[pos:pos + 6]
        sems = refs[pos + 6:]
        pj = pl.program_id(0)
        if n_side:
            @pl.when(pj == 0)
            def _():
                _exchange_start(_exchange_copies(side_in, side_out, side[1], *sems, recv=False))

        for i in range(nq):
            sl = slice(i * t, (i + 1) * t)
            dot_scr[:, sl] = do_ref[sl, :].T
            if rope is not None:
                for hh in range(2):
                    qt_scr[hh, :, sl] = _head_operand(q_ref[sl, :], hh, qr_ref[sl, :]).T
            else:
                qt_scr[0, :, sl] = q_ref[sl, :].T
        dkt_scr[...] = jnp.zeros(dkt_scr.shape, F32)
        dvt_scr[...] = jnp.zeros(dvt_scr.shape, F32)
        if has_bias:
            dck_ref[...] = jnp.zeros(dck_ref.shape, F32)

            @pl.when(pj == 0)
            def _():
                dcq_ref[...] = jnp.zeros(dcq_ref.shape, F32)

        def qbody(qi, _):
            qs = pl.multiple_of(qi * t, t)
            qt = q_ref[pl.ds(qs, t), :]
            qrt = qr_ref[pl.ds(qs, t), :] if rope is not None else None
            dot = do_ref[pl.ds(qs, t), :]
            pk = pack_ref[pl.ds(qs, t), :]
            lane = lax.broadcasted_iota(jnp.int32, pk.shape, 1)
            qh = [_head_operand(qt, hh, qrt) for hh in range(2)]
            doh = [_head_operand(dot, hh) for hh in range(2)]
            a_col = [jnp.sum(jnp.where(lane == 2 * pj + hh, pk, 0.0), axis=1, keepdims=True) for hh in range(2)]
            d_col = [jnp.sum(jnp.where(lane == HEADS + 2 * pj + hh, pk, 0.0), axis=1, keepdims=True)
                     for hh in range(2)]
            dq_scr[...] = jnp.zeros(dq_scr.shape, F32)
            if has_bias:
                dcq_scr[...] = jnp.zeros(dcq_scr.shape, F32)

            def block(ks, kn, q0, qn, masked):
                kt = k_ref[pl.ds(ks, kn), :]
                krt = kr_ref[pl.ds(ks, kn), :] if rope is not None else None
                kh = [_head_operand(kt, hh, krt) for hh in range(2)]
                vt = v_ref[pl.ds(ks, kn), :]
                qr_ = slice(q0, q0 + qn)
                qcols = pl.ds(pl.multiple_of(qs + q0, t // 2), qn)
                ss = [lax.dot_general(qh[hh][qr_], kh[hh] if rope is not None else kt, _NT,
                                      preferred_element_type=F32) + a_col[hh][qr_] for hh in range(2)]
                if has_bias:
                    ss = [ss[hh] - ck_ref[hh, :, pl.ds(ks, kn)] for hh in range(2)]
                dpds = [lax.dot_general(doh[hh][qr_], vt, _NT, preferred_element_type=F32) for hh in range(2)]
                ps = [jnp.exp2(s) for s in ss]
                if masked:
                    ps = [jnp.where(_block_mask(kn, qn, q0, chunk_mask, False), p, 0.0) for p in ps]
                dss = [ps[hh] * (dpds[hh] - d_col[hh][qr_]) for hh in range(2)]
                for hh in range(2):
                    rows = slice(hh * HEAD_DIM, (hh + 1) * HEAD_DIM)
                    dsb = dss[hh].astype(MXU)
                    dvt_scr[rows, pl.ds(ks, kn)] += jnp.dot(dot_scr[rows, qcols], ps[hh].astype(MXU),
                                                            preferred_element_type=F32)
                    if rope is not None:
                        dkt_scr[hh, :, pl.ds(ks, kn)] += jnp.dot(qt_scr[hh, :, qcols], dsb,
                                                                 preferred_element_type=F32)
                    else:
                        dkt_scr[0, rows, pl.ds(ks, kn)] += jnp.dot(qt_scr[0, rows, qcols], dsb,
                                                                   preferred_element_type=F32)
                    dq_scr[hh if rope is not None else 0, qr_, :] += jnp.dot(dsb, kh[hh], preferred_element_type=F32)
                    if has_bias:
                        dcq_scr[hh, qr_, :] += jnp.sum(dss[hh], axis=1, keepdims=True)
                        dck_ref[hh, :, pl.ds(ks, kn)] += -jnp.sum(dss[hh], axis=0, keepdims=True)

            def loop_body(ki, carry):
                block(pl.multiple_of(ki * t, t), t, 0, t, False)
                return carry

            lax.fori_loop(0, qi, loop_body, 0)
            block(qs, t // 2, 0, t // 2, True)
            block(qs, t, t // 2, t // 2, True)
            if rope is not None:
                first = lane < HEAD_DIM
                dq_ref[pl.ds(qs, t), :] = (jnp.where(first, dq_scr[0], dq_scr[1]) * q_scale).astype(dq_ref.dtype)
                dqr_ref[pl.ds(qs, t), :] = jnp.where(first, dq_scr[1], dq_scr[0]) * q_scale
            else:
                dq_ref[pl.ds(qs, t), :] = (dq_scr[0] * q_scale).astype(dq_ref.dtype)
            if has_bias:
                old = dcq_ref[pl.ds(qs, t), :]
                dcq_ref[pl.ds(qs, t), :] = jnp.where(lane == 2 * pj, dcq_scr[0],
                                                     jnp.where(lane == 2 * pj + 1, dcq_scr[1], old))
            return 0

        lax.fori_loop(0, nq, qbody, 0)
        for i in range(nq):
            sl = slice(i * t, (i + 1) * t)
            dv_ref[sl, :] = dvt_scr[:, sl].T.astype(dv_ref.dtype)
            if rope is not None:
                d0, d1 = dkt_scr[0, :, sl], dkt_scr[1, :, sl]
                first = lax.broadcasted_iota(jnp.int32, d0.shape, 0) < HEAD_DIM
                dk_ref[sl, :] = (jnp.where(first, d0, d1).T * k_scale).astype(dk_ref.dtype)
                dkr_ref[0, sl, :] = jnp.where(first, d1, d0).T * k_scale
            else:
                dk_ref[sl, :] = (dkt_scr[0, :, sl].T * k_scale).astype(dk_ref.dtype)
        if n_side:
            @pl.when(pj == PAIRS - 1)
            def _():
                _exchange_wait(_exchange_copies(side_in, side_out, side[1], *sems))

    def tok(blk):
        return pl.BlockSpec((s_len, LANES), lambda j: (0, blk + j))

    shared = _full((s_len, LANES))
    rowb = pl.BlockSpec((2, 1, s_len), lambda j: (j, 0, 0))
    slab = pl.BlockSpec((1, s_len, LANES), lambda j: (j, 0, 0))
    hbm = pl.BlockSpec(memory_space=pl.ANY)
    ins = [q, k, v, do, pack]
    in_specs = [tok(q_blk), tok(k_blk), tok(v_blk), tok(0), shared]
    out_shape = [_sds((s_len, PAIRS * LANES), MXU)] * 3
    out_specs = [tok(0)] * 3
    if has_bias:
        ins.append(ck_row)
        in_specs.append(rowb)
        out_shape += [_sds((s_len, LANES)), _sds((HEADS, 1, s_len))]
        out_specs += [shared, rowb]
    if rope is not None:
        ins += list(rope)
        in_specs += [tok(0), shared]
        out_shape += [_sds((s_len, PAIRS * LANES)), _sds((PAIRS, s_len, LANES))]
        out_specs += [tok(0), slab]
    scratch = [pltpu.VMEM((nv, LANES, s_len), q.dtype), pltpu.VMEM((LANES, s_len), do.dtype),
               pltpu.VMEM((nv, LANES, s_len), F32), pltpu.VMEM((LANES, s_len), F32),
               pltpu.VMEM((nv, t, LANES), F32), pltpu.VMEM((2, t, 1), F32)]
    if n_side:
        ins += list(side[0])
        out_shape += _exchange_out_shapes(*side)
        scratch += _exchange_sems(n_side)
    return pl.pallas_call(
        kern, name=name, grid=(PAIRS,), out_shape=out_shape,
        in_specs=in_specs + [hbm] * n_side, out_specs=out_specs + [hbm] * n_side, scratch_shapes=scratch,
        compiler_params=_params("arbitrary", side_effects=bool(n_side)),
    )(*ins)


def _silu(a):
    return a * jax.nn.sigmoid(a)


def _k_out(of, om, gates, x, gate, wout, tm=256):
    s_len, d = x.shape

    def kern(of_ref, om_ref, gates_ref, x_ref, gate_ref, w_ref, xo_ref, y_ref, u_ref):
        u_ref[:, :GROUP_W] = (of_ref[...] * _silu(gates_ref[:, :GROUP_W])).astype(MXU)
        u_ref[:, GROUP_W:] = (om_ref[...] * _silu(gates_ref[:, GROUP_W:])).astype(MXU)
        y = jnp.dot(u_ref[...], w_ref[...], preferred_element_type=F32)
        y_ref[...] = y
        xo_ref[...] = x_ref[...] + gate_ref[...] * y

    return pl.pallas_call(
        kern, name="k_out", grid=(s_len // tm,),
        out_shape=[_sds((s_len, d)), _sds((s_len, d)), _sds((s_len, 2 * GROUP_W), MXU)],
        in_specs=[_rows(tm, GROUP_W), _rows(tm, GROUP_W), _rows(tm, 2 * GROUP_W), _rows(tm, d), _full((1, d)),
                  _full((2 * GROUP_W, d))],
        out_specs=[_rows(tm, d), _rows(tm, d), _rows(tm, 2 * GROUP_W)],
        compiler_params=_params("arbitrary"),
    )(of, om, gates, x, gate, wout)


def _k_loss(x, gf, tgt, tm=256):
    s_len, d = x.shape

    def kern(x_ref, g_ref, t_ref, loss_ref, dx_ref, dg_ref):
        i = pl.program_id(0)
        xv = x_ref[...]
        r = lax.rsqrt(jnp.mean(xv * xv, axis=-1, keepdims=True) + EPS)
        xh = xv * r
        diff = xh * g_ref[...] - t_ref[...]
        part = 0.5 * jnp.sum(jnp.mean(diff * diff, axis=-1, keepdims=True))
        dout = diff * (1.0 / d)
        dxh = dout * g_ref[...]
        dx_ref[...] = r * (dxh - xh * jnp.mean(dxh * xh, axis=-1, keepdims=True))

        @pl.when(i == 0)
        def _():
            loss_ref[...] = jnp.zeros_like(loss_ref)
            dg_ref[...] = jnp.zeros_like(dg_ref)

        loss_ref[...] += jnp.full(loss_ref.shape, part, F32)
        dg_ref[...] += jnp.sum(dout * xh, axis=0, keepdims=True)

    return pl.pallas_call(
        kern, name="k_loss", grid=(s_len // tm,),
        out_shape=[_sds((1, LANES)), _sds((s_len, d)), _sds((1, d))],
        in_specs=[_rows(tm, d), _full((1, d)), _rows(tm, d)],
        out_specs=[_full((1, LANES)), _rows(tm, d), _full((1, d))],
        compiler_params=_params("arbitrary"),
    )(x, gf, tgt)


def _kb_out(dxo, y, gate, wout, of, om, gates, tm=256):
    s_len, d = dxo.shape

    def kern(dxo_ref, y_ref, gate_ref, wt_ref, of_ref, om_ref, gates_ref,
             dy_ref, dof_ref, dom_ref, dfg_ref, dmg_ref, dlf_ref, dlm_ref, dgate_ref):
        i = pl.program_id(0)
        dxv = dxo_ref[...]

        @pl.when(i == 0)
        def _():
            dgate_ref[...] = jnp.zeros_like(dgate_ref)

        dgate_ref[...] += jnp.sum(dxv * y_ref[...], axis=0, keepdims=True)
        dyb = (dxv * gate_ref[...]).astype(MXU)
        dy_ref[...] = dyb
        du = lax.dot_general(dyb, wt_ref[...], _NT, preferred_element_type=F32)
        head_of = (lax.broadcasted_iota(jnp.int32, (GROUP_W, LANES), 0) // HEAD_DIM
                   == lax.broadcasted_iota(jnp.int32, (GROUP_W, LANES), 1)).astype(F32)
        for du_g, o_ref, a, do_ref, dg_ref, dl_ref in (
                (du[:, :GROUP_W], of_ref, gates_ref[:, :GROUP_W], dof_ref, dfg_ref, dlf_ref),
                (du[:, GROUP_W:], om_ref, gates_ref[:, GROUP_W:], dom_ref, dmg_ref, dlm_ref)):
            sg = jax.nn.sigmoid(a)
            ov = o_ref[...]
            dov = du_g * (a * sg)
            do_ref[...] = dov.astype(MXU)
            dg_ref[...] = (du_g * ov * (sg * (1.0 + a * (1.0 - sg)))).astype(MXU)
            dl_ref[...] = jnp.dot(dov * ov, head_of, precision=lax.Precision.HIGH, preferred_element_type=F32)

    return pl.pallas_call(
        kern, name="kb_out", grid=(s_len // tm,),
        out_shape=[_sds((s_len, d), MXU), _sds((s_len, GROUP_W), MXU), _sds((s_len, GROUP_W), MXU),
                   _sds((s_len, GROUP_W), MXU), _sds((s_len, GROUP_W), MXU), _sds((s_len, LANES)),
                   _sds((s_len, LANES)), _sds((1, d))],
        in_specs=[_rows(tm, d), _rows(tm, d), _full((1, d)), _full((2 * GROUP_W, d)), _rows(tm, GROUP_W),
                  _rows(tm, GROUP_W), _rows(tm, 2 * GROUP_W)],
        out_specs=[_rows(tm, d), _rows(tm, GROUP_W), _rows(tm, GROUP_W), _rows(tm, GROUP_W),
                   _rows(tm, GROUP_W), _rows(tm, LANES), _rows(tm, LANES), _full((1, d))],
        compiler_params=_params("arbitrary"),
    )(dxo, y, gate, wout, of, om, gates)


def _kb_prep(dqn, dqr, dkn, dv, dkr, dff, tail, cos, sin, gq, gkv, wuq_t, wukv_t, tm=512):
    s_len = tail.shape[0]
    qw = 2 * GROUP_W

    def kern(dqn_ref, dqr_ref, dkn_ref, dv_ref, dkr_ref, dff_ref, tail_ref, cos_ref, sin_ref,
             gq_ref, gkv_ref, wuqt_ref, wukvt_ref, dq_ref, dz_ref, dgq_ref, dgkv_ref):
        i = pl.program_id(0)

        @pl.when(i == 0)
        def _():
            dgq_ref[...] = jnp.zeros_like(dgq_ref)
            dgkv_ref[...] = jnp.zeros_like(dgkv_ref)

        cs, sn = cos_ref[...], sin_ref[...]
        dq_ref[:, :GROUP_W] = dqn_ref[...]
        for blk in range(PAIRS):
            sl = slice(blk * LANES, (blk + 1) * LANES)
            dq_ref[:, GROUP_W + blk * LANES:GROUP_W + (blk + 1) * LANES] = _rope_bwd(dqr_ref[:, sl], cs, sn).astype(MXU)
        dqn = lax.dot_general(dq_ref[...], wuqt_ref[...], _NT, preferred_element_type=F32)
        ql = tail_ref[:, :T_KV]
        rq = lax.rsqrt(jnp.mean(ql * ql, axis=-1, keepdims=True) + EPS)
        qh = ql * rq
        dgq_ref[...] += jnp.sum(dqn * qh, axis=0, keepdims=True)
        dqh = dqn * gq_ref[...]
        dz_ref[:, :Q_LORA] = (rq * (dqh - qh * jnp.mean(dqh * qh, axis=-1, keepdims=True))).astype(MXU)

        dkvn = (lax.dot_general(dkn_ref[...], wukvt_ref[:, :GROUP_W], _NT, preferred_element_type=F32)
                + lax.dot_general(dv_ref[...], wukvt_ref[:, GROUP_W:], _NT, preferred_element_type=F32))
        kvl = tail_ref[:, T_KV:T_MISC]
        rk = lax.rsqrt(jnp.mean(kvl * kvl, axis=-1, keepdims=True) + EPS)
        kh = kvl * rk
        dgkv_ref[...] += jnp.sum(dkvn * kh, axis=0, keepdims=True)
        dkh = dkvn * gkv_ref[...]
        dz_ref[:, Q_LORA:Q_LORA + KV_LORA] = (
            rk * (dkh - kh * jnp.mean(dkh * kh, axis=-1, keepdims=True))).astype(MXU)

        g = dkr_ref[...] + pltpu.roll(dkr_ref[...], HEAD_DIM, 1)
        lane = lax.broadcasted_iota(jnp.int32, g.shape, 1)
        dmisc = jnp.where(lane < ROPE, _rope_bwd(g, cs, sn), 0.0) + dff_ref[...]
        dz_ref[:, Q_LORA + KV_LORA:] = dmisc.astype(MXU)

    return pl.pallas_call(
        kern, name="kb_prep", grid=(s_len // tm,),
        out_shape=[_sds((s_len, qw), MXU), _sds((s_len, TAIL_W), MXU), _sds((1, Q_LORA)), _sds((1, KV_LORA))],
        in_specs=[_rows(tm, GROUP_W), _rows(tm, GROUP_W), _rows(tm, GROUP_W), _rows(tm, GROUP_W), _rows(tm, LANES),
                  _rows(tm, LANES), _rows(tm, TAIL_W),
                  _rows(tm, LANES), _rows(tm, LANES), _full((1, Q_LORA)), _full((1, KV_LORA)),
                  _full((Q_LORA, qw)), _full((KV_LORA, 2 * GROUP_W))],
        out_specs=[_rows(tm, qw), _rows(tm, TAIL_W), _full((1, Q_LORA)), _full((1, KV_LORA))],
        compiler_params=_params("arbitrary"),
    )(dqn, dqr, dkn, dv, dkr, dff, tail, cos, sin, gq, gkv, wuq_t, wukv_t)


def _kb_in(dz_pieces, w, x, g, mod3, dxo, tm=256):
    s_len, d = x.shape
    widths = [p.shape[1] for p in dz_pieces]
    n_p = len(widths)

    def kern(*refs):
        dz_refs = refs[:n_p]
        w_ref, x_ref, g_ref, mod_ref, dxo_ref, dx_ref, acc_ref = refs[n_p:]
        i = pl.program_id(0)

        @pl.when(i == 0)
        def _():
            acc_ref[...] = jnp.zeros_like(acc_ref)

        dh = jnp.zeros((tm, d), F32)
        lo = 0
        for p_ref, wd in zip(dz_refs, widths):
            dh = dh + lax.dot_general(p_ref[...], w_ref[:, lo:lo + wd], _NT, preferred_element_type=F32)
            lo += wd
        xv = x_ref[...]
        r = lax.rsqrt(jnp.mean(xv * xv, axis=-1, keepdims=True) + EPS)
        xh = xv * r
        xn = xh * g_ref[...]
        dxn = dh * (1.0 + mod_ref[1:2, :])
        acc_ref[0:1, :] += jnp.sum(dh, axis=0, keepdims=True)
        acc_ref[1:2, :] += jnp.sum(dh * xn, axis=0, keepdims=True)
        acc_ref[2:3, :] += jnp.sum(dxn * xh, axis=0, keepdims=True)
        dxh = dxn * g_ref[...]
        dx_ref[...] = dxo_ref[...] + r * (dxh - xh * jnp.mean(dxh * xh, axis=-1, keepdims=True))

    return pl.pallas_call(
        kern, name="kb_in", grid=(s_len // tm,),
        out_shape=[_sds((s_len, d)), _sds((3, d))],
        in_specs=[_rows(tm, wd) for wd in widths] + [_full((d, Z_W)), _rows(tm, d), _full((1, d)), _full((3, d)),
                                                     _rows(tm, d)],
        out_specs=[_rows(tm, d), _full((3, d))],
        compiler_params=_params("arbitrary"),
    )(*dz_pieces, w, x, g, mod3, dxo)


def _weight_grad(a, pieces, name, out_dtype=F32, tk=512):
    s_len, m = a.shape
    widths = [p.shape[1] for p in pieces]
    n = sum(widths)
    tk = min(tk, s_len)
    steps = s_len // tk

    def kern(a_ref, *refs):
        o_ref, acc_ref = refs[-2:]

        @pl.when(pl.program_id(0) == 0)
        def _():
            acc_ref[...] = jnp.zeros_like(acc_ref)

        at = a_ref[...]
        lo = 0
        for p_ref, w in zip(refs[:-2], widths):
            acc_ref[:, lo:lo + w] += lax.dot_general(at, p_ref[...], _TN, preferred_element_type=F32)
            lo += w

        @pl.when(pl.program_id(0) == steps - 1)
        def _():
            o_ref[...] = acc_ref[...].astype(o_ref.dtype)

    return pl.pallas_call(
        kern, name=name, grid=(steps,), out_shape=_sds((m, n), out_dtype),
        in_specs=[_rows(tk, m)] + [_rows(tk, w) for w in widths],
        out_specs=_full((m, n)),
        scratch_shapes=[pltpu.VMEM((m, n), F32)],
        compiler_params=_params("arbitrary"),
    )(a, *pieces)


def _adamw(slabs, w, m, v, name):
    n_l = len(slabs)
    n, r, c = slabs[0].shape
    tm = r
    for cand in (256, 128, 64, 32, 16, 8):
        if r % cand == 0:
            tm = cand
            break
    steps = r // tm

    def kern(*refs):
        g_refs = refs[:n_l]
        w_ref, m_ref, v_ref, go_ref, d_ref, mo_ref, vo_ref, g_scr = refs[n_l:]
        for ll in range(n_l):
            @pl.when(pl.program_id(0) == ll)
            def _(g_ref=g_refs[ll]):
                g = g_ref[0].astype(F32)
                for s in range(1, n):
                    g = g + g_ref[s].astype(F32)
                g_scr[...] = g

        g = g_scr[...]
        m_new = ADAM_B1 * m_ref[...] + (1.0 - ADAM_B1) * g
        v_new = ADAM_B2 * v_ref[...] + (1.0 - ADAM_B2) * (g * g)
        m_hat = m_new / (1.0 - ADAM_B1 ** ADAM_STEP)
        v_hat = v_new / (1.0 - ADAM_B2 ** ADAM_STEP)
        go_ref[...] = g
        mo_ref[...] = m_new
        vo_ref[...] = v_new
        d_ref[...] = -ADAM_LR * (m_hat / (jnp.sqrt(v_hat) + ADAM_EPS) + ADAM_WD * w_ref[...])

    row = pl.BlockSpec((tm, c), lambda l, i: (l * steps + i, 0))

    def slab_spec(ll):
        return pl.BlockSpec((n, tm, c), lambda l, i: (0, jnp.where(l == ll, i, 0), 0))

    return pl.pallas_call(
        kern, name=name, grid=(n_l, steps), out_shape=[_sds((n_l * r, c))] * 4,
        in_specs=[slab_spec(ll) for ll in range(n_l)] + [row, row, row],
        out_specs=[row] * 4,
        scratch_shapes=[pltpu.VMEM((tm, c), F32)],
        compiler_params=_params("arbitrary", "arbitrary"),
    )(*slabs, w, m, v)


def _perm_w_in(w):
    pad = jnp.zeros(w.shape[:-1] + (Z_W - Z_MISC - ROPE - HEADS,), w.dtype)
    return jnp.concatenate([w[..., 0:1536], w[..., 1544:2056], w[..., 2472:2984], w[..., 2056:2312],
                            w[..., 2312:2440], w[..., 2440:2472], w[..., 1536:1544], pad], axis=-1)


def _unperm_w_in(g):
    ff0 = Z_MISC + MISC_FF
    return jnp.concatenate([g[..., 0:1536], g[..., ff0:ff0 + HEADS], g[..., Z_FG:Z_FG + GROUP_W],
                            g[..., Z_QL:Z_QL + Q_LORA], g[..., Z_KV:Z_KV + KV_LORA],
                            g[..., Z_MISC:Z_MISC + ROPE], g[..., Z_MG:Z_MG + GROUP_W]], axis=-1)


def _perm_w_uq(w):
    lead = w.shape[:-1]
    wh = w.reshape(lead + (PAIRS, 2, NOPE + ROPE))
    zero = jnp.zeros(lead + (PAIRS, HEAD_DIM - ROPE), w.dtype)
    rope = jnp.concatenate([wh[..., 1, NOPE:], zero, wh[..., 0, NOPE:], zero], axis=-1)
    return jnp.concatenate([wh[..., :NOPE].reshape(lead + (GROUP_W,)), rope.reshape(lead + (GROUP_W,))], axis=-1)


def _unperm_w_uq(g):
    lead = g.shape[:-1]
    nope = g[..., :GROUP_W].reshape(lead + (PAIRS, 2, NOPE))
    rp = g[..., GROUP_W:].reshape(lead + (PAIRS, 2, HEAD_DIM))[..., :ROPE]
    return jnp.concatenate([nope, rp[..., ::-1, :]], axis=-1).reshape(lead + (HEADS * (NOPE + ROPE),))


def _perm_w_ukv(w):
    lead = w.shape[:-1]
    wh = w.reshape(lead + (HEADS, 2 * HEAD_DIM))
    return jnp.concatenate([wh[..., :NOPE].reshape(lead + (GROUP_W,)),
                            wh[..., NOPE:].reshape(lead + (GROUP_W,))], axis=-1)


def _unperm_w_ukv(g):
    lead = g.shape[:-1]
    parts = [g[..., :GROUP_W].reshape(lead + (HEADS, NOPE)), g[..., GROUP_W:].reshape(lead + (HEADS, HEAD_DIM))]
    return jnp.concatenate(parts, axis=-1).reshape(lead + (2 * GROUP_W,))


def _rope_tables(positions):
    inv_freq = 1.0 / (ROPE_THETA ** (jnp.arange(0, ROPE, 2, dtype=F32) / ROPE))
    ang = positions.astype(F32)[:, None] * inv_freq
    cos, sin = jnp.cos(ang), jnp.sin(ang)
    reps = LANES // ROPE
    return jnp.tile(jnp.concatenate([cos, cos], axis=1), (1, reps)), jnp.tile(jnp.concatenate([-sin, sin], axis=1), (1, reps))


def _full_weights(g_in, g_uq, g_ukv, g_out):
    def cols(g):
        return g.transpose(1, 0, 2).reshape(g.shape[1], -1)
    return (_perm_w_in(cols(g_in)), _perm_w_uq(cols(g_uq)), _perm_w_ukv(cols(g_ukv)),
            g_out.reshape(-1, g_out.shape[2]))


def _grad_slabs(dw_in, dw_uq, dw_ukv):
    def cols(g):
        return g.reshape(g.shape[0], N_DEV, -1).transpose(1, 0, 2)
    return [cols(_unperm_w_in(dw_in)), cols(_unperm_w_uq(dw_uq)), cols(_unperm_w_ukv(dw_ukv))]


def _local_step(x, mod, positions, loss_target, norm_g, b_f, q_norm_g, kv_norm_g, final_g, weights, shards=None):
    n_l = norm_g.shape[0]
    s_len, d = x.shape
    cos, sin = _rope_tables(positions)
    qb, kb, vb = Z_FQ // LANES, Z_FK // LANES, Z_FV // LANES
    chunks = s_len // LANES
    weights = list(weights)

    def pack_rows(a_rows, delta):
        return jnp.concatenate([a_rows.T, delta[:, :HEADS], jnp.zeros((s_len, LANES - 2 * HEADS), F32)], axis=1)

    saved = []
    for l in range(n_l):
        w_in, w_uq, w_ukv, w_out = weights[l]
        mod3 = mod[l].reshape(3, d)
        h, qkv, gates, tail = _k_in(x, norm_g[l][None], mod3, w_in)
        fft = tail[:, T_MISC + MISC_FF:T_MISC + MISC_FF + HEADS].T.reshape(HEADS * chunks, LANES)
        bf = jnp.repeat(b_f[l], chunks)[:, None]
        c2 = _k_cum(fft, bf, chunks).reshape(HEADS, s_len) * LOG2E
        side = (list(shards[l + 1]), [True] * 4) if shards is not None and l + 1 < n_l else None
        ck_lanes = jnp.pad(c2.T, ((0, 0), (0, LANES - HEADS)))
        of, lse_f, *gathered = _attention_fwd(qkv, qb, qkv, kb, qkv, vb, ck_lanes, None, False,
                                              "fox_fwd_gather" if side else "fox_fwd", side)
        if side:
            weights.append(_full_weights(*gathered))
        mq, mqr, mk, mv, kr2, qn, kvn = _k_prep(tail, cos, sin, q_norm_g[l][None], kv_norm_g[l][None], w_uq, w_ukv)
        om, lse_m = _attention_fwd(mq, 0, mk, 0, mv, 0, None, (mqr, kr2), True, "mla_fwd")
        x_new, y, u = _k_out(of, om, gates, x, mod3[2:3], w_out)
        saved.append((x, gates, tail, h, qkv, fft, bf, c2, lse_f, mq, mqr, mk, mv, kr2, lse_m, of, om, qn, kvn, y, u,
                      mod3))
        x = x_new

    loss_row, dx, dfinal = _k_loss(x, final_g[None], loss_target)

    grads = {k: [] for k in ("norm_g", "mod", "w_in", "b_f", "q_norm_g", "w_uq", "kv_norm_g", "w_ukv", "w_out")}
    received, pending = {}, None
    wg_dtype = MXU if shards is not None else F32
    for l in range(n_l - 1, -1, -1):
        (x_l, gates, tail, h, qkv, fft, bf, c2, lse_f, mq, mqr, mk, mv, kr2, lse_m, of, om, qn, kvn, y, u,
         mod3) = saved[l]
        w_in, w_uq, w_ukv, w_out = weights[l]
        dyb, dof, dom, dfg, dmg, dlt_f, dlt_m, dgate = _kb_out(dx, y, mod3[2:3], w_out, of, om, gates)
        dw_out = _weight_grad(u, [dyb], "dw_out", wg_dtype)

        side = None
        if shards is not None:
            side_arrs = (pending or []) + [dw_out.reshape(N_DEV, -1, dw_out.shape[1])]
            side = (side_arrs, [False] * len(side_arrs))
        dfq, dfk, dfv, dcq, dck, *arrived = _attention_bwd(
            qkv, qb, qkv, kb, qkv, vb, dof, pack_rows(-lse_f.reshape(HEADS, s_len), dlt_f), c2[:, None, :], None,
            False, FOX_SCALE, 1.0 / LOG2E, "fox_bwd_exchange" if pending else "fox_bwd", side)
        if side:
            received[l] = [None, None, None, arrived[-1]]
            if pending:
                received[l + 1][:3] = arrived[:3]
        dcum = (dcq[:, :HEADS].T + dck.reshape(HEADS, s_len)).reshape(HEADS * chunks, LANES)
        dff_rows, dbf_rows = _k_cum_bwd(dcum, fft, bf, chunks)
        dfft = dff_rows.reshape(HEADS, s_len)
        grads["b_f"].append(jnp.sum(dbf_rows[:, 0].reshape(HEADS, chunks), axis=1))

        dmq, dkn, dmv, dqr, dkr_pairs = _attention_bwd(
            mq, 0, mk, 0, mv, 0, dom, pack_rows(-lse_m.reshape(HEADS, s_len), dlt_m), None, (mqr, kr2), True,
            MLA_SCALE, 1.0 / LOG2E, "mla_bwd")
        dkr = dkr_pairs[0] + dkr_pairs[1] + dkr_pairs[2] + dkr_pairs[3]
        dff = jnp.pad(dfft.T, ((0, 0), (MISC_FF, LANES - MISC_FF - HEADS)))
        dq_b, dz_tail, dgq, dgkv = _kb_prep(dmq, dqr, dkn, dmv, dkr, dff, tail, cos, sin, q_norm_g[l][None],
                                            kv_norm_g[l][None], w_uq, w_ukv)
        grads["q_norm_g"].append(dgq[0])
        grads["kv_norm_g"].append(dgkv[0])
        dw_uq = _weight_grad(qn, [dq_b], "dw_uq", wg_dtype)
        dw_ukv = _weight_grad(kvn, [dkn, dmv], "dw_ukv", wg_dtype)
        dz = [dfq, dfk, dfv, dfg, dmg, dz_tail]
        dw_in = _weight_grad(h, dz, "dw_in", wg_dtype)
        dx, acc3 = _kb_in(dz, w_in, x_l, norm_g[l][None], mod3, dx)
        grads["norm_g"].append(acc3[2])
        grads["mod"].append(jnp.concatenate([acc3[0], acc3[1], dgate[0]]))
        if shards is not None:
            pending = _grad_slabs(dw_in, dw_uq, dw_ukv)
        else:
            for name, g in (("w_in", dw_in), ("w_uq", dw_uq), ("w_ukv", dw_ukv), ("w_out", dw_out)):
                grads[name].append(g)
    grads = {k: jnp.stack(v[::-1]) for k, v in grads.items() if v}
    grads["final_g"] = dfinal[0]
    if shards is None:
        return loss_row[0, 0], dx, grads
    return loss_row[0, 0], dx, grads, received, pending


def _pack_small(parts, total):
    flat = jnp.concatenate([p.reshape(-1) for p in parts])
    return jnp.pad(flat, (0, total - flat.shape[0])).reshape(total // LANES, LANES)


def kernel(x, c, positions, norm_g, w_ada, b_ada, w_in, b_f, q_norm_g, w_uq, kv_norm_g, w_ukv, w_out, final_g, loss_target, m_norm_g, m_w_ada, m_b_ada, m_w_in, m_b_f, m_q_norm_g, m_w_uq, m_kv_norm_g, m_w_ukv, m_w_out, m_final_g, v_norm_g, v_w_ada, v_b_ada, v_w_in, v_b_f, v_q_norm_g, v_w_uq, v_kv_norm_g, v_w_ukv, v_w_out, v_final_g):
    n_l, d = norm_g.shape
    me = 4 * lax.axis_index("x") + 2 * lax.axis_index("y") + lax.axis_index("c")
    ada_c = w_ada.shape[2]

    cact = jnp.broadcast_to(jax.nn.silu(c), (N_DEV, d))
    shards = [[w[l].astype(MXU) for w in (w_in, w_uq, w_ukv, w_out)] for l in range(n_l)]
    *g_w0, g_cact = _gather_two_level(shards[0] + [cact], "gather_layer0")
    cact_all = g_cact[:, 0, :]

    b_cols = lax.dynamic_slice_in_dim(b_ada, me * ada_c, ada_c, axis=1)[:, None, :]
    modpart = _modpart(cact_all, w_ada, b_cols)
    mod_send = jnp.pad(modpart.transpose(1, 0, 2), ((0, 0), (0, 8 - n_l), (0, 0)))
    (mod_recv,) = _exchange([mod_send], [False], "scatter_mod")
    mod = mod_recv.transpose(1, 0, 2).reshape(8, N_DEV * ada_c)[:n_l]

    loss, dx, gr, received, pending = _local_step(x[0], mod, positions[0], loss_target[0], norm_g, b_f, q_norm_g,
                                                  kv_norm_g, final_g, [_full_weights(*g_w0)], shards)

    small_parts = [gr["norm_g"], gr["mod"], gr["b_f"], gr["q_norm_g"], gr["kv_norm_g"], gr["final_g"], cact[0]]
    sizes = [int(np.prod(p.shape)) for p in small_parts]
    total = -(-sum(sizes) // 1024) * 1024
    small = _pack_small(small_parts, total)
    *received[0][:3], r_small = _exchange(pending + [small], [False, False, False, True], "exchange_layer0")
    r_in, r_uq, r_ukv, r_out = ([received[l][i] for l in range(n_l)] for i in range(4))

    def upd(slabs, w, m, v, name):
        shp = w.shape
        w2, m2, v2 = (a.reshape(-1, slabs[0].shape[2]) for a in (w, m, v))
        return [o.reshape(shp) for o in _adamw(slabs, w2, m2, v2, name)]

    o_in = upd(r_in, w_in, m_w_in, v_w_in, "adamw_w_in")
    o_uq = upd(r_uq, w_uq, m_w_uq, v_w_uq, "adamw_w_uq")
    o_ukv = upd(r_ukv, w_ukv, m_w_ukv, v_w_ukv, "adamw_w_ukv")
    o_out = upd(r_out, w_out, m_w_out, v_w_out, "adamw_w_out")

    offs = np.cumsum([0] + sizes)
    flat_all = r_small.reshape(N_DEV, total)
    dmod_all = flat_all[:, offs[1]:offs[2]].reshape(N_DEV, n_l, 3 * d)
    dmod_cols = lax.dynamic_slice_in_dim(dmod_all, me * ada_c, ada_c, axis=2).transpose(1, 0, 2)
    cact_cols = flat_all[:, offs[6]:offs[7]][:, :, None]
    g_ada = _ada_grad(cact_cols, dmod_cols)
    o_ada = upd([g_ada.reshape(1, n_l * d, ada_c)], w_ada, m_w_ada, v_w_ada, "adamw_w_ada")

    zero_c = jnp.zeros((d,), F32)
    small_w = [_pack_small([norm_g, b_ada, b_f, q_norm_g, kv_norm_g, final_g, zero_c], total),
               _pack_small([m_norm_g, m_b_ada, m_b_f, m_q_norm_g, m_kv_norm_g, m_final_g, zero_c], total),
               _pack_small([v_norm_g, v_b_ada, v_b_f, v_q_norm_g, v_kv_norm_g, v_final_g, zero_c], total)]
    o_small = [o.reshape(-1) for o in _adamw([r_small], *small_w, "adamw_small")]
    shapes = [norm_g.shape, b_ada.shape, b_f.shape, q_norm_g.shape, kv_norm_g.shape, final_g.shape]

    def small_out(kind, idx):
        return o_small[kind][offs[idx]:offs[idx + 1]].reshape(shapes[idx])

    loss_all = lax.psum(loss, ("x", "y", "c"))
    outs = [loss_all, dx[None]]
    for kind in range(4):
        outs += [small_out(kind, 0), o_ada[kind], small_out(kind, 1), o_in[kind], small_out(kind, 2),
                 small_out(kind, 3), o_uq[kind], small_out(kind, 4), o_ukv[kind], o_out[kind], small_out(kind, 5)]
    return tuple(outs)
```

```python
import jax
import jax.numpy as jnp
import numpy as np
from jax import lax
from jax.experimental import pallas as pl
from jax.experimental.pallas import tpu as pltpu

F32 = jnp.float32
MXU = jnp.bfloat16

N_DEV = 8
HEADS = 8
PAIRS = HEADS // 2
HEAD_DIM = 64
NOPE = 64
ROPE = 32
HALF_ROPE = ROPE // 2
Q_LORA = 256
KV_LORA = 128
CHUNK = 64
GROUP_W = HEADS * HEAD_DIM
EPS = 1e-6
ROPE_THETA = 10000.0

Z_FQ, Z_FK, Z_FV, Z_FG, Z_MG, Z_QL, Z_KV, Z_MISC, Z_W = 0, 512, 1024, 1536, 2048, 2560, 2816, 2944, 3072
MISC_FF = ROPE
TAIL_W = Z_W - Z_QL
T_KV, T_MISC = Q_LORA, Q_LORA + KV_LORA

ADAM_LR = 0.001
ADAM_B1 = 0.9
ADAM_B2 = 0.999
ADAM_EPS = 1e-08
ADAM_WD = 0.01
ADAM_STEP = 10

VMEM_LIMIT_V7X = 56 * 1024 * 1024
LANES = 128
ATTN_TILE = 1024
V_ROWS = HEAD_DIM + 16
LOG2E = 1.4426950408889634
FOX_SCALE = HEAD_DIM ** -0.5
MLA_SCALE = (NOPE + ROPE) ** -0.5

_NT = (((1,), (1,)), ((), ()))
_TN = (((0,), (0,)), ((), ()))


def _params(*sem, side_effects=False):
    return pltpu.CompilerParams(dimension_semantics=sem, vmem_limit_bytes=VMEM_LIMIT_V7X,
                                has_side_effects=side_effects)


def _sds(shape, dtype=F32):
    return jax.ShapeDtypeStruct(shape, dtype)


def _full(shape):
    nd = len(shape)
    return pl.BlockSpec(shape, lambda *_: (0,) * nd)


def _rows(tm, width, col=0):
    return pl.BlockSpec((tm, width), lambda i: (i, col))


def _exchange(arrs, gather, name):
    n = len(arrs)

    def kern(*refs):
        copies = _exchange_copies(refs[:n], refs[n:2 * n], gather, *refs[2 * n:])
        _exchange_start(copies)
        _exchange_wait(copies)

    return pl.pallas_call(
        kern, name=name, out_shape=_exchange_out_shapes(arrs, gather),
        in_specs=[pl.BlockSpec(memory_space=pl.ANY)] * n,
        out_specs=[pl.BlockSpec(memory_space=pl.ANY)] * n,
        scratch_shapes=_exchange_sems(n),
        compiler_params=pltpu.CompilerParams(has_side_effects=True),
    )(*arrs)


def _gather_two_level(arrs, name):
    n = len(arrs)

    def kern(*refs):
        ins, outs = refs[:n], refs[n:2 * n]
        send_sems, recv_sems, loc_sems = refs[2 * n:]
        x, y, c = lax.axis_index("x"), lax.axis_index("y"), lax.axis_index("c")
        me, sibling = (x, y, c), (x, y, 1 - c)
        chips = [(1 - x, y), (x, 1 - y), (1 - x, 1 - y)]

        def slot(i, dev):
            return outs[i].at[4 * dev[0] + 2 * dev[1] + dev[2]]

        def copy(i, k, block, to, src=None):
            return pltpu.make_async_remote_copy(
                src_ref=slot(i, block) if src is None else src, dst_ref=slot(i, block), send_sem=send_sems.at[i, k],
                recv_sem=recv_sems.at[i, k], device_id=to, device_id_type=pl.DeviceIdType.MESH)

        mine = [pltpu.make_async_copy(ins[i], slot(i, me), loc_sems.at[i]) for i in range(n)]
        first = [copy(i, 0, me, sibling, src=ins[i]) for i in range(n)]
        first += [copy(i, 1 + j, me, (*chip, c), src=ins[i]) for j, chip in enumerate(chips) for i in range(n)]
        for cp in mine + first:
            cp.start()
        passed = []
        for j, chip in enumerate(chips):
            for i in range(n):
                copy(i, 1 + j, (*chip, c), me).wait_recv()
                fwd = copy(i, 4 + j, (*chip, c), sibling)
                fwd.start()
                passed.append(fwd)
        for i in range(n):
            copy(i, 0, sibling, me).wait_recv()
            for j, chip in enumerate(chips):
                copy(i, 4 + j, (*chip, 1 - c), me).wait_recv()
        for cp in first + passed:
            cp.wait_send()
        for cp in mine:
            cp.wait()

    return pl.pallas_call(
        kern, name=name, out_shape=_exchange_out_shapes(arrs, [True] * n),
        in_specs=[pl.BlockSpec(memory_space=pl.ANY)] * n,
        out_specs=[pl.BlockSpec(memory_space=pl.ANY)] * n,
        scratch_shapes=_exchange_sems(n),
        compiler_params=pltpu.CompilerParams(has_side_effects=True),
    )(*arrs)


def _exchange_out_shapes(arrs, gather):
    return [_sds((N_DEV,) + tuple(a.shape) if g else tuple(a.shape), a.dtype) for a, g in zip(arrs, gather)]


def _exchange_sems(n):
    return [pltpu.SemaphoreType.DMA((n, N_DEV)), pltpu.SemaphoreType.DMA((n, N_DEV)), pltpu.SemaphoreType.DMA((n,))]


def _exchange_copies(ins, outs, gather, send_sems, recv_sems, loc_sems, recv=True):
    n = len(ins)
    x, y, c = lax.axis_index("x"), lax.axis_index("y"), lax.axis_index("c")
    me = 4 * x + 2 * y + c

    def src(i, j):
        return ins[i] if gather[i] else ins[i].at[j]

    local = [pltpu.make_async_copy(src(i, me), outs[i].at[me], loc_sems.at[i]) for i in range(n)]
    sends, recvs = [], []
    for k in range(1, N_DEV):
        px = 1 - x if k & 4 else x
        py = 1 - y if k & 2 else y
        pc = 1 - c if k & 1 else c
        p = 4 * px + 2 * py + pc
        for i in range(n):
            sends.append(pltpu.make_async_remote_copy(
                src_ref=src(i, p), dst_ref=outs[i].at[me], send_sem=send_sems.at[i, k],
                recv_sem=recv_sems.at[i, k], device_id=(px, py, pc), device_id_type=pl.DeviceIdType.MESH))
            if recv:
                recvs.append(pltpu.make_async_remote_copy(
                    src_ref=src(i, p), dst_ref=outs[i].at[p], send_sem=send_sems.at[i, k],
                    recv_sem=recv_sems.at[i, k], device_id=(px, py, pc), device_id_type=pl.DeviceIdType.MESH))
    return local, sends, recvs


def _exchange_start(copies):
    local, sends, _ = copies
    for cp in local + sends:
        cp.start()


def _exchange_wait(copies):
    local, sends, recvs = copies
    for cp in recvs:
        cp.wait_recv()
    for cp in sends:
        cp.wait_send()
    for cp in local:
        cp.wait()


def _modpart(cact8, w_ada, b_cols):
    n_l, d, cw = w_ada.shape

    def kern(c_ref, w_ref, b_ref, o_ref):
        o_ref[0] = jnp.dot(c_ref[...].astype(MXU), w_ref[0].astype(MXU), preferred_element_type=F32) + b_ref[0]

    return pl.pallas_call(
        kern, name="modpart", grid=(n_l,), out_shape=_sds((n_l, N_DEV, cw)),
        in_specs=[_full((N_DEV, d)), pl.BlockSpec((1, d, cw), lambda l: (l, 0, 0)),
                  pl.BlockSpec((1, 1, cw), lambda l: (l, 0, 0))],
        out_specs=pl.BlockSpec((1, N_DEV, cw), lambda l: (l, 0, 0)),
        compiler_params=_params("arbitrary"),
    )(cact8, w_ada, b_cols)


def _ada_grad(cact_cols, dmod_cols):
    n_l, _, cw = dmod_cols.shape
    d = cact_cols.shape[1]

    def kern(c_ref, dm_ref, o_ref):
        acc = c_ref[0] * dm_ref[0, 0:1, :]
        for s in range(1, N_DEV):
            acc = acc + c_ref[s] * dm_ref[0, s:s + 1, :]
        o_ref[0] = acc

    return pl.pallas_call(
        kern, name="ada_grad", grid=(n_l,), out_shape=_sds((n_l, d, cw)),
        in_specs=[_full((N_DEV, d, 1)), pl.BlockSpec((1, N_DEV, cw), lambda l: (l, 0, 0))],
        out_specs=pl.BlockSpec((1, d, cw), lambda l: (l, 0, 0)),
        compiler_params=_params("arbitrary"),
    )(cact_cols, dmod_cols)


def _k_in(x, g, mod3, w, tm=256):
    s_len, d = x.shape
    qkv_w = 3 * GROUP_W

    def kern(x_ref, g_ref, mod_ref, w_ref, h_ref, qkv_ref, gates_ref, tail_ref):
        xv = x_ref[...]
        r = lax.rsqrt(jnp.mean(xv * xv, axis=-1, keepdims=True) + EPS)
        xn = xv * r * g_ref[...]
        h = (xn * (1.0 + mod_ref[1:2, :]) + mod_ref[0:1, :]).astype(MXU)
        h_ref[...] = h
        z = jnp.dot(h, w_ref[...], preferred_element_type=F32)
        qkv_ref[:, :GROUP_W] = (z[:, Z_FQ:Z_FQ + GROUP_W] * (FOX_SCALE * LOG2E)).astype(MXU)
        qkv_ref[:, GROUP_W:] = z[:, Z_FK:Z_FK + 2 * GROUP_W].astype(MXU)
        gates_ref[...] = z[:, Z_FG:Z_QL]
        tail_ref[...] = z[:, Z_QL:]

    return pl.pallas_call(
        kern, name="k_in", grid=(s_len // tm,),
        out_shape=[_sds((s_len, d), MXU), _sds((s_len, qkv_w), MXU), _sds((s_len, Z_QL - Z_FG)),
                   _sds((s_len, TAIL_W))],
        in_specs=[_rows(tm, d), _full((1, d)), _full((3, d)), _full((d, Z_W))],
        out_specs=[_rows(tm, d), _rows(tm, qkv_w), _rows(tm, Z_QL - Z_FG), _rows(tm, TAIL_W)],
        compiler_params=_params("arbitrary"),
    )(x, g, mod3, w)


def _scan_matrices(rows, chunks, reverse):
    r_i = lax.broadcasted_iota(jnp.int32, (LANES, LANES), 0)
    c_i = lax.broadcasted_iota(jnp.int32, (LANES, LANES), 1)
    a_i = lax.broadcasted_iota(jnp.int32, (rows, rows), 0)
    b_i = lax.broadcasted_iota(jnp.int32, (rows, rows), 1)
    same_head = (a_i // chunks) == (b_i // chunks)
    if reverse:
        return (r_i >= c_i).astype(F32), (same_head & (b_i > a_i)).astype(F32)
    return (r_i <= c_i).astype(F32), (same_head & (b_i < a_i)).astype(F32)


def _scan_rows(x, inner, outer):
    tot = jnp.broadcast_to(jnp.sum(x, axis=1, keepdims=True), x.shape)
    return (jnp.dot(x, inner, precision=lax.Precision.HIGHEST, preferred_element_type=F32)
            + jnp.dot(outer, tot, precision=lax.Precision.HIGHEST, preferred_element_type=F32))


def _k_cum(ff_rows, b_rows, chunks):
    rows = ff_rows.shape[0]

    def kern(ff_ref, b_ref, cum_ref):
        xc = ff_ref[...] + b_ref[...]
        lf = jnp.minimum(xc, 0.0) - jnp.log(1.0 + jnp.exp(-jnp.abs(xc)))
        cum_ref[...] = _scan_rows(lf, *_scan_matrices(rows, chunks, False))

    return pl.pallas_call(
        kern, name="k_cum", out_shape=_sds((rows, LANES)),
        in_specs=[pl.BlockSpec(memory_space=pltpu.VMEM)] * 2,
        out_specs=pl.BlockSpec(memory_space=pltpu.VMEM),
        compiler_params=_params(),
    )(ff_rows, b_rows)


def _k_cum_bwd(dc_rows, ff_rows, b_rows, chunks):
    rows = ff_rows.shape[0]

    def kern(dc_ref, ff_ref, b_ref, dff_ref, db_ref):
        dlf = _scan_rows(dc_ref[...], *_scan_matrices(rows, chunks, True))
        dff = dlf * jax.nn.sigmoid(-(ff_ref[...] + b_ref[...]))
        dff_ref[...] = dff
        db_ref[...] = jnp.broadcast_to(jnp.sum(dff, axis=1, keepdims=True), dff.shape)

    return pl.pallas_call(
        kern, name="k_cum_bwd", out_shape=[_sds((rows, LANES)), _sds((rows, LANES))],
        in_specs=[pl.BlockSpec(memory_space=pltpu.VMEM)] * 3,
        out_specs=[pl.BlockSpec(memory_space=pltpu.VMEM)] * 2,
        compiler_params=_params(),
    )(dc_rows, ff_rows, b_rows)


def _swap16(t):
    lane = lax.broadcasted_iota(jnp.int32, t.shape, 1)
    return jnp.where(lane % ROPE < HALF_ROPE, pltpu.roll(t, LANES - HALF_ROPE, 1), pltpu.roll(t, HALF_ROPE, 1))


def _rope(t, cos, sin):
    return t * cos + _swap16(t) * sin


def _rope_bwd(dt, cos, sin):
    return dt * cos - _swap16(dt) * sin


def _k_prep(tail, cos, sin, gq, gkv, wuq, wukv, tm=512):
    s_len = tail.shape[0]
    qc = MLA_SCALE * LOG2E

    def kern(tail_ref, cos_ref, sin_ref, gq_ref, gkv_ref, wuq_ref, wukv_ref,
             qn_out, qr_out, kn_out, v_out, kr_out, qn_ref, kvn_ref):
        cs, sn = cos_ref[...], sin_ref[...]
        ql = tail_ref[:, :T_KV]
        rq = lax.rsqrt(jnp.mean(ql * ql, axis=-1, keepdims=True) + EPS)
        qn = (ql * rq * gq_ref[...]).astype(MXU)
        qn_ref[...] = qn
        q = jnp.dot(qn, wuq_ref[...], preferred_element_type=F32)
        qn_out[...] = (q[:, :GROUP_W] * qc).astype(MXU)
        for blk in range(PAIRS):
            lo = GROUP_W + blk * LANES
            qr_out[:, blk * LANES:(blk + 1) * LANES] = (_rope(q[:, lo:lo + LANES], cs, sn) * qc).astype(MXU)
        kvl = tail_ref[:, T_KV:T_MISC]
        rk = lax.rsqrt(jnp.mean(kvl * kvl, axis=-1, keepdims=True) + EPS)
        kvn = (kvl * rk * gkv_ref[...]).astype(MXU)
        kvn_ref[...] = kvn
        kv = jnp.dot(kvn, wukv_ref[...], preferred_element_type=F32)
        kn_out[...] = kv[:, :GROUP_W].astype(MXU)
        v_out[...] = kv[:, GROUP_W:].astype(MXU)
        misc = tail_ref[:, T_MISC:]
        lane = lax.broadcasted_iota(jnp.int32, misc.shape, 1)
        kr = jnp.where(lane < ROPE, _rope(misc, cs, sn), 0.0)
        kr_out[...] = (kr + pltpu.roll(kr, HEAD_DIM, 1)).astype(MXU)

    return pl.pallas_call(
        kern, name="k_prep", grid=(s_len // tm,),
        out_shape=[_sds((s_len, GROUP_W), MXU), _sds((s_len, GROUP_W), MXU), _sds((s_len, GROUP_W), MXU),
                   _sds((s_len, GROUP_W), MXU), _sds((s_len, LANES), MXU), _sds((s_len, Q_LORA), MXU),
                   _sds((s_len, KV_LORA), MXU)],
        in_specs=[_rows(tm, TAIL_W), _rows(tm, LANES), _rows(tm, LANES),
                  _full((1, Q_LORA)), _full((1, KV_LORA)), _full((Q_LORA, 2 * GROUP_W)),
                  _full((KV_LORA, 2 * GROUP_W))],
        out_specs=[_rows(tm, GROUP_W), _rows(tm, GROUP_W), _rows(tm, GROUP_W), _rows(tm, GROUP_W), _rows(tm, LANES),
                   _rows(tm, Q_LORA), _rows(tm, KV_LORA)],
        compiler_params=_params("arbitrary"),
    )(tail, cos, sin, gq, gkv, wuq, wukv)


def _block_mask(kn, qn, q_off, chunk_mask, transposed):
    shape = (kn, qn) if transposed else (qn, kn)
    row = lax.broadcasted_iota(jnp.int32, shape, 0)
    col = lax.broadcasted_iota(jnp.int32, shape, 1)
    qi, ki = (col + q_off, row) if transposed else (row + q_off, col)
    if chunk_mask:
        return (ki // CHUNK) <= (qi // CHUNK)
    return ki <= qi


def _head_operand(x, hh, other=None):
    lane = lax.broadcasted_iota(jnp.int32, x.shape, 1)
    own = (lane >= hh * HEAD_DIM) & (lane < (hh + 1) * HEAD_DIM)
    return jnp.where(own, x, jnp.zeros_like(x) if other is None else other)


def _attention_fwd(q, q_blk, k, k_blk, v, v_blk, bias, rope, chunk_mask, name, side=None):
    s_len = q.shape[0]
    t = min(ATTN_TILE, s_len // 2)
    nq = s_len // t
    n_side = len(side[0]) if side else 0

    def kern(*refs):
        q_ref, k_ref, v_ref = refs[:3]
        pos = 3
        if bias is not None:
            ck_ref = refs[pos]
            pos += 1
        if rope is not None:
            qr_ref, kr_ref = refs[pos:pos + 2]
            pos += 2
        side_in = refs[pos:pos + n_side]
        pos += n_side
        o_ref, lse_ref = refs[pos:pos + 2]
        side_out = refs[pos + 2:pos + 2 + n_side]
        vt_scr, m_scr, acc_scr, ck_scr = refs[pos + 2 + n_side:pos + 6 + n_side]
        sems = refs[pos + 6 + n_side:]
        pj = pl.program_id(0)
        if n_side:
            @pl.when(pj == 0)
            def _():
                _exchange_start(_exchange_copies(side_in, side_out, side[1], *sems, recv=False))
        vt_scr[:, HEAD_DIM:, :] = jnp.ones((2, V_ROWS - HEAD_DIM, s_len), vt_scr.dtype)
        for i in range(nq):
            vtt = v_ref[i * t:(i + 1) * t, :].T
            for hh in range(2):
                vt_scr[hh, :HEAD_DIM, i * t:(i + 1) * t] = vtt[hh * HEAD_DIM:(hh + 1) * HEAD_DIM, :]
                if bias is not None:
                    ckt = ck_ref[i * t:(i + 1) * t, :]
                    lane = lax.broadcasted_iota(jnp.int32, ckt.shape, 1)
                    ck_scr[hh, i * t:(i + 1) * t, :] = jnp.sum(jnp.where(lane == 2 * pj + hh, ckt, 0.0), axis=1,
                                                               keepdims=True)

        def qbody(qi, _):
            qs = pl.multiple_of(qi * t, t)
            qt = q_ref[pl.ds(qs, t), :]
            qrt = qr_ref[pl.ds(qs, t), :] if rope is not None else None
            qh = [_head_operand(qt, hh, qrt) for hh in range(2)]
            m_scr[...] = jnp.full(m_scr.shape, -jnp.inf, F32)
            acc_scr[...] = jnp.zeros(acc_scr.shape, F32)

            def block(ks, kn, q0, qn, masked):
                kt = k_ref[pl.ds(ks, kn), :]
                kh = [_head_operand(kt, hh, kr_ref[pl.ds(ks, kn), :]) for hh in range(2)] if rope is not None else [kt, kt]
                qc = slice(q0, q0 + qn)
                sts = [lax.dot_general(kh[hh], qh[hh][qc], _NT, preferred_element_type=F32) for hh in range(2)]
                if bias is not None:
                    sts = [sts[hh] - ck_scr[hh, pl.ds(ks, kn), :] for hh in range(2)]
                if masked:
                    sts = [jnp.where(_block_mask(kn, qn, q0, chunk_mask, True), st, -jnp.inf) for st in sts]
                m_old = [m_scr[hh, :, qc] for hh in range(2)]
                m_new = [jnp.maximum(m_old[hh], jnp.max(sts[hh], axis=0, keepdims=True)) for hh in range(2)]
                pts = [jnp.exp2(sts[hh] - m_new[hh]).astype(MXU) for hh in range(2)]
                for hh in range(2):
                    alpha = jnp.exp2(m_old[hh] - m_new[hh])
                    acc_scr[hh, :, qc] = alpha * acc_scr[hh, :, qc] + jnp.dot(vt_scr[hh, :, pl.ds(ks, kn)], pts[hh],
                                                                            preferred_element_type=F32)
                    m_scr[hh, :, qc] = m_new[hh]

            def loop_body(ki, carry):
                block(pl.multiple_of(ki * t, t), t, 0, t, False)
                return carry

            lax.fori_loop(0, qi, loop_body, 0)
            block(qs, t, 0, t, True)
            outs = []
            for hh in range(2):
                acc = acc_scr[hh]
                l = acc[HEAD_DIM:HEAD_DIM + 1, :]
                outs.append(acc[:HEAD_DIM, :] / l)
                lse_ref[hh, :, pl.ds(qs, t)] = m_scr[hh] + jnp.log2(l)
            o_ref[pl.ds(qs, t), :] = jnp.concatenate(outs, axis=0).T
            return 0

        lax.fori_loop(0, nq, qbody, 0)
        if n_side:
            @pl.when(pj == PAIRS - 1)
            def _():
                _exchange_wait(_exchange_copies(side_in, side_out, side[1], *sems))

    def tok(blk):
        return pl.BlockSpec((s_len, LANES), lambda j: (0, blk + j))

    rowb = pl.BlockSpec((2, 1, s_len), lambda j: (j, 0, 0))
    hbm = pl.BlockSpec(memory_space=pl.ANY)
    ins = [q, k, v]
    in_specs = [tok(q_blk), tok(k_blk), tok(v_blk)]
    if bias is not None:
        ins.append(bias)
        in_specs.append(_full((s_len, LANES)))
    if rope is not None:
        ins += list(rope)
        in_specs += [tok(0), _full((s_len, LANES))]
    out_shape = [_sds((s_len, PAIRS * LANES)), _sds((HEADS, 1, s_len))]
    scratch = [pltpu.VMEM((2, V_ROWS, s_len), v.dtype), pltpu.VMEM((2, 1, t), F32), pltpu.VMEM((2, V_ROWS, t), F32),
               pltpu.VMEM((2, s_len if bias is not None else 8, 1), F32)]
    if n_side:
        ins += list(side[0])
        out_shape += _exchange_out_shapes(*side)
        scratch += _exchange_sems(n_side)
    return pl.pallas_call(
        kern, name=name, grid=(PAIRS,), out_shape=out_shape,
        in_specs=in_specs + [hbm] * n_side, out_specs=[tok(0), rowb] + [hbm] * n_side,
        scratch_shapes=scratch,
        compiler_params=_params("arbitrary", side_effects=bool(n_side)),
    )(*ins)


def _attention_bwd(q, q_blk, k, k_blk, v, v_blk, do, pack, ck_row, rope, chunk_mask, q_scale, k_scale, name,
                   side=None):
    s_len = q.shape[0]
    t = min(ATTN_TILE, s_len // 2)
    nq = s_len // t
    has_bias = ck_row is not None
    nv = 2 if rope is not None else 1
    n_side = len(side[0]) if side else 0

    def kern(*refs):
        q_ref, k_ref, v_ref, do_ref, pack_ref = refs[:5]
        pos = 5
        if has_bias:
            ck_ref = refs[pos]
            pos += 1
        if rope is not None:
            qr_ref, kr_ref = refs[pos:pos + 2]
            pos += 2
        side_in = refs[pos:pos + n_side]
        pos += n_side
        dq_ref, dk_ref, dv_ref = refs[pos:pos + 3]
        pos += 3
        if has_bias:
            dcq_ref, dck_ref = refs[pos:pos + 2]
            pos += 2
        if rope is not None:
            dqr_ref, dkr_ref = refs[pos:pos + 2]
            pos += 2
        side_out = refs[pos:pos + n_side]
        pos += n_side
        qt_scr, dot_scr, dkt_scr, dvt_scr, dq_scr, dcq_scr = refs[pos:pos + 6]
        sems = refs[pos + 6:]
        pj = pl.program_id(0)
        if n_side:
            @pl.when(pj == 0)
            def _():
                _exchange_start(_exchange_copies(side_in, side_out, side[1], *sems, recv=False))

        for i in range(nq):
            sl = slice(i * t, (i + 1) * t)
            dot_scr[:, sl] = do_ref[sl, :].T
            if rope is not None:
                for hh in range(2):
                    qt_scr[hh, :, sl] = _head_operand(q_ref[sl, :], hh, qr_ref[sl, :]).T
            else:
                qt_scr[0, :, sl] = q_ref[sl, :].T
        dkt_scr[...] = jnp.zeros(dkt_scr.shape, F32)
        dvt_scr[...] = jnp.zeros(dvt_scr.shape, F32)
        if has_bias:
            dck_ref[...] = jnp.zeros(dck_ref.shape, F32)

            @pl.when(pj == 0)
            def _():
                dcq_ref[...] = jnp.zeros(dcq_ref.shape, F32)

        def qbody(qi, _):
            qs = pl.multiple_of(qi * t, t)
            qt = q_ref[pl.ds(qs, t), :]
            qrt = qr_ref[pl.ds(qs, t), :] if rope is not None else None
            dot = do_ref[pl.ds(qs, t), :]
            pk = pack_ref[pl.ds(qs, t), :]
            lane = lax.broadcasted_iota(jnp.int32, pk.shape, 1)
            qh = [_head_operand(qt, hh, qrt) for hh in range(2)]
            doh = [_head_operand(dot, hh) for hh in range(2)]
            a_col = [jnp.sum(jnp.where(lane == 2 * pj + hh, pk, 0.0), axis=1, keepdims=True) for hh in range(2)]
            d_col = [jnp.sum(jnp.where(lane == HEADS + 2 * pj + hh, pk, 0.0), axis=1, keepdims=True)
                     for hh in range(2)]
            dq_scr[...] = jnp.zeros(dq_scr.shape, F32)
            if has_bias:
                dcq_scr[...] = jnp.zeros(dcq_scr.shape, F32)

            def block(ks, kn, q0, qn, masked):
                kt = k_ref[pl.ds(ks, kn), :]
                krt = kr_ref[pl.ds(ks, kn), :] if rope is not None else None
                kh = [_head_operand(kt, hh, krt) for hh in range(2)]
                vt = v_ref[pl.ds(ks, kn), :]
                qr_ = slice(q0, q0 + qn)
                qcols = pl.ds(pl.multiple_of(qs + q0, t // 2), qn)
                ss = [lax.dot_general(qh[hh][qr_], kh[hh] if rope is not None else kt, _NT,
                                      preferred_element_type=F32) + a_col[hh][qr_] for hh in range(2)]
                if has_bias:
                    ss = [ss[hh] - ck_ref[hh, :, pl.ds(ks, kn)] for hh in range(2)]
                dpds = [lax.dot_general(doh[hh][qr_], vt, _NT, preferred_element_type=F32) for hh in range(2)]
                ps = [jnp.exp2(s) for s in ss]
                if masked:
                    ps = [jnp.where(_block_mask(kn, qn, q0, chunk_mask, False), p, 0.0) for p in ps]
                dss = [ps[hh] * (dpds[hh] - d_col[hh][qr_]) for hh in range(2)]
                for hh in range(2):
                    rows = slice(hh * HEAD_DIM, (hh + 1) * HEAD_DIM)
                    dsb = dss[hh].astype(MXU)
                    dvt_scr[rows, pl.ds(ks, kn)] += jnp.dot(dot_scr[rows, qcols], ps[hh].astype(MXU),
                                                            preferred_element_type=F32)
                    if rope is not None:
                        dkt_scr[hh, :, pl.ds(ks, kn)] += jnp.dot(qt_scr[hh, :, qcols], dsb,
                                                                 preferred_element_type=F32)
                    else:
                        dkt_scr[0, rows, pl.ds(ks, kn)] += jnp.dot(qt_scr[0, rows, qcols], dsb,
                                                                   preferred_element_type=F32)
                    dq_scr[hh if rope is not None else 0, qr_, :] += jnp.dot(dsb, kh[hh], preferred_element_type=F32)
                    if has_bias:
                        dcq_scr[hh, qr_, :] += jnp.sum(dss[hh], axis=1, keepdims=True)
                        dck_ref[hh, :, pl.ds(ks, kn)] += -jnp.sum(dss[hh], axis=0, keepdims=True)

            def loop_body(ki, carry):
                block(pl.multiple_of(ki * t, t), t, 0, t, False)
                return carry

            lax.fori_loop(0, qi, loop_body, 0)
            block(qs, t // 2, 0, t // 2, True)
            block(qs, t, t // 2, t // 2, True)
            if rope is not None:
                first = lane < HEAD_DIM
                dq_ref[pl.ds(qs, t), :] = (jnp.where(first, dq_scr[0], dq_scr[1]) * q_scale).astype(dq_ref.dtype)
                dqr_ref[pl.ds(qs, t), :] = jnp.where(first, dq_scr[1], dq_scr[0]) * q_scale
            else:
                dq_ref[pl.ds(qs, t), :] = (dq_scr[0] * q_scale).astype(dq_ref.dtype)
            if has_bias:
                old = dcq_ref[pl.ds(qs, t), :]
                dcq_ref[pl.ds(qs, t), :] = jnp.where(lane == 2 * pj, dcq_scr[0],
                                                     jnp.where(lane == 2 * pj + 1, dcq_scr[1], old))
            return 0

        lax.fori_loop(0, nq, qbody, 0)
        for i in range(nq):
            sl = slice(i * t, (i + 1) * t)
            dv_ref[sl, :] = dvt_scr[:, sl].T.astype(dv_ref.dtype)
            if rope is not None:
                d0, d1 = dkt_scr[0, :, sl], dkt_scr[1, :, sl]
                first = lax.broadcasted_iota(jnp.int32, d0.shape, 0) < HEAD_DIM
                dk_ref[sl, :] = (jnp.where(first, d0, d1).T * k_scale).astype(dk_ref.dtype)
                dkr_ref[0, sl, :] = jnp.where(first, d1, d0).T * k_scale
            else:
                dk_ref[sl, :] = (dkt_scr[0, :, sl].T * k_scale).astype(dk_ref.dtype)
        if n_side:
            @pl.when(pj == PAIRS - 1)
            def _():
                _exchange_wait(_exchange_copies(side_in, side_out, side[1], *sems))

    def tok(blk):
        return pl.BlockSpec((s_len, LANES), lambda j: (0, blk + j))

    shared = _full((s_len, LANES))
    rowb = pl.BlockSpec((2, 1, s_len), lambda j: (j, 0, 0))
    slab = pl.BlockSpec((1, s_len, LANES), lambda j: (j, 0, 0))
    hbm = pl.BlockSpec(memory_space=pl.ANY)
    ins = [q, k, v, do, pack]
    in_specs = [tok(q_blk), tok(k_blk), tok(v_blk), tok(0), shared]
    out_shape = [_sds((s_len, PAIRS * LANES), MXU)] * 3
    out_specs = [tok(0)] * 3
    if has_bias:
        ins.append(ck_row)
        in_specs.append(rowb)
        out_shape += [_sds((s_len, LANES)), _sds((HEADS, 1, s_len))]
        out_specs += [shared, rowb]
    if rope is not None:
        ins += list(rope)
        in_specs += [tok(0), shared]
        out_shape += [_sds((s_len, PAIRS * LANES)), _sds((PAIRS, s_len, LANES))]
        out_specs += [tok(0), slab]
    scratch = [pltpu.VMEM((nv, LANES, s_len), q.dtype), pltpu.VMEM((LANES, s_len), do.dtype),
               pltpu.VMEM((nv, LANES, s_len), F32), pltpu.VMEM((LANES, s_len), F32),
               pltpu.VMEM((nv, t, LANES), F32), pltpu.VMEM((2, t, 1), F32)]
    if n_side:
        ins += list(side[0])
        out_shape += _exchange_out_shapes(*side)
        scratch += _exchange_sems(n_side)
    return pl.pallas_call(
        kern, name=name, grid=(PAIRS,), out_shape=out_shape,
        in_specs=in_specs + [hbm] * n_side, out_specs=out_specs + [hbm] * n_side, scratch_shapes=scratch,
        compiler_params=_params("arbitrary", side_effects=bool(n_side)),
    )(*ins)


def _silu(a):
    return a * jax.nn.sigmoid(a)


def _k_out(of, om, gates, x, gate, wout, tm=256):
    s_len, d = x.shape

    def kern(of_ref, om_ref, gates_ref, x_ref, gate_ref, w_ref, xo_ref, y_ref, u_ref):
        u_ref[:, :GROUP_W] = (of_ref[...] * _silu(gates_ref[:, :GROUP_W])).astype(MXU)
        u_ref[:, GROUP_W:] = (om_ref[...] * _silu(gates_ref[:, GROUP_W:])).astype(MXU)
        y = jnp.dot(u_ref[...], w_ref[...], preferred_element_type=F32)
        y_ref[...] = y
        xo_ref[...] = x_ref[...] + gate_ref[...] * y

    return pl.pallas_call(
        kern, name="k_out", grid=(s_len // tm,),
        out_shape=[_sds((s_len, d)), _sds((s_len, d)), _sds((s_len, 2 * GROUP_W), MXU)],
        in_specs=[_rows(tm, GROUP_W), _rows(tm, GROUP_W), _rows(tm, 2 * GROUP_W), _rows(tm, d), _full((1, d)),
                  _full((2 * GROUP_W, d))],
        out_specs=[_rows(tm, d), _rows(tm, d), _rows(tm, 2 * GROUP_W)],
        compiler_params=_params("arbitrary"),
    )(of, om, gates, x, gate, wout)


def _k_loss(x, gf, tgt, tm=256):
    s_len, d = x.shape

    def kern(x_ref, g_ref, t_ref, loss_ref, dx_ref, dg_ref):
        i = pl.program_id(0)
        xv = x_ref[...]
        r = lax.rsqrt(jnp.mean(xv * xv, axis=-1, keepdims=True) + EPS)
        xh = xv * r
        diff = xh * g_ref[...] - t_ref[...]
        part = 0.5 * jnp.sum(jnp.mean(diff * diff, axis=-1, keepdims=True))
        dout = diff * (1.0 / d)
        dxh = dout * g_ref[...]
        dx_ref[...] = r * (dxh - xh * jnp.mean(dxh * xh, axis=-1, keepdims=True))

        @pl.when(i == 0)
        def _():
            loss_ref[...] = jnp.zeros_like(loss_ref)
            dg_ref[...] = jnp.zeros_like(dg_ref)

        loss_ref[...] += jnp.full(loss_ref.shape, part, F32)
        dg_ref[...] += jnp.sum(dout * xh, axis=0, keepdims=True)

    return pl.pallas_call(
        kern, name="k_loss", grid=(s_len // tm,),
        out_shape=[_sds((1, LANES)), _sds((s_len, d)), _sds((1, d))],
        in_specs=[_rows(tm, d), _full((1, d)), _rows(tm, d)],
        out_specs=[_full((1, LANES)), _rows(tm, d), _full((1, d))],
        compiler_params=_params("arbitrary"),
    )(x, gf, tgt)


def _kb_out(dxo, y, u, gate, wout, of, om, gates, dw_dtype, tm=256):
    s_len, d = dxo.shape
    steps = s_len // tm

    def kern(dxo_ref, y_ref, u_ref, gate_ref, wt_ref, of_ref, om_ref, gates_ref,
             dof_ref, dom_ref, dfg_ref, dmg_ref, dlf_ref, dlm_ref, dgate_ref, dw_ref, dw_acc):
        i = pl.program_id(0)
        dxv = dxo_ref[...]

        @pl.when(i == 0)
        def _():
            dgate_ref[...] = jnp.zeros_like(dgate_ref)
            dw_acc[...] = jnp.zeros_like(dw_acc)

        dgate_ref[...] += jnp.sum(dxv * y_ref[...], axis=0, keepdims=True)
        dyb = (dxv * gate_ref[...]).astype(MXU)
        dw_acc[...] += lax.dot_general(u_ref[...], dyb, _TN, preferred_element_type=F32)

        @pl.when(i == steps - 1)
        def _():
            dw_ref[...] = dw_acc[...].astype(dw_ref.dtype)

        du = lax.dot_general(dyb, wt_ref[...], _NT, preferred_element_type=F32)
        head_of = (lax.broadcasted_iota(jnp.int32, (GROUP_W, LANES), 0) // HEAD_DIM
                   == lax.broadcasted_iota(jnp.int32, (GROUP_W, LANES), 1)).astype(F32)
        for du_g, o_ref, a, do_ref, dg_ref, dl_ref in (
                (du[:, :GROUP_W], of_ref, gates_ref[:, :GROUP_W], dof_ref, dfg_ref, dlf_ref),
                (du[:, GROUP_W:], om_ref, gates_ref[:, GROUP_W:], dom_ref, dmg_ref, dlm_ref)):
            sg = jax.nn.sigmoid(a)
            ov = o_ref[...]
            dov = du_g * (a * sg)
            do_ref[...] = dov.astype(MXU)
            dg_ref[...] = (du_g * ov * (sg * (1.0 + a * (1.0 - sg)))).astype(MXU)
            dl_ref[...] = jnp.dot(dov * ov, head_of, precision=lax.Precision.HIGH, preferred_element_type=F32)

    return pl.pallas_call(
        kern, name="kb_out", grid=(steps,),
        out_shape=[_sds((s_len, GROUP_W), MXU), _sds((s_len, GROUP_W), MXU),
                   _sds((s_len, GROUP_W), MXU), _sds((s_len, GROUP_W), MXU), _sds((s_len, LANES)),
                   _sds((s_len, LANES)), _sds((1, d)), _sds((2 * GROUP_W, d), dw_dtype)],
        in_specs=[_rows(tm, d), _rows(tm, d), _rows(tm, 2 * GROUP_W), _full((1, d)), _full((2 * GROUP_W, d)),
                  _rows(tm, GROUP_W), _rows(tm, GROUP_W), _rows(tm, 2 * GROUP_W)],
        out_specs=[_rows(tm, GROUP_W), _rows(tm, GROUP_W), _rows(tm, GROUP_W),
                   _rows(tm, GROUP_W), _rows(tm, LANES), _rows(tm, LANES), _full((1, d)), _full((2 * GROUP_W, d))],
        scratch_shapes=[pltpu.VMEM((2 * GROUP_W, d), F32)],
        compiler_params=_params("arbitrary"),
    )(dxo, y, u, gate, wout, of, om, gates)


def _kb_prep(dqn, dqr, dkn, dv, dkr, dff, tail, cos, sin, gq, gkv, wuq_t, wukv_t, tm=512):
    s_len = tail.shape[0]
    qw = 2 * GROUP_W

    def kern(dqn_ref, dqr_ref, dkn_ref, dv_ref, dkr_ref, dff_ref, tail_ref, cos_ref, sin_ref,
             gq_ref, gkv_ref, wuqt_ref, wukvt_ref, dq_ref, dz_ref, dgq_ref, dgkv_ref):
        i = pl.program_id(0)

        @pl.when(i == 0)
        def _():
            dgq_ref[...] = jnp.zeros_like(dgq_ref)
            dgkv_ref[...] = jnp.zeros_like(dgkv_ref)

        cs, sn = cos_ref[...], sin_ref[...]
        dq_ref[:, :GROUP_W] = dqn_ref[...]
        for blk in range(PAIRS):
            sl = slice(blk * LANES, (blk + 1) * LANES)
            dq_ref[:, GROUP_W + blk * LANES:GROUP_W + (blk + 1) * LANES] = _rope_bwd(dqr_ref[:, sl], cs, sn).astype(MXU)
        dqn = lax.dot_general(dq_ref[...], wuqt_ref[...], _NT, preferred_element_type=F32)
        ql = tail_ref[:, :T_KV]
        rq = lax.rsqrt(jnp.mean(ql * ql, axis=-1, keepdims=True) + EPS)
        qh = ql * rq
        dgq_ref[...] += jnp.sum(dqn * qh, axis=0, keepdims=True)
        dqh = dqn * gq_ref[...]
        dz_ref[:, :Q_LORA] = (rq * (dqh - qh * jnp.mean(dqh * qh, axis=-1, keepdims=True))).astype(MXU)

        dkvn = (lax.dot_general(dkn_ref[...], wukvt_ref[:, :GROUP_W], _NT, preferred_element_type=F32)
                + lax.dot_general(dv_ref[...], wukvt_ref[:, GROUP_W:], _NT, preferred_element_type=F32))
        kvl = tail_ref[:, T_KV:T_MISC]
        rk = lax.rsqrt(jnp.mean(kvl * kvl, axis=-1, keepdims=True) + EPS)
        kh = kvl * rk
        dgkv_ref[...] += jnp.sum(dkvn * kh, axis=0, keepdims=True)
        dkh = dkvn * gkv_ref[...]
        dz_ref[:, Q_LORA:Q_LORA + KV_LORA] = (
            rk * (dkh - kh * jnp.mean(dkh * kh, axis=-1, keepdims=True))).astype(MXU)

        g = dkr_ref[...] + pltpu.roll(dkr_ref[...], HEAD_DIM, 1)
        lane = lax.broadcasted_iota(jnp.int32, g.shape, 1)
        dmisc = jnp.where(lane < ROPE, _rope_bwd(g, cs, sn), 0.0) + dff_ref[...]
        dz_ref[:, Q_LORA + KV_LORA:] = dmisc.astype(MXU)

    return pl.pallas_call(
        kern, name="kb_prep", grid=(s_len // tm,),
        out_shape=[_sds((s_len, qw), MXU), _sds((s_len, TAIL_W), MXU), _sds((1, Q_LORA)), _sds((1, KV_LORA))],
        in_specs=[_rows(tm, GROUP_W), _rows(tm, GROUP_W), _rows(tm, GROUP_W), _rows(tm, GROUP_W), _rows(tm, LANES),
                  _rows(tm, LANES), _rows(tm, TAIL_W),
                  _rows(tm, LANES), _rows(tm, LANES), _full((1, Q_LORA)), _full((1, KV_LORA)),
                  _full((Q_LORA, qw)), _full((KV_LORA, 2 * GROUP_W))],
        out_specs=[_rows(tm, qw), _rows(tm, TAIL_W), _full((1, Q_LORA)), _full((1, KV_LORA))],
        compiler_params=_params("arbitrary"),
    )(dqn, dqr, dkn, dv, dkr, dff, tail, cos, sin, gq, gkv, wuq_t, wukv_t)


def _kb_in(dz_pieces, w, x, g, mod3, dxo, tm=256):
    s_len, d = x.shape
    widths = [p.shape[1] for p in dz_pieces]
    n_p = len(widths)

    def kern(*refs):
        dz_refs = refs[:n_p]
        w_ref, x_ref, g_ref, mod_ref, dxo_ref, dx_ref, acc_ref = refs[n_p:]
        i = pl.program_id(0)

        @pl.when(i == 0)
        def _():
            acc_ref[...] = jnp.zeros_like(acc_ref)

        dh = jnp.zeros((tm, d), F32)
        lo = 0
        for p_ref, wd in zip(dz_refs, widths):
            dh = dh + lax.dot_general(p_ref[...], w_ref[:, lo:lo + wd], _NT, preferred_element_type=F32)
            lo += wd
        xv = x_ref[...]
        r = lax.rsqrt(jnp.mean(xv * xv, axis=-1, keepdims=True) + EPS)
        xh = xv * r
        xn = xh * g_ref[...]
        dxn = dh * (1.0 + mod_ref[1:2, :])
        acc_ref[0:1, :] += jnp.sum(dh, axis=0, keepdims=True)
        acc_ref[1:2, :] += jnp.sum(dh * xn, axis=0, keepdims=True)
        acc_ref[2:3, :] += jnp.sum(dxn * xh, axis=0, keepdims=True)
        dxh = dxn * g_ref[...]
        dx_ref[...] = dxo_ref[...] + r * (dxh - xh * jnp.mean(dxh * xh, axis=-1, keepdims=True))

    return pl.pallas_call(
        kern, name="kb_in", grid=(s_len // tm,),
        out_shape=[_sds((s_len, d)), _sds((3, d))],
        in_specs=[_rows(tm, wd) for wd in widths] + [_full((d, Z_W)), _rows(tm, d), _full((1, d)), _full((3, d)),
                                                     _rows(tm, d)],
        out_specs=[_rows(tm, d), _full((3, d))],
        compiler_params=_params("arbitrary"),
    )(*dz_pieces, w, x, g, mod3, dxo)


def _weight_grad(a, pieces, name, out_dtype=F32, tk=512):
    s_len, m = a.shape
    widths = [p.shape[1] for p in pieces]
    n = sum(widths)
    tk = min(tk, s_len)
    steps = s_len // tk

    def kern(a_ref, *refs):
        o_ref, acc_ref = refs[-2:]

        @pl.when(pl.program_id(0) == 0)
        def _():
            acc_ref[...] = jnp.zeros_like(acc_ref)

        at = a_ref[...]
        lo = 0
        for p_ref, w in zip(refs[:-2], widths):
            acc_ref[:, lo:lo + w] += lax.dot_general(at, p_ref[...], _TN, preferred_element_type=F32)
            lo += w

        @pl.when(pl.program_id(0) == steps - 1)
        def _():
            o_ref[...] = acc_ref[...].astype(o_ref.dtype)

    return pl.pallas_call(
        kern, name=name, grid=(steps,), out_shape=_sds((m, n), out_dtype),
        in_specs=[_rows(tk, m)] + [_rows(tk, w) for w in widths],
        out_specs=_full((m, n)),
        scratch_shapes=[pltpu.VMEM((m, n), F32)],
        compiler_params=_params("arbitrary"),
    )(a, *pieces)


def _adamw(slabs, w, m, v, name):
    n_l = len(slabs)
    n, r, c = slabs[0].shape
    tm = r
    for cand in (256, 128, 64, 32, 16, 8):
        if r % cand == 0:
            tm = cand
            break
    steps = r // tm

    def kern(*refs):
        g_refs = refs[:n_l]
        w_ref, m_ref, v_ref, go_ref, d_ref, mo_ref, vo_ref, g_scr = refs[n_l:]
        for ll in range(n_l):
            @pl.when(pl.program_id(0) == ll)
            def _(g_ref=g_refs[ll]):
                g = g_ref[0].astype(F32)
                for s in range(1, n):
                    g = g + g_ref[s].astype(F32)
                g_scr[...] = g

        g = g_scr[...]
        m_new = ADAM_B1 * m_ref[...] + (1.0 - ADAM_B1) * g
        v_new = ADAM_B2 * v_ref[...] + (1.0 - ADAM_B2) * (g * g)
        m_hat = m_new / (1.0 - ADAM_B1 ** ADAM_STEP)
        v_hat = v_new / (1.0 - ADAM_B2 ** ADAM_STEP)
        go_ref[...] = g
        mo_ref[...] = m_new
        vo_ref[...] = v_new
        d_ref[...] = -ADAM_LR * (m_hat / (jnp.sqrt(v_hat) + ADAM_EPS) + ADAM_WD * w_ref[...])

    row = pl.BlockSpec((tm, c), lambda l, i: (l * steps + i, 0))

    def slab_spec(ll):
        return pl.BlockSpec((n, tm, c), lambda l, i: (0, jnp.where(l == ll, i, 0), 0))

    return pl.pallas_call(
        kern, name=name, grid=(n_l, steps), out_shape=[_sds((n_l * r, c))] * 4,
        in_specs=[slab_spec(ll) for ll in range(n_l)] + [row, row, row],
        out_specs=[row] * 4,
        scratch_shapes=[pltpu.VMEM((tm, c), F32)],
        compiler_params=_params("arbitrary", "arbitrary"),
    )(*slabs, w, m, v)


def _perm_w_in(w):
    pad = jnp.zeros(w.shape[:-1] + (Z_W - Z_MISC - ROPE - HEADS,), w.dtype)
    return jnp.concatenate([w[..., 0:1536], w[..., 1544:2056], w[..., 2472:2984], w[..., 2056:2312],
                            w[..., 2312:2440], w[..., 2440:2472], w[..., 1536:1544], pad], axis=-1)


def _unperm_w_in(g):
    ff0 = Z_MISC + MISC_FF
    return jnp.concatenate([g[..., 0:1536], g[..., ff0:ff0 + HEADS], g[..., Z_FG:Z_FG + GROUP_W],
                            g[..., Z_QL:Z_QL + Q_LORA], g[..., Z_KV:Z_KV + KV_LORA],
                            g[..., Z_MISC:Z_MISC + ROPE], g[..., Z_MG:Z_MG + GROUP_W]], axis=-1)


def _perm_w_uq(w):
    lead = w.shape[:-1]
    wh = w.reshape(lead + (PAIRS, 2, NOPE + ROPE))
    zero = jnp.zeros(lead + (PAIRS, HEAD_DIM - ROPE), w.dtype)
    rope = jnp.concatenate([wh[..., 1, NOPE:], zero, wh[..., 0, NOPE:], zero], axis=-1)
    return jnp.concatenate([wh[..., :NOPE].reshape(lead + (GROUP_W,)), rope.reshape(lead + (GROUP_W,))], axis=-1)


def _unperm_w_uq(g):
    lead = g.shape[:-1]
    nope = g[..., :GROUP_W].reshape(lead + (PAIRS, 2, NOPE))
    rp = g[..., GROUP_W:].reshape(lead + (PAIRS, 2, HEAD_DIM))[..., :ROPE]
    return jnp.concatenate([nope, rp[..., ::-1, :]], axis=-1).reshape(lead + (HEADS * (NOPE + ROPE),))


def _perm_w_ukv(w):
    lead = w.shape[:-1]
    wh = w.reshape(lead + (HEADS, 2 * HEAD_DIM))
    return jnp.concatenate([wh[..., :NOPE].reshape(lead + (GROUP_W,)),
                            wh[..., NOPE:].reshape(lead + (GROUP_W,))], axis=-1)


def _unperm_w_ukv(g):
    lead = g.shape[:-1]
    parts = [g[..., :GROUP_W].reshape(lead + (HEADS, NOPE)), g[..., GROUP_W:].reshape(lead + (HEADS, HEAD_DIM))]
    return jnp.concatenate(parts, axis=-1).reshape(lead + (2 * GROUP_W,))


def _rope_tables(positions):
    inv_freq = 1.0 / (ROPE_THETA ** (jnp.arange(0, ROPE, 2, dtype=F32) / ROPE))
    ang = positions.astype(F32)[:, None] * inv_freq
    cos, sin = jnp.cos(ang), jnp.sin(ang)
    reps = LANES // ROPE
    return jnp.tile(jnp.concatenate([cos, cos], axis=1), (1, reps)), jnp.tile(jnp.concatenate([-sin, sin], axis=1), (1, reps))


def _full_weights(g_in, g_uq, g_ukv, g_out):
    def cols(g):
        return g.transpose(1, 0, 2).reshape(g.shape[1], -1)
    return (_perm_w_in(cols(g_in)), _perm_w_uq(cols(g_uq)), _perm_w_ukv(cols(g_ukv)),
            g_out.reshape(-1, g_out.shape[2]))


def _grad_slabs(dw_in, dw_uq, dw_ukv):
    def cols(g):
        return g.reshape(g.shape[0], N_DEV, -1).transpose(1, 0, 2)
    return [cols(_unperm_w_in(dw_in)), cols(_unperm_w_uq(dw_uq)), cols(_unperm_w_ukv(dw_ukv))]


def _local_step(x, mod, positions, loss_target, norm_g, b_f, q_norm_g, kv_norm_g, final_g, weights, shards=None):
    n_l = norm_g.shape[0]
    s_len, d = x.shape
    cos, sin = _rope_tables(positions)
    qb, kb, vb = Z_FQ // LANES, Z_FK // LANES, Z_FV // LANES
    chunks = s_len // LANES
    weights = list(weights)

    def pack_rows(a_rows, delta):
        return jnp.concatenate([a_rows.T, delta[:, :HEADS], jnp.zeros((s_len, LANES - 2 * HEADS), F32)], axis=1)

    saved = []
    for l in range(n_l):
        w_in, w_uq, w_ukv, w_out = weights[l]
        mod3 = mod[l].reshape(3, d)
        h, qkv, gates, tail = _k_in(x, norm_g[l][None], mod3, w_in)
        fft = tail[:, T_MISC + MISC_FF:T_MISC + MISC_FF + HEADS].T.reshape(HEADS * chunks, LANES)
        bf = jnp.repeat(b_f[l], chunks)[:, None]
        c2 = _k_cum(fft, bf, chunks).reshape(HEADS, s_len) * LOG2E
        side = (list(shards[l + 1]), [True] * 4) if shards is not None and l + 1 < n_l else None
        ck_lanes = jnp.pad(c2.T, ((0, 0), (0, LANES - HEADS)))
        of, lse_f, *gathered = _attention_fwd(qkv, qb, qkv, kb, qkv, vb, ck_lanes, None, False,
                                              "fox_fwd_gather" if side else "fox_fwd", side)
        if side:
            weights.append(_full_weights(*gathered))
        mq, mqr, mk, mv, kr2, qn, kvn = _k_prep(tail, cos, sin, q_norm_g[l][None], kv_norm_g[l][None], w_uq, w_ukv)
        om, lse_m = _attention_fwd(mq, 0, mk, 0, mv, 0, None, (mqr, kr2), True, "mla_fwd")
        x_new, y, u = _k_out(of, om, gates, x, mod3[2:3], w_out)
        saved.append((x, gates, tail, h, qkv, fft, bf, c2, lse_f, mq, mqr, mk, mv, kr2, lse_m, of, om, qn, kvn, y, u,
                      mod3))
        x = x_new

    loss_row, dx, dfinal = _k_loss(x, final_g[None], loss_target)

    grads = {k: [] for k in ("norm_g", "mod", "w_in", "b_f", "q_norm_g", "w_uq", "kv_norm_g", "w_ukv", "w_out")}
    received, pending = {}, None
    wg_dtype = MXU if shards is not None else F32
    for l in range(n_l - 1, -1, -1):
        (x_l, gates, tail, h, qkv, fft, bf, c2, lse_f, mq, mqr, mk, mv, kr2, lse_m, of, om, qn, kvn, y, u,
         mod3) = saved[l]
        w_in, w_uq, w_ukv, w_out = weights[l]
        dof, dom, dfg, dmg, dlt_f, dlt_m, dgate, dw_out = _kb_out(dx, y, u, mod3[2:3], w_out, of, om, gates, wg_dtype)

        side = None
        if shards is not None:
            side_arrs = (pending or []) + [dw_out.reshape(N_DEV, -1, dw_out.shape[1])]
            side = (side_arrs, [False] * len(side_arrs))
        dfq, dfk, dfv, dcq, dck, *arrived = _attention_bwd(
            qkv, qb, qkv, kb, qkv, vb, dof, pack_rows(-lse_f.reshape(HEADS, s_len), dlt_f), c2[:, None, :], None,
            False, FOX_SCALE, 1.0 / LOG2E, "fox_bwd_exchange" if pending else "fox_bwd", side)
        if side:
            received[l] = [None, None, None, arrived[-1]]
            if pending:
                received[l + 1][:3] = arrived[:3]
        dcum = (dcq[:, :HEADS].T + dck.reshape(HEADS, s_len)).reshape(HEADS * chunks, LANES)
        dff_rows, dbf_rows = _k_cum_bwd(dcum, fft, bf, chunks)
        dfft = dff_rows.reshape(HEADS, s_len)
        grads["b_f"].append(jnp.sum(dbf_rows[:, 0].reshape(HEADS, chunks), axis=1))

        dmq, dkn, dmv, dqr, dkr_pairs = _attention_bwd(
            mq, 0, mk, 0, mv, 0, dom, pack_rows(-lse_m.reshape(HEADS, s_len), dlt_m), None, (mqr, kr2), True,
            MLA_SCALE, 1.0 / LOG2E, "mla_bwd")
        dkr = dkr_pairs[0] + dkr_pairs[1] + dkr_pairs[2] + dkr_pairs[3]
        dff = jnp.pad(dfft.T, ((0, 0), (MISC_FF, LANES - MISC_FF - HEADS)))
        dq_b, dz_tail, dgq, dgkv = _kb_prep(dmq, dqr, dkn, dmv, dkr, dff, tail, cos, sin, q_norm_g[l][None],
                                            kv_norm_g[l][None], w_uq, w_ukv)
        grads["q_norm_g"].append(dgq[0])
        grads["kv_norm_g"].append(dgkv[0])
        dw_uq = _weight_grad(qn, [dq_b], "dw_uq", wg_dtype)
        dw_ukv = _weight_grad(kvn, [dkn, dmv], "dw_ukv", wg_dtype)
        dz = [dfq, dfk, dfv, dfg, dmg, dz_tail]
        dw_in = _weight_grad(h, dz, "dw_in", wg_dtype)
        dx, acc3 = _kb_in(dz, w_in, x_l, norm_g[l][None], mod3, dx)
        grads["norm_g"].append(acc3[2])
        grads["mod"].append(jnp.concatenate([acc3[0], acc3[1], dgate[0]]))
        if shards is not None:
            pending = _grad_slabs(dw_in, dw_uq, dw_ukv)
        else:
            for name, g in (("w_in", dw_in), ("w_uq", dw_uq), ("w_ukv", dw_ukv), ("w_out", dw_out)):
                grads[name].append(g)
    grads = {k: jnp.stack(v[::-1]) for k, v in grads.items() if v}
    grads["final_g"] = dfinal[0]
    if shards is None:
        return loss_row[0, 0], dx, grads
    return loss_row[0, 0], dx, grads, received, pending


def _pack_small(parts, total):
    flat = jnp.concatenate([p.reshape(-1) for p in parts])
    return jnp.pad(flat, (0, total - flat.shape[0])).reshape(total // LANES, LANES)


def kernel(x, c, positions, norm_g, w_ada, b_ada, w_in, b_f, q_norm_g, w_uq, kv_norm_g, w_ukv, w_out, final_g, loss_target, m_norm_g, m_w_ada, m_b_ada, m_w_in, m_b_f, m_q_norm_g, m_w_uq, m_kv_norm_g, m_w_ukv, m_w_out, m_final_g, v_norm_g, v_w_ada, v_b_ada, v_w_in, v_b_f, v_q_norm_g, v_w_uq, v_kv_norm_g, v_w_ukv, v_w_out, v_final_g):
    n_l, d = norm_g.shape
    me = 4 * lax.axis_index("x") + 2 * lax.axis_index("y") + lax.axis_index("c")
    ada_c = w_ada.shape[2]

    cact = jnp.broadcast_to(jax.nn.silu(c), (N_DEV, d))
    shards = [[w[l].astype(MXU) for w in (w_in, w_uq, w_ukv, w_out)] for l in range(n_l)]
    *g_w0, g_cact = _gather_two_level(shards[0] + [cact], "gather_layer0")
    cact_all = g_cact[:, 0, :]

    b_cols = lax.dynamic_slice_in_dim(b_ada, me * ada_c, ada_c, axis=1)[:, None, :]
    modpart = _modpart(cact_all, w_ada, b_cols)
    mod_send = jnp.pad(modpart.transpose(1, 0, 2), ((0, 0), (0, 8 - n_l), (0, 0)))
    (mod_recv,) = _exchange([mod_send], [False], "scatter_mod")
    mod = mod_recv.transpose(1, 0, 2).reshape(8, N_DEV * ada_c)[:n_l]

    loss, dx, gr, received, pending = _local_step(x[0], mod, positions[0], loss_target[0], norm_g, b_f, q_norm_g,
                                                  kv_norm_g, final_g, [_full_weights(*g_w0)], shards)

    small_parts = [gr["norm_g"], gr["mod"], gr["b_f"], gr["q_norm_g"], gr["kv_norm_g"], gr["final_g"], cact[0]]
    sizes = [int(np.prod(p.shape)) for p in small_parts]
    total = -(-sum(sizes) // 1024) * 1024
    small = _pack_small(small_parts, total)
    *received[0][:3], r_small = _exchange(pending + [small], [False, False, False, True], "exchange_layer0")
    r_in, r_uq, r_ukv, r_out = ([received[l][i] for l in range(n_l)] for i in range(4))

    def upd(slabs, w, m, v, name):
        shp = w.shape
        w2, m2, v2 = (a.reshape(-1, slabs[0].shape[2]) for a in (w, m, v))
        return [o.reshape(shp) for o in _adamw(slabs, w2, m2, v2, name)]

    o_in = upd(r_in, w_in, m_w_in, v_w_in, "adamw_w_in")
    o_uq = upd(r_uq, w_uq, m_w_uq, v_w_uq, "adamw_w_uq")
    o_ukv = upd(r_ukv, w_ukv, m_w_ukv, v_w_ukv, "adamw_w_ukv")
    o_out = upd(r_out, w_out, m_w_out, v_w_out, "adamw_w_out")

    offs = np.cumsum([0] + sizes)
    flat_all = r_small.reshape(N_DEV, total)
    dmod_all = flat_all[:, offs[1]:offs[2]].reshape(N_DEV, n_l, 3 * d)
    dmod_cols = lax.dynamic_slice_in_dim(dmod_all, me * ada_c, ada_c, axis=2).transpose(1, 0, 2)
    cact_cols = flat_all[:, offs[6]:offs[7]][:, :, None]
    g_ada = _ada_grad(cact_cols, dmod_cols)
    o_ada = upd([g_ada.reshape(1, n_l * d, ada_c)], w_ada, m_w_ada, v_w_ada, "adamw_w_ada")

    zero_c = jnp.zeros((d,), F32)
    small_w = [_pack_small([norm_g, b_ada, b_f, q_norm_g, kv_norm_g, final_g, zero_c], total),
               _pack_small([m_norm_g, m_b_ada, m_b_f, m_q_norm_g, m_kv_norm_g, m_final_g, zero_c], total),
               _pack_small([v_norm_g, v_b_ada, v_b_f, v_q_norm_g, v_kv_norm_g, v_final_g, zero_c], total)]
    o_small = [o.reshape(-1) for o in _adamw([r_small], *small_w, "adamw_small")]
    shapes = [norm_g.shape, b_ada.shape, b_f.shape, q_norm_g.shape, kv_norm_g.shape, final_g.shape]

    def small_out(kind, idx):
        return o_small[kind][offs[idx]:offs[idx + 1]].reshape(shapes[idx])

    loss_all = lax.psum(loss, ("x", "y", "c"))
    outs = [loss_all, dx[None]]
    for kind in range(4):
        outs += [small_out(kind, 0), o_ada[kind], small_out(kind, 1), o_in[kind], small_out(kind, 2),
                 small_out(kind, 3), o_uq[kind], small_out(kind, 4), o_ukv[kind], o_out[kind], small_out(kind, 5)]
    return tuple(outs)
```

```python
import jax
import jax.numpy as jnp
import numpy as np
from jax import lax
from jax.experimental import pallas as pl
from jax.experimental.pallas import tpu as pltpu

F32 = jnp.float32
MXU = jnp.bfloat16

N_DEV = 8
HEADS = 8
PAIRS = HEADS // 2
HEAD_DIM = 64
NOPE = 64
ROPE = 32
HALF_ROPE = ROPE // 2
Q_LORA = 256
KV_LORA = 128
CHUNK = 64
GROUP_W = HEADS * HEAD_DIM
EPS = 1e-6
ROPE_THETA = 10000.0

Z_FQ, Z_FK, Z_FV, Z_FG, Z_MG, Z_QL, Z_KV, Z_MISC, Z_W = 0, 512, 1024, 1536, 2048, 2560, 2816, 2944, 3072
MISC_FF = ROPE
TAIL_W = Z_W - Z_QL
T_KV, T_MISC = Q_LORA, Q_LORA + KV_LORA

ADAM_LR = 0.001
ADAM_B1 = 0.9
ADAM_B2 = 0.999
ADAM_EPS = 1e-08
ADAM_WD = 0.01
ADAM_STEP = 10

VMEM_LIMIT_V7X = 56 * 1024 * 1024
LANES = 128
ATTN_TILE = 1024
V_ROWS = HEAD_DIM + 16
LOG2E = 1.4426950408889634
FOX_SCALE = HEAD_DIM ** -0.5
MLA_SCALE = (NOPE + ROPE) ** -0.5

_NT = (((1,), (1,)), ((), ()))
_TN = (((0,), (0,)), ((), ()))


def _params(*sem, side_effects=False):
    return pltpu.CompilerParams(dimension_semantics=sem, vmem_limit_bytes=VMEM_LIMIT_V7X,
                                has_side_effects=side_effects)


def _sds(shape, dtype=F32):
    return jax.ShapeDtypeStruct(shape, dtype)


def _full(shape):
    nd = len(shape)
    return pl.BlockSpec(shape, lambda *_: (0,) * nd)


def _rows(tm, width, col=0):
    return pl.BlockSpec((tm, width), lambda i: (i, col))


def _exchange(arrs, gather, name):
    n = len(arrs)

    def kern(*refs):
        copies = _exchange_copies(refs[:n], refs[n:2 * n], gather, *refs[2 * n:])
        _exchange_start(copies)
        _exchange_wait(copies)

    return pl.pallas_call(
        kern, name=name, out_shape=_exchange_out_shapes(arrs, gather),
        in_specs=[pl.BlockSpec(memory_space=pl.ANY)] * n,
        out_specs=[pl.BlockSpec(memory_space=pl.ANY)] * n,
        scratch_shapes=_exchange_sems(n),
        compiler_params=pltpu.CompilerParams(has_side_effects=True),
    )(*arrs)


def _gather_two_level(arrs, name):
    n = len(arrs)

    def kern(*refs):
        ins, outs = refs[:n], refs[n:2 * n]
        send_sems, recv_sems, loc_sems = refs[2 * n:]
        x, y, c = lax.axis_index("x"), lax.axis_index("y"), lax.axis_index("c")
        me, sibling = (x, y, c), (x, y, 1 - c)
        chips = [(1 - x, y), (x, 1 - y), (1 - x, 1 - y)]

        def slot(i, dev):
            return outs[i].at[4 * dev[0] + 2 * dev[1] + dev[2]]

        def copy(i, k, block, to, src=None):
            return pltpu.make_async_remote_copy(
                src_ref=slot(i, block) if src is None else src, dst_ref=slot(i, block), send_sem=send_sems.at[i, k],
                recv_sem=recv_sems.at[i, k], device_id=to, device_id_type=pl.DeviceIdType.MESH)

        mine = [pltpu.make_async_copy(ins[i], slot(i, me), loc_sems.at[i]) for i in range(n)]
        first = [copy(i, 0, me, sibling, src=ins[i]) for i in range(n)]
        first += [copy(i, 1 + j, me, (*chip, c), src=ins[i]) for j, chip in enumerate(chips) for i in range(n)]
        for cp in mine + first:
            cp.start()
        passed = []
        for j, chip in enumerate(chips):
            for i in range(n):
                copy(i, 1 + j, (*chip, c), me).wait_recv()
                fwd = copy(i, 4 + j, (*chip, c), sibling)
                fwd.start()
                passed.append(fwd)
        for i in range(n):
            copy(i, 0, sibling, me).wait_recv()
            for j, chip in enumerate(chips):
                copy(i, 4 + j, (*chip, 1 - c), me).wait_recv()
        for cp in first + passed:
            cp.wait_send()
        for cp in mine:
            cp.wait()

    return pl.pallas_call(
        kern, name=name, out_shape=_exchange_out_shapes(arrs, [True] * n),
        in_specs=[pl.BlockSpec(memory_space=pl.ANY)] * n,
        out_specs=[pl.BlockSpec(memory_space=pl.ANY)] * n,
        scratch_shapes=_exchange_sems(n),
        compiler_params=pltpu.CompilerParams(has_side_effects=True),
    )(*arrs)


def _exchange_out_shapes(arrs, gather):
    return [_sds((N_DEV,) + tuple(a.shape) if g else tuple(a.shape), a.dtype) for a, g in zip(arrs, gather)]


def _exchange_sems(n):
    return [pltpu.SemaphoreType.DMA((n, N_DEV)), pltpu.SemaphoreType.DMA((n, N_DEV)), pltpu.SemaphoreType.DMA((n,))]


def _exchange_copies(ins, outs, gather, send_sems, recv_sems, loc_sems, recv=True):
    n = len(ins)
    x, y, c = lax.axis_index("x"), lax.axis_index("y"), lax.axis_index("c")
    me = 4 * x + 2 * y + c

    def src(i, j):
        return ins[i] if gather[i] else ins[i].at[j]

    local = [pltpu.make_async_copy(src(i, me), outs[i].at[me], loc_sems.at[i]) for i in range(n)]
    sends, recvs = [], []
    for k in range(1, N_DEV):
        px = 1 - x if k & 4 else x
        py = 1 - y if k & 2 else y
        pc = 1 - c if k & 1 else c
        p = 4 * px + 2 * py + pc
        for i in range(n):
            sends.append(pltpu.make_async_remote_copy(
                src_ref=src(i, p), dst_ref=outs[i].at[me], send_sem=send_sems.at[i, k],
                recv_sem=recv_sems.at[i, k], device_id=(px, py, pc), device_id_type=pl.DeviceIdType.MESH))
            if recv:
                recvs.append(pltpu.make_async_remote_copy(
                    src_ref=src(i, p), dst_ref=outs[i].at[p], send_sem=send_sems.at[i, k],
                    recv_sem=recv_sems.at[i, k], device_id=(px, py, pc), device_id_type=pl.DeviceIdType.MESH))
    return local, sends, recvs


def _exchange_start(copies):
    local, sends, _ = copies
    for cp in local + sends:
        cp.start()


def _exchange_wait(copies):
    local, sends, recvs = copies
    for cp in recvs:
        cp.wait_recv()
    for cp in sends:
        cp.wait_send()
    for cp in local:
        cp.wait()


def _modpart(cact8, w_ada, b_cols):
    n_l, d, cw = w_ada.shape

    def kern(c_ref, w_ref, b_ref, o_ref):
        o_ref[0] = jnp.dot(c_ref[...].astype(MXU), w_ref[0].astype(MXU), preferred_element_type=F32) + b_ref[0]

    return pl.pallas_call(
        kern, name="modpart", grid=(n_l,), out_shape=_sds((n_l, N_DEV, cw)),
        in_specs=[_full((N_DEV, d)), pl.BlockSpec((1, d, cw), lambda l: (l, 0, 0)),
                  pl.BlockSpec((1, 1, cw), lambda l: (l, 0, 0))],
        out_specs=pl.BlockSpec((1, N_DEV, cw), lambda l: (l, 0, 0)),
        compiler_params=_params("arbitrary"),
    )(cact8, w_ada, b_cols)


def _ada_grad(cact_cols, dmod_cols):
    n_l, _, cw = dmod_cols.shape
    d = cact_cols.shape[1]

    def kern(c_ref, dm_ref, o_ref):
        acc = c_ref[0] * dm_ref[0, 0:1, :]
        for s in range(1, N_DEV):
            acc = acc + c_ref[s] * dm_ref[0, s:s + 1, :]
        o_ref[0] = acc

    return pl.pallas_call(
        kern, name="ada_grad", grid=(n_l,), out_shape=_sds((n_l, d, cw)),
        in_specs=[_full((N_DEV, d, 1)), pl.BlockSpec((1, N_DEV, cw), lambda l: (l, 0, 0))],
        out_specs=pl.BlockSpec((1, d, cw), lambda l: (l, 0, 0)),
        compiler_params=_params("arbitrary"),
    )(cact_cols, dmod_cols)


def _k_in(x, g, mod3, w, tm=256):
    s_len, d = x.shape
    qkv_w = 3 * GROUP_W

    def kern(x_ref, g_ref, mod_ref, w_ref, h_ref, qkv_ref, gates_ref, tail_ref):
        xv = x_ref[...]
        r = lax.rsqrt(jnp.mean(xv * xv, axis=-1, keepdims=True) + EPS)
        xn = xv * r * g_ref[...]
        h = (xn * (1.0 + mod_ref[1:2, :]) + mod_ref[0:1, :]).astype(MXU)
        h_ref[...] = h
        z = jnp.dot(h, w_ref[...], preferred_element_type=F32)
        qkv_ref[:, :GROUP_W] = (z[:, Z_FQ:Z_FQ + GROUP_W] * (FOX_SCALE * LOG2E)).astype(MXU)
        qkv_ref[:, GROUP_W:] = z[:, Z_FK:Z_FK + 2 * GROUP_W].astype(MXU)
        gates_ref[...] = z[:, Z_FG:Z_QL]
        tail_ref[...] = z[:, Z_QL:]

    return pl.pallas_call(
        kern, name="k_in", grid=(s_len // tm,),
        out_shape=[_sds((s_len, d), MXU), _sds((s_len, qkv_w), MXU), _sds((s_len, Z_QL - Z_FG)),
                   _sds((s_len, TAIL_W))],
        in_specs=[_rows(tm, d), _full((1, d)), _full((3, d)), _full((d, Z_W))],
        out_specs=[_rows(tm, d), _rows(tm, qkv_w), _rows(tm, Z_QL - Z_FG), _rows(tm, TAIL_W)],
        compiler_params=_params("arbitrary"),
    )(x, g, mod3, w)


def _scan_matrices(rows, chunks, reverse):
    r_i = lax.broadcasted_iota(jnp.int32, (LANES, LANES), 0)
    c_i = lax.broadcasted_iota(jnp.int32, (LANES, LANES), 1)
    a_i = lax.broadcasted_iota(jnp.int32, (rows, rows), 0)
    b_i = lax.broadcasted_iota(jnp.int32, (rows, rows), 1)
    same_head = (a_i // chunks) == (b_i // chunks)
    if reverse:
        return (r_i >= c_i).astype(F32), (same_head & (b_i > a_i)).astype(F32)
    return (r_i <= c_i).astype(F32), (same_head & (b_i < a_i)).astype(F32)


def _scan_rows(x, inner, outer):
    tot = jnp.broadcast_to(jnp.sum(x, axis=1, keepdims=True), x.shape)
    return (jnp.dot(x, inner, precision=lax.Precision.HIGHEST, preferred_element_type=F32)
            + jnp.dot(outer, tot, precision=lax.Precision.HIGHEST, preferred_element_type=F32))


def _k_cum(ff_rows, b_rows, chunks):
    rows = ff_rows.shape[0]

    def kern(ff_ref, b_ref, cum_ref):
        xc = ff_ref[...] + b_ref[...]
        lf = jnp.minimum(xc, 0.0) - jnp.log(1.0 + jnp.exp(-jnp.abs(xc)))
        cum_ref[...] = _scan_rows(lf, *_scan_matrices(rows, chunks, False))

    return pl.pallas_call(
        kern, name="k_cum", out_shape=_sds((rows, LANES)),
        in_specs=[pl.BlockSpec(memory_space=pltpu.VMEM)] * 2,
        out_specs=pl.BlockSpec(memory_space=pltpu.VMEM),
        compiler_params=_params(),
    )(ff_rows, b_rows)


def _k_cum_bwd(dc_rows, ff_rows, b_rows, chunks):
    rows = ff_rows.shape[0]

    def kern(dc_ref, ff_ref, b_ref, dff_ref, db_ref):
        dlf = _scan_rows(dc_ref[...], *_scan_matrices(rows, chunks, True))
        dff = dlf * jax.nn.sigmoid(-(ff_ref[...] + b_ref[...]))
        dff_ref[...] = dff
        db_ref[...] = jnp.broadcast_to(jnp.sum(dff, axis=1, keepdims=True), dff.shape)

    return pl.pallas_call(
        kern, name="k_cum_bwd", out_shape=[_sds((rows, LANES)), _sds((rows, LANES))],
        in_specs=[pl.BlockSpec(memory_space=pltpu.VMEM)] * 3,
        out_specs=[pl.BlockSpec(memory_space=pltpu.VMEM)] * 2,
        compiler_params=_params(),
    )(dc_rows, ff_rows, b_rows)


def _swap16(t):
    lane = lax.broadcasted_iota(jnp.int32, t.shape, 1)
    return jnp.where(lane % ROPE < HALF_ROPE, pltpu.roll(t, LANES - HALF_ROPE, 1), pltpu.roll(t, HALF_ROPE, 1))


def _rope(t, cos, sin):
    return t * cos + _swap16(t) * sin


def _rope_bwd(dt, cos, sin):
    return dt * cos - _swap16(dt) * sin


def _k_prep(tail, cos, sin, gq, gkv, wuq, wukv, tm=512):
    s_len = tail.shape[0]
    qc = MLA_SCALE * LOG2E

    def kern(tail_ref, cos_ref, sin_ref, gq_ref, gkv_ref, wuq_ref, wukv_ref,
             qn_out, qr_out, kn_out, v_out, kr_out, qn_ref, kvn_ref):
        cs, sn = cos_ref[...], sin_ref[...]
        ql = tail_ref[:, :T_KV]
        rq = lax.rsqrt(jnp.mean(ql * ql, axis=-1, keepdims=True) + EPS)
        qn = (ql * rq * gq_ref[...]).astype(MXU)
        qn_ref[...] = qn
        q = jnp.dot(qn, wuq_ref[...], preferred_element_type=F32)
        qn_out[...] = (q[:, :GROUP_W] * qc).astype(MXU)
        for blk in range(PAIRS):
            lo = GROUP_W + blk * LANES
            qr_out[:, blk * LANES:(blk + 1) * LANES] = (_rope(q[:, lo:lo + LANES], cs, sn) * qc).astype(MXU)
        kvl = tail_ref[:, T_KV:T_MISC]
        rk = lax.rsqrt(jnp.mean(kvl * kvl, axis=-1, keepdims=True) + EPS)
        kvn = (kvl * rk * gkv_ref[...]).astype(MXU)
        kvn_ref[...] = kvn
        kv = jnp.dot(kvn, wukv_ref[...], preferred_element_type=F32)
        kn_out[...] = kv[:, :GROUP_W].astype(MXU)
        v_out[...] = kv[:, GROUP_W:].astype(MXU)
        misc = tail_ref[:, T_MISC:]
        lane = lax.broadcasted_iota(jnp.int32, misc.shape, 1)
        kr = jnp.where(lane < ROPE, _rope(misc, cs, sn), 0.0)
        kr_out[...] = (kr + pltpu.roll(kr, HEAD_DIM, 1)).astype(MXU)

    return pl.pallas_call(
        kern, name="k_prep", grid=(s_len // tm,),
        out_shape=[_sds((s_len, GROUP_W), MXU), _sds((s_len, GROUP_W), MXU), _sds((s_len, GROUP_W), MXU),
                   _sds((s_len, GROUP_W), MXU), _sds((s_len, LANES), MXU), _sds((s_len, Q_LORA), MXU),
                   _sds((s_len, KV_LORA), MXU)],
        in_specs=[_rows(tm, TAIL_W), _rows(tm, LANES), _rows(tm, LANES),
                  _full((1, Q_LORA)), _full((1, KV_LORA)), _full((Q_LORA, 2 * GROUP_W)),
                  _full((KV_LORA, 2 * GROUP_W))],
        out_specs=[_rows(tm, GROUP_W), _rows(tm, GROUP_W), _rows(tm, GROUP_W), _rows(tm, GROUP_W), _rows(tm, LANES),
                   _rows(tm, Q_LORA), _rows(tm, KV_LORA)],
        compiler_params=_params("arbitrary"),
    )(tail, cos, sin, gq, gkv, wuq, wukv)


def _block_mask(kn, qn, q_off, chunk_mask, transposed):
    shape = (kn, qn) if transposed else (qn, kn)
    row = lax.broadcasted_iota(jnp.int32, shape, 0)
    col = lax.broadcasted_iota(jnp.int32, shape, 1)
    qi, ki = (col + q_off, row) if transposed else (row + q_off, col)
    if chunk_mask:
        return (ki // CHUNK) <= (qi // CHUNK)
    return ki <= qi


def _head_operand(x, hh, other=None):
    lane = lax.broadcasted_iota(jnp.int32, x.shape, 1)
    own = (lane >= hh * HEAD_DIM) & (lane < (hh + 1) * HEAD_DIM)
    return jnp.where(own, x, jnp.zeros_like(x) if other is None else other)


def _attention_fwd(q, q_blk, k, k_blk, v, v_blk, bias, rope, chunk_mask, name, side=None):
    s_len = q.shape[0]
    t = min(ATTN_TILE, s_len // 2)
    nq = s_len // t
    n_side = len(side[0]) if side else 0

    def kern(*refs):
        q_ref, k_ref, v_ref = refs[:3]
        pos = 3
        if bias is not None:
            ck_ref = refs[pos]
            pos += 1
        if rope is not None:
            qr_ref, kr_ref = refs[pos:pos + 2]
            pos += 2
        side_in = refs[pos:pos + n_side]
        pos += n_side
        o_ref, lse_ref = refs[pos:pos + 2]
        side_out = refs[pos + 2:pos + 2 + n_side]
        vt_scr, m_scr, acc_scr, ck_scr = refs[pos + 2 + n_side:pos + 6 + n_side]
        sems = refs[pos + 6 + n_side:]
        pj = pl.program_id(0)
        if n_side:
            @pl.when(pj == 0)
            def _():
                _exchange_start(_exchange_copies(side_in, side_out, side[1], *sems, recv=False))
        vt_scr[:, HEAD_DIM:, :] = jnp.ones((2, V_ROWS - HEAD_DIM, s_len), vt_scr.dtype)
        for i in range(nq):
            vtt = v_ref[i * t:(i + 1) * t, :].T
            for hh in range(2):
                vt_scr[hh, :HEAD_DIM, i * t:(i + 1) * t] = vtt[hh * HEAD_DIM:(hh + 1) * HEAD_DIM, :]
                if bias is not None:
                    ckt = ck_ref[i * t:(i + 1) * t, :]
                    lane = lax.broadcasted_iota(jnp.int32, ckt.shape, 1)
                    ck_scr[hh, i * t:(i + 1) * t, :] = jnp.sum(jnp.where(lane == 2 * pj + hh, ckt, 0.0), axis=1,
                                                               keepdims=True)

        def qbody(qi, _):
            qs = pl.multiple_of(qi * t, t)
            qt = q_ref[pl.ds(qs, t), :]
            qrt = qr_ref[pl.ds(qs, t), :] if rope is not None else None
            qh = [_head_operand(qt, hh, qrt) for hh in range(2)]
            m_scr[...] = jnp.full(m_scr.shape, -jnp.inf, F32)
            acc_scr[...] = jnp.zeros(acc_scr.shape, F32)

            def block(ks, kn, q0, qn, masked):
                kt = k_ref[pl.ds(ks, kn), :]
                kh = [_head_operand(kt, hh, kr_ref[pl.ds(ks, kn), :]) for hh in range(2)] if rope is not None else [kt, kt]
                qc = slice(q0, q0 + qn)
                sts = [lax.dot_general(kh[hh], qh[hh][qc], _NT, preferred_element_type=F32) for hh in range(2)]
                if bias is not None:
                    sts = [sts[hh] - ck_scr[hh, pl.ds(ks, kn), :] for hh in range(2)]
                if masked:
                    sts = [jnp.where(_block_mask(kn, qn, q0, chunk_mask, True), st, -jnp.inf) for st in sts]
                m_old = [m_scr[hh, :, qc] for hh in range(2)]
                m_new = [jnp.maximum(m_old[hh], jnp.max(sts[hh], axis=0, keepdims=True)) for hh in range(2)]
                pts = [jnp.exp2(sts[hh] - m_new[hh]).astype(MXU) for hh in range(2)]
                for hh in range(2):
                    alpha = jnp.exp2(m_old[hh] - m_new[hh])
                    acc_scr[hh, :, qc] = alpha * acc_scr[hh, :, qc] + jnp.dot(vt_scr[hh, :, pl.ds(ks, kn)], pts[hh],
                                                                            preferred_element_type=F32)
                    m_scr[hh, :, qc] = m_new[hh]

            def loop_body(ki, carry):
                block(pl.multiple_of(ki * t, t), t, 0, t, False)
                return carry

            lax.fori_loop(0, qi, loop_body, 0)
            block(qs, t, 0, t, True)
            outs = []
            for hh in range(2):
                acc = acc_scr[hh]
                l = acc[HEAD_DIM:HEAD_DIM + 1, :]
                outs.append(acc[:HEAD_DIM, :] / l)
                lse_ref[hh, :, pl.ds(qs, t)] = m_scr[hh] + jnp.log2(l)
            o_ref[pl.ds(qs, t), :] = jnp.concatenate(outs, axis=0).T
            return 0

        lax.fori_loop(0, nq, qbody, 0)
        if n_side:
            @pl.when(pj == PAIRS - 1)
            def _():
                _exchange_wait(_exchange_copies(side_in, side_out, side[1], *sems))

    def tok(blk):
        return pl.BlockSpec((s_len, LANES), lambda j: (0, blk + j))

    rowb = pl.BlockSpec((2, 1, s_len), lambda j: (j, 0, 0))
    hbm = pl.BlockSpec(memory_space=pl.ANY)
    ins = [q, k, v]
    in_specs = [tok(q_blk), tok(k_blk), tok(v_blk)]
    if bias is not None:
        ins.append(bias)
        in_specs.append(_full((s_len, LANES)))
    if rope is not None:
        ins += list(rope)
        in_specs += [tok(0), _full((s_len, LANES))]
    out_shape = [_sds((s_len, PAIRS * LANES)), _sds((HEADS, 1, s_len))]
    scratch = [pltpu.VMEM((2, V_ROWS, s_len), v.dtype), pltpu.VMEM((2, 1, t), F32), pltpu.VMEM((2, V_ROWS, t), F32),
               pltpu.VMEM((2, s_len if bias is not None else 8, 1), F32)]
    if n_side:
        ins += list(side[0])
        out_shape += _exchange_out_shapes(*side)
        scratch += _exchange_sems(n_side)
    return pl.pallas_call(
        kern, name=name, grid=(PAIRS,), out_shape=out_shape,
        in_specs=in_specs + [hbm] * n_side, out_specs=[tok(0), rowb] + [hbm] * n_side,
        scratch_shapes=scratch,
        compiler_params=_params("arbitrary", side_effects=bool(n_side)),
    )(*ins)


def _attention_bwd(q, q_blk, k, k_blk, v, v_blk, do, pack, ck_row, rope, chunk_mask, q_scale, k_scale, name,
                   side=None):
    s_len = q.shape[0]
    t = min(ATTN_TILE, s_len // 2)
    nq = s_len // t
    has_bias = ck_row is not None
    nv = 2 if rope is not None else 1
    n_side = len(side[0]) if side else 0

    def kern(*refs):
        q_ref, k_ref, v_ref, do_ref, pack_ref = refs[:5]
        pos = 5
        if has_bias:
            ck_ref = refs[pos]
            pos += 1
        if rope is not None:
            qr_ref, kr_ref = refs[pos:pos + 2]
            pos += 2
        side_in = refs[pos:pos + n_side]
        pos += n_side
        dq_ref, dk_ref, dv_ref = refs[pos:pos + 3]
        pos += 3
        if has_bias:
            dcq_ref, dck_ref = refs[pos:pos + 2]
            pos += 2
        if rope is not None:
            dqr_ref, dkr_ref = refs[pos:pos + 2]
            pos += 2
        side_out = refs[pos:pos + n_side]
        pos += n_side
        qt_scr, dot_scr, dkt_scr, dvt_scr, dq_scr, dcq_scr = refs[pos:pos + 6]
        sems = refs[pos + 6:]
        pj = pl.program_id(0)
        if n_side:
            @pl.when(pj == 0)
            def _():
                _exchange_start(_exchange_copies(side_in, side_out, side[1], *sems, recv=False))

        for i in range(nq):
            sl = slice(i * t, (i + 1) * t)
            dot_scr[:, sl] = do_ref[sl, :].T
            if rope is not None:
                for hh in range(2):
                    qt_scr[hh, :, sl] = _head_operand(q_ref[sl, :], hh, qr_ref[sl, :]).T
            else:
                qt_scr[0, :, sl] = q_ref[sl, :].T
        dkt_scr[...] = jnp.zeros(dkt_scr.shape, F32)
        dvt_scr[...] = jnp.zeros(dvt_scr.shape, F32)
        if has_bias:
            dck_ref[...] = jnp.zeros(dck_ref.shape, F32)

            @pl.when(pj == 0)
            def _():
                dcq_ref[...] = jnp.zeros(dcq_ref.shape, F32)

        def qbody(qi, _):
            qs = pl.multiple_of(qi * t, t)
            qt = q_ref[pl.ds(qs, t), :]
            qrt = qr_ref[pl.ds(qs, t), :] if rope is not None else None
            dot = do_ref[pl.ds(qs, t), :]
            pk = pack_ref[pl.ds(qs, t), :]
            lane = lax.broadcasted_iota(jnp.int32, pk.shape, 1)
            qh = [_head_operand(qt, hh, qrt) for hh in range(2)]
            doh = [_head_operand(dot, hh) for hh in range(2)]
            a_col = [jnp.sum(jnp.where(lane == 2 * pj + hh, pk, 0.0), axis=1, keepdims=True) for hh in range(2)]
            d_col = [jnp.sum(jnp.where(lane == HEADS + 2 * pj + hh, pk, 0.0), axis=1, keepdims=True)
                     for hh in range(2)]
            dq_scr[...] = jnp.zeros(dq_scr.shape, F32)
            if has_bias:
                dcq_scr[...] = jnp.zeros(dcq_scr.shape, F32)

            def block(ks, kn, q0, qn, masked):
                kt = k_ref[pl.ds(ks, kn), :]
                krt = kr_ref[pl.ds(ks, kn), :] if rope is not None else None
                kh = [_head_operand(kt, hh, krt) for hh in range(2)]
                vt = v_ref[pl.ds(ks, kn), :]
                qr_ = slice(q0, q0 + qn)
                qcols = pl.ds(pl.multiple_of(qs + q0, t // 2), qn)
                ss = [lax.dot_general(qh[hh][qr_], kh[hh] if rope is not None else kt, _NT,
                                      preferred_element_type=F32) + a_col[hh][qr_] for hh in range(2)]
                if has_bias:
                    ss = [ss[hh] - ck_ref[hh, :, pl.ds(ks, kn)] for hh in range(2)]
                dpds = [lax.dot_general(doh[hh][qr_], vt, _NT, preferred_element_type=F32) for hh in range(2)]
                ps = [jnp.exp2(s) for s in ss]
                if masked:
                    ps = [jnp.where(_block_mask(kn, qn, q0, chunk_mask, False), p, 0.0) for p in ps]
                dss = [ps[hh] * (dpds[hh] - d_col[hh][qr_]) for hh in range(2)]
                for hh in range(2):
                    rows = slice(hh * HEAD_DIM, (hh + 1) * HEAD_DIM)
                    dsb = dss[hh].astype(MXU)
                    dvt_scr[rows, pl.ds(ks, kn)] += jnp.dot(dot_scr[rows, qcols], ps[hh].astype(MXU),
                                                            preferred_element_type=F32)
                    if rope is not None:
                        dkt_scr[hh, :, pl.ds(ks, kn)] += jnp.dot(qt_scr[hh, :, qcols], dsb,
                                                                 preferred_element_type=F32)
                    else:
                        dkt_scr[0, rows, pl.ds(ks, kn)] += jnp.dot(qt_scr[0, rows, qcols], dsb,
                                                                   preferred_element_type=F32)
                    dq_scr[hh if rope is not None else 0, qr_, :] += jnp.dot(dsb, kh[hh], preferred_element_type=F32)
                    if has_bias:
                        dcq_scr[hh, qr_, :] += jnp.sum(dss[hh], axis=1, keepdims=True)
                        dck_ref[hh, :, pl.ds(ks, kn)] += -jnp.sum(dss[hh], axis=0, keepdims=True)

            def loop_body(ki, carry):
                block(pl.multiple_of(ki * t, t), t, 0, t, False)
                return carry

            lax.fori_loop(0, qi, loop_body, 0)
            block(qs, t // 2, 0, t // 2, True)
            block(qs, t, t // 2, t // 2, True)
            if rope is not None:
                first = lane < HEAD_DIM
                dq_ref[pl.ds(qs, t), :] = (jnp.where(first, dq_scr[0], dq_scr[1]) * q_scale).astype(dq_ref.dtype)
                dqr_ref[pl.ds(qs, t), :] = jnp.where(first, dq_scr[1], dq_scr[0]) * q_scale
            else:
                dq_ref[pl.ds(qs, t), :] = (dq_scr[0] * q_scale).astype(dq_ref.dtype)
            if has_bias:
                old = dcq_ref[pl.ds(qs, t), :]
                dcq_ref[pl.ds(qs, t), :] = jnp.where(lane == 2 * pj, dcq_scr[0],
                                                     jnp.where(lane == 2 * pj + 1, dcq_scr[1], old))
            return 0

        lax.fori_loop(0, nq, qbody, 0)
        for i in range(nq):
            sl = slice(i * t, (i + 1) * t)
            dv_ref[sl, :] = dvt_scr[:, sl].T.astype(dv_ref.dtype)
            if rope is not None:
                d0, d1 = dkt_scr[0, :, sl], dkt_scr[1, :, sl]
                first = lax.broadcasted_iota(jnp.int32, d0.shape, 0) < HEAD_DIM
                dk_ref[sl, :] = (jnp.where(first, d0, d1).T * k_scale).astype(dk_ref.dtype)
                dkr_ref[0, sl, :] = jnp.where(first, d1, d0).T * k_scale
            else:
                dk_ref[sl, :] = (dkt_scr[0, :, sl].T * k_scale).astype(dk_ref.dtype)
        if n_side:
            @pl.when(pj == PAIRS - 1)
            def _():
                _exchange_wait(_exchange_copies(side_in, side_out, side[1], *sems))

    def tok(blk):
        return pl.BlockSpec((s_len, LANES), lambda j: (0, blk + j))

    shared = _full((s_len, LANES))
    rowb = pl.BlockSpec((2, 1, s_len), lambda j: (j, 0, 0))
    slab = pl.BlockSpec((1, s_len, LANES), lambda j: (j, 0, 0))
    hbm = pl.BlockSpec(memory_space=pl.ANY)
    ins = [q, k, v, do, pack]
    in_specs = [tok(q_blk), tok(k_blk), tok(v_blk), tok(0), shared]
    out_shape = [_sds((s_len, PAIRS * LANES), MXU)] * 3
    out_specs = [tok(0)] * 3
    if has_bias:
        ins.append(ck_row)
        in_specs.append(rowb)
        out_shape += [_sds((s_len, LANES)), _sds((HEADS, 1, s_len))]
        out_specs += [shared, rowb]
    if rope is not None:
        ins += list(rope)
        in_specs += [tok(0), shared]
        out_shape += [_sds((s_len, PAIRS * LANES)), _sds((PAIRS, s_len, LANES))]
        out_specs += [tok(0), slab]
    scratch = [pltpu.VMEM((nv, LANES, s_len), q.dtype), pltpu.VMEM((LANES, s_len), do.dtype),
               pltpu.VMEM((nv, LANES, s_len), F32), pltpu.VMEM((LANES, s_len), F32),
               pltpu.VMEM((nv, t, LANES), F32), pltpu.VMEM((2, t, 1), F32)]
    if n_side:
        ins += list(side[0])
        out_shape += _exchange_out_shapes(*side)
        scratch += _exchange_sems(n_side)
    return pl.pallas_call(
        kern, name=name, grid=(PAIRS,), out_shape=out_shape,
        in_specs=in_specs + [hbm] * n_side, out_specs=out_specs + [hbm] * n_side, scratch_shapes=scratch,
        compiler_params=_params("arbitrary", side_effects=bool(n_side)),
    )(*ins)


def _silu(a):
    return a * jax.nn.sigmoid(a)


def _k_out(of, om, gates, x, gate, wout, tm=256):
    s_len, d = x.shape

    def kern(of_ref, om_ref, gates_ref, x_ref, gate_ref, w_ref, xo_ref, y_ref, u_ref):
        u_ref[:, :GROUP_W] = (of_ref[...] * _silu(gates_ref[:, :GROUP_W])).astype(MXU)
        u_ref[:, GROUP_W:] = (om_ref[...] * _silu(gates_ref[:, GROUP_W:])).astype(MXU)
        y = jnp.dot(u_ref[...], w_ref[...], preferred_element_type=F32)
        y_ref[...] = y
        xo_ref[...] = x_ref[...] + gate_ref[...] * y

    return pl.pallas_call(
        kern, name="k_out", grid=(s_len // tm,),
        out_shape=[_sds((s_len, d)), _sds((s_len, d)), _sds((s_len, 2 * GROUP_W), MXU)],
        in_specs=[_rows(tm, GROUP_W), _rows(tm, GROUP_W), _rows(tm, 2 * GROUP_W), _rows(tm, d), _full((1, d)),
                  _full((2 * GROUP_W, d))],
        out_specs=[_rows(tm, d), _rows(tm, d), _rows(tm, 2 * GROUP_W)],
        compiler_params=_params("arbitrary"),
    )(of, om, gates, x, gate, wout)


def _k_loss(x, gf, tgt, tm=256):
    s_len, d = x.shape

    def kern(x_ref, g_ref, t_ref, loss_ref, dx_ref, dg_ref):
        i = pl.program_id(0)
        xv = x_ref[...]
        r = lax.rsqrt(jnp.mean(xv * xv, axis=-1, keepdims=True) + EPS)
        xh = xv * r
        diff = xh * g_ref[...] - t_ref[...]
        part = 0.5 * jnp.sum(jnp.mean(diff * diff, axis=-1, keepdims=True))
        dout = diff * (1.0 / d)
        dxh = dout * g_ref[...]
        dx_ref[...] = r * (dxh - xh * jnp.mean(dxh * xh, axis=-1, keepdims=True))

        @pl.when(i == 0)
        def _():
            loss_ref[...] = jnp.zeros_like(loss_ref)
            dg_ref[...] = jnp.zeros_like(dg_ref)

        loss_ref[...] += jnp.full(loss_ref.shape, part, F32)
        dg_ref[...] += jnp.sum(dout * xh, axis=0, keepdims=True)

    return pl.pallas_call(
        kern, name="k_loss", grid=(s_len // tm,),
        out_shape=[_sds((1, LANES)), _sds((s_len, d)), _sds((1, d))],
        in_specs=[_rows(tm, d), _full((1, d)), _rows(tm, d)],
        out_specs=[_full((1, LANES)), _rows(tm, d), _full((1, d))],
        compiler_params=_params("arbitrary"),
    )(x, gf, tgt)


def _kb_out(dxo, y, u, gate, wout, of, om, gates, dw_dtype, tm=256):
    s_len, d = dxo.shape
    steps = s_len // tm

    def kern(dxo_ref, y_ref, u_ref, gate_ref, wt_ref, of_ref, om_ref, gates_ref,
             dof_ref, dom_ref, dfg_ref, dmg_ref, dlf_ref, dlm_ref, dgate_ref, dw_ref, dw_acc):
        i = pl.program_id(0)
        dxv = dxo_ref[...]

        @pl.when(i == 0)
        def _():
            dgate_ref[...] = jnp.zeros_like(dgate_ref)
            dw_acc[...] = jnp.zeros_like(dw_acc)

        dgate_ref[...] += jnp.sum(dxv * y_ref[...], axis=0, keepdims=True)
        dyb = (dxv * gate_ref[...]).astype(MXU)
        dw_acc[...] += lax.dot_general(u_ref[...], dyb, _TN, preferred_element_type=F32)

        @pl.when(i == steps - 1)
        def _():
            dw_ref[...] = dw_acc[...].astype(dw_ref.dtype)

        du = lax.dot_general(dyb, wt_ref[...], _NT, preferred_element_type=F32)
        head_of = (lax.broadcasted_iota(jnp.int32, (GROUP_W, LANES), 0) // HEAD_DIM
                   == lax.broadcasted_iota(jnp.int32, (GROUP_W, LANES), 1)).astype(F32)
        for du_g, o_ref, a, do_ref, dg_ref, dl_ref in (
                (du[:, :GROUP_W], of_ref, gates_ref[:, :GROUP_W], dof_ref, dfg_ref, dlf_ref),
                (du[:, GROUP_W:], om_ref, gates_ref[:, GROUP_W:], dom_ref, dmg_ref, dlm_ref)):
            sg = jax.nn.sigmoid(a)
            ov = o_ref[...]
            dov = du_g * (a * sg)
            do_ref[...] = dov.astype(MXU)
            dg_ref[...] = (du_g * ov * (sg * (1.0 + a * (1.0 - sg)))).astype(MXU)
            dl_ref[...] = jnp.dot(dov * ov, head_of, precision=lax.Precision.HIGH, preferred_element_type=F32)

    return pl.pallas_call(
        kern, name="kb_out", grid=(steps,),
        out_shape=[_sds((s_len, GROUP_W), MXU), _sds((s_len, GROUP_W), MXU),
                   _sds((s_len, GROUP_W), MXU), _sds((s_len, GROUP_W), MXU), _sds((s_len, LANES)),
                   _sds((s_len, LANES)), _sds((1, d)), _sds((2 * GROUP_W, d), dw_dtype)],
        in_specs=[_rows(tm, d), _rows(tm, d), _rows(tm, 2 * GROUP_W), _full((1, d)), _full((2 * GROUP_W, d)),
                  _rows(tm, GROUP_W), _rows(tm, GROUP_W), _rows(tm, 2 * GROUP_W)],
        out_specs=[_rows(tm, GROUP_W), _rows(tm, GROUP_W), _rows(tm, GROUP_W),
                   _rows(tm, GROUP_W), _rows(tm, LANES), _rows(tm, LANES), _full((1, d)), _full((2 * GROUP_W, d))],
        scratch_shapes=[pltpu.VMEM((2 * GROUP_W, d), F32)],
        compiler_params=_params("arbitrary"),
    )(dxo, y, u, gate, wout, of, om, gates)


def _kb_prep(dqn, dqr, dkn, dv, dkr, dff, tail, qn, kvn, cos, sin, gq, gkv, wuq_t, wukv_t, dw_dtype, tm=512):
    s_len = tail.shape[0]
    qw = 2 * GROUP_W
    steps = s_len // tm

    def kern(dqn_ref, dqr_ref, dkn_ref, dv_ref, dkr_ref, dff_ref, tail_ref, qn_ref, kvn_ref, cos_ref, sin_ref,
             gq_ref, gkv_ref, wuqt_ref, wukvt_ref, dz_ref, dgq_ref, dgkv_ref, dwuq_ref, dwukv_ref,
             dq_ref, uq_acc, ukv_acc):
        i = pl.program_id(0)

        @pl.when(i == 0)
        def _():
            dgq_ref[...] = jnp.zeros_like(dgq_ref)
            dgkv_ref[...] = jnp.zeros_like(dgkv_ref)
            uq_acc[...] = jnp.zeros_like(uq_acc)
            ukv_acc[...] = jnp.zeros_like(ukv_acc)

        cs, sn = cos_ref[...], sin_ref[...]
        dq_ref[:, :GROUP_W] = dqn_ref[...]
        for blk in range(PAIRS):
            sl = slice(blk * LANES, (blk + 1) * LANES)
            dq_ref[:, GROUP_W + blk * LANES:GROUP_W + (blk + 1) * LANES] = _rope_bwd(dqr_ref[:, sl], cs, sn).astype(MXU)
        dqn = lax.dot_general(dq_ref[...], wuqt_ref[...], _NT, preferred_element_type=F32)
        uq_acc[...] += lax.dot_general(qn_ref[...], dq_ref[...], _TN, preferred_element_type=F32)
        ukv_acc[:, :GROUP_W] += lax.dot_general(kvn_ref[...], dkn_ref[...], _TN, preferred_element_type=F32)
        ukv_acc[:, GROUP_W:] += lax.dot_general(kvn_ref[...], dv_ref[...], _TN, preferred_element_type=F32)

        @pl.when(i == steps - 1)
        def _():
            dwuq_ref[...] = uq_acc[...].astype(dwuq_ref.dtype)
            dwukv_ref[...] = ukv_acc[...].astype(dwukv_ref.dtype)

        ql = tail_ref[:, :T_KV]
        rq = lax.rsqrt(jnp.mean(ql * ql, axis=-1, keepdims=True) + EPS)
        qh = ql * rq
        dgq_ref[...] += jnp.sum(dqn * qh, axis=0, keepdims=True)
        dqh = dqn * gq_ref[...]
        dz_ref[:, :Q_LORA] = (rq * (dqh - qh * jnp.mean(dqh * qh, axis=-1, keepdims=True))).astype(MXU)

        dkvn = (lax.dot_general(dkn_ref[...], wukvt_ref[:, :GROUP_W], _NT, preferred_element_type=F32)
                + lax.dot_general(dv_ref[...], wukvt_ref[:, GROUP_W:], _NT, preferred_element_type=F32))
        kvl = tail_ref[:, T_KV:T_MISC]
        rk = lax.rsqrt(jnp.mean(kvl * kvl, axis=-1, keepdims=True) + EPS)
        kh = kvl * rk
        dgkv_ref[...] += jnp.sum(dkvn * kh, axis=0, keepdims=True)
        dkh = dkvn * gkv_ref[...]
        dz_ref[:, Q_LORA:Q_LORA + KV_LORA] = (
            rk * (dkh - kh * jnp.mean(dkh * kh, axis=-1, keepdims=True))).astype(MXU)

        g = dkr_ref[...] + pltpu.roll(dkr_ref[...], HEAD_DIM, 1)
        lane = lax.broadcasted_iota(jnp.int32, g.shape, 1)
        dmisc = jnp.where(lane < ROPE, _rope_bwd(g, cs, sn), 0.0) + dff_ref[...]
        dz_ref[:, Q_LORA + KV_LORA:] = dmisc.astype(MXU)

    return pl.pallas_call(
        kern, name="kb_prep", grid=(steps,),
        out_shape=[_sds((s_len, TAIL_W), MXU), _sds((1, Q_LORA)), _sds((1, KV_LORA)),
                   _sds((Q_LORA, qw), dw_dtype), _sds((KV_LORA, 2 * GROUP_W), dw_dtype)],
        in_specs=[_rows(tm, GROUP_W), _rows(tm, GROUP_W), _rows(tm, GROUP_W), _rows(tm, GROUP_W), _rows(tm, LANES),
                  _rows(tm, LANES), _rows(tm, TAIL_W), _rows(tm, Q_LORA), _rows(tm, KV_LORA),
                  _rows(tm, LANES), _rows(tm, LANES), _full((1, Q_LORA)), _full((1, KV_LORA)),
                  _full((Q_LORA, qw)), _full((KV_LORA, 2 * GROUP_W))],
        out_specs=[_rows(tm, TAIL_W), _full((1, Q_LORA)), _full((1, KV_LORA)), _full((Q_LORA, qw)),
                   _full((KV_LORA, 2 * GROUP_W))],
        scratch_shapes=[pltpu.VMEM((tm, qw), MXU), pltpu.VMEM((Q_LORA, qw), F32),
                        pltpu.VMEM((KV_LORA, 2 * GROUP_W), F32)],
        compiler_params=_params("arbitrary"),
    )(dqn, dqr, dkn, dv, dkr, dff, tail, qn, kvn, cos, sin, gq, gkv, wuq_t, wukv_t)


def _kb_in(dz_pieces, w, x, g, mod3, dxo, tm=256):
    s_len, d = x.shape
    widths = [p.shape[1] for p in dz_pieces]
    n_p = len(widths)

    def kern(*refs):
        dz_refs = refs[:n_p]
        w_ref, x_ref, g_ref, mod_ref, dxo_ref, dx_ref, acc_ref = refs[n_p:]
        i = pl.program_id(0)

        @pl.when(i == 0)
        def _():
            acc_ref[...] = jnp.zeros_like(acc_ref)

        dh = jnp.zeros((tm, d), F32)
        lo = 0
        for p_ref, wd in zip(dz_refs, widths):
            dh = dh + lax.dot_general(p_ref[...], w_ref[:, lo:lo + wd], _NT, preferred_element_type=F32)
            lo += wd
        xv = x_ref[...]
        r = lax.rsqrt(jnp.mean(xv * xv, axis=-1, keepdims=True) + EPS)
        xh = xv * r
        xn = xh * g_ref[...]
        dxn = dh * (1.0 + mod_ref[1:2, :])
        acc_ref[0:1, :] += jnp.sum(dh, axis=0, keepdims=True)
        acc_ref[1:2, :] += jnp.sum(dh * xn, axis=0, keepdims=True)
        acc_ref[2:3, :] += jnp.sum(dxn * xh, axis=0, keepdims=True)
        dxh = dxn * g_ref[...]
        dx_ref[...] = dxo_ref[...] + r * (dxh - xh * jnp.mean(dxh * xh, axis=-1, keepdims=True))

    return pl.pallas_call(
        kern, name="kb_in", grid=(s_len // tm,),
        out_shape=[_sds((s_len, d)), _sds((3, d))],
        in_specs=[_rows(tm, wd) for wd in widths] + [_full((d, Z_W)), _rows(tm, d), _full((1, d)), _full((3, d)),
                                                     _rows(tm, d)],
        out_specs=[_rows(tm, d), _full((3, d))],
        compiler_params=_params("arbitrary"),
    )(*dz_pieces, w, x, g, mod3, dxo)


def _weight_grad(a, pieces, name, out_dtype=F32, tk=512):
    s_len, m = a.shape
    widths = [p.shape[1] for p in pieces]
    n = sum(widths)
    tk = min(tk, s_len)
    steps = s_len // tk

    def kern(a_ref, *refs):
        o_ref, acc_ref = refs[-2:]

        @pl.when(pl.program_id(0) == 0)
        def _():
            acc_ref[...] = jnp.zeros_like(acc_ref)

        at = a_ref[...]
        lo = 0
        for p_ref, w in zip(refs[:-2], widths):
            acc_ref[:, lo:lo + w] += lax.dot_general(at, p_ref[...], _TN, preferred_element_type=F32)
            lo += w

        @pl.when(pl.program_id(0) == steps - 1)
        def _():
            o_ref[...] = acc_ref[...].astype(o_ref.dtype)

    return pl.pallas_call(
        kern, name=name, grid=(steps,), out_shape=_sds((m, n), out_dtype),
        in_specs=[_rows(tk, m)] + [_rows(tk, w) for w in widths],
        out_specs=_full((m, n)),
        scratch_shapes=[pltpu.VMEM((m, n), F32)],
        compiler_params=_params("arbitrary"),
    )(a, *pieces)


def _adamw(slabs, w, m, v, name):
    n_l = len(slabs)
    n, r, c = slabs[0].shape
    tm = r
    for cand in (256, 128, 64, 32, 16, 8):
        if r % cand == 0:
            tm = cand
            break
    steps = r // tm

    def kern(*refs):
        g_refs = refs[:n_l]
        w_ref, m_ref, v_ref, go_ref, d_ref, mo_ref, vo_ref, g_scr = refs[n_l:]
        for ll in range(n_l):
            @pl.when(pl.program_id(0) == ll)
            def _(g_ref=g_refs[ll]):
                g = g_ref[0].astype(F32)
                for s in range(1, n):
                    g = g + g_ref[s].astype(F32)
                g_scr[...] = g

        g = g_scr[...]
        m_new = ADAM_B1 * m_ref[...] + (1.0 - ADAM_B1) * g
        v_new = ADAM_B2 * v_ref[...] + (1.0 - ADAM_B2) * (g * g)
        m_hat = m_new / (1.0 - ADAM_B1 ** ADAM_STEP)
        v_hat = v_new / (1.0 - ADAM_B2 ** ADAM_STEP)
        go_ref[...] = g
        mo_ref[...] = m_new
        vo_ref[...] = v_new
        d_ref[...] = -ADAM_LR * (m_hat / (jnp.sqrt(v_hat) + ADAM_EPS) + ADAM_WD * w_ref[...])

    row = pl.BlockSpec((tm, c), lambda l, i: (l * steps + i, 0))

    def slab_spec(ll):
        return pl.BlockSpec((n, tm, c), lambda l, i: (0, jnp.where(l == ll, i, 0), 0))

    return pl.pallas_call(
        kern, name=name, grid=(n_l, steps), out_shape=[_sds((n_l * r, c))] * 4,
        in_specs=[slab_spec(ll) for ll in range(n_l)] + [row, row, row],
        out_specs=[row] * 4,
        scratch_shapes=[pltpu.VMEM((tm, c), F32)],
        compiler_params=_params("arbitrary", "arbitrary"),
    )(*slabs, w, m, v)


def _perm_w_in(w):
    pad = jnp.zeros(w.shape[:-1] + (Z_W - Z_MISC - ROPE - HEADS,), w.dtype)
    return jnp.concatenate([w[..., 0:1536], w[..., 1544:2056], w[..., 2472:2984], w[..., 2056:2312],
                            w[..., 2312:2440], w[..., 2440:2472], w[..., 1536:1544], pad], axis=-1)


def _unperm_w_in(g):
    ff0 = Z_MISC + MISC_FF
    return jnp.concatenate([g[..., 0:1536], g[..., ff0:ff0 + HEADS], g[..., Z_FG:Z_FG + GROUP_W],
                            g[..., Z_QL:Z_QL + Q_LORA], g[..., Z_KV:Z_KV + KV_LORA],
                            g[..., Z_MISC:Z_MISC + ROPE], g[..., Z_MG:Z_MG + GROUP_W]], axis=-1)


def _perm_w_uq(w):
    lead = w.shape[:-1]
    wh = w.reshape(lead + (PAIRS, 2, NOPE + ROPE))
    zero = jnp.zeros(lead + (PAIRS, HEAD_DIM - ROPE), w.dtype)
    rope = jnp.concatenate([wh[..., 1, NOPE:], zero, wh[..., 0, NOPE:], zero], axis=-1)
    return jnp.concatenate([wh[..., :NOPE].reshape(lead + (GROUP_W,)), rope.reshape(lead + (GROUP_W,))], axis=-1)


def _unperm_w_uq(g):
    lead = g.shape[:-1]
    nope = g[..., :GROUP_W].reshape(lead + (PAIRS, 2, NOPE))
    rp = g[..., GROUP_W:].reshape(lead + (PAIRS, 2, HEAD_DIM))[..., :ROPE]
    return jnp.concatenate([nope, rp[..., ::-1, :]], axis=-1).reshape(lead + (HEADS * (NOPE + ROPE),))


def _perm_w_ukv(w):
    lead = w.shape[:-1]
    wh = w.reshape(lead + (HEADS, 2 * HEAD_DIM))
    return jnp.concatenate([wh[..., :NOPE].reshape(lead + (GROUP_W,)),
                            wh[..., NOPE:].reshape(lead + (GROUP_W,))], axis=-1)


def _unperm_w_ukv(g):
    lead = g.shape[:-1]
    parts = [g[..., :GROUP_W].reshape(lead + (HEADS, NOPE)), g[..., GROUP_W:].reshape(lead + (HEADS, HEAD_DIM))]
    return jnp.concatenate(parts, axis=-1).reshape(lead + (2 * GROUP_W,))


def _rope_tables(positions):
    inv_freq = 1.0 / (ROPE_THETA ** (jnp.arange(0, ROPE, 2, dtype=F32) / ROPE))
    ang = positions.astype(F32)[:, None] * inv_freq
    cos, sin = jnp.cos(ang), jnp.sin(ang)
    reps = LANES // ROPE
    return jnp.tile(jnp.concatenate([cos, cos], axis=1), (1, reps)), jnp.tile(jnp.concatenate([-sin, sin], axis=1), (1, reps))


def _full_weights(g_in, g_uq, g_ukv, g_out):
    def cols(g):
        return g.transpose(1, 0, 2).reshape(g.shape[1], -1)
    return (_perm_w_in(cols(g_in)), _perm_w_uq(cols(g_uq)), _perm_w_ukv(cols(g_ukv)),
            g_out.reshape(-1, g_out.shape[2]))


def _grad_slabs(dw_in, dw_uq, dw_ukv):
    def cols(g):
        return g.reshape(g.shape[0], N_DEV, -1).transpose(1, 0, 2)
    return [cols(_unperm_w_in(dw_in)), cols(_unperm_w_uq(dw_uq)), cols(_unperm_w_ukv(dw_ukv))]


def _local_step(x, mod, positions, loss_target, norm_g, b_f, q_norm_g, kv_norm_g, final_g, weights, shards=None):
    n_l = norm_g.shape[0]
    s_len, d = x.shape
    cos, sin = _rope_tables(positions)
    qb, kb, vb = Z_FQ // LANES, Z_FK // LANES, Z_FV // LANES
    chunks = s_len // LANES
    weights = list(weights)

    def pack_rows(a_rows, delta):
        return jnp.concatenate([a_rows.T, delta[:, :HEADS], jnp.zeros((s_len, LANES - 2 * HEADS), F32)], axis=1)

    saved = []
    for l in range(n_l):
        w_in, w_uq, w_ukv, w_out = weights[l]
        mod3 = mod[l].reshape(3, d)
        h, qkv, gates, tail = _k_in(x, norm_g[l][None], mod3, w_in)
        fft = tail[:, T_MISC + MISC_FF:T_MISC + MISC_FF + HEADS].T.reshape(HEADS * chunks, LANES)
        bf = jnp.repeat(b_f[l], chunks)[:, None]
        c2 = _k_cum(fft, bf, chunks).reshape(HEADS, s_len) * LOG2E
        side = (list(shards[l + 1]), [True] * 4) if shards is not None and l + 1 < n_l else None
        ck_lanes = jnp.pad(c2.T, ((0, 0), (0, LANES - HEADS)))
        of, lse_f, *gathered = _attention_fwd(qkv, qb, qkv, kb, qkv, vb, ck_lanes, None, False,
                                              "fox_fwd_gather" if side else "fox_fwd", side)
        if side:
            weights.append(_full_weights(*gathered))
        mq, mqr, mk, mv, kr2, qn, kvn = _k_prep(tail, cos, sin, q_norm_g[l][None], kv_norm_g[l][None], w_uq, w_ukv)
        om, lse_m = _attention_fwd(mq, 0, mk, 0, mv, 0, None, (mqr, kr2), True, "mla_fwd")
        x_new, y, u = _k_out(of, om, gates, x, mod3[2:3], w_out)
        saved.append((x, gates, tail, h, qkv, fft, bf, c2, lse_f, mq, mqr, mk, mv, kr2, lse_m, of, om, qn, kvn, y, u,
                      mod3))
        x = x_new

    loss_row, dx, dfinal = _k_loss(x, final_g[None], loss_target)

    grads = {k: [] for k in ("norm_g", "mod", "w_in", "b_f", "q_norm_g", "w_uq", "kv_norm_g", "w_ukv", "w_out")}
    received, pending = {}, None
    wg_dtype = MXU if shards is not None else F32
    for l in range(n_l - 1, -1, -1):
        (x_l, gates, tail, h, qkv, fft, bf, c2, lse_f, mq, mqr, mk, mv, kr2, lse_m, of, om, qn, kvn, y, u,
         mod3) = saved[l]
        w_in, w_uq, w_ukv, w_out = weights[l]
        dof, dom, dfg, dmg, dlt_f, dlt_m, dgate, dw_out = _kb_out(dx, y, u, mod3[2:3], w_out, of, om, gates, wg_dtype)

        side = None
        if shards is not None:
            side_arrs = (pending or []) + [dw_out.reshape(N_DEV, -1, dw_out.shape[1])]
            side = (side_arrs, [False] * len(side_arrs))
        dfq, dfk, dfv, dcq, dck, *arrived = _attention_bwd(
            qkv, qb, qkv, kb, qkv, vb, dof, pack_rows(-lse_f.reshape(HEADS, s_len), dlt_f), c2[:, None, :], None,
            False, FOX_SCALE, 1.0 / LOG2E, "fox_bwd_exchange" if pending else "fox_bwd", side)
        if side:
            received[l] = [None, None, None, arrived[-1]]
            if pending:
                received[l + 1][:3] = arrived[:3]
        dcum = (dcq[:, :HEADS].T + dck.reshape(HEADS, s_len)).reshape(HEADS * chunks, LANES)
        dff_rows, dbf_rows = _k_cum_bwd(dcum, fft, bf, chunks)
        dfft = dff_rows.reshape(HEADS, s_len)
        grads["b_f"].append(jnp.sum(dbf_rows[:, 0].reshape(HEADS, chunks), axis=1))

        dmq, dkn, dmv, dqr, dkr_pairs = _attention_bwd(
            mq, 0, mk, 0, mv, 0, dom, pack_rows(-lse_m.reshape(HEADS, s_len), dlt_m), None, (mqr, kr2), True,
            MLA_SCALE, 1.0 / LOG2E, "mla_bwd")
        dkr = dkr_pairs[0] + dkr_pairs[1] + dkr_pairs[2] + dkr_pairs[3]
        dff = jnp.pad(dfft.T, ((0, 0), (MISC_FF, LANES - MISC_FF - HEADS)))
        dz_tail, dgq, dgkv, dw_uq, dw_ukv = _kb_prep(dmq, dqr, dkn, dmv, dkr, dff, tail, qn, kvn, cos, sin,
                                                     q_norm_g[l][None], kv_norm_g[l][None], w_uq, w_ukv, wg_dtype)
        grads["q_norm_g"].append(dgq[0])
        grads["kv_norm_g"].append(dgkv[0])
        dz = [dfq, dfk, dfv, dfg, dmg, dz_tail]
        dw_in = _weight_grad(h, dz, "dw_in", wg_dtype)
        dx, acc3 = _kb_in(dz, w_in, x_l, norm_g[l][None], mod3, dx)
        grads["norm_g"].append(acc3[2])
        grads["mod"].append(jnp.concatenate([acc3[0], acc3[1], dgate[0]]))
        if shards is not None:
            pending = _grad_slabs(dw_in, dw_uq, dw_ukv)
        else:
            for name, g in (("w_in", dw_in), ("w_uq", dw_uq), ("w_ukv", dw_ukv), ("w_out", dw_out)):
                grads[name].append(g)
    grads = {k: jnp.stack(v[::-1]) for k, v in grads.items() if v}
    grads["final_g"] = dfinal[0]
    if shards is None:
        return loss_row[0, 0], dx, grads
    return loss_row[0, 0], dx, grads, received, pending


def _pack_small(parts, total):
    flat = jnp.concatenate([p.reshape(-1) for p in parts])
    return jnp.pad(flat, (0, total - flat.shape[0])).reshape(total // LANES, LANES)


def kernel(x, c, positions, norm_g, w_ada, b_ada, w_in, b_f, q_norm_g, w_uq, kv_norm_g, w_ukv, w_out, final_g, loss_target, m_norm_g, m_w_ada, m_b_ada, m_w_in, m_b_f, m_q_norm_g, m_w_uq, m_kv_norm_g, m_w_ukv, m_w_out, m_final_g, v_norm_g, v_w_ada, v_b_ada, v_w_in, v_b_f, v_q_norm_g, v_w_uq, v_kv_norm_g, v_w_ukv, v_w_out, v_final_g):
    n_l, d = norm_g.shape
    me = 4 * lax.axis_index("x") + 2 * lax.axis_index("y") + lax.axis_index("c")
    ada_c = w_ada.shape[2]

    cact = jnp.broadcast_to(jax.nn.silu(c), (N_DEV, d))
    shards = [[w[l].astype(MXU) for w in (w_in, w_uq, w_ukv, w_out)] for l in range(n_l)]
    *g_w0, g_cact = _gather_two_level(shards[0] + [cact], "gather_layer0")
    cact_all = g_cact[:, 0, :]

    b_cols = lax.dynamic_slice_in_dim(b_ada, me * ada_c, ada_c, axis=1)[:, None, :]
    modpart = _modpart(cact_all, w_ada, b_cols)
    mod_send = jnp.pad(modpart.transpose(1, 0, 2), ((0, 0), (0, 8 - n_l), (0, 0)))
    (mod_recv,) = _exchange([mod_send], [False], "scatter_mod")
    mod = mod_recv.transpose(1, 0, 2).reshape(8, N_DEV * ada_c)[:n_l]

    loss, dx, gr, received, pending = _local_step(x[0], mod, positions[0], loss_target[0], norm_g, b_f, q_norm_g,
                                                  kv_norm_g, final_g, [_full_weights(*g_w0)], shards)

    small_parts = [gr["norm_g"], gr["mod"], gr["b_f"], gr["q_norm_g"], gr["kv_norm_g"], gr["final_g"], cact[0]]
    sizes = [int(np.prod(p.shape)) for p in small_parts]
    total = -(-sum(sizes) // 1024) * 1024
    small = _pack_small(small_parts, total)
    *received[0][:3], r_small = _exchange(pending + [small], [False, False, False, True], "exchange_layer0")
    r_in, r_uq, r_ukv, r_out = ([received[l][i] for l in range(n_l)] for i in range(4))

    def upd(slabs, w, m, v, name):
        shp = w.shape
        w2, m2, v2 = (a.reshape(-1, slabs[0].shape[2]) for a in (w, m, v))
        return [o.reshape(shp) for o in _adamw(slabs, w2, m2, v2, name)]

    o_in = upd(r_in, w_in, m_w_in, v_w_in, "adamw_w_in")
    o_uq = upd(r_uq, w_uq, m_w_uq, v_w_uq, "adamw_w_uq")
    o_ukv = upd(r_ukv, w_ukv, m_w_ukv, v_w_ukv, "adamw_w_ukv")
    o_out = upd(r_out, w_out, m_w_out, v_w_out, "adamw_w_out")

    offs = np.cumsum([0] + sizes)
    flat_all = r_small.reshape(N_DEV, total)
    dmod_all = flat_all[:, offs[1]:offs[2]].reshape(N_DEV, n_l, 3 * d)
    dmod_cols = lax.dynamic_slice_in_dim(dmod_all, me * ada_c, ada_c, axis=2).transpose(1, 0, 2)
    cact_cols = flat_all[:, offs[6]:offs[7]][:, :, None]
    g_ada = _ada_grad(cact_cols, dmod_cols)
    o_ada = upd([g_ada.reshape(1, n_l * d, ada_c)], w_ada, m_w_ada, v_w_ada, "adamw_w_ada")

    zero_c = jnp.zeros((d,), F32)
    small_w = [_pack_small([norm_g, b_ada, b_f, q_norm_g, kv_norm_g, final_g, zero_c], total),
               _pack_small([m_norm_g, m_b_ada, m_b_f, m_q_norm_g, m_kv_norm_g, m_final_g, zero_c], total),
               _pack_small([v_norm_g, v_b_ada, v_b_f, v_q_norm_g, v_kv_norm_g, v_final_g, zero_c], total)]
    o_small = [o.reshape(-1) for o in _adamw([r_small], *small_w, "adamw_small")]
    shapes = [norm_g.shape, b_ada.shape, b_f.shape, q_norm_g.shape, kv_norm_g.shape, final_g.shape]

    def small_out(kind, idx):
        return o_small[kind][offs[idx]:offs[idx + 1]].reshape(shapes[idx])

    loss_all = lax.psum(loss, ("x", "y", "c"))
    outs = [loss_all, dx[None]]
    for kind in range(4):
        outs += [small_out(kind, 0), o_ada[kind], small_out(kind, 1), o_in[kind], small_out(kind, 2),
                 small_out(kind, 3), o_uq[kind], small_out(kind, 4), o_ukv[kind], o_out[kind], small_out(kind, 5)]
    return tuple(outs)
```

```python
import jax
import jax.numpy as jnp
import numpy as np
from jax import lax
from jax.experimental import pallas as pl
from jax.experimental.pallas import tpu as pltpu

F32 = jnp.float32
MXU = jnp.bfloat16

N_DEV = 8
HEADS = 8
PAIRS = HEADS // 2
HEAD_DIM = 64
NOPE = 64
ROPE = 32
HALF_ROPE = ROPE // 2
Q_LORA = 256
KV_LORA = 128
CHUNK = 64
GROUP_W = HEADS * HEAD_DIM
EPS = 1e-6
ROPE_THETA = 10000.0

Z_FQ, Z_FK, Z_FV, Z_FG, Z_MG, Z_QL, Z_KV, Z_MISC, Z_W = 0, 512, 1024, 1536, 2048, 2560, 2816, 2944, 3072
MISC_FF = ROPE
TAIL_W = Z_W - Z_QL
T_KV, T_MISC = Q_LORA, Q_LORA + KV_LORA

ADAM_LR = 0.001
ADAM_B1 = 0.9
ADAM_B2 = 0.999
ADAM_EPS = 1e-08
ADAM_WD = 0.01
ADAM_STEP = 10

VMEM_LIMIT_V7X = 56 * 1024 * 1024
LANES = 128
ATTN_TILE = 1024
V_ROWS = HEAD_DIM + 16
LOG2E = 1.4426950408889634
FOX_SCALE = HEAD_DIM ** -0.5
MLA_SCALE = (NOPE + ROPE) ** -0.5

_NT = (((1,), (1,)), ((), ()))
_TN = (((0,), (0,)), ((), ()))


def _params(*sem, side_effects=False):
    return pltpu.CompilerParams(dimension_semantics=sem, vmem_limit_bytes=VMEM_LIMIT_V7X,
                                has_side_effects=side_effects)


def _sds(shape, dtype=F32):
    return jax.ShapeDtypeStruct(shape, dtype)


def _full(shape):
    nd = len(shape)
    return pl.BlockSpec(shape, lambda *_: (0,) * nd)


def _rows(tm, width, col=0):
    return pl.BlockSpec((tm, width), lambda i: (i, col))


def _exchange(arrs, gather, name):
    n = len(arrs)

    def kern(*refs):
        copies = _exchange_copies(refs[:n], refs[n:2 * n], gather, *refs[2 * n:])
        _exchange_start(copies)
        _exchange_wait(copies)

    return pl.pallas_call(
        kern, name=name, out_shape=_exchange_out_shapes(arrs, gather),
        in_specs=[pl.BlockSpec(memory_space=pl.ANY)] * n,
        out_specs=[pl.BlockSpec(memory_space=pl.ANY)] * n,
        scratch_shapes=_exchange_sems(n),
        compiler_params=pltpu.CompilerParams(has_side_effects=True),
    )(*arrs)


def _gather_two_level(arrs, name):
    n = len(arrs)

    def kern(*refs):
        ins, outs = refs[:n], refs[n:2 * n]
        send_sems, recv_sems, loc_sems = refs[2 * n:]
        x, y, c = lax.axis_index("x"), lax.axis_index("y"), lax.axis_index("c")
        me, sibling = (x, y, c), (x, y, 1 - c)
        chips = [(1 - x, y), (x, 1 - y), (1 - x, 1 - y)]

        def slot(i, dev):
            return outs[i].at[4 * dev[0] + 2 * dev[1] + dev[2]]

        def copy(i, k, block, to, src=None):
            return pltpu.make_async_remote_copy(
                src_ref=slot(i, block) if src is None else src, dst_ref=slot(i, block), send_sem=send_sems.at[i, k],
                recv_sem=recv_sems.at[i, k], device_id=to, device_id_type=pl.DeviceIdType.MESH)

        mine = [pltpu.make_async_copy(ins[i], slot(i, me), loc_sems.at[i]) for i in range(n)]
        first = [copy(i, 0, me, sibling, src=ins[i]) for i in range(n)]
        first += [copy(i, 1 + j, me, (*chip, c), src=ins[i]) for j, chip in enumerate(chips) for i in range(n)]
        for cp in mine + first:
            cp.start()
        passed = []
        for j, chip in enumerate(chips):
            for i in range(n):
                copy(i, 1 + j, (*chip, c), me).wait_recv()
                fwd = copy(i, 4 + j, (*chip, c), sibling)
                fwd.start()
                passed.append(fwd)
        for i in range(n):
            copy(i, 0, sibling, me).wait_recv()
            for j, chip in enumerate(chips):
                copy(i, 4 + j, (*chip, 1 - c), me).wait_recv()
        for cp in first + passed:
            cp.wait_send()
        for cp in mine:
            cp.wait()

    return pl.pallas_call(
        kern, name=name, out_shape=_exchange_out_shapes(arrs, [True] * n),
        in_specs=[pl.BlockSpec(memory_space=pl.ANY)] * n,
        out_specs=[pl.BlockSpec(memory_space=pl.ANY)] * n,
        scratch_shapes=_exchange_sems(n),
        compiler_params=pltpu.CompilerParams(has_side_effects=True),
    )(*arrs)


def _exchange_out_shapes(arrs, gather):
    return [_sds((N_DEV,) + tuple(a.shape) if g else tuple(a.shape), a.dtype) for a, g in zip(arrs, gather)]


def _exchange_sems(n):
    return [pltpu.SemaphoreType.DMA((n, N_DEV)), pltpu.SemaphoreType.DMA((n, N_DEV)), pltpu.SemaphoreType.DMA((n,))]


def _exchange_copies(ins, outs, gather, send_sems, recv_sems, loc_sems, recv=True):
    n = len(ins)
    x, y, c = lax.axis_index("x"), lax.axis_index("y"), lax.axis_index("c")
    me = 4 * x + 2 * y + c

    def src(i, j):
        return ins[i] if gather[i] else ins[i].at[j]

    local = [pltpu.make_async_copy(src(i, me), outs[i].at[me], loc_sems.at[i]) for i in range(n)]
    sends, recvs = [], []
    for k in range(1, N_DEV):
        px = 1 - x if k & 4 else x
        py = 1 - y if k & 2 else y
        pc = 1 - c if k & 1 else c
        p = 4 * px + 2 * py + pc
        for i in range(n):
            sends.append(pltpu.make_async_remote_copy(
                src_ref=src(i, p), dst_ref=outs[i].at[me], send_sem=send_sems.at[i, k],
                recv_sem=recv_sems.at[i, k], device_id=(px, py, pc), device_id_type=pl.DeviceIdType.MESH))
            if recv:
                recvs.append(pltpu.make_async_remote_copy(
                    src_ref=src(i, p), dst_ref=outs[i].at[p], send_sem=send_sems.at[i, k],
                    recv_sem=recv_sems.at[i, k], device_id=(px, py, pc), device_id_type=pl.DeviceIdType.MESH))
    return local, sends, recvs


def _exchange_start(copies):
    local, sends, _ = copies
    for cp in local + sends:
        cp.start()


def _exchange_wait(copies):
    local, sends, recvs = copies
    for cp in recvs:
        cp.wait_recv()
    for cp in sends:
        cp.wait_send()
    for cp in local:
        cp.wait()


def _modpart(cact8, w_ada, b_cols):
    n_l, d, cw = w_ada.shape

    def kern(c_ref, w_ref, b_ref, o_ref):
        o_ref[0] = jnp.dot(c_ref[...].astype(MXU), w_ref[0].astype(MXU), preferred_element_type=F32) + b_ref[0]

    return pl.pallas_call(
        kern, name="modpart", grid=(n_l,), out_shape=_sds((n_l, N_DEV, cw)),
        in_specs=[_full((N_DEV, d)), pl.BlockSpec((1, d, cw), lambda l: (l, 0, 0)),
                  pl.BlockSpec((1, 1, cw), lambda l: (l, 0, 0))],
        out_specs=pl.BlockSpec((1, N_DEV, cw), lambda l: (l, 0, 0)),
        compiler_params=_params("arbitrary"),
    )(cact8, w_ada, b_cols)


def _ada_grad(cact_cols, dmod_cols):
    n_l, _, cw = dmod_cols.shape
    d = cact_cols.shape[1]

    def kern(c_ref, dm_ref, o_ref):
        acc = c_ref[0] * dm_ref[0, 0:1, :]
        for s in range(1, N_DEV):
            acc = acc + c_ref[s] * dm_ref[0, s:s + 1, :]
        o_ref[0] = acc

    return pl.pallas_call(
        kern, name="ada_grad", grid=(n_l,), out_shape=_sds((n_l, d, cw)),
        in_specs=[_full((N_DEV, d, 1)), pl.BlockSpec((1, N_DEV, cw), lambda l: (l, 0, 0))],
        out_specs=pl.BlockSpec((1, d, cw), lambda l: (l, 0, 0)),
        compiler_params=_params("arbitrary"),
    )(cact_cols, dmod_cols)


def _k_in(x, g, mod3, w, tm=256):
    s_len, d = x.shape
    qkv_w = 3 * GROUP_W

    def kern(x_ref, g_ref, mod_ref, w_ref, h_ref, qkv_ref, gates_ref, tail_ref):
        xv = x_ref[...]
        r = lax.rsqrt(jnp.mean(xv * xv, axis=-1, keepdims=True) + EPS)
        xn = xv * r * g_ref[...]
        h = (xn * (1.0 + mod_ref[1:2, :]) + mod_ref[0:1, :]).astype(MXU)
        h_ref[...] = h
        z = jnp.dot(h, w_ref[...], preferred_element_type=F32)
        qkv_ref[:, :GROUP_W] = (z[:, Z_FQ:Z_FQ + GROUP_W] * (FOX_SCALE * LOG2E)).astype(MXU)
        qkv_ref[:, GROUP_W:] = z[:, Z_FK:Z_FK + 2 * GROUP_W].astype(MXU)
        gates_ref[...] = z[:, Z_FG:Z_QL]
        tail_ref[...] = z[:, Z_QL:]

    return pl.pallas_call(
        kern, name="k_in", grid=(s_len // tm,),
        out_shape=[_sds((s_len, d), MXU), _sds((s_len, qkv_w), MXU), _sds((s_len, Z_QL - Z_FG)),
                   _sds((s_len, TAIL_W))],
        in_specs=[_rows(tm, d), _full((1, d)), _full((3, d)), _full((d, Z_W))],
        out_specs=[_rows(tm, d), _rows(tm, qkv_w), _rows(tm, Z_QL - Z_FG), _rows(tm, TAIL_W)],
        compiler_params=_params("arbitrary"),
    )(x, g, mod3, w)


def _scan_matrices(rows, chunks, reverse):
    r_i = lax.broadcasted_iota(jnp.int32, (LANES, LANES), 0)
    c_i = lax.broadcasted_iota(jnp.int32, (LANES, LANES), 1)
    a_i = lax.broadcasted_iota(jnp.int32, (rows, rows), 0)
    b_i = lax.broadcasted_iota(jnp.int32, (rows, rows), 1)
    same_head = (a_i // chunks) == (b_i // chunks)
    if reverse:
        return (r_i >= c_i).astype(F32), (same_head & (b_i > a_i)).astype(F32)
    return (r_i <= c_i).astype(F32), (same_head & (b_i < a_i)).astype(F32)


def _scan_rows(x, inner, outer):
    tot = jnp.broadcast_to(jnp.sum(x, axis=1, keepdims=True), x.shape)
    return (jnp.dot(x, inner, precision=lax.Precision.HIGHEST, preferred_element_type=F32)
            + jnp.dot(outer, tot, precision=lax.Precision.HIGHEST, preferred_element_type=F32))


def _k_cum(ff_rows, b_rows, chunks):
    rows = ff_rows.shape[0]

    def kern(ff_ref, b_ref, cum_ref):
        xc = ff_ref[...] + b_ref[...]
        lf = jnp.minimum(xc, 0.0) - jnp.log(1.0 + jnp.exp(-jnp.abs(xc)))
        cum_ref[...] = _scan_rows(lf, *_scan_matrices(rows, chunks, False))

    return pl.pallas_call(
        kern, name="k_cum", out_shape=_sds((rows, LANES)),
        in_specs=[pl.BlockSpec(memory_space=pltpu.VMEM)] * 2,
        out_specs=pl.BlockSpec(memory_space=pltpu.VMEM),
        compiler_params=_params(),
    )(ff_rows, b_rows)


def _k_cum_bwd(dc_rows, ff_rows, b_rows, chunks):
    rows = ff_rows.shape[0]

    def kern(dc_ref, ff_ref, b_ref, dff_ref, db_ref):
        dlf = _scan_rows(dc_ref[...], *_scan_matrices(rows, chunks, True))
        dff = dlf * jax.nn.sigmoid(-(ff_ref[...] + b_ref[...]))
        dff_ref[...] = dff
        db_ref[...] = jnp.broadcast_to(jnp.sum(dff, axis=1, keepdims=True), dff.shape)

    return pl.pallas_call(
        kern, name="k_cum_bwd", out_shape=[_sds((rows, LANES)), _sds((rows, LANES))],
        in_specs=[pl.BlockSpec(memory_space=pltpu.VMEM)] * 3,
        out_specs=[pl.BlockSpec(memory_space=pltpu.VMEM)] * 2,
        compiler_params=_params(),
    )(dc_rows, ff_rows, b_rows)


def _swap16(t):
    lane = lax.broadcasted_iota(jnp.int32, t.shape, 1)
    return jnp.where(lane % ROPE < HALF_ROPE, pltpu.roll(t, LANES - HALF_ROPE, 1), pltpu.roll(t, HALF_ROPE, 1))


def _rope(t, cos, sin):
    return t * cos + _swap16(t) * sin


def _rope_bwd(dt, cos, sin):
    return dt * cos - _swap16(dt) * sin


def _k_prep(tail, cos, sin, gq, gkv, wuq, wukv, tm=512):
    s_len = tail.shape[0]
    qc = MLA_SCALE * LOG2E

    def kern(tail_ref, cos_ref, sin_ref, gq_ref, gkv_ref, wuq_ref, wukv_ref,
             qn_out, qr_out, kn_out, v_out, kr_out, qn_ref, kvn_ref):
        cs, sn = cos_ref[...], sin_ref[...]
        ql = tail_ref[:, :T_KV]
        rq = lax.rsqrt(jnp.mean(ql * ql, axis=-1, keepdims=True) + EPS)
        qn = (ql * rq * gq_ref[...]).astype(MXU)
        qn_ref[...] = qn
        q = jnp.dot(qn, wuq_ref[...], preferred_element_type=F32)
        qn_out[...] = (q[:, :GROUP_W] * qc).astype(MXU)
        for blk in range(PAIRS):
            lo = GROUP_W + blk * LANES
            qr_out[:, blk * LANES:(blk + 1) * LANES] = (_rope(q[:, lo:lo + LANES], cs, sn) * qc).astype(MXU)
        kvl = tail_ref[:, T_KV:T_MISC]
        rk = lax.rsqrt(jnp.mean(kvl * kvl, axis=-1, keepdims=True) + EPS)
        kvn = (kvl * rk * gkv_ref[...]).astype(MXU)
        kvn_ref[...] = kvn
        kv = jnp.dot(kvn, wukv_ref[...], preferred_element_type=F32)
        kn_out[...] = kv[:, :GROUP_W].astype(MXU)
        v_out[...] = kv[:, GROUP_W:].astype(MXU)
        misc = tail_ref[:, T_MISC:]
        lane = lax.broadcasted_iota(jnp.int32, misc.shape, 1)
        kr = jnp.where(lane < ROPE, _rope(misc, cs, sn), 0.0)
        kr_out[...] = (kr + pltpu.roll(kr, HEAD_DIM, 1)).astype(MXU)

    return pl.pallas_call(
        kern, name="k_prep", grid=(s_len // tm,),
        out_shape=[_sds((s_len, GROUP_W), MXU), _sds((s_len, GROUP_W), MXU), _sds((s_len, GROUP_W), MXU),
                   _sds((s_len, GROUP_W), MXU), _sds((s_len, LANES), MXU), _sds((s_len, Q_LORA), MXU),
                   _sds((s_len, KV_LORA), MXU)],
        in_specs=[_rows(tm, TAIL_W), _rows(tm, LANES), _rows(tm, LANES),
                  _full((1, Q_LORA)), _full((1, KV_LORA)), _full((Q_LORA, 2 * GROUP_W)),
                  _full((KV_LORA, 2 * GROUP_W))],
        out_specs=[_rows(tm, GROUP_W), _rows(tm, GROUP_W), _rows(tm, GROUP_W), _rows(tm, GROUP_W), _rows(tm, LANES),
                   _rows(tm, Q_LORA), _rows(tm, KV_LORA)],
        compiler_params=_params("arbitrary"),
    )(tail, cos, sin, gq, gkv, wuq, wukv)


def _block_mask(kn, qn, q_off, chunk_mask, transposed):
    shape = (kn, qn) if transposed else (qn, kn)
    row = lax.broadcasted_iota(jnp.int32, shape, 0)
    col = lax.broadcasted_iota(jnp.int32, shape, 1)
    qi, ki = (col + q_off, row) if transposed else (row + q_off, col)
    if chunk_mask:
        return (ki // CHUNK) <= (qi // CHUNK)
    return ki <= qi


def _head_operand(x, hh, other=None):
    lane = lax.broadcasted_iota(jnp.int32, x.shape, 1)
    own = (lane >= hh * HEAD_DIM) & (lane < (hh + 1) * HEAD_DIM)
    return jnp.where(own, x, jnp.zeros_like(x) if other is None else other)


def _attention_fwd(q, q_blk, k, k_blk, v, v_blk, bias, rope, chunk_mask, name, side=None):
    s_len = q.shape[0]
    t = min(ATTN_TILE, s_len // 2)
    nq = s_len // t
    n_side = len(side[0]) if side else 0

    def kern(*refs):
        q_ref, k_ref, v_ref = refs[:3]
        pos = 3
        if bias is not None:
            ck_ref = refs[pos]
            pos += 1
        if rope is not None:
            qr_ref, kr_ref = refs[pos:pos + 2]
            pos += 2
        side_in = refs[pos:pos + n_side]
        pos += n_side
        o_ref, lse_ref = refs[pos:pos + 2]
        side_out = refs[pos + 2:pos + 2 + n_side]
        vt_scr, m_scr, acc_scr, ck_scr = refs[pos + 2 + n_side:pos + 6 + n_side]
        sems = refs[pos + 6 + n_side:]
        pj = pl.program_id(0)
        if n_side:
            @pl.when(pj == 0)
            def _():
                _exchange_start(_exchange_copies(side_in, side_out, side[1], *sems, recv=False))
        vt_scr[:, HEAD_DIM:, :] = jnp.ones((2, V_ROWS - HEAD_DIM, s_len), vt_scr.dtype)
        for i in range(nq):
            vtt = v_ref[i * t:(i + 1) * t, :].T
            for hh in range(2):
                vt_scr[hh, :HEAD_DIM, i * t:(i + 1) * t] = vtt[hh * HEAD_DIM:(hh + 1) * HEAD_DIM, :]
                if bias is not None:
                    ckt = ck_ref[i * t:(i + 1) * t, :]
                    lane = lax.broadcasted_iota(jnp.int32, ckt.shape, 1)
                    ck_scr[hh, i * t:(i + 1) * t, :] = jnp.sum(jnp.where(lane == 2 * pj + hh, ckt, 0.0), axis=1,
                                                               keepdims=True)

        def qbody(qi, _):
            qs = pl.multiple_of(qi * t, t)
            qt = q_ref[pl.ds(qs, t), :]
            qrt = qr_ref[pl.ds(qs, t), :] if rope is not None else None
            qh = [_head_operand(qt, hh, qrt) for hh in range(2)]
            m_scr[...] = jnp.full(m_scr.shape, -jnp.inf, F32)
            acc_scr[...] = jnp.zeros(acc_scr.shape, F32)

            def block(ks, kn, q0, qn, masked):
                kt = k_ref[pl.ds(ks, kn), :]
                kh = [_head_operand(kt, hh, kr_ref[pl.ds(ks, kn), :]) for hh in range(2)] if rope is not None else [kt, kt]
                qc = slice(q0, q0 + qn)
                sts = [lax.dot_general(kh[hh], qh[hh][qc], _NT, preferred_element_type=F32) for hh in range(2)]
                if bias is not None:
                    sts = [sts[hh] - ck_scr[hh, pl.ds(ks, kn), :] for hh in range(2)]
                if masked:
                    sts = [jnp.where(_block_mask(kn, qn, q0, chunk_mask, True), st, -jnp.inf) for st in sts]
                m_old = [m_scr[hh, :, qc] for hh in range(2)]
                m_new = [jnp.maximum(m_old[hh], jnp.max(sts[hh], axis=0, keepdims=True)) for hh in range(2)]
                pts = [jnp.exp2(sts[hh] - m_new[hh]).astype(MXU) for hh in range(2)]
                for hh in range(2):
                    alpha = jnp.exp2(m_old[hh] - m_new[hh])
                    acc_scr[hh, :, qc] = alpha * acc_scr[hh, :, qc] + jnp.dot(vt_scr[hh, :, pl.ds(ks, kn)], pts[hh],
                                                                            preferred_element_type=F32)
                    m_scr[hh, :, qc] = m_new[hh]

            def loop_body(ki, carry):
                block(pl.multiple_of(ki * t, t), t, 0, t, False)
                return carry

            lax.fori_loop(0, qi, loop_body, 0)
            block(qs, t, 0, t, True)
            outs = []
            for hh in range(2):
                acc = acc_scr[hh]
                l = acc[HEAD_DIM:HEAD_DIM + 1, :]
                outs.append(acc[:HEAD_DIM, :] / l)
                lse_ref[hh, :, pl.ds(qs, t)] = m_scr[hh] + jnp.log2(l)
            o_ref[pl.ds(qs, t), :] = jnp.concatenate(outs, axis=0).T
            return 0

        lax.fori_loop(0, nq, qbody, 0)
        if n_side:
            @pl.when(pj == PAIRS - 1)
            def _():
                _exchange_wait(_exchange_copies(side_in, side_out, side[1], *sems))

    def tok(blk):
        return pl.BlockSpec((s_len, LANES), lambda j: (0, blk + j))

    rowb = pl.BlockSpec((2, 1, s_len), lambda j: (j, 0, 0))
    hbm = pl.BlockSpec(memory_space=pl.ANY)
    ins = [q, k, v]
    in_specs = [tok(q_blk), tok(k_blk), tok(v_blk)]
    if bias is not None:
        ins.append(bias)
        in_specs.append(_full((s_len, LANES)))
    if rope is not None:
        ins += list(rope)
        in_specs += [tok(0), _full((s_len, LANES))]
    out_shape = [_sds((s_len, PAIRS * LANES)), _sds((HEADS, 1, s_len))]
    scratch = [pltpu.VMEM((2, V_ROWS, s_len), v.dtype), pltpu.VMEM((2, 1, t), F32), pltpu.VMEM((2, V_ROWS, t), F32),
               pltpu.VMEM((2, s_len if bias is not None else 8, 1), F32)]
    if n_side:
        ins += list(side[0])
        out_shape += _exchange_out_shapes(*side)
        scratch += _exchange_sems(n_side)
    return pl.pallas_call(
        kern, name=name, grid=(PAIRS,), out_shape=out_shape,
        in_specs=in_specs + [hbm] * n_side, out_specs=[tok(0), rowb] + [hbm] * n_side,
        scratch_shapes=scratch,
        compiler_params=_params("arbitrary", side_effects=bool(n_side)),
    )(*ins)


def _attention_bwd(q, q_blk, k, k_blk, v, v_blk, do, pack, ck_row, rope, chunk_mask, q_scale, k_scale, name,
                   side=None):
    s_len = q.shape[0]
    t = min(ATTN_TILE, s_len // 2)
    nq = s_len // t
    has_bias = ck_row is not None
    nv = 2 if rope is not None else 1
    n_side = len(side[0]) if side else 0

    def kern(*refs):
        q_ref, k_ref, v_ref, do_ref, pack_ref = refs[:5]
        pos = 5
        if has_bias:
            ck_ref = refs[pos]
            pos += 1
        if rope is not None:
            qr_ref, kr_ref = refs[pos:pos + 2]
            pos += 2
        side_in = refs[pos:pos + n_side]
        pos += n_side
        dq_ref, dk_ref, dv_ref = refs[pos:pos + 3]
        pos += 3
        if has_bias:
            dcq_ref, dck_ref = refs[pos:pos + 2]
            pos += 2
        if rope is not None:
            dqr_ref, dkr_ref = refs[pos:pos + 2]
            pos += 2
        side_out = refs[pos:pos + n_side]
        pos += n_side
        qt_scr, dot_scr, dkt_scr, dvt_scr, dq_scr, dcq_scr = refs[pos:pos + 6]
        sems = refs[pos + 6:]
        pj = pl.program_id(0)
        if n_side:
            @pl.when(pj == 0)
            def _():
                _exchange_start(_exchange_copies(side_in, side_out, side[1], *sems, recv=False))

        for i in range(nq):
            sl = slice(i * t, (i + 1) * t)
            dot_scr[:, sl] = do_ref[sl, :].T
            if rope is not None:
                for hh in range(2):
                    qt_scr[hh, :, sl] = _head_operand(q_ref[sl, :], hh, qr_ref[sl, :]).T
            else:
                qt_scr[0, :, sl] = q_ref[sl, :].T
        dkt_scr[...] = jnp.zeros(dkt_scr.shape, F32)
        dvt_scr[...] = jnp.zeros(dvt_scr.shape, F32)
        if has_bias:
            dck_ref[...] = jnp.zeros(dck_ref.shape, F32)

            @pl.when(pj == 0)
            def _():
                dcq_ref[...] = jnp.zeros(dcq_ref.shape, F32)

        def qbody(qi, _):
            qs = pl.multiple_of(qi * t, t)
            qt = q_ref[pl.ds(qs, t), :]
            qrt = qr_ref[pl.ds(qs, t), :] if rope is not None else None
            dot = do_ref[pl.ds(qs, t), :]
            pk = pack_ref[pl.ds(qs, t), :]
            lane = lax.broadcasted_iota(jnp.int32, pk.shape, 1)
            qh = [_head_operand(qt, hh, qrt) for hh in range(2)]
            doh = [_head_operand(dot, hh) for hh in range(2)]
            a_col = [jnp.sum(jnp.where(lane == 2 * pj + hh, pk, 0.0), axis=1, keepdims=True) for hh in range(2)]
            d_col = [jnp.sum(jnp.where(lane == HEADS + 2 * pj + hh, pk, 0.0), axis=1, keepdims=True)
                     for hh in range(2)]
            dq_scr[...] = jnp.zeros(dq_scr.shape, F32)
            if has_bias:
                dcq_scr[...] = jnp.zeros(dcq_scr.shape, F32)

            def block(ks, kn, q0, qn, masked):
                kt = k_ref[pl.ds(ks, kn), :]
                krt = kr_ref[pl.ds(ks, kn), :] if rope is not None else None
                kh = [_head_operand(kt, hh, krt) for hh in range(2)]
                vt = v_ref[pl.ds(ks, kn), :]
                qr_ = slice(q0, q0 + qn)
                qcols = pl.ds(pl.multiple_of(qs + q0, t // 2), qn)
                ss = [lax.dot_general(qh[hh][qr_], kh[hh] if rope is not None else kt, _NT,
                                      preferred_element_type=F32) + a_col[hh][qr_] for hh in range(2)]
                if has_bias:
                    ss = [ss[hh] - ck_ref[hh, :, pl.ds(ks, kn)] for hh in range(2)]
                dpds = [lax.dot_general(doh[hh][qr_], vt, _NT, preferred_element_type=F32) for hh in range(2)]
                ps = [jnp.exp2(s) for s in ss]
                if masked:
                    ps = [jnp.where(_block_mask(kn, qn, q0, chunk_mask, False), p, 0.0) for p in ps]
                dss = [ps[hh] * (dpds[hh] - d_col[hh][qr_]) for hh in range(2)]
                for hh in range(2):
                    rows = slice(hh * HEAD_DIM, (hh + 1) * HEAD_DIM)
                    dsb = dss[hh].astype(MXU)
                    dvt_scr[rows, pl.ds(ks, kn)] += jnp.dot(dot_scr[rows, qcols], ps[hh].astype(MXU),
                                                            preferred_element_type=F32)
                    if rope is not None:
                        dkt_scr[hh, :, pl.ds(ks, kn)] += jnp.dot(qt_scr[hh, :, qcols], dsb,
                                                                 preferred_element_type=F32)
                    else:
                        dkt_scr[0, rows, pl.ds(ks, kn)] += jnp.dot(qt_scr[0, rows, qcols], dsb,
                                                                   preferred_element_type=F32)
                    dq_scr[hh if rope is not None else 0, qr_, :] += jnp.dot(dsb, kh[hh], preferred_element_type=F32)
                    if has_bias:
                        dcq_scr[hh, qr_, :] += jnp.sum(dss[hh], axis=1, keepdims=True)
                        dck_ref[hh, :, pl.ds(ks, kn)] += -jnp.sum(dss[hh], axis=0, keepdims=True)

            def loop_body(ki, carry):
                block(pl.multiple_of(ki * t, t), t, 0, t, False)
                return carry

            lax.fori_loop(0, qi, loop_body, 0)
            block(qs, t // 2, 0, t // 2, True)
            block(qs, t, t // 2, t // 2, True)
            if rope is not None:
                first = lane < HEAD_DIM
                dq_ref[pl.ds(qs, t), :] = (jnp.where(first, dq_scr[0], dq_scr[1]) * q_scale).astype(dq_ref.dtype)
                dqr_ref[pl.ds(qs, t), :] = jnp.where(first, dq_scr[1], dq_scr[0]) * q_scale
            else:
                dq_ref[pl.ds(qs, t), :] = (dq_scr[0] * q_scale).astype(dq_ref.dtype)
            if has_bias:
                old = dcq_ref[pl.ds(qs, t), :]
                dcq_ref[pl.ds(qs, t), :] = jnp.where(lane == 2 * pj, dcq_scr[0],
                                                     jnp.where(lane == 2 * pj + 1, dcq_scr[1], old))
            return 0

        lax.fori_loop(0, nq, qbody, 0)
        for i in range(nq):
            sl = slice(i * t, (i + 1) * t)
            dv_ref[sl, :] = dvt_scr[:, sl].T.astype(dv_ref.dtype)
            if rope is not None:
                d0, d1 = dkt_scr[0, :, sl], dkt_scr[1, :, sl]
                first = lax.broadcasted_iota(jnp.int32, d0.shape, 0) < HEAD_DIM
                dk_ref[sl, :] = (jnp.where(first, d0, d1).T * k_scale).astype(dk_ref.dtype)
                dkr_ref[0, sl, :] = jnp.where(first, d1, d0).T * k_scale
            else:
                dk_ref[sl, :] = (dkt_scr[0, :, sl].T * k_scale).astype(dk_ref.dtype)
        if n_side:
            @pl.when(pj == PAIRS - 1)
            def _():
                _exchange_wait(_exchange_copies(side_in, side_out, side[1], *sems))

    def tok(blk):
        return pl.BlockSpec((s_len, LANES), lambda j: (0, blk + j))

    shared = _full((s_len, LANES))
    rowb = pl.BlockSpec((2, 1, s_len), lambda j: (j, 0, 0))
    slab = pl.BlockSpec((1, s_len, LANES), lambda j: (j, 0, 0))
    hbm = pl.BlockSpec(memory_space=pl.ANY)
    ins = [q, k, v, do, pack]
    in_specs = [tok(q_blk), tok(k_blk), tok(v_blk), tok(0), shared]
    out_shape = [_sds((s_len, PAIRS * LANES), MXU)] * 3
    out_specs = [tok(0)] * 3
    if has_bias:
        ins.append(ck_row)
        in_specs.append(rowb)
        out_shape += [_sds((s_len, LANES)), _sds((HEADS, 1, s_len))]
        out_specs += [shared, rowb]
    if rope is not None:
        ins += list(rope)
        in_specs += [tok(0), shared]
        out_shape += [_sds((s_len, PAIRS * LANES)), _sds((PAIRS, s_len, LANES))]
        out_specs += [tok(0), slab]
    scratch = [pltpu.VMEM((nv, LANES, s_len), q.dtype), pltpu.VMEM((LANES, s_len), do.dtype),
               pltpu.VMEM((nv, LANES, s_len), F32), pltpu.VMEM((LANES, s_len), F32),
               pltpu.VMEM((nv, t, LANES), F32), pltpu.VMEM((2, t, 1), F32)]
    if n_side:
        ins += list(side[0])
        out_shape += _exchange_out_shapes(*side)
        scratch += _exchange_sems(n_side)
    return pl.pallas_call(
        kern, name=name, grid=(PAIRS,), out_shape=out_shape,
        in_specs=in_specs + [hbm] * n_side, out_specs=out_specs + [hbm] * n_side, scratch_shapes=scratch,
        compiler_params=_params("arbitrary", side_effects=bool(n_side)),
    )(*ins)


def _silu(a):
    return a * jax.nn.sigmoid(a)


def _k_out(of, om, gates, x, gate, wout, tm=256):
    s_len, d = x.shape

    def kern(of_ref, om_ref, gates_ref, x_ref, gate_ref, w_ref, xo_ref, y_ref, u_ref):
        u_ref[:, :GROUP_W] = (of_ref[...] * _silu(gates_ref[:, :GROUP_W])).astype(MXU)
        u_ref[:, GROUP_W:] = (om_ref[...] * _silu(gates_ref[:, GROUP_W:])).astype(MXU)
        y = jnp.dot(u_ref[...], w_ref[...], preferred_element_type=F32)
        y_ref[...] = y
        xo_ref[...] = x_ref[...] + gate_ref[...] * y

    return pl.pallas_call(
        kern, name="k_out", grid=(s_len // tm,),
        out_shape=[_sds((s_len, d)), _sds((s_len, d)), _sds((s_len, 2 * GROUP_W), MXU)],
        in_specs=[_rows(tm, GROUP_W), _rows(tm, GROUP_W), _rows(tm, 2 * GROUP_W), _rows(tm, d), _full((1, d)),
                  _full((2 * GROUP_W, d))],
        out_specs=[_rows(tm, d), _rows(tm, d), _rows(tm, 2 * GROUP_W)],
        compiler_params=_params("arbitrary"),
    )(of, om, gates, x, gate, wout)


def _k_loss(x, gf, tgt, tm=256):
    s_len, d = x.shape

    def kern(x_ref, g_ref, t_ref, loss_ref, dx_ref, dg_ref):
        i = pl.program_id(0)
        xv = x_ref[...]
        r = lax.rsqrt(jnp.mean(xv * xv, axis=-1, keepdims=True) + EPS)
        xh = xv * r
        diff = xh * g_ref[...] - t_ref[...]
        part = 0.5 * jnp.sum(jnp.mean(diff * diff, axis=-1, keepdims=True))
        dout = diff * (1.0 / d)
        dxh = dout * g_ref[...]
        dx_ref[...] = r * (dxh - xh * jnp.mean(dxh * xh, axis=-1, keepdims=True))

        @pl.when(i == 0)
        def _():
            loss_ref[...] = jnp.zeros_like(loss_ref)
            dg_ref[...] = jnp.zeros_like(dg_ref)

        loss_ref[...] += jnp.full(loss_ref.shape, part, F32)
        dg_ref[...] += jnp.sum(dout * xh, axis=0, keepdims=True)

    return pl.pallas_call(
        kern, name="k_loss", grid=(s_len // tm,),
        out_shape=[_sds((1, LANES)), _sds((s_len, d)), _sds((1, d))],
        in_specs=[_rows(tm, d), _full((1, d)), _rows(tm, d)],
        out_specs=[_full((1, LANES)), _rows(tm, d), _full((1, d))],
        compiler_params=_params("arbitrary"),
    )(x, gf, tgt)


def _kb_out(dxo, y, u, gate, wout, of, om, gates, dw_dtype, tm=256):
    s_len, d = dxo.shape
    steps = s_len // tm

    def kern(dxo_ref, y_ref, u_ref, gate_ref, wt_ref, of_ref, om_ref, gates_ref,
             dof_ref, dom_ref, dfg_ref, dmg_ref, dlf_ref, dlm_ref, dgate_ref, dw_ref, dw_acc):
        i = pl.program_id(0)
        dxv = dxo_ref[...]

        @pl.when(i == 0)
        def _():
            dgate_ref[...] = jnp.zeros_like(dgate_ref)
            dw_acc[...] = jnp.zeros_like(dw_acc)

        dgate_ref[...] += jnp.sum(dxv * y_ref[...], axis=0, keepdims=True)
        dyb = (dxv * gate_ref[...]).astype(MXU)
        dw_acc[...] += lax.dot_general(u_ref[...], dyb, _TN, preferred_element_type=F32)

        @pl.when(i == steps - 1)
        def _():
            dw_ref[...] = dw_acc[...].astype(dw_ref.dtype)

        du = lax.dot_general(dyb, wt_ref[...], _NT, preferred_element_type=F32)
        head_of = (lax.broadcasted_iota(jnp.int32, (GROUP_W, LANES), 0) // HEAD_DIM
                   == lax.broadcasted_iota(jnp.int32, (GROUP_W, LANES), 1)).astype(F32)
        for du_g, o_ref, a, do_ref, dg_ref, dl_ref in (
                (du[:, :GROUP_W], of_ref, gates_ref[:, :GROUP_W], dof_ref, dfg_ref, dlf_ref),
                (du[:, GROUP_W:], om_ref, gates_ref[:, GROUP_W:], dom_ref, dmg_ref, dlm_ref)):
            sg = jax.nn.sigmoid(a)
            ov = o_ref[...]
            dov = du_g * (a * sg)
            do_ref[...] = dov.astype(MXU)
            dg_ref[...] = (du_g * ov * (sg * (1.0 + a * (1.0 - sg)))).astype(MXU)
            dl_ref[...] = jnp.dot(dov * ov, head_of, precision=lax.Precision.HIGH, preferred_element_type=F32)

    return pl.pallas_call(
        kern, name="kb_out", grid=(steps,),
        out_shape=[_sds((s_len, GROUP_W), MXU), _sds((s_len, GROUP_W), MXU),
                   _sds((s_len, GROUP_W), MXU), _sds((s_len, GROUP_W), MXU), _sds((s_len, LANES)),
                   _sds((s_len, LANES)), _sds((1, d)), _sds((2 * GROUP_W, d), dw_dtype)],
        in_specs=[_rows(tm, d), _rows(tm, d), _rows(tm, 2 * GROUP_W), _full((1, d)), _full((2 * GROUP_W, d)),
                  _rows(tm, GROUP_W), _rows(tm, GROUP_W), _rows(tm, 2 * GROUP_W)],
        out_specs=[_rows(tm, GROUP_W), _rows(tm, GROUP_W), _rows(tm, GROUP_W),
                   _rows(tm, GROUP_W), _rows(tm, LANES), _rows(tm, LANES), _full((1, d)), _full((2 * GROUP_W, d))],
        scratch_shapes=[pltpu.VMEM((2 * GROUP_W, d), F32)],
        compiler_params=_params("arbitrary"),
    )(dxo, y, u, gate, wout, of, om, gates)


def _kb_prep(dqn, dqr, dkn, dv, dkr, dff, tail, qn, kvn, cos, sin, gq, gkv, wuq_t, wukv_t, dw_dtype, tm=512):
    s_len = tail.shape[0]
    qw = 2 * GROUP_W
    steps = s_len // tm

    def kern(dqn_ref, dqr_ref, dkn_ref, dv_ref, dkr_ref, dff_ref, tail_ref, qn_ref, kvn_ref, cos_ref, sin_ref,
             gq_ref, gkv_ref, wuqt_ref, wukvt_ref, dz_ref, dgq_ref, dgkv_ref, dwuq_ref, dwukv_ref,
             dq_ref, uq_acc, ukv_acc):
        i = pl.program_id(0)

        @pl.when(i == 0)
        def _():
            dgq_ref[...] = jnp.zeros_like(dgq_ref)
            dgkv_ref[...] = jnp.zeros_like(dgkv_ref)
            uq_acc[...] = jnp.zeros_like(uq_acc)
            ukv_acc[...] = jnp.zeros_like(ukv_acc)

        cs, sn = cos_ref[...], sin_ref[...]
        dq_ref[:, :GROUP_W] = dqn_ref[...]
        for blk in range(PAIRS):
            sl = slice(blk * LANES, (blk + 1) * LANES)
            dq_ref[:, GROUP_W + blk * LANES:GROUP_W + (blk + 1) * LANES] = _rope_bwd(dqr_ref[:, sl], cs, sn).astype(MXU)
        dqn = lax.dot_general(dq_ref[...], wuqt_ref[...], _NT, preferred_element_type=F32)
        uq_acc[...] += lax.dot_general(qn_ref[...], dq_ref[...], _TN, preferred_element_type=F32)
        ukv_acc[:, :GROUP_W] += lax.dot_general(kvn_ref[...], dkn_ref[...], _TN, preferred_element_type=F32)
        ukv_acc[:, GROUP_W:] += lax.dot_general(kvn_ref[...], dv_ref[...], _TN, preferred_element_type=F32)

        @pl.when(i == steps - 1)
        def _():
            dwuq_ref[...] = uq_acc[...].astype(dwuq_ref.dtype)
            dwukv_ref[...] = ukv_acc[...].astype(dwukv_ref.dtype)

        ql = tail_ref[:, :T_KV]
        rq = lax.rsqrt(jnp.mean(ql * ql, axis=-1, keepdims=True) + EPS)
        qh = ql * rq
        dgq_ref[...] += jnp.sum(dqn * qh, axis=0, keepdims=True)
        dqh = dqn * gq_ref[...]
        dz_ref[:, :Q_LORA] = (rq * (dqh - qh * jnp.mean(dqh * qh, axis=-1, keepdims=True))).astype(MXU)

        dkvn = (lax.dot_general(dkn_ref[...], wukvt_ref[:, :GROUP_W], _NT, preferred_element_type=F32)
                + lax.dot_general(dv_ref[...], wukvt_ref[:, GROUP_W:], _NT, preferred_element_type=F32))
        kvl = tail_ref[:, T_KV:T_MISC]
        rk = lax.rsqrt(jnp.mean(kvl * kvl, axis=-1, keepdims=True) + EPS)
        kh = kvl * rk
        dgkv_ref[...] += jnp.sum(dkvn * kh, axis=0, keepdims=True)
        dkh = dkvn * gkv_ref[...]
        dz_ref[:, Q_LORA:Q_LORA + KV_LORA] = (
            rk * (dkh - kh * jnp.mean(dkh * kh, axis=-1, keepdims=True))).astype(MXU)

        g = dkr_ref[...] + pltpu.roll(dkr_ref[...], HEAD_DIM, 1)
        lane = lax.broadcasted_iota(jnp.int32, g.shape, 1)
        dmisc = jnp.where(lane < ROPE, _rope_bwd(g, cs, sn), 0.0) + dff_ref[...]
        dz_ref[:, Q_LORA + KV_LORA:] = dmisc.astype(MXU)

    return pl.pallas_call(
        kern, name="kb_prep", grid=(steps,),
        out_shape=[_sds((s_len, TAIL_W), MXU), _sds((1, Q_LORA)), _sds((1, KV_LORA)),
                   _sds((Q_LORA, qw), dw_dtype), _sds((KV_LORA, 2 * GROUP_W), dw_dtype)],
        in_specs=[_rows(tm, GROUP_W), _rows(tm, GROUP_W), _rows(tm, GROUP_W), _rows(tm, GROUP_W), _rows(tm, LANES),
                  _rows(tm, LANES), _rows(tm, TAIL_W), _rows(tm, Q_LORA), _rows(tm, KV_LORA),
                  _rows(tm, LANES), _rows(tm, LANES), _full((1, Q_LORA)), _full((1, KV_LORA)),
                  _full((Q_LORA, qw)), _full((KV_LORA, 2 * GROUP_W))],
        out_specs=[_rows(tm, TAIL_W), _full((1, Q_LORA)), _full((1, KV_LORA)), _full((Q_LORA, qw)),
                   _full((KV_LORA, 2 * GROUP_W))],
        scratch_shapes=[pltpu.VMEM((tm, qw), MXU), pltpu.VMEM((Q_LORA, qw), F32),
                        pltpu.VMEM((KV_LORA, 2 * GROUP_W), F32)],
        compiler_params=_params("arbitrary"),
    )(dqn, dqr, dkn, dv, dkr, dff, tail, qn, kvn, cos, sin, gq, gkv, wuq_t, wukv_t)


def _kb_in(dz_pieces, w, h, x, g, mod3, dxo, dw_dtype, tm=256):
    s_len, d = x.shape
    widths = [p.shape[1] for p in dz_pieces]
    n_p = len(widths)
    steps = s_len // tm

    def kern(*refs):
        dz_refs = refs[:n_p]
        w_ref, h_ref, x_ref, g_ref, mod_ref, dxo_ref, dx_ref, acc_ref, dw_ref, dw_acc = refs[n_p:]
        i = pl.program_id(0)

        @pl.when(i == 0)
        def _():
            acc_ref[...] = jnp.zeros_like(acc_ref)
            dw_acc[...] = jnp.zeros_like(dw_acc)

        dh = jnp.zeros((tm, d), F32)
        ht = h_ref[...]
        lo = 0
        for p_ref, wd in zip(dz_refs, widths):
            dh = dh + lax.dot_general(p_ref[...], w_ref[:, lo:lo + wd], _NT, preferred_element_type=F32)
            dw_acc[:, lo:lo + wd] += lax.dot_general(ht, p_ref[...], _TN, preferred_element_type=F32)
            lo += wd

        @pl.when(i == steps - 1)
        def _():
            dw_ref[...] = dw_acc[...].astype(dw_ref.dtype)

        xv = x_ref[...]
        r = lax.rsqrt(jnp.mean(xv * xv, axis=-1, keepdims=True) + EPS)
        xh = xv * r
        xn = xh * g_ref[...]
        dxn = dh * (1.0 + mod_ref[1:2, :])
        acc_ref[0:1, :] += jnp.sum(dh, axis=0, keepdims=True)
        acc_ref[1:2, :] += jnp.sum(dh * xn, axis=0, keepdims=True)
        acc_ref[2:3, :] += jnp.sum(dxn * xh, axis=0, keepdims=True)
        dxh = dxn * g_ref[...]
        dx_ref[...] = dxo_ref[...] + r * (dxh - xh * jnp.mean(dxh * xh, axis=-1, keepdims=True))

    return pl.pallas_call(
        kern, name="kb_in", grid=(steps,),
        out_shape=[_sds((s_len, d)), _sds((3, d)), _sds((d, Z_W), dw_dtype)],
        in_specs=[_rows(tm, wd) for wd in widths] + [_full((d, Z_W)), _rows(tm, d), _rows(tm, d), _full((1, d)),
                                                     _full((3, d)), _rows(tm, d)],
        out_specs=[_rows(tm, d), _full((3, d)), _full((d, Z_W))],
        scratch_shapes=[pltpu.VMEM((d, Z_W), F32)],
        compiler_params=_params("arbitrary"),
    )(*dz_pieces, w, h, x, g, mod3, dxo)


def _adamw(slabs, w, m, v, name):
    n_l = len(slabs)
    n, r, c = slabs[0].shape
    tm = r
    for cand in (256, 128, 64, 32, 16, 8):
        if r % cand == 0:
            tm = cand
            break
    steps = r // tm

    def kern(*refs):
        g_refs = refs[:n_l]
        w_ref, m_ref, v_ref, go_ref, d_ref, mo_ref, vo_ref, g_scr = refs[n_l:]
        for ll in range(n_l):
            @pl.when(pl.program_id(0) == ll)
            def _(g_ref=g_refs[ll]):
                g = g_ref[0].astype(F32)
                for s in range(1, n):
                    g = g + g_ref[s].astype(F32)
                g_scr[...] = g

        g = g_scr[...]
        m_new = ADAM_B1 * m_ref[...] + (1.0 - ADAM_B1) * g
        v_new = ADAM_B2 * v_ref[...] + (1.0 - ADAM_B2) * (g * g)
        m_hat = m_new / (1.0 - ADAM_B1 ** ADAM_STEP)
        v_hat = v_new / (1.0 - ADAM_B2 ** ADAM_STEP)
        go_ref[...] = g
        mo_ref[...] = m_new
        vo_ref[...] = v_new
        d_ref[...] = -ADAM_LR * (m_hat / (jnp.sqrt(v_hat) + ADAM_EPS) + ADAM_WD * w_ref[...])

    row = pl.BlockSpec((tm, c), lambda l, i: (l * steps + i, 0))

    def slab_spec(ll):
        return pl.BlockSpec((n, tm, c), lambda l, i: (0, jnp.where(l == ll, i, 0), 0))

    return pl.pallas_call(
        kern, name=name, grid=(n_l, steps), out_shape=[_sds((n_l * r, c))] * 4,
        in_specs=[slab_spec(ll) for ll in range(n_l)] + [row, row, row],
        out_specs=[row] * 4,
        scratch_shapes=[pltpu.VMEM((tm, c), F32)],
        compiler_params=_params("arbitrary", "arbitrary"),
    )(*slabs, w, m, v)


def _perm_w_in(w):
    pad = jnp.zeros(w.shape[:-1] + (Z_W - Z_MISC - ROPE - HEADS,), w.dtype)
    return jnp.concatenate([w[..., 0:1536], w[..., 1544:2056], w[..., 2472:2984], w[..., 2056:2312],
                            w[..., 2312:2440], w[..., 2440:2472], w[..., 1536:1544], pad], axis=-1)


def _unperm_w_in(g):
    ff0 = Z_MISC + MISC_FF
    return jnp.concatenate([g[..., 0:1536], g[..., ff0:ff0 + HEADS], g[..., Z_FG:Z_FG + GROUP_W],
                            g[..., Z_QL:Z_QL + Q_LORA], g[..., Z_KV:Z_KV + KV_LORA],
                            g[..., Z_MISC:Z_MISC + ROPE], g[..., Z_MG:Z_MG + GROUP_W]], axis=-1)


def _perm_w_uq(w):
    lead = w.shape[:-1]
    wh = w.reshape(lead + (PAIRS, 2, NOPE + ROPE))
    zero = jnp.zeros(lead + (PAIRS, HEAD_DIM - ROPE), w.dtype)
    rope = jnp.concatenate([wh[..., 1, NOPE:], zero, wh[..., 0, NOPE:], zero], axis=-1)
    return jnp.concatenate([wh[..., :NOPE].reshape(lead + (GROUP_W,)), rope.reshape(lead + (GROUP_W,))], axis=-1)


def _unperm_w_uq(g):
    lead = g.shape[:-1]
    nope = g[..., :GROUP_W].reshape(lead + (PAIRS, 2, NOPE))
    rp = g[..., GROUP_W:].reshape(lead + (PAIRS, 2, HEAD_DIM))[..., :ROPE]
    return jnp.concatenate([nope, rp[..., ::-1, :]], axis=-1).reshape(lead + (HEADS * (NOPE + ROPE),))


def _perm_w_ukv(w):
    lead = w.shape[:-1]
    wh = w.reshape(lead + (HEADS, 2 * HEAD_DIM))
    return jnp.concatenate([wh[..., :NOPE].reshape(lead + (GROUP_W,)),
                            wh[..., NOPE:].reshape(lead + (GROUP_W,))], axis=-1)


def _unperm_w_ukv(g):
    lead = g.shape[:-1]
    parts = [g[..., :GROUP_W].reshape(lead + (HEADS, NOPE)), g[..., GROUP_W:].reshape(lead + (HEADS, HEAD_DIM))]
    return jnp.concatenate(parts, axis=-1).reshape(lead + (2 * GROUP_W,))


def _rope_tables(positions):
    inv_freq = 1.0 / (ROPE_THETA ** (jnp.arange(0, ROPE, 2, dtype=F32) / ROPE))
    ang = positions.astype(F32)[:, None] * inv_freq
    cos, sin = jnp.cos(ang), jnp.sin(ang)
    reps = LANES // ROPE
    return jnp.tile(jnp.concatenate([cos, cos], axis=1), (1, reps)), jnp.tile(jnp.concatenate([-sin, sin], axis=1), (1, reps))


def _full_weights(g_in, g_uq, g_ukv, g_out):
    def cols(g):
        return g.transpose(1, 0, 2).reshape(g.shape[1], -1)
    return (_perm_w_in(cols(g_in)), _perm_w_uq(cols(g_uq)), _perm_w_ukv(cols(g_ukv)),
            g_out.reshape(-1, g_out.shape[2]))


def _grad_slabs(dw_in, dw_uq, dw_ukv):
    def cols(g):
        return g.reshape(g.shape[0], N_DEV, -1).transpose(1, 0, 2)
    return [cols(_unperm_w_in(dw_in)), cols(_unperm_w_uq(dw_uq)), cols(_unperm_w_ukv(dw_ukv))]


def _local_step(x, mod, positions, loss_target, norm_g, b_f, q_norm_g, kv_norm_g, final_g, weights, shards=None):
    n_l = norm_g.shape[0]
    s_len, d = x.shape
    cos, sin = _rope_tables(positions)
    qb, kb, vb = Z_FQ // LANES, Z_FK // LANES, Z_FV // LANES
    chunks = s_len // LANES
    weights = list(weights)

    def pack_rows(a_rows, delta):
        return jnp.concatenate([a_rows.T, delta[:, :HEADS], jnp.zeros((s_len, LANES - 2 * HEADS), F32)], axis=1)

    saved = []
    for l in range(n_l):
        w_in, w_uq, w_ukv, w_out = weights[l]
        mod3 = mod[l].reshape(3, d)
        h, qkv, gates, tail = _k_in(x, norm_g[l][None], mod3, w_in)
        fft = tail[:, T_MISC + MISC_FF:T_MISC + MISC_FF + HEADS].T.reshape(HEADS * chunks, LANES)
        bf = jnp.repeat(b_f[l], chunks)[:, None]
        c2 = _k_cum(fft, bf, chunks).reshape(HEADS, s_len) * LOG2E
        side = (list(shards[l + 1]), [True] * 4) if shards is not None and l + 1 < n_l else None
        ck_lanes = jnp.pad(c2.T, ((0, 0), (0, LANES - HEADS)))
        of, lse_f, *gathered = _attention_fwd(qkv, qb, qkv, kb, qkv, vb, ck_lanes, None, False,
                                              "fox_fwd_gather" if side else "fox_fwd", side)
        if side:
            weights.append(_full_weights(*gathered))
        mq, mqr, mk, mv, kr2, qn, kvn = _k_prep(tail, cos, sin, q_norm_g[l][None], kv_norm_g[l][None], w_uq, w_ukv)
        om, lse_m = _attention_fwd(mq, 0, mk, 0, mv, 0, None, (mqr, kr2), True, "mla_fwd")
        x_new, y, u = _k_out(of, om, gates, x, mod3[2:3], w_out)
        saved.append((x, gates, tail, h, qkv, fft, bf, c2, lse_f, mq, mqr, mk, mv, kr2, lse_m, of, om, qn, kvn, y, u,
                      mod3))
        x = x_new

    loss_row, dx, dfinal = _k_loss(x, final_g[None], loss_target)

    grads = {k: [] for k in ("norm_g", "mod", "w_in", "b_f", "q_norm_g", "w_uq", "kv_norm_g", "w_ukv", "w_out")}
    received, pending = {}, None
    wg_dtype = MXU if shards is not None else F32
    for l in range(n_l - 1, -1, -1):
        (x_l, gates, tail, h, qkv, fft, bf, c2, lse_f, mq, mqr, mk, mv, kr2, lse_m, of, om, qn, kvn, y, u,
         mod3) = saved[l]
        w_in, w_uq, w_ukv, w_out = weights[l]
        dof, dom, dfg, dmg, dlt_f, dlt_m, dgate, dw_out = _kb_out(dx, y, u, mod3[2:3], w_out, of, om, gates, wg_dtype)

        side = None
        if shards is not None:
            side_arrs = (pending or []) + [dw_out.reshape(N_DEV, -1, dw_out.shape[1])]
            side = (side_arrs, [False] * len(side_arrs))
        dfq, dfk, dfv, dcq, dck, *arrived = _attention_bwd(
            qkv, qb, qkv, kb, qkv, vb, dof, pack_rows(-lse_f.reshape(HEADS, s_len), dlt_f), c2[:, None, :], None,
            False, FOX_SCALE, 1.0 / LOG2E, "fox_bwd_exchange" if pending else "fox_bwd", side)
        if side:
            received[l] = [None, None, None, arrived[-1]]
            if pending:
                received[l + 1][:3] = arrived[:3]
        dcum = (dcq[:, :HEADS].T + dck.reshape(HEADS, s_len)).reshape(HEADS * chunks, LANES)
        dff_rows, dbf_rows = _k_cum_bwd(dcum, fft, bf, chunks)
        dfft = dff_rows.reshape(HEADS, s_len)
        grads["b_f"].append(jnp.sum(dbf_rows[:, 0].reshape(HEADS, chunks), axis=1))

        dmq, dkn, dmv, dqr, dkr_pairs = _attention_bwd(
            mq, 0, mk, 0, mv, 0, dom, pack_rows(-lse_m.reshape(HEADS, s_len), dlt_m), None, (mqr, kr2), True,
            MLA_SCALE, 1.0 / LOG2E, "mla_bwd")
        dkr = dkr_pairs[0] + dkr_pairs[1] + dkr_pairs[2] + dkr_pairs[3]
        dff = jnp.pad(dfft.T, ((0, 0), (MISC_FF, LANES - MISC_FF - HEADS)))
        dz_tail, dgq, dgkv, dw_uq, dw_ukv = _kb_prep(dmq, dqr, dkn, dmv, dkr, dff, tail, qn, kvn, cos, sin,
                                                     q_norm_g[l][None], kv_norm_g[l][None], w_uq, w_ukv, wg_dtype)
        grads["q_norm_g"].append(dgq[0])
        grads["kv_norm_g"].append(dgkv[0])
        dz = [dfq, dfk, dfv, dfg, dmg, dz_tail]
        dx, acc3, dw_in = _kb_in(dz, w_in, h, x_l, norm_g[l][None], mod3, dx, wg_dtype)
        grads["norm_g"].append(acc3[2])
        grads["mod"].append(jnp.concatenate([acc3[0], acc3[1], dgate[0]]))
        if shards is not None:
            pending = _grad_slabs(dw_in, dw_uq, dw_ukv)
        else:
            for name, g in (("w_in", dw_in), ("w_uq", dw_uq), ("w_ukv", dw_ukv), ("w_out", dw_out)):
                grads[name].append(g)
    grads = {k: jnp.stack(v[::-1]) for k, v in grads.items() if v}
    grads["final_g"] = dfinal[0]
    if shards is None:
        return loss_row[0, 0], dx, grads
    return loss_row[0, 0], dx, grads, received, pending


def _pack_small(parts, total):
    flat = jnp.concatenate([p.reshape(-1) for p in parts])
    return jnp.pad(flat, (0, total - flat.shape[0])).reshape(total // LANES, LANES)


def kernel(x, c, positions, norm_g, w_ada, b_ada, w_in, b_f, q_norm_g, w_uq, kv_norm_g, w_ukv, w_out, final_g, loss_target, m_norm_g, m_w_ada, m_b_ada, m_w_in, m_b_f, m_q_norm_g, m_w_uq, m_kv_norm_g, m_w_ukv, m_w_out, m_final_g, v_norm_g, v_w_ada, v_b_ada, v_w_in, v_b_f, v_q_norm_g, v_w_uq, v_kv_norm_g, v_w_ukv, v_w_out, v_final_g):
    n_l, d = norm_g.shape
    me = 4 * lax.axis_index("x") + 2 * lax.axis_index("y") + lax.axis_index("c")
    ada_c = w_ada.shape[2]

    cact = jnp.broadcast_to(jax.nn.silu(c), (N_DEV, d))
    shards = [[w[l].astype(MXU) for w in (w_in, w_uq, w_ukv, w_out)] for l in range(n_l)]
    *g_w0, g_cact = _gather_two_level(shards[0] + [cact], "gather_layer0")
    cact_all = g_cact[:, 0, :]

    b_cols = lax.dynamic_slice_in_dim(b_ada, me * ada_c, ada_c, axis=1)[:, None, :]
    modpart = _modpart(cact_all, w_ada, b_cols)
    mod_send = jnp.pad(modpart.transpose(1, 0, 2), ((0, 0), (0, 8 - n_l), (0, 0)))
    (mod_recv,) = _exchange([mod_send], [False], "scatter_mod")
    mod = mod_recv.transpose(1, 0, 2).reshape(8, N_DEV * ada_c)[:n_l]

    loss, dx, gr, received, pending = _local_step(x[0], mod, positions[0], loss_target[0], norm_g, b_f, q_norm_g,
                                                  kv_norm_g, final_g, [_full_weights(*g_w0)], shards)

    small_parts = [gr["norm_g"], gr["mod"], gr["b_f"], gr["q_norm_g"], gr["kv_norm_g"], gr["final_g"], cact[0]]
    sizes = [int(np.prod(p.shape)) for p in small_parts]
    total = -(-sum(sizes) // 1024) * 1024
    small = _pack_small(small_parts, total)
    *received[0][:3], r_small = _exchange(pending + [small], [False, False, False, True], "exchange_layer0")
    r_in, r_uq, r_ukv, r_out = ([received[l][i] for l in range(n_l)] for i in range(4))

    def upd(slabs, w, m, v, name):
        shp = w.shape
        w2, m2, v2 = (a.reshape(-1, slabs[0].shape[2]) for a in (w, m, v))
        return [o.reshape(shp) for o in _adamw(slabs, w2, m2, v2, name)]

    o_in = upd(r_in, w_in, m_w_in, v_w_in, "adamw_w_in")
    o_uq = upd(r_uq, w_uq, m_w_uq, v_w_uq, "adamw_w_uq")
    o_ukv = upd(r_ukv, w_ukv, m_w_ukv, v_w_ukv, "adamw_w_ukv")
    o_out = upd(r_out, w_out, m_w_out, v_w_out, "adamw_w_out")

    offs = np.cumsum([0] + sizes)
    flat_all = r_small.reshape(N_DEV, total)
    dmod_all = flat_all[:, offs[1]:offs[2]].reshape(N_DEV, n_l, 3 * d)
    dmod_cols = lax.dynamic_slice_in_dim(dmod_all, me * ada_c, ada_c, axis=2).transpose(1, 0, 2)
    cact_cols = flat_all[:, offs[6]:offs[7]][:, :, None]
    g_ada = _ada_grad(cact_cols, dmod_cols)
    o_ada = upd([g_ada.reshape(1, n_l * d, ada_c)], w_ada, m_w_ada, v_w_ada, "adamw_w_ada")

    zero_c = jnp.zeros((d,), F32)
    small_w = [_pack_small([norm_g, b_ada, b_f, q_norm_g, kv_norm_g, final_g, zero_c], total),
               _pack_small([m_norm_g, m_b_ada, m_b_f, m_q_norm_g, m_kv_norm_g, m_final_g, zero_c], total),
               _pack_small([v_norm_g, v_b_ada, v_b_f, v_q_norm_g, v_kv_norm_g, v_final_g, zero_c], total)]
    o_small = [o.reshape(-1) for o in _adamw([r_small], *small_w, "adamw_small")]
    shapes = [norm_g.shape, b_ada.shape, b_f.shape, q_norm_g.shape, kv_norm_g.shape, final_g.shape]

    def small_out(kind, idx):
        return o_small[kind][offs[idx]:offs[idx + 1]].reshape(shapes[idx])

    loss_all = lax.psum(loss, ("x", "y", "c"))
    outs = [loss_all, dx[None]]
    for kind in range(4):
        outs += [small_out(kind, 0), o_ada[kind], small_out(kind, 1), o_in[kind], small_out(kind, 2),
                 small_out(kind, 3), o_uq[kind], small_out(kind, 4), o_ukv[kind], o_out[kind], small_out(kind, 5)]
    return tuple(outs)
```

```python
import jax
import jax.numpy as jnp
import numpy as np
from jax import lax
from jax.experimental import pallas as pl
from jax.experimental.pallas import tpu as pltpu

F32 = jnp.float32
MXU = jnp.bfloat16

N_DEV = 8
HEADS = 8
PAIRS = HEADS // 2
HEAD_DIM = 64
NOPE = 64
ROPE = 32
HALF_ROPE = ROPE // 2
Q_LORA = 256
KV_LORA = 128
CHUNK = 64
GROUP_W = HEADS * HEAD_DIM
EPS = 1e-6
ROPE_THETA = 10000.0

Z_FQ, Z_FK, Z_FV, Z_FG, Z_MG, Z_QL, Z_KV, Z_MISC, Z_W = 0, 512, 1024, 1536, 2048, 2560, 2816, 2944, 3072
MISC_FF = ROPE
TAIL_W = Z_W - Z_QL
T_KV, T_MISC = Q_LORA, Q_LORA + KV_LORA

ADAM_LR = 0.001
ADAM_B1 = 0.9
ADAM_B2 = 0.999
ADAM_EPS = 1e-08
ADAM_WD = 0.01
ADAM_STEP = 10

VMEM_LIMIT_V7X = 56 * 1024 * 1024
LANES = 128
ATTN_TILE = 1024
V_ROWS = HEAD_DIM + 16
LOG2E = 1.4426950408889634
FOX_SCALE = HEAD_DIM ** -0.5
MLA_SCALE = (NOPE + ROPE) ** -0.5

_NT = (((1,), (1,)), ((), ()))
_TN = (((0,), (0,)), ((), ()))


def _params(*sem, side_effects=False):
    return pltpu.CompilerParams(dimension_semantics=sem, vmem_limit_bytes=VMEM_LIMIT_V7X,
                                has_side_effects=side_effects)


def _sds(shape, dtype=F32):
    return jax.ShapeDtypeStruct(shape, dtype)


def _full(shape):
    nd = len(shape)
    return pl.BlockSpec(shape, lambda *_: (0,) * nd)


def _rows(tm, width, col=0):
    return pl.BlockSpec((tm, width), lambda i: (i, col))


def _exchange(arrs, gather, name):
    n = len(arrs)

    def kern(*refs):
        copies = _exchange_copies(refs[:n], refs[n:2 * n], gather, *refs[2 * n:])
        _exchange_start(copies)
        _exchange_wait(copies)

    return pl.pallas_call(
        kern, name=name, out_shape=_exchange_out_shapes(arrs, gather),
        in_specs=[pl.BlockSpec(memory_space=pl.ANY)] * n,
        out_specs=[pl.BlockSpec(memory_space=pl.ANY)] * n,
        scratch_shapes=_exchange_sems(n),
        compiler_params=pltpu.CompilerParams(has_side_effects=True),
    )(*arrs)


def _gather_two_level(arrs, name):
    n = len(arrs)

    def kern(*refs):
        ins, outs = refs[:n], refs[n:2 * n]
        send_sems, recv_sems, loc_sems = refs[2 * n:]
        x, y, c = lax.axis_index("x"), lax.axis_index("y"), lax.axis_index("c")
        me, sibling = (x, y, c), (x, y, 1 - c)
        chips = [(1 - x, y), (x, 1 - y), (1 - x, 1 - y)]

        def slot(i, dev):
            return outs[i].at[4 * dev[0] + 2 * dev[1] + dev[2]]

        def copy(i, k, block, to, src=None):
            return pltpu.make_async_remote_copy(
                src_ref=slot(i, block) if src is None else src, dst_ref=slot(i, block), send_sem=send_sems.at[i, k],
                recv_sem=recv_sems.at[i, k], device_id=to, device_id_type=pl.DeviceIdType.MESH)

        mine = [pltpu.make_async_copy(ins[i], slot(i, me), loc_sems.at[i]) for i in range(n)]
        first = [copy(i, 0, me, sibling, src=ins[i]) for i in range(n)]
        first += [copy(i, 1 + j, me, (*chip, c), src=ins[i]) for j, chip in enumerate(chips) for i in range(n)]
        for cp in mine + first:
            cp.start()
        passed = []
        for j, chip in enumerate(chips):
            for i in range(n):
                copy(i, 1 + j, (*chip, c), me).wait_recv()
                fwd = copy(i, 4 + j, (*chip, c), sibling)
                fwd.start()
                passed.append(fwd)
        for i in range(n):
            copy(i, 0, sibling, me).wait_recv()
            for j, chip in enumerate(chips):
                copy(i, 4 + j, (*chip, 1 - c), me).wait_recv()
        for cp in first + passed:
            cp.wait_send()
        for cp in mine:
            cp.wait()

    return pl.pallas_call(
        kern, name=name, out_shape=_exchange_out_shapes(arrs, [True] * n),
        in_specs=[pl.BlockSpec(memory_space=pl.ANY)] * n,
        out_specs=[pl.BlockSpec(memory_space=pl.ANY)] * n,
        scratch_shapes=_exchange_sems(n),
        compiler_params=pltpu.CompilerParams(has_side_effects=True),
    )(*arrs)


def _exchange_out_shapes(arrs, gather):
    return [_sds((N_DEV,) + tuple(a.shape) if g else tuple(a.shape), a.dtype) for a, g in zip(arrs, gather)]


def _exchange_sems(n):
    return [pltpu.SemaphoreType.DMA((n, N_DEV)), pltpu.SemaphoreType.DMA((n, N_DEV)), pltpu.SemaphoreType.DMA((n,))]


def _exchange_copies(ins, outs, gather, send_sems, recv_sems, loc_sems, recv=True):
    n = len(ins)
    x, y, c = lax.axis_index("x"), lax.axis_index("y"), lax.axis_index("c")
    me = 4 * x + 2 * y + c

    def src(i, j):
        return ins[i] if gather[i] else ins[i].at[j]

    local = [pltpu.make_async_copy(src(i, me), outs[i].at[me], loc_sems.at[i]) for i in range(n)]
    sends, recvs = [], []
    for k in range(1, N_DEV):
        px = 1 - x if k & 4 else x
        py = 1 - y if k & 2 else y
        pc = 1 - c if k & 1 else c
        p = 4 * px + 2 * py + pc
        for i in range(n):
            sends.append(pltpu.make_async_remote_copy(
                src_ref=src(i, p), dst_ref=outs[i].at[me], send_sem=send_sems.at[i, k],
                recv_sem=recv_sems.at[i, k], device_id=(px, py, pc), device_id_type=pl.DeviceIdType.MESH))
            if recv:
                recvs.append(pltpu.make_async_remote_copy(
                    src_ref=src(i, p), dst_ref=outs[i].at[p], send_sem=send_sems.at[i, k],
                    recv_sem=recv_sems.at[i, k], device_id=(px, py, pc), device_id_type=pl.DeviceIdType.MESH))
    return local, sends, recvs


def _exchange_start(copies):
    local, sends, _ = copies
    for cp in local + sends:
        cp.start()


def _exchange_wait(copies):
    local, sends, recvs = copies
    for cp in recvs:
        cp.wait_recv()
    for cp in sends:
        cp.wait_send()
    for cp in local:
        cp.wait()


def _modpart(cact8, w_ada, b_cols):
    n_l, d, cw = w_ada.shape

    def kern(c_ref, w_ref, b_ref, o_ref):
        o_ref[0] = jnp.dot(c_ref[...].astype(MXU), w_ref[0].astype(MXU), preferred_element_type=F32) + b_ref[0]

    return pl.pallas_call(
        kern, name="modpart", grid=(n_l,), out_shape=_sds((n_l, N_DEV, cw)),
        in_specs=[_full((N_DEV, d)), pl.BlockSpec((1, d, cw), lambda l: (l, 0, 0)),
                  pl.BlockSpec((1, 1, cw), lambda l: (l, 0, 0))],
        out_specs=pl.BlockSpec((1, N_DEV, cw), lambda l: (l, 0, 0)),
        compiler_params=_params("arbitrary"),
    )(cact8, w_ada, b_cols)


def _ada_grad(cact_cols, dmod_cols):
    n_l, _, cw = dmod_cols.shape
    d = cact_cols.shape[1]

    def kern(c_ref, dm_ref, o_ref):
        acc = c_ref[0] * dm_ref[0, 0:1, :]
        for s in range(1, N_DEV):
            acc = acc + c_ref[s] * dm_ref[0, s:s + 1, :]
        o_ref[0] = acc

    return pl.pallas_call(
        kern, name="ada_grad", grid=(n_l,), out_shape=_sds((n_l, d, cw)),
        in_specs=[_full((N_DEV, d, 1)), pl.BlockSpec((1, N_DEV, cw), lambda l: (l, 0, 0))],
        out_specs=pl.BlockSpec((1, d, cw), lambda l: (l, 0, 0)),
        compiler_params=_params("arbitrary"),
    )(cact_cols, dmod_cols)


def _k_in(x, g, mod3, w, tm=256):
    s_len, d = x.shape
    qkv_w = 3 * GROUP_W

    def kern(x_ref, g_ref, mod_ref, w_ref, h_ref, qkv_ref, gates_ref, tail_ref):
        xv = x_ref[...]
        r = lax.rsqrt(jnp.mean(xv * xv, axis=-1, keepdims=True) + EPS)
        xn = xv * r * g_ref[...]
        h = (xn * (1.0 + mod_ref[1:2, :]) + mod_ref[0:1, :]).astype(MXU)
        h_ref[...] = h
        z = jnp.dot(h, w_ref[...], preferred_element_type=F32)
        qkv_ref[:, :GROUP_W] = (z[:, Z_FQ:Z_FQ + GROUP_W] * (FOX_SCALE * LOG2E)).astype(MXU)
        qkv_ref[:, GROUP_W:] = z[:, Z_FK:Z_FK + 2 * GROUP_W].astype(MXU)
        gates_ref[...] = z[:, Z_FG:Z_QL]
        tail_ref[...] = z[:, Z_QL:]

    return pl.pallas_call(
        kern, name="k_in", grid=(s_len // tm,),
        out_shape=[_sds((s_len, d), MXU), _sds((s_len, qkv_w), MXU), _sds((s_len, Z_QL - Z_FG)),
                   _sds((s_len, TAIL_W))],
        in_specs=[_rows(tm, d), _full((1, d)), _full((3, d)), _full((d, Z_W))],
        out_specs=[_rows(tm, d), _rows(tm, qkv_w), _rows(tm, Z_QL - Z_FG), _rows(tm, TAIL_W)],
        compiler_params=_params("arbitrary"),
    )(x, g, mod3, w)


def _scan_matrices(rows, chunks, reverse):
    r_i = lax.broadcasted_iota(jnp.int32, (LANES, LANES), 0)
    c_i = lax.broadcasted_iota(jnp.int32, (LANES, LANES), 1)
    a_i = lax.broadcasted_iota(jnp.int32, (rows, rows), 0)
    b_i = lax.broadcasted_iota(jnp.int32, (rows, rows), 1)
    same_head = (a_i // chunks) == (b_i // chunks)
    if reverse:
        return (r_i >= c_i).astype(F32), (same_head & (b_i > a_i)).astype(F32)
    return (r_i <= c_i).astype(F32), (same_head & (b_i < a_i)).astype(F32)


def _scan_rows(x, inner, outer):
    tot = jnp.broadcast_to(jnp.sum(x, axis=1, keepdims=True), x.shape)
    return (jnp.dot(x, inner, precision=lax.Precision.HIGHEST, preferred_element_type=F32)
            + jnp.dot(outer, tot, precision=lax.Precision.HIGHEST, preferred_element_type=F32))


def _k_cum(ff_rows, b_rows, chunks):
    rows = ff_rows.shape[0]

    def kern(ff_ref, b_ref, cum_ref):
        xc = ff_ref[...] + b_ref[...]
        lf = jnp.minimum(xc, 0.0) - jnp.log(1.0 + jnp.exp(-jnp.abs(xc)))
        cum_ref[...] = _scan_rows(lf, *_scan_matrices(rows, chunks, False))

    return pl.pallas_call(
        kern, name="k_cum", out_shape=_sds((rows, LANES)),
        in_specs=[pl.BlockSpec(memory_space=pltpu.VMEM)] * 2,
        out_specs=pl.BlockSpec(memory_space=pltpu.VMEM),
        compiler_params=_params(),
    )(ff_rows, b_rows)


def _k_cum_bwd(dc_rows, ff_rows, b_rows, chunks):
    rows = ff_rows.shape[0]

    def kern(dc_ref, ff_ref, b_ref, dff_ref, db_ref):
        dlf = _scan_rows(dc_ref[...], *_scan_matrices(rows, chunks, True))
        dff = dlf * jax.nn.sigmoid(-(ff_ref[...] + b_ref[...]))
        dff_ref[...] = dff
        db_ref[...] = jnp.broadcast_to(jnp.sum(dff, axis=1, keepdims=True), dff.shape)

    return pl.pallas_call(
        kern, name="k_cum_bwd", out_shape=[_sds((rows, LANES)), _sds((rows, LANES))],
        in_specs=[pl.BlockSpec(memory_space=pltpu.VMEM)] * 3,
        out_specs=[pl.BlockSpec(memory_space=pltpu.VMEM)] * 2,
        compiler_params=_params(),
    )(dc_rows, ff_rows, b_rows)


def _swap16(t):
    lane = lax.broadcasted_iota(jnp.int32, t.shape, 1)
    return jnp.where(lane % ROPE < HALF_ROPE, pltpu.roll(t, LANES - HALF_ROPE, 1), pltpu.roll(t, HALF_ROPE, 1))


def _rope(t, cos, sin):
    return t * cos + _swap16(t) * sin


def _rope_bwd(dt, cos, sin):
    return dt * cos - _swap16(dt) * sin


def _k_prep(tail, cos, sin, gq, gkv, wuq, wukv, tm=512):
    s_len = tail.shape[0]
    qc = MLA_SCALE * LOG2E

    def kern(tail_ref, cos_ref, sin_ref, gq_ref, gkv_ref, wuq_ref, wukv_ref,
             qn_out, qr_out, kn_out, v_out, kr_out, qn_ref, kvn_ref):
        cs, sn = cos_ref[...], sin_ref[...]
        ql = tail_ref[:, :T_KV]
        rq = lax.rsqrt(jnp.mean(ql * ql, axis=-1, keepdims=True) + EPS)
        qn = (ql * rq * gq_ref[...]).astype(MXU)
        qn_ref[...] = qn
        q = jnp.dot(qn, wuq_ref[...], preferred_element_type=F32)
        qn_out[...] = (q[:, :GROUP_W] * qc).astype(MXU)
        for blk in range(PAIRS):
            lo = GROUP_W + blk * LANES
            qr_out[:, blk * LANES:(blk + 1) * LANES] = (_rope(q[:, lo:lo + LANES], cs, sn) * qc).astype(MXU)
        kvl = tail_ref[:, T_KV:T_MISC]
        rk = lax.rsqrt(jnp.mean(kvl * kvl, axis=-1, keepdims=True) + EPS)
        kvn = (kvl * rk * gkv_ref[...]).astype(MXU)
        kvn_ref[...] = kvn
        kv = jnp.dot(kvn, wukv_ref[...], preferred_element_type=F32)
        kn_out[...] = kv[:, :GROUP_W].astype(MXU)
        v_out[...] = kv[:, GROUP_W:].astype(MXU)
        misc = tail_ref[:, T_MISC:]
        lane = lax.broadcasted_iota(jnp.int32, misc.shape, 1)
        kr = jnp.where(lane < ROPE, _rope(misc, cs, sn), 0.0)
        kr_out[...] = (kr + pltpu.roll(kr, HEAD_DIM, 1)).astype(MXU)

    return pl.pallas_call(
        kern, name="k_prep", grid=(s_len // tm,),
        out_shape=[_sds((s_len, GROUP_W), MXU), _sds((s_len, GROUP_W), MXU), _sds((s_len, GROUP_W), MXU),
                   _sds((s_len, GROUP_W), MXU), _sds((s_len, LANES), MXU), _sds((s_len, Q_LORA), MXU),
                   _sds((s_len, KV_LORA), MXU)],
        in_specs=[_rows(tm, TAIL_W), _rows(tm, LANES), _rows(tm, LANES),
                  _full((1, Q_LORA)), _full((1, KV_LORA)), _full((Q_LORA, 2 * GROUP_W)),
                  _full((KV_LORA, 2 * GROUP_W))],
        out_specs=[_rows(tm, GROUP_W), _rows(tm, GROUP_W), _rows(tm, GROUP_W), _rows(tm, GROUP_W), _rows(tm, LANES),
                   _rows(tm, Q_LORA), _rows(tm, KV_LORA)],
        compiler_params=_params("arbitrary"),
    )(tail, cos, sin, gq, gkv, wuq, wukv)


def _block_mask(kn, qn, q_off, chunk_mask, transposed):
    shape = (kn, qn) if transposed else (qn, kn)
    row = lax.broadcasted_iota(jnp.int32, shape, 0)
    col = lax.broadcasted_iota(jnp.int32, shape, 1)
    qi, ki = (col + q_off, row) if transposed else (row + q_off, col)
    if chunk_mask:
        return (ki // CHUNK) <= (qi // CHUNK)
    return ki <= qi


def _head_operand(x, hh, other=None):
    lane = lax.broadcasted_iota(jnp.int32, x.shape, 1)
    own = (lane >= hh * HEAD_DIM) & (lane < (hh + 1) * HEAD_DIM)
    return jnp.where(own, x, jnp.zeros_like(x) if other is None else other)


def _attention_fwd(q, q_blk, k, k_blk, v, v_blk, bias, rope, chunk_mask, name, side=None):
    s_len = q.shape[0]
    t = min(ATTN_TILE, s_len // 2)
    nq = s_len // t
    n_side = len(side[0]) if side else 0

    def kern(*refs):
        q_ref, k_ref, v_ref = refs[:3]
        pos = 3
        if bias is not None:
            ck_ref = refs[pos]
            pos += 1
        if rope is not None:
            qr_ref, kr_ref = refs[pos:pos + 2]
            pos += 2
        side_in = refs[pos:pos + n_side]
        pos += n_side
        o_ref, lse_ref = refs[pos:pos + 2]
        side_out = refs[pos + 2:pos + 2 + n_side]
        vt_scr, m_scr, acc_scr, ck_scr = refs[pos + 2 + n_side:pos + 6 + n_side]
        sems = refs[pos + 6 + n_side:]
        pj = pl.program_id(0)
        if n_side:
            @pl.when(pj == 0)
            def _():
                _exchange_start(_exchange_copies(side_in, side_out, side[1], *sems, recv=False))
        vt_scr[:, HEAD_DIM:, :] = jnp.ones((2, V_ROWS - HEAD_DIM, s_len), vt_scr.dtype)
        for i in range(nq):
            vtt = v_ref[i * t:(i + 1) * t, :].T
            for hh in range(2):
                vt_scr[hh, :HEAD_DIM, i * t:(i + 1) * t] = vtt[hh * HEAD_DIM:(hh + 1) * HEAD_DIM, :]
                if bias is not None:
                    ckt = ck_ref[i * t:(i + 1) * t, :]
                    lane = lax.broadcasted_iota(jnp.int32, ckt.shape, 1)
                    ck_scr[hh, i * t:(i + 1) * t, :] = jnp.sum(jnp.where(lane == 2 * pj + hh, ckt, 0.0), axis=1,
                                                               keepdims=True)

        def qbody(qi, _):
            qs = pl.multiple_of(qi * t, t)
            qt = q_ref[pl.ds(qs, t), :]
            qrt = qr_ref[pl.ds(qs, t), :] if rope is not None else None
            qh = [_head_operand(qt, hh, qrt) for hh in range(2)]
            m_scr[...] = jnp.full(m_scr.shape, -jnp.inf, F32)
            acc_scr[...] = jnp.zeros(acc_scr.shape, F32)

            def block(ks, kn, q0, qn, masked):
                kt = k_ref[pl.ds(ks, kn), :]
                kh = [_head_operand(kt, hh, kr_ref[pl.ds(ks, kn), :]) for hh in range(2)] if rope is not None else [kt, kt]
                qc = slice(q0, q0 + qn)
                sts = [lax.dot_general(kh[hh], qh[hh][qc], _NT, preferred_element_type=F32) for hh in range(2)]
                if bias is not None:
                    sts = [sts[hh] - ck_scr[hh, pl.ds(ks, kn), :] for hh in range(2)]
                if masked:
                    sts = [jnp.where(_block_mask(kn, qn, q0, chunk_mask, True), st, -jnp.inf) for st in sts]
                m_old = [m_scr[hh, :, qc] for hh in range(2)]
                m_new = [jnp.maximum(m_old[hh], jnp.max(sts[hh], axis=0, keepdims=True)) for hh in range(2)]
                pts = [jnp.exp2(sts[hh] - m_new[hh]).astype(MXU) for hh in range(2)]
                for hh in range(2):
                    alpha = jnp.exp2(m_old[hh] - m_new[hh])
                    acc_scr[hh, :, qc] = alpha * acc_scr[hh, :, qc] + jnp.dot(vt_scr[hh, :, pl.ds(ks, kn)], pts[hh],
                                                                            preferred_element_type=F32)
                    m_scr[hh, :, qc] = m_new[hh]

            def loop_body(ki, carry):
                block(pl.multiple_of(ki * t, t), t, 0, t, False)
                return carry

            lax.fori_loop(0, qi, loop_body, 0)
            block(qs, t, 0, t, True)
            outs = []
            for hh in range(2):
                acc = acc_scr[hh]
                l = acc[HEAD_DIM:HEAD_DIM + 1, :]
                outs.append(acc[:HEAD_DIM, :] / l)
                lse_ref[hh, :, pl.ds(qs, t)] = m_scr[hh] + jnp.log2(l)
            o_ref[pl.ds(qs, t), :] = jnp.concatenate(outs, axis=0).T
            return 0

        lax.fori_loop(0, nq, qbody, 0)
        if n_side:
            @pl.when(pj == PAIRS - 1)
            def _():
                _exchange_wait(_exchange_copies(side_in, side_out, side[1], *sems))

    def tok(blk):
        return pl.BlockSpec((s_len, LANES), lambda j: (0, blk + j))

    rowb = pl.BlockSpec((2, 1, s_len), lambda j: (j, 0, 0))
    hbm = pl.BlockSpec(memory_space=pl.ANY)
    ins = [q, k, v]
    in_specs = [tok(q_blk), tok(k_blk), tok(v_blk)]
    if bias is not None:
        ins.append(bias)
        in_specs.append(_full((s_len, LANES)))
    if rope is not None:
        ins += list(rope)
        in_specs += [tok(0), _full((s_len, LANES))]
    out_shape = [_sds((s_len, PAIRS * LANES)), _sds((HEADS, 1, s_len))]
    scratch = [pltpu.VMEM((2, V_ROWS, s_len), v.dtype), pltpu.VMEM((2, 1, t), F32), pltpu.VMEM((2, V_ROWS, t), F32),
               pltpu.VMEM((2, s_len if bias is not None else 8, 1), F32)]
    if n_side:
        ins += list(side[0])
        out_shape += _exchange_out_shapes(*side)
        scratch += _exchange_sems(n_side)
    return pl.pallas_call(
        kern, name=name, grid=(PAIRS,), out_shape=out_shape,
        in_specs=in_specs + [hbm] * n_side, out_specs=[tok(0), rowb] + [hbm] * n_side,
        scratch_shapes=scratch,
        compiler_params=_params("arbitrary", side_effects=bool(n_side)),
    )(*ins)


def _attention_bwd(q, q_blk, k, k_blk, v, v_blk, do, pack, ck_row, rope, chunk_mask, q_scale, k_scale, name,
                   side=None):
    s_len = q.shape[0]
    t = min(ATTN_TILE, s_len // 2)
    nq = s_len // t
    has_bias = ck_row is not None
    nv = 2 if rope is not None else 1
    n_side = len(side[0]) if side else 0

    def kern(*refs):
        q_ref, k_ref, v_ref, do_ref, pack_ref = refs[:5]
        pos = 5
        if has_bias:
            ck_ref = refs[pos]
            pos += 1
        if rope is not None:
            qr_ref, kr_ref = refs[pos:pos + 2]
            pos += 2
        side_in = refs[pos:pos + n_side]
        pos += n_side
        dq_ref, dk_ref, dv_ref = refs[pos:pos + 3]
        pos += 3
        if has_bias:
            dcq_ref, dck_ref = refs[pos:pos + 2]
            pos += 2
        if rope is not None:
            dqr_ref, dkr_ref = refs[pos:pos + 2]
            pos += 2
        side_out = refs[pos:pos + n_side]
        pos += n_side
        qt_scr, dot_scr, dkt_scr, dvt_scr, dq_scr, dcq_scr = refs[pos:pos + 6]
        sems = refs[pos + 6:]
        pj = pl.program_id(0)
        if n_side:
            @pl.when(pj == 0)
            def _():
                _exchange_start(_exchange_copies(side_in, side_out, side[1], *sems, recv=False))

        for i in range(nq):
            sl = slice(i * t, (i + 1) * t)
            dot_scr[:, sl] = do_ref[sl, :].T
            if rope is not None:
                for hh in range(2):
                    qt_scr[hh, :, sl] = _head_operand(q_ref[sl, :], hh, qr_ref[sl, :]).T
            else:
                qt_scr[0, :, sl] = q_ref[sl, :].T
        dkt_scr[...] = jnp.zeros(dkt_scr.shape, F32)
        dvt_scr[...] = jnp.zeros(dvt_scr.shape, F32)
        if has_bias:
            dck_ref[...] = jnp.zeros(dck_ref.shape, F32)

            @pl.when(pj == 0)
            def _():
                dcq_ref[...] = jnp.zeros(dcq_ref.shape, F32)

        def qbody(qi, _):
            qs = pl.multiple_of(qi * t, t)
            qt = q_ref[pl.ds(qs, t), :]
            qrt = qr_ref[pl.ds(qs, t), :] if rope is not None else None
            dot = do_ref[pl.ds(qs, t), :]
            pk = pack_ref[pl.ds(qs, t), :]
            lane = lax.broadcasted_iota(jnp.int32, pk.shape, 1)
            qh = [_head_operand(qt, hh, qrt) for hh in range(2)]
            doh = [_head_operand(dot, hh) for hh in range(2)]
            a_col = [jnp.sum(jnp.where(lane == 2 * pj + hh, pk, 0.0), axis=1, keepdims=True) for hh in range(2)]
            d_col = [jnp.sum(jnp.where(lane == HEADS + 2 * pj + hh, pk, 0.0), axis=1, keepdims=True)
                     for hh in range(2)]
            dq_scr[...] = jnp.zeros(dq_scr.shape, F32)
            if has_bias:
                dcq_scr[...] = jnp.zeros(dcq_scr.shape, F32)

            def block(ks, kn, q0, qn, masked):
                kt = k_ref[pl.ds(ks, kn), :]
                krt = kr_ref[pl.ds(ks, kn), :] if rope is not None else None
                kh = [_head_operand(kt, hh, krt) for hh in range(2)]
                vt = v_ref[pl.ds(ks, kn), :]
                qr_ = slice(q0, q0 + qn)
                qcols = pl.ds(pl.multiple_of(qs + q0, t // 2), qn)
                ss = [lax.dot_general(qh[hh][qr_], kh[hh] if rope is not None else kt, _NT,
                                      preferred_element_type=F32) + a_col[hh][qr_] for hh in range(2)]
                if has_bias:
                    ss = [ss[hh] - ck_ref[hh, :, pl.ds(ks, kn)] for hh in range(2)]
                dpds = [lax.dot_general(doh[hh][qr_], vt, _NT, preferred_element_type=F32) for hh in range(2)]
                ps = [jnp.exp2(s) for s in ss]
                if masked:
                    ps = [jnp.where(_block_mask(kn, qn, q0, chunk_mask, False), p, 0.0) for p in ps]
                dss = [ps[hh] * (dpds[hh] - d_col[hh][qr_]) for hh in range(2)]
                for hh in range(2):
                    rows = slice(hh * HEAD_DIM, (hh + 1) * HEAD_DIM)
                    dsb = dss[hh].astype(MXU)
                    dvt_scr[rows, pl.ds(ks, kn)] += jnp.dot(dot_scr[rows, qcols], ps[hh].astype(MXU),
                                                            preferred_element_type=F32)
                    if rope is not None:
                        dkt_scr[hh, :, pl.ds(ks, kn)] += jnp.dot(qt_scr[hh, :, qcols], dsb,
                                                                 preferred_element_type=F32)
                    else:
                        dkt_scr[0, rows, pl.ds(ks, kn)] += jnp.dot(qt_scr[0, rows, qcols], dsb,
                                                                   preferred_element_type=F32)
                    dq_scr[hh if rope is not None else 0, qr_, :] += jnp.dot(dsb, kh[hh], preferred_element_type=F32)
                    if has_bias:
                        dcq_scr[hh, qr_, :] += jnp.sum(dss[hh], axis=1, keepdims=True)
                        dck_ref[hh, :, pl.ds(ks, kn)] += -jnp.sum(dss[hh], axis=0, keepdims=True)

            def loop_body(ki, carry):
                block(pl.multiple_of(ki * t, t), t, 0, t, False)
                return carry

            lax.fori_loop(0, qi, loop_body, 0)
            block(qs, t // 2, 0, t // 2, True)
            block(qs, t, t // 2, t // 2, True)
            if rope is not None:
                first = lane < HEAD_DIM
                dq_ref[pl.ds(qs, t), :] = (jnp.where(first, dq_scr[0], dq_scr[1]) * q_scale).astype(dq_ref.dtype)
                dqr_ref[pl.ds(qs, t), :] = jnp.where(first, dq_scr[1], dq_scr[0]) * q_scale
            else:
                dq_ref[pl.ds(qs, t), :] = (dq_scr[0] * q_scale).astype(dq_ref.dtype)
            if has_bias:
                old = dcq_ref[pl.ds(qs, t), :]
                dcq_ref[pl.ds(qs, t), :] = jnp.where(lane == 2 * pj, dcq_scr[0],
                                                     jnp.where(lane == 2 * pj + 1, dcq_scr[1], old))
            return 0

        lax.fori_loop(0, nq, qbody, 0)
        for i in range(nq):
            sl = slice(i * t, (i + 1) * t)
            dv_ref[sl, :] = dvt_scr[:, sl].T.astype(dv_ref.dtype)
            if rope is not None:
                d0, d1 = dkt_scr[0, :, sl], dkt_scr[1, :, sl]
                first = lax.broadcasted_iota(jnp.int32, d0.shape, 0) < HEAD_DIM
                dk_ref[sl, :] = (jnp.where(first, d0, d1).T * k_scale).astype(dk_ref.dtype)
                dkr_ref[0, sl, :] = jnp.where(first, d1, d0).T * k_scale
            else:
                dk_ref[sl, :] = (dkt_scr[0, :, sl].T * k_scale).astype(dk_ref.dtype)
        if n_side:
            @pl.when(pj == PAIRS - 1)
            def _():
                _exchange_wait(_exchange_copies(side_in, side_out, side[1], *sems))

    def tok(blk):
        return pl.BlockSpec((s_len, LANES), lambda j: (0, blk + j))

    shared = _full((s_len, LANES))
    rowb = pl.BlockSpec((2, 1, s_len), lambda j: (j, 0, 0))
    slab = pl.BlockSpec((1, s_len, LANES), lambda j: (j, 0, 0))
    hbm = pl.BlockSpec(memory_space=pl.ANY)
    ins = [q, k, v, do, pack]
    in_specs = [tok(q_blk), tok(k_blk), tok(v_blk), tok(0), shared]
    out_shape = [_sds((s_len, PAIRS * LANES), MXU)] * 3
    out_specs = [tok(0)] * 3
    if has_bias:
        ins.append(ck_row)
        in_specs.append(rowb)
        out_shape += [_sds((s_len, LANES)), _sds((HEADS, 1, s_len))]
        out_specs += [shared, rowb]
    if rope is not None:
        ins += list(rope)
        in_specs += [tok(0), shared]
        out_shape += [_sds((s_len, PAIRS * LANES)), _sds((PAIRS, s_len, LANES))]
        out_specs += [tok(0), slab]
    scratch = [pltpu.VMEM((nv, LANES, s_len), q.dtype), pltpu.VMEM((LANES, s_len), do.dtype),
               pltpu.VMEM((nv, LANES, s_len), F32), pltpu.VMEM((LANES, s_len), F32),
               pltpu.VMEM((nv, t, LANES), F32), pltpu.VMEM((2, t, 1), F32)]
    if n_side:
        ins += list(side[0])
        out_shape += _exchange_out_shapes(*side)
        scratch += _exchange_sems(n_side)
    return pl.pallas_call(
        kern, name=name, grid=(PAIRS,), out_shape=out_shape,
        in_specs=in_specs + [hbm] * n_side, out_specs=out_specs + [hbm] * n_side, scratch_shapes=scratch,
        compiler_params=_params("arbitrary", side_effects=bool(n_side)),
    )(*ins)


def _silu(a):
    return a * jax.nn.sigmoid(a)


def _k_out(of, om, gates, x, gate, wout, tm=256):
    s_len, d = x.shape

    def kern(of_ref, om_ref, gates_ref, x_ref, gate_ref, w_ref, xo_ref, y_ref, u_ref):
        u_ref[:, :GROUP_W] = (of_ref[...] * _silu(gates_ref[:, :GROUP_W])).astype(MXU)
        u_ref[:, GROUP_W:] = (om_ref[...] * _silu(gates_ref[:, GROUP_W:])).astype(MXU)
        y = jnp.dot(u_ref[...], w_ref[...], preferred_element_type=F32)
        y_ref[...] = y
        xo_ref[...] = x_ref[...] + gate_ref[...] * y

    return pl.pallas_call(
        kern, name="k_out", grid=(s_len // tm,),
        out_shape=[_sds((s_len, d)), _sds((s_len, d)), _sds((s_len, 2 * GROUP_W), MXU)],
        in_specs=[_rows(tm, GROUP_W), _rows(tm, GROUP_W), _rows(tm, 2 * GROUP_W), _rows(tm, d), _full((1, d)),
                  _full((2 * GROUP_W, d))],
        out_specs=[_rows(tm, d), _rows(tm, d), _rows(tm, 2 * GROUP_W)],
        compiler_params=_params("arbitrary"),
    )(of, om, gates, x, gate, wout)


def _k_loss(x, gf, tgt, tm=256):
    s_len, d = x.shape

    def kern(x_ref, g_ref, t_ref, loss_ref, dx_ref, dg_ref):
        i = pl.program_id(0)
        xv = x_ref[...]
        r = lax.rsqrt(jnp.mean(xv * xv, axis=-1, keepdims=True) + EPS)
        xh = xv * r
        diff = xh * g_ref[...] - t_ref[...]
        part = 0.5 * jnp.sum(jnp.mean(diff * diff, axis=-1, keepdims=True))
        dout = diff * (1.0 / d)
        dxh = dout * g_ref[...]
        dx_ref[...] = r * (dxh - xh * jnp.mean(dxh * xh, axis=-1, keepdims=True))

        @pl.when(i == 0)
        def _():
            loss_ref[...] = jnp.zeros_like(loss_ref)
            dg_ref[...] = jnp.zeros_like(dg_ref)

        loss_ref[...] += jnp.full(loss_ref.shape, part, F32)
        dg_ref[...] += jnp.sum(dout * xh, axis=0, keepdims=True)

    return pl.pallas_call(
        kern, name="k_loss", grid=(s_len // tm,),
        out_shape=[_sds((1, LANES)), _sds((s_len, d)), _sds((1, d))],
        in_specs=[_rows(tm, d), _full((1, d)), _rows(tm, d)],
        out_specs=[_full((1, LANES)), _rows(tm, d), _full((1, d))],
        compiler_params=_params("arbitrary"),
    )(x, gf, tgt)


def _kb_out(dxo, y, u, gate, wout, of, om, gates, dw_dtype, tm=512):
    s_len, d = dxo.shape
    steps = s_len // tm

    def kern(dxo_ref, y_ref, u_ref, gate_ref, wt_ref, of_ref, om_ref, gates_ref,
             dof_ref, dom_ref, dfg_ref, dmg_ref, dlf_ref, dlm_ref, dgate_ref, dw_ref, dw_acc):
        i = pl.program_id(0)
        dxv = dxo_ref[...]

        @pl.when(i == 0)
        def _():
            dgate_ref[...] = jnp.zeros_like(dgate_ref)
            dw_acc[...] = jnp.zeros_like(dw_acc)

        dgate_ref[...] += jnp.sum(dxv * y_ref[...], axis=0, keepdims=True)
        dyb = (dxv * gate_ref[...]).astype(MXU)
        dw_acc[...] += lax.dot_general(u_ref[...], dyb, _TN, preferred_element_type=F32)

        @pl.when(i == steps - 1)
        def _():
            dw_ref[...] = dw_acc[...].astype(dw_ref.dtype)

        du = lax.dot_general(dyb, wt_ref[...], _NT, preferred_element_type=F32)
        head_of = (lax.broadcasted_iota(jnp.int32, (GROUP_W, LANES), 0) // HEAD_DIM
                   == lax.broadcasted_iota(jnp.int32, (GROUP_W, LANES), 1)).astype(F32)
        for du_g, o_ref, a, do_ref, dg_ref, dl_ref in (
                (du[:, :GROUP_W], of_ref, gates_ref[:, :GROUP_W], dof_ref, dfg_ref, dlf_ref),
                (du[:, GROUP_W:], om_ref, gates_ref[:, GROUP_W:], dom_ref, dmg_ref, dlm_ref)):
            sg = jax.nn.sigmoid(a)
            ov = o_ref[...]
            dov = du_g * (a * sg)
            do_ref[...] = dov.astype(MXU)
            dg_ref[...] = (du_g * ov * (sg * (1.0 + a * (1.0 - sg)))).astype(MXU)
            dl_ref[...] = jnp.dot(dov * ov, head_of, precision=lax.Precision.HIGH, preferred_element_type=F32)

    return pl.pallas_call(
        kern, name="kb_out", grid=(steps,),
        out_shape=[_sds((s_len, GROUP_W), MXU), _sds((s_len, GROUP_W), MXU),
                   _sds((s_len, GROUP_W), MXU), _sds((s_len, GROUP_W), MXU), _sds((s_len, LANES)),
                   _sds((s_len, LANES)), _sds((1, d)), _sds((2 * GROUP_W, d), dw_dtype)],
        in_specs=[_rows(tm, d), _rows(tm, d), _rows(tm, 2 * GROUP_W), _full((1, d)), _full((2 * GROUP_W, d)),
                  _rows(tm, GROUP_W), _rows(tm, GROUP_W), _rows(tm, 2 * GROUP_W)],
        out_specs=[_rows(tm, GROUP_W), _rows(tm, GROUP_W), _rows(tm, GROUP_W),
                   _rows(tm, GROUP_W), _rows(tm, LANES), _rows(tm, LANES), _full((1, d)), _full((2 * GROUP_W, d))],
        scratch_shapes=[pltpu.VMEM((2 * GROUP_W, d), F32)],
        compiler_params=_params("arbitrary"),
    )(dxo, y, u, gate, wout, of, om, gates)


def _kb_prep(dqn, dqr, dkn, dv, dkr, dff, tail, qn, kvn, cos, sin, gq, gkv, wuq_t, wukv_t, dw_dtype, tm=512):
    s_len = tail.shape[0]
    qw = 2 * GROUP_W
    steps = s_len // tm

    def kern(dqn_ref, dqr_ref, dkn_ref, dv_ref, dkr_ref, dff_ref, tail_ref, qn_ref, kvn_ref, cos_ref, sin_ref,
             gq_ref, gkv_ref, wuqt_ref, wukvt_ref, dz_ref, dgq_ref, dgkv_ref, dwuq_ref, dwukv_ref,
             dq_ref, uq_acc, ukv_acc):
        i = pl.program_id(0)

        @pl.when(i == 0)
        def _():
            dgq_ref[...] = jnp.zeros_like(dgq_ref)
            dgkv_ref[...] = jnp.zeros_like(dgkv_ref)
            uq_acc[...] = jnp.zeros_like(uq_acc)
            ukv_acc[...] = jnp.zeros_like(ukv_acc)

        cs, sn = cos_ref[...], sin_ref[...]
        dq_ref[:, :GROUP_W] = dqn_ref[...]
        for blk in range(PAIRS):
            sl = slice(blk * LANES, (blk + 1) * LANES)
            dq_ref[:, GROUP_W + blk * LANES:GROUP_W + (blk + 1) * LANES] = _rope_bwd(dqr_ref[:, sl], cs, sn).astype(MXU)
        dqn = lax.dot_general(dq_ref[...], wuqt_ref[...], _NT, preferred_element_type=F32)
        uq_acc[...] += lax.dot_general(qn_ref[...], dq_ref[...], _TN, preferred_element_type=F32)
        ukv_acc[:, :GROUP_W] += lax.dot_general(kvn_ref[...], dkn_ref[...], _TN, preferred_element_type=F32)
        ukv_acc[:, GROUP_W:] += lax.dot_general(kvn_ref[...], dv_ref[...], _TN, preferred_element_type=F32)

        @pl.when(i == steps - 1)
        def _():
            dwuq_ref[...] = uq_acc[...].astype(dwuq_ref.dtype)
            dwukv_ref[...] = ukv_acc[...].astype(dwukv_ref.dtype)

        ql = tail_ref[:, :T_KV]
        rq = lax.rsqrt(jnp.mean(ql * ql, axis=-1, keepdims=True) + EPS)
        qh = ql * rq
        dgq_ref[...] += jnp.sum(dqn * qh, axis=0, keepdims=True)
        dqh = dqn * gq_ref[...]
        dz_ref[:, :Q_LORA] = (rq * (dqh - qh * jnp.mean(dqh * qh, axis=-1, keepdims=True))).astype(MXU)

        dkvn = (lax.dot_general(dkn_ref[...], wukvt_ref[:, :GROUP_W], _NT, preferred_element_type=F32)
                + lax.dot_general(dv_ref[...], wukvt_ref[:, GROUP_W:], _NT, preferred_element_type=F32))
        kvl = tail_ref[:, T_KV:T_MISC]
        rk = lax.rsqrt(jnp.mean(kvl * kvl, axis=-1, keepdims=True) + EPS)
        kh = kvl * rk
        dgkv_ref[...] += jnp.sum(dkvn * kh, axis=0, keepdims=True)
        dkh = dkvn * gkv_ref[...]
        dz_ref[:, Q_LORA:Q_LORA + KV_LORA] = (
            rk * (dkh - kh * jnp.mean(dkh * kh, axis=-1, keepdims=True))).astype(MXU)

        g = dkr_ref[...] + pltpu.roll(dkr_ref[...], HEAD_DIM, 1)
        lane = lax.broadcasted_iota(jnp.int32, g.shape, 1)
        dmisc = jnp.where(lane < ROPE, _rope_bwd(g, cs, sn), 0.0) + dff_ref[...]
        dz_ref[:, Q_LORA + KV_LORA:] = dmisc.astype(MXU)

    return pl.pallas_call(
        kern, name="kb_prep", grid=(steps,),
        out_shape=[_sds((s_len, TAIL_W), MXU), _sds((1, Q_LORA)), _sds((1, KV_LORA)),
                   _sds((Q_LORA, qw), dw_dtype), _sds((KV_LORA, 2 * GROUP_W), dw_dtype)],
        in_specs=[_rows(tm, GROUP_W), _rows(tm, GROUP_W), _rows(tm, GROUP_W), _rows(tm, GROUP_W), _rows(tm, LANES),
                  _rows(tm, LANES), _rows(tm, TAIL_W), _rows(tm, Q_LORA), _rows(tm, KV_LORA),
                  _rows(tm, LANES), _rows(tm, LANES), _full((1, Q_LORA)), _full((1, KV_LORA)),
                  _full((Q_LORA, qw)), _full((KV_LORA, 2 * GROUP_W))],
        out_specs=[_rows(tm, TAIL_W), _full((1, Q_LORA)), _full((1, KV_LORA)), _full((Q_LORA, qw)),
                   _full((KV_LORA, 2 * GROUP_W))],
        scratch_shapes=[pltpu.VMEM((tm, qw), MXU), pltpu.VMEM((Q_LORA, qw), F32),
                        pltpu.VMEM((KV_LORA, 2 * GROUP_W), F32)],
        compiler_params=_params("arbitrary"),
    )(dqn, dqr, dkn, dv, dkr, dff, tail, qn, kvn, cos, sin, gq, gkv, wuq_t, wukv_t)


def _kb_in(dz_pieces, w, h, x, g, mod3, dxo, dw_dtype, tm=512):
    s_len, d = x.shape
    widths = [p.shape[1] for p in dz_pieces]
    n_p = len(widths)
    steps = s_len // tm

    def kern(*refs):
        dz_refs = refs[:n_p]
        w_ref, h_ref, x_ref, g_ref, mod_ref, dxo_ref, dx_ref, acc_ref, dw_ref, dw_acc = refs[n_p:]
        i = pl.program_id(0)

        @pl.when(i == 0)
        def _():
            acc_ref[...] = jnp.zeros_like(acc_ref)
            dw_acc[...] = jnp.zeros_like(dw_acc)

        dh = jnp.zeros((tm, d), F32)
        ht = h_ref[...]
        lo = 0
        for p_ref, wd in zip(dz_refs, widths):
            dh = dh + lax.dot_general(p_ref[...], w_ref[:, lo:lo + wd], _NT, preferred_element_type=F32)
            dw_acc[:, lo:lo + wd] += lax.dot_general(ht, p_ref[...], _TN, preferred_element_type=F32)
            lo += wd

        @pl.when(i == steps - 1)
        def _():
            dw_ref[...] = dw_acc[...].astype(dw_ref.dtype)

        xv = x_ref[...]
        r = lax.rsqrt(jnp.mean(xv * xv, axis=-1, keepdims=True) + EPS)
        xh = xv * r
        xn = xh * g_ref[...]
        dxn = dh * (1.0 + mod_ref[1:2, :])
        acc_ref[0:1, :] += jnp.sum(dh, axis=0, keepdims=True)
        acc_ref[1:2, :] += jnp.sum(dh * xn, axis=0, keepdims=True)
        acc_ref[2:3, :] += jnp.sum(dxn * xh, axis=0, keepdims=True)
        dxh = dxn * g_ref[...]
        dx_ref[...] = dxo_ref[...] + r * (dxh - xh * jnp.mean(dxh * xh, axis=-1, keepdims=True))

    return pl.pallas_call(
        kern, name="kb_in", grid=(steps,),
        out_shape=[_sds((s_len, d)), _sds((3, d)), _sds((d, Z_W), dw_dtype)],
        in_specs=[_rows(tm, wd) for wd in widths] + [_full((d, Z_W)), _rows(tm, d), _rows(tm, d), _full((1, d)),
                                                     _full((3, d)), _rows(tm, d)],
        out_specs=[_rows(tm, d), _full((3, d)), _full((d, Z_W))],
        scratch_shapes=[pltpu.VMEM((d, Z_W), F32)],
        compiler_params=_params("arbitrary"),
    )(*dz_pieces, w, h, x, g, mod3, dxo)


def _adamw(slabs, w, m, v, name):
    n_l = len(slabs)
    n, r, c = slabs[0].shape
    tm = r
    for cand in (256, 128, 64, 32, 16, 8):
        if r % cand == 0:
            tm = cand
            break
    steps = r // tm

    def kern(*refs):
        g_refs = refs[:n_l]
        w_ref, m_ref, v_ref, go_ref, d_ref, mo_ref, vo_ref, g_scr = refs[n_l:]
        for ll in range(n_l):
            @pl.when(pl.program_id(0) == ll)
            def _(g_ref=g_refs[ll]):
                g = g_ref[0].astype(F32)
                for s in range(1, n):
                    g = g + g_ref[s].astype(F32)
                g_scr[...] = g

        g = g_scr[...]
        m_new = ADAM_B1 * m_ref[...] + (1.0 - ADAM_B1) * g
        v_new = ADAM_B2 * v_ref[...] + (1.0 - ADAM_B2) * (g * g)
        m_hat = m_new / (1.0 - ADAM_B1 ** ADAM_STEP)
        v_hat = v_new / (1.0 - ADAM_B2 ** ADAM_STEP)
        go_ref[...] = g
        mo_ref[...] = m_new
        vo_ref[...] = v_new
        d_ref[...] = -ADAM_LR * (m_hat / (jnp.sqrt(v_hat) + ADAM_EPS) + ADAM_WD * w_ref[...])

    row = pl.BlockSpec((tm, c), lambda l, i: (l * steps + i, 0))

    def slab_spec(ll):
        return pl.BlockSpec((n, tm, c), lambda l, i: (0, jnp.where(l == ll, i, 0), 0))

    return pl.pallas_call(
        kern, name=name, grid=(n_l, steps), out_shape=[_sds((n_l * r, c))] * 4,
        in_specs=[slab_spec(ll) for ll in range(n_l)] + [row, row, row],
        out_specs=[row] * 4,
        scratch_shapes=[pltpu.VMEM((tm, c), F32)],
        compiler_params=_params("arbitrary", "arbitrary"),
    )(*slabs, w, m, v)


def _perm_w_in(w):
    pad = jnp.zeros(w.shape[:-1] + (Z_W - Z_MISC - ROPE - HEADS,), w.dtype)
    return jnp.concatenate([w[..., 0:1536], w[..., 1544:2056], w[..., 2472:2984], w[..., 2056:2312],
                            w[..., 2312:2440], w[..., 2440:2472], w[..., 1536:1544], pad], axis=-1)


def _unperm_w_in(g):
    ff0 = Z_MISC + MISC_FF
    return jnp.concatenate([g[..., 0:1536], g[..., ff0:ff0 + HEADS], g[..., Z_FG:Z_FG + GROUP_W],
                            g[..., Z_QL:Z_QL + Q_LORA], g[..., Z_KV:Z_KV + KV_LORA],
                            g[..., Z_MISC:Z_MISC + ROPE], g[..., Z_MG:Z_MG + GROUP_W]], axis=-1)


def _perm_w_uq(w):
    lead = w.shape[:-1]
    wh = w.reshape(lead + (PAIRS, 2, NOPE + ROPE))
    zero = jnp.zeros(lead + (PAIRS, HEAD_DIM - ROPE), w.dtype)
    rope = jnp.concatenate([wh[..., 1, NOPE:], zero, wh[..., 0, NOPE:], zero], axis=-1)
    return jnp.concatenate([wh[..., :NOPE].reshape(lead + (GROUP_W,)), rope.reshape(lead + (GROUP_W,))], axis=-1)


def _unperm_w_uq(g):
    lead = g.shape[:-1]
    nope = g[..., :GROUP_W].reshape(lead + (PAIRS, 2, NOPE))
    rp = g[..., GROUP_W:].reshape(lead + (PAIRS, 2, HEAD_DIM))[..., :ROPE]
    return jnp.concatenate([nope, rp[..., ::-1, :]], axis=-1).reshape(lead + (HEADS * (NOPE + ROPE),))


def _perm_w_ukv(w):
    lead = w.shape[:-1]
    wh = w.reshape(lead + (HEADS, 2 * HEAD_DIM))
    return jnp.concatenate([wh[..., :NOPE].reshape(lead + (GROUP_W,)),
                            wh[..., NOPE:].reshape(lead + (GROUP_W,))], axis=-1)


def _unperm_w_ukv(g):
    lead = g.shape[:-1]
    parts = [g[..., :GROUP_W].reshape(lead + (HEADS, NOPE)), g[..., GROUP_W:].reshape(lead + (HEADS, HEAD_DIM))]
    return jnp.concatenate(parts, axis=-1).reshape(lead + (2 * GROUP_W,))


def _rope_tables(positions):
    inv_freq = 1.0 / (ROPE_THETA ** (jnp.arange(0, ROPE, 2, dtype=F32) / ROPE))
    ang = positions.astype(F32)[:, None] * inv_freq
    cos, sin = jnp.cos(ang), jnp.sin(ang)
    reps = LANES // ROPE
    return jnp.tile(jnp.concatenate([cos, cos], axis=1), (1, reps)), jnp.tile(jnp.concatenate([-sin, sin], axis=1), (1, reps))


def _full_weights(g_in, g_uq, g_ukv, g_out):
    def cols(g):
        return g.transpose(1, 0, 2).reshape(g.shape[1], -1)
    return (_perm_w_in(cols(g_in)), _perm_w_uq(cols(g_uq)), _perm_w_ukv(cols(g_ukv)),
            g_out.reshape(-1, g_out.shape[2]))


def _grad_slabs(dw_in, dw_uq, dw_ukv):
    def cols(g):
        return g.reshape(g.shape[0], N_DEV, -1).transpose(1, 0, 2)
    return [cols(_unperm_w_in(dw_in)), cols(_unperm_w_uq(dw_uq)), cols(_unperm_w_ukv(dw_ukv))]


def _local_step(x, mod, positions, loss_target, norm_g, b_f, q_norm_g, kv_norm_g, final_g, weights, shards=None):
    n_l = norm_g.shape[0]
    s_len, d = x.shape
    cos, sin = _rope_tables(positions)
    qb, kb, vb = Z_FQ // LANES, Z_FK // LANES, Z_FV // LANES
    chunks = s_len // LANES
    weights = list(weights)

    def pack_rows(a_rows, delta):
        return jnp.concatenate([a_rows.T, delta[:, :HEADS], jnp.zeros((s_len, LANES - 2 * HEADS), F32)], axis=1)

    saved = []
    for l in range(n_l):
        w_in, w_uq, w_ukv, w_out = weights[l]
        mod3 = mod[l].reshape(3, d)
        h, qkv, gates, tail = _k_in(x, norm_g[l][None], mod3, w_in)
        fft = tail[:, T_MISC + MISC_FF:T_MISC + MISC_FF + HEADS].T.reshape(HEADS * chunks, LANES)
        bf = jnp.repeat(b_f[l], chunks)[:, None]
        c2 = _k_cum(fft, bf, chunks).reshape(HEADS, s_len) * LOG2E
        side = (list(shards[l + 1]), [True] * 4) if shards is not None and l + 1 < n_l else None
        ck_lanes = jnp.pad(c2.T, ((0, 0), (0, LANES - HEADS)))
        of, lse_f, *gathered = _attention_fwd(qkv, qb, qkv, kb, qkv, vb, ck_lanes, None, False,
                                              "fox_fwd_gather" if side else "fox_fwd", side)
        if side:
            weights.append(_full_weights(*gathered))
        mq, mqr, mk, mv, kr2, qn, kvn = _k_prep(tail, cos, sin, q_norm_g[l][None], kv_norm_g[l][None], w_uq, w_ukv)
        om, lse_m = _attention_fwd(mq, 0, mk, 0, mv, 0, None, (mqr, kr2), True, "mla_fwd")
        x_new, y, u = _k_out(of, om, gates, x, mod3[2:3], w_out)
        saved.append((x, gates, tail, h, qkv, fft, bf, c2, lse_f, mq, mqr, mk, mv, kr2, lse_m, of, om, qn, kvn, y, u,
                      mod3))
        x = x_new

    loss_row, dx, dfinal = _k_loss(x, final_g[None], loss_target)

    grads = {k: [] for k in ("norm_g", "mod", "w_in", "b_f", "q_norm_g", "w_uq", "kv_norm_g", "w_ukv", "w_out")}
    received, pending = {}, None
    wg_dtype = MXU if shards is not None else F32
    for l in range(n_l - 1, -1, -1):
        (x_l, gates, tail, h, qkv, fft, bf, c2, lse_f, mq, mqr, mk, mv, kr2, lse_m, of, om, qn, kvn, y, u,
         mod3) = saved[l]
        w_in, w_uq, w_ukv, w_out = weights[l]
        dof, dom, dfg, dmg, dlt_f, dlt_m, dgate, dw_out = _kb_out(dx, y, u, mod3[2:3], w_out, of, om, gates, wg_dtype)

        side = None
        if shards is not None:
            side_arrs = (pending or []) + [dw_out.reshape(N_DEV, -1, dw_out.shape[1])]
            side = (side_arrs, [False] * len(side_arrs))
        dfq, dfk, dfv, dcq, dck, *arrived = _attention_bwd(
            qkv, qb, qkv, kb, qkv, vb, dof, pack_rows(-lse_f.reshape(HEADS, s_len), dlt_f), c2[:, None, :], None,
            False, FOX_SCALE, 1.0 / LOG2E, "fox_bwd_exchange" if pending else "fox_bwd", side)
        if side:
            received[l] = [None, None, None, arrived[-1]]
            if pending:
                received[l + 1][:3] = arrived[:3]
        dcum = (dcq[:, :HEADS].T + dck.reshape(HEADS, s_len)).reshape(HEADS * chunks, LANES)
        dff_rows, dbf_rows = _k_cum_bwd(dcum, fft, bf, chunks)
        dfft = dff_rows.reshape(HEADS, s_len)
        grads["b_f"].append(jnp.sum(dbf_rows[:, 0].reshape(HEADS, chunks), axis=1))

        dmq, dkn, dmv, dqr, dkr_pairs = _attention_bwd(
            mq, 0, mk, 0, mv, 0, dom, pack_rows(-lse_m.reshape(HEADS, s_len), dlt_m), None, (mqr, kr2), True,
            MLA_SCALE, 1.0 / LOG2E, "mla_bwd")
        dkr = dkr_pairs[0] + dkr_pairs[1] + dkr_pairs[2] + dkr_pairs[3]
        dff = jnp.pad(dfft.T, ((0, 0), (MISC_FF, LANES - MISC_FF - HEADS)))
        dz_tail, dgq, dgkv, dw_uq, dw_ukv = _kb_prep(dmq, dqr, dkn, dmv, dkr, dff, tail, qn, kvn, cos, sin,
                                                     q_norm_g[l][None], kv_norm_g[l][None], w_uq, w_ukv, wg_dtype)
        grads["q_norm_g"].append(dgq[0])
        grads["kv_norm_g"].append(dgkv[0])
        dz = [dfq, dfk, dfv, dfg, dmg, dz_tail]
        dx, acc3, dw_in = _kb_in(dz, w_in, h, x_l, norm_g[l][None], mod3, dx, wg_dtype)
        grads["norm_g"].append(acc3[2])
        grads["mod"].append(jnp.concatenate([acc3[0], acc3[1], dgate[0]]))
        if shards is not None:
            pending = _grad_slabs(dw_in, dw_uq, dw_ukv)
        else:
            for name, g in (("w_in", dw_in), ("w_uq", dw_uq), ("w_ukv", dw_ukv), ("w_out", dw_out)):
                grads[name].append(g)
    grads = {k: jnp.stack(v[::-1]) for k, v in grads.items() if v}
    grads["final_g"] = dfinal[0]
    if shards is None:
        return loss_row[0, 0], dx, grads
    return loss_row[0, 0], dx, grads, received, pending


def _pack_small(parts, total):
    flat = jnp.concatenate([p.reshape(-1) for p in parts])
    return jnp.pad(flat, (0, total - flat.shape[0])).reshape(total // LANES, LANES)


def kernel(x, c, positions, norm_g, w_ada, b_ada, w_in, b_f, q_norm_g, w_uq, kv_norm_g, w_ukv, w_out, final_g, loss_target, m_norm_g, m_w_ada, m_b_ada, m_w_in, m_b_f, m_q_norm_g, m_w_uq, m_kv_norm_g, m_w_ukv, m_w_out, m_final_g, v_norm_g, v_w_ada, v_b_ada, v_w_in, v_b_f, v_q_norm_g, v_w_uq, v_kv_norm_g, v_w_ukv, v_w_out, v_final_g):
    n_l, d = norm_g.shape
    me = 4 * lax.axis_index("x") + 2 * lax.axis_index("y") + lax.axis_index("c")
    ada_c = w_ada.shape[2]

    cact = jnp.broadcast_to(jax.nn.silu(c), (N_DEV, d))
    shards = [[w[l].astype(MXU) for w in (w_in, w_uq, w_ukv, w_out)] for l in range(n_l)]
    *g_w0, g_cact = _gather_two_level(shards[0] + [cact], "gather_layer0")
    cact_all = g_cact[:, 0, :]

    b_cols = lax.dynamic_slice_in_dim(b_ada, me * ada_c, ada_c, axis=1)[:, None, :]
    modpart = _modpart(cact_all, w_ada, b_cols)
    mod_send = jnp.pad(modpart.transpose(1, 0, 2), ((0, 0), (0, 8 - n_l), (0, 0)))
    (mod_recv,) = _exchange([mod_send], [False], "scatter_mod")
    mod = mod_recv.transpose(1, 0, 2).reshape(8, N_DEV * ada_c)[:n_l]

    loss, dx, gr, received, pending = _local_step(x[0], mod, positions[0], loss_target[0], norm_g, b_f, q_norm_g,
                                                  kv_norm_g, final_g, [_full_weights(*g_w0)], shards)

    small_parts = [gr["norm_g"], gr["mod"], gr["b_f"], gr["q_norm_g"], gr["kv_norm_g"], gr["final_g"], cact[0]]
    sizes = [int(np.prod(p.shape)) for p in small_parts]
    total = -(-sum(sizes) // 1024) * 1024
    small = _pack_small(small_parts, total)
    *received[0][:3], r_small = _exchange(pending + [small], [False, False, False, True], "exchange_layer0")
    r_in, r_uq, r_ukv, r_out = ([received[l][i] for l in range(n_l)] for i in range(4))

    def upd(slabs, w, m, v, name):
        shp = w.shape
        w2, m2, v2 = (a.reshape(-1, slabs[0].shape[2]) for a in (w, m, v))
        return [o.reshape(shp) for o in _adamw(slabs, w2, m2, v2, name)]

    o_in = upd(r_in, w_in, m_w_in, v_w_in, "adamw_w_in")
    o_uq = upd(r_uq, w_uq, m_w_uq, v_w_uq, "adamw_w_uq")
    o_ukv = upd(r_ukv, w_ukv, m_w_ukv, v_w_ukv, "adamw_w_ukv")
    o_out = upd(r_out, w_out, m_w_out, v_w_out, "adamw_w_out")

    offs = np.cumsum([0] + sizes)
    flat_all = r_small.reshape(N_DEV, total)
    dmod_all = flat_all[:, offs[1]:offs[2]].reshape(N_DEV, n_l, 3 * d)
    dmod_cols = lax.dynamic_slice_in_dim(dmod_all, me * ada_c, ada_c, axis=2).transpose(1, 0, 2)
    cact_cols = flat_all[:, offs[6]:offs[7]][:, :, None]
    g_ada = _ada_grad(cact_cols, dmod_cols)
    o_ada = upd([g_ada.reshape(1, n_l * d, ada_c)], w_ada, m_w_ada, v_w_ada, "adamw_w_ada")

    zero_c = jnp.zeros((d,), F32)
    small_w = [_pack_small([norm_g, b_ada, b_f, q_norm_g, kv_norm_g, final_g, zero_c], total),
               _pack_small([m_norm_g, m_b_ada, m_b_f, m_q_norm_g, m_kv_norm_g, m_final_g, zero_c], total),
               _pack_small([v_norm_g, v_b_ada, v_b_f, v_q_norm_g, v_kv_norm_g, v_final_g, zero_c], total)]
    o_small = [o.reshape(-1) for o in _adamw([r_small], *small_w, "adamw_small")]
    shapes = [norm_g.shape, b_ada.shape, b_f.shape, q_norm_g.shape, kv_norm_g.shape, final_g.shape]

    def small_out(kind, idx):
        return o_small[kind][offs[idx]:offs[idx + 1]].reshape(shapes[idx])

    loss_all = lax.psum(loss, ("x", "y", "c"))
    outs = [loss_all, dx[None]]
    for kind in range(4):
        outs += [small_out(kind, 0), o_ada[kind], small_out(kind, 1), o_in[kind], small_out(kind, 2),
                 small_out(kind, 3), o_uq[kind], small_out(kind, 4), o_ukv[kind], o_out[kind], small_out(kind, 5)]
    return tuple(outs)
```

```python
import jax
import jax.numpy as jnp
import numpy as np
from jax import lax
from jax.experimental import pallas as pl
from jax.experimental.pallas import tpu as pltpu

F32 = jnp.float32
MXU = jnp.bfloat16

N_DEV = 8
HEADS = 8
PAIRS = HEADS // 2
HEAD_DIM = 64
NOPE = 64
ROPE = 32
HALF_ROPE = ROPE // 2
Q_LORA = 256
KV_LORA = 128
CHUNK = 64
GROUP_W = HEADS * HEAD_DIM
EPS = 1e-6
ROPE_THETA = 10000.0

Z_FQ, Z_FK, Z_FV, Z_FG, Z_MG, Z_QL, Z_KV, Z_MISC, Z_W = 0, 512, 1024, 1536, 2048, 2560, 2816, 2944, 3072
MISC_FF = ROPE
TAIL_W = Z_W - Z_QL
T_KV, T_MISC = Q_LORA, Q_LORA + KV_LORA

ADAM_LR = 0.001
ADAM_B1 = 0.9
ADAM_B2 = 0.999
ADAM_EPS = 1e-08
ADAM_WD = 0.01
ADAM_STEP = 10

VMEM_LIMIT_V7X = 56 * 1024 * 1024
LANES = 128
ATTN_TILE = 1024
V_ROWS = HEAD_DIM + 16
LOG2E = 1.4426950408889634
FOX_SCALE = HEAD_DIM ** -0.5
MLA_SCALE = (NOPE + ROPE) ** -0.5

_NT = (((1,), (1,)), ((), ()))
_TN = (((0,), (0,)), ((), ()))


def _params(*sem, side_effects=False):
    return pltpu.CompilerParams(dimension_semantics=sem, vmem_limit_bytes=VMEM_LIMIT_V7X,
                                has_side_effects=side_effects)


def _sds(shape, dtype=F32):
    return jax.ShapeDtypeStruct(shape, dtype)


def _full(shape):
    nd = len(shape)
    return pl.BlockSpec(shape, lambda *_: (0,) * nd)


def _rows(tm, width, col=0):
    return pl.BlockSpec((tm, width), lambda i: (i, col))


def _exchange(arrs, gather, name):
    n = len(arrs)

    def kern(*refs):
        copies = _exchange_copies(refs[:n], refs[n:2 * n], gather, *refs[2 * n:])
        _exchange_start(copies)
        _exchange_wait(copies)

    return pl.pallas_call(
        kern, name=name, out_shape=_exchange_out_shapes(arrs, gather),
        in_specs=[pl.BlockSpec(memory_space=pl.ANY)] * n,
        out_specs=[pl.BlockSpec(memory_space=pl.ANY)] * n,
        scratch_shapes=_exchange_sems(n),
        compiler_params=pltpu.CompilerParams(has_side_effects=True),
    )(*arrs)


def _gather_two_level(arrs, name):
    n = len(arrs)

    def kern(*refs):
        ins, outs = refs[:n], refs[n:2 * n]
        send_sems, recv_sems, loc_sems = refs[2 * n:]
        x, y, c = lax.axis_index("x"), lax.axis_index("y"), lax.axis_index("c")
        me, sibling = (x, y, c), (x, y, 1 - c)
        chips = [(1 - x, y), (x, 1 - y), (1 - x, 1 - y)]

        def slot(i, dev):
            return outs[i].at[4 * dev[0] + 2 * dev[1] + dev[2]]

        def copy(i, k, block, to, src=None):
            return pltpu.make_async_remote_copy(
                src_ref=slot(i, block) if src is None else src, dst_ref=slot(i, block), send_sem=send_sems.at[i, k],
                recv_sem=recv_sems.at[i, k], device_id=to, device_id_type=pl.DeviceIdType.MESH)

        mine = [pltpu.make_async_copy(ins[i], slot(i, me), loc_sems.at[i]) for i in range(n)]
        first = [copy(i, 0, me, sibling, src=ins[i]) for i in range(n)]
        first += [copy(i, 1 + j, me, (*chip, c), src=ins[i]) for j, chip in enumerate(chips) for i in range(n)]
        for cp in mine + first:
            cp.start()
        passed = []
        for j, chip in enumerate(chips):
            for i in range(n):
                copy(i, 1 + j, (*chip, c), me).wait_recv()
                fwd = copy(i, 4 + j, (*chip, c), sibling)
                fwd.start()
                passed.append(fwd)
        for i in range(n):
            copy(i, 0, sibling, me).wait_recv()
            for j, chip in enumerate(chips):
                copy(i, 4 + j, (*chip, 1 - c), me).wait_recv()
        for cp in first + passed:
            cp.wait_send()
        for cp in mine:
            cp.wait()

    return pl.pallas_call(
        kern, name=name, out_shape=_exchange_out_shapes(arrs, [True] * n),
        in_specs=[pl.BlockSpec(memory_space=pl.ANY)] * n,
        out_specs=[pl.BlockSpec(memory_space=pl.ANY)] * n,
        scratch_shapes=_exchange_sems(n),
        compiler_params=pltpu.CompilerParams(has_side_effects=True),
    )(*arrs)


def _exchange_out_shapes(arrs, gather):
    return [_sds((N_DEV,) + tuple(a.shape) if g else tuple(a.shape), a.dtype) for a, g in zip(arrs, gather)]


def _exchange_sems(n):
    return [pltpu.SemaphoreType.DMA((n, N_DEV)), pltpu.SemaphoreType.DMA((n, N_DEV)), pltpu.SemaphoreType.DMA((n,))]


def _exchange_copies(ins, outs, gather, send_sems, recv_sems, loc_sems, recv=True):
    n = len(ins)
    x, y, c = lax.axis_index("x"), lax.axis_index("y"), lax.axis_index("c")
    me = 4 * x + 2 * y + c

    def src(i, j):
        return ins[i] if gather[i] else ins[i].at[j]

    local = [pltpu.make_async_copy(src(i, me), outs[i].at[me], loc_sems.at[i]) for i in range(n)]
    sends, recvs = [], []
    for k in range(1, N_DEV):
        px = 1 - x if k & 4 else x
        py = 1 - y if k & 2 else y
        pc = 1 - c if k & 1 else c
        p = 4 * px + 2 * py + pc
        for i in range(n):
            sends.append(pltpu.make_async_remote_copy(
                src_ref=src(i, p), dst_ref=outs[i].at[me], send_sem=send_sems.at[i, k],
                recv_sem=recv_sems.at[i, k], device_id=(px, py, pc), device_id_type=pl.DeviceIdType.MESH))
            if recv:
                recvs.append(pltpu.make_async_remote_copy(
                    src_ref=src(i, p), dst_ref=outs[i].at[p], send_sem=send_sems.at[i, k],
                    recv_sem=recv_sems.at[i, k], device_id=(px, py, pc), device_id_type=pl.DeviceIdType.MESH))
    return local, sends, recvs


def _exchange_start(copies):
    local, sends, _ = copies
    for cp in local + sends:
        cp.start()


def _exchange_wait(copies):
    local, sends, recvs = copies
    for cp in recvs:
        cp.wait_recv()
    for cp in sends:
        cp.wait_send()
    for cp in local:
        cp.wait()


def _modpart(cact8, w_ada, b_cols):
    n_l, d, cw = w_ada.shape

    def kern(c_ref, w_ref, b_ref, o_ref):
        o_ref[0] = jnp.dot(c_ref[...].astype(MXU), w_ref[0].astype(MXU), preferred_element_type=F32) + b_ref[0]

    return pl.pallas_call(
        kern, name="modpart", grid=(n_l,), out_shape=_sds((n_l, N_DEV, cw)),
        in_specs=[_full((N_DEV, d)), pl.BlockSpec((1, d, cw), lambda l: (l, 0, 0)),
                  pl.BlockSpec((1, 1, cw), lambda l: (l, 0, 0))],
        out_specs=pl.BlockSpec((1, N_DEV, cw), lambda l: (l, 0, 0)),
        compiler_params=_params("arbitrary"),
    )(cact8, w_ada, b_cols)


def _ada_grad(cact_cols, dmod_cols):
    n_l, _, cw = dmod_cols.shape
    d = cact_cols.shape[1]

    def kern(c_ref, dm_ref, o_ref):
        acc = c_ref[0] * dm_ref[0, 0:1, :]
        for s in range(1, N_DEV):
            acc = acc + c_ref[s] * dm_ref[0, s:s + 1, :]
        o_ref[0] = acc

    return pl.pallas_call(
        kern, name="ada_grad", grid=(n_l,), out_shape=_sds((n_l, d, cw)),
        in_specs=[_full((N_DEV, d, 1)), pl.BlockSpec((1, N_DEV, cw), lambda l: (l, 0, 0))],
        out_specs=pl.BlockSpec((1, d, cw), lambda l: (l, 0, 0)),
        compiler_params=_params("arbitrary"),
    )(cact_cols, dmod_cols)


def _k_in(x, g, mod3, w, tm=256):
    s_len, d = x.shape
    qkv_w = 3 * GROUP_W

    def kern(x_ref, g_ref, mod_ref, w_ref, h_ref, qkv_ref, gates_ref, tail_ref):
        xv = x_ref[...]
        r = lax.rsqrt(jnp.mean(xv * xv, axis=-1, keepdims=True) + EPS)
        xn = xv * r * g_ref[...]
        h = (xn * (1.0 + mod_ref[1:2, :]) + mod_ref[0:1, :]).astype(MXU)
        h_ref[...] = h
        z = jnp.dot(h, w_ref[...], preferred_element_type=F32)
        qkv_ref[:, :GROUP_W] = (z[:, Z_FQ:Z_FQ + GROUP_W] * (FOX_SCALE * LOG2E)).astype(MXU)
        qkv_ref[:, GROUP_W:] = z[:, Z_FK:Z_FK + 2 * GROUP_W].astype(MXU)
        gates_ref[...] = z[:, Z_FG:Z_QL]
        tail_ref[...] = z[:, Z_QL:]

    return pl.pallas_call(
        kern, name="k_in", grid=(s_len // tm,),
        out_shape=[_sds((s_len, d), MXU), _sds((s_len, qkv_w), MXU), _sds((s_len, Z_QL - Z_FG)),
                   _sds((s_len, TAIL_W))],
        in_specs=[_rows(tm, d), _full((1, d)), _full((3, d)), _full((d, Z_W))],
        out_specs=[_rows(tm, d), _rows(tm, qkv_w), _rows(tm, Z_QL - Z_FG), _rows(tm, TAIL_W)],
        compiler_params=_params("arbitrary"),
    )(x, g, mod3, w)


def _scan_matrices(rows, chunks, reverse):
    r_i = lax.broadcasted_iota(jnp.int32, (LANES, LANES), 0)
    c_i = lax.broadcasted_iota(jnp.int32, (LANES, LANES), 1)
    a_i = lax.broadcasted_iota(jnp.int32, (rows, rows), 0)
    b_i = lax.broadcasted_iota(jnp.int32, (rows, rows), 1)
    same_head = (a_i // chunks) == (b_i // chunks)
    if reverse:
        return (r_i >= c_i).astype(F32), (same_head & (b_i > a_i)).astype(F32)
    return (r_i <= c_i).astype(F32), (same_head & (b_i < a_i)).astype(F32)


def _scan_rows(x, inner, outer):
    tot = jnp.broadcast_to(jnp.sum(x, axis=1, keepdims=True), x.shape)
    return (jnp.dot(x, inner, precision=lax.Precision.HIGHEST, preferred_element_type=F32)
            + jnp.dot(outer, tot, precision=lax.Precision.HIGHEST, preferred_element_type=F32))


def _k_cum(ff_rows, b_rows, chunks):
    rows = ff_rows.shape[0]

    def kern(ff_ref, b_ref, cum_ref):
        xc = ff_ref[...] + b_ref[...]
        lf = jnp.minimum(xc, 0.0) - jnp.log(1.0 + jnp.exp(-jnp.abs(xc)))
        cum_ref[...] = _scan_rows(lf, *_scan_matrices(rows, chunks, False))

    return pl.pallas_call(
        kern, name="k_cum", out_shape=_sds((rows, LANES)),
        in_specs=[pl.BlockSpec(memory_space=pltpu.VMEM)] * 2,
        out_specs=pl.BlockSpec(memory_space=pltpu.VMEM),
        compiler_params=_params(),
    )(ff_rows, b_rows)


def _k_cum_bwd(dc_rows, ff_rows, b_rows, chunks):
    rows = ff_rows.shape[0]

    def kern(dc_ref, ff_ref, b_ref, dff_ref, db_ref):
        dlf = _scan_rows(dc_ref[...], *_scan_matrices(rows, chunks, True))
        dff = dlf * jax.nn.sigmoid(-(ff_ref[...] + b_ref[...]))
        dff_ref[...] = dff
        db_ref[...] = jnp.broadcast_to(jnp.sum(dff, axis=1, keepdims=True), dff.shape)

    return pl.pallas_call(
        kern, name="k_cum_bwd", out_shape=[_sds((rows, LANES)), _sds((rows, LANES))],
        in_specs=[pl.BlockSpec(memory_space=pltpu.VMEM)] * 3,
        out_specs=[pl.BlockSpec(memory_space=pltpu.VMEM)] * 2,
        compiler_params=_params(),
    )(dc_rows, ff_rows, b_rows)


def _swap16(t):
    lane = lax.broadcasted_iota(jnp.int32, t.shape, 1)
    return jnp.where(lane % ROPE < HALF_ROPE, pltpu.roll(t, LANES - HALF_ROPE, 1), pltpu.roll(t, HALF_ROPE, 1))


def _rope(t, cos, sin):
    return t * cos + _swap16(t) * sin


def _rope_bwd(dt, cos, sin):
    return dt * cos - _swap16(dt) * sin


def _k_prep(tail, cos, sin, gq, gkv, wuq, wukv, tm=512):
    s_len = tail.shape[0]
    qc = MLA_SCALE * LOG2E

    def kern(tail_ref, cos_ref, sin_ref, gq_ref, gkv_ref, wuq_ref, wukv_ref,
             qn_out, qr_out, kn_out, v_out, kr_out, qn_ref, kvn_ref):
        cs, sn = cos_ref[...], sin_ref[...]
        ql = tail_ref[:, :T_KV]
        rq = lax.rsqrt(jnp.mean(ql * ql, axis=-1, keepdims=True) + EPS)
        qn = (ql * rq * gq_ref[...]).astype(MXU)
        qn_ref[...] = qn
        q = jnp.dot(qn, wuq_ref[...], preferred_element_type=F32)
        qn_out[...] = (q[:, :GROUP_W] * qc).astype(MXU)
        for blk in range(PAIRS):
            lo, sw = GROUP_W + blk * LANES, 2 * GROUP_W + blk * LANES
            qr_out[:, blk * LANES:(blk + 1) * LANES] = (
                (q[:, lo:lo + LANES] * cs + q[:, sw:sw + LANES] * sn) * qc).astype(MXU)
        kvl = tail_ref[:, T_KV:T_MISC]
        rk = lax.rsqrt(jnp.mean(kvl * kvl, axis=-1, keepdims=True) + EPS)
        kvn = (kvl * rk * gkv_ref[...]).astype(MXU)
        kvn_ref[...] = kvn
        kv = jnp.dot(kvn, wukv_ref[...], preferred_element_type=F32)
        kn_out[...] = kv[:, :GROUP_W].astype(MXU)
        v_out[...] = kv[:, GROUP_W:].astype(MXU)
        misc = tail_ref[:, T_MISC:]
        lane = lax.broadcasted_iota(jnp.int32, misc.shape, 1)
        kr = jnp.where(lane < ROPE, _rope(misc, cs, sn), 0.0)
        kr_out[...] = (kr + pltpu.roll(kr, HEAD_DIM, 1)).astype(MXU)

    return pl.pallas_call(
        kern, name="k_prep", grid=(s_len // tm,),
        out_shape=[_sds((s_len, GROUP_W), MXU), _sds((s_len, GROUP_W), MXU), _sds((s_len, GROUP_W), MXU),
                   _sds((s_len, GROUP_W), MXU), _sds((s_len, LANES), MXU), _sds((s_len, Q_LORA), MXU),
                   _sds((s_len, KV_LORA), MXU)],
        in_specs=[_rows(tm, TAIL_W), _rows(tm, LANES), _rows(tm, LANES),
                  _full((1, Q_LORA)), _full((1, KV_LORA)), _full((Q_LORA, 3 * GROUP_W)),
                  _full((KV_LORA, 2 * GROUP_W))],
        out_specs=[_rows(tm, GROUP_W), _rows(tm, GROUP_W), _rows(tm, GROUP_W), _rows(tm, GROUP_W), _rows(tm, LANES),
                   _rows(tm, Q_LORA), _rows(tm, KV_LORA)],
        compiler_params=_params("arbitrary"),
    )(tail, cos, sin, gq, gkv, wuq, wukv)


def _block_mask(kn, qn, q_off, chunk_mask, transposed):
    shape = (kn, qn) if transposed else (qn, kn)
    row = lax.broadcasted_iota(jnp.int32, shape, 0)
    col = lax.broadcasted_iota(jnp.int32, shape, 1)
    qi, ki = (col + q_off, row) if transposed else (row + q_off, col)
    if chunk_mask:
        return (ki // CHUNK) <= (qi // CHUNK)
    return ki <= qi


def _head_operand(x, hh, other=None):
    lane = lax.broadcasted_iota(jnp.int32, x.shape, 1)
    own = (lane >= hh * HEAD_DIM) & (lane < (hh + 1) * HEAD_DIM)
    return jnp.where(own, x, jnp.zeros_like(x) if other is None else other)


def _attention_fwd(q, q_blk, k, k_blk, v, v_blk, bias, rope, chunk_mask, name, side=None):
    s_len = q.shape[0]
    t = min(ATTN_TILE, s_len // 2)
    nq = s_len // t
    n_side = len(side[0]) if side else 0

    def kern(*refs):
        q_ref, k_ref, v_ref = refs[:3]
        pos = 3
        if bias is not None:
            ck_ref = refs[pos]
            pos += 1
        if rope is not None:
            qr_ref, kr_ref = refs[pos:pos + 2]
            pos += 2
        side_in = refs[pos:pos + n_side]
        pos += n_side
        o_ref, lse_ref = refs[pos:pos + 2]
        side_out = refs[pos + 2:pos + 2 + n_side]
        vt_scr, m_scr, acc_scr, ck_scr = refs[pos + 2 + n_side:pos + 6 + n_side]
        sems = refs[pos + 6 + n_side:]
        pj = pl.program_id(0)
        if n_side:
            @pl.when(pj == 0)
            def _():
                _exchange_start(_exchange_copies(side_in, side_out, side[1], *sems, recv=False))
        vt_scr[:, HEAD_DIM:, :] = jnp.ones((2, V_ROWS - HEAD_DIM, s_len), vt_scr.dtype)
        for i in range(nq):
            vtt = v_ref[i * t:(i + 1) * t, :].T
            for hh in range(2):
                vt_scr[hh, :HEAD_DIM, i * t:(i + 1) * t] = vtt[hh * HEAD_DIM:(hh + 1) * HEAD_DIM, :]
                if bias is not None:
                    ckt = ck_ref[i * t:(i + 1) * t, :]
                    lane = lax.broadcasted_iota(jnp.int32, ckt.shape, 1)
                    ck_scr[hh, i * t:(i + 1) * t, :] = jnp.sum(jnp.where(lane == 2 * pj + hh, ckt, 0.0), axis=1,
                                                               keepdims=True)

        def qbody(qi, _):
            qs = pl.multiple_of(qi * t, t)
            qt = q_ref[pl.ds(qs, t), :]
            qrt = qr_ref[pl.ds(qs, t), :] if rope is not None else None
            qh = [_head_operand(qt, hh, qrt) for hh in range(2)]
            m_scr[...] = jnp.full(m_scr.shape, -jnp.inf, F32)
            acc_scr[...] = jnp.zeros(acc_scr.shape, F32)

            def block(ks, kn, q0, qn, masked):
                kt = k_ref[pl.ds(ks, kn), :]
                kh = [_head_operand(kt, hh, kr_ref[pl.ds(ks, kn), :]) for hh in range(2)] if rope is not None else [kt, kt]
                qc = slice(q0, q0 + qn)
                sts = [lax.dot_general(kh[hh], qh[hh][qc], _NT, preferred_element_type=F32) for hh in range(2)]
                if bias is not None:
                    sts = [sts[hh] - ck_scr[hh, pl.ds(ks, kn), :] for hh in range(2)]
                if masked:
                    sts = [jnp.where(_block_mask(kn, qn, q0, chunk_mask, True), st, -jnp.inf) for st in sts]
                m_old = [m_scr[hh, :, qc] for hh in range(2)]
                m_new = [jnp.maximum(m_old[hh], jnp.max(sts[hh], axis=0, keepdims=True)) for hh in range(2)]
                pts = [jnp.exp2(sts[hh] - m_new[hh]).astype(MXU) for hh in range(2)]
                for hh in range(2):
                    alpha = jnp.exp2(m_old[hh] - m_new[hh])
                    acc_scr[hh, :, qc] = alpha * acc_scr[hh, :, qc] + jnp.dot(vt_scr[hh, :, pl.ds(ks, kn)], pts[hh],
                                                                            preferred_element_type=F32)
                    m_scr[hh, :, qc] = m_new[hh]

            def loop_body(ki, carry):
                block(pl.multiple_of(ki * t, t), t, 0, t, False)
                return carry

            lax.fori_loop(0, qi, loop_body, 0)
            block(qs, t, 0, t, True)
            outs = []
            for hh in range(2):
                acc = acc_scr[hh]
                l = acc[HEAD_DIM:HEAD_DIM + 1, :]
                outs.append(acc[:HEAD_DIM, :] / l)
                lse_ref[hh, :, pl.ds(qs, t)] = m_scr[hh] + jnp.log2(l)
            o_ref[pl.ds(qs, t), :] = jnp.concatenate(outs, axis=0).T
            return 0

        lax.fori_loop(0, nq, qbody, 0)
        if n_side:
            @pl.when(pj == PAIRS - 1)
            def _():
                _exchange_wait(_exchange_copies(side_in, side_out, side[1], *sems))

    def tok(blk):
        return pl.BlockSpec((s_len, LANES), lambda j: (0, blk + j))

    rowb = pl.BlockSpec((2, 1, s_len), lambda j: (j, 0, 0))
    hbm = pl.BlockSpec(memory_space=pl.ANY)
    ins = [q, k, v]
    in_specs = [tok(q_blk), tok(k_blk), tok(v_blk)]
    if bias is not None:
        ins.append(bias)
        in_specs.append(_full((s_len, LANES)))
    if rope is not None:
        ins += list(rope)
        in_specs += [tok(0), _full((s_len, LANES))]
    out_shape = [_sds((s_len, PAIRS * LANES)), _sds((HEADS, 1, s_len))]
    scratch = [pltpu.VMEM((2, V_ROWS, s_len), v.dtype), pltpu.VMEM((2, 1, t), F32), pltpu.VMEM((2, V_ROWS, t), F32),
               pltpu.VMEM((2, s_len if bias is not None else 8, 1), F32)]
    if n_side:
        ins += list(side[0])
        out_shape += _exchange_out_shapes(*side)
        scratch += _exchange_sems(n_side)
    return pl.pallas_call(
        kern, name=name, grid=(PAIRS,), out_shape=out_shape,
        in_specs=in_specs + [hbm] * n_side, out_specs=[tok(0), rowb] + [hbm] * n_side,
        scratch_shapes=scratch,
        compiler_params=_params("arbitrary", side_effects=bool(n_side)),
    )(*ins)


def _attention_bwd(q, q_blk, k, k_blk, v, v_blk, do, pack, ck_row, rope, chunk_mask, q_scale, k_scale, name,
                   side=None):
    s_len = q.shape[0]
    t = min(ATTN_TILE, s_len // 2)
    nq = s_len // t
    has_bias = ck_row is not None
    nv = 2 if rope is not None else 1
    n_side = len(side[0]) if side else 0

    def kern(*refs):
        q_ref, k_ref, v_ref, do_ref, pack_ref = refs[:5]
        pos = 5
        if has_bias:
            ck_ref = refs[pos]
            pos += 1
        if rope is not None:
            qr_ref, kr_ref = refs[pos:pos + 2]
            pos += 2
        side_in = refs[pos:pos + n_side]
        pos += n_side
        dq_ref, dk_ref, dv_ref = refs[pos:pos + 3]
        pos += 3
        if has_bias:
            dcq_ref, dck_ref = refs[pos:pos + 2]
            pos += 2
        if rope is not None:
            dqr_ref, dkr_ref = refs[pos:pos + 2]
            pos += 2
        side_out = refs[pos:pos + n_side]
        pos += n_side
        qt_scr, dot_scr, dkt_scr, dvt_scr, dq_scr, dcq_scr = refs[pos:pos + 6]
        sems = refs[pos + 6:]
        pj = pl.program_id(0)
        if n_side:
            @pl.when(pj == 0)
            def _():
                _exchange_start(_exchange_copies(side_in, side_out, side[1], *sems, recv=False))

        for i in range(nq):
            sl = slice(i * t, (i + 1) * t)
            dot_scr[:, sl] = do_ref[sl, :].T
            if rope is not None:
                for hh in range(2):
                    qt_scr[hh, :, sl] = _head_operand(q_ref[sl, :], hh, qr_ref[sl, :]).T
            else:
                qt_scr[0, :, sl] = q_ref[sl, :].T
        dkt_scr[...] = jnp.zeros(dkt_scr.shape, F32)
        dvt_scr[...] = jnp.zeros(dvt_scr.shape, F32)
        if has_bias:
            dck_ref[...] = jnp.zeros(dck_ref.shape, F32)

            @pl.when(pj == 0)
            def _():
                dcq_ref[...] = jnp.zeros(dcq_ref.shape, F32)

        def qbody(qi, _):
            qs = pl.multiple_of(qi * t, t)
            qt = q_ref[pl.ds(qs, t), :]
            qrt = qr_ref[pl.ds(qs, t), :] if rope is not None else None
            dot = do_ref[pl.ds(qs, t), :]
            pk = pack_ref[pl.ds(qs, t), :]
            lane = lax.broadcasted_iota(jnp.int32, pk.shape, 1)
            qh = [_head_operand(qt, hh, qrt) for hh in range(2)]
            doh = [_head_operand(dot, hh) for hh in range(2)]
            a_col = [jnp.sum(jnp.where(lane == 2 * pj + hh, pk, 0.0), axis=1, keepdims=True) for hh in range(2)]
            d_col = [jnp.sum(jnp.where(lane == HEADS + 2 * pj + hh, pk, 0.0), axis=1, keepdims=True)
                     for hh in range(2)]
            dq_scr[...] = jnp.zeros(dq_scr.shape, F32)
            if has_bias:
                dcq_scr[...] = jnp.zeros(dcq_scr.shape, F32)

            def block(ks, kn, q0, qn, masked):
                kt = k_ref[pl.ds(ks, kn), :]
                krt = kr_ref[pl.ds(ks, kn), :] if rope is not None else None
                kh = [_head_operand(kt, hh, krt) for hh in range(2)]
                vt = v_ref[pl.ds(ks, kn), :]
                qr_ = slice(q0, q0 + qn)
                qcols = pl.ds(pl.multiple_of(qs + q0, t // 2), qn)
                ss = [lax.dot_general(qh[hh][qr_], kh[hh] if rope is not None else kt, _NT,
                                      preferred_element_type=F32) + a_col[hh][qr_] for hh in range(2)]
                if has_bias:
                    ss = [ss[hh] - ck_ref[hh, :, pl.ds(ks, kn)] for hh in range(2)]
                dpds = [lax.dot_general(doh[hh][qr_], vt, _NT, preferred_element_type=F32) for hh in range(2)]
                ps = [jnp.exp2(s) for s in ss]
                if masked:
                    ps = [jnp.where(_block_mask(kn, qn, q0, chunk_mask, False), p, 0.0) for p in ps]
                dss = [ps[hh] * (dpds[hh] - d_col[hh][qr_]) for hh in range(2)]
                for hh in range(2):
                    rows = slice(hh * HEAD_DIM, (hh + 1) * HEAD_DIM)
                    dsb = dss[hh].astype(MXU)
                    dvt_scr[rows, pl.ds(ks, kn)] += jnp.dot(dot_scr[rows, qcols], ps[hh].astype(MXU),
                                                            preferred_element_type=F32)
                    if rope is not None:
                        dkt_scr[hh, :, pl.ds(ks, kn)] += jnp.dot(qt_scr[hh, :, qcols], dsb,
                                                                 preferred_element_type=F32)
                    else:
                        dkt_scr[0, rows, pl.ds(ks, kn)] += jnp.dot(qt_scr[0, rows, qcols], dsb,
                                                                   preferred_element_type=F32)
                    dq_scr[hh if rope is not None else 0, qr_, :] += jnp.dot(dsb, kh[hh], preferred_element_type=F32)
                    if has_bias:
                        dcq_scr[hh, qr_, :] += jnp.sum(dss[hh], axis=1, keepdims=True)
                        dck_ref[hh, :, pl.ds(ks, kn)] += -jnp.sum(dss[hh], axis=0, keepdims=True)

            def loop_body(ki, carry):
                block(pl.multiple_of(ki * t, t), t, 0, t, False)
                return carry

            lax.fori_loop(0, qi, loop_body, 0)
            block(qs, t // 2, 0, t // 2, True)
            block(qs, t, t // 2, t // 2, True)
            if rope is not None:
                first = lane < HEAD_DIM
                dq_ref[pl.ds(qs, t), :] = (jnp.where(first, dq_scr[0], dq_scr[1]) * q_scale).astype(dq_ref.dtype)
                dqr_ref[pl.ds(qs, t), :] = jnp.where(first, dq_scr[1], dq_scr[0]) * q_scale
            else:
                dq_ref[pl.ds(qs, t), :] = (dq_scr[0] * q_scale).astype(dq_ref.dtype)
            if has_bias:
                old = dcq_ref[pl.ds(qs, t), :]
                dcq_ref[pl.ds(qs, t), :] = jnp.where(lane == 2 * pj, dcq_scr[0],
                                                     jnp.where(lane == 2 * pj + 1, dcq_scr[1], old))
            return 0

        lax.fori_loop(0, nq, qbody, 0)
        for i in range(nq):
            sl = slice(i * t, (i + 1) * t)
            dv_ref[sl, :] = dvt_scr[:, sl].T.astype(dv_ref.dtype)
            if rope is not None:
                d0, d1 = dkt_scr[0, :, sl], dkt_scr[1, :, sl]
                first = lax.broadcasted_iota(jnp.int32, d0.shape, 0) < HEAD_DIM
                dk_ref[sl, :] = (jnp.where(first, d0, d1).T * k_scale).astype(dk_ref.dtype)
                dkr_ref[0, sl, :] = jnp.where(first, d1, d0).T * k_scale
            else:
                dk_ref[sl, :] = (dkt_scr[0, :, sl].T * k_scale).astype(dk_ref.dtype)
        if n_side:
            @pl.when(pj == PAIRS - 1)
            def _():
                _exchange_wait(_exchange_copies(side_in, side_out, side[1], *sems))

    def tok(blk):
        return pl.BlockSpec((s_len, LANES), lambda j: (0, blk + j))

    shared = _full((s_len, LANES))
    rowb = pl.BlockSpec((2, 1, s_len), lambda j: (j, 0, 0))
    slab = pl.BlockSpec((1, s_len, LANES), lambda j: (j, 0, 0))
    hbm = pl.BlockSpec(memory_space=pl.ANY)
    ins = [q, k, v, do, pack]
    in_specs = [tok(q_blk), tok(k_blk), tok(v_blk), tok(0), shared]
    out_shape = [_sds((s_len, PAIRS * LANES), MXU)] * 3
    out_specs = [tok(0)] * 3
    if has_bias:
        ins.append(ck_row)
        in_specs.append(rowb)
        out_shape += [_sds((s_len, LANES)), _sds((HEADS, 1, s_len))]
        out_specs += [shared, rowb]
    if rope is not None:
        ins += list(rope)
        in_specs += [tok(0), shared]
        out_shape += [_sds((s_len, PAIRS * LANES)), _sds((PAIRS, s_len, LANES))]
        out_specs += [tok(0), slab]
    scratch = [pltpu.VMEM((nv, LANES, s_len), q.dtype), pltpu.VMEM((LANES, s_len), do.dtype),
               pltpu.VMEM((nv, LANES, s_len), F32), pltpu.VMEM((LANES, s_len), F32),
               pltpu.VMEM((nv, t, LANES), F32), pltpu.VMEM((2, t, 1), F32)]
    if n_side:
        ins += list(side[0])
        out_shape += _exchange_out_shapes(*side)
        scratch += _exchange_sems(n_side)
    return pl.pallas_call(
        kern, name=name, grid=(PAIRS,), out_shape=out_shape,
        in_specs=in_specs + [hbm] * n_side, out_specs=out_specs + [hbm] * n_side, scratch_shapes=scratch,
        compiler_params=_params("arbitrary", side_effects=bool(n_side)),
    )(*ins)


def _silu(a):
    return a * jax.nn.sigmoid(a)


def _k_out(of, om, gates, x, gate, wout, tm=256):
    s_len, d = x.shape

    def kern(of_ref, om_ref, gates_ref, x_ref, gate_ref, w_ref, xo_ref, y_ref, u_ref):
        u_ref[:, :GROUP_W] = (of_ref[...] * _silu(gates_ref[:, :GROUP_W])).astype(MXU)
        u_ref[:, GROUP_W:] = (om_ref[...] * _silu(gates_ref[:, GROUP_W:])).astype(MXU)
        y = jnp.dot(u_ref[...], w_ref[...], preferred_element_type=F32)
        y_ref[...] = y
        xo_ref[...] = x_ref[...] + gate_ref[...] * y

    return pl.pallas_call(
        kern, name="k_out", grid=(s_len // tm,),
        out_shape=[_sds((s_len, d)), _sds((s_len, d)), _sds((s_len, 2 * GROUP_W), MXU)],
        in_specs=[_rows(tm, GROUP_W), _rows(tm, GROUP_W), _rows(tm, 2 * GROUP_W), _rows(tm, d), _full((1, d)),
                  _full((2 * GROUP_W, d))],
        out_specs=[_rows(tm, d), _rows(tm, d), _rows(tm, 2 * GROUP_W)],
        compiler_params=_params("arbitrary"),
    )(of, om, gates, x, gate, wout)


def _k_loss(x, gf, tgt, tm=256):
    s_len, d = x.shape

    def kern(x_ref, g_ref, t_ref, loss_ref, dx_ref, dg_ref):
        i = pl.program_id(0)
        xv = x_ref[...]
        r = lax.rsqrt(jnp.mean(xv * xv, axis=-1, keepdims=True) + EPS)
        xh = xv * r
        diff = xh * g_ref[...] - t_ref[...]
        part = 0.5 * jnp.sum(jnp.mean(diff * diff, axis=-1, keepdims=True))
        dout = diff * (1.0 / d)
        dxh = dout * g_ref[...]
        dx_ref[...] = r * (dxh - xh * jnp.mean(dxh * xh, axis=-1, keepdims=True))

        @pl.when(i == 0)
        def _():
            loss_ref[...] = jnp.zeros_like(loss_ref)
            dg_ref[...] = jnp.zeros_like(dg_ref)

        loss_ref[...] += jnp.full(loss_ref.shape, part, F32)
        dg_ref[...] += jnp.sum(dout * xh, axis=0, keepdims=True)

    return pl.pallas_call(
        kern, name="k_loss", grid=(s_len // tm,),
        out_shape=[_sds((1, LANES)), _sds((s_len, d)), _sds((1, d))],
        in_specs=[_rows(tm, d), _full((1, d)), _rows(tm, d)],
        out_specs=[_full((1, LANES)), _rows(tm, d), _full((1, d))],
        compiler_params=_params("arbitrary"),
    )(x, gf, tgt)


def _kb_out(dxo, y, u, gate, wout, of, om, gates, dw_dtype, tm=512):
    s_len, d = dxo.shape
    steps = s_len // tm

    def kern(dxo_ref, y_ref, u_ref, gate_ref, wt_ref, of_ref, om_ref, gates_ref,
             dof_ref, dom_ref, dfg_ref, dmg_ref, dlf_ref, dlm_ref, dgate_ref, dw_ref, dw_acc):
        i = pl.program_id(0)
        dxv = dxo_ref[...]

        @pl.when(i == 0)
        def _():
            dgate_ref[...] = jnp.zeros_like(dgate_ref)
            dw_acc[...] = jnp.zeros_like(dw_acc)

        dgate_ref[...] += jnp.sum(dxv * y_ref[...], axis=0, keepdims=True)
        dyb = (dxv * gate_ref[...]).astype(MXU)
        dw_acc[...] += lax.dot_general(u_ref[...], dyb, _TN, preferred_element_type=F32)

        @pl.when(i == steps - 1)
        def _():
            dw_ref[...] = dw_acc[...].astype(dw_ref.dtype)

        du = lax.dot_general(dyb, wt_ref[...], _NT, preferred_element_type=F32)
        head_of = (lax.broadcasted_iota(jnp.int32, (GROUP_W, LANES), 0) // HEAD_DIM
                   == lax.broadcasted_iota(jnp.int32, (GROUP_W, LANES), 1)).astype(F32)
        for du_g, o_ref, a, do_ref, dg_ref, dl_ref in (
                (du[:, :GROUP_W], of_ref, gates_ref[:, :GROUP_W], dof_ref, dfg_ref, dlf_ref),
                (du[:, GROUP_W:], om_ref, gates_ref[:, GROUP_W:], dom_ref, dmg_ref, dlm_ref)):
            sg = jax.nn.sigmoid(a)
            ov = o_ref[...]
            dov = du_g * (a * sg)
            do_ref[...] = dov.astype(MXU)
            dg_ref[...] = (du_g * ov * (sg * (1.0 + a * (1.0 - sg)))).astype(MXU)
            dl_ref[...] = jnp.dot(dov * ov, head_of, precision=lax.Precision.HIGH, preferred_element_type=F32)

    return pl.pallas_call(
        kern, name="kb_out", grid=(steps,),
        out_shape=[_sds((s_len, GROUP_W), MXU), _sds((s_len, GROUP_W), MXU),
                   _sds((s_len, GROUP_W), MXU), _sds((s_len, GROUP_W), MXU), _sds((s_len, LANES)),
                   _sds((s_len, LANES)), _sds((1, d)), _sds((2 * GROUP_W, d), dw_dtype)],
        in_specs=[_rows(tm, d), _rows(tm, d), _rows(tm, 2 * GROUP_W), _full((1, d)), _full((2 * GROUP_W, d)),
                  _rows(tm, GROUP_W), _rows(tm, GROUP_W), _rows(tm, 2 * GROUP_W)],
        out_specs=[_rows(tm, GROUP_W), _rows(tm, GROUP_W), _rows(tm, GROUP_W),
                   _rows(tm, GROUP_W), _rows(tm, LANES), _rows(tm, LANES), _full((1, d)), _full((2 * GROUP_W, d))],
        scratch_shapes=[pltpu.VMEM((2 * GROUP_W, d), F32)],
        compiler_params=_params("arbitrary"),
    )(dxo, y, u, gate, wout, of, om, gates)


def _kb_prep(dqn, dqr, dkn, dv, dkr, dff, tail, qn, kvn, cos, sin, gq, gkv, wuq_t, wukv_t, dw_dtype, tm=512):
    s_len = tail.shape[0]
    qw = 3 * GROUP_W
    steps = s_len // tm

    def kern(dqn_ref, dqr_ref, dkn_ref, dv_ref, dkr_ref, dff_ref, tail_ref, qn_ref, kvn_ref, cos_ref, sin_ref,
             gq_ref, gkv_ref, wuqt_ref, wukvt_ref, dz_ref, dgq_ref, dgkv_ref, dwuq_ref, dwukv_ref,
             dq_ref, uq_acc, ukv_acc):
        i = pl.program_id(0)

        @pl.when(i == 0)
        def _():
            dgq_ref[...] = jnp.zeros_like(dgq_ref)
            dgkv_ref[...] = jnp.zeros_like(dgkv_ref)
            uq_acc[...] = jnp.zeros_like(uq_acc)
            ukv_acc[...] = jnp.zeros_like(ukv_acc)

        cs, sn = cos_ref[...], sin_ref[...]
        dq_ref[:, :GROUP_W] = dqn_ref[...]
        for blk in range(PAIRS):
            sl = slice(blk * LANES, (blk + 1) * LANES)
            dq_ref[:, GROUP_W + blk * LANES:GROUP_W + (blk + 1) * LANES] = (dqr_ref[:, sl] * cs).astype(MXU)
            dq_ref[:, 2 * GROUP_W + blk * LANES:2 * GROUP_W + (blk + 1) * LANES] = (dqr_ref[:, sl] * sn).astype(MXU)
        dqn = lax.dot_general(dq_ref[...], wuqt_ref[...], _NT, preferred_element_type=F32)
        uq_acc[...] += lax.dot_general(qn_ref[...], dq_ref[...], _TN, preferred_element_type=F32)
        ukv_acc[:, :GROUP_W] += lax.dot_general(kvn_ref[...], dkn_ref[...], _TN, preferred_element_type=F32)
        ukv_acc[:, GROUP_W:] += lax.dot_general(kvn_ref[...], dv_ref[...], _TN, preferred_element_type=F32)

        @pl.when(i == steps - 1)
        def _():
            dwuq_ref[...] = uq_acc[...].astype(dwuq_ref.dtype)
            dwukv_ref[...] = ukv_acc[...].astype(dwukv_ref.dtype)

        ql = tail_ref[:, :T_KV]
        rq = lax.rsqrt(jnp.mean(ql * ql, axis=-1, keepdims=True) + EPS)
        qh = ql * rq
        dgq_ref[...] += jnp.sum(dqn * qh, axis=0, keepdims=True)
        dqh = dqn * gq_ref[...]
        dz_ref[:, :Q_LORA] = (rq * (dqh - qh * jnp.mean(dqh * qh, axis=-1, keepdims=True))).astype(MXU)

        dkvn = (lax.dot_general(dkn_ref[...], wukvt_ref[:, :GROUP_W], _NT, preferred_element_type=F32)
                + lax.dot_general(dv_ref[...], wukvt_ref[:, GROUP_W:], _NT, preferred_element_type=F32))
        kvl = tail_ref[:, T_KV:T_MISC]
        rk = lax.rsqrt(jnp.mean(kvl * kvl, axis=-1, keepdims=True) + EPS)
        kh = kvl * rk
        dgkv_ref[...] += jnp.sum(dkvn * kh, axis=0, keepdims=True)
        dkh = dkvn * gkv_ref[...]
        dz_ref[:, Q_LORA:Q_LORA + KV_LORA] = (
            rk * (dkh - kh * jnp.mean(dkh * kh, axis=-1, keepdims=True))).astype(MXU)

        g = dkr_ref[...] + pltpu.roll(dkr_ref[...], HEAD_DIM, 1)
        lane = lax.broadcasted_iota(jnp.int32, g.shape, 1)
        dmisc = jnp.where(lane < ROPE, _rope_bwd(g, cs, sn), 0.0) + dff_ref[...]
        dz_ref[:, Q_LORA + KV_LORA:] = dmisc.astype(MXU)

    return pl.pallas_call(
        kern, name="kb_prep", grid=(steps,),
        out_shape=[_sds((s_len, TAIL_W), MXU), _sds((1, Q_LORA)), _sds((1, KV_LORA)),
                   _sds((Q_LORA, qw), dw_dtype), _sds((KV_LORA, 2 * GROUP_W), dw_dtype)],
        in_specs=[_rows(tm, GROUP_W), _rows(tm, GROUP_W), _rows(tm, GROUP_W), _rows(tm, GROUP_W), _rows(tm, LANES),
                  _rows(tm, LANES), _rows(tm, TAIL_W), _rows(tm, Q_LORA), _rows(tm, KV_LORA),
                  _rows(tm, LANES), _rows(tm, LANES), _full((1, Q_LORA)), _full((1, KV_LORA)),
                  _full((Q_LORA, qw)), _full((KV_LORA, 2 * GROUP_W))],
        out_specs=[_rows(tm, TAIL_W), _full((1, Q_LORA)), _full((1, KV_LORA)), _full((Q_LORA, qw)),
                   _full((KV_LORA, 2 * GROUP_W))],
        scratch_shapes=[pltpu.VMEM((tm, qw), MXU), pltpu.VMEM((Q_LORA, qw), F32),
                        pltpu.VMEM((KV_LORA, 2 * GROUP_W), F32)],
        compiler_params=_params("arbitrary"),
    )(dqn, dqr, dkn, dv, dkr, dff, tail, qn, kvn, cos, sin, gq, gkv, wuq_t, wukv_t)


def _kb_in(dz_pieces, w, h, x, g, mod3, dxo, dw_dtype, tm=512):
    s_len, d = x.shape
    widths = [p.shape[1] for p in dz_pieces]
    n_p = len(widths)
    steps = s_len // tm

    def kern(*refs):
        dz_refs = refs[:n_p]
        w_ref, h_ref, x_ref, g_ref, mod_ref, dxo_ref, dx_ref, acc_ref, dw_ref, dw_acc = refs[n_p:]
        i = pl.program_id(0)

        @pl.when(i == 0)
        def _():
            acc_ref[...] = jnp.zeros_like(acc_ref)
            dw_acc[...] = jnp.zeros_like(dw_acc)

        dh = jnp.zeros((tm, d), F32)
        ht = h_ref[...]
        lo = 0
        for p_ref, wd in zip(dz_refs, widths):
            dh = dh + lax.dot_general(p_ref[...], w_ref[:, lo:lo + wd], _NT, preferred_element_type=F32)
            dw_acc[:, lo:lo + wd] += lax.dot_general(ht, p_ref[...], _TN, preferred_element_type=F32)
            lo += wd

        @pl.when(i == steps - 1)
        def _():
            dw_ref[...] = dw_acc[...].astype(dw_ref.dtype)

        xv = x_ref[...]
        r = lax.rsqrt(jnp.mean(xv * xv, axis=-1, keepdims=True) + EPS)
        xh = xv * r
        xn = xh * g_ref[...]
        dxn = dh * (1.0 + mod_ref[1:2, :])
        acc_ref[0:1, :] += jnp.sum(dh, axis=0, keepdims=True)
        acc_ref[1:2, :] += jnp.sum(dh * xn, axis=0, keepdims=True)
        acc_ref[2:3, :] += jnp.sum(dxn * xh, axis=0, keepdims=True)
        dxh = dxn * g_ref[...]
        dx_ref[...] = dxo_ref[...] + r * (dxh - xh * jnp.mean(dxh * xh, axis=-1, keepdims=True))

    return pl.pallas_call(
        kern, name="kb_in", grid=(steps,),
        out_shape=[_sds((s_len, d)), _sds((3, d)), _sds((d, Z_W), dw_dtype)],
        in_specs=[_rows(tm, wd) for wd in widths] + [_full((d, Z_W)), _rows(tm, d), _rows(tm, d), _full((1, d)),
                                                     _full((3, d)), _rows(tm, d)],
        out_specs=[_rows(tm, d), _full((3, d)), _full((d, Z_W))],
        scratch_shapes=[pltpu.VMEM((d, Z_W), F32)],
        compiler_params=_params("arbitrary"),
    )(*dz_pieces, w, h, x, g, mod3, dxo)


def _adamw(slabs, w, m, v, name):
    n_l = len(slabs)
    n, r, c = slabs[0].shape
    tm = r
    for cand in (256, 128, 64, 32, 16, 8):
        if r % cand == 0:
            tm = cand
            break
    steps = r // tm

    def kern(*refs):
        g_refs = refs[:n_l]
        w_ref, m_ref, v_ref, go_ref, d_ref, mo_ref, vo_ref, g_scr = refs[n_l:]
        for ll in range(n_l):
            @pl.when(pl.program_id(0) == ll)
            def _(g_ref=g_refs[ll]):
                g = g_ref[0].astype(F32)
                for s in range(1, n):
                    g = g + g_ref[s].astype(F32)
                g_scr[...] = g

        g = g_scr[...]
        m_new = ADAM_B1 * m_ref[...] + (1.0 - ADAM_B1) * g
        v_new = ADAM_B2 * v_ref[...] + (1.0 - ADAM_B2) * (g * g)
        m_hat = m_new / (1.0 - ADAM_B1 ** ADAM_STEP)
        v_hat = v_new / (1.0 - ADAM_B2 ** ADAM_STEP)
        go_ref[...] = g
        mo_ref[...] = m_new
        vo_ref[...] = v_new
        d_ref[...] = -ADAM_LR * (m_hat / (jnp.sqrt(v_hat) + ADAM_EPS) + ADAM_WD * w_ref[...])

    row = pl.BlockSpec((tm, c), lambda l, i: (l * steps + i, 0))

    def slab_spec(ll):
        return pl.BlockSpec((n, tm, c), lambda l, i: (0, jnp.where(l == ll, i, 0), 0))

    return pl.pallas_call(
        kern, name=name, grid=(n_l, steps), out_shape=[_sds((n_l * r, c))] * 4,
        in_specs=[slab_spec(ll) for ll in range(n_l)] + [row, row, row],
        out_specs=[row] * 4,
        scratch_shapes=[pltpu.VMEM((tm, c), F32)],
        compiler_params=_params("arbitrary", "arbitrary"),
    )(*slabs, w, m, v)


def _perm_w_in(w):
    pad = jnp.zeros(w.shape[:-1] + (Z_W - Z_MISC - ROPE - HEADS,), w.dtype)
    return jnp.concatenate([w[..., 0:1536], w[..., 1544:2056], w[..., 2472:2984], w[..., 2056:2312],
                            w[..., 2312:2440], w[..., 2440:2472], w[..., 1536:1544], pad], axis=-1)


def _unperm_w_in(g):
    ff0 = Z_MISC + MISC_FF
    return jnp.concatenate([g[..., 0:1536], g[..., ff0:ff0 + HEADS], g[..., Z_FG:Z_FG + GROUP_W],
                            g[..., Z_QL:Z_QL + Q_LORA], g[..., Z_KV:Z_KV + KV_LORA],
                            g[..., Z_MISC:Z_MISC + ROPE], g[..., Z_MG:Z_MG + GROUP_W]], axis=-1)


def _perm_w_uq(w):
    lead = w.shape[:-1]
    wh = w.reshape(lead + (PAIRS, 2, NOPE + ROPE))
    zero = jnp.zeros(lead + (PAIRS, HEAD_DIM - ROPE), w.dtype)

    def pair_layout(r):
        return jnp.concatenate([r[..., 1, :], zero, r[..., 0, :], zero], axis=-1).reshape(lead + (GROUP_W,))

    rope = wh[..., NOPE:]
    swapped = jnp.concatenate([rope[..., HALF_ROPE:], rope[..., :HALF_ROPE]], axis=-1)
    return jnp.concatenate([wh[..., :NOPE].reshape(lead + (GROUP_W,)), pair_layout(rope), pair_layout(swapped)],
                           axis=-1)


def _unperm_w_uq(g):
    lead = g.shape[:-1]
    nope = g[..., :GROUP_W].reshape(lead + (PAIRS, 2, NOPE))

    def heads(block):
        return block.reshape(lead + (PAIRS, 2, HEAD_DIM))[..., ::-1, :ROPE]

    rope, swapped = heads(g[..., GROUP_W:2 * GROUP_W]), heads(g[..., 2 * GROUP_W:])
    rope = rope + jnp.concatenate([swapped[..., HALF_ROPE:], swapped[..., :HALF_ROPE]], axis=-1)
    return jnp.concatenate([nope, rope], axis=-1).reshape(lead + (HEADS * (NOPE + ROPE),))


def _perm_w_ukv(w):
    lead = w.shape[:-1]
    wh = w.reshape(lead + (HEADS, 2 * HEAD_DIM))
    return jnp.concatenate([wh[..., :NOPE].reshape(lead + (GROUP_W,)),
                            wh[..., NOPE:].reshape(lead + (GROUP_W,))], axis=-1)


def _unperm_w_ukv(g):
    lead = g.shape[:-1]
    parts = [g[..., :GROUP_W].reshape(lead + (HEADS, NOPE)), g[..., GROUP_W:].reshape(lead + (HEADS, HEAD_DIM))]
    return jnp.concatenate(parts, axis=-1).reshape(lead + (2 * GROUP_W,))


def _rope_tables(positions):
    inv_freq = 1.0 / (ROPE_THETA ** (jnp.arange(0, ROPE, 2, dtype=F32) / ROPE))
    ang = positions.astype(F32)[:, None] * inv_freq
    cos, sin = jnp.cos(ang), jnp.sin(ang)
    reps = LANES // ROPE
    return jnp.tile(jnp.concatenate([cos, cos], axis=1), (1, reps)), jnp.tile(jnp.concatenate([-sin, sin], axis=1), (1, reps))


def _full_weights(g_in, g_uq, g_ukv, g_out):
    def cols(g):
        return g.transpose(1, 0, 2).reshape(g.shape[1], -1)
    return (_perm_w_in(cols(g_in)), _perm_w_uq(cols(g_uq)), _perm_w_ukv(cols(g_ukv)),
            g_out.reshape(-1, g_out.shape[2]))


def _grad_slabs(dw_in, dw_uq, dw_ukv):
    def cols(g):
        return g.reshape(g.shape[0], N_DEV, -1).transpose(1, 0, 2)
    return [cols(_unperm_w_in(dw_in)), cols(_unperm_w_uq(dw_uq)), cols(_unperm_w_ukv(dw_ukv))]


def _local_step(x, mod, positions, loss_target, norm_g, b_f, q_norm_g, kv_norm_g, final_g, weights, shards=None):
    n_l = norm_g.shape[0]
    s_len, d = x.shape
    cos, sin = _rope_tables(positions)
    qb, kb, vb = Z_FQ // LANES, Z_FK // LANES, Z_FV // LANES
    chunks = s_len // LANES
    weights = list(weights)

    def pack_rows(a_rows, delta):
        return jnp.concatenate([a_rows.T, delta[:, :HEADS], jnp.zeros((s_len, LANES - 2 * HEADS), F32)], axis=1)

    saved = []
    for l in range(n_l):
        w_in, w_uq, w_ukv, w_out = weights[l]
        mod3 = mod[l].reshape(3, d)
        h, qkv, gates, tail = _k_in(x, norm_g[l][None], mod3, w_in)
        fft = tail[:, T_MISC + MISC_FF:T_MISC + MISC_FF + HEADS].T.reshape(HEADS * chunks, LANES)
        bf = jnp.repeat(b_f[l], chunks)[:, None]
        c2 = _k_cum(fft, bf, chunks).reshape(HEADS, s_len) * LOG2E
        side = (list(shards[l + 1]), [True] * 4) if shards is not None and l + 1 < n_l else None
        ck_lanes = jnp.pad(c2.T, ((0, 0), (0, LANES - HEADS)))
        of, lse_f, *gathered = _attention_fwd(qkv, qb, qkv, kb, qkv, vb, ck_lanes, None, False,
                                              "fox_fwd_gather" if side else "fox_fwd", side)
        if side:
            weights.append(_full_weights(*gathered))
        mq, mqr, mk, mv, kr2, qn, kvn = _k_prep(tail, cos, sin, q_norm_g[l][None], kv_norm_g[l][None], w_uq, w_ukv)
        om, lse_m = _attention_fwd(mq, 0, mk, 0, mv, 0, None, (mqr, kr2), True, "mla_fwd")
        x_new, y, u = _k_out(of, om, gates, x, mod3[2:3], w_out)
        saved.append((x, gates, tail, h, qkv, fft, bf, c2, lse_f, mq, mqr, mk, mv, kr2, lse_m, of, om, qn, kvn, y, u,
                      mod3))
        x = x_new

    loss_row, dx, dfinal = _k_loss(x, final_g[None], loss_target)

    grads = {k: [] for k in ("norm_g", "mod", "w_in", "b_f", "q_norm_g", "w_uq", "kv_norm_g", "w_ukv", "w_out")}
    received, pending = {}, None
    wg_dtype = MXU if shards is not None else F32
    for l in range(n_l - 1, -1, -1):
        (x_l, gates, tail, h, qkv, fft, bf, c2, lse_f, mq, mqr, mk, mv, kr2, lse_m, of, om, qn, kvn, y, u,
         mod3) = saved[l]
        w_in, w_uq, w_ukv, w_out = weights[l]
        dof, dom, dfg, dmg, dlt_f, dlt_m, dgate, dw_out = _kb_out(dx, y, u, mod3[2:3], w_out, of, om, gates, wg_dtype)

        side = None
        if shards is not None:
            side_arrs = (pending or []) + [dw_out.reshape(N_DEV, -1, dw_out.shape[1])]
            side = (side_arrs, [False] * len(side_arrs))
        dfq, dfk, dfv, dcq, dck, *arrived = _attention_bwd(
            qkv, qb, qkv, kb, qkv, vb, dof, pack_rows(-lse_f.reshape(HEADS, s_len), dlt_f), c2[:, None, :], None,
            False, FOX_SCALE, 1.0 / LOG2E, "fox_bwd_exchange" if pending else "fox_bwd", side)
        if side:
            received[l] = [None, None, None, arrived[-1]]
            if pending:
                received[l + 1][:3] = arrived[:3]
        dcum = (dcq[:, :HEADS].T + dck.reshape(HEADS, s_len)).reshape(HEADS * chunks, LANES)
        dff_rows, dbf_rows = _k_cum_bwd(dcum, fft, bf, chunks)
        dfft = dff_rows.reshape(HEADS, s_len)
        grads["b_f"].append(jnp.sum(dbf_rows[:, 0].reshape(HEADS, chunks), axis=1))

        dmq, dkn, dmv, dqr, dkr_pairs = _attention_bwd(
            mq, 0, mk, 0, mv, 0, dom, pack_rows(-lse_m.reshape(HEADS, s_len), dlt_m), None, (mqr, kr2), True,
            MLA_SCALE, 1.0 / LOG2E, "mla_bwd")
        dkr = dkr_pairs[0] + dkr_pairs[1] + dkr_pairs[2] + dkr_pairs[3]
        dff = jnp.pad(dfft.T, ((0, 0), (MISC_FF, LANES - MISC_FF - HEADS)))
        dz_tail, dgq, dgkv, dw_uq, dw_ukv = _kb_prep(dmq, dqr, dkn, dmv, dkr, dff, tail, qn, kvn, cos, sin,
                                                     q_norm_g[l][None], kv_norm_g[l][None], w_uq, w_ukv, wg_dtype)
        grads["q_norm_g"].append(dgq[0])
        grads["kv_norm_g"].append(dgkv[0])
        dz = [dfq, dfk, dfv, dfg, dmg, dz_tail]
        dx, acc3, dw_in = _kb_in(dz, w_in, h, x_l, norm_g[l][None], mod3, dx, wg_dtype)
        grads["norm_g"].append(acc3[2])
        grads["mod"].append(jnp.concatenate([acc3[0], acc3[1], dgate[0]]))
        if shards is not None:
            pending = _grad_slabs(dw_in, dw_uq, dw_ukv)
        else:
            for name, g in (("w_in", dw_in), ("w_uq", dw_uq), ("w_ukv", dw_ukv), ("w_out", dw_out)):
                grads[name].append(g)
    grads = {k: jnp.stack(v[::-1]) for k, v in grads.items() if v}
    grads["final_g"] = dfinal[0]
    if shards is None:
        return loss_row[0, 0], dx, grads
    return loss_row[0, 0], dx, grads, received, pending


def _pack_small(parts, total):
    flat = jnp.concatenate([p.reshape(-1) for p in parts])
    return jnp.pad(flat, (0, total - flat.shape[0])).reshape(total // LANES, LANES)


def kernel(x, c, positions, norm_g, w_ada, b_ada, w_in, b_f, q_norm_g, w_uq, kv_norm_g, w_ukv, w_out, final_g, loss_target, m_norm_g, m_w_ada, m_b_ada, m_w_in, m_b_f, m_q_norm_g, m_w_uq, m_kv_norm_g, m_w_ukv, m_w_out, m_final_g, v_norm_g, v_w_ada, v_b_ada, v_w_in, v_b_f, v_q_norm_g, v_w_uq, v_kv_norm_g, v_w_ukv, v_w_out, v_final_g):
    n_l, d = norm_g.shape
    me = 4 * lax.axis_index("x") + 2 * lax.axis_index("y") + lax.axis_index("c")
    ada_c = w_ada.shape[2]

    cact = jnp.broadcast_to(jax.nn.silu(c), (N_DEV, d))
    shards = [[w[l].astype(MXU) for w in (w_in, w_uq, w_ukv, w_out)] for l in range(n_l)]
    *g_w0, g_cact = _gather_two_level(shards[0] + [cact], "gather_layer0")
    cact_all = g_cact[:, 0, :]

    b_cols = lax.dynamic_slice_in_dim(b_ada, me * ada_c, ada_c, axis=1)[:, None, :]
    modpart = _modpart(cact_all, w_ada, b_cols)
    mod_send = jnp.pad(modpart.transpose(1, 0, 2), ((0, 0), (0, 8 - n_l), (0, 0)))
    (mod_recv,) = _exchange([mod_send], [False], "scatter_mod")
    mod = mod_recv.transpose(1, 0, 2).reshape(8, N_DEV * ada_c)[:n_l]

    loss, dx, gr, received, pending = _local_step(x[0], mod, positions[0], loss_target[0], norm_g, b_f, q_norm_g,
                                                  kv_norm_g, final_g, [_full_weights(*g_w0)], shards)

    small_parts = [gr["norm_g"], gr["mod"], gr["b_f"], gr["q_norm_g"], gr["kv_norm_g"], gr["final_g"], cact[0]]
    sizes = [int(np.prod(p.shape)) for p in small_parts]
    total = -(-sum(sizes) // 1024) * 1024
    small = _pack_small(small_parts, total)
    *received[0][:3], r_small = _exchange(pending + [small], [False, False, False, True], "exchange_layer0")
    r_in, r_uq, r_ukv, r_out = ([received[l][i] for l in range(n_l)] for i in range(4))

    def upd(slabs, w, m, v, name):
        shp = w.shape
        w2, m2, v2 = (a.reshape(-1, slabs[0].shape[2]) for a in (w, m, v))
        return [o.reshape(shp) for o in _adamw(slabs, w2, m2, v2, name)]

    o_in = upd(r_in, w_in, m_w_in, v_w_in, "adamw_w_in")
    o_uq = upd(r_uq, w_uq, m_w_uq, v_w_uq, "adamw_w_uq")
    o_ukv = upd(r_ukv, w_ukv, m_w_ukv, v_w_ukv, "adamw_w_ukv")
    o_out = upd(r_out, w_out, m_w_out, v_w_out, "adamw_w_out")

    offs = np.cumsum([0] + sizes)
    flat_all = r_small.reshape(N_DEV, total)
    dmod_all = flat_all[:, offs[1]:offs[2]].reshape(N_DEV, n_l, 3 * d)
    dmod_cols = lax.dynamic_slice_in_dim(dmod_all, me * ada_c, ada_c, axis=2).transpose(1, 0, 2)
    cact_cols = flat_all[:, offs[6]:offs[7]][:, :, None]
    g_ada = _ada_grad(cact_cols, dmod_cols)
    o_ada = upd([g_ada.reshape(1, n_l * d, ada_c)], w_ada, m_w_ada, v_w_ada, "adamw_w_ada")

    zero_c = jnp.zeros((d,), F32)
    small_w = [_pack_small([norm_g, b_ada, b_f, q_norm_g, kv_norm_g, final_g, zero_c], total),
               _pack_small([m_norm_g, m_b_ada, m_b_f, m_q_norm_g, m_kv_norm_g, m_final_g, zero_c], total),
               _pack_small([v_norm_g, v_b_ada, v_b_f, v_q_norm_g, v_kv_norm_g, v_final_g, zero_c], total)]
    o_small = [o.reshape(-1) for o in _adamw([r_small], *small_w, "adamw_small")]
    shapes = [norm_g.shape, b_ada.shape, b_f.shape, q_norm_g.shape, kv_norm_g.shape, final_g.shape]

    def small_out(kind, idx):
        return o_small[kind][offs[idx]:offs[idx + 1]].reshape(shapes[idx])

    loss_all = lax.psum(loss, ("x", "y", "c"))
    outs = [loss_all, dx[None]]
    for kind in range(4):
        outs += [small_out(kind, 0), o_ada[kind], small_out(kind, 1), o_in[kind], small_out(kind, 2),
                 small_out(kind, 3), o_uq[kind], small_out(kind, 4), o_ukv[kind], o_out[kind], small_out(kind, 5)]
    return tuple(outs)
```

```python
import jax
import jax.numpy as jnp
import numpy as np
from jax import lax
from jax.experimental import pallas as pl
from jax.experimental.pallas import tpu as pltpu

F32 = jnp.float32
MXU = jnp.bfloat16

N_DEV = 8
HEADS = 8
PAIRS = HEADS // 2
HEAD_DIM = 64
NOPE = 64
ROPE = 32
HALF_ROPE = ROPE // 2
Q_LORA = 256
KV_LORA = 128
CHUNK = 64
GROUP_W = HEADS * HEAD_DIM
EPS = 1e-6
ROPE_THETA = 10000.0

Z_FQ, Z_FK, Z_FV, Z_FG, Z_MG, Z_QL, Z_KV, Z_MISC, Z_W = 0, 512, 1024, 1536, 2048, 2560, 2816, 2944, 3072
MISC_FF = ROPE
TAIL_W = Z_W - Z_QL
T_KV, T_MISC = Q_LORA, Q_LORA + KV_LORA

ADAM_LR = 0.001
ADAM_B1 = 0.9
ADAM_B2 = 0.999
ADAM_EPS = 1e-08
ADAM_WD = 0.01
ADAM_STEP = 10

VMEM_LIMIT_V7X = 56 * 1024 * 1024
LANES = 128
ATTN_TILE = 1024
V_ROWS = HEAD_DIM + 16
LOG2E = 1.4426950408889634
FOX_SCALE = HEAD_DIM ** -0.5
MLA_SCALE = (NOPE + ROPE) ** -0.5

_NT = (((1,), (1,)), ((), ()))
_TN = (((0,), (0,)), ((), ()))


def _params(*sem, side_effects=False):
    return pltpu.CompilerParams(dimension_semantics=sem, vmem_limit_bytes=VMEM_LIMIT_V7X,
                                has_side_effects=side_effects)


def _sds(shape, dtype=F32):
    return jax.ShapeDtypeStruct(shape, dtype)


def _full(shape):
    nd = len(shape)
    return pl.BlockSpec(shape, lambda *_: (0,) * nd)


def _rows(tm, width, col=0):
    return pl.BlockSpec((tm, width), lambda i: (i, col))


def _exchange(arrs, gather, name):
    n = len(arrs)

    def kern(*refs):
        copies = _exchange_copies(refs[:n], refs[n:2 * n], gather, *refs[2 * n:])
        _exchange_start(copies)
        _exchange_wait(copies)

    return pl.pallas_call(
        kern, name=name, out_shape=_exchange_out_shapes(arrs, gather),
        in_specs=[pl.BlockSpec(memory_space=pl.ANY)] * n,
        out_specs=[pl.BlockSpec(memory_space=pl.ANY)] * n,
        scratch_shapes=_exchange_sems(n),
        compiler_params=pltpu.CompilerParams(has_side_effects=True),
    )(*arrs)


def _gather_two_level(arrs, name):
    n = len(arrs)

    def kern(*refs):
        ins, outs = refs[:n], refs[n:2 * n]
        send_sems, recv_sems, loc_sems = refs[2 * n:]
        x, y, c = lax.axis_index("x"), lax.axis_index("y"), lax.axis_index("c")
        me, sibling = (x, y, c), (x, y, 1 - c)
        chips = [(1 - x, y), (x, 1 - y), (1 - x, 1 - y)]

        def slot(i, dev):
            return outs[i].at[4 * dev[0] + 2 * dev[1] + dev[2]]

        def copy(i, k, block, to, src=None):
            return pltpu.make_async_remote_copy(
                src_ref=slot(i, block) if src is None else src, dst_ref=slot(i, block), send_sem=send_sems.at[i, k],
                recv_sem=recv_sems.at[i, k], device_id=to, device_id_type=pl.DeviceIdType.MESH)

        mine = [pltpu.make_async_copy(ins[i], slot(i, me), loc_sems.at[i]) for i in range(n)]
        first = [copy(i, 0, me, sibling, src=ins[i]) for i in range(n)]
        first += [copy(i, 1 + j, me, (*chip, c), src=ins[i]) for j, chip in enumerate(chips) for i in range(n)]
        for cp in mine + first:
            cp.start()
        passed = []
        for j, chip in enumerate(chips):
            for i in range(n):
                copy(i, 1 + j, (*chip, c), me).wait_recv()
                fwd = copy(i, 4 + j, (*chip, c), sibling)
                fwd.start()
                passed.append(fwd)
        for i in range(n):
            copy(i, 0, sibling, me).wait_recv()
            for j, chip in enumerate(chips):
                copy(i, 4 + j, (*chip, 1 - c), me).wait_recv()
        for cp in first + passed:
            cp.wait_send()
        for cp in mine:
            cp.wait()

    return pl.pallas_call(
        kern, name=name, out_shape=_exchange_out_shapes(arrs, [True] * n),
        in_specs=[pl.BlockSpec(memory_space=pl.ANY)] * n,
        out_specs=[pl.BlockSpec(memory_space=pl.ANY)] * n,
        scratch_shapes=_exchange_sems(n),
        compiler_params=pltpu.CompilerParams(has_side_effects=True),
    )(*arrs)


def _exchange_out_shapes(arrs, gather):
    return [_sds((N_DEV,) + tuple(a.shape) if g else tuple(a.shape), a.dtype) for a, g in zip(arrs, gather)]


def _exchange_sems(n):
    return [pltpu.SemaphoreType.DMA((n, N_DEV)), pltpu.SemaphoreType.DMA((n, N_DEV)), pltpu.SemaphoreType.DMA((n,))]


def _exchange_copies(ins, outs, gather, send_sems, recv_sems, loc_sems, recv=True):
    n = len(ins)
    x, y, c = lax.axis_index("x"), lax.axis_index("y"), lax.axis_index("c")
    me = 4 * x + 2 * y + c

    def src(i, j):
        return ins[i] if gather[i] else ins[i].at[j]

    local = [pltpu.make_async_copy(src(i, me), outs[i].at[me], loc_sems.at[i]) for i in range(n)]
    sends, recvs = [], []
    for k in range(1, N_DEV):
        px = 1 - x if k & 4 else x
        py = 1 - y if k & 2 else y
        pc = 1 - c if k & 1 else c
        p = 4 * px + 2 * py + pc
        for i in range(n):
            sends.append(pltpu.make_async_remote_copy(
                src_ref=src(i, p), dst_ref=outs[i].at[me], send_sem=send_sems.at[i, k],
                recv_sem=recv_sems.at[i, k], device_id=(px, py, pc), device_id_type=pl.DeviceIdType.MESH))
            if recv:
                recvs.append(pltpu.make_async_remote_copy(
                    src_ref=src(i, p), dst_ref=outs[i].at[p], send_sem=send_sems.at[i, k],
                    recv_sem=recv_sems.at[i, k], device_id=(px, py, pc), device_id_type=pl.DeviceIdType.MESH))
    return local, sends, recvs


def _exchange_start(copies):
    local, sends, _ = copies
    for cp in local + sends:
        cp.start()


def _exchange_wait(copies):
    local, sends, recvs = copies
    for cp in recvs:
        cp.wait_recv()
    for cp in sends:
        cp.wait_send()
    for cp in local:
        cp.wait()


def _modpart(cact8, w_ada, b_cols):
    n_l, d, cw = w_ada.shape

    def kern(c_ref, w_ref, b_ref, o_ref):
        o_ref[0] = jnp.dot(c_ref[...].astype(MXU), w_ref[0].astype(MXU), preferred_element_type=F32) + b_ref[0]

    return pl.pallas_call(
        kern, name="modpart", grid=(n_l,), out_shape=_sds((n_l, N_DEV, cw)),
        in_specs=[_full((N_DEV, d)), pl.BlockSpec((1, d, cw), lambda l: (l, 0, 0)),
                  pl.BlockSpec((1, 1, cw), lambda l: (l, 0, 0))],
        out_specs=pl.BlockSpec((1, N_DEV, cw), lambda l: (l, 0, 0)),
        compiler_params=_params("arbitrary"),
    )(cact8, w_ada, b_cols)


def _ada_grad(cact_cols, dmod_cols):
    n_l, _, cw = dmod_cols.shape
    d = cact_cols.shape[1]

    def kern(c_ref, dm_ref, o_ref):
        acc = c_ref[0] * dm_ref[0, 0:1, :]
        for s in range(1, N_DEV):
            acc = acc + c_ref[s] * dm_ref[0, s:s + 1, :]
        o_ref[0] = acc

    return pl.pallas_call(
        kern, name="ada_grad", grid=(n_l,), out_shape=_sds((n_l, d, cw)),
        in_specs=[_full((N_DEV, d, 1)), pl.BlockSpec((1, N_DEV, cw), lambda l: (l, 0, 0))],
        out_specs=pl.BlockSpec((1, d, cw), lambda l: (l, 0, 0)),
        compiler_params=_params("arbitrary"),
    )(cact_cols, dmod_cols)


def _k_in(x, g, mod3, w, tm=256):
    s_len, d = x.shape
    qkv_w = 3 * GROUP_W

    def kern(x_ref, g_ref, mod_ref, w_ref, h_ref, qkv_ref, gates_ref, tail_ref):
        xv = x_ref[...]
        r = lax.rsqrt(jnp.mean(xv * xv, axis=-1, keepdims=True) + EPS)
        xn = xv * r * g_ref[...]
        h = (xn * (1.0 + mod_ref[1:2, :]) + mod_ref[0:1, :]).astype(MXU)
        h_ref[...] = h
        z = jnp.dot(h, w_ref[...], preferred_element_type=F32)
        qkv_ref[:, :GROUP_W] = (z[:, Z_FQ:Z_FQ + GROUP_W] * (FOX_SCALE * LOG2E)).astype(MXU)
        qkv_ref[:, GROUP_W:] = z[:, Z_FK:Z_FK + 2 * GROUP_W].astype(MXU)
        gates_ref[...] = z[:, Z_FG:Z_QL]
        tail_ref[...] = z[:, Z_QL:]

    return pl.pallas_call(
        kern, name="k_in", grid=(s_len // tm,),
        out_shape=[_sds((s_len, d), MXU), _sds((s_len, qkv_w), MXU), _sds((s_len, Z_QL - Z_FG)),
                   _sds((s_len, TAIL_W))],
        in_specs=[_rows(tm, d), _full((1, d)), _full((3, d)), _full((d, Z_W))],
        out_specs=[_rows(tm, d), _rows(tm, qkv_w), _rows(tm, Z_QL - Z_FG), _rows(tm, TAIL_W)],
        compiler_params=_params("arbitrary"),
    )(x, g, mod3, w)


def _scan_matrices(rows, chunks, reverse):
    r_i = lax.broadcasted_iota(jnp.int32, (LANES, LANES), 0)
    c_i = lax.broadcasted_iota(jnp.int32, (LANES, LANES), 1)
    a_i = lax.broadcasted_iota(jnp.int32, (rows, rows), 0)
    b_i = lax.broadcasted_iota(jnp.int32, (rows, rows), 1)
    same_head = (a_i // chunks) == (b_i // chunks)
    if reverse:
        return (r_i >= c_i).astype(F32), (same_head & (b_i > a_i)).astype(F32)
    return (r_i <= c_i).astype(F32), (same_head & (b_i < a_i)).astype(F32)


def _scan_rows(x, inner, outer):
    tot = jnp.broadcast_to(jnp.sum(x, axis=1, keepdims=True), x.shape)
    return (jnp.dot(x, inner, precision=lax.Precision.HIGHEST, preferred_element_type=F32)
            + jnp.dot(outer, tot, precision=lax.Precision.HIGHEST, preferred_element_type=F32))


def _k_cum(ff_rows, b_rows, chunks):
    rows = ff_rows.shape[0]

    def kern(ff_ref, b_ref, cum_ref):
        xc = ff_ref[...] + b_ref[...]
        lf = jnp.minimum(xc, 0.0) - jnp.log(1.0 + jnp.exp(-jnp.abs(xc)))
        cum_ref[...] = _scan_rows(lf, *_scan_matrices(rows, chunks, False))

    return pl.pallas_call(
        kern, name="k_cum", out_shape=_sds((rows, LANES)),
        in_specs=[pl.BlockSpec(memory_space=pltpu.VMEM)] * 2,
        out_specs=pl.BlockSpec(memory_space=pltpu.VMEM),
        compiler_params=_params(),
    )(ff_rows, b_rows)


def _k_cum_bwd(dc_rows, ff_rows, b_rows, chunks):
    rows = ff_rows.shape[0]

    def kern(dc_ref, ff_ref, b_ref, dff_ref, db_ref):
        dlf = _scan_rows(dc_ref[...], *_scan_matrices(rows, chunks, True))
        dff = dlf * jax.nn.sigmoid(-(ff_ref[...] + b_ref[...]))
        dff_ref[...] = dff
        db_ref[...] = jnp.broadcast_to(jnp.sum(dff, axis=1, keepdims=True), dff.shape)

    return pl.pallas_call(
        kern, name="k_cum_bwd", out_shape=[_sds((rows, LANES)), _sds((rows, LANES))],
        in_specs=[pl.BlockSpec(memory_space=pltpu.VMEM)] * 3,
        out_specs=[pl.BlockSpec(memory_space=pltpu.VMEM)] * 2,
        compiler_params=_params(),
    )(dc_rows, ff_rows, b_rows)


def _swap16(t):
    lane = lax.broadcasted_iota(jnp.int32, t.shape, 1)
    return jnp.where(lane % ROPE < HALF_ROPE, pltpu.roll(t, LANES - HALF_ROPE, 1), pltpu.roll(t, HALF_ROPE, 1))


def _rope(t, cos, sin):
    return t * cos + _swap16(t) * sin


def _rope_bwd(dt, cos, sin):
    return dt * cos - _swap16(dt) * sin


def _k_prep(tail, cos, sin, gq, gkv, wuq, wukv, tm=512):
    s_len = tail.shape[0]
    qc = MLA_SCALE * LOG2E

    def kern(tail_ref, cos_ref, sin_ref, gq_ref, gkv_ref, wuq_ref, wukv_ref,
             qn_out, qr_out, kn_out, v_out, kr_out, qn_ref, kvn_ref):
        cs, sn = cos_ref[...], sin_ref[...]
        ql = tail_ref[:, :T_KV]
        rq = lax.rsqrt(jnp.mean(ql * ql, axis=-1, keepdims=True) + EPS)
        qn = (ql * rq * gq_ref[...]).astype(MXU)
        qn_ref[...] = qn
        q = jnp.dot(qn, wuq_ref[...], preferred_element_type=F32)
        qn_out[...] = (q[:, :GROUP_W] * qc).astype(MXU)
        for blk in range(PAIRS):
            lo = GROUP_W + blk * LANES
            qr_out[:, blk * LANES:(blk + 1) * LANES] = (_rope(q[:, lo:lo + LANES], cs, sn) * qc).astype(MXU)
        kvl = tail_ref[:, T_KV:T_MISC]
        rk = lax.rsqrt(jnp.mean(kvl * kvl, axis=-1, keepdims=True) + EPS)
        kvn = (kvl * rk * gkv_ref[...]).astype(MXU)
        kvn_ref[...] = kvn
        kv = jnp.dot(kvn, wukv_ref[...], preferred_element_type=F32)
        kn_out[...] = kv[:, :GROUP_W].astype(MXU)
        v_out[...] = kv[:, GROUP_W:].astype(MXU)
        misc = tail_ref[:, T_MISC:]
        lane = lax.broadcasted_iota(jnp.int32, misc.shape, 1)
        kr = jnp.where(lane < ROPE, _rope(misc, cs, sn), 0.0)
        kr_out[...] = (kr + pltpu.roll(kr, HEAD_DIM, 1)).astype(MXU)

    return pl.pallas_call(
        kern, name="k_prep", grid=(s_len // tm,),
        out_shape=[_sds((s_len, GROUP_W), MXU), _sds((s_len, GROUP_W), MXU), _sds((s_len, GROUP_W), MXU),
                   _sds((s_len, GROUP_W), MXU), _sds((s_len, LANES), MXU), _sds((s_len, Q_LORA), MXU),
                   _sds((s_len, KV_LORA), MXU)],
        in_specs=[_rows(tm, TAIL_W), _rows(tm, LANES), _rows(tm, LANES),
                  _full((1, Q_LORA)), _full((1, KV_LORA)), _full((Q_LORA, 2 * GROUP_W)),
                  _full((KV_LORA, 2 * GROUP_W))],
        out_specs=[_rows(tm, GROUP_W), _rows(tm, GROUP_W), _rows(tm, GROUP_W), _rows(tm, GROUP_W), _rows(tm, LANES),
                   _rows(tm, Q_LORA), _rows(tm, KV_LORA)],
        compiler_params=_params("arbitrary"),
    )(tail, cos, sin, gq, gkv, wuq, wukv)


def _block_mask(kn, qn, q_off, chunk_mask, transposed):
    shape = (kn, qn) if transposed else (qn, kn)
    row = lax.broadcasted_iota(jnp.int32, shape, 0)
    col = lax.broadcasted_iota(jnp.int32, shape, 1)
    qi, ki = (col + q_off, row) if transposed else (row + q_off, col)
    if chunk_mask:
        return (ki // CHUNK) <= (qi // CHUNK)
    return ki <= qi


def _head_operand(x, hh, other=None):
    lane = lax.broadcasted_iota(jnp.int32, x.shape, 1)
    own = (lane >= hh * HEAD_DIM) & (lane < (hh + 1) * HEAD_DIM)
    return jnp.where(own, x, jnp.zeros_like(x) if other is None else other)


def _attention_fwd(q, q_blk, k, k_blk, v, v_blk, bias, rope, chunk_mask, name, side=None):
    s_len = q.shape[0]
    t = min(ATTN_TILE, s_len // 2)
    nq = s_len // t
    n_side = len(side[0]) if side else 0

    def kern(*refs):
        q_ref, k_ref, v_ref = refs[:3]
        pos = 3
        if bias is not None:
            ck_ref = refs[pos]
            pos += 1
        if rope is not None:
            qr_ref, kr_ref = refs[pos:pos + 2]
            pos += 2
        side_in = refs[pos:pos + n_side]
        pos += n_side
        o_ref, lse_ref = refs[pos:pos + 2]
        side_out = refs[pos + 2:pos + 2 + n_side]
        vt_scr, m_scr, acc_scr, ck_scr = refs[pos + 2 + n_side:pos + 6 + n_side]
        sems = refs[pos + 6 + n_side:]
        pj = pl.program_id(0)
        if n_side:
            @pl.when(pj == 0)
            def _():
                _exchange_start(_exchange_copies(side_in, side_out, side[1], *sems, recv=False))
        vt_scr[:, HEAD_DIM:, :] = jnp.ones((2, V_ROWS - HEAD_DIM, s_len), vt_scr.dtype)
        for i in range(nq):
            vtt = v_ref[i * t:(i + 1) * t, :].T
            for hh in range(2):
                vt_scr[hh, :HEAD_DIM, i * t:(i + 1) * t] = vtt[hh * HEAD_DIM:(hh + 1) * HEAD_DIM, :]
                if bias is not None:
                    ckt = ck_ref[i * t:(i + 1) * t, :]
                    lane = lax.broadcasted_iota(jnp.int32, ckt.shape, 1)
                    ck_scr[hh, i * t:(i + 1) * t, :] = jnp.sum(jnp.where(lane == 2 * pj + hh, ckt, 0.0), axis=1,
                                                               keepdims=True)

        def qbody(qi, _):
            qs = pl.multiple_of(qi * t, t)
            qt = q_ref[pl.ds(qs, t), :]
            qrt = qr_ref[pl.ds(qs, t), :] if rope is not None else None
            qh = [_head_operand(qt, hh, qrt) for hh in range(2)]
            m_scr[...] = jnp.full(m_scr.shape, -jnp.inf, F32)
            acc_scr[...] = jnp.zeros(acc_scr.shape, F32)

            def block(ks, kn, q0, qn, masked):
                kt = k_ref[pl.ds(ks, kn), :]
                kh = [_head_operand(kt, hh, kr_ref[pl.ds(ks, kn), :]) for hh in range(2)] if rope is not None else [kt, kt]
                qc = slice(q0, q0 + qn)
                sts = [lax.dot_general(kh[hh], qh[hh][qc], _NT, preferred_element_type=F32) for hh in range(2)]
                if bias is not None:
                    sts = [sts[hh] - ck_scr[hh, pl.ds(ks, kn), :] for hh in range(2)]
                if masked:
                    sts = [jnp.where(_block_mask(kn, qn, q0, chunk_mask, True), st, -jnp.inf) for st in sts]
                m_old = [m_scr[hh, :, qc] for hh in range(2)]
                m_new = [jnp.maximum(m_old[hh], jnp.max(sts[hh], axis=0, keepdims=True)) for hh in range(2)]
                pts = [jnp.exp2(sts[hh] - m_new[hh]).astype(MXU) for hh in range(2)]
                for hh in range(2):
                    alpha = jnp.exp2(m_old[hh] - m_new[hh])
                    acc_scr[hh, :, qc] = alpha * acc_scr[hh, :, qc] + jnp.dot(vt_scr[hh, :, pl.ds(ks, kn)], pts[hh],
                                                                            preferred_element_type=F32)
                    m_scr[hh, :, qc] = m_new[hh]

            def loop_body(ki, carry):
                block(pl.multiple_of(ki * t, t), t, 0, t, False)
                return carry

            lax.fori_loop(0, qi, loop_body, 0)
            block(qs, t, 0, t, True)
            outs = []
            for hh in range(2):
                acc = acc_scr[hh]
                l = acc[HEAD_DIM:HEAD_DIM + 1, :]
                outs.append(acc[:HEAD_DIM, :] / l)
                lse_ref[hh, :, pl.ds(qs, t)] = m_scr[hh] + jnp.log2(l)
            o_ref[pl.ds(qs, t), :] = jnp.concatenate(outs, axis=0).T
            return 0

        lax.fori_loop(0, nq, qbody, 0)
        if n_side:
            @pl.when(pj == PAIRS - 1)
            def _():
                _exchange_wait(_exchange_copies(side_in, side_out, side[1], *sems))

    def tok(blk):
        return pl.BlockSpec((s_len, LANES), lambda j: (0, blk + j))

    rowb = pl.BlockSpec((2, 1, s_len), lambda j: (j, 0, 0))
    hbm = pl.BlockSpec(memory_space=pl.ANY)
    ins = [q, k, v]
    in_specs = [tok(q_blk), tok(k_blk), tok(v_blk)]
    if bias is not None:
        ins.append(bias)
        in_specs.append(_full((s_len, LANES)))
    if rope is not None:
        ins += list(rope)
        in_specs += [tok(0), _full((s_len, LANES))]
    out_shape = [_sds((s_len, PAIRS * LANES)), _sds((HEADS, 1, s_len))]
    scratch = [pltpu.VMEM((2, V_ROWS, s_len), v.dtype), pltpu.VMEM((2, 1, t), F32), pltpu.VMEM((2, V_ROWS, t), F32),
               pltpu.VMEM((2, s_len if bias is not None else 8, 1), F32)]
    if n_side:
        ins += list(side[0])
        out_shape += _exchange_out_shapes(*side)
        scratch += _exchange_sems(n_side)
    return pl.pallas_call(
        kern, name=name, grid=(PAIRS,), out_shape=out_shape,
        in_specs=in_specs + [hbm] * n_side, out_specs=[tok(0), rowb] + [hbm] * n_side,
        scratch_shapes=scratch,
        compiler_params=_params("arbitrary", side_effects=bool(n_side)),
    )(*ins)


def _attention_bwd(q, q_blk, k, k_blk, v, v_blk, do, pack, ck_row, rope, chunk_mask, q_scale, k_scale, name,
                   side=None):
    s_len = q.shape[0]
    t = min(ATTN_TILE, s_len // 2)
    nq = s_len // t
    has_bias = ck_row is not None
    nv = 2 if rope is not None else 1
    n_side = len(side[0]) if side else 0

    def kern(*refs):
        q_ref, k_ref, v_ref, do_ref, pack_ref = refs[:5]
        pos = 5
        if has_bias:
            ck_ref = refs[pos]
            pos += 1
        if rope is not None:
            qr_ref, kr_ref = refs[pos:pos + 2]
            pos += 2
        side_in = refs[pos:pos + n_side]
        pos += n_side
        dq_ref, dk_ref, dv_ref = refs[pos:pos + 3]
        pos += 3
        if has_bias:
            dcq_ref, dck_ref = refs[pos:pos + 2]
            pos += 2
        if rope is not None:
            dqr_ref, dkr_ref = refs[pos:pos + 2]
            pos += 2
        side_out = refs[pos:pos + n_side]
        pos += n_side
        qt_scr, dot_scr, dkt_scr, dvt_scr, dq_scr, dcq_scr = refs[pos:pos + 6]
        sems = refs[pos + 6:]
        pj = pl.program_id(0)
        if n_side:
            @pl.when(pj == 0)
            def _():
                _exchange_start(_exchange_copies(side_in, side_out, side[1], *sems, recv=False))

        for i in range(nq):
            sl = slice(i * t, (i + 1) * t)
            dot_scr[:, sl] = do_ref[sl, :].T
            if rope is not None:
                for hh in range(2):
                    qt_scr[hh, :, sl] = _head_operand(q_ref[sl, :], hh, qr_ref[sl, :]).T
            else:
                qt_scr[0, :, sl] = q_ref[sl, :].T
        dkt_scr[...] = jnp.zeros(dkt_scr.shape, F32)
        dvt_scr[...] = jnp.zeros(dvt_scr.shape, F32)
        if has_bias:
            dck_ref[...] = jnp.zeros(dck_ref.shape, F32)

            @pl.when(pj == 0)
            def _():
                dcq_ref[...] = jnp.zeros(dcq_ref.shape, F32)

        def qbody(qi, _):
            qs = pl.multiple_of(qi * t, t)
            qt = q_ref[pl.ds(qs, t), :]
            qrt = qr_ref[pl.ds(qs, t), :] if rope is not None else None
            dot = do_ref[pl.ds(qs, t), :]
            pk = pack_ref[pl.ds(qs, t), :]
            lane = lax.broadcasted_iota(jnp.int32, pk.shape, 1)
            qh = [_head_operand(qt, hh, qrt) for hh in range(2)]
            doh = [_head_operand(dot, hh) for hh in range(2)]
            a_col = [jnp.sum(jnp.where(lane == 2 * pj + hh, pk, 0.0), axis=1, keepdims=True) for hh in range(2)]
            d_col = [jnp.sum(jnp.where(lane == HEADS + 2 * pj + hh, pk, 0.0), axis=1, keepdims=True)
                     for hh in range(2)]
            dq_scr[...] = jnp.zeros(dq_scr.shape, F32)
            if has_bias:
                dcq_scr[...] = jnp.zeros(dcq_scr.shape, F32)

            def block(ks, kn, q0, qn, masked):
                kt = k_ref[pl.ds(ks, kn), :]
                krt = kr_ref[pl.ds(ks, kn), :] if rope is not None else None
                kh = [_head_operand(kt, hh, krt) for hh in range(2)]
                vt = v_ref[pl.ds(ks, kn), :]
                qr_ = slice(q0, q0 + qn)
                qcols = pl.ds(pl.multiple_of(qs + q0, t // 2), qn)
                ss = [lax.dot_general(qh[hh][qr_], kh[hh] if rope is not None else kt, _NT,
                                      preferred_element_type=F32) + a_col[hh][qr_] for hh in range(2)]
                if has_bias:
                    ss = [ss[hh] - ck_ref[hh, :, pl.ds(ks, kn)] for hh in range(2)]
                dpds = [lax.dot_general(doh[hh][qr_], vt, _NT, preferred_element_type=F32) for hh in range(2)]
                ps = [jnp.exp2(s) for s in ss]
                if masked:
                    ps = [jnp.where(_block_mask(kn, qn, q0, chunk_mask, False), p, 0.0) for p in ps]
                dss = [ps[hh] * (dpds[hh] - d_col[hh][qr_]) for hh in range(2)]
                for hh in range(2):
                    rows = slice(hh * HEAD_DIM, (hh + 1) * HEAD_DIM)
                    dsb = dss[hh].astype(MXU)
                    dvt_scr[rows, pl.ds(ks, kn)] += jnp.dot(dot_scr[rows, qcols], ps[hh].astype(MXU),
                                                            preferred_element_type=F32)
                    if rope is not None:
                        dkt_scr[hh, :, pl.ds(ks, kn)] += jnp.dot(qt_scr[hh, :, qcols], dsb,
                                                                 preferred_element_type=F32)
                    else:
                        dkt_scr[0, rows, pl.ds(ks, kn)] += jnp.dot(qt_scr[0, rows, qcols], dsb,
                                                                   preferred_element_type=F32)
                    dq_scr[hh if rope is not None else 0, qr_, :] += jnp.dot(dsb, kh[hh], preferred_element_type=F32)
                    if has_bias:
                        dcq_scr[hh, qr_, :] += jnp.sum(dss[hh], axis=1, keepdims=True)
                        dck_ref[hh, :, pl.ds(ks, kn)] += -jnp.sum(dss[hh], axis=0, keepdims=True)

            def loop_body(ki, carry):
                block(pl.multiple_of(ki * t, t), t, 0, t, False)
                return carry

            lax.fori_loop(0, qi, loop_body, 0)
            block(qs, t // 2, 0, t // 2, True)
            block(qs, t, t // 2, t // 2, True)
            if rope is not None:
                first = lane < HEAD_DIM
                dq_ref[pl.ds(qs, t), :] = (jnp.where(first, dq_scr[0], dq_scr[1]) * q_scale).astype(dq_ref.dtype)
                dqr_ref[pl.ds(qs, t), :] = jnp.where(first, dq_scr[1], dq_scr[0]) * q_scale
            else:
                dq_ref[pl.ds(qs, t), :] = (dq_scr[0] * q_scale).astype(dq_ref.dtype)
            if has_bias:
                old = dcq_ref[pl.ds(qs, t), :]
                dcq_ref[pl.ds(qs, t), :] = jnp.where(lane == 2 * pj, dcq_scr[0],
                                                     jnp.where(lane == 2 * pj + 1, dcq_scr[1], old))
            return 0

        lax.fori_loop(0, nq, qbody, 0)
        for i in range(nq):
            sl = slice(i * t, (i + 1) * t)
            dv_ref[sl, :] = dvt_scr[:, sl].T.astype(dv_ref.dtype)
            if rope is not None:
                d0, d1 = dkt_scr[0, :, sl], dkt_scr[1, :, sl]
                first = lax.broadcasted_iota(jnp.int32, d0.shape, 0) < HEAD_DIM
                dk_ref[sl, :] = (jnp.where(first, d0, d1).T * k_scale).astype(dk_ref.dtype)
                dkr_ref[0, sl, :] = jnp.where(first, d1, d0).T * k_scale
            else:
                dk_ref[sl, :] = (dkt_scr[0, :, sl].T * k_scale).astype(dk_ref.dtype)
        if n_side:
            @pl.when(pj == PAIRS - 1)
            def _():
                _exchange_wait(_exchange_copies(side_in, side_out, side[1], *sems))

    def tok(blk):
        return pl.BlockSpec((s_len, LANES), lambda j: (0, blk + j))

    shared = _full((s_len, LANES))
    rowb = pl.BlockSpec((2, 1, s_len), lambda j: (j, 0, 0))
    slab = pl.BlockSpec((1, s_len, LANES), lambda j: (j, 0, 0))
    hbm = pl.BlockSpec(memory_space=pl.ANY)
    ins = [q, k, v, do, pack]
    in_specs = [tok(q_blk), tok(k_blk), tok(v_blk), tok(0), shared]
    out_shape = [_sds((s_len, PAIRS * LANES), MXU)] * 3
    out_specs = [tok(0)] * 3
    if has_bias:
        ins.append(ck_row)
        in_specs.append(rowb)
        out_shape += [_sds((s_len, LANES)), _sds((HEADS, 1, s_len))]
        out_specs += [shared, rowb]
    if rope is not None:
        ins += list(rope)
        in_specs += [tok(0), shared]
        out_shape += [_sds((s_len, PAIRS * LANES)), _sds((PAIRS, s_len, LANES))]
        out_specs += [tok(0), slab]
    scratch = [pltpu.VMEM((nv, LANES, s_len), q.dtype), pltpu.VMEM((LANES, s_len), do.dtype),
               pltpu.VMEM((nv, LANES, s_len), F32), pltpu.VMEM((LANES, s_len), F32),
               pltpu.VMEM((nv, t, LANES), F32), pltpu.VMEM((2, t, 1), F32)]
    if n_side:
        ins += list(side[0])
        out_shape += _exchange_out_shapes(*side)
        scratch += _exchange_sems(n_side)
    return pl.pallas_call(
        kern, name=name, grid=(PAIRS,), out_shape=out_shape,
        in_specs=in_specs + [hbm] * n_side, out_specs=out_specs + [hbm] * n_side, scratch_shapes=scratch,
        compiler_params=_params("arbitrary", side_effects=bool(n_side)),
    )(*ins)


def _silu(a):
    return a * jax.nn.sigmoid(a)


def _k_out(of, om, gates, x, gate, wout, tm=256):
    s_len, d = x.shape

    def kern(of_ref, om_ref, gates_ref, x_ref, gate_ref, w_ref, xo_ref, y_ref, u_ref):
        u_ref[:, :GROUP_W] = (of_ref[...] * _silu(gates_ref[:, :GROUP_W])).astype(MXU)
        u_ref[:, GROUP_W:] = (om_ref[...] * _silu(gates_ref[:, GROUP_W:])).astype(MXU)
        y = jnp.dot(u_ref[...], w_ref[...], preferred_element_type=F32)
        y_ref[...] = y.astype(y_ref.dtype)
        xo_ref[...] = x_ref[...] + gate_ref[...] * y

    return pl.pallas_call(
        kern, name="k_out", grid=(s_len // tm,),
        out_shape=[_sds((s_len, d)), _sds((s_len, d), MXU), _sds((s_len, 2 * GROUP_W), MXU)],
        in_specs=[_rows(tm, GROUP_W), _rows(tm, GROUP_W), _rows(tm, 2 * GROUP_W), _rows(tm, d), _full((1, d)),
                  _full((2 * GROUP_W, d))],
        out_specs=[_rows(tm, d), _rows(tm, d), _rows(tm, 2 * GROUP_W)],
        compiler_params=_params("arbitrary"),
    )(of, om, gates, x, gate, wout)


def _k_loss(x, gf, tgt, tm=256):
    s_len, d = x.shape

    def kern(x_ref, g_ref, t_ref, loss_ref, dx_ref, dg_ref):
        i = pl.program_id(0)
        xv = x_ref[...]
        r = lax.rsqrt(jnp.mean(xv * xv, axis=-1, keepdims=True) + EPS)
        xh = xv * r
        diff = xh * g_ref[...] - t_ref[...]
        part = 0.5 * jnp.sum(jnp.mean(diff * diff, axis=-1, keepdims=True))
        dout = diff * (1.0 / d)
        dxh = dout * g_ref[...]
        dx_ref[...] = r * (dxh - xh * jnp.mean(dxh * xh, axis=-1, keepdims=True))

        @pl.when(i == 0)
        def _():
            loss_ref[...] = jnp.zeros_like(loss_ref)
            dg_ref[...] = jnp.zeros_like(dg_ref)

        loss_ref[...] += jnp.full(loss_ref.shape, part, F32)
        dg_ref[...] += jnp.sum(dout * xh, axis=0, keepdims=True)

    return pl.pallas_call(
        kern, name="k_loss", grid=(s_len // tm,),
        out_shape=[_sds((1, LANES)), _sds((s_len, d)), _sds((1, d))],
        in_specs=[_rows(tm, d), _full((1, d)), _rows(tm, d)],
        out_specs=[_full((1, LANES)), _rows(tm, d), _full((1, d))],
        compiler_params=_params("arbitrary"),
    )(x, gf, tgt)


def _kb_out(dxo, y, u, gate, wout, of, om, gates, dw_dtype, tm=512):
    s_len, d = dxo.shape
    steps = s_len // tm

    def kern(dxo_ref, y_ref, u_ref, gate_ref, wt_ref, of_ref, om_ref, gates_ref,
             dof_ref, dom_ref, dfg_ref, dmg_ref, dlf_ref, dlm_ref, dgate_ref, dw_ref, dw_acc):
        i = pl.program_id(0)
        dxv = dxo_ref[...]

        @pl.when(i == 0)
        def _():
            dgate_ref[...] = jnp.zeros_like(dgate_ref)
            dw_acc[...] = jnp.zeros_like(dw_acc)

        dgate_ref[...] += jnp.sum(dxv * y_ref[...], axis=0, keepdims=True)
        dyb = (dxv * gate_ref[...]).astype(MXU)
        dw_acc[...] += lax.dot_general(u_ref[...], dyb, _TN, preferred_element_type=F32)

        @pl.when(i == steps - 1)
        def _():
            dw_ref[...] = dw_acc[...].astype(dw_ref.dtype)

        du = lax.dot_general(dyb, wt_ref[...], _NT, preferred_element_type=F32)
        head_of = (lax.broadcasted_iota(jnp.int32, (GROUP_W, LANES), 0) // HEAD_DIM
                   == lax.broadcasted_iota(jnp.int32, (GROUP_W, LANES), 1)).astype(F32)
        for du_g, o_ref, a, do_ref, dg_ref, dl_ref in (
                (du[:, :GROUP_W], of_ref, gates_ref[:, :GROUP_W], dof_ref, dfg_ref, dlf_ref),
                (du[:, GROUP_W:], om_ref, gates_ref[:, GROUP_W:], dom_ref, dmg_ref, dlm_ref)):
            sg = jax.nn.sigmoid(a)
            ov = o_ref[...]
            dov = du_g * (a * sg)
            do_ref[...] = dov.astype(MXU)
            dg_ref[...] = (du_g * ov * (sg * (1.0 + a * (1.0 - sg)))).astype(MXU)
            dl_ref[...] = jnp.dot(dov * ov, head_of, precision=lax.Precision.HIGH, preferred_element_type=F32)

    return pl.pallas_call(
        kern, name="kb_out", grid=(steps,),
        out_shape=[_sds((s_len, GROUP_W), MXU), _sds((s_len, GROUP_W), MXU),
                   _sds((s_len, GROUP_W), MXU), _sds((s_len, GROUP_W), MXU), _sds((s_len, LANES)),
                   _sds((s_len, LANES)), _sds((1, d)), _sds((2 * GROUP_W, d), dw_dtype)],
        in_specs=[_rows(tm, d), _rows(tm, d), _rows(tm, 2 * GROUP_W), _full((1, d)), _full((2 * GROUP_W, d)),
                  _rows(tm, GROUP_W), _rows(tm, GROUP_W), _rows(tm, 2 * GROUP_W)],
        out_specs=[_rows(tm, GROUP_W), _rows(tm, GROUP_W), _rows(tm, GROUP_W),
                   _rows(tm, GROUP_W), _rows(tm, LANES), _rows(tm, LANES), _full((1, d)), _full((2 * GROUP_W, d))],
        scratch_shapes=[pltpu.VMEM((2 * GROUP_W, d), F32)],
        compiler_params=_params("arbitrary"),
    )(dxo, y, u, gate, wout, of, om, gates)


def _kb_prep(dqn, dqr, dkn, dv, dkr, dff, tail, qn, kvn, cos, sin, gq, gkv, wuq_t, wukv_t, dw_dtype, tm=512):
    s_len = tail.shape[0]
    qw = 2 * GROUP_W
    steps = s_len // tm

    def kern(dqn_ref, dqr_ref, dkn_ref, dv_ref, dkr_ref, dff_ref, tail_ref, qn_ref, kvn_ref, cos_ref, sin_ref,
             gq_ref, gkv_ref, wuqt_ref, wukvt_ref, dz_ref, dgq_ref, dgkv_ref, dwuq_ref, dwukv_ref,
             dq_ref, uq_acc, ukv_acc):
        i = pl.program_id(0)

        @pl.when(i == 0)
        def _():
            dgq_ref[...] = jnp.zeros_like(dgq_ref)
            dgkv_ref[...] = jnp.zeros_like(dgkv_ref)
            uq_acc[...] = jnp.zeros_like(uq_acc)
            ukv_acc[...] = jnp.zeros_like(ukv_acc)

        cs, sn = cos_ref[...], sin_ref[...]
        dq_ref[:, :GROUP_W] = dqn_ref[...]
        for blk in range(PAIRS):
            sl = slice(blk * LANES, (blk + 1) * LANES)
            dq_ref[:, GROUP_W + blk * LANES:GROUP_W + (blk + 1) * LANES] = _rope_bwd(dqr_ref[:, sl], cs, sn).astype(MXU)
        dqn = lax.dot_general(dq_ref[...], wuqt_ref[...], _NT, preferred_element_type=F32)
        uq_acc[...] += lax.dot_general(qn_ref[...], dq_ref[...], _TN, preferred_element_type=F32)
        ukv_acc[:, :GROUP_W] += lax.dot_general(kvn_ref[...], dkn_ref[...], _TN, preferred_element_type=F32)
        ukv_acc[:, GROUP_W:] += lax.dot_general(kvn_ref[...], dv_ref[...], _TN, preferred_element_type=F32)

        @pl.when(i == steps - 1)
        def _():
            dwuq_ref[...] = uq_acc[...].astype(dwuq_ref.dtype)
            dwukv_ref[...] = ukv_acc[...].astype(dwukv_ref.dtype)

        ql = tail_ref[:, :T_KV]
        rq = lax.rsqrt(jnp.mean(ql * ql, axis=-1, keepdims=True) + EPS)
        qh = ql * rq
        dgq_ref[...] += jnp.sum(dqn * qh, axis=0, keepdims=True)
        dqh = dqn * gq_ref[...]
        dz_ref[:, :Q_LORA] = (rq * (dqh - qh * jnp.mean(dqh * qh, axis=-1, keepdims=True))).astype(MXU)

        dkvn = (lax.dot_general(dkn_ref[...], wukvt_ref[:, :GROUP_W], _NT, preferred_element_type=F32)
                + lax.dot_general(dv_ref[...], wukvt_ref[:, GROUP_W:], _NT, preferred_element_type=F32))
        kvl = tail_ref[:, T_KV:T_MISC]
        rk = lax.rsqrt(jnp.mean(kvl * kvl, axis=-1, keepdims=True) + EPS)
        kh = kvl * rk
        dgkv_ref[...] += jnp.sum(dkvn * kh, axis=0, keepdims=True)
        dkh = dkvn * gkv_ref[...]
        dz_ref[:, Q_LORA:Q_LORA + KV_LORA] = (
            rk * (dkh - kh * jnp.mean(dkh * kh, axis=-1, keepdims=True))).astype(MXU)

        g = dkr_ref[...] + pltpu.roll(dkr_ref[...], HEAD_DIM, 1)
        lane = lax.broadcasted_iota(jnp.int32, g.shape, 1)
        dmisc = jnp.where(lane < ROPE, _rope_bwd(g, cs, sn), 0.0) + dff_ref[...]
        dz_ref[:, Q_LORA + KV_LORA:] = dmisc.astype(MXU)

    return pl.pallas_call(
        kern, name="kb_prep", grid=(steps,),
        out_shape=[_sds((s_len, TAIL_W), MXU), _sds((1, Q_LORA)), _sds((1, KV_LORA)),
                   _sds((Q_LORA, qw), dw_dtype), _sds((KV_LORA, 2 * GROUP_W), dw_dtype)],
        in_specs=[_rows(tm, GROUP_W), _rows(tm, GROUP_W), _rows(tm, GROUP_W), _rows(tm, GROUP_W), _rows(tm, LANES),
                  _rows(tm, LANES), _rows(tm, TAIL_W), _rows(tm, Q_LORA), _rows(tm, KV_LORA),
                  _rows(tm, LANES), _rows(tm, LANES), _full((1, Q_LORA)), _full((1, KV_LORA)),
                  _full((Q_LORA, qw)), _full((KV_LORA, 2 * GROUP_W))],
        out_specs=[_rows(tm, TAIL_W), _full((1, Q_LORA)), _full((1, KV_LORA)), _full((Q_LORA, qw)),
                   _full((KV_LORA, 2 * GROUP_W))],
        scratch_shapes=[pltpu.VMEM((tm, qw), MXU), pltpu.VMEM((Q_LORA, qw), F32),
                        pltpu.VMEM((KV_LORA, 2 * GROUP_W), F32)],
        compiler_params=_params("arbitrary"),
    )(dqn, dqr, dkn, dv, dkr, dff, tail, qn, kvn, cos, sin, gq, gkv, wuq_t, wukv_t)


def _kb_in(dz_pieces, w, h, x, g, mod3, dxo, dw_dtype, tm=512):
    s_len, d = x.shape
    widths = [p.shape[1] for p in dz_pieces]
    n_p = len(widths)
    steps = s_len // tm

    def kern(*refs):
        dz_refs = refs[:n_p]
        w_ref, h_ref, x_ref, g_ref, mod_ref, dxo_ref, dx_ref, acc_ref, dw_ref, dw_acc = refs[n_p:]
        i = pl.program_id(0)

        @pl.when(i == 0)
        def _():
            acc_ref[...] = jnp.zeros_like(acc_ref)
            dw_acc[...] = jnp.zeros_like(dw_acc)

        dh = jnp.zeros((tm, d), F32)
        ht = h_ref[...]
        lo = 0
        for p_ref, wd in zip(dz_refs, widths):
            dh = dh + lax.dot_general(p_ref[...], w_ref[:, lo:lo + wd], _NT, preferred_element_type=F32)
            dw_acc[:, lo:lo + wd] += lax.dot_general(ht, p_ref[...], _TN, preferred_element_type=F32)
            lo += wd

        @pl.when(i == steps - 1)
        def _():
            dw_ref[...] = dw_acc[...].astype(dw_ref.dtype)

        xv = x_ref[...]
        r = lax.rsqrt(jnp.mean(xv * xv, axis=-1, keepdims=True) + EPS)
        xh = xv * r
        xn = xh * g_ref[...]
        dxn = dh * (1.0 + mod_ref[1:2, :])
        acc_ref[0:1, :] += jnp.sum(dh, axis=0, keepdims=True)
        acc_ref[1:2, :] += jnp.sum(dh * xn, axis=0, keepdims=True)
        acc_ref[2:3, :] += jnp.sum(dxn * xh, axis=0, keepdims=True)
        dxh = dxn * g_ref[...]
        dx_ref[...] = dxo_ref[...] + r * (dxh - xh * jnp.mean(dxh * xh, axis=-1, keepdims=True))

    return pl.pallas_call(
        kern, name="kb_in", grid=(steps,),
        out_shape=[_sds((s_len, d)), _sds((3, d)), _sds((d, Z_W), dw_dtype)],
        in_specs=[_rows(tm, wd) for wd in widths] + [_full((d, Z_W)), _rows(tm, d), _rows(tm, d), _full((1, d)),
                                                     _full((3, d)), _rows(tm, d)],
        out_specs=[_rows(tm, d), _full((3, d)), _full((d, Z_W))],
        scratch_shapes=[pltpu.VMEM((d, Z_W), F32)],
        compiler_params=_params("arbitrary"),
    )(*dz_pieces, w, h, x, g, mod3, dxo)


def _adamw(slabs, w, m, v, name):
    n_l = len(slabs)
    n, r, c = slabs[0].shape
    tm = r
    for cand in (256, 128, 64, 32, 16, 8):
        if r % cand == 0:
            tm = cand
            break
    steps = r // tm

    def kern(*refs):
        g_refs = refs[:n_l]
        w_ref, m_ref, v_ref, go_ref, d_ref, mo_ref, vo_ref, g_scr = refs[n_l:]
        for ll in range(n_l):
            @pl.when(pl.program_id(0) == ll)
            def _(g_ref=g_refs[ll]):
                g = g_ref[0].astype(F32)
                for s in range(1, n):
                    g = g + g_ref[s].astype(F32)
                g_scr[...] = g

        g = g_scr[...]
        m_new = ADAM_B1 * m_ref[...] + (1.0 - ADAM_B1) * g
        v_new = ADAM_B2 * v_ref[...] + (1.0 - ADAM_B2) * (g * g)
        m_hat = m_new / (1.0 - ADAM_B1 ** ADAM_STEP)
        v_hat = v_new / (1.0 - ADAM_B2 ** ADAM_STEP)
        go_ref[...] = g
        mo_ref[...] = m_new
        vo_ref[...] = v_new
        d_ref[...] = -ADAM_LR * (m_hat / (jnp.sqrt(v_hat) + ADAM_EPS) + ADAM_WD * w_ref[...])

    row = pl.BlockSpec((tm, c), lambda l, i: (l * steps + i, 0))

    def slab_spec(ll):
        return pl.BlockSpec((n, tm, c), lambda l, i: (0, jnp.where(l == ll, i, 0), 0))

    return pl.pallas_call(
        kern, name=name, grid=(n_l, steps), out_shape=[_sds((n_l * r, c))] * 4,
        in_specs=[slab_spec(ll) for ll in range(n_l)] + [row, row, row],
        out_specs=[row] * 4,
        scratch_shapes=[pltpu.VMEM((tm, c), F32)],
        compiler_params=_params("arbitrary", "arbitrary"),
    )(*slabs, w, m, v)


def _perm_w_in(w):
    pad = jnp.zeros(w.shape[:-1] + (Z_W - Z_MISC - ROPE - HEADS,), w.dtype)
    return jnp.concatenate([w[..., 0:1536], w[..., 1544:2056], w[..., 2472:2984], w[..., 2056:2312],
                            w[..., 2312:2440], w[..., 2440:2472], w[..., 1536:1544], pad], axis=-1)


def _unperm_w_in(g):
    ff0 = Z_MISC + MISC_FF
    return jnp.concatenate([g[..., 0:1536], g[..., ff0:ff0 + HEADS], g[..., Z_FG:Z_FG + GROUP_W],
                            g[..., Z_QL:Z_QL + Q_LORA], g[..., Z_KV:Z_KV + KV_LORA],
                            g[..., Z_MISC:Z_MISC + ROPE], g[..., Z_MG:Z_MG + GROUP_W]], axis=-1)


def _perm_w_uq(w):
    lead = w.shape[:-1]
    wh = w.reshape(lead + (PAIRS, 2, NOPE + ROPE))
    zero = jnp.zeros(lead + (PAIRS, HEAD_DIM - ROPE), w.dtype)
    rope = jnp.concatenate([wh[..., 1, NOPE:], zero, wh[..., 0, NOPE:], zero], axis=-1)
    return jnp.concatenate([wh[..., :NOPE].reshape(lead + (GROUP_W,)), rope.reshape(lead + (GROUP_W,))], axis=-1)


def _unperm_w_uq(g):
    lead = g.shape[:-1]
    nope = g[..., :GROUP_W].reshape(lead + (PAIRS, 2, NOPE))
    rp = g[..., GROUP_W:].reshape(lead + (PAIRS, 2, HEAD_DIM))[..., :ROPE]
    return jnp.concatenate([nope, rp[..., ::-1, :]], axis=-1).reshape(lead + (HEADS * (NOPE + ROPE),))


def _perm_w_ukv(w):
    lead = w.shape[:-1]
    wh = w.reshape(lead + (HEADS, 2 * HEAD_DIM))
    return jnp.concatenate([wh[..., :NOPE].reshape(lead + (GROUP_W,)),
                            wh[..., NOPE:].reshape(lead + (GROUP_W,))], axis=-1)


def _unperm_w_ukv(g):
    lead = g.shape[:-1]
    parts = [g[..., :GROUP_W].reshape(lead + (HEADS, NOPE)), g[..., GROUP_W:].reshape(lead + (HEADS, HEAD_DIM))]
    return jnp.concatenate(parts, axis=-1).reshape(lead + (2 * GROUP_W,))


def _rope_tables(positions):
    inv_freq = 1.0 / (ROPE_THETA ** (jnp.arange(0, ROPE, 2, dtype=F32) / ROPE))
    ang = positions.astype(F32)[:, None] * inv_freq
    cos, sin = jnp.cos(ang), jnp.sin(ang)
    reps = LANES // ROPE
    return jnp.tile(jnp.concatenate([cos, cos], axis=1), (1, reps)), jnp.tile(jnp.concatenate([-sin, sin], axis=1), (1, reps))


def _full_weights(g_in, g_uq, g_ukv, g_out):
    def cols(g):
        return g.transpose(1, 0, 2).reshape(g.shape[1], -1)
    return (_perm_w_in(cols(g_in)), _perm_w_uq(cols(g_uq)), _perm_w_ukv(cols(g_ukv)),
            g_out.reshape(-1, g_out.shape[2]))


def _grad_slabs(dw_in, dw_uq, dw_ukv):
    def cols(g):
        return g.reshape(g.shape[0], N_DEV, -1).transpose(1, 0, 2)
    return [cols(_unperm_w_in(dw_in)), cols(_unperm_w_uq(dw_uq)), cols(_unperm_w_ukv(dw_ukv))]


def _local_step(x, mod, positions, loss_target, norm_g, b_f, q_norm_g, kv_norm_g, final_g, weights, shards=None):
    n_l = norm_g.shape[0]
    s_len, d = x.shape
    cos, sin = _rope_tables(positions)
    qb, kb, vb = Z_FQ // LANES, Z_FK // LANES, Z_FV // LANES
    chunks = s_len // LANES
    weights = list(weights)

    def pack_rows(a_rows, delta):
        return jnp.concatenate([a_rows.T, delta[:, :HEADS], jnp.zeros((s_len, LANES - 2 * HEADS), F32)], axis=1)

    saved = []
    for l in range(n_l):
        w_in, w_uq, w_ukv, w_out = weights[l]
        mod3 = mod[l].reshape(3, d)
        h, qkv, gates, tail = _k_in(x, norm_g[l][None], mod3, w_in)
        fft = tail[:, T_MISC + MISC_FF:T_MISC + MISC_FF + HEADS].T.reshape(HEADS * chunks, LANES)
        bf = jnp.repeat(b_f[l], chunks)[:, None]
        c2 = _k_cum(fft, bf, chunks).reshape(HEADS, s_len) * LOG2E
        side = (list(shards[l + 1]), [True] * 4) if shards is not None and l + 1 < n_l else None
        ck_lanes = jnp.pad(c2.T, ((0, 0), (0, LANES - HEADS)))
        of, lse_f, *gathered = _attention_fwd(qkv, qb, qkv, kb, qkv, vb, ck_lanes, None, False,
                                              "fox_fwd_gather" if side else "fox_fwd", side)
        if side:
            weights.append(_full_weights(*gathered))
        mq, mqr, mk, mv, kr2, qn, kvn = _k_prep(tail, cos, sin, q_norm_g[l][None], kv_norm_g[l][None], w_uq, w_ukv)
        om, lse_m = _attention_fwd(mq, 0, mk, 0, mv, 0, None, (mqr, kr2), True, "mla_fwd")
        x_new, y, u = _k_out(of, om, gates, x, mod3[2:3], w_out)
        saved.append((x, gates, tail, h, qkv, fft, bf, c2, lse_f, mq, mqr, mk, mv, kr2, lse_m, of, om, qn, kvn, y, u,
                      mod3))
        x = x_new

    loss_row, dx, dfinal = _k_loss(x, final_g[None], loss_target)

    grads = {k: [] for k in ("norm_g", "mod", "w_in", "b_f", "q_norm_g", "w_uq", "kv_norm_g", "w_ukv", "w_out")}
    received, pending = {}, None
    wg_dtype = MXU if shards is not None else F32
    for l in range(n_l - 1, -1, -1):
        (x_l, gates, tail, h, qkv, fft, bf, c2, lse_f, mq, mqr, mk, mv, kr2, lse_m, of, om, qn, kvn, y, u,
         mod3) = saved[l]
        w_in, w_uq, w_ukv, w_out = weights[l]
        dof, dom, dfg, dmg, dlt_f, dlt_m, dgate, dw_out = _kb_out(dx, y, u, mod3[2:3], w_out, of, om, gates, wg_dtype)

        side = None
        if shards is not None:
            side_arrs = (pending or []) + [dw_out.reshape(N_DEV, -1, dw_out.shape[1])]
            side = (side_arrs, [False] * len(side_arrs))
        dfq, dfk, dfv, dcq, dck, *arrived = _attention_bwd(
            qkv, qb, qkv, kb, qkv, vb, dof, pack_rows(-lse_f.reshape(HEADS, s_len), dlt_f), c2[:, None, :], None,
            False, FOX_SCALE, 1.0 / LOG2E, "fox_bwd_exchange" if pending else "fox_bwd", side)
        if side:
            received[l] = [None, None, None, arrived[-1]]
            if pending:
                received[l + 1][:3] = arrived[:3]
        dcum = (dcq[:, :HEADS].T + dck.reshape(HEADS, s_len)).reshape(HEADS * chunks, LANES)
        dff_rows, dbf_rows = _k_cum_bwd(dcum, fft, bf, chunks)
        dfft = dff_rows.reshape(HEADS, s_len)
        grads["b_f"].append(jnp.sum(dbf_rows[:, 0].reshape(HEADS, chunks), axis=1))

        dmq, dkn, dmv, dqr, dkr_pairs = _attention_bwd(
            mq, 0, mk, 0, mv, 0, dom, pack_rows(-lse_m.reshape(HEADS, s_len), dlt_m), None, (mqr, kr2), True,
            MLA_SCALE, 1.0 / LOG2E, "mla_bwd")
        dkr = dkr_pairs[0] + dkr_pairs[1] + dkr_pairs[2] + dkr_pairs[3]
        dff = jnp.pad(dfft.T, ((0, 0), (MISC_FF, LANES - MISC_FF - HEADS)))
        dz_tail, dgq, dgkv, dw_uq, dw_ukv = _kb_prep(dmq, dqr, dkn, dmv, dkr, dff, tail, qn, kvn, cos, sin,
                                                     q_norm_g[l][None], kv_norm_g[l][None], w_uq, w_ukv, wg_dtype)
        grads["q_norm_g"].append(dgq[0])
        grads["kv_norm_g"].append(dgkv[0])
        dz = [dfq, dfk, dfv, dfg, dmg, dz_tail]
        dx, acc3, dw_in = _kb_in(dz, w_in, h, x_l, norm_g[l][None], mod3, dx, wg_dtype)
        grads["norm_g"].append(acc3[2])
        grads["mod"].append(jnp.concatenate([acc3[0], acc3[1], dgate[0]]))
        if shards is not None:
            pending = _grad_slabs(dw_in, dw_uq, dw_ukv)
        else:
            for name, g in (("w_in", dw_in), ("w_uq", dw_uq), ("w_ukv", dw_ukv), ("w_out", dw_out)):
                grads[name].append(g)
    grads = {k: jnp.stack(v[::-1]) for k, v in grads.items() if v}
    grads["final_g"] = dfinal[0]
    if shards is None:
        return loss_row[0, 0], dx, grads
    return loss_row[0, 0], dx, grads, received, pending


def _pack_small(parts, total):
    flat = jnp.concatenate([p.reshape(-1) for p in parts])
    return jnp.pad(flat, (0, total - flat.shape[0])).reshape(total // LANES, LANES)


def kernel(x, c, positions, norm_g, w_ada, b_ada, w_in, b_f, q_norm_g, w_uq, kv_norm_g, w_ukv, w_out, final_g, loss_target, m_norm_g, m_w_ada, m_b_ada, m_w_in, m_b_f, m_q_norm_g, m_w_uq, m_kv_norm_g, m_w_ukv, m_w_out, m_final_g, v_norm_g, v_w_ada, v_b_ada, v_w_in, v_b_f, v_q_norm_g, v_w_uq, v_kv_norm_g, v_w_ukv, v_w_out, v_final_g):
    n_l, d = norm_g.shape
    me = 4 * lax.axis_index("x") + 2 * lax.axis_index("y") + lax.axis_index("c")
    ada_c = w_ada.shape[2]

    cact = jnp.broadcast_to(jax.nn.silu(c), (N_DEV, d))
    shards = [[w[l].astype(MXU) for w in (w_in, w_uq, w_ukv, w_out)] for l in range(n_l)]
    *g_w0, g_cact = _gather_two_level(shards[0] + [cact], "gather_layer0")
    cact_all = g_cact[:, 0, :]

    b_cols = lax.dynamic_slice_in_dim(b_ada, me * ada_c, ada_c, axis=1)[:, None, :]
    modpart = _modpart(cact_all, w_ada, b_cols)
    mod_send = jnp.pad(modpart.transpose(1, 0, 2), ((0, 0), (0, 8 - n_l), (0, 0)))
    (mod_recv,) = _exchange([mod_send], [False], "scatter_mod")
    mod = mod_recv.transpose(1, 0, 2).reshape(8, N_DEV * ada_c)[:n_l]

    loss, dx, gr, received, pending = _local_step(x[0], mod, positions[0], loss_target[0], norm_g, b_f, q_norm_g,
                                                  kv_norm_g, final_g, [_full_weights(*g_w0)], shards)

    small_parts = [gr["norm_g"], gr["mod"], gr["b_f"], gr["q_norm_g"], gr["kv_norm_g"], gr["final_g"], cact[0],
                   loss.reshape(1)]
    sizes = [int(np.prod(p.shape)) for p in small_parts]
    total = -(-sum(sizes) // 1024) * 1024
    small = _pack_small(small_parts, total)
    *received[0][:3], r_small = _exchange(pending + [small], [False, False, False, True], "exchange_layer0")
    r_in, r_uq, r_ukv, r_out = ([received[l][i] for l in range(n_l)] for i in range(4))

    def upd(slabs, w, m, v, name):
        shp = w.shape
        w2, m2, v2 = (a.reshape(-1, slabs[0].shape[2]) for a in (w, m, v))
        return [o.reshape(shp) for o in _adamw(slabs, w2, m2, v2, name)]

    o_in = upd(r_in, w_in, m_w_in, v_w_in, "adamw_w_in")
    o_uq = upd(r_uq, w_uq, m_w_uq, v_w_uq, "adamw_w_uq")
    o_ukv = upd(r_ukv, w_ukv, m_w_ukv, v_w_ukv, "adamw_w_ukv")
    o_out = upd(r_out, w_out, m_w_out, v_w_out, "adamw_w_out")

    offs = np.cumsum([0] + sizes)
    flat_all = r_small.reshape(N_DEV, total)
    dmod_all = flat_all[:, offs[1]:offs[2]].reshape(N_DEV, n_l, 3 * d)
    dmod_cols = lax.dynamic_slice_in_dim(dmod_all, me * ada_c, ada_c, axis=2).transpose(1, 0, 2)
    cact_cols = flat_all[:, offs[6]:offs[7]][:, :, None]
    g_ada = _ada_grad(cact_cols, dmod_cols)
    o_ada = upd([g_ada.reshape(1, n_l * d, ada_c)], w_ada, m_w_ada, v_w_ada, "adamw_w_ada")

    zero_c = jnp.zeros((d,), F32)
    small_w = [_pack_small([norm_g, b_ada, b_f, q_norm_g, kv_norm_g, final_g, zero_c], total),
               _pack_small([m_norm_g, m_b_ada, m_b_f, m_q_norm_g, m_kv_norm_g, m_final_g, zero_c], total),
               _pack_small([v_norm_g, v_b_ada, v_b_f, v_q_norm_g, v_kv_norm_g, v_final_g, zero_c], total)]
    o_small = [o.reshape(-1) for o in _adamw([r_small], *small_w, "adamw_small")]
    shapes = [norm_g.shape, b_ada.shape, b_f.shape, q_norm_g.shape, kv_norm_g.shape, final_g.shape]

    def small_out(kind, idx):
        return o_small[kind][offs[idx]:offs[idx + 1]].reshape(shapes[idx])

    outs = [o_small[0][offs[7]], dx[None]]
    for kind in range(4):
        outs += [small_out(kind, 0), o_ada[kind], small_out(kind, 1), o_in[kind], small_out(kind, 2),
                 small_out(kind, 3), o_uq[kind], small_out(kind, 4), o_ukv[kind], o_out[kind], small_out(kind, 5)]
    return tuple(outs)
```

```python
import jax
import jax.numpy as jnp
import numpy as np
from jax import lax
from jax.experimental import pallas as pl
from jax.experimental.pallas import tpu as pltpu

F32 = jnp.float32
MXU = jnp.bfloat16

N_DEV = 8
HEADS = 8
PAIRS = HEADS // 2
HEAD_DIM = 64
NOPE = 64
ROPE = 32
HALF_ROPE = ROPE // 2
Q_LORA = 256
KV_LORA = 128
CHUNK = 64
GROUP_W = HEADS * HEAD_DIM
EPS = 1e-6
ROPE_THETA = 10000.0

Z_FQ, Z_FK, Z_FV, Z_FG, Z_MG, Z_QL, Z_KV, Z_MISC, Z_W = 0, 512, 1024, 1536, 2048, 2560, 2816, 2944, 3072
MISC_FF = ROPE
TAIL_W = Z_W - Z_QL
T_KV, T_MISC = Q_LORA, Q_LORA + KV_LORA

ADAM_LR = 0.001
ADAM_B1 = 0.9
ADAM_B2 = 0.999
ADAM_EPS = 1e-08
ADAM_WD = 0.01
ADAM_STEP = 10

VMEM_LIMIT_V7X = 56 * 1024 * 1024
LANES = 128
ATTN_TILE = 1024
V_ROWS = HEAD_DIM + 16
LOG2E = 1.4426950408889634
FOX_SCALE = HEAD_DIM ** -0.5
MLA_SCALE = (NOPE + ROPE) ** -0.5

_NT = (((1,), (1,)), ((), ()))
_TN = (((0,), (0,)), ((), ()))


def _params(*sem, side_effects=False):
    return pltpu.CompilerParams(dimension_semantics=sem, vmem_limit_bytes=VMEM_LIMIT_V7X,
                                has_side_effects=side_effects)


def _sds(shape, dtype=F32):
    return jax.ShapeDtypeStruct(shape, dtype)


def _full(shape):
    nd = len(shape)
    return pl.BlockSpec(shape, lambda *_: (0,) * nd)


def _rows(tm, width, col=0):
    return pl.BlockSpec((tm, width), lambda i: (i, col))


def _exchange(arrs, gather, name):
    n = len(arrs)

    def kern(*refs):
        copies = _exchange_copies(refs[:n], refs[n:2 * n], gather, *refs[2 * n:])
        _exchange_start(copies)
        _exchange_wait(copies)

    return pl.pallas_call(
        kern, name=name, out_shape=_exchange_out_shapes(arrs, gather),
        in_specs=[pl.BlockSpec(memory_space=pl.ANY)] * n,
        out_specs=[pl.BlockSpec(memory_space=pl.ANY)] * n,
        scratch_shapes=_exchange_sems(n),
        compiler_params=pltpu.CompilerParams(has_side_effects=True),
    )(*arrs)


def _gather_two_level(arrs, name):
    n = len(arrs)

    def kern(*refs):
        ins, outs = refs[:n], refs[n:2 * n]
        send_sems, recv_sems, loc_sems = refs[2 * n:]
        x, y, c = lax.axis_index("x"), lax.axis_index("y"), lax.axis_index("c")
        me, sibling = (x, y, c), (x, y, 1 - c)
        chips = [(1 - x, y), (x, 1 - y), (1 - x, 1 - y)]

        def slot(i, dev):
            return outs[i].at[4 * dev[0] + 2 * dev[1] + dev[2]]

        def copy(i, k, block, to, src=None):
            return pltpu.make_async_remote_copy(
                src_ref=slot(i, block) if src is None else src, dst_ref=slot(i, block), send_sem=send_sems.at[i, k],
                recv_sem=recv_sems.at[i, k], device_id=to, device_id_type=pl.DeviceIdType.MESH)

        mine = [pltpu.make_async_copy(ins[i], slot(i, me), loc_sems.at[i]) for i in range(n)]
        first = [copy(i, 0, me, sibling, src=ins[i]) for i in range(n)]
        first += [copy(i, 1 + j, me, (*chip, c), src=ins[i]) for j, chip in enumerate(chips) for i in range(n)]
        for cp in mine + first:
            cp.start()
        passed = []
        for j, chip in enumerate(chips):
            for i in range(n):
                copy(i, 1 + j, (*chip, c), me).wait_recv()
                fwd = copy(i, 4 + j, (*chip, c), sibling)
                fwd.start()
                passed.append(fwd)
        for i in range(n):
            copy(i, 0, sibling, me).wait_recv()
            for j, chip in enumerate(chips):
                copy(i, 4 + j, (*chip, 1 - c), me).wait_recv()
        for cp in first + passed:
            cp.wait_send()
        for cp in mine:
            cp.wait()

    return pl.pallas_call(
        kern, name=name, out_shape=_exchange_out_shapes(arrs, [True] * n),
        in_specs=[pl.BlockSpec(memory_space=pl.ANY)] * n,
        out_specs=[pl.BlockSpec(memory_space=pl.ANY)] * n,
        scratch_shapes=_exchange_sems(n),
        compiler_params=pltpu.CompilerParams(has_side_effects=True),
    )(*arrs)


def _exchange_out_shapes(arrs, gather):
    return [_sds((N_DEV,) + tuple(a.shape) if g else tuple(a.shape), a.dtype) for a, g in zip(arrs, gather)]


def _exchange_sems(n):
    return [pltpu.SemaphoreType.DMA((n, N_DEV)), pltpu.SemaphoreType.DMA((n, N_DEV)), pltpu.SemaphoreType.DMA((n,))]


def _exchange_copies(ins, outs, gather, send_sems, recv_sems, loc_sems, recv=True):
    n = len(ins)
    x, y, c = lax.axis_index("x"), lax.axis_index("y"), lax.axis_index("c")
    me = 4 * x + 2 * y + c

    def src(i, j):
        return ins[i] if gather[i] else ins[i].at[j]

    local = [pltpu.make_async_copy(src(i, me), outs[i].at[me], loc_sems.at[i]) for i in range(n)]
    sends, recvs = [], []
    for k in range(1, N_DEV):
        px = 1 - x if k & 4 else x
        py = 1 - y if k & 2 else y
        pc = 1 - c if k & 1 else c
        p = 4 * px + 2 * py + pc
        for i in range(n):
            sends.append(pltpu.make_async_remote_copy(
                src_ref=src(i, p), dst_ref=outs[i].at[me], send_sem=send_sems.at[i, k],
                recv_sem=recv_sems.at[i, k], device_id=(px, py, pc), device_id_type=pl.DeviceIdType.MESH))
            if recv:
                recvs.append(pltpu.make_async_remote_copy(
                    src_ref=src(i, p), dst_ref=outs[i].at[p], send_sem=send_sems.at[i, k],
                    recv_sem=recv_sems.at[i, k], device_id=(px, py, pc), device_id_type=pl.DeviceIdType.MESH))
    return local, sends, recvs


def _exchange_start(copies):
    local, sends, _ = copies
    for cp in local + sends:
        cp.start()


def _exchange_wait(copies):
    local, sends, recvs = copies
    for cp in recvs:
        cp.wait_recv()
    for cp in sends:
        cp.wait_send()
    for cp in local:
        cp.wait()


def _modpart(cact8, w_ada, b_cols):
    n_l, d, cw = w_ada.shape

    def kern(c_ref, w_ref, b_ref, o_ref):
        o_ref[0] = jnp.dot(c_ref[...].astype(MXU), w_ref[0].astype(MXU), preferred_element_type=F32) + b_ref[0]

    return pl.pallas_call(
        kern, name="modpart", grid=(n_l,), out_shape=_sds((n_l, N_DEV, cw)),
        in_specs=[_full((N_DEV, d)), pl.BlockSpec((1, d, cw), lambda l: (l, 0, 0)),
                  pl.BlockSpec((1, 1, cw), lambda l: (l, 0, 0))],
        out_specs=pl.BlockSpec((1, N_DEV, cw), lambda l: (l, 0, 0)),
        compiler_params=_params("arbitrary"),
    )(cact8, w_ada, b_cols)


def _ada_grad(cact_cols, dmod_cols):
    n_l, _, cw = dmod_cols.shape
    d = cact_cols.shape[1]

    def kern(c_ref, dm_ref, o_ref):
        acc = c_ref[0] * dm_ref[0, 0:1, :]
        for s in range(1, N_DEV):
            acc = acc + c_ref[s] * dm_ref[0, s:s + 1, :]
        o_ref[0] = acc

    return pl.pallas_call(
        kern, name="ada_grad", grid=(n_l,), out_shape=_sds((n_l, d, cw)),
        in_specs=[_full((N_DEV, d, 1)), pl.BlockSpec((1, N_DEV, cw), lambda l: (l, 0, 0))],
        out_specs=pl.BlockSpec((1, d, cw), lambda l: (l, 0, 0)),
        compiler_params=_params("arbitrary"),
    )(cact_cols, dmod_cols)


def _k_in(x, g, mod3, w, tm=256):
    s_len, d = x.shape
    qkv_w = 3 * GROUP_W

    def kern(x_ref, g_ref, mod_ref, w_ref, h_ref, qkv_ref, gates_ref, tail_ref):
        xv = x_ref[...]
        r = lax.rsqrt(jnp.mean(xv * xv, axis=-1, keepdims=True) + EPS)
        xn = xv * r * g_ref[...]
        h = (xn * (1.0 + mod_ref[1:2, :]) + mod_ref[0:1, :]).astype(MXU)
        h_ref[...] = h
        z = jnp.dot(h, w_ref[...], preferred_element_type=F32)
        qkv_ref[:, :GROUP_W] = (z[:, Z_FQ:Z_FQ + GROUP_W] * (FOX_SCALE * LOG2E)).astype(MXU)
        qkv_ref[:, GROUP_W:] = z[:, Z_FK:Z_FK + 2 * GROUP_W].astype(MXU)
        gates_ref[...] = z[:, Z_FG:Z_QL]
        tail_ref[...] = z[:, Z_QL:]

    return pl.pallas_call(
        kern, name="k_in", grid=(s_len // tm,),
        out_shape=[_sds((s_len, d), MXU), _sds((s_len, qkv_w), MXU), _sds((s_len, Z_QL - Z_FG)),
                   _sds((s_len, TAIL_W))],
        in_specs=[_rows(tm, d), _full((1, d)), _full((3, d)), _full((d, Z_W))],
        out_specs=[_rows(tm, d), _rows(tm, qkv_w), _rows(tm, Z_QL - Z_FG), _rows(tm, TAIL_W)],
        compiler_params=_params("arbitrary"),
    )(x, g, mod3, w)


def _scan_matrices(rows, chunks, reverse):
    r_i = lax.broadcasted_iota(jnp.int32, (LANES, LANES), 0)
    c_i = lax.broadcasted_iota(jnp.int32, (LANES, LANES), 1)
    a_i = lax.broadcasted_iota(jnp.int32, (rows, rows), 0)
    b_i = lax.broadcasted_iota(jnp.int32, (rows, rows), 1)
    same_head = (a_i // chunks) == (b_i // chunks)
    if reverse:
        return (r_i >= c_i).astype(F32), (same_head & (b_i > a_i)).astype(F32)
    return (r_i <= c_i).astype(F32), (same_head & (b_i < a_i)).astype(F32)


def _scan_rows(x, inner, outer):
    tot = jnp.broadcast_to(jnp.sum(x, axis=1, keepdims=True), x.shape)
    return (jnp.dot(x, inner, precision=lax.Precision.HIGHEST, preferred_element_type=F32)
            + jnp.dot(outer, tot, precision=lax.Precision.HIGHEST, preferred_element_type=F32))


def _k_cum(ff_rows, b_rows, chunks):
    rows = ff_rows.shape[0]

    def kern(ff_ref, b_ref, cum_ref):
        xc = ff_ref[...] + b_ref[...]
        lf = jnp.minimum(xc, 0.0) - jnp.log(1.0 + jnp.exp(-jnp.abs(xc)))
        cum_ref[...] = _scan_rows(lf, *_scan_matrices(rows, chunks, False))

    return pl.pallas_call(
        kern, name="k_cum", out_shape=_sds((rows, LANES)),
        in_specs=[pl.BlockSpec(memory_space=pltpu.VMEM)] * 2,
        out_specs=pl.BlockSpec(memory_space=pltpu.VMEM),
        compiler_params=_params(),
    )(ff_rows, b_rows)


def _k_cum_bwd(dc_rows, ff_rows, b_rows, chunks):
    rows = ff_rows.shape[0]

    def kern(dc_ref, ff_ref, b_ref, dff_ref, db_ref):
        dlf = _scan_rows(dc_ref[...], *_scan_matrices(rows, chunks, True))
        dff = dlf * jax.nn.sigmoid(-(ff_ref[...] + b_ref[...]))
        dff_ref[...] = dff
        db_ref[...] = jnp.broadcast_to(jnp.sum(dff, axis=1, keepdims=True), dff.shape)

    return pl.pallas_call(
        kern, name="k_cum_bwd", out_shape=[_sds((rows, LANES)), _sds((rows, LANES))],
        in_specs=[pl.BlockSpec(memory_space=pltpu.VMEM)] * 3,
        out_specs=[pl.BlockSpec(memory_space=pltpu.VMEM)] * 2,
        compiler_params=_params(),
    )(dc_rows, ff_rows, b_rows)


def _swap16(t):
    lane = lax.broadcasted_iota(jnp.int32, t.shape, 1)
    return jnp.where(lane % ROPE < HALF_ROPE, pltpu.roll(t, LANES - HALF_ROPE, 1), pltpu.roll(t, HALF_ROPE, 1))


def _rope(t, cos, sin):
    return t * cos + _swap16(t) * sin


def _rope_bwd(dt, cos, sin):
    return dt * cos - _swap16(dt) * sin


def _k_prep(tail, cos, sin, gq, gkv, wuq, wukv, tm=512):
    s_len = tail.shape[0]
    qc = MLA_SCALE * LOG2E

    def kern(tail_ref, cos_ref, sin_ref, gq_ref, gkv_ref, wuq_ref, wukv_ref,
             qn_out, qr_out, kn_out, v_out, kr_out, qn_ref, kvn_ref):
        cs, sn = cos_ref[...], sin_ref[...]
        ql = tail_ref[:, :T_KV]
        rq = lax.rsqrt(jnp.mean(ql * ql, axis=-1, keepdims=True) + EPS)
        qn = (ql * rq * gq_ref[...]).astype(MXU)
        qn_ref[...] = qn
        q = jnp.dot(qn, wuq_ref[...], preferred_element_type=F32)
        qn_out[...] = (q[:, :GROUP_W] * qc).astype(MXU)
        for blk in range(PAIRS):
            lo = GROUP_W + blk * LANES
            qr_out[:, blk * LANES:(blk + 1) * LANES] = (_rope(q[:, lo:lo + LANES], cs, sn) * qc).astype(MXU)
        kvl = tail_ref[:, T_KV:T_MISC]
        rk = lax.rsqrt(jnp.mean(kvl * kvl, axis=-1, keepdims=True) + EPS)
        kvn = (kvl * rk * gkv_ref[...]).astype(MXU)
        kvn_ref[...] = kvn
        kv = jnp.dot(kvn, wukv_ref[...], preferred_element_type=F32)
        kn_out[...] = kv[:, :GROUP_W].astype(MXU)
        v_out[...] = kv[:, GROUP_W:].astype(MXU)
        misc = tail_ref[:, T_MISC:]
        lane = lax.broadcasted_iota(jnp.int32, misc.shape, 1)
        kr = jnp.where(lane < ROPE, _rope(misc, cs, sn), 0.0)
        kr_out[...] = (kr + pltpu.roll(kr, HEAD_DIM, 1)).astype(MXU)

    return pl.pallas_call(
        kern, name="k_prep", grid=(s_len // tm,),
        out_shape=[_sds((s_len, GROUP_W), MXU), _sds((s_len, GROUP_W), MXU), _sds((s_len, GROUP_W), MXU),
                   _sds((s_len, GROUP_W), MXU), _sds((s_len, LANES), MXU), _sds((s_len, Q_LORA), MXU),
                   _sds((s_len, KV_LORA), MXU)],
        in_specs=[_rows(tm, TAIL_W), _rows(tm, LANES), _rows(tm, LANES),
                  _full((1, Q_LORA)), _full((1, KV_LORA)), _full((Q_LORA, 2 * GROUP_W)),
                  _full((KV_LORA, 2 * GROUP_W))],
        out_specs=[_rows(tm, GROUP_W), _rows(tm, GROUP_W), _rows(tm, GROUP_W), _rows(tm, GROUP_W), _rows(tm, LANES),
                   _rows(tm, Q_LORA), _rows(tm, KV_LORA)],
        compiler_params=_params("arbitrary"),
    )(tail, cos, sin, gq, gkv, wuq, wukv)


def _block_mask(kn, qn, q_off, chunk_mask, transposed):
    shape = (kn, qn) if transposed else (qn, kn)
    row = lax.broadcasted_iota(jnp.int32, shape, 0)
    col = lax.broadcasted_iota(jnp.int32, shape, 1)
    qi, ki = (col + q_off, row) if transposed else (row + q_off, col)
    if chunk_mask:
        return (ki // CHUNK) <= (qi // CHUNK)
    return ki <= qi


def _head_operand(x, hh, other=None):
    lane = lax.broadcasted_iota(jnp.int32, x.shape, 1)
    own = (lane >= hh * HEAD_DIM) & (lane < (hh + 1) * HEAD_DIM)
    return jnp.where(own, x, jnp.zeros_like(x) if other is None else other)


def _attention_fwd(q, q_blk, k, k_blk, v, v_blk, bias, rope, chunk_mask, name, side=None):
    s_len = q.shape[0]
    t = min(ATTN_TILE, s_len // 2)
    nq = s_len // t
    n_side = len(side[0]) if side else 0

    def kern(*refs):
        q_ref, k_ref, v_ref = refs[:3]
        pos = 3
        if bias is not None:
            ck_ref = refs[pos]
            pos += 1
        if rope is not None:
            qr_ref, kr_ref = refs[pos:pos + 2]
            pos += 2
        side_in = refs[pos:pos + n_side]
        pos += n_side
        o_ref, lse_ref = refs[pos:pos + 2]
        side_out = refs[pos + 2:pos + 2 + n_side]
        vt_scr, m_scr, acc_scr, ck_scr = refs[pos + 2 + n_side:pos + 6 + n_side]
        sems = refs[pos + 6 + n_side:]
        pj = pl.program_id(0)
        if n_side:
            @pl.when(pj == 0)
            def _():
                _exchange_start(_exchange_copies(side_in, side_out, side[1], *sems, recv=False))
        vt_scr[:, HEAD_DIM:, :] = jnp.ones((2, V_ROWS - HEAD_DIM, s_len), vt_scr.dtype)
        for i in range(nq):
            vtt = v_ref[i * t:(i + 1) * t, :].T
            for hh in range(2):
                vt_scr[hh, :HEAD_DIM, i * t:(i + 1) * t] = vtt[hh * HEAD_DIM:(hh + 1) * HEAD_DIM, :]
                if bias is not None:
                    ckt = ck_ref[i * t:(i + 1) * t, :]
                    lane = lax.broadcasted_iota(jnp.int32, ckt.shape, 1)
                    ck_scr[hh, i * t:(i + 1) * t, :] = jnp.sum(jnp.where(lane == 2 * pj + hh, ckt, 0.0), axis=1,
                                                               keepdims=True)

        def qbody(qi, _):
            qs = pl.multiple_of(qi * t, t)
            qt = q_ref[pl.ds(qs, t), :]
            qrt = qr_ref[pl.ds(qs, t), :] if rope is not None else None
            qh = [_head_operand(qt, hh, qrt) for hh in range(2)]
            m_scr[...] = jnp.full(m_scr.shape, -jnp.inf, F32)
            acc_scr[...] = jnp.zeros(acc_scr.shape, F32)

            def block(ks, kn, q0, qn, masked):
                kt = k_ref[pl.ds(ks, kn), :]
                kh = [_head_operand(kt, hh, kr_ref[pl.ds(ks, kn), :]) for hh in range(2)] if rope is not None else [kt, kt]
                qc = slice(q0, q0 + qn)
                sts = [lax.dot_general(kh[hh], qh[hh][qc], _NT, preferred_element_type=F32) for hh in range(2)]
                if bias is not None:
                    sts = [sts[hh] - ck_scr[hh, pl.ds(ks, kn), :] for hh in range(2)]
                if masked:
                    sts = [jnp.where(_block_mask(kn, qn, q0, chunk_mask, True), st, -jnp.inf) for st in sts]
                m_old = [m_scr[hh, :, qc] for hh in range(2)]
                m_new = [jnp.maximum(m_old[hh], jnp.max(sts[hh], axis=0, keepdims=True)) for hh in range(2)]
                pts = [jnp.exp2(sts[hh] - m_new[hh]).astype(MXU) for hh in range(2)]
                for hh in range(2):
                    alpha = jnp.exp2(m_old[hh] - m_new[hh])
                    acc_scr[hh, :, qc] = alpha * acc_scr[hh, :, qc] + jnp.dot(vt_scr[hh, :, pl.ds(ks, kn)], pts[hh],
                                                                            preferred_element_type=F32)
                    m_scr[hh, :, qc] = m_new[hh]

            def loop_body(ki, carry):
                block(pl.multiple_of(ki * t, t), t, 0, t, False)
                return carry

            lax.fori_loop(0, qi, loop_body, 0)
            block(qs, t, 0, t, True)
            outs = []
            for hh in range(2):
                acc = acc_scr[hh]
                l = acc[HEAD_DIM:HEAD_DIM + 1, :]
                outs.append(acc[:HEAD_DIM, :] / l)
                lse_ref[hh, :, pl.ds(qs, t)] = m_scr[hh] + jnp.log2(l)
            o_ref[pl.ds(qs, t), :] = jnp.concatenate(outs, axis=0).T
            return 0

        lax.fori_loop(0, nq, qbody, 0)
        if n_side:
            @pl.when(pj == PAIRS - 1)
            def _():
                _exchange_wait(_exchange_copies(side_in, side_out, side[1], *sems))

    def tok(blk):
        return pl.BlockSpec((s_len, LANES), lambda j: (0, blk + j))

    rowb = pl.BlockSpec((2, 1, s_len), lambda j: (j, 0, 0))
    hbm = pl.BlockSpec(memory_space=pl.ANY)
    ins = [q, k, v]
    in_specs = [tok(q_blk), tok(k_blk), tok(v_blk)]
    if bias is not None:
        ins.append(bias)
        in_specs.append(_full((s_len, LANES)))
    if rope is not None:
        ins += list(rope)
        in_specs += [tok(0), _full((s_len, LANES))]
    out_shape = [_sds((s_len, PAIRS * LANES)), _sds((HEADS, 1, s_len))]
    scratch = [pltpu.VMEM((2, V_ROWS, s_len), v.dtype), pltpu.VMEM((2, 1, t), F32), pltpu.VMEM((2, V_ROWS, t), F32),
               pltpu.VMEM((2, s_len if bias is not None else 8, 1), F32)]
    if n_side:
        ins += list(side[0])
        out_shape += _exchange_out_shapes(*side)
        scratch += _exchange_sems(n_side)
    return pl.pallas_call(
        kern, name=name, grid=(PAIRS,), out_shape=out_shape,
        in_specs=in_specs + [hbm] * n_side, out_specs=[tok(0), rowb] + [hbm] * n_side,
        scratch_shapes=scratch,
        compiler_params=_params("arbitrary", side_effects=bool(n_side)),
    )(*ins)


def _attention_bwd(q, q_blk, k, k_blk, v, v_blk, do, pack, ck_row, rope, chunk_mask, q_scale, k_scale, name,
                   side=None):
    s_len = q.shape[0]
    t = min(ATTN_TILE, s_len // 2)
    nq = s_len // t
    has_bias = ck_row is not None
    nv = 2 if rope is not None else 1
    n_side = len(side[0]) if side else 0

    def kern(*refs):
        q_ref, k_ref, v_ref, do_ref, pack_ref = refs[:5]
        pos = 5
        if has_bias:
            ck_ref = refs[pos]
            pos += 1
        if rope is not None:
            qr_ref, kr_ref = refs[pos:pos + 2]
            pos += 2
        side_in = refs[pos:pos + n_side]
        pos += n_side
        dq_ref, dk_ref, dv_ref = refs[pos:pos + 3]
        pos += 3
        if has_bias:
            dcq_ref, dck_ref = refs[pos:pos + 2]
            pos += 2
        if rope is not None:
            dqr_ref, dkr_ref = refs[pos:pos + 2]
            pos += 2
        side_out = refs[pos:pos + n_side]
        pos += n_side
        qt_scr, dot_scr, dkt_scr, dvt_scr, dq_scr, dcq_scr = refs[pos:pos + 6]
        sems = refs[pos + 6:]
        pj = pl.program_id(0)
        if n_side:
            @pl.when(pj == 0)
            def _():
                _exchange_start(_exchange_copies(side_in, side_out, side[1], *sems, recv=False))

        for i in range(nq):
            sl = slice(i * t, (i + 1) * t)
            dot_scr[:, sl] = do_ref[sl, :].T
            if rope is not None:
                for hh in range(2):
                    qt_scr[hh, :, sl] = _head_operand(q_ref[sl, :], hh, qr_ref[sl, :]).T
            else:
                qt_scr[0, :, sl] = q_ref[sl, :].T
        dkt_scr[...] = jnp.zeros(dkt_scr.shape, F32)
        dvt_scr[...] = jnp.zeros(dvt_scr.shape, F32)
        if has_bias:
            dck_ref[...] = jnp.zeros(dck_ref.shape, F32)

            @pl.when(pj == 0)
            def _():
                dcq_ref[...] = jnp.zeros(dcq_ref.shape, F32)

        def qbody(qi, _):
            qs = pl.multiple_of(qi * t, t)
            qt = q_ref[pl.ds(qs, t), :]
            qrt = qr_ref[pl.ds(qs, t), :] if rope is not None else None
            dot = do_ref[pl.ds(qs, t), :]
            pk = pack_ref[pl.ds(qs, t), :]
            lane = lax.broadcasted_iota(jnp.int32, pk.shape, 1)
            qh = [_head_operand(qt, hh, qrt) for hh in range(2)]
            doh = [_head_operand(dot, hh) for hh in range(2)]
            a_col = [jnp.sum(jnp.where(lane == 2 * pj + hh, pk, 0.0), axis=1, keepdims=True) for hh in range(2)]
            d_col = [jnp.sum(jnp.where(lane == HEADS + 2 * pj + hh, pk, 0.0), axis=1, keepdims=True)
                     for hh in range(2)]
            dq_scr[...] = jnp.zeros(dq_scr.shape, F32)
            if has_bias:
                dcq_scr[...] = jnp.zeros(dcq_scr.shape, F32)

            def block(ks, kn, q0, qn, masked):
                kt = k_ref[pl.ds(ks, kn), :]
                krt = kr_ref[pl.ds(ks, kn), :] if rope is not None else None
                kh = [_head_operand(kt, hh, krt) for hh in range(2)]
                vt = v_ref[pl.ds(ks, kn), :]
                qr_ = slice(q0, q0 + qn)
                qcols = pl.ds(pl.multiple_of(qs + q0, t // 2), qn)
                ss = [lax.dot_general(qh[hh][qr_], kh[hh] if rope is not None else kt, _NT,
                                      preferred_element_type=F32) + a_col[hh][qr_] for hh in range(2)]
                if has_bias:
                    ss = [ss[hh] - ck_ref[hh, :, pl.ds(ks, kn)] for hh in range(2)]
                dpds = [lax.dot_general(doh[hh][qr_], vt, _NT, preferred_element_type=F32) for hh in range(2)]
                ps = [jnp.exp2(s) for s in ss]
                if masked:
                    ps = [jnp.where(_block_mask(kn, qn, q0, chunk_mask, False), p, 0.0) for p in ps]
                dss = [ps[hh] * (dpds[hh] - d_col[hh][qr_]) for hh in range(2)]
                for hh in range(2):
                    rows = slice(hh * HEAD_DIM, (hh + 1) * HEAD_DIM)
                    dsb = dss[hh].astype(MXU)
                    dvt_scr[rows, pl.ds(ks, kn)] += jnp.dot(dot_scr[rows, qcols], ps[hh].astype(MXU),
                                                            preferred_element_type=F32)
                    if rope is not None:
                        dkt_scr[hh, :, pl.ds(ks, kn)] += jnp.dot(qt_scr[hh, :, qcols], dsb,
                                                                 preferred_element_type=F32)
                    else:
                        dkt_scr[0, rows, pl.ds(ks, kn)] += jnp.dot(qt_scr[0, rows, qcols], dsb,
                                                                   preferred_element_type=F32)
                    dq_scr[hh if rope is not None else 0, qr_, :] += jnp.dot(dsb, kh[hh], preferred_element_type=F32)
                    if has_bias:
                        dcq_scr[hh, qr_, :] += jnp.sum(dss[hh], axis=1, keepdims=True)
                        dck_ref[hh, :, pl.ds(ks, kn)] += -jnp.sum(dss[hh], axis=0, keepdims=True)

            def loop_body(ki, carry):
                block(pl.multiple_of(ki * t, t), t, 0, t, False)
                return carry

            lax.fori_loop(0, qi, loop_body, 0)
            block(qs, t // 2, 0, t // 2, True)
            block(qs, t, t // 2, t // 2, True)
            if rope is not None:
                first = lane < HEAD_DIM
                dq_ref[pl.ds(qs, t), :] = (jnp.where(first, dq_scr[0], dq_scr[1]) * q_scale).astype(dq_ref.dtype)
                dqr_ref[pl.ds(qs, t), :] = jnp.where(first, dq_scr[1], dq_scr[0]) * q_scale
            else:
                dq_ref[pl.ds(qs, t), :] = (dq_scr[0] * q_scale).astype(dq_ref.dtype)
            if has_bias:
                old = dcq_ref[pl.ds(qs, t), :]
                dcq_ref[pl.ds(qs, t), :] = jnp.where(lane == 2 * pj, dcq_scr[0],
                                                     jnp.where(lane == 2 * pj + 1, dcq_scr[1], old))
            return 0

        lax.fori_loop(0, nq, qbody, 0)
        for i in range(nq):
            sl = slice(i * t, (i + 1) * t)
            dv_ref[sl, :] = dvt_scr[:, sl].T.astype(dv_ref.dtype)
            if rope is not None:
                d0, d1 = dkt_scr[0, :, sl], dkt_scr[1, :, sl]
                first = lax.broadcasted_iota(jnp.int32, d0.shape, 0) < HEAD_DIM
                dk_ref[sl, :] = (jnp.where(first, d0, d1).T * k_scale).astype(dk_ref.dtype)
                dkr_ref[0, sl, :] = jnp.where(first, d1, d0).T * k_scale
            else:
                dk_ref[sl, :] = (dkt_scr[0, :, sl].T * k_scale).astype(dk_ref.dtype)
        if n_side:
            @pl.when(pj == PAIRS - 1)
            def _():
                _exchange_wait(_exchange_copies(side_in, side_out, side[1], *sems))

    def tok(blk):
        return pl.BlockSpec((s_len, LANES), lambda j: (0, blk + j))

    shared = _full((s_len, LANES))
    rowb = pl.BlockSpec((2, 1, s_len), lambda j: (j, 0, 0))
    slab = pl.BlockSpec((1, s_len, LANES), lambda j: (j, 0, 0))
    hbm = pl.BlockSpec(memory_space=pl.ANY)
    ins = [q, k, v, do, pack]
    in_specs = [tok(q_blk), tok(k_blk), tok(v_blk), tok(0), shared]
    out_shape = [_sds((s_len, PAIRS * LANES), MXU)] * 3
    out_specs = [tok(0)] * 3
    if has_bias:
        ins.append(ck_row)
        in_specs.append(rowb)
        out_shape += [_sds((s_len, LANES)), _sds((HEADS, 1, s_len))]
        out_specs += [shared, rowb]
    if rope is not None:
        ins += list(rope)
        in_specs += [tok(0), shared]
        out_shape += [_sds((s_len, PAIRS * LANES)), _sds((PAIRS, s_len, LANES))]
        out_specs += [tok(0), slab]
    scratch = [pltpu.VMEM((nv, LANES, s_len), q.dtype), pltpu.VMEM((LANES, s_len), do.dtype),
               pltpu.VMEM((nv, LANES, s_len), F32), pltpu.VMEM((LANES, s_len), F32),
               pltpu.VMEM((nv, t, LANES), F32), pltpu.VMEM((2, t, 1), F32)]
    if n_side:
        ins += list(side[0])
        out_shape += _exchange_out_shapes(*side)
        scratch += _exchange_sems(n_side)
    return pl.pallas_call(
        kern, name=name, grid=(PAIRS,), out_shape=out_shape,
        in_specs=in_specs + [hbm] * n_side, out_specs=out_specs + [hbm] * n_side, scratch_shapes=scratch,
        compiler_params=_params("arbitrary", side_effects=bool(n_side)),
    )(*ins)


def _silu(a):
    return a * jax.nn.sigmoid(a)


def _k_out(of, om, gates, x, gate, wout, tm=256):
    s_len, d = x.shape

    def kern(of_ref, om_ref, gates_ref, x_ref, gate_ref, w_ref, xo_ref, y_ref, u_ref):
        u_ref[:, :GROUP_W] = (of_ref[...] * _silu(gates_ref[:, :GROUP_W])).astype(MXU)
        u_ref[:, GROUP_W:] = (om_ref[...] * _silu(gates_ref[:, GROUP_W:])).astype(MXU)
        y = jnp.dot(u_ref[...], w_ref[...], preferred_element_type=F32)
        y_ref[...] = y.astype(y_ref.dtype)
        xo_ref[...] = x_ref[...] + gate_ref[...] * y

    return pl.pallas_call(
        kern, name="k_out", grid=(s_len // tm,),
        out_shape=[_sds((s_len, d)), _sds((s_len, d), MXU), _sds((s_len, 2 * GROUP_W), MXU)],
        in_specs=[_rows(tm, GROUP_W), _rows(tm, GROUP_W), _rows(tm, 2 * GROUP_W), _rows(tm, d), _full((1, d)),
                  _full((2 * GROUP_W, d))],
        out_specs=[_rows(tm, d), _rows(tm, d), _rows(tm, 2 * GROUP_W)],
        compiler_params=_params("arbitrary"),
    )(of, om, gates, x, gate, wout)


def _k_loss(x, gf, tgt, tm=256):
    s_len, d = x.shape

    def kern(x_ref, g_ref, t_ref, loss_ref, dx_ref, dg_ref):
        i = pl.program_id(0)
        xv = x_ref[...]
        r = lax.rsqrt(jnp.mean(xv * xv, axis=-1, keepdims=True) + EPS)
        xh = xv * r
        diff = xh * g_ref[...] - t_ref[...]
        part = 0.5 * jnp.sum(jnp.mean(diff * diff, axis=-1, keepdims=True))
        dout = diff * (1.0 / d)
        dxh = dout * g_ref[...]
        dx_ref[...] = r * (dxh - xh * jnp.mean(dxh * xh, axis=-1, keepdims=True))

        @pl.when(i == 0)
        def _():
            loss_ref[...] = jnp.zeros_like(loss_ref)
            dg_ref[...] = jnp.zeros_like(dg_ref)

        loss_ref[...] += jnp.full(loss_ref.shape, part, F32)
        dg_ref[...] += jnp.sum(dout * xh, axis=0, keepdims=True)

    return pl.pallas_call(
        kern, name="k_loss", grid=(s_len // tm,),
        out_shape=[_sds((1, LANES)), _sds((s_len, d)), _sds((1, d))],
        in_specs=[_rows(tm, d), _full((1, d)), _rows(tm, d)],
        out_specs=[_full((1, LANES)), _rows(tm, d), _full((1, d))],
        compiler_params=_params("arbitrary"),
    )(x, gf, tgt)


def _kb_out(dxo, y, u, gate, wout, of, om, gates, dw_dtype, tm=512):
    s_len, d = dxo.shape
    steps = s_len // tm

    def kern(dxo_ref, y_ref, u_ref, gate_ref, wt_ref, of_ref, om_ref, gates_ref,
             dof_ref, dom_ref, dfg_ref, dmg_ref, dlf_ref, dlm_ref, dgate_ref, dw_ref, dw_acc):
        i = pl.program_id(0)
        dxv = dxo_ref[...]

        @pl.when(i == 0)
        def _():
            dgate_ref[...] = jnp.zeros_like(dgate_ref)
            dw_acc[...] = jnp.zeros_like(dw_acc)

        dgate_ref[...] += jnp.sum(dxv * y_ref[...], axis=0, keepdims=True)
        dyb = (dxv * gate_ref[...]).astype(MXU)
        dw_acc[...] += lax.dot_general(u_ref[...], dyb, _TN, preferred_element_type=F32)

        @pl.when(i == steps - 1)
        def _():
            dw_ref[...] = dw_acc[...].astype(dw_ref.dtype)

        du = lax.dot_general(dyb, wt_ref[...], _NT, preferred_element_type=F32)
        head_of = (lax.broadcasted_iota(jnp.int32, (GROUP_W, LANES), 0) // HEAD_DIM
                   == lax.broadcasted_iota(jnp.int32, (GROUP_W, LANES), 1)).astype(F32)
        for du_g, o_ref, a, do_ref, dg_ref, dl_ref in (
                (du[:, :GROUP_W], of_ref, gates_ref[:, :GROUP_W], dof_ref, dfg_ref, dlf_ref),
                (du[:, GROUP_W:], om_ref, gates_ref[:, GROUP_W:], dom_ref, dmg_ref, dlm_ref)):
            sg = jax.nn.sigmoid(a)
            ov = o_ref[...]
            dov = du_g * (a * sg)
            do_ref[...] = dov.astype(MXU)
            dg_ref[...] = (du_g * ov * (sg * (1.0 + a * (1.0 - sg)))).astype(MXU)
            dl_ref[...] = jnp.dot(dov * ov, head_of, precision=lax.Precision.HIGH, preferred_element_type=F32)

    return pl.pallas_call(
        kern, name="kb_out", grid=(steps,),
        out_shape=[_sds((s_len, GROUP_W), MXU), _sds((s_len, GROUP_W), MXU),
                   _sds((s_len, GROUP_W), MXU), _sds((s_len, GROUP_W), MXU), _sds((s_len, LANES)),
                   _sds((s_len, LANES)), _sds((1, d)), _sds((2 * GROUP_W, d), dw_dtype)],
        in_specs=[_rows(tm, d), _rows(tm, d), _rows(tm, 2 * GROUP_W), _full((1, d)), _full((2 * GROUP_W, d)),
                  _rows(tm, GROUP_W), _rows(tm, GROUP_W), _rows(tm, 2 * GROUP_W)],
        out_specs=[_rows(tm, GROUP_W), _rows(tm, GROUP_W), _rows(tm, GROUP_W),
                   _rows(tm, GROUP_W), _rows(tm, LANES), _rows(tm, LANES), _full((1, d)), _full((2 * GROUP_W, d))],
        scratch_shapes=[pltpu.VMEM((2 * GROUP_W, d), F32)],
        compiler_params=_params("arbitrary"),
    )(dxo, y, u, gate, wout, of, om, gates)


def _kb_prep(dqn, dqr, dkn, dv, dkr, dff, tail, qn, kvn, cos, sin, gq, gkv, wuq_t, wukv_t, dw_dtype, tm=512):
    s_len = tail.shape[0]
    qw = 2 * GROUP_W
    steps = s_len // tm

    def kern(dqn_ref, dqr_ref, dkn_ref, dv_ref, dkr_ref, dff_ref, tail_ref, qn_ref, kvn_ref, cos_ref, sin_ref,
             gq_ref, gkv_ref, wuqt_ref, wukvt_ref, dz_ref, dgq_ref, dgkv_ref, dwuq_ref, dwukv_ref,
             dq_ref, uq_acc, ukv_acc):
        i = pl.program_id(0)

        @pl.when(i == 0)
        def _():
            dgq_ref[...] = jnp.zeros_like(dgq_ref)
            dgkv_ref[...] = jnp.zeros_like(dgkv_ref)
            uq_acc[...] = jnp.zeros_like(uq_acc)
            ukv_acc[...] = jnp.zeros_like(ukv_acc)

        cs, sn = cos_ref[...], sin_ref[...]
        dq_ref[:, :GROUP_W] = dqn_ref[...]
        for blk in range(PAIRS):
            sl = slice(blk * LANES, (blk + 1) * LANES)
            dq_ref[:, GROUP_W + blk * LANES:GROUP_W + (blk + 1) * LANES] = _rope_bwd(dqr_ref[:, sl], cs, sn).astype(MXU)
        dqn = lax.dot_general(dq_ref[...], wuqt_ref[...], _NT, preferred_element_type=F32)
        uq_acc[...] += lax.dot_general(qn_ref[...], dq_ref[...], _TN, preferred_element_type=F32)
        ukv_acc[:, :GROUP_W] += lax.dot_general(kvn_ref[...], dkn_ref[...], _TN, preferred_element_type=F32)
        ukv_acc[:, GROUP_W:] += lax.dot_general(kvn_ref[...], dv_ref[...], _TN, preferred_element_type=F32)

        @pl.when(i == steps - 1)
        def _():
            dwuq_ref[...] = uq_acc[...].astype(dwuq_ref.dtype)
            dwukv_ref[...] = ukv_acc[...].astype(dwukv_ref.dtype)

        ql = tail_ref[:, :T_KV]
        rq = lax.rsqrt(jnp.mean(ql * ql, axis=-1, keepdims=True) + EPS)
        qh = ql * rq
        dgq_ref[...] += jnp.sum(dqn * qh, axis=0, keepdims=True)
        dqh = dqn * gq_ref[...]
        dz_ref[:, :Q_LORA] = (rq * (dqh - qh * jnp.mean(dqh * qh, axis=-1, keepdims=True))).astype(MXU)

        dkvn = (lax.dot_general(dkn_ref[...], wukvt_ref[:, :GROUP_W], _NT, preferred_element_type=F32)
                + lax.dot_general(dv_ref[...], wukvt_ref[:, GROUP_W:], _NT, preferred_element_type=F32))
        kvl = tail_ref[:, T_KV:T_MISC]
        rk = lax.rsqrt(jnp.mean(kvl * kvl, axis=-1, keepdims=True) + EPS)
        kh = kvl * rk
        dgkv_ref[...] += jnp.sum(dkvn * kh, axis=0, keepdims=True)
        dkh = dkvn * gkv_ref[...]
        dz_ref[:, Q_LORA:Q_LORA + KV_LORA] = (
            rk * (dkh - kh * jnp.mean(dkh * kh, axis=-1, keepdims=True))).astype(MXU)

        g = dkr_ref[...] + pltpu.roll(dkr_ref[...], HEAD_DIM, 1)
        lane = lax.broadcasted_iota(jnp.int32, g.shape, 1)
        dmisc = jnp.where(lane < ROPE, _rope_bwd(g, cs, sn), 0.0) + dff_ref[...]
        dz_ref[:, Q_LORA + KV_LORA:] = dmisc.astype(MXU)

    return pl.pallas_call(
        kern, name="kb_prep", grid=(steps,),
        out_shape=[_sds((s_len, TAIL_W), MXU), _sds((1, Q_LORA)), _sds((1, KV_LORA)),
                   _sds((Q_LORA, qw), dw_dtype), _sds((KV_LORA, 2 * GROUP_W), dw_dtype)],
        in_specs=[_rows(tm, GROUP_W), _rows(tm, GROUP_W), _rows(tm, GROUP_W), _rows(tm, GROUP_W), _rows(tm, LANES),
                  _rows(tm, LANES), _rows(tm, TAIL_W), _rows(tm, Q_LORA), _rows(tm, KV_LORA),
                  _rows(tm, LANES), _rows(tm, LANES), _full((1, Q_LORA)), _full((1, KV_LORA)),
                  _full((Q_LORA, qw)), _full((KV_LORA, 2 * GROUP_W))],
        out_specs=[_rows(tm, TAIL_W), _full((1, Q_LORA)), _full((1, KV_LORA)), _full((Q_LORA, qw)),
                   _full((KV_LORA, 2 * GROUP_W))],
        scratch_shapes=[pltpu.VMEM((tm, qw), MXU), pltpu.VMEM((Q_LORA, qw), F32),
                        pltpu.VMEM((KV_LORA, 2 * GROUP_W), F32)],
        compiler_params=_params("arbitrary"),
    )(dqn, dqr, dkn, dv, dkr, dff, tail, qn, kvn, cos, sin, gq, gkv, wuq_t, wukv_t)


def _kb_in(dz_pieces, w, h, x, g, mod3, dxo, dw_dtype, tm=512):
    s_len, d = x.shape
    widths = [p.shape[1] for p in dz_pieces]
    n_p = len(widths)
    steps = s_len // tm

    def kern(*refs):
        dz_refs = refs[:n_p]
        w_ref, h_ref, x_ref, g_ref, mod_ref, dxo_ref, dx_ref, acc_ref, dw_ref, dw_acc = refs[n_p:]
        i = pl.program_id(0)

        @pl.when(i == 0)
        def _():
            acc_ref[...] = jnp.zeros_like(acc_ref)
            dw_acc[...] = jnp.zeros_like(dw_acc)

        dh = jnp.zeros((tm, d), F32)
        ht = h_ref[...]
        lo = 0
        for p_ref, wd in zip(dz_refs, widths):
            dh = dh + lax.dot_general(p_ref[...], w_ref[:, lo:lo + wd], _NT, preferred_element_type=F32)
            dw_acc[:, lo:lo + wd] += lax.dot_general(ht, p_ref[...], _TN, preferred_element_type=F32)
            lo += wd

        @pl.when(i == steps - 1)
        def _():
            dw_ref[...] = dw_acc[...].astype(dw_ref.dtype)

        xv = x_ref[...]
        r = lax.rsqrt(jnp.mean(xv * xv, axis=-1, keepdims=True) + EPS)
        xh = xv * r
        xn = xh * g_ref[...]
        dxn = dh * (1.0 + mod_ref[1:2, :])
        acc_ref[0:1, :] += jnp.sum(dh, axis=0, keepdims=True)
        acc_ref[1:2, :] += jnp.sum(dh * xn, axis=0, keepdims=True)
        acc_ref[2:3, :] += jnp.sum(dxn * xh, axis=0, keepdims=True)
        dxh = dxn * g_ref[...]
        dx_ref[...] = dxo_ref[...] + r * (dxh - xh * jnp.mean(dxh * xh, axis=-1, keepdims=True))

    return pl.pallas_call(
        kern, name="kb_in", grid=(steps,),
        out_shape=[_sds((s_len, d)), _sds((3, d)), _sds((d, Z_W), dw_dtype)],
        in_specs=[_rows(tm, wd) for wd in widths] + [_full((d, Z_W)), _rows(tm, d), _rows(tm, d), _full((1, d)),
                                                     _full((3, d)), _rows(tm, d)],
        out_specs=[_rows(tm, d), _full((3, d)), _full((d, Z_W))],
        scratch_shapes=[pltpu.VMEM((d, Z_W), F32)],
        compiler_params=_params("arbitrary"),
    )(*dz_pieces, w, h, x, g, mod3, dxo)


def _adamw(slabs, w, m, v, name):
    n_l = len(slabs)
    n, r, c = slabs[0].shape
    tm = r
    for cand in (256, 128, 64, 32, 16, 8):
        if r % cand == 0:
            tm = cand
            break
    steps = r // tm

    def kern(*refs):
        g_refs = refs[:n_l]
        w_ref, m_ref, v_ref, go_ref, d_ref, mo_ref, vo_ref, g_scr = refs[n_l:]
        for ll in range(n_l):
            @pl.when(pl.program_id(0) == ll)
            def _(g_ref=g_refs[ll]):
                g = g_ref[0].astype(F32)
                for s in range(1, n):
                    g = g + g_ref[s].astype(F32)
                g_scr[...] = g

        g = g_scr[...]
        m_new = ADAM_B1 * m_ref[...] + (1.0 - ADAM_B1) * g
        v_new = ADAM_B2 * v_ref[...] + (1.0 - ADAM_B2) * (g * g)
        m_hat = m_new / (1.0 - ADAM_B1 ** ADAM_STEP)
        v_hat = v_new / (1.0 - ADAM_B2 ** ADAM_STEP)
        go_ref[...] = g
        mo_ref[...] = m_new
        vo_ref[...] = v_new
        d_ref[...] = -ADAM_LR * (m_hat / (jnp.sqrt(v_hat) + ADAM_EPS) + ADAM_WD * w_ref[...])

    row = pl.BlockSpec((tm, c), lambda l, i: (l * steps + i, 0))

    def slab_spec(ll):
        return pl.BlockSpec((n, tm, c), lambda l, i: (0, jnp.where(l == ll, i, 0), 0))

    return pl.pallas_call(
        kern, name=name, grid=(n_l, steps), out_shape=[_sds((n_l * r, c))] * 4,
        in_specs=[slab_spec(ll) for ll in range(n_l)] + [row, row, row],
        out_specs=[row] * 4,
        scratch_shapes=[pltpu.VMEM((tm, c), F32)],
        compiler_params=_params("arbitrary", "arbitrary"),
    )(*slabs, w, m, v)


def _perm_w_in(w):
    pad = jnp.zeros(w.shape[:-1] + (Z_W - Z_MISC - ROPE - HEADS,), w.dtype)
    return jnp.concatenate([w[..., 0:1536], w[..., 1544:2056], w[..., 2472:2984], w[..., 2056:2312],
                            w[..., 2312:2440], w[..., 2440:2472], w[..., 1536:1544], pad], axis=-1)


def _unperm_w_in(g):
    ff0 = Z_MISC + MISC_FF
    return jnp.concatenate([g[..., 0:1536], g[..., ff0:ff0 + HEADS], g[..., Z_FG:Z_FG + GROUP_W],
                            g[..., Z_QL:Z_QL + Q_LORA], g[..., Z_KV:Z_KV + KV_LORA],
                            g[..., Z_MISC:Z_MISC + ROPE], g[..., Z_MG:Z_MG + GROUP_W]], axis=-1)


def _perm_w_uq(w):
    lead = w.shape[:-1]
    wh = w.reshape(lead + (PAIRS, 2, NOPE + ROPE))
    zero = jnp.zeros(lead + (PAIRS, HEAD_DIM - ROPE), w.dtype)
    rope = jnp.concatenate([wh[..., 1, NOPE:], zero, wh[..., 0, NOPE:], zero], axis=-1)
    return jnp.concatenate([wh[..., :NOPE].reshape(lead + (GROUP_W,)), rope.reshape(lead + (GROUP_W,))], axis=-1)


def _unperm_w_uq(g):
    lead = g.shape[:-1]
    nope = g[..., :GROUP_W].reshape(lead + (PAIRS, 2, NOPE))
    rp = g[..., GROUP_W:].reshape(lead + (PAIRS, 2, HEAD_DIM))[..., :ROPE]
    return jnp.concatenate([nope, rp[..., ::-1, :]], axis=-1).reshape(lead + (HEADS * (NOPE + ROPE),))


def _perm_w_ukv(w):
    lead = w.shape[:-1]
    wh = w.reshape(lead + (HEADS, 2 * HEAD_DIM))
    return jnp.concatenate([wh[..., :NOPE].reshape(lead + (GROUP_W,)),
                            wh[..., NOPE:].reshape(lead + (GROUP_W,))], axis=-1)


def _unperm_w_ukv(g):
    lead = g.shape[:-1]
    parts = [g[..., :GROUP_W].reshape(lead + (HEADS, NOPE)), g[..., GROUP_W:].reshape(lead + (HEADS, HEAD_DIM))]
    return jnp.concatenate(parts, axis=-1).reshape(lead + (2 * GROUP_W,))


def _rope_tables(positions):
    inv_freq = 1.0 / (ROPE_THETA ** (jnp.arange(0, ROPE, 2, dtype=F32) / ROPE))
    ang = positions.astype(F32)[:, None] * inv_freq
    cos, sin = jnp.cos(ang), jnp.sin(ang)
    reps = LANES // ROPE
    return jnp.tile(jnp.concatenate([cos, cos], axis=1), (1, reps)), jnp.tile(jnp.concatenate([-sin, sin], axis=1), (1, reps))


def _full_weights(g_in, g_uq, g_ukv, g_out):
    def cols(g):
        return g.transpose(1, 0, 2).reshape(g.shape[1], -1)
    return (None if g_in is None else _perm_w_in(cols(g_in)),
            None if g_uq is None else _perm_w_uq(cols(g_uq)),
            None if g_ukv is None else _perm_w_ukv(cols(g_ukv)),
            None if g_out is None else g_out.reshape(-1, g_out.shape[2]))


def _grad_slabs(dw_in, dw_uq, dw_ukv):
    def cols(g):
        return g.reshape(g.shape[0], N_DEV, -1).transpose(1, 0, 2)
    return [cols(_unperm_w_in(dw_in)), cols(_unperm_w_uq(dw_uq)), cols(_unperm_w_ukv(dw_ukv))]


def _local_step(x, mod, positions, loss_target, norm_g, b_f, q_norm_g, kv_norm_g, final_g, weights, shards=None):
    n_l = norm_g.shape[0]
    s_len, d = x.shape
    cos, sin = _rope_tables(positions)
    qb, kb, vb = Z_FQ // LANES, Z_FK // LANES, Z_FV // LANES
    chunks = s_len // LANES
    weights = list(weights)

    def pack_rows(a_rows, delta):
        return jnp.concatenate([a_rows.T, delta[:, :HEADS], jnp.zeros((s_len, LANES - 2 * HEADS), F32)], axis=1)

    saved = []
    for l in range(n_l):
        w_in, w_uq, w_ukv, w_out = weights[l]
        mod3 = mod[l].reshape(3, d)
        h, qkv, gates, tail = _k_in(x, norm_g[l][None], mod3, w_in)
        fft = tail[:, T_MISC + MISC_FF:T_MISC + MISC_FF + HEADS].T.reshape(HEADS * chunks, LANES)
        bf = jnp.repeat(b_f[l], chunks)[:, None]
        c2 = _k_cum(fft, bf, chunks).reshape(HEADS, s_len) * LOG2E
        side_arrs = []
        if shards is not None:
            side_arrs += list(shards[l][1:]) if w_uq is None else []
            side_arrs += list(shards[l + 1]) if l + 1 < n_l else []
        side = (side_arrs, [True] * len(side_arrs)) if side_arrs else None
        ck_lanes = jnp.pad(c2.T, ((0, 0), (0, LANES - HEADS)))
        of, lse_f, *gathered = _attention_fwd(qkv, qb, qkv, kb, qkv, vb, ck_lanes, None, False,
                                              "fox_fwd_gather" if side else "fox_fwd", side)
        if w_uq is None:
            _, w_uq, w_ukv, w_out = _full_weights(None, *gathered[:3])
            weights[l], gathered = (w_in, w_uq, w_ukv, w_out), gathered[3:]
        if gathered:
            weights.append(_full_weights(*gathered))
        mq, mqr, mk, mv, kr2, qn, kvn = _k_prep(tail, cos, sin, q_norm_g[l][None], kv_norm_g[l][None], w_uq, w_ukv)
        om, lse_m = _attention_fwd(mq, 0, mk, 0, mv, 0, None, (mqr, kr2), True, "mla_fwd")
        x_new, y, u = _k_out(of, om, gates, x, mod3[2:3], w_out)
        saved.append((x, gates, tail, h, qkv, fft, bf, c2, lse_f, mq, mqr, mk, mv, kr2, lse_m, of, om, qn, kvn, y, u,
                      mod3))
        x = x_new

    loss_row, dx, dfinal = _k_loss(x, final_g[None], loss_target)

    grads = {k: [] for k in ("norm_g", "mod", "w_in", "b_f", "q_norm_g", "w_uq", "kv_norm_g", "w_ukv", "w_out")}
    received, pending = {}, None
    wg_dtype = MXU if shards is not None else F32
    for l in range(n_l - 1, -1, -1):
        (x_l, gates, tail, h, qkv, fft, bf, c2, lse_f, mq, mqr, mk, mv, kr2, lse_m, of, om, qn, kvn, y, u,
         mod3) = saved[l]
        w_in, w_uq, w_ukv, w_out = weights[l]
        dof, dom, dfg, dmg, dlt_f, dlt_m, dgate, dw_out = _kb_out(dx, y, u, mod3[2:3], w_out, of, om, gates, wg_dtype)

        side = None
        if shards is not None:
            side_arrs = (pending or []) + [dw_out.reshape(N_DEV, -1, dw_out.shape[1])]
            side = (side_arrs, [False] * len(side_arrs))
        dfq, dfk, dfv, dcq, dck, *arrived = _attention_bwd(
            qkv, qb, qkv, kb, qkv, vb, dof, pack_rows(-lse_f.reshape(HEADS, s_len), dlt_f), c2[:, None, :], None,
            False, FOX_SCALE, 1.0 / LOG2E, "fox_bwd_exchange" if pending else "fox_bwd", side)
        if side:
            received[l] = [None, None, None, arrived[-1]]
            if pending:
                received[l + 1][:3] = arrived[:3]
        dcum = (dcq[:, :HEADS].T + dck.reshape(HEADS, s_len)).reshape(HEADS * chunks, LANES)
        dff_rows, dbf_rows = _k_cum_bwd(dcum, fft, bf, chunks)
        dfft = dff_rows.reshape(HEADS, s_len)
        grads["b_f"].append(jnp.sum(dbf_rows[:, 0].reshape(HEADS, chunks), axis=1))

        dmq, dkn, dmv, dqr, dkr_pairs = _attention_bwd(
            mq, 0, mk, 0, mv, 0, dom, pack_rows(-lse_m.reshape(HEADS, s_len), dlt_m), None, (mqr, kr2), True,
            MLA_SCALE, 1.0 / LOG2E, "mla_bwd")
        dkr = dkr_pairs[0] + dkr_pairs[1] + dkr_pairs[2] + dkr_pairs[3]
        dff = jnp.pad(dfft.T, ((0, 0), (MISC_FF, LANES - MISC_FF - HEADS)))
        dz_tail, dgq, dgkv, dw_uq, dw_ukv = _kb_prep(dmq, dqr, dkn, dmv, dkr, dff, tail, qn, kvn, cos, sin,
                                                     q_norm_g[l][None], kv_norm_g[l][None], w_uq, w_ukv, wg_dtype)
        grads["q_norm_g"].append(dgq[0])
        grads["kv_norm_g"].append(dgkv[0])
        dz = [dfq, dfk, dfv, dfg, dmg, dz_tail]
        dx, acc3, dw_in = _kb_in(dz, w_in, h, x_l, norm_g[l][None], mod3, dx, wg_dtype)
        grads["norm_g"].append(acc3[2])
        grads["mod"].append(jnp.concatenate([acc3[0], acc3[1], dgate[0]]))
        if shards is not None:
            pending = _grad_slabs(dw_in, dw_uq, dw_ukv)
        else:
            for name, g in (("w_in", dw_in), ("w_uq", dw_uq), ("w_ukv", dw_ukv), ("w_out", dw_out)):
                grads[name].append(g)
    grads = {k: jnp.stack(v[::-1]) for k, v in grads.items() if v}
    grads["final_g"] = dfinal[0]
    if shards is None:
        return loss_row[0, 0], dx, grads
    return loss_row[0, 0], dx, grads, received, pending


def _pack_small(parts, total):
    flat = jnp.concatenate([p.reshape(-1) for p in parts])
    return jnp.pad(flat, (0, total - flat.shape[0])).reshape(total // LANES, LANES)


def kernel(x, c, positions, norm_g, w_ada, b_ada, w_in, b_f, q_norm_g, w_uq, kv_norm_g, w_ukv, w_out, final_g, loss_target, m_norm_g, m_w_ada, m_b_ada, m_w_in, m_b_f, m_q_norm_g, m_w_uq, m_kv_norm_g, m_w_ukv, m_w_out, m_final_g, v_norm_g, v_w_ada, v_b_ada, v_w_in, v_b_f, v_q_norm_g, v_w_uq, v_kv_norm_g, v_w_ukv, v_w_out, v_final_g):
    n_l, d = norm_g.shape
    me = 4 * lax.axis_index("x") + 2 * lax.axis_index("y") + lax.axis_index("c")
    ada_c = w_ada.shape[2]

    cact = jnp.broadcast_to(jax.nn.silu(c), (N_DEV, d))
    shards = [[w[l].astype(MXU) for w in (w_in, w_uq, w_ukv, w_out)] for l in range(n_l)]
    g_in0, g_cact = _gather_two_level([shards[0][0], cact], "gather_layer0")
    cact_all = g_cact[:, 0, :]

    b_cols = lax.dynamic_slice_in_dim(b_ada, me * ada_c, ada_c, axis=1)[:, None, :]
    modpart = _modpart(cact_all, w_ada, b_cols)
    mod_send = jnp.pad(modpart.transpose(1, 0, 2), ((0, 0), (0, 8 - n_l), (0, 0)))
    (mod_recv,) = _exchange([mod_send], [False], "scatter_mod")
    mod = mod_recv.transpose(1, 0, 2).reshape(8, N_DEV * ada_c)[:n_l]

    loss, dx, gr, received, pending = _local_step(x[0], mod, positions[0], loss_target[0], norm_g, b_f, q_norm_g,
                                                  kv_norm_g, final_g, [_full_weights(g_in0, None, None, None)],
                                                  shards)

    small_parts = [gr["norm_g"], gr["mod"], gr["b_f"], gr["q_norm_g"], gr["kv_norm_g"], gr["final_g"], cact[0],
                   loss.reshape(1)]
    sizes = [int(np.prod(p.shape)) for p in small_parts]
    total = -(-sum(sizes) // 1024) * 1024
    small = _pack_small(small_parts, total)
    *received[0][:3], r_small = _exchange(pending + [small], [False, False, False, True], "exchange_layer0")
    r_in, r_uq, r_ukv, r_out = ([received[l][i] for l in range(n_l)] for i in range(4))

    def upd(slabs, w, m, v, name):
        shp = w.shape
        w2, m2, v2 = (a.reshape(-1, slabs[0].shape[2]) for a in (w, m, v))
        return [o.reshape(shp) for o in _adamw(slabs, w2, m2, v2, name)]

    o_in = upd(r_in, w_in, m_w_in, v_w_in, "adamw_w_in")
    o_uq = upd(r_uq, w_uq, m_w_uq, v_w_uq, "adamw_w_uq")
    o_ukv = upd(r_ukv, w_ukv, m_w_ukv, v_w_ukv, "adamw_w_ukv")
    o_out = upd(r_out, w_out, m_w_out, v_w_out, "adamw_w_out")

    offs = np.cumsum([0] + sizes)
    flat_all = r_small.reshape(N_DEV, total)
    dmod_all = flat_all[:, offs[1]:offs[2]].reshape(N_DEV, n_l, 3 * d)
    dmod_cols = lax.dynamic_slice_in_dim(dmod_all, me * ada_c, ada_c, axis=2).transpose(1, 0, 2)
    cact_cols = flat_all[:, offs[6]:offs[7]][:, :, None]
    g_ada = _ada_grad(cact_cols, dmod_cols)
    o_ada = upd([g_ada.reshape(1, n_l * d, ada_c)], w_ada, m_w_ada, v_w_ada, "adamw_w_ada")

    zero_c = jnp.zeros((d,), F32)
    small_w = [_pack_small([norm_g, b_ada, b_f, q_norm_g, kv_norm_g, final_g, zero_c], total),
               _pack_small([m_norm_g, m_b_ada, m_b_f, m_q_norm_g, m_kv_norm_g, m_final_g, zero_c], total),
               _pack_small([v_norm_g, v_b_ada, v_b_f, v_q_norm_g, v_kv_norm_g, v_final_g, zero_c], total)]
    o_small = [o.reshape(-1) for o in _adamw([r_small], *small_w, "adamw_small")]
    shapes = [norm_g.shape, b_ada.shape, b_f.shape, q_norm_g.shape, kv_norm_g.shape, final_g.shape]

    def small_out(kind, idx):
        return o_small[kind][offs[idx]:offs[idx + 1]].reshape(shapes[idx])

    outs = [o_small[0][offs[7]], dx[None]]
    for kind in range(4):
        outs += [small_out(kind, 0), o_ada[kind], small_out(kind, 1), o_in[kind], small_out(kind, 2),
                 small_out(kind, 3), o_uq[kind], small_out(kind, 4), o_ukv[kind], o_out[kind], small_out(kind, 5)]
    return tuple(outs)
```

```python
import jax
import jax.numpy as jnp
import numpy as np
from jax import lax
from jax.experimental import pallas as pl
from jax.experimental.pallas import tpu as pltpu

F32 = jnp.float32
MXU = jnp.bfloat16

N_DEV = 8
HEADS = 8
PAIRS = HEADS // 2
HEAD_DIM = 64
NOPE = 64
ROPE = 32
HALF_ROPE = ROPE // 2
Q_LORA = 256
KV_LORA = 128
CHUNK = 64
GROUP_W = HEADS * HEAD_DIM
EPS = 1e-6
ROPE_THETA = 10000.0

Z_FQ, Z_FK, Z_FV, Z_FG, Z_MG, Z_QL, Z_KV, Z_MISC, Z_W = 0, 512, 1024, 1536, 2048, 2560, 2816, 2944, 3072
MISC_FF = ROPE
TAIL_W = Z_W - Z_QL
T_KV, T_MISC = Q_LORA, Q_LORA + KV_LORA

ADAM_LR = 0.001
ADAM_B1 = 0.9
ADAM_B2 = 0.999
ADAM_EPS = 1e-08
ADAM_WD = 0.01
ADAM_STEP = 10

VMEM_LIMIT_V7X = 56 * 1024 * 1024
LANES = 128
ATTN_TILE = 1024
V_ROWS = HEAD_DIM + 16
LOG2E = 1.4426950408889634
FOX_SCALE = HEAD_DIM ** -0.5
MLA_SCALE = (NOPE + ROPE) ** -0.5

_NT = (((1,), (1,)), ((), ()))
_TN = (((0,), (0,)), ((), ()))


def _params(*sem, side_effects=False):
    return pltpu.CompilerParams(dimension_semantics=sem, vmem_limit_bytes=VMEM_LIMIT_V7X,
                                has_side_effects=side_effects)


def _sds(shape, dtype=F32):
    return jax.ShapeDtypeStruct(shape, dtype)


def _full(shape):
    nd = len(shape)
    return pl.BlockSpec(shape, lambda *_: (0,) * nd)


def _rows(tm, width, col=0):
    return pl.BlockSpec((tm, width), lambda i: (i, col))


def _exchange(arrs, gather, name):
    n = len(arrs)

    def kern(*refs):
        copies = _exchange_copies(refs[:n], refs[n:2 * n], gather, *refs[2 * n:])
        _exchange_start(copies)
        _exchange_wait(copies)

    return pl.pallas_call(
        kern, name=name, out_shape=_exchange_out_shapes(arrs, gather),
        in_specs=[pl.BlockSpec(memory_space=pl.ANY)] * n,
        out_specs=[pl.BlockSpec(memory_space=pl.ANY)] * n,
        scratch_shapes=_exchange_sems(n),
        compiler_params=pltpu.CompilerParams(has_side_effects=True),
    )(*arrs)


def _gather_two_level(arrs, name):
    n = len(arrs)

    def kern(*refs):
        ins, outs = refs[:n], refs[n:2 * n]
        send_sems, recv_sems, loc_sems = refs[2 * n:]
        x, y, c = lax.axis_index("x"), lax.axis_index("y"), lax.axis_index("c")
        me, sibling = (x, y, c), (x, y, 1 - c)
        chips = [(1 - x, y), (x, 1 - y), (1 - x, 1 - y)]

        def slot(i, dev):
            return outs[i].at[4 * dev[0] + 2 * dev[1] + dev[2]]

        def copy(i, k, block, to, src=None):
            return pltpu.make_async_remote_copy(
                src_ref=slot(i, block) if src is None else src, dst_ref=slot(i, block), send_sem=send_sems.at[i, k],
                recv_sem=recv_sems.at[i, k], device_id=to, device_id_type=pl.DeviceIdType.MESH)

        mine = [pltpu.make_async_copy(ins[i], slot(i, me), loc_sems.at[i]) for i in range(n)]
        first = [copy(i, 0, me, sibling, src=ins[i]) for i in range(n)]
        first += [copy(i, 1 + j, me, (*chip, c), src=ins[i]) for j, chip in enumerate(chips) for i in range(n)]
        for cp in mine + first:
            cp.start()
        passed = []
        for j, chip in enumerate(chips):
            for i in range(n):
                copy(i, 1 + j, (*chip, c), me).wait_recv()
                fwd = copy(i, 4 + j, (*chip, c), sibling)
                fwd.start()
                passed.append(fwd)
        for i in range(n):
            copy(i, 0, sibling, me).wait_recv()
            for j, chip in enumerate(chips):
                copy(i, 4 + j, (*chip, 1 - c), me).wait_recv()
        for cp in first + passed:
            cp.wait_send()
        for cp in mine:
            cp.wait()

    return pl.pallas_call(
        kern, name=name, out_shape=_exchange_out_shapes(arrs, [True] * n),
        in_specs=[pl.BlockSpec(memory_space=pl.ANY)] * n,
        out_specs=[pl.BlockSpec(memory_space=pl.ANY)] * n,
        scratch_shapes=_exchange_sems(n),
        compiler_params=pltpu.CompilerParams(has_side_effects=True),
    )(*arrs)


def _exchange_out_shapes(arrs, gather):
    return [_sds((N_DEV,) + tuple(a.shape) if g else tuple(a.shape), a.dtype) for a, g in zip(arrs, gather)]


def _exchange_sems(n):
    return [pltpu.SemaphoreType.DMA((n, N_DEV)), pltpu.SemaphoreType.DMA((n, N_DEV)), pltpu.SemaphoreType.DMA((n,))]


def _exchange_copies(ins, outs, gather, send_sems, recv_sems, loc_sems, recv=True):
    n = len(ins)
    x, y, c = lax.axis_index("x"), lax.axis_index("y"), lax.axis_index("c")
    me = 4 * x + 2 * y + c

    def src(i, j):
        return ins[i] if gather[i] else ins[i].at[j]

    local = [pltpu.make_async_copy(src(i, me), outs[i].at[me], loc_sems.at[i]) for i in range(n)]
    sends, recvs = [], []
    for k in range(1, N_DEV):
        px = 1 - x if k & 4 else x
        py = 1 - y if k & 2 else y
        pc = 1 - c if k & 1 else c
        p = 4 * px + 2 * py + pc
        for i in range(n):
            sends.append(pltpu.make_async_remote_copy(
                src_ref=src(i, p), dst_ref=outs[i].at[me], send_sem=send_sems.at[i, k],
                recv_sem=recv_sems.at[i, k], device_id=(px, py, pc), device_id_type=pl.DeviceIdType.MESH))
            if recv:
                recvs.append(pltpu.make_async_remote_copy(
                    src_ref=src(i, p), dst_ref=outs[i].at[p], send_sem=send_sems.at[i, k],
                    recv_sem=recv_sems.at[i, k], device_id=(px, py, pc), device_id_type=pl.DeviceIdType.MESH))
    return local, sends, recvs


def _exchange_start(copies):
    local, sends, _ = copies
    for cp in local + sends:
        cp.start()


def _exchange_wait(copies):
    local, sends, recvs = copies
    for cp in recvs:
        cp.wait_recv()
    for cp in sends:
        cp.wait_send()
    for cp in local:
        cp.wait()


def _modpart(cact8, w_ada, b_cols):
    n_l, d, cw = w_ada.shape

    def kern(c_ref, w_ref, b_ref, o_ref):
        o_ref[0] = jnp.dot(c_ref[...].astype(MXU), w_ref[0].astype(MXU), preferred_element_type=F32) + b_ref[0]

    return pl.pallas_call(
        kern, name="modpart", grid=(n_l,), out_shape=_sds((n_l, N_DEV, cw)),
        in_specs=[_full((N_DEV, d)), pl.BlockSpec((1, d, cw), lambda l: (l, 0, 0)),
                  pl.BlockSpec((1, 1, cw), lambda l: (l, 0, 0))],
        out_specs=pl.BlockSpec((1, N_DEV, cw), lambda l: (l, 0, 0)),
        compiler_params=_params("arbitrary"),
    )(cact8, w_ada, b_cols)


def _ada_grad(cact_cols, dmod_cols):
    n_l, _, cw = dmod_cols.shape
    d = cact_cols.shape[1]

    def kern(c_ref, dm_ref, o_ref):
        acc = c_ref[0] * dm_ref[0, 0:1, :]
        for s in range(1, N_DEV):
            acc = acc + c_ref[s] * dm_ref[0, s:s + 1, :]
        o_ref[0] = acc

    return pl.pallas_call(
        kern, name="ada_grad", grid=(n_l,), out_shape=_sds((n_l, d, cw)),
        in_specs=[_full((N_DEV, d, 1)), pl.BlockSpec((1, N_DEV, cw), lambda l: (l, 0, 0))],
        out_specs=pl.BlockSpec((1, d, cw), lambda l: (l, 0, 0)),
        compiler_params=_params("arbitrary"),
    )(cact_cols, dmod_cols)


def _k_in(x, g, mod3, w, tm=256):
    s_len, d = x.shape
    qkv_w = 3 * GROUP_W

    def kern(x_ref, g_ref, mod_ref, w_ref, h_ref, qkv_ref, gates_ref, tail_ref):
        xv = x_ref[...]
        r = lax.rsqrt(jnp.mean(xv * xv, axis=-1, keepdims=True) + EPS)
        xn = xv * r * g_ref[...]
        h = (xn * (1.0 + mod_ref[1:2, :]) + mod_ref[0:1, :]).astype(MXU)
        h_ref[...] = h
        z = jnp.dot(h, w_ref[...], preferred_element_type=F32)
        qkv_ref[:, :GROUP_W] = (z[:, Z_FQ:Z_FQ + GROUP_W] * (FOX_SCALE * LOG2E)).astype(MXU)
        qkv_ref[:, GROUP_W:] = z[:, Z_FK:Z_FK + 2 * GROUP_W].astype(MXU)
        gates_ref[...] = z[:, Z_FG:Z_QL]
        tail_ref[...] = z[:, Z_QL:]

    return pl.pallas_call(
        kern, name="k_in", grid=(s_len // tm,),
        out_shape=[_sds((s_len, d), MXU), _sds((s_len, qkv_w), MXU), _sds((s_len, Z_QL - Z_FG)),
                   _sds((s_len, TAIL_W))],
        in_specs=[_rows(tm, d), _full((1, d)), _full((3, d)), _full((d, Z_W))],
        out_specs=[_rows(tm, d), _rows(tm, qkv_w), _rows(tm, Z_QL - Z_FG), _rows(tm, TAIL_W)],
        compiler_params=_params("arbitrary"),
    )(x, g, mod3, w)


def _scan_matrices(rows, chunks, reverse):
    r_i = lax.broadcasted_iota(jnp.int32, (LANES, LANES), 0)
    c_i = lax.broadcasted_iota(jnp.int32, (LANES, LANES), 1)
    a_i = lax.broadcasted_iota(jnp.int32, (rows, rows), 0)
    b_i = lax.broadcasted_iota(jnp.int32, (rows, rows), 1)
    same_head = (a_i // chunks) == (b_i // chunks)
    if reverse:
        return (r_i >= c_i).astype(F32), (same_head & (b_i > a_i)).astype(F32)
    return (r_i <= c_i).astype(F32), (same_head & (b_i < a_i)).astype(F32)


def _scan_rows(x, inner, outer):
    tot = jnp.broadcast_to(jnp.sum(x, axis=1, keepdims=True), x.shape)
    return (jnp.dot(x, inner, precision=lax.Precision.HIGHEST, preferred_element_type=F32)
            + jnp.dot(outer, tot, precision=lax.Precision.HIGHEST, preferred_element_type=F32))


def _k_cum(ff_rows, b_rows, chunks):
    rows = ff_rows.shape[0]

    def kern(ff_ref, b_ref, cum_ref):
        xc = ff_ref[...] + b_ref[...]
        lf = jnp.minimum(xc, 0.0) - jnp.log(1.0 + jnp.exp(-jnp.abs(xc)))
        cum_ref[...] = _scan_rows(lf, *_scan_matrices(rows, chunks, False))

    return pl.pallas_call(
        kern, name="k_cum", out_shape=_sds((rows, LANES)),
        in_specs=[pl.BlockSpec(memory_space=pltpu.VMEM)] * 2,
        out_specs=pl.BlockSpec(memory_space=pltpu.VMEM),
        compiler_params=_params(),
    )(ff_rows, b_rows)


def _k_cum_bwd(dc_rows, ff_rows, b_rows, chunks):
    rows = ff_rows.shape[0]

    def kern(dc_ref, ff_ref, b_ref, dff_ref, db_ref):
        dlf = _scan_rows(dc_ref[...], *_scan_matrices(rows, chunks, True))
        dff = dlf * jax.nn.sigmoid(-(ff_ref[...] + b_ref[...]))
        dff_ref[...] = dff
        db_ref[...] = jnp.broadcast_to(jnp.sum(dff, axis=1, keepdims=True), dff.shape)

    return pl.pallas_call(
        kern, name="k_cum_bwd", out_shape=[_sds((rows, LANES)), _sds((rows, LANES))],
        in_specs=[pl.BlockSpec(memory_space=pltpu.VMEM)] * 3,
        out_specs=[pl.BlockSpec(memory_space=pltpu.VMEM)] * 2,
        compiler_params=_params(),
    )(dc_rows, ff_rows, b_rows)


def _swap16(t):
    lane = lax.broadcasted_iota(jnp.int32, t.shape, 1)
    return jnp.where(lane % ROPE < HALF_ROPE, pltpu.roll(t, LANES - HALF_ROPE, 1), pltpu.roll(t, HALF_ROPE, 1))


def _rope(t, cos, sin):
    return t * cos + _swap16(t) * sin


def _rope_bwd(dt, cos, sin):
    return dt * cos - _swap16(dt) * sin


def _k_prep(tail, cos, sin, gq, gkv, wuq, wukv, tm=512):
    s_len = tail.shape[0]
    qc = MLA_SCALE * LOG2E

    def kern(tail_ref, cos_ref, sin_ref, gq_ref, gkv_ref, wuq_ref, wukv_ref,
             qn_out, qr_out, kn_out, v_out, kr_out, qn_ref, kvn_ref):
        cs, sn = cos_ref[...], sin_ref[...]
        ql = tail_ref[:, :T_KV]
        rq = lax.rsqrt(jnp.mean(ql * ql, axis=-1, keepdims=True) + EPS)
        qn = (ql * rq * gq_ref[...]).astype(MXU)
        qn_ref[...] = qn
        q = jnp.dot(qn, wuq_ref[...], preferred_element_type=F32)
        qn_out[...] = (q[:, :GROUP_W] * qc).astype(MXU)
        for blk in range(PAIRS):
            lo = GROUP_W + blk * LANES
            qr_out[:, blk * LANES:(blk + 1) * LANES] = (_rope(q[:, lo:lo + LANES], cs, sn) * qc).astype(MXU)
        kvl = tail_ref[:, T_KV:T_MISC]
        rk = lax.rsqrt(jnp.mean(kvl * kvl, axis=-1, keepdims=True) + EPS)
        kvn = (kvl * rk * gkv_ref[...]).astype(MXU)
        kvn_ref[...] = kvn
        kv = jnp.dot(kvn, wukv_ref[...], preferred_element_type=F32)
        kn_out[...] = kv[:, :GROUP_W].astype(MXU)
        v_out[...] = kv[:, GROUP_W:].astype(MXU)
        misc = tail_ref[:, T_MISC:]
        lane = lax.broadcasted_iota(jnp.int32, misc.shape, 1)
        kr = jnp.where(lane < ROPE, _rope(misc, cs, sn), 0.0)
        kr_out[...] = (kr + pltpu.roll(kr, HEAD_DIM, 1)).astype(MXU)

    return pl.pallas_call(
        kern, name="k_prep", grid=(s_len // tm,),
        out_shape=[_sds((s_len, GROUP_W), MXU), _sds((s_len, GROUP_W), MXU), _sds((s_len, GROUP_W), MXU),
                   _sds((s_len, GROUP_W), MXU), _sds((s_len, LANES), MXU), _sds((s_len, Q_LORA), MXU),
                   _sds((s_len, KV_LORA), MXU)],
        in_specs=[_rows(tm, TAIL_W), _rows(tm, LANES), _rows(tm, LANES),
                  _full((1, Q_LORA)), _full((1, KV_LORA)), _full((Q_LORA, 2 * GROUP_W)),
                  _full((KV_LORA, 2 * GROUP_W))],
        out_specs=[_rows(tm, GROUP_W), _rows(tm, GROUP_W), _rows(tm, GROUP_W), _rows(tm, GROUP_W), _rows(tm, LANES),
                   _rows(tm, Q_LORA), _rows(tm, KV_LORA)],
        compiler_params=_params("arbitrary"),
    )(tail, cos, sin, gq, gkv, wuq, wukv)


def _block_mask(kn, qn, q_off, chunk_mask, transposed):
    shape = (kn, qn) if transposed else (qn, kn)
    row = lax.broadcasted_iota(jnp.int32, shape, 0)
    col = lax.broadcasted_iota(jnp.int32, shape, 1)
    qi, ki = (col + q_off, row) if transposed else (row + q_off, col)
    if chunk_mask:
        return (ki // CHUNK) <= (qi // CHUNK)
    return ki <= qi


def _head_operand(x, hh, other=None):
    lane = lax.broadcasted_iota(jnp.int32, x.shape, 1)
    own = (lane >= hh * HEAD_DIM) & (lane < (hh + 1) * HEAD_DIM)
    return jnp.where(own, x, jnp.zeros_like(x) if other is None else other)


def _attention_fwd(q, q_blk, k, k_blk, v, v_blk, bias, rope, chunk_mask, name, side=None):
    s_len = q.shape[0]
    t = min(ATTN_TILE, s_len // 2)
    nq = s_len // t
    n_side = len(side[0]) if side else 0

    def kern(*refs):
        q_ref, k_ref, v_ref = refs[:3]
        pos = 3
        if bias is not None:
            ck_ref = refs[pos]
            pos += 1
        if rope is not None:
            qr_ref, kr_ref = refs[pos:pos + 2]
            pos += 2
        side_in = refs[pos:pos + n_side]
        pos += n_side
        o_ref, lse_ref = refs[pos:pos + 2]
        side_out = refs[pos + 2:pos + 2 + n_side]
        vt_scr, m_scr, acc_scr, ck_scr, s_scr = refs[pos + 2 + n_side:pos + 7 + n_side]
        sems = refs[pos + 7 + n_side:]
        pj = pl.program_id(0)
        if n_side:
            @pl.when(pj == 0)
            def _():
                _exchange_start(_exchange_copies(side_in, side_out, side[1], *sems, recv=False))
        vt_scr[:, HEAD_DIM:, :] = jnp.ones((2, V_ROWS - HEAD_DIM, s_len), vt_scr.dtype)
        for i in range(nq):
            vtt = v_ref[i * t:(i + 1) * t, :].T
            for hh in range(2):
                vt_scr[hh, :HEAD_DIM, i * t:(i + 1) * t] = vtt[hh * HEAD_DIM:(hh + 1) * HEAD_DIM, :]
                if bias is not None:
                    ckt = ck_ref[i * t:(i + 1) * t, :]
                    lane = lax.broadcasted_iota(jnp.int32, ckt.shape, 1)
                    ck_scr[hh, i * t:(i + 1) * t, :] = jnp.sum(jnp.where(lane == 2 * pj + hh, ckt, 0.0), axis=1,
                                                               keepdims=True)

        def qbody(qi, _):
            qs = pl.multiple_of(qi * t, t)
            qt = q_ref[pl.ds(qs, t), :]
            qrt = qr_ref[pl.ds(qs, t), :] if rope is not None else None
            qh = [_head_operand(qt, hh, qrt) for hh in range(2)]
            m_scr[...] = jnp.full(m_scr.shape, -jnp.inf, F32)
            acc_scr[...] = jnp.zeros(acc_scr.shape, F32)

            def logits(ks, slot):
                kt = k_ref[pl.ds(ks, t), :]
                kh = [_head_operand(kt, hh, kr_ref[pl.ds(ks, t), :]) for hh in range(2)] if rope is not None else [kt, kt]
                for hh in range(2):
                    s_scr[slot, hh] = lax.dot_general(kh[hh], qh[hh], _NT, preferred_element_type=F32)

            def consume(ks, slot, masked):
                sts = [s_scr[slot, hh] for hh in range(2)]
                if bias is not None:
                    sts = [sts[hh] - ck_scr[hh, pl.ds(ks, t), :] for hh in range(2)]
                if masked:
                    sts = [jnp.where(_block_mask(t, t, 0, chunk_mask, True), st, -jnp.inf) for st in sts]
                m_old = [m_scr[hh] for hh in range(2)]
                m_new = [jnp.maximum(m_old[hh], jnp.max(sts[hh], axis=0, keepdims=True)) for hh in range(2)]
                pts = [jnp.exp2(sts[hh] - m_new[hh]).astype(MXU) for hh in range(2)]
                for hh in range(2):
                    alpha = jnp.exp2(m_old[hh] - m_new[hh])
                    acc_scr[hh] = alpha * acc_scr[hh] + jnp.dot(vt_scr[hh, :, pl.ds(ks, t)], pts[hh],
                                                                preferred_element_type=F32)
                    m_scr[hh] = m_new[hh]

            logits(0, 0)

            def loop_body(kp, carry):
                k0 = pl.multiple_of(2 * kp * t, t)
                logits(k0 + t, 1)
                consume(k0, 0, False)
                logits(k0 + 2 * t, 0)
                consume(k0 + t, 1, False)
                return carry

            lax.fori_loop(0, qi // 2, loop_body, 0)

            @pl.when(qi % 2 == 0)
            def _():
                consume(qs, 0, True)

            @pl.when(qi % 2 == 1)
            def _():
                logits(qs, 1)
                consume(qs - t, 0, False)
                consume(qs, 1, True)
            outs = []
            for hh in range(2):
                acc = acc_scr[hh]
                l = acc[HEAD_DIM:HEAD_DIM + 1, :]
                outs.append(acc[:HEAD_DIM, :] / l)
                lse_ref[hh, :, pl.ds(qs, t)] = m_scr[hh] + jnp.log2(l)
            o_ref[pl.ds(qs, t), :] = jnp.concatenate(outs, axis=0).T
            return 0

        lax.fori_loop(0, nq, qbody, 0)
        if n_side:
            @pl.when(pj == PAIRS - 1)
            def _():
                _exchange_wait(_exchange_copies(side_in, side_out, side[1], *sems))

    def tok(blk):
        return pl.BlockSpec((s_len, LANES), lambda j: (0, blk + j))

    rowb = pl.BlockSpec((2, 1, s_len), lambda j: (j, 0, 0))
    hbm = pl.BlockSpec(memory_space=pl.ANY)
    ins = [q, k, v]
    in_specs = [tok(q_blk), tok(k_blk), tok(v_blk)]
    if bias is not None:
        ins.append(bias)
        in_specs.append(_full((s_len, LANES)))
    if rope is not None:
        ins += list(rope)
        in_specs += [tok(0), _full((s_len, LANES))]
    out_shape = [_sds((s_len, PAIRS * LANES)), _sds((HEADS, 1, s_len))]
    scratch = [pltpu.VMEM((2, V_ROWS, s_len), v.dtype), pltpu.VMEM((2, 1, t), F32), pltpu.VMEM((2, V_ROWS, t), F32),
               pltpu.VMEM((2, s_len if bias is not None else 8, 1), F32), pltpu.VMEM((2, 2, t, t), F32)]
    if n_side:
        ins += list(side[0])
        out_shape += _exchange_out_shapes(*side)
        scratch += _exchange_sems(n_side)
    return pl.pallas_call(
        kern, name=name, grid=(PAIRS,), out_shape=out_shape,
        in_specs=in_specs + [hbm] * n_side, out_specs=[tok(0), rowb] + [hbm] * n_side,
        scratch_shapes=scratch,
        compiler_params=_params("arbitrary", side_effects=bool(n_side)),
    )(*ins)


def _attention_bwd(q, q_blk, k, k_blk, v, v_blk, do, pack, ck_row, rope, chunk_mask, q_scale, k_scale, name,
                   side=None):
    s_len = q.shape[0]
    t = min(ATTN_TILE, s_len // 2)
    nq = s_len // t
    has_bias = ck_row is not None
    nv = 2 if rope is not None else 1
    n_side = len(side[0]) if side else 0

    def kern(*refs):
        q_ref, k_ref, v_ref, do_ref, pack_ref = refs[:5]
        pos = 5
        if has_bias:
            ck_ref = refs[pos]
            pos += 1
        if rope is not None:
            qr_ref, kr_ref = refs[pos:pos + 2]
            pos += 2
        side_in = refs[pos:pos + n_side]
        pos += n_side
        dq_ref, dk_ref, dv_ref = refs[pos:pos + 3]
        pos += 3
        if has_bias:
            dcq_ref, dck_ref = refs[pos:pos + 2]
            pos += 2
        if rope is not None:
            dqr_ref, dkr_ref = refs[pos:pos + 2]
            pos += 2
        side_out = refs[pos:pos + n_side]
        pos += n_side
        qt_scr, dot_scr, dkt_scr, dvt_scr, dq_scr, dcq_scr = refs[pos:pos + 6]
        sems = refs[pos + 6:]
        pj = pl.program_id(0)
        if n_side:
            @pl.when(pj == 0)
            def _():
                _exchange_start(_exchange_copies(side_in, side_out, side[1], *sems, recv=False))

        for i in range(nq):
            sl = slice(i * t, (i + 1) * t)
            dot_scr[:, sl] = do_ref[sl, :].T
            if rope is not None:
                for hh in range(2):
                    qt_scr[hh, :, sl] = _head_operand(q_ref[sl, :], hh, qr_ref[sl, :]).T
            else:
                qt_scr[0, :, sl] = q_ref[sl, :].T
        dkt_scr[...] = jnp.zeros(dkt_scr.shape, F32)
        dvt_scr[...] = jnp.zeros(dvt_scr.shape, F32)
        if has_bias:
            dck_ref[...] = jnp.zeros(dck_ref.shape, F32)

            @pl.when(pj == 0)
            def _():
                dcq_ref[...] = jnp.zeros(dcq_ref.shape, F32)

        def qbody(qi, _):
            qs = pl.multiple_of(qi * t, t)
            qt = q_ref[pl.ds(qs, t), :]
            qrt = qr_ref[pl.ds(qs, t), :] if rope is not None else None
            dot = do_ref[pl.ds(qs, t), :]
            pk = pack_ref[pl.ds(qs, t), :]
            lane = lax.broadcasted_iota(jnp.int32, pk.shape, 1)
            qh = [_head_operand(qt, hh, qrt) for hh in range(2)]
            doh = [_head_operand(dot, hh) for hh in range(2)]
            a_col = [jnp.sum(jnp.where(lane == 2 * pj + hh, pk, 0.0), axis=1, keepdims=True) for hh in range(2)]
            d_col = [jnp.sum(jnp.where(lane == HEADS + 2 * pj + hh, pk, 0.0), axis=1, keepdims=True)
                     for hh in range(2)]
            dq_scr[...] = jnp.zeros(dq_scr.shape, F32)
            if has_bias:
                dcq_scr[...] = jnp.zeros(dcq_scr.shape, F32)

            def block(ks, kn, q0, qn, masked):
                kt = k_ref[pl.ds(ks, kn), :]
                krt = kr_ref[pl.ds(ks, kn), :] if rope is not None else None
                kh = [_head_operand(kt, hh, krt) for hh in range(2)]
                vt = v_ref[pl.ds(ks, kn), :]
                qr_ = slice(q0, q0 + qn)
                qcols = pl.ds(pl.multiple_of(qs + q0, t // 2), qn)
                ss = [lax.dot_general(qh[hh][qr_], kh[hh] if rope is not None else kt, _NT,
                                      preferred_element_type=F32) + a_col[hh][qr_] for hh in range(2)]
                if has_bias:
                    ss = [ss[hh] - ck_ref[hh, :, pl.ds(ks, kn)] for hh in range(2)]
                dpds = [lax.dot_general(doh[hh][qr_], vt, _NT, preferred_element_type=F32) for hh in range(2)]
                ps = [jnp.exp2(s) for s in ss]
                if masked:
                    ps = [jnp.where(_block_mask(kn, qn, q0, chunk_mask, False), p, 0.0) for p in ps]
                dss = [ps[hh] * (dpds[hh] - d_col[hh][qr_]) for hh in range(2)]
                for hh in range(2):
                    rows = slice(hh * HEAD_DIM, (hh + 1) * HEAD_DIM)
                    dsb = dss[hh].astype(MXU)
                    dvt_scr[rows, pl.ds(ks, kn)] += jnp.dot(dot_scr[rows, qcols], ps[hh].astype(MXU),
                                                            preferred_element_type=F32)
                    if rope is not None:
                        dkt_scr[hh, :, pl.ds(ks, kn)] += jnp.dot(qt_scr[hh, :, qcols], dsb,
                                                                 preferred_element_type=F32)
                    else:
                        dkt_scr[0, rows, pl.ds(ks, kn)] += jnp.dot(qt_scr[0, rows, qcols], dsb,
                                                                   preferred_element_type=F32)
                    dq_scr[hh if rope is not None else 0, qr_, :] += jnp.dot(dsb, kh[hh], preferred_element_type=F32)
                    if has_bias:
                        dcq_scr[hh, qr_, :] += jnp.sum(dss[hh], axis=1, keepdims=True)
                        dck_ref[hh, :, pl.ds(ks, kn)] += -jnp.sum(dss[hh], axis=0, keepdims=True)

            def loop_body(ki, carry):
                block(pl.multiple_of(ki * t, t), t, 0, t, False)
                return carry

            lax.fori_loop(0, qi, loop_body, 0)
            block(qs, t // 2, 0, t // 2, True)
            block(qs, t, t // 2, t // 2, True)
            if rope is not None:
                first = lane < HEAD_DIM
                dq_ref[pl.ds(qs, t), :] = (jnp.where(first, dq_scr[0], dq_scr[1]) * q_scale).astype(dq_ref.dtype)
                dqr_ref[pl.ds(qs, t), :] = jnp.where(first, dq_scr[1], dq_scr[0]) * q_scale
            else:
                dq_ref[pl.ds(qs, t), :] = (dq_scr[0] * q_scale).astype(dq_ref.dtype)
            if has_bias:
                old = dcq_ref[pl.ds(qs, t), :]
                dcq_ref[pl.ds(qs, t), :] = jnp.where(lane == 2 * pj, dcq_scr[0],
                                                     jnp.where(lane == 2 * pj + 1, dcq_scr[1], old))
            return 0

        lax.fori_loop(0, nq, qbody, 0)
        for i in range(nq):
            sl = slice(i * t, (i + 1) * t)
            dv_ref[sl, :] = dvt_scr[:, sl].T.astype(dv_ref.dtype)
            if rope is not None:
                d0, d1 = dkt_scr[0, :, sl], dkt_scr[1, :, sl]
                first = lax.broadcasted_iota(jnp.int32, d0.shape, 0) < HEAD_DIM
                dk_ref[sl, :] = (jnp.where(first, d0, d1).T * k_scale).astype(dk_ref.dtype)
                dkr_ref[0, sl, :] = jnp.where(first, d1, d0).T * k_scale
            else:
                dk_ref[sl, :] = (dkt_scr[0, :, sl].T * k_scale).astype(dk_ref.dtype)
        if n_side:
            @pl.when(pj == PAIRS - 1)
            def _():
                _exchange_wait(_exchange_copies(side_in, side_out, side[1], *sems))

    def tok(blk):
        return pl.BlockSpec((s_len, LANES), lambda j: (0, blk + j))

    shared = _full((s_len, LANES))
    rowb = pl.BlockSpec((2, 1, s_len), lambda j: (j, 0, 0))
    slab = pl.BlockSpec((1, s_len, LANES), lambda j: (j, 0, 0))
    hbm = pl.BlockSpec(memory_space=pl.ANY)
    ins = [q, k, v, do, pack]
    in_specs = [tok(q_blk), tok(k_blk), tok(v_blk), tok(0), shared]
    out_shape = [_sds((s_len, PAIRS * LANES), MXU)] * 3
    out_specs = [tok(0)] * 3
    if has_bias:
        ins.append(ck_row)
        in_specs.append(rowb)
        out_shape += [_sds((s_len, LANES)), _sds((HEADS, 1, s_len))]
        out_specs += [shared, rowb]
    if rope is not None:
        ins += list(rope)
        in_specs += [tok(0), shared]
        out_shape += [_sds((s_len, PAIRS * LANES)), _sds((PAIRS, s_len, LANES))]
        out_specs += [tok(0), slab]
    scratch = [pltpu.VMEM((nv, LANES, s_len), q.dtype), pltpu.VMEM((LANES, s_len), do.dtype),
               pltpu.VMEM((nv, LANES, s_len), F32), pltpu.VMEM((LANES, s_len), F32),
               pltpu.VMEM((nv, t, LANES), F32), pltpu.VMEM((2, t, 1), F32)]
    if n_side:
        ins += list(side[0])
        out_shape += _exchange_out_shapes(*side)
        scratch += _exchange_sems(n_side)
    return pl.pallas_call(
        kern, name=name, grid=(PAIRS,), out_shape=out_shape,
        in_specs=in_specs + [hbm] * n_side, out_specs=out_specs + [hbm] * n_side, scratch_shapes=scratch,
        compiler_params=_params("arbitrary", side_effects=bool(n_side)),
    )(*ins)


def _silu(a):
    return a * jax.nn.sigmoid(a)


def _k_out(of, om, gates, x, gate, wout, tm=256):
    s_len, d = x.shape

    def kern(of_ref, om_ref, gates_ref, x_ref, gate_ref, w_ref, xo_ref, y_ref, u_ref):
        u_ref[:, :GROUP_W] = (of_ref[...] * _silu(gates_ref[:, :GROUP_W])).astype(MXU)
        u_ref[:, GROUP_W:] = (om_ref[...] * _silu(gates_ref[:, GROUP_W:])).astype(MXU)
        y = jnp.dot(u_ref[...], w_ref[...], preferred_element_type=F32)
        y_ref[...] = y.astype(y_ref.dtype)
        xo_ref[...] = x_ref[...] + gate_ref[...] * y

    return pl.pallas_call(
        kern, name="k_out", grid=(s_len // tm,),
        out_shape=[_sds((s_len, d)), _sds((s_len, d), MXU), _sds((s_len, 2 * GROUP_W), MXU)],
        in_specs=[_rows(tm, GROUP_W), _rows(tm, GROUP_W), _rows(tm, 2 * GROUP_W), _rows(tm, d), _full((1, d)),
                  _full((2 * GROUP_W, d))],
        out_specs=[_rows(tm, d), _rows(tm, d), _rows(tm, 2 * GROUP_W)],
        compiler_params=_params("arbitrary"),
    )(of, om, gates, x, gate, wout)


def _k_loss(x, gf, tgt, tm=256):
    s_len, d = x.shape

    def kern(x_ref, g_ref, t_ref, loss_ref, dx_ref, dg_ref):
        i = pl.program_id(0)
        xv = x_ref[...]
        r = lax.rsqrt(jnp.mean(xv * xv, axis=-1, keepdims=True) + EPS)
        xh = xv * r
        diff = xh * g_ref[...] - t_ref[...]
        part = 0.5 * jnp.sum(jnp.mean(diff * diff, axis=-1, keepdims=True))
        dout = diff * (1.0 / d)
        dxh = dout * g_ref[...]
        dx_ref[...] = r * (dxh - xh * jnp.mean(dxh * xh, axis=-1, keepdims=True))

        @pl.when(i == 0)
        def _():
            loss_ref[...] = jnp.zeros_like(loss_ref)
            dg_ref[...] = jnp.zeros_like(dg_ref)

        loss_ref[...] += jnp.full(loss_ref.shape, part, F32)
        dg_ref[...] += jnp.sum(dout * xh, axis=0, keepdims=True)

    return pl.pallas_call(
        kern, name="k_loss", grid=(s_len // tm,),
        out_shape=[_sds((1, LANES)), _sds((s_len, d)), _sds((1, d))],
        in_specs=[_rows(tm, d), _full((1, d)), _rows(tm, d)],
        out_specs=[_full((1, LANES)), _rows(tm, d), _full((1, d))],
        compiler_params=_params("arbitrary"),
    )(x, gf, tgt)


def _kb_out(dxo, y, u, gate, wout, of, om, gates, dw_dtype, tm=512):
    s_len, d = dxo.shape
    steps = s_len // tm

    def kern(dxo_ref, y_ref, u_ref, gate_ref, wt_ref, of_ref, om_ref, gates_ref,
             dof_ref, dom_ref, dfg_ref, dmg_ref, dlf_ref, dlm_ref, dgate_ref, dw_ref, dw_acc):
        i = pl.program_id(0)
        dxv = dxo_ref[...]

        @pl.when(i == 0)
        def _():
            dgate_ref[...] = jnp.zeros_like(dgate_ref)
            dw_acc[...] = jnp.zeros_like(dw_acc)

        dgate_ref[...] += jnp.sum(dxv * y_ref[...], axis=0, keepdims=True)
        dyb = (dxv * gate_ref[...]).astype(MXU)
        dw_acc[...] += lax.dot_general(u_ref[...], dyb, _TN, preferred_element_type=F32)

        @pl.when(i == steps - 1)
        def _():
            dw_ref[...] = dw_acc[...].astype(dw_ref.dtype)

        du = lax.dot_general(dyb, wt_ref[...], _NT, preferred_element_type=F32)
        head_of = (lax.broadcasted_iota(jnp.int32, (GROUP_W, LANES), 0) // HEAD_DIM
                   == lax.broadcasted_iota(jnp.int32, (GROUP_W, LANES), 1)).astype(F32)
        for du_g, o_ref, a, do_ref, dg_ref, dl_ref in (
                (du[:, :GROUP_W], of_ref, gates_ref[:, :GROUP_W], dof_ref, dfg_ref, dlf_ref),
                (du[:, GROUP_W:], om_ref, gates_ref[:, GROUP_W:], dom_ref, dmg_ref, dlm_ref)):
            sg = jax.nn.sigmoid(a)
            ov = o_ref[...]
            dov = du_g * (a * sg)
            do_ref[...] = dov.astype(MXU)
            dg_ref[...] = (du_g * ov * (sg * (1.0 + a * (1.0 - sg)))).astype(MXU)
            dl_ref[...] = jnp.dot(dov * ov, head_of, precision=lax.Precision.HIGH, preferred_element_type=F32)

    return pl.pallas_call(
        kern, name="kb_out", grid=(steps,),
        out_shape=[_sds((s_len, GROUP_W), MXU), _sds((s_len, GROUP_W), MXU),
                   _sds((s_len, GROUP_W), MXU), _sds((s_len, GROUP_W), MXU), _sds((s_len, LANES)),
                   _sds((s_len, LANES)), _sds((1, d)), _sds((2 * GROUP_W, d), dw_dtype)],
        in_specs=[_rows(tm, d), _rows(tm, d), _rows(tm, 2 * GROUP_W), _full((1, d)), _full((2 * GROUP_W, d)),
                  _rows(tm, GROUP_W), _rows(tm, GROUP_W), _rows(tm, 2 * GROUP_W)],
        out_specs=[_rows(tm, GROUP_W), _rows(tm, GROUP_W), _rows(tm, GROUP_W),
                   _rows(tm, GROUP_W), _rows(tm, LANES), _rows(tm, LANES), _full((1, d)), _full((2 * GROUP_W, d))],
        scratch_shapes=[pltpu.VMEM((2 * GROUP_W, d), F32)],
        compiler_params=_params("arbitrary"),
    )(dxo, y, u, gate, wout, of, om, gates)


def _kb_prep(dqn, dqr, dkn, dv, dkr, dff, tail, qn, kvn, cos, sin, gq, gkv, wuq_t, wukv_t, dw_dtype, tm=512):
    s_len = tail.shape[0]
    qw = 2 * GROUP_W
    steps = s_len // tm

    def kern(dqn_ref, dqr_ref, dkn_ref, dv_ref, dkr_ref, dff_ref, tail_ref, qn_ref, kvn_ref, cos_ref, sin_ref,
             gq_ref, gkv_ref, wuqt_ref, wukvt_ref, dz_ref, dgq_ref, dgkv_ref, dwuq_ref, dwukv_ref,
             dq_ref, uq_acc, ukv_acc):
        i = pl.program_id(0)

        @pl.when(i == 0)
        def _():
            dgq_ref[...] = jnp.zeros_like(dgq_ref)
            dgkv_ref[...] = jnp.zeros_like(dgkv_ref)
            uq_acc[...] = jnp.zeros_like(uq_acc)
            ukv_acc[...] = jnp.zeros_like(ukv_acc)

        cs, sn = cos_ref[...], sin_ref[...]
        dq_ref[:, :GROUP_W] = dqn_ref[...]
        for blk in range(PAIRS):
            sl = slice(blk * LANES, (blk + 1) * LANES)
            dq_ref[:, GROUP_W + blk * LANES:GROUP_W + (blk + 1) * LANES] = _rope_bwd(dqr_ref[:, sl], cs, sn).astype(MXU)
        dqn = lax.dot_general(dq_ref[...], wuqt_ref[...], _NT, preferred_element_type=F32)
        uq_acc[...] += lax.dot_general(qn_ref[...], dq_ref[...], _TN, preferred_element_type=F32)
        ukv_acc[:, :GROUP_W] += lax.dot_general(kvn_ref[...], dkn_ref[...], _TN, preferred_element_type=F32)
        ukv_acc[:, GROUP_W:] += lax.dot_general(kvn_ref[...], dv_ref[...], _TN, preferred_element_type=F32)

        @pl.when(i == steps - 1)
        def _():
            dwuq_ref[...] = uq_acc[...].astype(dwuq_ref.dtype)
            dwukv_ref[...] = ukv_acc[...].astype(dwukv_ref.dtype)

        ql = tail_ref[:, :T_KV]
        rq = lax.rsqrt(jnp.mean(ql * ql, axis=-1, keepdims=True) + EPS)
        qh = ql * rq
        dgq_ref[...] += jnp.sum(dqn * qh, axis=0, keepdims=True)
        dqh = dqn * gq_ref[...]
        dz_ref[:, :Q_LORA] = (rq * (dqh - qh * jnp.mean(dqh * qh, axis=-1, keepdims=True))).astype(MXU)

        dkvn = (lax.dot_general(dkn_ref[...], wukvt_ref[:, :GROUP_W], _NT, preferred_element_type=F32)
                + lax.dot_general(dv_ref[...], wukvt_ref[:, GROUP_W:], _NT, preferred_element_type=F32))
        kvl = tail_ref[:, T_KV:T_MISC]
        rk = lax.rsqrt(jnp.mean(kvl * kvl, axis=-1, keepdims=True) + EPS)
        kh = kvl * rk
        dgkv_ref[...] += jnp.sum(dkvn * kh, axis=0, keepdims=True)
        dkh = dkvn * gkv_ref[...]
        dz_ref[:, Q_LORA:Q_LORA + KV_LORA] = (
            rk * (dkh - kh * jnp.mean(dkh * kh, axis=-1, keepdims=True))).astype(MXU)

        g = dkr_ref[...] + pltpu.roll(dkr_ref[...], HEAD_DIM, 1)
        lane = lax.broadcasted_iota(jnp.int32, g.shape, 1)
        dmisc = jnp.where(lane < ROPE, _rope_bwd(g, cs, sn), 0.0) + dff_ref[...]
        dz_ref[:, Q_LORA + KV_LORA:] = dmisc.astype(MXU)

    return pl.pallas_call(
        kern, name="kb_prep", grid=(steps,),
        out_shape=[_sds((s_len, TAIL_W), MXU), _sds((1, Q_LORA)), _sds((1, KV_LORA)),
                   _sds((Q_LORA, qw), dw_dtype), _sds((KV_LORA, 2 * GROUP_W), dw_dtype)],
        in_specs=[_rows(tm, GROUP_W), _rows(tm, GROUP_W), _rows(tm, GROUP_W), _rows(tm, GROUP_W), _rows(tm, LANES),
                  _rows(tm, LANES), _rows(tm, TAIL_W), _rows(tm, Q_LORA), _rows(tm, KV_LORA),
                  _rows(tm, LANES), _rows(tm, LANES), _full((1, Q_LORA)), _full((1, KV_LORA)),
                  _full((Q_LORA, qw)), _full((KV_LORA, 2 * GROUP_W))],
        out_specs=[_rows(tm, TAIL_W), _full((1, Q_LORA)), _full((1, KV_LORA)), _full((Q_LORA, qw)),
                   _full((KV_LORA, 2 * GROUP_W))],
        scratch_shapes=[pltpu.VMEM((tm, qw), MXU), pltpu.VMEM((Q_LORA, qw), F32),
                        pltpu.VMEM((KV_LORA, 2 * GROUP_W), F32)],
        compiler_params=_params("arbitrary"),
    )(dqn, dqr, dkn, dv, dkr, dff, tail, qn, kvn, cos, sin, gq, gkv, wuq_t, wukv_t)


def _kb_in(dz_pieces, w, h, x, g, mod3, dxo, dw_dtype, tm=512):
    s_len, d = x.shape
    widths = [p.shape[1] for p in dz_pieces]
    n_p = len(widths)
    steps = s_len // tm

    def kern(*refs):
        dz_refs = refs[:n_p]
        w_ref, h_ref, x_ref, g_ref, mod_ref, dxo_ref, dx_ref, acc_ref, dw_ref, dw_acc = refs[n_p:]
        i = pl.program_id(0)

        @pl.when(i == 0)
        def _():
            acc_ref[...] = jnp.zeros_like(acc_ref)
            dw_acc[...] = jnp.zeros_like(dw_acc)

        dh = jnp.zeros((tm, d), F32)
        ht = h_ref[...]
        lo = 0
        for p_ref, wd in zip(dz_refs, widths):
            dh = dh + lax.dot_general(p_ref[...], w_ref[:, lo:lo + wd], _NT, preferred_element_type=F32)
            dw_acc[:, lo:lo + wd] += lax.dot_general(ht, p_ref[...], _TN, preferred_element_type=F32)
            lo += wd

        @pl.when(i == steps - 1)
        def _():
            dw_ref[...] = dw_acc[...].astype(dw_ref.dtype)

        xv = x_ref[...]
        r = lax.rsqrt(jnp.mean(xv * xv, axis=-1, keepdims=True) + EPS)
        xh = xv * r
        xn = xh * g_ref[...]
        dxn = dh * (1.0 + mod_ref[1:2, :])
        acc_ref[0:1, :] += jnp.sum(dh, axis=0, keepdims=True)
        acc_ref[1:2, :] += jnp.sum(dh * xn, axis=0, keepdims=True)
        acc_ref[2:3, :] += jnp.sum(dxn * xh, axis=0, keepdims=True)
        dxh = dxn * g_ref[...]
        dx_ref[...] = dxo_ref[...] + r * (dxh - xh * jnp.mean(dxh * xh, axis=-1, keepdims=True))

    return pl.pallas_call(
        kern, name="kb_in", grid=(steps,),
        out_shape=[_sds((s_len, d)), _sds((3, d)), _sds((d, Z_W), dw_dtype)],
        in_specs=[_rows(tm, wd) for wd in widths] + [_full((d, Z_W)), _rows(tm, d), _rows(tm, d), _full((1, d)),
                                                     _full((3, d)), _rows(tm, d)],
        out_specs=[_rows(tm, d), _full((3, d)), _full((d, Z_W))],
        scratch_shapes=[pltpu.VMEM((d, Z_W), F32)],
        compiler_params=_params("arbitrary"),
    )(*dz_pieces, w, h, x, g, mod3, dxo)


def _adamw(slabs, w, m, v, name):
    n_l = len(slabs)
    n, r, c = slabs[0].shape
    tm = r
    for cand in (256, 128, 64, 32, 16, 8):
        if r % cand == 0:
            tm = cand
            break
    steps = r // tm

    def kern(*refs):
        g_refs = refs[:n_l]
        w_ref, m_ref, v_ref, go_ref, d_ref, mo_ref, vo_ref, g_scr = refs[n_l:]
        for ll in range(n_l):
            @pl.when(pl.program_id(0) == ll)
            def _(g_ref=g_refs[ll]):
                g = g_ref[0].astype(F32)
                for s in range(1, n):
                    g = g + g_ref[s].astype(F32)
                g_scr[...] = g

        g = g_scr[...]
        m_new = ADAM_B1 * m_ref[...] + (1.0 - ADAM_B1) * g
        v_new = ADAM_B2 * v_ref[...] + (1.0 - ADAM_B2) * (g * g)
        m_hat = m_new / (1.0 - ADAM_B1 ** ADAM_STEP)
        v_hat = v_new / (1.0 - ADAM_B2 ** ADAM_STEP)
        go_ref[...] = g
        mo_ref[...] = m_new
        vo_ref[...] = v_new
        d_ref[...] = -ADAM_LR * (m_hat / (jnp.sqrt(v_hat) + ADAM_EPS) + ADAM_WD * w_ref[...])

    row = pl.BlockSpec((tm, c), lambda l, i: (l * steps + i, 0))

    def slab_spec(ll):
        return pl.BlockSpec((n, tm, c), lambda l, i: (0, jnp.where(l == ll, i, 0), 0))

    return pl.pallas_call(
        kern, name=name, grid=(n_l, steps), out_shape=[_sds((n_l * r, c))] * 4,
        in_specs=[slab_spec(ll) for ll in range(n_l)] + [row, row, row],
        out_specs=[row] * 4,
        scratch_shapes=[pltpu.VMEM((tm, c), F32)],
        compiler_params=_params("arbitrary", "arbitrary"),
    )(*slabs, w, m, v)


def _perm_w_in(w):
    pad = jnp.zeros(w.shape[:-1] + (Z_W - Z_MISC - ROPE - HEADS,), w.dtype)
    return jnp.concatenate([w[..., 0:1536], w[..., 1544:2056], w[..., 2472:2984], w[..., 2056:2312],
                            w[..., 2312:2440], w[..., 2440:2472], w[..., 1536:1544], pad], axis=-1)


def _unperm_w_in(g):
    ff0 = Z_MISC + MISC_FF
    return jnp.concatenate([g[..., 0:1536], g[..., ff0:ff0 + HEADS], g[..., Z_FG:Z_FG + GROUP_W],
                            g[..., Z_QL:Z_QL + Q_LORA], g[..., Z_KV:Z_KV + KV_LORA],
                            g[..., Z_MISC:Z_MISC + ROPE], g[..., Z_MG:Z_MG + GROUP_W]], axis=-1)


def _perm_w_uq(w):
    lead = w.shape[:-1]
    wh = w.reshape(lead + (PAIRS, 2, NOPE + ROPE))
    zero = jnp.zeros(lead + (PAIRS, HEAD_DIM - ROPE), w.dtype)
    rope = jnp.concatenate([wh[..., 1, NOPE:], zero, wh[..., 0, NOPE:], zero], axis=-1)
    return jnp.concatenate([wh[..., :NOPE].reshape(lead + (GROUP_W,)), rope.reshape(lead + (GROUP_W,))], axis=-1)


def _unperm_w_uq(g):
    lead = g.shape[:-1]
    nope = g[..., :GROUP_W].reshape(lead + (PAIRS, 2, NOPE))
    rp = g[..., GROUP_W:].reshape(lead + (PAIRS, 2, HEAD_DIM))[..., :ROPE]
    return jnp.concatenate([nope, rp[..., ::-1, :]], axis=-1).reshape(lead + (HEADS * (NOPE + ROPE),))


def _perm_w_ukv(w):
    lead = w.shape[:-1]
    wh = w.reshape(lead + (HEADS, 2 * HEAD_DIM))
    return jnp.concatenate([wh[..., :NOPE].reshape(lead + (GROUP_W,)),
                            wh[..., NOPE:].reshape(lead + (GROUP_W,))], axis=-1)


def _unperm_w_ukv(g):
    lead = g.shape[:-1]
    parts = [g[..., :GROUP_W].reshape(lead + (HEADS, NOPE)), g[..., GROUP_W:].reshape(lead + (HEADS, HEAD_DIM))]
    return jnp.concatenate(parts, axis=-1).reshape(lead + (2 * GROUP_W,))


def _rope_tables(positions):
    inv_freq = 1.0 / (ROPE_THETA ** (jnp.arange(0, ROPE, 2, dtype=F32) / ROPE))
    ang = positions.astype(F32)[:, None] * inv_freq
    cos, sin = jnp.cos(ang), jnp.sin(ang)
    reps = LANES // ROPE
    return jnp.tile(jnp.concatenate([cos, cos], axis=1), (1, reps)), jnp.tile(jnp.concatenate([-sin, sin], axis=1), (1, reps))


def _full_weights(g_in, g_uq, g_ukv, g_out):
    def cols(g):
        return g.transpose(1, 0, 2).reshape(g.shape[1], -1)
    return (None if g_in is None else _perm_w_in(cols(g_in)),
            None if g_uq is None else _perm_w_uq(cols(g_uq)),
            None if g_ukv is None else _perm_w_ukv(cols(g_ukv)),
            None if g_out is None else g_out.reshape(-1, g_out.shape[2]))


def _grad_slabs(dw_in, dw_uq, dw_ukv):
    def cols(g):
        return g.reshape(g.shape[0], N_DEV, -1).transpose(1, 0, 2)
    return [cols(_unperm_w_in(dw_in)), cols(_unperm_w_uq(dw_uq)), cols(_unperm_w_ukv(dw_ukv))]


def _local_step(x, mod, positions, loss_target, norm_g, b_f, q_norm_g, kv_norm_g, final_g, weights, shards=None):
    n_l = norm_g.shape[0]
    s_len, d = x.shape
    cos, sin = _rope_tables(positions)
    qb, kb, vb = Z_FQ // LANES, Z_FK // LANES, Z_FV // LANES
    chunks = s_len // LANES
    weights = list(weights)

    def pack_rows(a_rows, delta):
        return jnp.concatenate([a_rows.T, delta[:, :HEADS], jnp.zeros((s_len, LANES - 2 * HEADS), F32)], axis=1)

    saved = []
    for l in range(n_l):
        w_in, w_uq, w_ukv, w_out = weights[l]
        mod3 = mod[l].reshape(3, d)
        h, qkv, gates, tail = _k_in(x, norm_g[l][None], mod3, w_in)
        fft = tail[:, T_MISC + MISC_FF:T_MISC + MISC_FF + HEADS].T.reshape(HEADS * chunks, LANES)
        bf = jnp.repeat(b_f[l], chunks)[:, None]
        c2 = _k_cum(fft, bf, chunks).reshape(HEADS, s_len) * LOG2E
        side_arrs = []
        if shards is not None:
            side_arrs += list(shards[l][1:]) if w_uq is None else []
            side_arrs += list(shards[l + 1]) if l + 1 < n_l else []
        side = (side_arrs, [True] * len(side_arrs)) if side_arrs else None
        ck_lanes = jnp.pad(c2.T, ((0, 0), (0, LANES - HEADS)))
        of, lse_f, *gathered = _attention_fwd(qkv, qb, qkv, kb, qkv, vb, ck_lanes, None, False,
                                              "fox_fwd_gather" if side else "fox_fwd", side)
        if w_uq is None:
            _, w_uq, w_ukv, w_out = _full_weights(None, *gathered[:3])
            weights[l], gathered = (w_in, w_uq, w_ukv, w_out), gathered[3:]
        if gathered:
            weights.append(_full_weights(*gathered))
        mq, mqr, mk, mv, kr2, qn, kvn = _k_prep(tail, cos, sin, q_norm_g[l][None], kv_norm_g[l][None], w_uq, w_ukv)
        om, lse_m = _attention_fwd(mq, 0, mk, 0, mv, 0, None, (mqr, kr2), True, "mla_fwd")
        x_new, y, u = _k_out(of, om, gates, x, mod3[2:3], w_out)
        saved.append((x, gates, tail, h, qkv, fft, bf, c2, lse_f, mq, mqr, mk, mv, kr2, lse_m, of, om, qn, kvn, y, u,
                      mod3))
        x = x_new

    loss_row, dx, dfinal = _k_loss(x, final_g[None], loss_target)

    grads = {k: [] for k in ("norm_g", "mod", "w_in", "b_f", "q_norm_g", "w_uq", "kv_norm_g", "w_ukv", "w_out")}
    received, pending = {}, None
    wg_dtype = MXU if shards is not None else F32
    for l in range(n_l - 1, -1, -1):
        (x_l, gates, tail, h, qkv, fft, bf, c2, lse_f, mq, mqr, mk, mv, kr2, lse_m, of, om, qn, kvn, y, u,
         mod3) = saved[l]
        w_in, w_uq, w_ukv, w_out = weights[l]
        dof, dom, dfg, dmg, dlt_f, dlt_m, dgate, dw_out = _kb_out(dx, y, u, mod3[2:3], w_out, of, om, gates, wg_dtype)

        side = None
        if shards is not None:
            side_arrs = (pending or []) + [dw_out.reshape(N_DEV, -1, dw_out.shape[1])]
            side = (side_arrs, [False] * len(side_arrs))
        dfq, dfk, dfv, dcq, dck, *arrived = _attention_bwd(
            qkv, qb, qkv, kb, qkv, vb, dof, pack_rows(-lse_f.reshape(HEADS, s_len), dlt_f), c2[:, None, :], None,
            False, FOX_SCALE, 1.0 / LOG2E, "fox_bwd_exchange" if pending else "fox_bwd", side)
        if side:
            received[l] = [None, None, None, arrived[-1]]
            if pending:
                received[l + 1][:3] = arrived[:3]
        dcum = (dcq[:, :HEADS].T + dck.reshape(HEADS, s_len)).reshape(HEADS * chunks, LANES)
        dff_rows, dbf_rows = _k_cum_bwd(dcum, fft, bf, chunks)
        dfft = dff_rows.reshape(HEADS, s_len)
        grads["b_f"].append(jnp.sum(dbf_rows[:, 0].reshape(HEADS, chunks), axis=1))

        dmq, dkn, dmv, dqr, dkr_pairs = _attention_bwd(
            mq, 0, mk, 0, mv, 0, dom, pack_rows(-lse_m.reshape(HEADS, s_len), dlt_m), None, (mqr, kr2), True,
            MLA_SCALE, 1.0 / LOG2E, "mla_bwd")
        dkr = dkr_pairs[0] + dkr_pairs[1] + dkr_pairs[2] + dkr_pairs[3]
        dff = jnp.pad(dfft.T, ((0, 0), (MISC_FF, LANES - MISC_FF - HEADS)))
        dz_tail, dgq, dgkv, dw_uq, dw_ukv = _kb_prep(dmq, dqr, dkn, dmv, dkr, dff, tail, qn, kvn, cos, sin,
                                                     q_norm_g[l][None], kv_norm_g[l][None], w_uq, w_ukv, wg_dtype)
        grads["q_norm_g"].append(dgq[0])
        grads["kv_norm_g"].append(dgkv[0])
        dz = [dfq, dfk, dfv, dfg, dmg, dz_tail]
        dx, acc3, dw_in = _kb_in(dz, w_in, h, x_l, norm_g[l][None], mod3, dx, wg_dtype)
        grads["norm_g"].append(acc3[2])
        grads["mod"].append(jnp.concatenate([acc3[0], acc3[1], dgate[0]]))
        if shards is not None:
            pending = _grad_slabs(dw_in, dw_uq, dw_ukv)
        else:
            for name, g in (("w_in", dw_in), ("w_uq", dw_uq), ("w_ukv", dw_ukv), ("w_out", dw_out)):
                grads[name].append(g)
    grads = {k: jnp.stack(v[::-1]) for k, v in grads.items() if v}
    grads["final_g"] = dfinal[0]
    if shards is None:
        return loss_row[0, 0], dx, grads
    return loss_row[0, 0], dx, grads, received, pending


def _pack_small(parts, total):
    flat = jnp.concatenate([p.reshape(-1) for p in parts])
    return jnp.pad(flat, (0, total - flat.shape[0])).reshape(total // LANES, LANES)


def kernel(x, c, positions, norm_g, w_ada, b_ada, w_in, b_f, q_norm_g, w_uq, kv_norm_g, w_ukv, w_out, final_g, loss_target, m_norm_g, m_w_ada, m_b_ada, m_w_in, m_b_f, m_q_norm_g, m_w_uq, m_kv_norm_g, m_w_ukv, m_w_out, m_final_g, v_norm_g, v_w_ada, v_b_ada, v_w_in, v_b_f, v_q_norm_g, v_w_uq, v_kv_norm_g, v_w_ukv, v_w_out, v_final_g):
    n_l, d = norm_g.shape
    me = 4 * lax.axis_index("x") + 2 * lax.axis_index("y") + lax.axis_index("c")
    ada_c = w_ada.shape[2]

    cact = jnp.broadcast_to(jax.nn.silu(c), (N_DEV, d))
    shards = [[w[l].astype(MXU) for w in (w_in, w_uq, w_ukv, w_out)] for l in range(n_l)]
    g_in0, g_cact = _gather_two_level([shards[0][0], cact], "gather_layer0")
    cact_all = g_cact[:, 0, :]

    b_cols = lax.dynamic_slice_in_dim(b_ada, me * ada_c, ada_c, axis=1)[:, None, :]
    modpart = _modpart(cact_all, w_ada, b_cols)
    mod_send = jnp.pad(modpart.transpose(1, 0, 2), ((0, 0), (0, 8 - n_l), (0, 0)))
    (mod_recv,) = _exchange([mod_send], [False], "scatter_mod")
    mod = mod_recv.transpose(1, 0, 2).reshape(8, N_DEV * ada_c)[:n_l]

    loss, dx, gr, received, pending = _local_step(x[0], mod, positions[0], loss_target[0], norm_g, b_f, q_norm_g,
                                                  kv_norm_g, final_g, [_full_weights(g_in0, None, None, None)],
                                                  shards)

    small_parts = [gr["norm_g"], gr["mod"], gr["b_f"], gr["q_norm_g"], gr["kv_norm_g"], gr["final_g"], cact[0],
                   loss.reshape(1)]
    sizes = [int(np.prod(p.shape)) for p in small_parts]
    total = -(-sum(sizes) // 1024) * 1024
    small = _pack_small(small_parts, total)
    *received[0][:3], r_small = _exchange(pending + [small], [False, False, False, True], "exchange_layer0")
    r_in, r_uq, r_ukv, r_out = ([received[l][i] for l in range(n_l)] for i in range(4))

    def upd(slabs, w, m, v, name):
        shp = w.shape
        w2, m2, v2 = (a.reshape(-1, slabs[0].shape[2]) for a in (w, m, v))
        return [o.reshape(shp) for o in _adamw(slabs, w2, m2, v2, name)]

    o_in = upd(r_in, w_in, m_w_in, v_w_in, "adamw_w_in")
    o_uq = upd(r_uq, w_uq, m_w_uq, v_w_uq, "adamw_w_uq")
    o_ukv = upd(r_ukv, w_ukv, m_w_ukv, v_w_ukv, "adamw_w_ukv")
    o_out = upd(r_out, w_out, m_w_out, v_w_out, "adamw_w_out")

    offs = np.cumsum([0] + sizes)
    flat_all = r_small.reshape(N_DEV, total)
    dmod_all = flat_all[:, offs[1]:offs[2]].reshape(N_DEV, n_l, 3 * d)
    dmod_cols = lax.dynamic_slice_in_dim(dmod_all, me * ada_c, ada_c, axis=2).transpose(1, 0, 2)
    cact_cols = flat_all[:, offs[6]:offs[7]][:, :, None]
    g_ada = _ada_grad(cact_cols, dmod_cols)
    o_ada = upd([g_ada.reshape(1, n_l * d, ada_c)], w_ada, m_w_ada, v_w_ada, "adamw_w_ada")

    zero_c = jnp.zeros((d,), F32)
    small_w = [_pack_small([norm_g, b_ada, b_f, q_norm_g, kv_norm_g, final_g, zero_c], total),
               _pack_small([m_norm_g, m_b_ada, m_b_f, m_q_norm_g, m_kv_norm_g, m_final_g, zero_c], total),
               _pack_small([v_norm_g, v_b_ada, v_b_f, v_q_norm_g, v_kv_norm_g, v_final_g, zero_c], total)]
    o_small = [o.reshape(-1) for o in _adamw([r_small], *small_w, "adamw_small")]
    shapes = [norm_g.shape, b_ada.shape, b_f.shape, q_norm_g.shape, kv_norm_g.shape, final_g.shape]

    def small_out(kind, idx):
        return o_small[kind][offs[idx]:offs[idx + 1]].reshape(shapes[idx])

    outs = [o_small[0][offs[7]], dx[None]]
    for kind in range(4):
        outs += [small_out(kind, 0), o_ada[kind], small_out(kind, 1), o_in[kind], small_out(kind, 2),
                 small_out(kind, 3), o_uq[kind], small_out(kind, 4), o_ukv[kind], o_out[kind], small_out(kind, 5)]
    return tuple(outs)
```

```python
import jax
import jax.numpy as jnp
import numpy as np
from jax import lax
from jax.experimental import pallas as pl
from jax.experimental.pallas import tpu as pltpu

F32 = jnp.float32
MXU = jnp.bfloat16

N_DEV = 8
HEADS = 8
PAIRS = HEADS // 2
HEAD_DIM = 64
NOPE = 64
ROPE = 32
HALF_ROPE = ROPE // 2
Q_LORA = 256
KV_LORA = 128
CHUNK = 64
GROUP_W = HEADS * HEAD_DIM
EPS = 1e-6
ROPE_THETA = 10000.0

Z_FQ, Z_FK, Z_FV, Z_FG, Z_MG, Z_QL, Z_KV, Z_MISC, Z_W = 0, 512, 1024, 1536, 2048, 2560, 2816, 2944, 3072
MISC_FF = ROPE
TAIL_W = Z_W - Z_QL
T_KV, T_MISC = Q_LORA, Q_LORA + KV_LORA

ADAM_LR = 0.001
ADAM_B1 = 0.9
ADAM_B2 = 0.999
ADAM_EPS = 1e-08
ADAM_WD = 0.01
ADAM_STEP = 10

VMEM_LIMIT_V7X = 56 * 1024 * 1024
LANES = 128
ATTN_TILE = 1024
V_ROWS = HEAD_DIM + 16
LOG2E = 1.4426950408889634
FOX_SCALE = HEAD_DIM ** -0.5
MLA_SCALE = (NOPE + ROPE) ** -0.5

_NT = (((1,), (1,)), ((), ()))
_TN = (((0,), (0,)), ((), ()))


def _params(*sem, side_effects=False):
    return pltpu.CompilerParams(dimension_semantics=sem, vmem_limit_bytes=VMEM_LIMIT_V7X,
                                has_side_effects=side_effects)


def _sds(shape, dtype=F32):
    return jax.ShapeDtypeStruct(shape, dtype)


def _full(shape):
    nd = len(shape)
    return pl.BlockSpec(shape, lambda *_: (0,) * nd)


def _rows(tm, width, col=0):
    return pl.BlockSpec((tm, width), lambda i: (i, col))


def _exchange(arrs, gather, name):
    n = len(arrs)

    def kern(*refs):
        copies = _exchange_copies(refs[:n], refs[n:2 * n], gather, *refs[2 * n:])
        _exchange_start(copies)
        _exchange_wait(copies)

    return pl.pallas_call(
        kern, name=name, out_shape=_exchange_out_shapes(arrs, gather),
        in_specs=[pl.BlockSpec(memory_space=pl.ANY)] * n,
        out_specs=[pl.BlockSpec(memory_space=pl.ANY)] * n,
        scratch_shapes=_exchange_sems(n),
        compiler_params=pltpu.CompilerParams(has_side_effects=True),
    )(*arrs)


def _gather_two_level(arrs, name):
    n = len(arrs)

    def kern(*refs):
        ins, outs = refs[:n], refs[n:2 * n]
        send_sems, recv_sems, loc_sems = refs[2 * n:]
        x, y, c = lax.axis_index("x"), lax.axis_index("y"), lax.axis_index("c")
        me, sibling = (x, y, c), (x, y, 1 - c)
        chips = [(1 - x, y), (x, 1 - y), (1 - x, 1 - y)]

        def slot(i, dev):
            return outs[i].at[4 * dev[0] + 2 * dev[1] + dev[2]]

        def copy(i, k, block, to, src=None):
            return pltpu.make_async_remote_copy(
                src_ref=slot(i, block) if src is None else src, dst_ref=slot(i, block), send_sem=send_sems.at[i, k],
                recv_sem=recv_sems.at[i, k], device_id=to, device_id_type=pl.DeviceIdType.MESH)

        mine = [pltpu.make_async_copy(ins[i], slot(i, me), loc_sems.at[i]) for i in range(n)]
        first = [copy(i, 0, me, sibling, src=ins[i]) for i in range(n)]
        first += [copy(i, 1 + j, me, (*chip, c), src=ins[i]) for j, chip in enumerate(chips) for i in range(n)]
        for cp in mine + first:
            cp.start()
        passed = []
        for j, chip in enumerate(chips):
            for i in range(n):
                copy(i, 1 + j, (*chip, c), me).wait_recv()
                fwd = copy(i, 4 + j, (*chip, c), sibling)
                fwd.start()
                passed.append(fwd)
        for i in range(n):
            copy(i, 0, sibling, me).wait_recv()
            for j, chip in enumerate(chips):
                copy(i, 4 + j, (*chip, 1 - c), me).wait_recv()
        for cp in first + passed:
            cp.wait_send()
        for cp in mine:
            cp.wait()

    return pl.pallas_call(
        kern, name=name, out_shape=_exchange_out_shapes(arrs, [True] * n),
        in_specs=[pl.BlockSpec(memory_space=pl.ANY)] * n,
        out_specs=[pl.BlockSpec(memory_space=pl.ANY)] * n,
        scratch_shapes=_exchange_sems(n),
        compiler_params=pltpu.CompilerParams(has_side_effects=True),
    )(*arrs)


def _exchange_out_shapes(arrs, gather):
    return [_sds((N_DEV,) + tuple(a.shape) if g else tuple(a.shape), a.dtype) for a, g in zip(arrs, gather)]


def _exchange_sems(n):
    return [pltpu.SemaphoreType.DMA((n, N_DEV)), pltpu.SemaphoreType.DMA((n, N_DEV)), pltpu.SemaphoreType.DMA((n,))]


def _exchange_copies(ins, outs, gather, send_sems, recv_sems, loc_sems, recv=True):
    n = len(ins)
    x, y, c = lax.axis_index("x"), lax.axis_index("y"), lax.axis_index("c")
    me = 4 * x + 2 * y + c

    def src(i, j):
        return ins[i] if gather[i] else ins[i].at[j]

    local = [pltpu.make_async_copy(src(i, me), outs[i].at[me], loc_sems.at[i]) for i in range(n)]
    sends, recvs = [], []
    for k in range(1, N_DEV):
        px = 1 - x if k & 4 else x
        py = 1 - y if k & 2 else y
        pc = 1 - c if k & 1 else c
        p = 4 * px + 2 * py + pc
        for i in range(n):
            sends.append(pltpu.make_async_remote_copy(
                src_ref=src(i, p), dst_ref=outs[i].at[me], send_sem=send_sems.at[i, k],
                recv_sem=recv_sems.at[i, k], device_id=(px, py, pc), device_id_type=pl.DeviceIdType.MESH))
            if recv:
                recvs.append(pltpu.make_async_remote_copy(
                    src_ref=src(i, p), dst_ref=outs[i].at[p], send_sem=send_sems.at[i, k],
                    recv_sem=recv_sems.at[i, k], device_id=(px, py, pc), device_id_type=pl.DeviceIdType.MESH))
    return local, sends, recvs


def _exchange_start(copies):
    local, sends, _ = copies
    for cp in local + sends:
        cp.start()


def _exchange_wait(copies):
    local, sends, recvs = copies
    for cp in recvs:
        cp.wait_recv()
    for cp in sends:
        cp.wait_send()
    for cp in local:
        cp.wait()


def _modpart(cact8, w_ada, b_cols):
    n_l, d, cw = w_ada.shape

    def kern(c_ref, w_ref, b_ref, o_ref):
        o_ref[0] = jnp.dot(c_ref[...].astype(MXU), w_ref[0].astype(MXU), preferred_element_type=F32) + b_ref[0]

    return pl.pallas_call(
        kern, name="modpart", grid=(n_l,), out_shape=_sds((n_l, N_DEV, cw)),
        in_specs=[_full((N_DEV, d)), pl.BlockSpec((1, d, cw), lambda l: (l, 0, 0)),
                  pl.BlockSpec((1, 1, cw), lambda l: (l, 0, 0))],
        out_specs=pl.BlockSpec((1, N_DEV, cw), lambda l: (l, 0, 0)),
        compiler_params=_params("arbitrary"),
    )(cact8, w_ada, b_cols)


def _ada_grad(cact_cols, dmod_cols):
    n_l, _, cw = dmod_cols.shape
    d = cact_cols.shape[1]

    def kern(c_ref, dm_ref, o_ref):
        acc = c_ref[0] * dm_ref[0, 0:1, :]
        for s in range(1, N_DEV):
            acc = acc + c_ref[s] * dm_ref[0, s:s + 1, :]
        o_ref[0] = acc

    return pl.pallas_call(
        kern, name="ada_grad", grid=(n_l,), out_shape=_sds((n_l, d, cw)),
        in_specs=[_full((N_DEV, d, 1)), pl.BlockSpec((1, N_DEV, cw), lambda l: (l, 0, 0))],
        out_specs=pl.BlockSpec((1, d, cw), lambda l: (l, 0, 0)),
        compiler_params=_params("arbitrary"),
    )(cact_cols, dmod_cols)


def _k_in(x, g, mod3, w, tm=256):
    s_len, d = x.shape
    qkv_w = 3 * GROUP_W

    def kern(x_ref, g_ref, mod_ref, w_ref, h_ref, qkv_ref, gates_ref, tail_ref):
        xv = x_ref[...]
        r = lax.rsqrt(jnp.mean(xv * xv, axis=-1, keepdims=True) + EPS)
        xn = xv * r * g_ref[...]
        h = (xn * (1.0 + mod_ref[1:2, :]) + mod_ref[0:1, :]).astype(MXU)
        h_ref[...] = h
        z = jnp.dot(h, w_ref[...], preferred_element_type=F32)
        qkv_ref[:, :GROUP_W] = (z[:, Z_FQ:Z_FQ + GROUP_W] * (FOX_SCALE * LOG2E)).astype(MXU)
        qkv_ref[:, GROUP_W:] = z[:, Z_FK:Z_FK + 2 * GROUP_W].astype(MXU)
        gates_ref[...] = z[:, Z_FG:Z_QL]
        tail_ref[...] = z[:, Z_QL:]

    return pl.pallas_call(
        kern, name="k_in", grid=(s_len // tm,),
        out_shape=[_sds((s_len, d), MXU), _sds((s_len, qkv_w), MXU), _sds((s_len, Z_QL - Z_FG)),
                   _sds((s_len, TAIL_W))],
        in_specs=[_rows(tm, d), _full((1, d)), _full((3, d)), _full((d, Z_W))],
        out_specs=[_rows(tm, d), _rows(tm, qkv_w), _rows(tm, Z_QL - Z_FG), _rows(tm, TAIL_W)],
        compiler_params=_params("arbitrary"),
    )(x, g, mod3, w)


def _scan_matrices(rows, chunks, reverse):
    r_i = lax.broadcasted_iota(jnp.int32, (LANES, LANES), 0)
    c_i = lax.broadcasted_iota(jnp.int32, (LANES, LANES), 1)
    a_i = lax.broadcasted_iota(jnp.int32, (rows, rows), 0)
    b_i = lax.broadcasted_iota(jnp.int32, (rows, rows), 1)
    same_head = (a_i // chunks) == (b_i // chunks)
    if reverse:
        return (r_i >= c_i).astype(F32), (same_head & (b_i > a_i)).astype(F32)
    return (r_i <= c_i).astype(F32), (same_head & (b_i < a_i)).astype(F32)


def _scan_rows(x, inner, outer):
    tot = jnp.broadcast_to(jnp.sum(x, axis=1, keepdims=True), x.shape)
    return (jnp.dot(x, inner, precision=lax.Precision.HIGHEST, preferred_element_type=F32)
            + jnp.dot(outer, tot, precision=lax.Precision.HIGHEST, preferred_element_type=F32))


def _k_cum(ff_rows, b_rows, chunks):
    rows = ff_rows.shape[0]

    def kern(ff_ref, b_ref, cum_ref):
        xc = ff_ref[...] + b_ref[...]
        lf = jnp.minimum(xc, 0.0) - jnp.log(1.0 + jnp.exp(-jnp.abs(xc)))
        cum_ref[...] = _scan_rows(lf, *_scan_matrices(rows, chunks, False))

    return pl.pallas_call(
        kern, name="k_cum", out_shape=_sds((rows, LANES)),
        in_specs=[pl.BlockSpec(memory_space=pltpu.VMEM)] * 2,
        out_specs=pl.BlockSpec(memory_space=pltpu.VMEM),
        compiler_params=_params(),
    )(ff_rows, b_rows)


def _k_cum_bwd(dc_rows, ff_rows, b_rows, chunks):
    rows = ff_rows.shape[0]

    def kern(dc_ref, ff_ref, b_ref, dff_ref, db_ref):
        dlf = _scan_rows(dc_ref[...], *_scan_matrices(rows, chunks, True))
        dff = dlf * jax.nn.sigmoid(-(ff_ref[...] + b_ref[...]))
        dff_ref[...] = dff
        db_ref[...] = jnp.broadcast_to(jnp.sum(dff, axis=1, keepdims=True), dff.shape)

    return pl.pallas_call(
        kern, name="k_cum_bwd", out_shape=[_sds((rows, LANES)), _sds((rows, LANES))],
        in_specs=[pl.BlockSpec(memory_space=pltpu.VMEM)] * 3,
        out_specs=[pl.BlockSpec(memory_space=pltpu.VMEM)] * 2,
        compiler_params=_params(),
    )(dc_rows, ff_rows, b_rows)


def _swap16(t):
    lane = lax.broadcasted_iota(jnp.int32, t.shape, 1)
    return jnp.where(lane % ROPE < HALF_ROPE, pltpu.roll(t, LANES - HALF_ROPE, 1), pltpu.roll(t, HALF_ROPE, 1))


def _rope(t, cos, sin):
    return t * cos + _swap16(t) * sin


def _rope_bwd(dt, cos, sin):
    return dt * cos - _swap16(dt) * sin


def _k_prep(tail, cos, sin, gq, gkv, wuq, wukv, tm=512):
    s_len = tail.shape[0]
    qc = MLA_SCALE * LOG2E

    def kern(tail_ref, cos_ref, sin_ref, gq_ref, gkv_ref, wuq_ref, wukv_ref,
             qn_out, qr_out, kn_out, v_out, kr_out, qn_ref, kvn_ref):
        cs, sn = cos_ref[...], sin_ref[...]
        ql = tail_ref[:, :T_KV]
        rq = lax.rsqrt(jnp.mean(ql * ql, axis=-1, keepdims=True) + EPS)
        qn = (ql * rq * gq_ref[...]).astype(MXU)
        qn_ref[...] = qn
        q = jnp.dot(qn, wuq_ref[...], preferred_element_type=F32)
        qn_out[...] = (q[:, :GROUP_W] * qc).astype(MXU)
        for blk in range(PAIRS):
            lo = GROUP_W + blk * LANES
            qr_out[:, blk * LANES:(blk + 1) * LANES] = (_rope(q[:, lo:lo + LANES], cs, sn) * qc).astype(MXU)
        kvl = tail_ref[:, T_KV:T_MISC]
        rk = lax.rsqrt(jnp.mean(kvl * kvl, axis=-1, keepdims=True) + EPS)
        kvn = (kvl * rk * gkv_ref[...]).astype(MXU)
        kvn_ref[...] = kvn
        kv = jnp.dot(kvn, wukv_ref[...], preferred_element_type=F32)
        kn_out[...] = kv[:, :GROUP_W].astype(MXU)
        v_out[...] = kv[:, GROUP_W:].astype(MXU)
        misc = tail_ref[:, T_MISC:]
        lane = lax.broadcasted_iota(jnp.int32, misc.shape, 1)
        kr = jnp.where(lane < ROPE, _rope(misc, cs, sn), 0.0)
        kr_out[...] = (kr + pltpu.roll(kr, HEAD_DIM, 1)).astype(MXU)

    return pl.pallas_call(
        kern, name="k_prep", grid=(s_len // tm,),
        out_shape=[_sds((s_len, GROUP_W), MXU), _sds((s_len, GROUP_W), MXU), _sds((s_len, GROUP_W), MXU),
                   _sds((s_len, GROUP_W), MXU), _sds((s_len, LANES), MXU), _sds((s_len, Q_LORA), MXU),
                   _sds((s_len, KV_LORA), MXU)],
        in_specs=[_rows(tm, TAIL_W), _rows(tm, LANES), _rows(tm, LANES),
                  _full((1, Q_LORA)), _full((1, KV_LORA)), _full((Q_LORA, 2 * GROUP_W)),
                  _full((KV_LORA, 2 * GROUP_W))],
        out_specs=[_rows(tm, GROUP_W), _rows(tm, GROUP_W), _rows(tm, GROUP_W), _rows(tm, GROUP_W), _rows(tm, LANES),
                   _rows(tm, Q_LORA), _rows(tm, KV_LORA)],
        compiler_params=_params("arbitrary"),
    )(tail, cos, sin, gq, gkv, wuq, wukv)


def _block_mask(kn, qn, q_off, chunk_mask, transposed):
    shape = (kn, qn) if transposed else (qn, kn)
    row = lax.broadcasted_iota(jnp.int32, shape, 0)
    col = lax.broadcasted_iota(jnp.int32, shape, 1)
    qi, ki = (col + q_off, row) if transposed else (row + q_off, col)
    if chunk_mask:
        return (ki // CHUNK) <= (qi // CHUNK)
    return ki <= qi


def _head_operand(x, hh, other=None):
    lane = lax.broadcasted_iota(jnp.int32, x.shape, 1)
    own = (lane >= hh * HEAD_DIM) & (lane < (hh + 1) * HEAD_DIM)
    return jnp.where(own, x, jnp.zeros_like(x) if other is None else other)


def _attention_fwd(q, q_blk, k, k_blk, v, v_blk, bias, rope, chunk_mask, name, side=None):
    s_len = q.shape[0]
    t = min(ATTN_TILE, s_len // 2)
    nq = s_len // t
    n_side = len(side[0]) if side else 0

    def kern(*refs):
        q_ref, k_ref, v_ref = refs[:3]
        pos = 3
        if bias is not None:
            ck_ref = refs[pos]
            pos += 1
        if rope is not None:
            qr_ref, kr_ref = refs[pos:pos + 2]
            pos += 2
        side_in = refs[pos:pos + n_side]
        pos += n_side
        o_ref, lse_ref = refs[pos:pos + 2]
        side_out = refs[pos + 2:pos + 2 + n_side]
        vt_scr, m_scr, acc_scr, ck_scr, s_scr = refs[pos + 2 + n_side:pos + 7 + n_side]
        sems = refs[pos + 7 + n_side:]
        pj = pl.program_id(0)
        if n_side:
            @pl.when(pj == 0)
            def _():
                _exchange_start(_exchange_copies(side_in, side_out, side[1], *sems, recv=False))
        vt_scr[:, HEAD_DIM:, :] = jnp.ones((2, V_ROWS - HEAD_DIM, s_len), vt_scr.dtype)
        for i in range(nq):
            vtt = v_ref[i * t:(i + 1) * t, :].T
            for hh in range(2):
                vt_scr[hh, :HEAD_DIM, i * t:(i + 1) * t] = vtt[hh * HEAD_DIM:(hh + 1) * HEAD_DIM, :]
                if bias is not None:
                    ckt = ck_ref[i * t:(i + 1) * t, :]
                    lane = lax.broadcasted_iota(jnp.int32, ckt.shape, 1)
                    ck_scr[hh, i * t:(i + 1) * t, :] = jnp.sum(jnp.where(lane == 2 * pj + hh, ckt, 0.0), axis=1,
                                                               keepdims=True)

        def qbody(qi):
            qs = qi * t
            qt = q_ref[pl.ds(qs, t), :]
            qrt = qr_ref[pl.ds(qs, t), :] if rope is not None else None
            qh = [_head_operand(qt, hh, qrt) for hh in range(2)]
            m_scr[...] = jnp.full(m_scr.shape, -jnp.inf, F32)
            acc_scr[...] = jnp.zeros(acc_scr.shape, F32)

            def logits(ks, slot):
                kt = k_ref[pl.ds(ks, t), :]
                kh = [_head_operand(kt, hh, kr_ref[pl.ds(ks, t), :]) for hh in range(2)] if rope is not None else [kt, kt]
                for hh in range(2):
                    s_scr[slot, hh] = lax.dot_general(kh[hh], qh[hh], _NT, preferred_element_type=F32)

            def consume(ks, slot, masked):
                sts = [s_scr[slot, hh] for hh in range(2)]
                if bias is not None:
                    sts = [sts[hh] - ck_scr[hh, pl.ds(ks, t), :] for hh in range(2)]
                if masked:
                    sts = [jnp.where(_block_mask(t, t, 0, chunk_mask, True), st, -jnp.inf) for st in sts]
                m_old = [m_scr[hh] for hh in range(2)]
                m_new = [jnp.maximum(m_old[hh], jnp.max(sts[hh], axis=0, keepdims=True)) for hh in range(2)]
                pts = [jnp.exp2(sts[hh] - m_new[hh]).astype(MXU) for hh in range(2)]
                for hh in range(2):
                    alpha = jnp.exp2(m_old[hh] - m_new[hh])
                    acc_scr[hh] = alpha * acc_scr[hh] + jnp.dot(vt_scr[hh, :, pl.ds(ks, t)], pts[hh],
                                                                preferred_element_type=F32)
                    m_scr[hh] = m_new[hh]

            logits(0, 0)
            for ki in range(qi):
                logits((ki + 1) * t, (ki + 1) % 2)
                consume(ki * t, ki % 2, False)
            consume(qs, qi % 2, True)
            outs = []
            for hh in range(2):
                acc = acc_scr[hh]
                l = acc[HEAD_DIM:HEAD_DIM + 1, :]
                outs.append(acc[:HEAD_DIM, :] / l)
                lse_ref[hh, :, pl.ds(qs, t)] = m_scr[hh] + jnp.log2(l)
            o_ref[pl.ds(qs, t), :] = jnp.concatenate(outs, axis=0).T

        for qi in range(nq):
            qbody(qi)
        if n_side:
            @pl.when(pj == PAIRS - 1)
            def _():
                _exchange_wait(_exchange_copies(side_in, side_out, side[1], *sems))

    def tok(blk):
        return pl.BlockSpec((s_len, LANES), lambda j: (0, blk + j))

    rowb = pl.BlockSpec((2, 1, s_len), lambda j: (j, 0, 0))
    hbm = pl.BlockSpec(memory_space=pl.ANY)
    ins = [q, k, v]
    in_specs = [tok(q_blk), tok(k_blk), tok(v_blk)]
    if bias is not None:
        ins.append(bias)
        in_specs.append(_full((s_len, LANES)))
    if rope is not None:
        ins += list(rope)
        in_specs += [tok(0), _full((s_len, LANES))]
    out_shape = [_sds((s_len, PAIRS * LANES)), _sds((HEADS, 1, s_len))]
    scratch = [pltpu.VMEM((2, V_ROWS, s_len), v.dtype), pltpu.VMEM((2, 1, t), F32), pltpu.VMEM((2, V_ROWS, t), F32),
               pltpu.VMEM((2, s_len if bias is not None else 8, 1), F32), pltpu.VMEM((2, 2, t, t), F32)]
    if n_side:
        ins += list(side[0])
        out_shape += _exchange_out_shapes(*side)
        scratch += _exchange_sems(n_side)
    return pl.pallas_call(
        kern, name=name, grid=(PAIRS,), out_shape=out_shape,
        in_specs=in_specs + [hbm] * n_side, out_specs=[tok(0), rowb] + [hbm] * n_side,
        scratch_shapes=scratch,
        compiler_params=_params("arbitrary", side_effects=bool(n_side)),
    )(*ins)


def _attention_bwd(q, q_blk, k, k_blk, v, v_blk, do, pack, ck_row, rope, chunk_mask, q_scale, k_scale, name,
                   side=None):
    s_len = q.shape[0]
    t = min(ATTN_TILE, s_len // 2)
    nq = s_len // t
    has_bias = ck_row is not None
    nv = 2 if rope is not None else 1
    n_side = len(side[0]) if side else 0

    def kern(*refs):
        q_ref, k_ref, v_ref, do_ref, pack_ref = refs[:5]
        pos = 5
        if has_bias:
            ck_ref = refs[pos]
            pos += 1
        if rope is not None:
            qr_ref, kr_ref = refs[pos:pos + 2]
            pos += 2
        side_in = refs[pos:pos + n_side]
        pos += n_side
        dq_ref, dk_ref, dv_ref = refs[pos:pos + 3]
        pos += 3
        if has_bias:
            dcq_ref, dck_ref = refs[pos:pos + 2]
            pos += 2
        if rope is not None:
            dqr_ref, dkr_ref = refs[pos:pos + 2]
            pos += 2
        side_out = refs[pos:pos + n_side]
        pos += n_side
        qt_scr, dot_scr, dkt_scr, dvt_scr, dq_scr, dcq_scr = refs[pos:pos + 6]
        sems = refs[pos + 6:]
        pj = pl.program_id(0)
        if n_side:
            @pl.when(pj == 0)
            def _():
                _exchange_start(_exchange_copies(side_in, side_out, side[1], *sems, recv=False))

        for i in range(nq):
            sl = slice(i * t, (i + 1) * t)
            dot_scr[:, sl] = do_ref[sl, :].T
            if rope is not None:
                for hh in range(2):
                    qt_scr[hh, :, sl] = _head_operand(q_ref[sl, :], hh, qr_ref[sl, :]).T
            else:
                qt_scr[0, :, sl] = q_ref[sl, :].T
        dkt_scr[...] = jnp.zeros(dkt_scr.shape, F32)
        dvt_scr[...] = jnp.zeros(dvt_scr.shape, F32)
        if has_bias:
            dck_ref[...] = jnp.zeros(dck_ref.shape, F32)

            @pl.when(pj == 0)
            def _():
                dcq_ref[...] = jnp.zeros(dcq_ref.shape, F32)

        def qbody(qi, _):
            qs = pl.multiple_of(qi * t, t)
            qt = q_ref[pl.ds(qs, t), :]
            qrt = qr_ref[pl.ds(qs, t), :] if rope is not None else None
            dot = do_ref[pl.ds(qs, t), :]
            pk = pack_ref[pl.ds(qs, t), :]
            lane = lax.broadcasted_iota(jnp.int32, pk.shape, 1)
            qh = [_head_operand(qt, hh, qrt) for hh in range(2)]
            doh = [_head_operand(dot, hh) for hh in range(2)]
            a_col = [jnp.sum(jnp.where(lane == 2 * pj + hh, pk, 0.0), axis=1, keepdims=True) for hh in range(2)]
            d_col = [jnp.sum(jnp.where(lane == HEADS + 2 * pj + hh, pk, 0.0), axis=1, keepdims=True)
                     for hh in range(2)]
            dq_scr[...] = jnp.zeros(dq_scr.shape, F32)
            if has_bias:
                dcq_scr[...] = jnp.zeros(dcq_scr.shape, F32)

            def block(ks, kn, q0, qn, masked):
                kt = k_ref[pl.ds(ks, kn), :]
                krt = kr_ref[pl.ds(ks, kn), :] if rope is not None else None
                kh = [_head_operand(kt, hh, krt) for hh in range(2)]
                vt = v_ref[pl.ds(ks, kn), :]
                qr_ = slice(q0, q0 + qn)
                qcols = pl.ds(pl.multiple_of(qs + q0, t // 2), qn)
                ss = [lax.dot_general(qh[hh][qr_], kh[hh] if rope is not None else kt, _NT,
                                      preferred_element_type=F32) + a_col[hh][qr_] for hh in range(2)]
                if has_bias:
                    ss = [ss[hh] - ck_ref[hh, :, pl.ds(ks, kn)] for hh in range(2)]
                dpds = [lax.dot_general(doh[hh][qr_], vt, _NT, preferred_element_type=F32) for hh in range(2)]
                ps = [jnp.exp2(s) for s in ss]
                if masked:
                    ps = [jnp.where(_block_mask(kn, qn, q0, chunk_mask, False), p, 0.0) for p in ps]
                dss = [ps[hh] * (dpds[hh] - d_col[hh][qr_]) for hh in range(2)]
                for hh in range(2):
                    rows = slice(hh * HEAD_DIM, (hh + 1) * HEAD_DIM)
                    dsb = dss[hh].astype(MXU)
                    dvt_scr[rows, pl.ds(ks, kn)] += jnp.dot(dot_scr[rows, qcols], ps[hh].astype(MXU),
                                                            preferred_element_type=F32)
                    if rope is not None:
                        dkt_scr[hh, :, pl.ds(ks, kn)] += jnp.dot(qt_scr[hh, :, qcols], dsb,
                                                                 preferred_element_type=F32)
                    else:
                        dkt_scr[0, rows, pl.ds(ks, kn)] += jnp.dot(qt_scr[0, rows, qcols], dsb,
                                                                   preferred_element_type=F32)
                    dq_scr[hh if rope is not None else 0, qr_, :] += jnp.dot(dsb, kh[hh], preferred_element_type=F32)
                    if has_bias:
                        dcq_scr[hh, qr_, :] += jnp.sum(dss[hh], axis=1, keepdims=True)
                        dck_ref[hh, :, pl.ds(ks, kn)] += -jnp.sum(dss[hh], axis=0, keepdims=True)

            def loop_body(ki, carry):
                block(pl.multiple_of(ki * t, t), t, 0, t, False)
                return carry

            lax.fori_loop(0, qi, loop_body, 0)
            block(qs, t // 2, 0, t // 2, True)
            block(qs, t, t // 2, t // 2, True)
            if rope is not None:
                first = lane < HEAD_DIM
                dq_ref[pl.ds(qs, t), :] = (jnp.where(first, dq_scr[0], dq_scr[1]) * q_scale).astype(dq_ref.dtype)
                dqr_ref[pl.ds(qs, t), :] = jnp.where(first, dq_scr[1], dq_scr[0]) * q_scale
            else:
                dq_ref[pl.ds(qs, t), :] = (dq_scr[0] * q_scale).astype(dq_ref.dtype)
            if has_bias:
                old = dcq_ref[pl.ds(qs, t), :]
                dcq_ref[pl.ds(qs, t), :] = jnp.where(lane == 2 * pj, dcq_scr[0],
                                                     jnp.where(lane == 2 * pj + 1, dcq_scr[1], old))
            return 0

        lax.fori_loop(0, nq, qbody, 0)
        for i in range(nq):
            sl = slice(i * t, (i + 1) * t)
            dv_ref[sl, :] = dvt_scr[:, sl].T.astype(dv_ref.dtype)
            if rope is not None:
                d0, d1 = dkt_scr[0, :, sl], dkt_scr[1, :, sl]
                first = lax.broadcasted_iota(jnp.int32, d0.shape, 0) < HEAD_DIM
                dk_ref[sl, :] = (jnp.where(first, d0, d1).T * k_scale).astype(dk_ref.dtype)
                dkr_ref[0, sl, :] = jnp.where(first, d1, d0).T * k_scale
            else:
                dk_ref[sl, :] = (dkt_scr[0, :, sl].T * k_scale).astype(dk_ref.dtype)
        if n_side:
            @pl.when(pj == PAIRS - 1)
            def _():
                _exchange_wait(_exchange_copies(side_in, side_out, side[1], *sems))

    def tok(blk):
        return pl.BlockSpec((s_len, LANES), lambda j: (0, blk + j))

    shared = _full((s_len, LANES))
    rowb = pl.BlockSpec((2, 1, s_len), lambda j: (j, 0, 0))
    slab = pl.BlockSpec((1, s_len, LANES), lambda j: (j, 0, 0))
    hbm = pl.BlockSpec(memory_space=pl.ANY)
    ins = [q, k, v, do, pack]
    in_specs = [tok(q_blk), tok(k_blk), tok(v_blk), tok(0), shared]
    out_shape = [_sds((s_len, PAIRS * LANES), MXU)] * 3
    out_specs = [tok(0)] * 3
    if has_bias:
        ins.append(ck_row)
        in_specs.append(rowb)
        out_shape += [_sds((s_len, LANES)), _sds((HEADS, 1, s_len))]
        out_specs += [shared, rowb]
    if rope is not None:
        ins += list(rope)
        in_specs += [tok(0), shared]
        out_shape += [_sds((s_len, PAIRS * LANES)), _sds((PAIRS, s_len, LANES))]
        out_specs += [tok(0), slab]
    scratch = [pltpu.VMEM((nv, LANES, s_len), q.dtype), pltpu.VMEM((LANES, s_len), do.dtype),
               pltpu.VMEM((nv, LANES, s_len), F32), pltpu.VMEM((LANES, s_len), F32),
               pltpu.VMEM((nv, t, LANES), F32), pltpu.VMEM((2, t, 1), F32)]
    if n_side:
        ins += list(side[0])
        out_shape += _exchange_out_shapes(*side)
        scratch += _exchange_sems(n_side)
    return pl.pallas_call(
        kern, name=name, grid=(PAIRS,), out_shape=out_shape,
        in_specs=in_specs + [hbm] * n_side, out_specs=out_specs + [hbm] * n_side, scratch_shapes=scratch,
        compiler_params=_params("arbitrary", side_effects=bool(n_side)),
    )(*ins)


def _silu(a):
    return a * jax.nn.sigmoid(a)


def _k_out(of, om, gates, x, gate, wout, tm=256):
    s_len, d = x.shape

    def kern(of_ref, om_ref, gates_ref, x_ref, gate_ref, w_ref, xo_ref, y_ref, u_ref):
        u_ref[:, :GROUP_W] = (of_ref[...] * _silu(gates_ref[:, :GROUP_W])).astype(MXU)
        u_ref[:, GROUP_W:] = (om_ref[...] * _silu(gates_ref[:, GROUP_W:])).astype(MXU)
        y = jnp.dot(u_ref[...], w_ref[...], preferred_element_type=F32)
        y_ref[...] = y.astype(y_ref.dtype)
        xo_ref[...] = x_ref[...] + gate_ref[...] * y

    return pl.pallas_call(
        kern, name="k_out", grid=(s_len // tm,),
        out_shape=[_sds((s_len, d)), _sds((s_len, d), MXU), _sds((s_len, 2 * GROUP_W), MXU)],
        in_specs=[_rows(tm, GROUP_W), _rows(tm, GROUP_W), _rows(tm, 2 * GROUP_W), _rows(tm, d), _full((1, d)),
                  _full((2 * GROUP_W, d))],
        out_specs=[_rows(tm, d), _rows(tm, d), _rows(tm, 2 * GROUP_W)],
        compiler_params=_params("arbitrary"),
    )(of, om, gates, x, gate, wout)


def _k_loss(x, gf, tgt, tm=256):
    s_len, d = x.shape

    def kern(x_ref, g_ref, t_ref, loss_ref, dx_ref, dg_ref):
        i = pl.program_id(0)
        xv = x_ref[...]
        r = lax.rsqrt(jnp.mean(xv * xv, axis=-1, keepdims=True) + EPS)
        xh = xv * r
        diff = xh * g_ref[...] - t_ref[...]
        part = 0.5 * jnp.sum(jnp.mean(diff * diff, axis=-1, keepdims=True))
        dout = diff * (1.0 / d)
        dxh = dout * g_ref[...]
        dx_ref[...] = r * (dxh - xh * jnp.mean(dxh * xh, axis=-1, keepdims=True))

        @pl.when(i == 0)
        def _():
            loss_ref[...] = jnp.zeros_like(loss_ref)
            dg_ref[...] = jnp.zeros_like(dg_ref)

        loss_ref[...] += jnp.full(loss_ref.shape, part, F32)
        dg_ref[...] += jnp.sum(dout * xh, axis=0, keepdims=True)

    return pl.pallas_call(
        kern, name="k_loss", grid=(s_len // tm,),
        out_shape=[_sds((1, LANES)), _sds((s_len, d)), _sds((1, d))],
        in_specs=[_rows(tm, d), _full((1, d)), _rows(tm, d)],
        out_specs=[_full((1, LANES)), _rows(tm, d), _full((1, d))],
        compiler_params=_params("arbitrary"),
    )(x, gf, tgt)


def _kb_out(dxo, y, u, gate, wout, of, om, gates, dw_dtype, tm=512):
    s_len, d = dxo.shape
    steps = s_len // tm

    def kern(dxo_ref, y_ref, u_ref, gate_ref, wt_ref, of_ref, om_ref, gates_ref,
             dof_ref, dom_ref, dfg_ref, dmg_ref, dlf_ref, dlm_ref, dgate_ref, dw_ref, dw_acc):
        i = pl.program_id(0)
        dxv = dxo_ref[...]

        @pl.when(i == 0)
        def _():
            dgate_ref[...] = jnp.zeros_like(dgate_ref)
            dw_acc[...] = jnp.zeros_like(dw_acc)

        dgate_ref[...] += jnp.sum(dxv * y_ref[...], axis=0, keepdims=True)
        dyb = (dxv * gate_ref[...]).astype(MXU)
        dw_acc[...] += lax.dot_general(u_ref[...], dyb, _TN, preferred_element_type=F32)

        @pl.when(i == steps - 1)
        def _():
            dw_ref[...] = dw_acc[...].astype(dw_ref.dtype)

        du = lax.dot_general(dyb, wt_ref[...], _NT, preferred_element_type=F32)
        head_of = (lax.broadcasted_iota(jnp.int32, (GROUP_W, LANES), 0) // HEAD_DIM
                   == lax.broadcasted_iota(jnp.int32, (GROUP_W, LANES), 1)).astype(F32)
        for du_g, o_ref, a, do_ref, dg_ref, dl_ref in (
                (du[:, :GROUP_W], of_ref, gates_ref[:, :GROUP_W], dof_ref, dfg_ref, dlf_ref),
                (du[:, GROUP_W:], om_ref, gates_ref[:, GROUP_W:], dom_ref, dmg_ref, dlm_ref)):
            sg = jax.nn.sigmoid(a)
            ov = o_ref[...]
            dov = du_g * (a * sg)
            do_ref[...] = dov.astype(MXU)
            dg_ref[...] = (du_g * ov * (sg * (1.0 + a * (1.0 - sg)))).astype(MXU)
            dl_ref[...] = jnp.dot(dov * ov, head_of, precision=lax.Precision.HIGH, preferred_element_type=F32)

    return pl.pallas_call(
        kern, name="kb_out", grid=(steps,),
        out_shape=[_sds((s_len, GROUP_W), MXU), _sds((s_len, GROUP_W), MXU),
                   _sds((s_len, GROUP_W), MXU), _sds((s_len, GROUP_W), MXU), _sds((s_len, LANES)),
                   _sds((s_len, LANES)), _sds((1, d)), _sds((2 * GROUP_W, d), dw_dtype)],
        in_specs=[_rows(tm, d), _rows(tm, d), _rows(tm, 2 * GROUP_W), _full((1, d)), _full((2 * GROUP_W, d)),
                  _rows(tm, GROUP_W), _rows(tm, GROUP_W), _rows(tm, 2 * GROUP_W)],
        out_specs=[_rows(tm, GROUP_W), _rows(tm, GROUP_W), _rows(tm, GROUP_W),
                   _rows(tm, GROUP_W), _rows(tm, LANES), _rows(tm, LANES), _full((1, d)), _full((2 * GROUP_W, d))],
        scratch_shapes=[pltpu.VMEM((2 * GROUP_W, d), F32)],
        compiler_params=_params("arbitrary"),
    )(dxo, y, u, gate, wout, of, om, gates)


def _kb_prep(dqn, dqr, dkn, dv, dkr, dff, tail, qn, kvn, cos, sin, gq, gkv, wuq_t, wukv_t, dw_dtype, tm=512):
    s_len = tail.shape[0]
    qw = 2 * GROUP_W
    steps = s_len // tm

    def kern(dqn_ref, dqr_ref, dkn_ref, dv_ref, dkr_ref, dff_ref, tail_ref, qn_ref, kvn_ref, cos_ref, sin_ref,
             gq_ref, gkv_ref, wuqt_ref, wukvt_ref, dz_ref, dgq_ref, dgkv_ref, dwuq_ref, dwukv_ref,
             dq_ref, uq_acc, ukv_acc):
        i = pl.program_id(0)

        @pl.when(i == 0)
        def _():
            dgq_ref[...] = jnp.zeros_like(dgq_ref)
            dgkv_ref[...] = jnp.zeros_like(dgkv_ref)
            uq_acc[...] = jnp.zeros_like(uq_acc)
            ukv_acc[...] = jnp.zeros_like(ukv_acc)

        cs, sn = cos_ref[...], sin_ref[...]
        dq_ref[:, :GROUP_W] = dqn_ref[...]
        for blk in range(PAIRS):
            sl = slice(blk * LANES, (blk + 1) * LANES)
            dq_ref[:, GROUP_W + blk * LANES:GROUP_W + (blk + 1) * LANES] = _rope_bwd(dqr_ref[:, sl], cs, sn).astype(MXU)
        dqn = lax.dot_general(dq_ref[...], wuqt_ref[...], _NT, preferred_element_type=F32)
        uq_acc[...] += lax.dot_general(qn_ref[...], dq_ref[...], _TN, preferred_element_type=F32)
        ukv_acc[:, :GROUP_W] += lax.dot_general(kvn_ref[...], dkn_ref[...], _TN, preferred_element_type=F32)
        ukv_acc[:, GROUP_W:] += lax.dot_general(kvn_ref[...], dv_ref[...], _TN, preferred_element_type=F32)

        @pl.when(i == steps - 1)
        def _():
            dwuq_ref[...] = uq_acc[...].astype(dwuq_ref.dtype)
            dwukv_ref[...] = ukv_acc[...].astype(dwukv_ref.dtype)

        ql = tail_ref[:, :T_KV]
        rq = lax.rsqrt(jnp.mean(ql * ql, axis=-1, keepdims=True) + EPS)
        qh = ql * rq
        dgq_ref[...] += jnp.sum(dqn * qh, axis=0, keepdims=True)
        dqh = dqn * gq_ref[...]
        dz_ref[:, :Q_LORA] = (rq * (dqh - qh * jnp.mean(dqh * qh, axis=-1, keepdims=True))).astype(MXU)

        dkvn = (lax.dot_general(dkn_ref[...], wukvt_ref[:, :GROUP_W], _NT, preferred_element_type=F32)
                + lax.dot_general(dv_ref[...], wukvt_ref[:, GROUP_W:], _NT, preferred_element_type=F32))
        kvl = tail_ref[:, T_KV:T_MISC]
        rk = lax.rsqrt(jnp.mean(kvl * kvl, axis=-1, keepdims=True) + EPS)
        kh = kvl * rk
        dgkv_ref[...] += jnp.sum(dkvn * kh, axis=0, keepdims=True)
        dkh = dkvn * gkv_ref[...]
        dz_ref[:, Q_LORA:Q_LORA + KV_LORA] = (
            rk * (dkh - kh * jnp.mean(dkh * kh, axis=-1, keepdims=True))).astype(MXU)

        g = dkr_ref[...] + pltpu.roll(dkr_ref[...], HEAD_DIM, 1)
        lane = lax.broadcasted_iota(jnp.int32, g.shape, 1)
        dmisc = jnp.where(lane < ROPE, _rope_bwd(g, cs, sn), 0.0) + dff_ref[...]
        dz_ref[:, Q_LORA + KV_LORA:] = dmisc.astype(MXU)

    return pl.pallas_call(
        kern, name="kb_prep", grid=(steps,),
        out_shape=[_sds((s_len, TAIL_W), MXU), _sds((1, Q_LORA)), _sds((1, KV_LORA)),
                   _sds((Q_LORA, qw), dw_dtype), _sds((KV_LORA, 2 * GROUP_W), dw_dtype)],
        in_specs=[_rows(tm, GROUP_W), _rows(tm, GROUP_W), _rows(tm, GROUP_W), _rows(tm, GROUP_W), _rows(tm, LANES),
                  _rows(tm, LANES), _rows(tm, TAIL_W), _rows(tm, Q_LORA), _rows(tm, KV_LORA),
                  _rows(tm, LANES), _rows(tm, LANES), _full((1, Q_LORA)), _full((1, KV_LORA)),
                  _full((Q_LORA, qw)), _full((KV_LORA, 2 * GROUP_W))],
        out_specs=[_rows(tm, TAIL_W), _full((1, Q_LORA)), _full((1, KV_LORA)), _full((Q_LORA, qw)),
                   _full((KV_LORA, 2 * GROUP_W))],
        scratch_shapes=[pltpu.VMEM((tm, qw), MXU), pltpu.VMEM((Q_LORA, qw), F32),
                        pltpu.VMEM((KV_LORA, 2 * GROUP_W), F32)],
        compiler_params=_params("arbitrary"),
    )(dqn, dqr, dkn, dv, dkr, dff, tail, qn, kvn, cos, sin, gq, gkv, wuq_t, wukv_t)


def _kb_in(dz_pieces, w, h, x, g, mod3, dxo, dw_dtype, tm=512):
    s_len, d = x.shape
    widths = [p.shape[1] for p in dz_pieces]
    n_p = len(widths)
    steps = s_len // tm

    def kern(*refs):
        dz_refs = refs[:n_p]
        w_ref, h_ref, x_ref, g_ref, mod_ref, dxo_ref, dx_ref, acc_ref, dw_ref, dw_acc = refs[n_p:]
        i = pl.program_id(0)

        @pl.when(i == 0)
        def _():
            acc_ref[...] = jnp.zeros_like(acc_ref)
            dw_acc[...] = jnp.zeros_like(dw_acc)

        dh = jnp.zeros((tm, d), F32)
        ht = h_ref[...]
        lo = 0
        for p_ref, wd in zip(dz_refs, widths):
            dh = dh + lax.dot_general(p_ref[...], w_ref[:, lo:lo + wd], _NT, preferred_element_type=F32)
            dw_acc[:, lo:lo + wd] += lax.dot_general(ht, p_ref[...], _TN, preferred_element_type=F32)
            lo += wd

        @pl.when(i == steps - 1)
        def _():
            dw_ref[...] = dw_acc[...].astype(dw_ref.dtype)

        xv = x_ref[...]
        r = lax.rsqrt(jnp.mean(xv * xv, axis=-1, keepdims=True) + EPS)
        xh = xv * r
        xn = xh * g_ref[...]
        dxn = dh * (1.0 + mod_ref[1:2, :])
        acc_ref[0:1, :] += jnp.sum(dh, axis=0, keepdims=True)
        acc_ref[1:2, :] += jnp.sum(dh * xn, axis=0, keepdims=True)
        acc_ref[2:3, :] += jnp.sum(dxn * xh, axis=0, keepdims=True)
        dxh = dxn * g_ref[...]
        dx_ref[...] = dxo_ref[...] + r * (dxh - xh * jnp.mean(dxh * xh, axis=-1, keepdims=True))

    return pl.pallas_call(
        kern, name="kb_in", grid=(steps,),
        out_shape=[_sds((s_len, d)), _sds((3, d)), _sds((d, Z_W), dw_dtype)],
        in_specs=[_rows(tm, wd) for wd in widths] + [_full((d, Z_W)), _rows(tm, d), _rows(tm, d), _full((1, d)),
                                                     _full((3, d)), _rows(tm, d)],
        out_specs=[_rows(tm, d), _full((3, d)), _full((d, Z_W))],
        scratch_shapes=[pltpu.VMEM((d, Z_W), F32)],
        compiler_params=_params("arbitrary"),
    )(*dz_pieces, w, h, x, g, mod3, dxo)


def _adamw(slabs, w, m, v, name):
    n_l = len(slabs)
    n, r, c = slabs[0].shape
    tm = r
    for cand in (256, 128, 64, 32, 16, 8):
        if r % cand == 0:
            tm = cand
            break
    steps = r // tm

    def kern(*refs):
        g_refs = refs[:n_l]
        w_ref, m_ref, v_ref, go_ref, d_ref, mo_ref, vo_ref, g_scr = refs[n_l:]
        for ll in range(n_l):
            @pl.when(pl.program_id(0) == ll)
            def _(g_ref=g_refs[ll]):
                g = g_ref[0].astype(F32)
                for s in range(1, n):
                    g = g + g_ref[s].astype(F32)
                g_scr[...] = g

        g = g_scr[...]
        m_new = ADAM_B1 * m_ref[...] + (1.0 - ADAM_B1) * g
        v_new = ADAM_B2 * v_ref[...] + (1.0 - ADAM_B2) * (g * g)
        m_hat = m_new / (1.0 - ADAM_B1 ** ADAM_STEP)
        v_hat = v_new / (1.0 - ADAM_B2 ** ADAM_STEP)
        go_ref[...] = g
        mo_ref[...] = m_new
        vo_ref[...] = v_new
        d_ref[...] = -ADAM_LR * (m_hat / (jnp.sqrt(v_hat) + ADAM_EPS) + ADAM_WD * w_ref[...])

    row = pl.BlockSpec((tm, c), lambda l, i: (l * steps + i, 0))

    def slab_spec(ll):
        return pl.BlockSpec((n, tm, c), lambda l, i: (0, jnp.where(l == ll, i, 0), 0))

    return pl.pallas_call(
        kern, name=name, grid=(n_l, steps), out_shape=[_sds((n_l * r, c))] * 4,
        in_specs=[slab_spec(ll) for ll in range(n_l)] + [row, row, row],
        out_specs=[row] * 4,
        scratch_shapes=[pltpu.VMEM((tm, c), F32)],
        compiler_params=_params("arbitrary", "arbitrary"),
    )(*slabs, w, m, v)


def _perm_w_in(w):
    pad = jnp.zeros(w.shape[:-1] + (Z_W - Z_MISC - ROPE - HEADS,), w.dtype)
    return jnp.concatenate([w[..., 0:1536], w[..., 1544:2056], w[..., 2472:2984], w[..., 2056:2312],
                            w[..., 2312:2440], w[..., 2440:2472], w[..., 1536:1544], pad], axis=-1)


def _unperm_w_in(g):
    ff0 = Z_MISC + MISC_FF
    return jnp.concatenate([g[..., 0:1536], g[..., ff0:ff0 + HEADS], g[..., Z_FG:Z_FG + GROUP_W],
                            g[..., Z_QL:Z_QL + Q_LORA], g[..., Z_KV:Z_KV + KV_LORA],
                            g[..., Z_MISC:Z_MISC + ROPE], g[..., Z_MG:Z_MG + GROUP_W]], axis=-1)


def _perm_w_uq(w):
    lead = w.shape[:-1]
    wh = w.reshape(lead + (PAIRS, 2, NOPE + ROPE))
    zero = jnp.zeros(lead + (PAIRS, HEAD_DIM - ROPE), w.dtype)
    rope = jnp.concatenate([wh[..., 1, NOPE:], zero, wh[..., 0, NOPE:], zero], axis=-1)
    return jnp.concatenate([wh[..., :NOPE].reshape(lead + (GROUP_W,)), rope.reshape(lead + (GROUP_W,))], axis=-1)


def _unperm_w_uq(g):
    lead = g.shape[:-1]
    nope = g[..., :GROUP_W].reshape(lead + (PAIRS, 2, NOPE))
    rp = g[..., GROUP_W:].reshape(lead + (PAIRS, 2, HEAD_DIM))[..., :ROPE]
    return jnp.concatenate([nope, rp[..., ::-1, :]], axis=-1).reshape(lead + (HEADS * (NOPE + ROPE),))


def _perm_w_ukv(w):
    lead = w.shape[:-1]
    wh = w.reshape(lead + (HEADS, 2 * HEAD_DIM))
    return jnp.concatenate([wh[..., :NOPE].reshape(lead + (GROUP_W,)),
                            wh[..., NOPE:].reshape(lead + (GROUP_W,))], axis=-1)


def _unperm_w_ukv(g):
    lead = g.shape[:-1]
    parts = [g[..., :GROUP_W].reshape(lead + (HEADS, NOPE)), g[..., GROUP_W:].reshape(lead + (HEADS, HEAD_DIM))]
    return jnp.concatenate(parts, axis=-1).reshape(lead + (2 * GROUP_W,))


def _rope_tables(positions):
    inv_freq = 1.0 / (ROPE_THETA ** (jnp.arange(0, ROPE, 2, dtype=F32) / ROPE))
    ang = positions.astype(F32)[:, None] * inv_freq
    cos, sin = jnp.cos(ang), jnp.sin(ang)
    reps = LANES // ROPE
    return jnp.tile(jnp.concatenate([cos, cos], axis=1), (1, reps)), jnp.tile(jnp.concatenate([-sin, sin], axis=1), (1, reps))


def _full_weights(g_in, g_uq, g_ukv, g_out):
    def cols(g):
        return g.transpose(1, 0, 2).reshape(g.shape[1], -1)
    return (None if g_in is None else _perm_w_in(cols(g_in)),
            None if g_uq is None else _perm_w_uq(cols(g_uq)),
            None if g_ukv is None else _perm_w_ukv(cols(g_ukv)),
            None if g_out is None else g_out.reshape(-1, g_out.shape[2]))


def _grad_slabs(dw_in, dw_uq, dw_ukv):
    def cols(g):
        return g.reshape(g.shape[0], N_DEV, -1).transpose(1, 0, 2)
    return [cols(_unperm_w_in(dw_in)), cols(_unperm_w_uq(dw_uq)), cols(_unperm_w_ukv(dw_ukv))]


def _local_step(x, mod, positions, loss_target, norm_g, b_f, q_norm_g, kv_norm_g, final_g, weights, shards=None):
    n_l = norm_g.shape[0]
    s_len, d = x.shape
    cos, sin = _rope_tables(positions)
    qb, kb, vb = Z_FQ // LANES, Z_FK // LANES, Z_FV // LANES
    chunks = s_len // LANES
    weights = list(weights)

    def pack_rows(a_rows, delta):
        return jnp.concatenate([a_rows.T, delta[:, :HEADS], jnp.zeros((s_len, LANES - 2 * HEADS), F32)], axis=1)

    saved = []
    for l in range(n_l):
        w_in, w_uq, w_ukv, w_out = weights[l]
        mod3 = mod[l].reshape(3, d)
        h, qkv, gates, tail = _k_in(x, norm_g[l][None], mod3, w_in)
        fft = tail[:, T_MISC + MISC_FF:T_MISC + MISC_FF + HEADS].T.reshape(HEADS * chunks, LANES)
        bf = jnp.repeat(b_f[l], chunks)[:, None]
        c2 = _k_cum(fft, bf, chunks).reshape(HEADS, s_len) * LOG2E
        side_arrs = []
        if shards is not None:
            side_arrs += list(shards[l][1:]) if w_uq is None else []
            side_arrs += list(shards[l + 1]) if l + 1 < n_l else []
        side = (side_arrs, [True] * len(side_arrs)) if side_arrs else None
        ck_lanes = jnp.pad(c2.T, ((0, 0), (0, LANES - HEADS)))
        of, lse_f, *gathered = _attention_fwd(qkv, qb, qkv, kb, qkv, vb, ck_lanes, None, False,
                                              "fox_fwd_gather" if side else "fox_fwd", side)
        if w_uq is None:
            _, w_uq, w_ukv, w_out = _full_weights(None, *gathered[:3])
            weights[l], gathered = (w_in, w_uq, w_ukv, w_out), gathered[3:]
        if gathered:
            weights.append(_full_weights(*gathered))
        mq, mqr, mk, mv, kr2, qn, kvn = _k_prep(tail, cos, sin, q_norm_g[l][None], kv_norm_g[l][None], w_uq, w_ukv)
        om, lse_m = _attention_fwd(mq, 0, mk, 0, mv, 0, None, (mqr, kr2), True, "mla_fwd")
        x_new, y, u = _k_out(of, om, gates, x, mod3[2:3], w_out)
        saved.append((x, gates, tail, h, qkv, fft, bf, c2, lse_f, mq, mqr, mk, mv, kr2, lse_m, of, om, qn, kvn, y, u,
                      mod3))
        x = x_new

    loss_row, dx, dfinal = _k_loss(x, final_g[None], loss_target)

    grads = {k: [] for k in ("norm_g", "mod", "w_in", "b_f", "q_norm_g", "w_uq", "kv_norm_g", "w_ukv", "w_out")}
    received, pending = {}, None
    wg_dtype = MXU if shards is not None else F32
    for l in range(n_l - 1, -1, -1):
        (x_l, gates, tail, h, qkv, fft, bf, c2, lse_f, mq, mqr, mk, mv, kr2, lse_m, of, om, qn, kvn, y, u,
         mod3) = saved[l]
        w_in, w_uq, w_ukv, w_out = weights[l]
        dof, dom, dfg, dmg, dlt_f, dlt_m, dgate, dw_out = _kb_out(dx, y, u, mod3[2:3], w_out, of, om, gates, wg_dtype)

        side = None
        if shards is not None:
            side_arrs = (pending or []) + [dw_out.reshape(N_DEV, -1, dw_out.shape[1])]
            side = (side_arrs, [False] * len(side_arrs))
        dfq, dfk, dfv, dcq, dck, *arrived = _attention_bwd(
            qkv, qb, qkv, kb, qkv, vb, dof, pack_rows(-lse_f.reshape(HEADS, s_len), dlt_f), c2[:, None, :], None,
            False, FOX_SCALE, 1.0 / LOG2E, "fox_bwd_exchange" if pending else "fox_bwd", side)
        if side:
            received[l] = [None, None, None, arrived[-1]]
            if pending:
                received[l + 1][:3] = arrived[:3]
        dcum = (dcq[:, :HEADS].T + dck.reshape(HEADS, s_len)).reshape(HEADS * chunks, LANES)
        dff_rows, dbf_rows = _k_cum_bwd(dcum, fft, bf, chunks)
        dfft = dff_rows.reshape(HEADS, s_len)
        grads["b_f"].append(jnp.sum(dbf_rows[:, 0].reshape(HEADS, chunks), axis=1))

        dmq, dkn, dmv, dqr, dkr_pairs = _attention_bwd(
            mq, 0, mk, 0, mv, 0, dom, pack_rows(-lse_m.reshape(HEADS, s_len), dlt_m), None, (mqr, kr2), True,
            MLA_SCALE, 1.0 / LOG2E, "mla_bwd")
        dkr = dkr_pairs[0] + dkr_pairs[1] + dkr_pairs[2] + dkr_pairs[3]
        dff = jnp.pad(dfft.T, ((0, 0), (MISC_FF, LANES - MISC_FF - HEADS)))
        dz_tail, dgq, dgkv, dw_uq, dw_ukv = _kb_prep(dmq, dqr, dkn, dmv, dkr, dff, tail, qn, kvn, cos, sin,
                                                     q_norm_g[l][None], kv_norm_g[l][None], w_uq, w_ukv, wg_dtype)
        grads["q_norm_g"].append(dgq[0])
        grads["kv_norm_g"].append(dgkv[0])
        dz = [dfq, dfk, dfv, dfg, dmg, dz_tail]
        dx, acc3, dw_in = _kb_in(dz, w_in, h, x_l, norm_g[l][None], mod3, dx, wg_dtype)
        grads["norm_g"].append(acc3[2])
        grads["mod"].append(jnp.concatenate([acc3[0], acc3[1], dgate[0]]))
        if shards is not None:
            pending = _grad_slabs(dw_in, dw_uq, dw_ukv)
        else:
            for name, g in (("w_in", dw_in), ("w_uq", dw_uq), ("w_ukv", dw_ukv), ("w_out", dw_out)):
                grads[name].append(g)
    grads = {k: jnp.stack(v[::-1]) for k, v in grads.items() if v}
    grads["final_g"] = dfinal[0]
    if shards is None:
        return loss_row[0, 0], dx, grads
    return loss_row[0, 0], dx, grads, received, pending


def _pack_small(parts, total):
    flat = jnp.concatenate([p.reshape(-1) for p in parts])
    return jnp.pad(flat, (0, total - flat.shape[0])).reshape(total // LANES, LANES)


def kernel(x, c, positions, norm_g, w_ada, b_ada, w_in, b_f, q_norm_g, w_uq, kv_norm_g, w_ukv, w_out, final_g, loss_target, m_norm_g, m_w_ada, m_b_ada, m_w_in, m_b_f, m_q_norm_g, m_w_uq, m_kv_norm_g, m_w_ukv, m_w_out, m_final_g, v_norm_g, v_w_ada, v_b_ada, v_w_in, v_b_f, v_q_norm_g, v_w_uq, v_kv_norm_g, v_w_ukv, v_w_out, v_final_g):
    n_l, d = norm_g.shape
    me = 4 * lax.axis_index("x") + 2 * lax.axis_index("y") + lax.axis_index("c")
    ada_c = w_ada.shape[2]

    cact = jnp.broadcast_to(jax.nn.silu(c), (N_DEV, d))
    shards = [[w[l].astype(MXU) for w in (w_in, w_uq, w_ukv, w_out)] for l in range(n_l)]
    g_in0, g_cact = _gather_two_level([shards[0][0], cact], "gather_layer0")
    cact_all = g_cact[:, 0, :]

    b_cols = lax.dynamic_slice_in_dim(b_ada, me * ada_c, ada_c, axis=1)[:, None, :]
    modpart = _modpart(cact_all, w_ada, b_cols)
    mod_send = jnp.pad(modpart.transpose(1, 0, 2), ((0, 0), (0, 8 - n_l), (0, 0)))
    (mod_recv,) = _exchange([mod_send], [False], "scatter_mod")
    mod = mod_recv.transpose(1, 0, 2).reshape(8, N_DEV * ada_c)[:n_l]

    loss, dx, gr, received, pending = _local_step(x[0], mod, positions[0], loss_target[0], norm_g, b_f, q_norm_g,
                                                  kv_norm_g, final_g, [_full_weights(g_in0, None, None, None)],
                                                  shards)

    small_parts = [gr["norm_g"], gr["mod"], gr["b_f"], gr["q_norm_g"], gr["kv_norm_g"], gr["final_g"], cact[0],
                   loss.reshape(1)]
    sizes = [int(np.prod(p.shape)) for p in small_parts]
    total = -(-sum(sizes) // 1024) * 1024
    small = _pack_small(small_parts, total)
    *received[0][:3], r_small = _exchange(pending + [small], [False, False, False, True], "exchange_layer0")
    r_in, r_uq, r_ukv, r_out = ([received[l][i] for l in range(n_l)] for i in range(4))

    def upd(slabs, w, m, v, name):
        shp = w.shape
        w2, m2, v2 = (a.reshape(-1, slabs[0].shape[2]) for a in (w, m, v))
        return [o.reshape(shp) for o in _adamw(slabs, w2, m2, v2, name)]

    o_in = upd(r_in, w_in, m_w_in, v_w_in, "adamw_w_in")
    o_uq = upd(r_uq, w_uq, m_w_uq, v_w_uq, "adamw_w_uq")
    o_ukv = upd(r_ukv, w_ukv, m_w_ukv, v_w_ukv, "adamw_w_ukv")
    o_out = upd(r_out, w_out, m_w_out, v_w_out, "adamw_w_out")

    offs = np.cumsum([0] + sizes)
    flat_all = r_small.reshape(N_DEV, total)
    dmod_all = flat_all[:, offs[1]:offs[2]].reshape(N_DEV, n_l, 3 * d)
    dmod_cols = lax.dynamic_slice_in_dim(dmod_all, me * ada_c, ada_c, axis=2).transpose(1, 0, 2)
    cact_cols = flat_all[:, offs[6]:offs[7]][:, :, None]
    g_ada = _ada_grad(cact_cols, dmod_cols)
    o_ada = upd([g_ada.reshape(1, n_l * d, ada_c)], w_ada, m_w_ada, v_w_ada, "adamw_w_ada")

    zero_c = jnp.zeros((d,), F32)
    small_w = [_pack_small([norm_g, b_ada, b_f, q_norm_g, kv_norm_g, final_g, zero_c], total),
               _pack_small([m_norm_g, m_b_ada, m_b_f, m_q_norm_g, m_kv_norm_g, m_final_g, zero_c], total),
               _pack_small([v_norm_g, v_b_ada, v_b_f, v_q_norm_g, v_kv_norm_g, v_final_g, zero_c], total)]
    o_small = [o.reshape(-1) for o in _adamw([r_small], *small_w, "adamw_small")]
    shapes = [norm_g.shape, b_ada.shape, b_f.shape, q_norm_g.shape, kv_norm_g.shape, final_g.shape]

    def small_out(kind, idx):
        return o_small[kind][offs[idx]:offs[idx + 1]].reshape(shapes[idx])

    outs = [o_small[0][offs[7]], dx[None]]
    for kind in range(4):
        outs += [small_out(kind, 0), o_ada[kind], small_out(kind, 1), o_in[kind], small_out(kind, 2),
                 small_out(kind, 3), o_uq[kind], small_out(kind, 4), o_ukv[kind], o_out[kind], small_out(kind, 5)]
    return tuple(outs)
```
